```python
import jax, jax.numpy as jnp
from jax import lax
import numpy as np

D_MODEL = 1024
BATCH = 16
SEQ = 256
DEPTH = 1
DEC_BATCH = 4
DEC_SEQ = 1024
PAST_LEN = 512

GRID_W = 64
MIX_WIDTH = D_MODEL
LRU_WIDTH = MIX_WIDTH // 2
LRU_HEADS = 8
LRU_HEAD_DIM = LRU_WIDTH // LRU_HEADS
LRU_C = 8.0
CONV_W = 4
CONV_PAD_LEFT = 2
RET_WIDTH = MIX_WIDTH - LRU_WIDTH
RET_HEADS = 4
RET_DK = RET_WIDTH // RET_HEADS
RET_DV = RET_WIDTH // RET_HEADS
RET_CHUNK = 64
RET_DECAY_OFFSET_FWD = 5.0
RET_DECAY_OFFSET_BWD = 5.5
ROPE_BASE = 10000.0
IN_PROJ_WIDTH = 2 * LRU_WIDTH + 4 * RET_WIDTH
IN_SPLITS = (LRU_WIDTH, 2 * LRU_WIDTH, 2 * LRU_WIDTH + RET_WIDTH,
             2 * LRU_WIDTH + 2 * RET_WIDTH, 2 * LRU_WIDTH + 3 * RET_WIDTH)
N_EXPERTS = 16
EC_CAPACITY_FACTOR = 2
EXPERT_FF = 2 * D_MODEL
N_MOD = 6
EPS = 1e-6

kernel_name = 'hybrid_rglru_retention_ecmoe_diffusion_step'


def _normal(k, shape, scale):
    return jax.random.normal(k, shape, jnp.float32) * scale


def _rms_norm(x, gain):
    xf = x.astype(jnp.float32)
    y = xf * lax.rsqrt(jnp.mean(xf * xf, axis=-1, keepdims=True) + EPS)
    return (y * gain.astype(jnp.float32)).astype(x.dtype)


def _dwconv_centred(x, w, b):
    n = x.shape[1]
    xp = jnp.pad(x, ((0, 0), (CONV_PAD_LEFT, CONV_W - 1 - CONV_PAD_LEFT), (0, 0)))
    y = b
    for tap in range(CONV_W):
        y = y + xp[:, tap:tap + n] * w[tap]
    return y


def _lin_combine(e1, e2):
    a1, b1 = e1
    a2, b2 = e2
    return a1 * a2, a2 * b1 + b2


def _rglru_dir(x, wa, ba, wi, bi, lam, h0, reverse):
    bsz, n, w = x.shape
    xf = x.astype(jnp.float32)
    xh = xf.reshape(bsz, n, LRU_HEADS, LRU_HEAD_DIM)
    r = jax.nn.sigmoid(jnp.einsum('bnhi,hij->bnhj', xh, wa.astype(jnp.float32)).reshape(bsz, n, w)
                       + ba.astype(jnp.float32))
    gi = jax.nn.sigmoid(jnp.einsum('bnhi,hij->bnhj', xh, wi.astype(jnp.float32)).reshape(bsz, n, w)
                        + bi.astype(jnp.float32))
    log_a = -LRU_C * r * jax.nn.softplus(-lam.astype(jnp.float32))
    a = jnp.exp(log_a)
    u = jnp.sqrt(-jnp.expm1(2.0 * log_a)) * gi * xf
    first = n - 1 if reverse else 0
    last = 0 if reverse else n - 1
    u = u.at[:, first].add(a[:, first] * h0.astype(jnp.float32))
    _, h = lax.associative_scan(_lin_combine, (a, u), reverse=reverse, axis=1)
    return h, h[:, last]


def _ret_log_decay(offset):
    return jnp.log1p(-jnp.exp2(-(offset + jnp.arange(RET_HEADS, dtype=jnp.float32))))


def _retention_dir(q, k, v, s0, log_g, strict):
    bsz, n, h, dk = q.shape
    dv = v.shape[-1]
    nc = n // RET_CHUNK
    qc = q.reshape(bsz, nc, RET_CHUNK, h, dk)
    kc = k.reshape(bsz, nc, RET_CHUNK, h, dk)
    vc = v.reshape(bsz, nc, RET_CHUNK, h, dv)
    pos = jnp.arange(RET_CHUNK, dtype=jnp.float32)
    diff = pos[:, None] - pos[None, :]
    mask = (diff > 0) if strict else (diff >= 0)
    d_intra = jnp.where(mask[None], jnp.exp(jnp.where(mask, diff, 0.0)[None] * log_g[:, None, None]), 0.0)
    xi = jnp.exp((pos + 1.0)[None, :] * log_g[:, None])
    zeta = jnp.exp((RET_CHUNK - 1.0 - pos)[None, :] * log_g[:, None])
    chunk_decay = jnp.exp(RET_CHUNK * log_g)[:, None, None]
    scores = jnp.einsum('bcihd,bcjhd->bchij', qc, kc) * d_intra
    o_inner = jnp.einsum('bchij,bcjhe->bcihe', scores, vc)
    kv = jnp.einsum('bcjhd,hj,bcjhe->bchde', kc, zeta, vc)

    def step(s, kv_c):
        return s * chunk_decay + kv_c, s

    s_final, s_prev = lax.scan(step, s0, jnp.moveaxis(kv, 1, 0))
    s_prev = jnp.moveaxis(s_prev, 0, 1)
    o_cross = jnp.einsum('bcihd,hi,bchde->bcihe', qc, xi, s_prev)
    return (o_inner + o_cross).reshape(bsz, n, h, dv), s_final


def _rope_grid(x):
    n = x.shape[1]
    rows = n // GRID_W
    row = jnp.repeat(jnp.arange(rows, dtype=jnp.float32), GRID_W)
    col = jnp.tile(jnp.arange(GRID_W, dtype=jnp.float32), rows)
    nf = RET_DK // 4
    freqs = ROPE_BASE ** (-jnp.arange(nf, dtype=jnp.float32) / nf)
    ang = jnp.concatenate([row[:, None] * freqs, col[:, None] * freqs], axis=-1)
    cos = jnp.cos(ang)[None, :, None, :]
    sin = jnp.sin(ang)[None, :, None, :]
    xa, xb = jnp.split(x, 2, axis=-1)
    return jnp.concatenate([xa * cos - xb * sin, xa * sin + xb * cos], axis=-1)


def _mixer(h, lp, lru_h0, ret_s0, latent):
    bsz, n, _ = h.shape
    f32 = jnp.float32
    proj = h @ lp['w_in']
    xl, yl, q, k, v, g = jnp.split(proj, IN_SPLITS, axis=-1)
    xl = _dwconv_centred(xl, lp['conv_w'], lp['conv_b'])
    hf, hf_last = _rglru_dir(xl, lp['lru_wa'][0], lp['lru_ba'][0], lp['lru_wi'][0], lp['lru_bi'][0],
                             lp['lru_lambda'][0], lru_h0[:, 0], reverse=False)
    hb, hb_first = _rglru_dir(xl, lp['lru_wa'][1], lp['lru_ba'][1], lp['lru_wi'][1], lp['lru_bi'][1],
                              lp['lru_lambda'][1], lru_h0[:, 1], reverse=True)
    lru_out = ((hf + hb) * jax.nn.gelu(yl.astype(f32))).astype(h.dtype)
    qh = q.astype(f32).reshape(bsz, n, RET_HEADS, RET_DK) * (RET_DK ** -0.5)
    kh = k.astype(f32).reshape(bsz, n, RET_HEADS, RET_DK)
    vh = v.astype(f32).reshape(bsz, n, RET_HEADS, RET_DV)
    if latent:
        qh = _rope_grid(qh)
        kh = _rope_grid(kh)
    of, sf = _retention_dir(qh, kh, vh, ret_s0[:, 0].astype(f32),
                            _ret_log_decay(RET_DECAY_OFFSET_FWD), strict=False)
    ob, sb = _retention_dir(jnp.flip(qh, 1), jnp.flip(kh, 1), jnp.flip(vh, 1), ret_s0[:, 1].astype(f32),
                            _ret_log_decay(RET_DECAY_OFFSET_BWD), strict=True)
    o = of + jnp.flip(ob, 1)
    o = o * lax.rsqrt(jnp.mean(o * o, axis=-1, keepdims=True) + EPS)
    ret_out = (o.reshape(bsz, n, RET_WIDTH) * jax.nn.silu(g.astype(f32))).astype(h.dtype)
    out = jnp.concatenate([lru_out, ret_out], axis=-1) @ lp['w_out']
    lru_state = jnp.stack([hf_last, hb_first], axis=1).astype(h.dtype)
    ret_state = jnp.stack([sf, sb], axis=1).astype(h.dtype)
    return out, lru_state, ret_state


def _expert_choice_ffn(h, router_w, wg, wu, wd):
    bsz, n, _ = h.shape
    cap = EC_CAPACITY_FACTOR * n // N_EXPERTS
    probs = jax.nn.softmax((h @ router_w).astype(jnp.float32), axis=-1)
    gate, idx = lax.top_k(jnp.swapaxes(probs, 1, 2), cap)
    b_idx = jnp.arange(bsz)[:, None, None]
    xs = h[b_idx, idx]
    hid = jax.nn.silu(jnp.einsum('becd,edf->becf', xs, wg)) * jnp.einsum('becd,edf->becf', xs, wu)
    ye = jnp.einsum('becf,efd->becd', hid, wd) * gate[..., None].astype(h.dtype)
    return jnp.zeros_like(h).at[b_idx, idx].add(ye)


def _block(x, mod, lp, lru_h0, ret_s0, latent):
    m = mod.reshape(mod.shape[0], N_MOD, D_MODEL)[:, :, None, :]
    sh1, sc1, g1, sh2, sc2, g2 = (m[:, i] for i in range(N_MOD))
    h = _rms_norm(x, lp['norm_mix_pre']) * (1.0 + sc1) + sh1
    mix, lru_state, ret_state = _mixer(h, lp, lru_h0, ret_s0, latent)
    x = x + g1 * _rms_norm(mix, lp['norm_mix_post'])
    h = _rms_norm(x, lp['norm_ffn_pre']) * (1.0 + sc2) + sh2
    f = _expert_choice_ffn(h, lp['router_w'], lp['exp_w_gate'], lp['exp_w_up'], lp['exp_w_down'])
    x = x + g2 * _rms_norm(f, lp['norm_ffn_post'])
    return x, lru_state, ret_state


def setup_inputs(seed: int = 0) -> dict:
    key = jax.random.key(seed)
    ks = jax.random.split(key, 27)
    a_c = jax.random.uniform(ks[22], (DEPTH, 2, LRU_WIDTH), jnp.float32, minval=0.9, maxval=0.999)
    a = a_c ** (1.0 / LRU_C)
    lam = jnp.log(a) - jnp.log1p(-a)
    return {
        'x_prompt': _normal(ks[0], (BATCH, SEQ, D_MODEL), 1.0),
        'x_sample': _normal(ks[1], (DEC_BATCH, DEC_SEQ, D_MODEL), 1.0),
        'c': _normal(ks[2], (DEC_BATCH, D_MODEL), 1.0),
        'state_lru': _normal(ks[3], (DEC_BATCH, DEPTH, 2, LRU_WIDTH), 0.5),
        'state_ret': _normal(ks[4], (DEC_BATCH, DEPTH, 2, RET_HEADS, RET_DK, RET_DV), 4.0),
        'c_ctx': _normal(ks[5], (D_MODEL,), 1.0),
        'ada_w': _normal(ks[6], (DEPTH, D_MODEL, N_MOD * D_MODEL), 0.5 * D_MODEL ** -0.5),
        'ada_b': _normal(ks[7], (DEPTH, N_MOD * D_MODEL), 0.01),
        'norm_mix_pre': 1.0 + _normal(ks[8], (DEPTH, D_MODEL), 0.05),
        'norm_mix_post': 1.0 + _normal(ks[9], (DEPTH, D_MODEL), 0.05),
        'norm_ffn_pre': 1.0 + _normal(ks[10], (DEPTH, D_MODEL), 0.05),
        'norm_ffn_post': 1.0 + _normal(ks[11], (DEPTH, D_MODEL), 0.05),
        'w_in': _normal(ks[12], (DEPTH, D_MODEL, IN_PROJ_WIDTH), D_MODEL ** -0.5),
        'conv_w': _normal(ks[13], (DEPTH, CONV_W, LRU_WIDTH), CONV_W ** -0.5),
        'conv_b': _normal(ks[14], (DEPTH, LRU_WIDTH), 0.01),
        'lru_wa': _normal(ks[15], (DEPTH, 2, LRU_HEADS, LRU_HEAD_DIM, LRU_HEAD_DIM), LRU_HEAD_DIM ** -0.5),
        'lru_ba': _normal(ks[16], (DEPTH, 2, LRU_WIDTH), 0.01),
        'lru_wi': _normal(ks[17], (DEPTH, 2, LRU_HEADS, LRU_HEAD_DIM, LRU_HEAD_DIM), LRU_HEAD_DIM ** -0.5),
        'lru_bi': _normal(ks[18], (DEPTH, 2, LRU_WIDTH), 0.01),
        'lru_lambda': lam,
        'w_out': _normal(ks[19], (DEPTH, MIX_WIDTH, D_MODEL), MIX_WIDTH ** -0.5),
        'router_w': _normal(ks[20], (DEPTH, D_MODEL, N_EXPERTS), D_MODEL ** -0.5),
        'exp_w_gate': _normal(ks[21], (DEPTH, N_EXPERTS, D_MODEL, EXPERT_FF), D_MODEL ** -0.5),
        'exp_w_up': _normal(ks[23], (DEPTH, N_EXPERTS, D_MODEL, EXPERT_FF), D_MODEL ** -0.5),
        'exp_w_down': _normal(ks[24], (DEPTH, N_EXPERTS, EXPERT_FF, D_MODEL), EXPERT_FF ** -0.5),
    }


def reference(x_prompt, x_sample, c, state_lru, state_ret, c_ctx, ada_w, ada_b,
              norm_mix_pre, norm_mix_post, norm_ffn_pre, norm_ffn_post, w_in, conv_w, conv_b,
              lru_wa, lru_ba, lru_wi, lru_bi, lru_lambda, w_out, router_w,
              exp_w_gate, exp_w_up, exp_w_down):
    bp = x_prompt.shape[0]
    y_prompt = x_prompt
    y_sample = x_sample
    new_lru = []
    new_ret = []
    for l in range(DEPTH):
        lp = {
            'norm_mix_pre': norm_mix_pre[l], 'norm_mix_post': norm_mix_post[l],
            'norm_ffn_pre': norm_ffn_pre[l], 'norm_ffn_post': norm_ffn_post[l],
            'w_in': w_in[l], 'conv_w': conv_w[l], 'conv_b': conv_b[l],
            'lru_wa': lru_wa[l], 'lru_ba': lru_ba[l], 'lru_wi': lru_wi[l], 'lru_bi': lru_bi[l],
            'lru_lambda': lru_lambda[l], 'w_out': w_out[l], 'router_w': router_w[l],
            'exp_w_gate': exp_w_gate[l], 'exp_w_up': exp_w_up[l], 'exp_w_down': exp_w_down[l],
        }
        mod_ctx = jax.nn.silu(c_ctx)[None, :] @ ada_w[l] + ada_b[l]
        mod_lat = jax.nn.silu(c) @ ada_w[l] + ada_b[l]
        zero_lru = jnp.zeros((bp, 2, LRU_WIDTH), x_prompt.dtype)
        zero_ret = jnp.zeros((bp, 2, RET_HEADS, RET_DK, RET_DV), x_prompt.dtype)
        y_prompt, lru_l, ret_l = _block(y_prompt, mod_ctx, lp, zero_lru, zero_ret, latent=False)
        y_sample, _, _ = _block(y_sample, mod_lat, lp, state_lru[:, l], state_ret[:, l], latent=True)
        new_lru.append(lru_l)
        new_ret.append(ret_l)
    new_state_lru = jnp.stack(new_lru, axis=1)
    new_state_ret = jnp.stack(new_ret, axis=1)
    return (y_prompt, y_sample, new_state_lru, new_state_ret)
```

```python
import functools
import math

import jax
import jax.numpy as jnp
import numpy as np
from jax import lax
from jax.experimental import pallas as pl
from jax.experimental.pallas import tpu as pltpu

D = 1024
LRU_W = 512
LRU_HEADS = 8
LRU_HD = 64
LRU_C = 8.0
RET_W = 512
RET_H = 4
DK = 128
N_EXP = 16
FF = 2048
N_MOD = 6
EPS = 1e-6
GRID_W = 64
ROPE_BASE = 10000.0
IN_W = 3072

ROW_TILE = 256
SUBLANES = 8
VMEM_LIMIT = 60 * 1024 * 1024

LOG_G_FWD = [math.log1p(-2.0 ** -(5.0 + h)) for h in range(RET_H)]
LOG_G_BWD = [math.log1p(-2.0 ** -(5.5 + h)) for h in range(RET_H)]

_BF = jnp.bfloat16
_F32 = jnp.float32


def _sigmoid(x):
    return 1.0 / (1.0 + jnp.exp(-x))


def _silu(x):
    return x * _sigmoid(x)


def _gelu_tanh(x):
    c = math.sqrt(2.0 / math.pi)
    return 0.5 * x * (1.0 + jnp.tanh(c * (x + 0.044715 * (x * x * x))))


def _rms(x, gain):
    return x * lax.rsqrt(jnp.mean(x * x, axis=-1, keepdims=True) + EPS) * gain


def _dot(a, b):
    return jnp.dot(a, b, preferred_element_type=_F32)


def _dot_nt(a, b):
    return lax.dot_general(a, b, (((1,), (1,)), ((), ())), preferred_element_type=_F32)


def _dot_tn(a, b):
    return lax.dot_general(a, b, (((0,), (0,)), ((), ())), preferred_element_type=_F32)


def _ada_kernel(c_ref, w_ref, b_ref, o_ref):
    s = _silu(c_ref[...]).astype(_BF)
    o_ref[...] = _dot(s, w_ref[...].astype(_BF)) + b_ref[...]


def _ada_call(c_all, ada_w, ada_b):
    tn = 1536
    n_out = ada_w.shape[1]
    return pl.pallas_call(
        _ada_kernel,
        out_shape=jax.ShapeDtypeStruct((8, n_out), _F32),
        grid=(n_out // tn,),
        in_specs=[
            pl.BlockSpec((8, D), lambda j: (0, 0)),
            pl.BlockSpec((D, tn), lambda j: (0, j)),
            pl.BlockSpec((1, tn), lambda j: (0, j)),
        ],
        out_specs=pl.BlockSpec((8, tn), lambda j: (0, j)),
        compiler_params=pltpu.CompilerParams(
            dimension_semantics=("arbitrary",), vmem_limit_bytes=VMEM_LIMIT),
        name="ada_mod",
    )(c_all, ada_w, ada_b)


def _mixer_kernel(*refs, n, latent):
    if latent:
        (x_ref, mod_ref, nv_ref, win_ref, wout_ref, lv_ref, wg_ref, rwt_ref,
         h0_ref, s0_ref, cos_ref, sin_ref,
         x1_ref, h2_ref, lt_ref,
         xlp, gy, sg, a_f, a_b, h_f, h_b, qb, kb, vb, qx_f, qx_b, mixo) = refs
    else:
        (x_ref, mod_ref, nv_ref, win_ref, wout_ref, lv_ref, wg_ref, rwt_ref,
         x1_ref, h2_ref, lt_ref, stl_ref, str_ref,
         xlp, gy, sg, a_f, a_b, h_f, h_b, qb, kb, vb, kz_f, kz_b, mixo) = refs

    nt = n // ROW_TILE
    sh1 = mod_ref[0:1, :]
    sc1 = mod_ref[1:2, :]
    g1 = mod_ref[2:3, :]
    sh2 = mod_ref[3:4, :]
    sc2 = mod_ref[4:5, :]

    xlp[0:SUBLANES, :] = jnp.zeros((SUBLANES, LRU_W), _F32)
    xlp[n + SUBLANES:n + 2 * SUBLANES, :] = jnp.zeros((SUBLANES, LRU_W), _F32)

    def phase_a(r, carry):
        r0 = pl.multiple_of(r * ROW_TILE, ROW_TILE)
        rows = pl.ds(r0, ROW_TILE)
        x = x_ref[rows, :]
        h = _rms(x, nv_ref[0:1, :]) * (1.0 + sc1) + sh1
        p = _dot(h.astype(_BF), win_ref[...])
        xlp[pl.ds(r0 + SUBLANES, ROW_TILE), :] = p[:, 0:LRU_W]
        gy[rows, :] = _gelu_tanh(p[:, LRU_W:2 * LRU_W])
        sg[rows, :] = _silu(p[:, 2 * LRU_W + 3 * RET_W:])
        tpos = (r0 + lax.broadcasted_iota(jnp.int32, (ROW_TILE, DK), 0)).astype(_F32)
        if latent:
            cos2 = cos_ref[rows, :]
            sin2 = sin_ref[rows, :]
        for hd in range(RET_H):
            c0 = 2 * LRU_W + hd * DK
            q = p[:, c0:c0 + DK] * (DK ** -0.5)
            k = p[:, c0 + RET_W:c0 + RET_W + DK]
            v = p[:, c0 + 2 * RET_W:c0 + 2 * RET_W + DK]
            cols = slice(hd * DK, (hd + 1) * DK)
            if latent:
                q = q * cos2 + pltpu.roll(q, DK // 2, axis=1) * sin2
                k = k * cos2 + pltpu.roll(k, DK // 2, axis=1) * sin2
                qx_f[rows, cols] = (q * jnp.exp((tpos + 1.0) * LOG_G_FWD[hd])).astype(_BF)
                qx_b[rows, cols] = (q * jnp.exp((n - tpos) * LOG_G_BWD[hd])).astype(_BF)
            else:
                kz_f[rows, cols] = (k * jnp.exp((n - 1.0 - tpos) * LOG_G_FWD[hd])).astype(_BF)
                kz_b[rows, cols] = (k * jnp.exp(tpos * LOG_G_BWD[hd])).astype(_BF)
            qb[rows, cols] = q.astype(_BF)
            kb[rows, cols] = k.astype(_BF)
            vb[rows, cols] = v.astype(_BF)
        return carry

    lax.fori_loop(0, nt, phase_a, 0)

    half = LRU_W // 2

    def softplus_neg(lam):
        z = -lam
        return jnp.maximum(z, 0.0) + jnp.log1p(jnp.exp(-jnp.abs(z)))

    sp = (softplus_neg(lv_ref[7:8, :]), softplus_neg(lv_ref[10:11, :]))

    def phase_b(r, carry):
        r0 = pl.multiple_of(r * ROW_TILE, ROW_TILE)
        rows = pl.ds(r0, ROW_TILE)
        ext = xlp[pl.ds(r0, ROW_TILE + 2 * SUBLANES), :]
        xc = lv_ref[4:5, :]
        for tap in range(4):
            o = SUBLANES - 2 + tap
            xc = xc + ext[o:o + ROW_TILE, :] * lv_ref[tap:tap + 1, :]
        xcb = xc.astype(_BF)
        for d, (a_ref, u_ref) in enumerate(((a_f, h_f), (a_b, h_b))):
            ba = lv_ref[5 + 3 * d:6 + 3 * d, :]
            bi = lv_ref[6 + 3 * d:7 + 3 * d, :]
            for hh in range(2):
                cs = slice(hh * half, (hh + 1) * half)
                pre = _dot(xcb[:, cs], wg_ref[d, hh])
                rg = _sigmoid(pre[:, 0:half] + ba[:, cs])
                gi = _sigmoid(pre[:, half:] + bi[:, cs])
                log_a = -LRU_C * rg * sp[d][:, cs]
                a = jnp.exp(log_a)
                a_ref[rows, cs] = a
                u_ref[rows, cs] = jnp.sqrt(-jnp.tanh(log_a) * (a * a + 1.0)) * gi * xc[:, cs]
        return carry

    lax.fori_loop(0, nt, phase_b, 0)

    row_id = lax.broadcasted_iota(jnp.int32, (SUBLANES, LRU_W), 0)

    def scan_group(a, b, shift_of, mask_of):
        for s in (1, 2, 4):
            m = mask_of(s)
            a_s = jnp.where(m, pltpu.roll(a, shift_of(s), axis=0), 1.0)
            b_s = jnp.where(m, pltpu.roll(b, shift_of(s), axis=0), 0.0)
            b = a * b_s + b
            a = a * a_s
        return a, b

    if latent:
        init_f = h0_ref[0:1, :]
        init_b = h0_ref[1:2, :]
    else:
        init_f = jnp.zeros((1, LRU_W), _F32)
        init_b = jnp.zeros((1, LRU_W), _F32)
    ng = n // SUBLANES

    def fwd_body(g, carry):
        rows = pl.ds(pl.multiple_of(g * SUBLANES, SUBLANES), SUBLANES)
        a, b = scan_group(a_f[rows, :], h_f[rows, :], lambda s: s, lambda s: row_id >= s)
        h = a * carry + b
        h_f[rows, :] = h
        return h[SUBLANES - 1:SUBLANES, :]

    last_f = lax.fori_loop(0, ng, fwd_body, init_f, unroll=4)

    def bwd_body(i, carry):
        g = ng - 1 - i
        rows = pl.ds(pl.multiple_of(g * SUBLANES, SUBLANES), SUBLANES)
        a, b = scan_group(a_b[rows, :], h_b[rows, :], lambda s: SUBLANES - s,
                          lambda s: row_id < SUBLANES - s)
        h = a * carry + b
        h_b[rows, :] = h
        return h[0:1, :]

    first_b = lax.fori_loop(0, ng, bwd_body, init_b, unroll=4)

    if not latent:
        stl_ref[0:1, :] = last_f
        stl_ref[1:2, :] = first_b

    def lru_out(r, carry):
        rows = pl.ds(pl.multiple_of(r * ROW_TILE, ROW_TILE), ROW_TILE)
        mixo[rows, 0:LRU_W] = ((h_f[rows, :] + h_b[rows, :]) * gy[rows, :]).astype(_BF)
        return carry

    lax.fori_loop(0, nt, lru_out, 0)

    base = (lax.broadcasted_iota(jnp.int32, (ROW_TILE, n), 0)
            - lax.broadcasted_iota(jnp.int32, (ROW_TILE, n), 1)).astype(_F32)

    for hd in range(RET_H):
        cols = slice(hd * DK, (hd + 1) * DK)
        k_h = kb[:, cols]
        v_h = vb[:, cols]
        if latent:
            s0f = s0_ref[0, hd].astype(_BF)
            s0b = s0_ref[1, hd].astype(_BF)

        def ret_body(r, carry, cols=cols, k_h=k_h, v_h=v_h, hd=hd):
            r0 = pl.multiple_of(r * ROW_TILE, ROW_TILE)
            rows = pl.ds(r0, ROW_TILE)
            s = _dot_nt(qb[rows, cols], k_h)
            diff = base + r0.astype(_F32)
            decay = jnp.exp(jnp.where(diff >= 0.0, diff * LOG_G_FWD[hd], diff * (-LOG_G_BWD[hd])))
            o = _dot((s * decay).astype(_BF), v_h)
            if latent:
                o = o + _dot(qx_f[rows, cols], s0f) + _dot(qx_b[rows, cols], s0b)
            o = o * lax.rsqrt(jnp.mean(o * o, axis=-1, keepdims=True) + EPS)
            mixo[rows, LRU_W + hd * DK:LRU_W + (hd + 1) * DK] = (o * sg[rows, cols]).astype(_BF)
            return carry

        lax.fori_loop(0, nt, ret_body, 0)
        if not latent:
            str_ref[0, hd] = _dot_tn(kz_f[:, cols], v_h)
            str_ref[1, hd] = _dot_tn(kz_b[:, cols], v_h)

    rw = rwt_ref[...]
    rw_hi = rw.astype(_BF)
    rw_lo = (rw - rw_hi.astype(_F32)).astype(_BF)

    def phase_d(r, carry):
        r0 = pl.multiple_of(r * ROW_TILE, ROW_TILE)
        rows = pl.ds(r0, ROW_TILE)
        mix = _dot(mixo[rows, :], wout_ref[...])
        x1 = x_ref[rows, :] + g1 * _rms(mix, nv_ref[1:2, :])
        x1_ref[rows, :] = x1
        h2 = _rms(x1, nv_ref[2:3, :]) * (1.0 + sc2) + sh2
        h2_hi = h2.astype(_BF)
        h2_ref[rows, :] = h2_hi
        h2_lo = (h2 - h2_hi.astype(_F32)).astype(_BF)
        lt_ref[:, rows] = _dot_nt(rw_hi, h2_hi) + (_dot_nt(rw_hi, h2_lo) + _dot_nt(rw_lo, h2_hi))
        return carry

    lax.fori_loop(0, nt, phase_d, 0)


def _mixer_call(x, mod, nvec, w_in, w_out, lvec, wg, rwt, latent, extra=()):
    b, n, _ = x.shape
    const2 = lambda i: (0, 0)
    mod_idx = (lambda i: (i + 1, 0, 0)) if latent else (lambda i: (0, 0, 0))
    in_specs = [
        pl.BlockSpec((None, n, D), lambda i: (i, 0, 0)),
        pl.BlockSpec((None, N_MOD, D), mod_idx),
        pl.BlockSpec((8, D), const2),
        pl.BlockSpec((D, IN_W), const2),
        pl.BlockSpec((D, D), const2),
        pl.BlockSpec((16, LRU_W), const2),
        pl.BlockSpec((2, 2, LRU_W // 2, LRU_W), lambda i: (0, 0, 0, 0)),
        pl.BlockSpec((N_EXP, D), const2),
    ]
    out_shape = [
        jax.ShapeDtypeStruct((b, n, D), _F32),
        jax.ShapeDtypeStruct((b, n, D), _BF),
        jax.ShapeDtypeStruct((b, N_EXP, n), _F32),
    ]
    out_specs = [
        pl.BlockSpec((None, n, D), lambda i: (i, 0, 0)),
        pl.BlockSpec((None, n, D), lambda i: (i, 0, 0)),
        pl.BlockSpec((None, N_EXP, n), lambda i: (i, 0, 0)),
    ]
    if latent:
        in_specs += [
            pl.BlockSpec((None, 2, LRU_W), lambda i: (i, 0, 0)),
            pl.BlockSpec((None, 2, RET_H, DK, DK), lambda i: (i, 0, 0, 0, 0)),
            pl.BlockSpec((n, DK), const2),
            pl.BlockSpec((n, DK), const2),
        ]
    else:
        out_shape += [
            jax.ShapeDtypeStruct((b, 1, 2, LRU_W), _F32),
            jax.ShapeDtypeStruct((b, 1, 2, RET_H, DK, DK), _F32),
        ]
        out_specs += [
            pl.BlockSpec((None, None, 2, LRU_W), lambda i: (i, 0, 0, 0)),
            pl.BlockSpec((None, None, 2, RET_H, DK, DK), lambda i: (i, 0, 0, 0, 0, 0)),
        ]
    f32s = lambda shape: pltpu.VMEM(shape, _F32)
    bfs = lambda shape: pltpu.VMEM(shape, _BF)
    scratch = [
        f32s((n + 2 * SUBLANES, LRU_W)),
        f32s((n, LRU_W)), f32s((n, LRU_W)),
        f32s((n, LRU_W)), f32s((n, LRU_W)),
        f32s((n, LRU_W)), f32s((n, LRU_W)),
        bfs((n, RET_W)), bfs((n, RET_W)), bfs((n, RET_W)),
        bfs((n, RET_W)), bfs((n, RET_W)),
        bfs((n, D)),
    ]
    return pl.pallas_call(
        functools.partial(_mixer_kernel, n=n, latent=latent),
        out_shape=out_shape,
        grid=(b,),
        in_specs=in_specs,
        out_specs=out_specs,
        scratch_shapes=scratch,
        compiler_params=pltpu.CompilerParams(
            dimension_semantics=("arbitrary",), vmem_limit_bytes=VMEM_LIMIT),
        name="mixer_latent" if latent else "mixer_context",
    )(x, mod, nvec, w_in, w_out, lvec, wg, rwt, *extra)


def _select(l3, cap):
    bsz, _, n = l3.shape
    rows = bsz * N_EXP
    m = jnp.max(l3, axis=1, keepdims=True)
    e = jnp.exp(l3 - m)
    p = (e / jnp.sum(e, axis=1, keepdims=True)).reshape(rows, n)
    bits = pltpu.bitcast(p, jnp.int32)
    capf = float(cap)

    def count(mask):
        return jnp.sum(jnp.where(mask, 1.0, 0.0), axis=-1, keepdims=True)

    def val_body(i, t):
        cand = t | (jnp.int32(1) << (30 - i))
        return jnp.where(count(bits >= cand) >= capf, cand, t)

    thr = lax.fori_loop(0, 31, val_body, jnp.zeros((rows, 1), jnp.int32))
    gt = bits > thr
    eq = bits == thr
    need = capf - count(gt)
    idx = lax.broadcasted_iota(jnp.int32, (rows, n), 1)
    nbits = int(math.log2(n))

    def idx_body(i, j):
        cand = j | (jnp.int32(1) << (nbits - 1 - i))
        return jnp.where(count(eq & (idx < cand)) < need, cand, j)

    jlast = lax.fori_loop(0, nbits, idx_body, jnp.zeros((rows, 1), jnp.int32))
    sel = gt | (eq & (idx <= jlast))
    before = (lax.broadcasted_iota(jnp.int32, (n, n), 0)
              < lax.broadcasted_iota(jnp.int32, (n, n), 1))
    pos = _dot(jnp.where(sel, 1.0, 0.0).astype(_BF), jnp.where(before, 1.0, 0.0).astype(_BF))
    return jnp.where(sel, pos, -1.0), jnp.where(sel, p, 0.0)


def _route_kernel(lp_ref, ls_ref, pp_ref, gp_ref, ps_ref, gs_ref, *, cap_p, cap_s):
    pos, gate = _select(lp_ref[...], cap_p)
    pp_ref[...] = pos
    gp_ref[...] = gate
    pos, gate = _select(ls_ref[...], cap_s)
    ps_ref[...] = pos
    gs_ref[...] = gate


def _route_call(lt_p, lt_s, cap_p, cap_s):
    bp, _, n_p = lt_p.shape
    bs, _, n_s = lt_s.shape
    shapes = [
        jax.ShapeDtypeStruct((bp * N_EXP, n_p), _F32), jax.ShapeDtypeStruct((bp * N_EXP, n_p), _F32),
        jax.ShapeDtypeStruct((bs * N_EXP, n_s), _F32), jax.ShapeDtypeStruct((bs * N_EXP, n_s), _F32),
    ]
    return pl.pallas_call(
        functools.partial(_route_kernel, cap_p=cap_p, cap_s=cap_s),
        out_shape=shapes,
        compiler_params=pltpu.CompilerParams(vmem_limit_bytes=VMEM_LIMIT),
        name="route_select",
    )(lt_p, lt_s)


def _dispatch_kernel(pos_ref, gate_ref, h_ref, xs_ref, gs_ref, *, n, cap):
    slot = lax.broadcasted_iota(jnp.int32, (cap, n), 0).astype(_F32)
    parts = []
    for e in range(N_EXP):
        hit = pos_ref[e:e + 1, :] == slot
        parts.append(jnp.where(hit, 1.0, 0.0).astype(_BF))
        gs_ref[e] = jnp.sum(jnp.where(hit, gate_ref[e:e + 1, :], 0.0), axis=-1, keepdims=True)
    onehot = jnp.concatenate(parts, axis=0)
    xs = _dot(onehot, h_ref[...]).astype(_BF)
    for e in range(N_EXP):
        xs_ref[e] = xs[e * cap:(e + 1) * cap, :]


def _dispatch_call(pos, gate, h2, cap):
    b, n, _ = h2.shape
    return pl.pallas_call(
        functools.partial(_dispatch_kernel, n=n, cap=cap),
        out_shape=[
            jax.ShapeDtypeStruct((N_EXP, b * cap, D), _BF),
            jax.ShapeDtypeStruct((N_EXP, b * cap, 1), _F32),
        ],
        grid=(b,),
        in_specs=[
            pl.BlockSpec((None, N_EXP, n), lambda i: (i, 0, 0)),
            pl.BlockSpec((None, N_EXP, n), lambda i: (i, 0, 0)),
            pl.BlockSpec((None, n, D), lambda i: (i, 0, 0)),
        ],
        out_specs=[
            pl.BlockSpec((N_EXP, cap, D), lambda i: (0, i, 0)),
            pl.BlockSpec((N_EXP, cap, 1), lambda i: (0, i, 0)),
        ],
        compiler_params=pltpu.CompilerParams(
            dimension_semantics=("arbitrary",), vmem_limit_bytes=VMEM_LIMIT),
        name="dispatch",
    )(pos, gate, h2)


def _expert_kernel(xp_ref, xs_ref, gp_ref, gs_ref, wg_ref, wu_ref, wd_ref, y_ref, xcat, acc,
                   *, sp, nf):
    f = pl.program_id(1)

    @pl.when(f == 0)
    def _():
        xcat[0:sp, :] = xp_ref[...]
        xcat[sp:, :] = xs_ref[...]

    x = xcat[...]
    hg = _dot(x, wg_ref[...].astype(_BF))
    hu = _dot(x, wu_ref[...].astype(_BF))
    hid = (_silu(hg) * hu).astype(_BF)
    part = _dot(hid, wd_ref[...].astype(_BF))

    @pl.when(f == 0)
    def _():
        acc[...] = part

    @pl.when(f > 0)
    def _():
        acc[...] += part

    @pl.when(f == nf - 1)
    def _():
        y_ref[0:sp, :] = (acc[0:sp, :] * gp_ref[...]).astype(_BF)
        y_ref[sp:, :] = (acc[sp:, :] * gs_ref[...]).astype(_BF)


def _expert_call(xs_p, xs_s, g_p, g_s, w_gate, w_up, w_down):
    tf = 512
    sp = xs_p.shape[1]
    ss = xs_s.shape[1]
    nf = FF // tf
    return pl.pallas_call(
        functools.partial(_expert_kernel, sp=sp, nf=nf),
        out_shape=jax.ShapeDtypeStruct((N_EXP, sp + ss, D), _BF),
        grid=(N_EXP, nf),
        in_specs=[
            pl.BlockSpec((None, sp, D), lambda e, f: (e, 0, 0)),
            pl.BlockSpec((None, ss, D), lambda e, f: (e, 0, 0)),
            pl.BlockSpec((None, sp, 1), lambda e, f: (e, 0, 0)),
            pl.BlockSpec((None, ss, 1), lambda e, f: (e, 0, 0)),
            pl.BlockSpec((None, D, tf), lambda e, f: (e, 0, f)),
            pl.BlockSpec((None, D, tf), lambda e, f: (e, 0, f)),
            pl.BlockSpec((None, tf, D), lambda e, f: (e, f, 0)),
        ],
        out_specs=pl.BlockSpec((None, sp + ss, D), lambda e, f: (e, 0, 0)),
        scratch_shapes=[pltpu.VMEM((sp + ss, D), _BF), pltpu.VMEM((sp + ss, D), _F32)],
        compiler_params=pltpu.CompilerParams(
            dimension_semantics=("arbitrary", "arbitrary"), vmem_limit_bytes=VMEM_LIMIT),
        name="expert_ffn",
    )(xs_p, xs_s, g_p, g_s, w_gate, w_up, w_down)


def _combine_kernel(post_ref, ye_ref, x1_ref, mod_ref, nv_ref, y_ref, *, n, cap):
    width = N_EXP * cap
    lane = lax.broadcasted_iota(jnp.int32, (N_EXP, width), 1)
    expand = jnp.where(lane // cap == lax.broadcasted_iota(jnp.int32, (N_EXP, width), 0),
                       1.0, 0.0).astype(_BF)
    slot = (lax.broadcasted_iota(jnp.int32, (ROW_TILE, width), 1) % cap).astype(_F32)
    ye = ye_ref[...].reshape(width, D)
    g2 = mod_ref[5:6, :]

    def body(r, carry):
        rows = pl.ds(pl.multiple_of(r * ROW_TILE, ROW_TILE), ROW_TILE)
        pos_e = _dot(post_ref[rows, :], expand)
        onehot = jnp.where(pos_e == slot, 1.0, 0.0).astype(_BF)
        f = _dot(onehot, ye)
        y_ref[rows, :] = x1_ref[rows, :] + g2 * _rms(f, nv_ref[3:4, :])
        return carry

    lax.fori_loop(0, n // ROW_TILE, body, 0)


def _combine_call(post, ye, x1, mod, nvec, cap, slot_off, latent):
    b, n, _ = x1.shape
    blk_off = slot_off // cap
    mod_idx = (lambda i: (i + 1, 0, 0)) if latent else (lambda i: (0, 0, 0))
    return pl.pallas_call(
        functools.partial(_combine_kernel, n=n, cap=cap),
        out_shape=jax.ShapeDtypeStruct((b, n, D), _F32),
        grid=(b,),
        in_specs=[
            pl.BlockSpec((None, n, N_EXP), lambda i: (i, 0, 0)),
            pl.BlockSpec((N_EXP, cap, D), lambda i: (0, i + blk_off, 0)),
            pl.BlockSpec((None, n, D), lambda i: (i, 0, 0)),
            pl.BlockSpec((None, N_MOD, D), mod_idx),
            pl.BlockSpec((8, D), lambda i: (0, 0)),
        ],
        out_specs=pl.BlockSpec((None, n, D), lambda i: (i, 0, 0)),
        compiler_params=pltpu.CompilerParams(
            dimension_semantics=("arbitrary",), vmem_limit_bytes=VMEM_LIMIT),
        name="combine_latent" if latent else "combine_context",
    )(post, ye, x1, mod, nvec)


def _block_diag_gates(wa, wi):
    per_half = LRU_HEADS // 2
    eye = jnp.eye(per_half, dtype=wa.dtype)

    def bd(w):
        w = w.reshape(2, 2, per_half, LRU_HD, LRU_HD)
        full = jnp.einsum('dghij,hk->dghikj', w, eye)
        return full.reshape(2, 2, per_half * LRU_HD, per_half * LRU_HD)

    return jnp.concatenate([bd(wa), bd(wi)], axis=-1).astype(_BF)


def _rope_tables(n):
    t = np.arange(n)
    nf = DK // 4
    freqs = ROPE_BASE ** (-np.arange(nf, dtype=np.float32) / nf)
    ang = np.concatenate([(t // GRID_W)[:, None].astype(np.float32) * freqs,
                          (t % GRID_W)[:, None].astype(np.float32) * freqs], axis=-1).astype(np.float32)
    cos = np.cos(ang)
    sin = np.sin(ang)
    return (jnp.asarray(np.concatenate([cos, cos], axis=-1), _F32),
            jnp.asarray(np.concatenate([-sin, sin], axis=-1), _F32))


def kernel(x_prompt, x_sample, c, state_lru, state_ret, c_ctx, ada_w, ada_b, norm_mix_pre, norm_mix_post, norm_ffn_pre, norm_ffn_post, w_in, conv_w, conv_b, lru_wa, lru_ba, lru_wi, lru_bi, lru_lambda, w_out, router_w, exp_w_gate, exp_w_up, exp_w_down):
    bp, n_p, _ = x_prompt.shape
    bs, n_s, _ = x_sample.shape
    cap_p = 2 * n_p // N_EXP
    cap_s = 2 * n_s // N_EXP
    l = 0

    c_all = jnp.concatenate([c_ctx[None, :], c, jnp.zeros((8 - 1 - bs, D), _F32)], axis=0)
    mod = _ada_call(c_all, ada_w[l], ada_b[l][None, :]).reshape(8, N_MOD, D)

    nvec = jnp.concatenate([norm_mix_pre[l][None], norm_mix_post[l][None], norm_ffn_pre[l][None],
                            norm_ffn_post[l][None], jnp.zeros((4, D), _F32)], axis=0)
    lvec = jnp.concatenate([
        conv_w[l], conv_b[l][None],
        lru_ba[l, 0][None], lru_bi[l, 0][None], lru_lambda[l, 0][None],
        lru_ba[l, 1][None], lru_bi[l, 1][None], lru_lambda[l, 1][None],
        jnp.zeros((5, LRU_W), _F32)], axis=0)
    wg = _block_diag_gates(lru_wa[l], lru_wi[l])
    w_in_b = w_in[l].astype(_BF)
    w_out_b = w_out[l].astype(_BF)
    rwt = router_w[l].T
    cos2, sin2 = _rope_tables(n_s)

    x1_p, h2_p, lt_p, st_lru, st_ret = _mixer_call(
        x_prompt, mod, nvec, w_in_b, w_out_b, lvec, wg, rwt, latent=False)
    x1_s, h2_s, lt_s = _mixer_call(
        x_sample, mod, nvec, w_in_b, w_out_b, lvec, wg, rwt, latent=True,
        extra=(state_lru[:, l], state_ret[:, l], cos2, sin2))

    pos_p, gate_p, pos_s, gate_s = _route_call(lt_p, lt_s, cap_p, cap_s)
    pos_p = pos_p.reshape(bp, N_EXP, n_p)
    gate_p = gate_p.reshape(bp, N_EXP, n_p)
    pos_s = pos_s.reshape(bs, N_EXP, n_s)
    gate_s = gate_s.reshape(bs, N_EXP, n_s)

    xs_p, gsl_p = _dispatch_call(pos_p, gate_p, h2_p, cap_p)
    xs_s, gsl_s = _dispatch_call(pos_s, gate_s, h2_s, cap_s)
    ye = _expert_call(xs_p, xs_s, gsl_p, gsl_s, exp_w_gate[l], exp_w_up[l], exp_w_down[l])

    post_p = jnp.swapaxes(pos_p, 1, 2).astype(_BF)
    post_s = jnp.swapaxes(pos_s, 1, 2).astype(_BF)
    y_p = _combine_call(post_p, ye, x1_p, mod, nvec, cap_p, 0, latent=False)
    y_s = _combine_call(post_s, ye, x1_s, mod, nvec, cap_s, bp * cap_p, latent=True)
    return (y_p, y_s, st_lru, st_ret)
```

```python
import functools
import math

import jax
import jax.numpy as jnp
import numpy as np
from jax import lax
from jax.experimental import pallas as pl
from jax.experimental.pallas import tpu as pltpu

D = 1024
LRU_W = 512
LRU_HEADS = 8
LRU_HD = 64
LRU_C = 8.0
RET_W = 512
RET_H = 4
DK = 128
N_EXP = 16
FF = 2048
N_MOD = 6
EPS = 1e-6
GRID_W = 64
ROPE_BASE = 10000.0
IN_W = 3072

ROW_TILE = 256
HALF_TILE = 128
SUBLANES = 8
VMEM_LIMIT = 60 * 1024 * 1024

LOG_G_FWD = [math.log1p(-2.0 ** -(5.0 + h)) for h in range(RET_H)]
LOG_G_BWD = [math.log1p(-2.0 ** -(5.5 + h)) for h in range(RET_H)]

_BF = jnp.bfloat16
_F32 = jnp.float32


def _sigmoid(x):
    return 0.5 * jnp.tanh(0.5 * x) + 0.5


def _silu(x):
    return x * _sigmoid(x)


def _gelu_tanh(x):
    c = math.sqrt(2.0 / math.pi)
    return 0.5 * x * (1.0 + jnp.tanh(c * (x + 0.044715 * (x * x * x))))


def _rms(x, gain):
    return x * lax.rsqrt(jnp.mean(x * x, axis=-1, keepdims=True) + EPS) * gain


def _dot(a, b):
    return jnp.dot(a, b, preferred_element_type=_F32)


def _dot_nt(a, b):
    return lax.dot_general(a, b, (((1,), (1,)), ((), ())), preferred_element_type=_F32)


def _dot_tn(a, b):
    return lax.dot_general(a, b, (((0,), (0,)), ((), ())), preferred_element_type=_F32)


def _ada_kernel(c_ref, w_ref, b_ref, o_ref):
    s = _silu(c_ref[...]).astype(_BF)
    o_ref[...] = _dot(s, w_ref[...].astype(_BF)) + b_ref[...]


def _ada_call(c_all, ada_w, ada_b):
    tn = 1536
    n_out = ada_w.shape[1]
    return pl.pallas_call(
        _ada_kernel,
        out_shape=jax.ShapeDtypeStruct((8, n_out), _F32),
        grid=(n_out // tn,),
        in_specs=[
            pl.BlockSpec((8, D), lambda j: (0, 0)),
            pl.BlockSpec((D, tn), lambda j: (0, j)),
            pl.BlockSpec((1, tn), lambda j: (0, j)),
        ],
        out_specs=pl.BlockSpec((8, tn), lambda j: (0, j)),
        compiler_params=pltpu.CompilerParams(
            dimension_semantics=("arbitrary",), vmem_limit_bytes=VMEM_LIMIT),
        name="ada_mod",
    )(c_all, ada_w, ada_b)


def _mixer_kernel(*refs, n, latent):
    if latent:
        (x_ref, mod_ref, nv_ref, win_ref, wout_ref, lv_ref, wg_ref, rw_ref,
         h0_ref, s0_ref, cos_ref, sin_ref,
         x1_ref, h2_ref, lt_ref,
         xlp, gy, sg, a_f, a_b, h_f, h_b, q_f, q_b, k_f, k_b, vb, mixo) = refs
    else:
        (x_ref, mod_ref, nv_ref, win_ref, wout_ref, lv_ref, wg_ref, rw_ref,
         x1_ref, h2_ref, lt_ref, stl_ref, str_ref,
         xlp, gy, sg, a_f, a_b, h_f, h_b, q_f, q_b, k_f, k_b, vb, mixo) = refs

    nt = n // ROW_TILE
    sh1 = mod_ref[0:1, :]
    sc1 = mod_ref[1:2, :]
    g1 = mod_ref[2:3, :]
    sh2 = mod_ref[3:4, :]
    sc2 = mod_ref[4:5, :]

    xlp[0:SUBLANES, :] = jnp.zeros((SUBLANES, LRU_W), _F32)
    xlp[n + SUBLANES:n + 2 * SUBLANES, :] = jnp.zeros((SUBLANES, LRU_W), _F32)

    def phase_a(r, carry):
        r0 = pl.multiple_of(r * ROW_TILE, ROW_TILE)
        rows = pl.ds(r0, ROW_TILE)
        x = x_ref[rows, :]
        h = _rms(x, nv_ref[0:1, :]) * (1.0 + sc1) + sh1
        p = _dot(h.astype(_BF), win_ref[...])
        xlp[pl.ds(r0 + SUBLANES, ROW_TILE), :] = p[:, 0:LRU_W]
        gy[rows, :] = _gelu_tanh(p[:, LRU_W:2 * LRU_W])
        sg[rows, :] = _silu(p[:, 2 * LRU_W + 3 * RET_W:])
        tpos = (r0 + lax.broadcasted_iota(jnp.int32, (ROW_TILE, DK), 0)).astype(_F32)
        if latent:
            cos2 = cos_ref[rows, :]
            sin2 = sin_ref[rows, :]
        for hd in range(RET_H):
            c0 = 2 * LRU_W + hd * DK
            q = p[:, c0:c0 + DK] * (DK ** -0.5)
            k = p[:, c0 + RET_W:c0 + RET_W + DK]
            v = p[:, c0 + 2 * RET_W:c0 + 2 * RET_W + DK]
            cols = slice(hd * DK, (hd + 1) * DK)
            if latent:
                q = q * cos2 + pltpu.roll(q, DK // 2, axis=1) * sin2
                k = k * cos2 + pltpu.roll(k, DK // 2, axis=1) * sin2
            q_f[rows, cols] = (q * jnp.exp(tpos * LOG_G_FWD[hd])).astype(_BF)
            k_f[rows, cols] = (k * jnp.exp(tpos * (-LOG_G_FWD[hd]))).astype(_BF)
            q_b[rows, cols] = (q * jnp.exp(tpos * (-LOG_G_BWD[hd]))).astype(_BF)
            k_b[rows, cols] = (k * jnp.exp(tpos * LOG_G_BWD[hd])).astype(_BF)
            vb[rows, cols] = v.astype(_BF)
        return carry

    lax.fori_loop(0, nt, phase_a, 0)

    half = LRU_W // 2

    def softplus_neg(lam):
        z = -lam
        return jnp.maximum(z, 0.0) + jnp.log1p(jnp.exp(-jnp.abs(z)))

    sp = (softplus_neg(lv_ref[7:8, :]), softplus_neg(lv_ref[10:11, :]))

    def phase_b(r, carry):
        r0 = pl.multiple_of(r * ROW_TILE, ROW_TILE)
        rows = pl.ds(r0, ROW_TILE)
        ext = xlp[pl.ds(r0, ROW_TILE + 2 * SUBLANES), :]
        xc = lv_ref[4:5, :]
        for tap in range(4):
            o = SUBLANES - 2 + tap
            xc = xc + ext[o:o + ROW_TILE, :] * lv_ref[tap:tap + 1, :]
        xcb = xc.astype(_BF)
        for d, (a_ref, u_ref) in enumerate(((a_f, h_f), (a_b, h_b))):
            ba = lv_ref[5 + 3 * d:6 + 3 * d, :]
            bi = lv_ref[6 + 3 * d:7 + 3 * d, :]
            for hh in range(2):
                cs = slice(hh * half, (hh + 1) * half)
                pre = _dot(xcb[:, cs], wg_ref[d, hh])
                rg = _sigmoid(pre[:, 0:half] + ba[:, cs])
                gi = _sigmoid(pre[:, half:] + bi[:, cs])
                log_a = -LRU_C * rg * sp[d][:, cs]
                a = jnp.exp(log_a)
                a_ref[rows, cs] = a
                u_ref[rows, cs] = jnp.sqrt(-jnp.tanh(log_a) * (a * a + 1.0)) * gi * xc[:, cs]
        return carry

    lax.fori_loop(0, nt, phase_b, 0)

    row_id = lax.broadcasted_iota(jnp.int32, (SUBLANES, LRU_W), 0)

    def scan_group(a, b, shift_of, mask_of):
        for s in (1, 2, 4):
            m = mask_of(s)
            a_s = jnp.where(m, pltpu.roll(a, shift_of(s), axis=0), 1.0)
            b_s = jnp.where(m, pltpu.roll(b, shift_of(s), axis=0), 0.0)
            b = a * b_s + b
            a = a * a_s
        return a, b

    if latent:
        init_f = h0_ref[0:1, :]
        init_b = h0_ref[1:2, :]
    else:
        init_f = jnp.zeros((1, LRU_W), _F32)
        init_b = jnp.zeros((1, LRU_W), _F32)
    ng = n // SUBLANES

    def fwd_body(g, carry):
        rows = pl.ds(pl.multiple_of(g * SUBLANES, SUBLANES), SUBLANES)
        a, b = scan_group(a_f[rows, :], h_f[rows, :], lambda s: s, lambda s: row_id >= s)
        h = a * carry + b
        h_f[rows, :] = h
        return h[SUBLANES - 1:SUBLANES, :]

    last_f = lax.fori_loop(0, ng, fwd_body, init_f, unroll=4)

    def bwd_body(i, carry):
        g = ng - 1 - i
        rows = pl.ds(pl.multiple_of(g * SUBLANES, SUBLANES), SUBLANES)
        a, b = scan_group(a_b[rows, :], h_b[rows, :], lambda s: SUBLANES - s,
                          lambda s: row_id < SUBLANES - s)
        h = a * carry + b
        h_b[rows, :] = h
        return h[0:1, :]

    first_b = lax.fori_loop(0, ng, bwd_body, init_b, unroll=4)

    if not latent:
        stl_ref[0:1, :] = last_f
        stl_ref[1:2, :] = first_b

    def lru_out(r, carry):
        rows = pl.ds(pl.multiple_of(r * ROW_TILE, ROW_TILE), ROW_TILE)
        mixo[rows, 0:LRU_W] = ((h_f[rows, :] + h_b[rows, :]) * gy[rows, :]).astype(_BF)
        return carry

    lax.fori_loop(0, nt, lru_out, 0)

    lag = (lax.broadcasted_iota(jnp.int32, (ROW_TILE, n), 0)
           - lax.broadcasted_iota(jnp.int32, (ROW_TILE, n), 1))

    for hd in range(RET_H):
        cols = slice(hd * DK, (hd + 1) * DK)
        kf_h = k_f[:, cols]
        kb_h = k_b[:, cols]
        v_h = vb[:, cols]
        if latent:
            s0f = (s0_ref[0, hd] * math.exp(LOG_G_FWD[hd])).astype(_BF)
            s0b = (s0_ref[1, hd] * math.exp(n * LOG_G_BWD[hd])).astype(_BF)

        def ret_body(r, carry, cols=cols, kf_h=kf_h, kb_h=kb_h, v_h=v_h):
            r0 = pl.multiple_of(r * ROW_TILE, ROW_TILE)
            rows = pl.ds(r0, ROW_TILE)
            qf = q_f[rows, cols]
            qb = q_b[rows, cols]
            s = jnp.where(lag >= -r0, _dot_nt(qf, kf_h), _dot_nt(qb, kb_h))
            o = _dot(s.astype(_BF), v_h)
            if latent:
                o = o + _dot(qf, s0f) + _dot(qb, s0b)
            o = o * lax.rsqrt(jnp.mean(o * o, axis=-1, keepdims=True) + EPS)
            mixo[rows, LRU_W + cols.start:LRU_W + cols.stop] = (o * sg[rows, cols]).astype(_BF)
            return carry

        lax.fori_loop(0, nt, ret_body, 0)
        if not latent:
            str_ref[0, hd] = _dot_tn(kf_h, v_h) * math.exp((n - 1) * LOG_G_FWD[hd])
            str_ref[1, hd] = _dot_tn(kb_h, v_h)

    rw = rw_ref[...]
    rw_hi = rw.astype(_BF)
    rw_lo = (rw - rw_hi.astype(_F32)).astype(_BF)
    rw3 = jnp.concatenate([rw_hi, rw_hi, rw_lo], axis=0)

    def phase_d(r, carry):
        r0 = pl.multiple_of(r * HALF_TILE, HALF_TILE)
        rows = pl.ds(r0, HALF_TILE)
        mix = _dot(mixo[rows, :], wout_ref[...])
        x1 = x_ref[rows, :] + g1 * _rms(mix, nv_ref[1:2, :])
        x1_ref[rows, :] = x1
        h2 = _rms(x1, nv_ref[2:3, :]) * (1.0 + sc2) + sh2
        h2_hi = h2.astype(_BF)
        h2_ref[rows, :] = h2_hi
        h2_lo = (h2 - h2_hi.astype(_F32)).astype(_BF)
        logits = _dot(jnp.concatenate([h2_hi, h2_lo, h2_hi], axis=1), rw3)
        lt_ref[:, rows] = logits.T[0:N_EXP, :]
        return carry

    lax.fori_loop(0, n // HALF_TILE, phase_d, 0, unroll=2)


def _mixer_call(x, mod, nvec, w_in, w_out, lvec, wg, rwt, latent, extra=()):
    b, n, _ = x.shape
    const2 = lambda i: (0, 0)
    mod_idx = (lambda i: (i + 1, 0, 0)) if latent else (lambda i: (0, 0, 0))
    in_specs = [
        pl.BlockSpec((None, n, D), lambda i: (i, 0, 0)),
        pl.BlockSpec((None, N_MOD, D), mod_idx),
        pl.BlockSpec((8, D), const2),
        pl.BlockSpec((D, IN_W), const2),
        pl.BlockSpec((D, D), const2),
        pl.BlockSpec((16, LRU_W), const2),
        pl.BlockSpec((2, 2, LRU_W // 2, LRU_W), lambda i: (0, 0, 0, 0)),
        pl.BlockSpec((D, 128), const2),
    ]
    out_shape = [
        jax.ShapeDtypeStruct((b, n, D), _F32),
        jax.ShapeDtypeStruct((b, n, D), _BF),
        jax.ShapeDtypeStruct((b, N_EXP, n), _F32),
    ]
    out_specs = [
        pl.BlockSpec((None, n, D), lambda i: (i, 0, 0)),
        pl.BlockSpec((None, n, D), lambda i: (i, 0, 0)),
        pl.BlockSpec((None, N_EXP, n), lambda i: (i, 0, 0)),
    ]
    if latent:
        in_specs += [
            pl.BlockSpec((None, 2, LRU_W), lambda i: (i, 0, 0)),
            pl.BlockSpec((None, 2, RET_H, DK, DK), lambda i: (i, 0, 0, 0, 0)),
            pl.BlockSpec((n, DK), const2),
            pl.BlockSpec((n, DK), const2),
        ]
    else:
        out_shape += [
            jax.ShapeDtypeStruct((b, 1, 2, LRU_W), _F32),
            jax.ShapeDtypeStruct((b, 1, 2, RET_H, DK, DK), _F32),
        ]
        out_specs += [
            pl.BlockSpec((None, None, 2, LRU_W), lambda i: (i, 0, 0, 0)),
            pl.BlockSpec((None, None, 2, RET_H, DK, DK), lambda i: (i, 0, 0, 0, 0, 0)),
        ]
    f32s = lambda shape: pltpu.VMEM(shape, _F32)
    bfs = lambda shape: pltpu.VMEM(shape, _BF)
    scratch = [
        f32s((n + 2 * SUBLANES, LRU_W)),
        f32s((n, LRU_W)), f32s((n, LRU_W)),
        f32s((n, LRU_W)), f32s((n, LRU_W)),
        f32s((n, LRU_W)), f32s((n, LRU_W)),
        bfs((n, RET_W)), bfs((n, RET_W)),
        bfs((n, RET_W)), bfs((n, RET_W)), bfs((n, RET_W)),
        bfs((n, D)),
    ]
    return pl.pallas_call(
        functools.partial(_mixer_kernel, n=n, latent=latent),
        out_shape=out_shape,
        grid=(b,),
        in_specs=in_specs,
        out_specs=out_specs,
        scratch_shapes=scratch,
        compiler_params=pltpu.CompilerParams(
            dimension_semantics=("arbitrary",), vmem_limit_bytes=VMEM_LIMIT),
        name="mixer_latent" if latent else "mixer_context",
    )(x, mod, nvec, w_in, w_out, lvec, wg, rwt, *extra)


def _select(l3, cap):
    bsz, _, n = l3.shape
    rows = bsz * N_EXP
    m = jnp.max(l3, axis=1, keepdims=True)
    e = jnp.exp(l3 - m)
    p = (e / jnp.sum(e, axis=1, keepdims=True)).reshape(rows, n)
    bits = pltpu.bitcast(p, jnp.int32)
    capf = float(cap)

    def count(mask):
        return jnp.sum(jnp.where(mask, 1.0, 0.0), axis=-1, keepdims=True)

    def val_body(i, t):
        cand = t | (jnp.int32(1) << (30 - i))
        return jnp.where(count(bits >= cand) >= capf, cand, t)

    thr = lax.fori_loop(0, 31, val_body, jnp.zeros((rows, 1), jnp.int32))
    gt = bits > thr
    eq = bits == thr
    need = capf - count(gt)
    idx = lax.broadcasted_iota(jnp.int32, (rows, n), 1)
    nbits = int(math.log2(n))

    def idx_body(i, j):
        cand = j | (jnp.int32(1) << (nbits - 1 - i))
        return jnp.where(count(eq & (idx < cand)) < need, cand, j)

    jlast = lax.fori_loop(0, nbits, idx_body, jnp.zeros((rows, 1), jnp.int32))
    sel = gt | (eq & (idx <= jlast))
    before = (lax.broadcasted_iota(jnp.int32, (n, n), 0)
              < lax.broadcasted_iota(jnp.int32, (n, n), 1))
    pos = _dot(jnp.where(sel, 1.0, 0.0).astype(_BF), jnp.where(before, 1.0, 0.0).astype(_BF))
    return jnp.where(sel, pos, -1.0), jnp.where(sel, p, 0.0)


def _route_kernel(lp_ref, ls_ref, pp_ref, gp_ref, ps_ref, gs_ref, *, cap_p, cap_s):
    pos, gate = _select(lp_ref[...], cap_p)
    pp_ref[...] = pos
    gp_ref[...] = gate
    pos, gate = _select(ls_ref[...], cap_s)
    ps_ref[...] = pos
    gs_ref[...] = gate


def _route_call(lt_p, lt_s, cap_p, cap_s):
    bp, _, n_p = lt_p.shape
    bs, _, n_s = lt_s.shape
    shapes = [
        jax.ShapeDtypeStruct((bp * N_EXP, n_p), _F32), jax.ShapeDtypeStruct((bp * N_EXP, n_p), _F32),
        jax.ShapeDtypeStruct((bs * N_EXP, n_s), _F32), jax.ShapeDtypeStruct((bs * N_EXP, n_s), _F32),
    ]
    return pl.pallas_call(
        functools.partial(_route_kernel, cap_p=cap_p, cap_s=cap_s),
        out_shape=shapes,
        compiler_params=pltpu.CompilerParams(vmem_limit_bytes=VMEM_LIMIT),
        name="route_select",
    )(lt_p, lt_s)


def _dispatch_kernel(pos_ref, gate_ref, h_ref, xs_ref, gs_ref, *, n, cap):
    slot = lax.broadcasted_iota(jnp.int32, (cap, n), 0).astype(_F32)
    parts = []
    for e in range(N_EXP):
        hit = pos_ref[e:e + 1, :] == slot
        parts.append(jnp.where(hit, 1.0, 0.0).astype(_BF))
        gs_ref[e] = jnp.sum(jnp.where(hit, gate_ref[e:e + 1, :], 0.0), axis=-1, keepdims=True)
    onehot = jnp.concatenate(parts, axis=0)
    xs = _dot(onehot, h_ref[...]).astype(_BF)
    for e in range(N_EXP):
        xs_ref[e] = xs[e * cap:(e + 1) * cap, :]


def _dispatch_call(pos, gate, h2, cap):
    b, n, _ = h2.shape
    return pl.pallas_call(
        functools.partial(_dispatch_kernel, n=n, cap=cap),
        out_shape=[
            jax.ShapeDtypeStruct((N_EXP, b * cap, D), _BF),
            jax.ShapeDtypeStruct((N_EXP, b * cap, 1), _F32),
        ],
        grid=(b,),
        in_specs=[
            pl.BlockSpec((None, N_EXP, n), lambda i: (i, 0, 0)),
            pl.BlockSpec((None, N_EXP, n), lambda i: (i, 0, 0)),
            pl.BlockSpec((None, n, D), lambda i: (i, 0, 0)),
        ],
        out_specs=[
            pl.BlockSpec((N_EXP, cap, D), lambda i: (0, i, 0)),
            pl.BlockSpec((N_EXP, cap, 1), lambda i: (0, i, 0)),
        ],
        compiler_params=pltpu.CompilerParams(
            dimension_semantics=("arbitrary",), vmem_limit_bytes=VMEM_LIMIT),
        name="dispatch",
    )(pos, gate, h2)


def _expert_kernel(xp_ref, xs_ref, gp_ref, gs_ref, wg_ref, wu_ref, wd_ref, y_ref, xcat, acc,
                   *, sp, nf):
    f = pl.program_id(1)

    @pl.when(f == 0)
    def _():
        xcat[0:sp, :] = xp_ref[...]
        xcat[sp:, :] = xs_ref[...]
        acc[...] = jnp.zeros_like(acc)

    x = xcat[...]
    hg = _dot(x, wg_ref[...].astype(_BF))
    hu = _dot(x, wu_ref[...].astype(_BF))
    hid = (_silu(hg) * hu).astype(_BF)
    acc[...] += _dot(hid, wd_ref[...].astype(_BF))

    @pl.when(f == nf - 1)
    def _():
        y_ref[0:sp, :] = (acc[0:sp, :] * gp_ref[...]).astype(_BF)
        y_ref[sp:, :] = (acc[sp:, :] * gs_ref[...]).astype(_BF)


def _expert_call(xs_p, xs_s, g_p, g_s, w_gate, w_up, w_down):
    tf = 512
    sp = xs_p.shape[1]
    ss = xs_s.shape[1]
    nf = FF // tf
    return pl.pallas_call(
        functools.partial(_expert_kernel, sp=sp, nf=nf),
        out_shape=jax.ShapeDtypeStruct((N_EXP, sp + ss, D), _BF),
        grid=(N_EXP, nf),
        in_specs=[
            pl.BlockSpec((None, sp, D), lambda e, f: (e, 0, 0)),
            pl.BlockSpec((None, ss, D), lambda e, f: (e, 0, 0)),
            pl.BlockSpec((None, sp, 1), lambda e, f: (e, 0, 0)),
            pl.BlockSpec((None, ss, 1), lambda e, f: (e, 0, 0)),
            pl.BlockSpec((None, D, tf), lambda e, f: (e, 0, f)),
            pl.BlockSpec((None, D, tf), lambda e, f: (e, 0, f)),
            pl.BlockSpec((None, tf, D), lambda e, f: (e, f, 0)),
        ],
        out_specs=pl.BlockSpec((None, sp + ss, D), lambda e, f: (e, 0, 0)),
        scratch_shapes=[pltpu.VMEM((sp + ss, D), _BF), pltpu.VMEM((sp + ss, D), _F32)],
        compiler_params=pltpu.CompilerParams(
            dimension_semantics=("arbitrary", "arbitrary"), vmem_limit_bytes=VMEM_LIMIT),
        name="expert_ffn",
    )(xs_p, xs_s, g_p, g_s, w_gate, w_up, w_down)


def _combine_kernel(post_ref, ye_ref, x1_ref, mod_ref, nv_ref, y_ref, *, n, cap):
    width = N_EXP * cap
    lane = lax.broadcasted_iota(jnp.int32, (N_EXP, width), 1)
    expand = jnp.where(lane // cap == lax.broadcasted_iota(jnp.int32, (N_EXP, width), 0),
                       1.0, 0.0).astype(_BF)
    slot = (lax.broadcasted_iota(jnp.int32, (ROW_TILE, width), 1) % cap).astype(_F32)
    ye = ye_ref[...].reshape(width, D)
    g2 = mod_ref[5:6, :]

    def body(r, carry):
        rows = pl.ds(pl.multiple_of(r * ROW_TILE, ROW_TILE), ROW_TILE)
        pos_e = _dot(post_ref[rows, :], expand)
        onehot = jnp.where(pos_e == slot, 1.0, 0.0).astype(_BF)
        f = _dot(onehot, ye)
        y_ref[rows, :] = x1_ref[rows, :] + g2 * _rms(f, nv_ref[3:4, :])
        return carry

    lax.fori_loop(0, n // ROW_TILE, body, 0)


def _combine_call(post, ye, x1, mod, nvec, cap, slot_off, latent):
    b, n, _ = x1.shape
    blk_off = slot_off // cap
    mod_idx = (lambda i: (i + 1, 0, 0)) if latent else (lambda i: (0, 0, 0))
    return pl.pallas_call(
        functools.partial(_combine_kernel, n=n, cap=cap),
        out_shape=jax.ShapeDtypeStruct((b, n, D), _F32),
        grid=(b,),
        in_specs=[
            pl.BlockSpec((None, n, N_EXP), lambda i: (i, 0, 0)),
            pl.BlockSpec((N_EXP, cap, D), lambda i: (0, i + blk_off, 0)),
            pl.BlockSpec((None, n, D), lambda i: (i, 0, 0)),
            pl.BlockSpec((None, N_MOD, D), mod_idx),
            pl.BlockSpec((8, D), lambda i: (0, 0)),
        ],
        out_specs=pl.BlockSpec((None, n, D), lambda i: (i, 0, 0)),
        compiler_params=pltpu.CompilerParams(
            dimension_semantics=("arbitrary",), vmem_limit_bytes=VMEM_LIMIT),
        name="combine_latent" if latent else "combine_context",
    )(post, ye, x1, mod, nvec)


def _block_diag_gates(wa, wi):
    per_half = LRU_HEADS // 2
    eye = jnp.eye(per_half, dtype=wa.dtype)

    def bd(w):
        w = w.reshape(2, 2, per_half, LRU_HD, LRU_HD)
        full = jnp.einsum('dghij,hk->dghikj', w, eye)
        return full.reshape(2, 2, per_half * LRU_HD, per_half * LRU_HD)

    return jnp.concatenate([bd(wa), bd(wi)], axis=-1).astype(_BF)


def _rope_tables(n):
    t = np.arange(n)
    nf = DK // 4
    freqs = ROPE_BASE ** (-np.arange(nf, dtype=np.float32) / nf)
    ang = np.concatenate([(t // GRID_W)[:, None].astype(np.float32) * freqs,
                          (t % GRID_W)[:, None].astype(np.float32) * freqs], axis=-1).astype(np.float32)
    cos = np.cos(ang)
    sin = np.sin(ang)
    return (jnp.asarray(np.concatenate([cos, cos], axis=-1), _F32),
            jnp.asarray(np.concatenate([-sin, sin], axis=-1), _F32))


def kernel(x_prompt, x_sample, c, state_lru, state_ret, c_ctx, ada_w, ada_b, norm_mix_pre, norm_mix_post, norm_ffn_pre, norm_ffn_post, w_in, conv_w, conv_b, lru_wa, lru_ba, lru_wi, lru_bi, lru_lambda, w_out, router_w, exp_w_gate, exp_w_up, exp_w_down):
    bp, n_p, _ = x_prompt.shape
    bs, n_s, _ = x_sample.shape
    cap_p = 2 * n_p // N_EXP
    cap_s = 2 * n_s // N_EXP
    l = 0

    c_all = jnp.concatenate([c_ctx[None, :], c, jnp.zeros((8 - 1 - bs, D), _F32)], axis=0)
    mod = _ada_call(c_all, ada_w[l], ada_b[l][None, :]).reshape(8, N_MOD, D)

    nvec = jnp.concatenate([norm_mix_pre[l][None], norm_mix_post[l][None], norm_ffn_pre[l][None],
                            norm_ffn_post[l][None], jnp.zeros((4, D), _F32)], axis=0)
    lvec = jnp.concatenate([
        conv_w[l], conv_b[l][None],
        lru_ba[l, 0][None], lru_bi[l, 0][None], lru_lambda[l, 0][None],
        lru_ba[l, 1][None], lru_bi[l, 1][None], lru_lambda[l, 1][None],
        jnp.zeros((5, LRU_W), _F32)], axis=0)
    wg = _block_diag_gates(lru_wa[l], lru_wi[l])
    w_in_b = w_in[l].astype(_BF)
    w_out_b = w_out[l].astype(_BF)
    rwt = jnp.pad(router_w[l], ((0, 0), (0, 128 - N_EXP)))
    cos2, sin2 = _rope_tables(n_s)

    x1_p, h2_p, lt_p, st_lru, st_ret = _mixer_call(
        x_prompt, mod, nvec, w_in_b, w_out_b, lvec, wg, rwt, latent=False)
    x1_s, h2_s, lt_s = _mixer_call(
        x_sample, mod, nvec, w_in_b, w_out_b, lvec, wg, rwt, latent=True,
        extra=(state_lru[:, l], state_ret[:, l], cos2, sin2))

    pos_p, gate_p, pos_s, gate_s = _route_call(lt_p, lt_s, cap_p, cap_s)
    pos_p = pos_p.reshape(bp, N_EXP, n_p)
    gate_p = gate_p.reshape(bp, N_EXP, n_p)
    pos_s = pos_s.reshape(bs, N_EXP, n_s)
    gate_s = gate_s.reshape(bs, N_EXP, n_s)

    xs_p, gsl_p = _dispatch_call(pos_p, gate_p, h2_p, cap_p)
    xs_s, gsl_s = _dispatch_call(pos_s, gate_s, h2_s, cap_s)
    ye = _expert_call(xs_p, xs_s, gsl_p, gsl_s, exp_w_gate[l], exp_w_up[l], exp_w_down[l])

    post_p = jnp.swapaxes(pos_p, 1, 2).astype(_BF)
    post_s = jnp.swapaxes(pos_s, 1, 2).astype(_BF)
    y_p = _combine_call(post_p, ye, x1_p, mod, nvec, cap_p, 0, latent=False)
    y_s = _combine_call(post_s, ye, x1_s, mod, nvec, cap_s, bp * cap_p, latent=True)
    return (y_p, y_s, st_lru, st_ret)
```

```python
import functools
import math

import jax
import jax.numpy as jnp
import numpy as np
from jax import lax
from jax.experimental import pallas as pl
from jax.experimental.pallas import tpu as pltpu

D = 1024
LRU_W = 512
LRU_HEADS = 8
LRU_HD = 64
LRU_C = 8.0
RET_W = 512
RET_H = 4
DK = 128
N_EXP = 16
FF = 2048
N_MOD = 6
EPS = 1e-6
GRID_W = 64
ROPE_BASE = 10000.0
IN_W = 3072

ROW_TILE = 256
FF_CHUNK = 512
SUBLANES = 8
VMEM_LIMIT = 60 * 1024 * 1024

LOG_G_FWD = [math.log1p(-2.0 ** -(5.0 + h)) for h in range(RET_H)]
LOG_G_BWD = [math.log1p(-2.0 ** -(5.5 + h)) for h in range(RET_H)]

_BF = jnp.bfloat16
_F32 = jnp.float32


def _sigmoid(x):
    return 0.5 * jnp.tanh(0.5 * x) + 0.5


def _silu(x):
    return x * _sigmoid(x)


def _gelu_tanh(x):
    c = math.sqrt(2.0 / math.pi)
    return 0.5 * x * (1.0 + jnp.tanh(c * (x + 0.044715 * (x * x * x))))


def _rms(x, gain):
    return x * lax.rsqrt(jnp.mean(x * x, axis=-1, keepdims=True) + EPS) * gain


def _dot(a, b):
    return jnp.dot(a, b, preferred_element_type=_F32)


def _dot_nt(a, b):
    return lax.dot_general(a, b, (((1,), (1,)), ((), ())), preferred_element_type=_F32)


def _dot_tn(a, b):
    return lax.dot_general(a, b, (((0,), (0,)), ((), ())), preferred_element_type=_F32)


def _ada_kernel(c_ref, w_ref, b_ref, o_ref):
    s = _silu(c_ref[...]).astype(_BF)
    o_ref[...] = _dot(s, w_ref[...].astype(_BF)) + b_ref[...]


def _ada_call(c_all, ada_w, ada_b):
    tn = 1536
    n_out = ada_w.shape[1]
    return pl.pallas_call(
        _ada_kernel,
        out_shape=jax.ShapeDtypeStruct((8, n_out), _F32),
        grid=(n_out // tn,),
        in_specs=[
            pl.BlockSpec((8, D), lambda j: (0, 0)),
            pl.BlockSpec((D, tn), lambda j: (0, j)),
            pl.BlockSpec((1, tn), lambda j: (0, j)),
        ],
        out_specs=pl.BlockSpec((8, tn), lambda j: (0, j)),
        compiler_params=pltpu.CompilerParams(
            dimension_semantics=("arbitrary",), vmem_limit_bytes=VMEM_LIMIT),
        name="ada_mod",
    )(c_all, ada_w, ada_b)


def _mixer_kernel(*refs, n, latent):
    if latent:
        (x_ref, mod_ref, nv_ref, win_ref, wout_ref, lv_ref, wg_ref, rw_ref,
         h0_ref, s0_ref, cos_ref, sin_ref,
         x1_ref, h2_ref, lt_ref,
         xlp, gy, sg, a_f, a_b, h_f, h_b, q_f, q_b, k_f, k_b, vb, mixo) = refs
    else:
        (x_ref, mod_ref, nv_ref, win_ref, wout_ref, lv_ref, wg_ref, rw_ref,
         x1_ref, h2_ref, lt_ref, stl_ref, str_ref,
         xlp, gy, sg, a_f, a_b, h_f, h_b, q_f, q_b, k_f, k_b, vb, mixo) = refs

    nt = n // ROW_TILE
    sh1 = mod_ref[0:1, :]
    sc1 = mod_ref[1:2, :]
    g1 = mod_ref[2:3, :]
    sh2 = mod_ref[3:4, :]
    sc2 = mod_ref[4:5, :]

    xlp[0:SUBLANES, :] = jnp.zeros((SUBLANES, LRU_W), _F32)
    xlp[n + SUBLANES:n + 2 * SUBLANES, :] = jnp.zeros((SUBLANES, LRU_W), _F32)

    def phase_a(r, carry):
        r0 = pl.multiple_of(r * ROW_TILE, ROW_TILE)
        rows = pl.ds(r0, ROW_TILE)
        x = x_ref[rows, :]
        h = _rms(x, nv_ref[0:1, :]) * (1.0 + sc1) + sh1
        p = _dot(h.astype(_BF), win_ref[...])
        xlp[pl.ds(r0 + SUBLANES, ROW_TILE), :] = p[:, 0:LRU_W]
        gy[rows, :] = _gelu_tanh(p[:, LRU_W:2 * LRU_W])
        sg[rows, :] = _silu(p[:, 2 * LRU_W + 3 * RET_W:])
        tpos = (r0 + lax.broadcasted_iota(jnp.int32, (ROW_TILE, DK), 0)).astype(_F32)
        if latent:
            cos2 = cos_ref[rows, :]
            sin2 = sin_ref[rows, :]
        for hd in range(RET_H):
            c0 = 2 * LRU_W + hd * DK
            q = p[:, c0:c0 + DK] * (DK ** -0.5)
            k = p[:, c0 + RET_W:c0 + RET_W + DK]
            v = p[:, c0 + 2 * RET_W:c0 + 2 * RET_W + DK]
            cols = slice(hd * DK, (hd + 1) * DK)
            if latent:
                q = q * cos2 + pltpu.roll(q, DK // 2, axis=1) * sin2
                k = k * cos2 + pltpu.roll(k, DK // 2, axis=1) * sin2
            q_f[rows, cols] = (q * jnp.exp(tpos * LOG_G_FWD[hd])).astype(_BF)
            k_f[rows, cols] = (k * jnp.exp(tpos * (-LOG_G_FWD[hd]))).astype(_BF)
            q_b[rows, cols] = (q * jnp.exp(tpos * (-LOG_G_BWD[hd]))).astype(_BF)
            k_b[rows, cols] = (k * jnp.exp(tpos * LOG_G_BWD[hd])).astype(_BF)
            vb[rows, cols] = v.astype(_BF)
        return carry

    lax.fori_loop(0, nt, phase_a, 0)

    half = LRU_W // 2

    def softplus_neg(lam):
        z = -lam
        return jnp.maximum(z, 0.0) + jnp.log1p(jnp.exp(-jnp.abs(z)))

    sp = (softplus_neg(lv_ref[7:8, :]), softplus_neg(lv_ref[10:11, :]))

    def phase_b(r, carry):
        r0 = pl.multiple_of(r * ROW_TILE, ROW_TILE)
        rows = pl.ds(r0, ROW_TILE)
        ext = xlp[pl.ds(r0, ROW_TILE + 2 * SUBLANES), :]
        xc = lv_ref[4:5, :]
        for tap in range(4):
            o = SUBLANES - 2 + tap
            xc = xc + ext[o:o + ROW_TILE, :] * lv_ref[tap:tap + 1, :]
        xcb = xc.astype(_BF)
        for d, (a_ref, u_ref) in enumerate(((a_f, h_f), (a_b, h_b))):
            ba = lv_ref[5 + 3 * d:6 + 3 * d, :]
            bi = lv_ref[6 + 3 * d:7 + 3 * d, :]
            for hh in range(2):
                cs = slice(hh * half, (hh + 1) * half)
                pre = _dot(xcb[:, cs], wg_ref[d, hh])
                rg = _sigmoid(pre[:, 0:half] + ba[:, cs])
                gi = _sigmoid(pre[:, half:] + bi[:, cs])
                log_a = -LRU_C * rg * sp[d][:, cs]
                a = jnp.exp(log_a)
                a_ref[rows, cs] = a
                u_ref[rows, cs] = jnp.sqrt(-jnp.tanh(log_a) * (a * a + 1.0)) * gi * xc[:, cs]
        return carry

    lax.fori_loop(0, nt, phase_b, 0)

    row_id = lax.broadcasted_iota(jnp.int32, (SUBLANES, LRU_W), 0)

    def scan_group(a, b, shift_of, mask_of):
        for s in (1, 2, 4):
            m = mask_of(s)
            a_s = jnp.where(m, pltpu.roll(a, shift_of(s), axis=0), 1.0)
            b_s = jnp.where(m, pltpu.roll(b, shift_of(s), axis=0), 0.0)
            b = a * b_s + b
            a = a * a_s
        return a, b

    if latent:
        init_f = h0_ref[0:1, :]
        init_b = h0_ref[1:2, :]
    else:
        init_f = jnp.zeros((1, LRU_W), _F32)
        init_b = jnp.zeros((1, LRU_W), _F32)
    ng = n // SUBLANES

    def fwd_body(g, carry):
        rows = pl.ds(pl.multiple_of(g * SUBLANES, SUBLANES), SUBLANES)
        a, b = scan_group(a_f[rows, :], h_f[rows, :], lambda s: s, lambda s: row_id >= s)
        h = a * carry + b
        h_f[rows, :] = h
        return h[SUBLANES - 1:SUBLANES, :]

    last_f = lax.fori_loop(0, ng, fwd_body, init_f, unroll=4)

    def bwd_body(i, carry):
        g = ng - 1 - i
        rows = pl.ds(pl.multiple_of(g * SUBLANES, SUBLANES), SUBLANES)
        a, b = scan_group(a_b[rows, :], h_b[rows, :], lambda s: SUBLANES - s,
                          lambda s: row_id < SUBLANES - s)
        h = a * carry + b
        h_b[rows, :] = h
        return h[0:1, :]

    first_b = lax.fori_loop(0, ng, bwd_body, init_b, unroll=4)

    if not latent:
        stl_ref[0:1, :] = last_f
        stl_ref[1:2, :] = first_b

    def lru_out(r, carry):
        rows = pl.ds(pl.multiple_of(r * ROW_TILE, ROW_TILE), ROW_TILE)
        mixo[rows, 0:LRU_W] = ((h_f[rows, :] + h_b[rows, :]) * gy[rows, :]).astype(_BF)
        return carry

    lax.fori_loop(0, nt, lru_out, 0)

    lower = (lax.broadcasted_iota(jnp.int32, (ROW_TILE, ROW_TILE), 0)
             >= lax.broadcasted_iota(jnp.int32, (ROW_TILE, ROW_TILE), 1))
    blocks = [slice(r * ROW_TILE, (r + 1) * ROW_TILE) for r in range(nt)]

    for hd in range(RET_H):
        cols = slice(hd * DK, (hd + 1) * DK)
        kv_f = [_dot_tn(k_f[rows, cols], vb[rows, cols]) if (r < nt - 1 or not latent) else 0.0
                for r, rows in enumerate(blocks)]
        kv_b = [_dot_tn(k_b[rows, cols], vb[rows, cols]) if (r > 0 or not latent) else 0.0
                for r, rows in enumerate(blocks)]
        if latent:
            run_f = s0_ref[0, hd] * math.exp(LOG_G_FWD[hd])
            run_b = s0_ref[1, hd] * math.exp(n * LOG_G_BWD[hd])
        else:
            run_f = run_b = None
        before = []
        for r in range(nt):
            before.append(run_f)
            run_f = kv_f[r] if run_f is None else run_f + kv_f[r]
        after = [None] * nt
        for r in reversed(range(nt)):
            after[r] = run_b
            run_b = kv_b[r] if run_b is None else run_b + kv_b[r]

        for r, rows in enumerate(blocks):
            qf = q_f[rows, cols]
            qb = q_b[rows, cols]
            s = jnp.where(lower, _dot_nt(qf, k_f[rows, cols]), _dot_nt(qb, k_b[rows, cols]))
            o = _dot(s.astype(_BF), vb[rows, cols])
            if before[r] is not None:
                o = o + _dot(qf, before[r].astype(_BF))
            if after[r] is not None:
                o = o + _dot(qb, after[r].astype(_BF))
            o = o * lax.rsqrt(jnp.mean(o * o, axis=-1, keepdims=True) + EPS)
            mixo[rows, LRU_W + cols.start:LRU_W + cols.stop] = (o * sg[rows, cols]).astype(_BF)
        if not latent:
            str_ref[0, hd] = run_f * math.exp((n - 1) * LOG_G_FWD[hd])
            str_ref[1, hd] = run_b

    rw = rw_ref[...]
    rw_hi = rw.astype(_BF)
    rw_lo = (rw - rw_hi.astype(_F32)).astype(_BF)
    rw3 = jnp.concatenate([rw_hi, rw_hi, rw_lo], axis=0)

    def phase_d(r, carry):
        r0 = pl.multiple_of(r * ROW_TILE, ROW_TILE)
        rows = pl.ds(r0, ROW_TILE)
        mix = _dot(mixo[rows, :], wout_ref[...])
        x1 = x_ref[rows, :] + g1 * _rms(mix, nv_ref[1:2, :])
        x1_ref[rows, :] = x1
        h2 = _rms(x1, nv_ref[2:3, :]) * (1.0 + sc2) + sh2
        h2_hi = h2.astype(_BF)
        h2_ref[rows, :] = h2_hi
        h2_lo = (h2 - h2_hi.astype(_F32)).astype(_BF)
        logits = _dot(jnp.concatenate([h2_hi, h2_lo, h2_hi], axis=1), rw3)
        lt_ref[:, rows] = logits.T[0:N_EXP, :]
        return carry

    lax.fori_loop(0, nt, phase_d, 0)


def _mixer_call(x, mod, nvec, w_in, w_out, lvec, wg, rwt, latent, extra=()):
    b, n, _ = x.shape
    const2 = lambda i: (0, 0)
    mod_idx = (lambda i: (i + 1, 0, 0)) if latent else (lambda i: (0, 0, 0))
    in_specs = [
        pl.BlockSpec((None, n, D), lambda i: (i, 0, 0)),
        pl.BlockSpec((None, N_MOD, D), mod_idx),
        pl.BlockSpec((8, D), const2),
        pl.BlockSpec((D, IN_W), const2),
        pl.BlockSpec((D, D), const2),
        pl.BlockSpec((16, LRU_W), const2),
        pl.BlockSpec((2, 2, LRU_W // 2, LRU_W), lambda i: (0, 0, 0, 0)),
        pl.BlockSpec((D, 128), const2),
    ]
    out_shape = [
        jax.ShapeDtypeStruct((b, n, D), _F32),
        jax.ShapeDtypeStruct((b, n, D), _BF),
        jax.ShapeDtypeStruct((b, N_EXP, n), _F32),
    ]
    out_specs = [
        pl.BlockSpec((None, n, D), lambda i: (i, 0, 0)),
        pl.BlockSpec((None, n, D), lambda i: (i, 0, 0)),
        pl.BlockSpec((None, N_EXP, n), lambda i: (i, 0, 0)),
    ]
    if latent:
        in_specs += [
            pl.BlockSpec((None, 2, LRU_W), lambda i: (i, 0, 0)),
            pl.BlockSpec((None, 2, RET_H, DK, DK), lambda i: (i, 0, 0, 0, 0)),
            pl.BlockSpec((n, DK), const2),
            pl.BlockSpec((n, DK), const2),
        ]
    else:
        out_shape += [
            jax.ShapeDtypeStruct((b, 1, 2, LRU_W), _F32),
            jax.ShapeDtypeStruct((b, 1, 2, RET_H, DK, DK), _F32),
        ]
        out_specs += [
            pl.BlockSpec((None, None, 2, LRU_W), lambda i: (i, 0, 0, 0)),
            pl.BlockSpec((None, None, 2, RET_H, DK, DK), lambda i: (i, 0, 0, 0, 0, 0)),
        ]
    f32s = lambda shape: pltpu.VMEM(shape, _F32)
    bfs = lambda shape: pltpu.VMEM(shape, _BF)
    scratch = [
        f32s((n + 2 * SUBLANES, LRU_W)),
        f32s((n, LRU_W)), f32s((n, LRU_W)),
        f32s((n, LRU_W)), f32s((n, LRU_W)),
        f32s((n, LRU_W)), f32s((n, LRU_W)),
        bfs((n, RET_W)), bfs((n, RET_W)),
        bfs((n, RET_W)), bfs((n, RET_W)), bfs((n, RET_W)),
        bfs((n, D)),
    ]
    return pl.pallas_call(
        functools.partial(_mixer_kernel, n=n, latent=latent),
        out_shape=out_shape,
        grid=(b,),
        in_specs=in_specs,
        out_specs=out_specs,
        scratch_shapes=scratch,
        compiler_params=pltpu.CompilerParams(
            dimension_semantics=("arbitrary",), vmem_limit_bytes=VMEM_LIMIT),
        name="mixer_latent" if latent else "mixer_context",
    )(x, mod, nvec, w_in, w_out, lvec, wg, rwt, *extra)


def _select(l3, cap):
    bsz, _, n = l3.shape
    rows = bsz * N_EXP
    m = jnp.max(l3, axis=1, keepdims=True)
    e = jnp.exp(l3 - m)
    p = (e / jnp.sum(e, axis=1, keepdims=True)).reshape(rows, n)
    bits = pltpu.bitcast(p, jnp.int32)
    capf = float(cap)

    def count(mask):
        return jnp.sum(jnp.where(mask, 1.0, 0.0), axis=-1, keepdims=True)

    def val_body(i, t):
        cand = t | (jnp.int32(1) << (30 - i))
        return jnp.where(count(bits >= cand) >= capf, cand, t)

    thr = lax.fori_loop(0, 31, val_body, jnp.zeros((rows, 1), jnp.int32))
    gt = bits > thr
    eq = bits == thr
    need = capf - count(gt)
    idx = lax.broadcasted_iota(jnp.int32, (rows, n), 1)
    nbits = int(math.log2(n))

    def idx_body(i, j):
        cand = j | (jnp.int32(1) << (nbits - 1 - i))
        return jnp.where(count(eq & (idx < cand)) < need, cand, j)

    jlast = lax.fori_loop(0, nbits, idx_body, jnp.zeros((rows, 1), jnp.int32))
    sel = gt | (eq & (idx <= jlast))
    before = (lax.broadcasted_iota(jnp.int32, (n, n), 0)
              < lax.broadcasted_iota(jnp.int32, (n, n), 1))
    pos = _dot(jnp.where(sel, 1.0, 0.0).astype(_BF), jnp.where(before, 1.0, 0.0).astype(_BF))
    return jnp.where(sel, pos, -1.0), jnp.where(sel, p, 0.0)


def _route_kernel(lp_ref, ls_ref, pp_ref, gp_ref, ps_ref, gs_ref, *, cap_p, cap_s):
    pos, gate = _select(lp_ref[...], cap_p)
    pp_ref[...] = pos
    gp_ref[...] = gate
    pos, gate = _select(ls_ref[...], cap_s)
    ps_ref[...] = pos
    gs_ref[...] = gate


def _route_call(lt_p, lt_s, cap_p, cap_s):
    bp, _, n_p = lt_p.shape
    bs, _, n_s = lt_s.shape
    shapes = [
        jax.ShapeDtypeStruct((bp * N_EXP, n_p), _F32), jax.ShapeDtypeStruct((bp * N_EXP, n_p), _F32),
        jax.ShapeDtypeStruct((bs * N_EXP, n_s), _F32), jax.ShapeDtypeStruct((bs * N_EXP, n_s), _F32),
    ]
    return pl.pallas_call(
        functools.partial(_route_kernel, cap_p=cap_p, cap_s=cap_s),
        out_shape=shapes,
        compiler_params=pltpu.CompilerParams(vmem_limit_bytes=VMEM_LIMIT),
        name="route_select",
    )(lt_p, lt_s)


def _dispatch_kernel(pos_ref, gate_ref, h_ref, xs_ref, gs_ref, *, n, cap):
    slot = lax.broadcasted_iota(jnp.int32, (cap, n), 0).astype(_F32)
    parts = []
    for e in range(N_EXP):
        hit = pos_ref[e:e + 1, :] == slot
        parts.append(jnp.where(hit, 1.0, 0.0).astype(_BF))
        gs_ref[e] = jnp.sum(jnp.where(hit, gate_ref[e:e + 1, :], 0.0), axis=-1, keepdims=True)
    onehot = jnp.concatenate(parts, axis=0)
    xs = _dot(onehot, h_ref[...]).astype(_BF)
    for e in range(N_EXP):
        xs_ref[e] = xs[e * cap:(e + 1) * cap, :]


def _dispatch_call(pos, gate, h2, cap):
    b, n, _ = h2.shape
    return pl.pallas_call(
        functools.partial(_dispatch_kernel, n=n, cap=cap),
        out_shape=[
            jax.ShapeDtypeStruct((N_EXP, b * cap, D), _BF),
            jax.ShapeDtypeStruct((N_EXP, b * cap, 1), _F32),
        ],
        grid=(b,),
        in_specs=[
            pl.BlockSpec((None, N_EXP, n), lambda i: (i, 0, 0)),
            pl.BlockSpec((None, N_EXP, n), lambda i: (i, 0, 0)),
            pl.BlockSpec((None, n, D), lambda i: (i, 0, 0)),
        ],
        out_specs=[
            pl.BlockSpec((N_EXP, cap, D), lambda i: (0, i, 0)),
            pl.BlockSpec((N_EXP, cap, 1), lambda i: (0, i, 0)),
        ],
        compiler_params=pltpu.CompilerParams(
            dimension_semantics=("arbitrary",), vmem_limit_bytes=VMEM_LIMIT),
        name="dispatch",
    )(pos, gate, h2)


def _expert_kernel(xp_ref, xs_ref, gp_ref, gs_ref, wg_ref, wu_ref, wd_ref, y_ref, xcat, acc,
                   *, sp, nf, tf):
    f = pl.program_id(1)

    @pl.when(f == 0)
    def _():
        xcat[0:sp, :] = xp_ref[...]
        xcat[sp:, :] = xs_ref[...]
        acc[...] = jnp.zeros_like(acc)

    x = xcat[...]
    for c in range(tf // FF_CHUNK):
        cs = slice(c * FF_CHUNK, (c + 1) * FF_CHUNK)
        hg = _dot(x, wg_ref[:, cs].astype(_BF))
        hu = _dot(x, wu_ref[:, cs].astype(_BF))
        hid = (_silu(hg) * hu).astype(_BF)
        acc[...] += _dot(hid, wd_ref[cs, :].astype(_BF))

    @pl.when(f == nf - 1)
    def _():
        y_ref[0:sp, :] = (acc[0:sp, :] * gp_ref[...]).astype(_BF)
        y_ref[sp:, :] = (acc[sp:, :] * gs_ref[...]).astype(_BF)


def _expert_call(xs_p, xs_s, g_p, g_s, w_gate, w_up, w_down):
    tf = 1024
    sp = xs_p.shape[1]
    ss = xs_s.shape[1]
    nf = FF // tf
    return pl.pallas_call(
        functools.partial(_expert_kernel, sp=sp, nf=nf, tf=tf),
        out_shape=jax.ShapeDtypeStruct((N_EXP, sp + ss, D), _BF),
        grid=(N_EXP, nf),
        in_specs=[
            pl.BlockSpec((None, sp, D), lambda e, f: (e, 0, 0)),
            pl.BlockSpec((None, ss, D), lambda e, f: (e, 0, 0)),
            pl.BlockSpec((None, sp, 1), lambda e, f: (e, 0, 0)),
            pl.BlockSpec((None, ss, 1), lambda e, f: (e, 0, 0)),
            pl.BlockSpec((None, D, tf), lambda e, f: (e, 0, f)),
            pl.BlockSpec((None, D, tf), lambda e, f: (e, 0, f)),
            pl.BlockSpec((None, tf, D), lambda e, f: (e, f, 0)),
        ],
        out_specs=pl.BlockSpec((None, sp + ss, D), lambda e, f: (e, 0, 0)),
        scratch_shapes=[pltpu.VMEM((sp + ss, D), _BF), pltpu.VMEM((sp + ss, D), _F32)],
        compiler_params=pltpu.CompilerParams(
            dimension_semantics=("arbitrary", "arbitrary"), vmem_limit_bytes=VMEM_LIMIT),
        name="expert_ffn",
    )(xs_p, xs_s, g_p, g_s, w_gate, w_up, w_down)


def _combine_kernel(post_ref, ye_ref, x1_ref, mod_ref, nv_ref, y_ref, *, n, cap):
    width = N_EXP * cap
    lane = lax.broadcasted_iota(jnp.int32, (N_EXP, width), 1)
    expand = jnp.where(lane // cap == lax.broadcasted_iota(jnp.int32, (N_EXP, width), 0),
                       1.0, 0.0).astype(_BF)
    slot = (lax.broadcasted_iota(jnp.int32, (ROW_TILE, width), 1) % cap).astype(_F32)
    ye = ye_ref[...].reshape(width, D)
    g2 = mod_ref[5:6, :]

    def body(r, carry):
        rows = pl.ds(pl.multiple_of(r * ROW_TILE, ROW_TILE), ROW_TILE)
        pos_e = _dot(post_ref[rows, :], expand)
        onehot = jnp.where(pos_e == slot, 1.0, 0.0).astype(_BF)
        f = _dot(onehot, ye)
        y_ref[rows, :] = x1_ref[rows, :] + g2 * _rms(f, nv_ref[3:4, :])
        return carry

    lax.fori_loop(0, n // ROW_TILE, body, 0)


def _combine_call(post, ye, x1, mod, nvec, cap, slot_off, latent):
    b, n, _ = x1.shape
    blk_off = slot_off // cap
    mod_idx = (lambda i: (i + 1, 0, 0)) if latent else (lambda i: (0, 0, 0))
    return pl.pallas_call(
        functools.partial(_combine_kernel, n=n, cap=cap),
        out_shape=jax.ShapeDtypeStruct((b, n, D), _F32),
        grid=(b,),
        in_specs=[
            pl.BlockSpec((None, n, N_EXP), lambda i: (i, 0, 0)),
            pl.BlockSpec((N_EXP, cap, D), lambda i: (0, i + blk_off, 0)),
            pl.BlockSpec((None, n, D), lambda i: (i, 0, 0)),
            pl.BlockSpec((None, N_MOD, D), mod_idx),
            pl.BlockSpec((8, D), lambda i: (0, 0)),
        ],
        out_specs=pl.BlockSpec((None, n, D), lambda i: (i, 0, 0)),
        compiler_params=pltpu.CompilerParams(
            dimension_semantics=("arbitrary",), vmem_limit_bytes=VMEM_LIMIT),
        name="combine_latent" if latent else "combine_context",
    )(post, ye, x1, mod, nvec)


def _block_diag_gates(wa, wi):
    per_half = LRU_HEADS // 2
    eye = jnp.eye(per_half, dtype=wa.dtype)

    def bd(w):
        w = w.reshape(2, 2, per_half, LRU_HD, LRU_HD)
        full = jnp.einsum('dghij,hk->dghikj', w, eye)
        return full.reshape(2, 2, per_half * LRU_HD, per_half * LRU_HD)

    return jnp.concatenate([bd(wa), bd(wi)], axis=-1).astype(_BF)


def _rope_tables(n):
    t = np.arange(n)
    nf = DK // 4
    freqs = ROPE_BASE ** (-np.arange(nf, dtype=np.float32) / nf)
    ang = np.concatenate([(t // GRID_W)[:, None].astype(np.float32) * freqs,
                          (t % GRID_W)[:, None].astype(np.float32) * freqs], axis=-1).astype(np.float32)
    cos = np.cos(ang)
    sin = np.sin(ang)
    return (jnp.asarray(np.concatenate([cos, cos], axis=-1), _F32),
            jnp.asarray(np.concatenate([-sin, sin], axis=-1), _F32))


def kernel(x_prompt, x_sample, c, state_lru, state_ret, c_ctx, ada_w, ada_b, norm_mix_pre, norm_mix_post, norm_ffn_pre, norm_ffn_post, w_in, conv_w, conv_b, lru_wa, lru_ba, lru_wi, lru_bi, lru_lambda, w_out, router_w, exp_w_gate, exp_w_up, exp_w_down):
    bp, n_p, _ = x_prompt.shape
    bs, n_s, _ = x_sample.shape
    cap_p = 2 * n_p // N_EXP
    cap_s = 2 * n_s // N_EXP
    l = 0

    c_all = jnp.concatenate([c_ctx[None, :], c, jnp.zeros((8 - 1 - bs, D), _F32)], axis=0)
    mod = _ada_call(c_all, ada_w[l], ada_b[l][None, :]).reshape(8, N_MOD, D)

    nvec = jnp.concatenate([norm_mix_pre[l][None], norm_mix_post[l][None], norm_ffn_pre[l][None],
                            norm_ffn_post[l][None], jnp.zeros((4, D), _F32)], axis=0)
    lvec = jnp.concatenate([
        conv_w[l], conv_b[l][None],
        lru_ba[l, 0][None], lru_bi[l, 0][None], lru_lambda[l, 0][None],
        lru_ba[l, 1][None], lru_bi[l, 1][None], lru_lambda[l, 1][None],
        jnp.zeros((5, LRU_W), _F32)], axis=0)
    wg = _block_diag_gates(lru_wa[l], lru_wi[l])
    w_in_b = w_in[l].astype(_BF)
    w_out_b = w_out[l].astype(_BF)
    rwt = jnp.pad(router_w[l], ((0, 0), (0, 128 - N_EXP)))
    cos2, sin2 = _rope_tables(n_s)

    x1_p, h2_p, lt_p, st_lru, st_ret = _mixer_call(
        x_prompt, mod, nvec, w_in_b, w_out_b, lvec, wg, rwt, latent=False)
    x1_s, h2_s, lt_s = _mixer_call(
        x_sample, mod, nvec, w_in_b, w_out_b, lvec, wg, rwt, latent=True,
        extra=(state_lru[:, l], state_ret[:, l], cos2, sin2))

    pos_p, gate_p, pos_s, gate_s = _route_call(lt_p, lt_s, cap_p, cap_s)
    pos_p = pos_p.reshape(bp, N_EXP, n_p)
    gate_p = gate_p.reshape(bp, N_EXP, n_p)
    pos_s = pos_s.reshape(bs, N_EXP, n_s)
    gate_s = gate_s.reshape(bs, N_EXP, n_s)

    xs_p, gsl_p = _dispatch_call(pos_p, gate_p, h2_p, cap_p)
    xs_s, gsl_s = _dispatch_call(pos_s, gate_s, h2_s, cap_s)
    ye = _expert_call(xs_p, xs_s, gsl_p, gsl_s, exp_w_gate[l], exp_w_up[l], exp_w_down[l])

    post_p = jnp.swapaxes(pos_p, 1, 2).astype(_BF)
    post_s = jnp.swapaxes(pos_s, 1, 2).astype(_BF)
    y_p = _combine_call(post_p, ye, x1_p, mod, nvec, cap_p, 0, latent=False)
    y_s = _combine_call(post_s, ye, x1_s, mod, nvec, cap_s, bp * cap_p, latent=True)
    return (y_p, y_s, st_lru, st_ret)
```

```python
import functools
import math

import jax
import jax.numpy as jnp
from jax import lax
from jax.experimental import pallas as pl
from jax.experimental.pallas import tpu as pltpu

D = 1024
LRU_W = 512
LRU_HEADS = 8
LRU_HD = 64
LRU_C = 8.0
RET_W = 512
RET_H = 4
DK = 128
N_EXP = 16
FF = 2048
N_MOD = 6
EPS = 1e-6
GRID_W = 64
ROPE_BASE = 10000.0
RET_DECAY_OFFSET_FWD = 5.0
RET_DECAY_OFFSET_BWD = 5.5

ROW_TILE = 256
POST_TILE = 512
FF_CHUNK = 512
COMBINE_TILE = 512
SUBLANES = 8
LANES = 128
VMEM_LIMIT = 60 * 1024 * 1024

DEC_LOG_F, DEC_LOG_B, DEC_G_F, DEC_GN_B, DEC_GN1_F, DEC_ROWS = 0, 1, 2, 3, 4, 5

_BF = jnp.bfloat16
_F32 = jnp.float32


def _sigmoid(x):
    return 0.5 * jnp.tanh(0.5 * x) + 0.5


def _silu(x):
    return x * _sigmoid(x)


def _gelu_tanh(x):
    c = math.sqrt(2.0 / math.pi)
    return 0.5 * x * (1.0 + jnp.tanh(c * (x + 0.044715 * (x * x * x))))


def _rms(x, gain):
    return x * lax.rsqrt(jnp.mean(x * x, axis=-1, keepdims=True) + EPS) * gain


def _dot(a, b):
    return jnp.dot(a, b, preferred_element_type=_F32)


def _dot_nt(a, b):
    return lax.dot_general(a, b, (((1,), (1,)), ((), ())), preferred_element_type=_F32)


def _dot_tn(a, b):
    return lax.dot_general(a, b, (((0,), (0,)), ((), ())), preferred_element_type=_F32)


def _ada_kernel(c_ref, w_ref, b_ref, o_ref):
    o_ref[...] = _dot(_silu(c_ref[...]), w_ref[...]) + b_ref[...]


def _ada_call(c_all, ada_w, ada_b):
    tn = 1536
    n_out = ada_w.shape[1]
    return pl.pallas_call(
        _ada_kernel,
        out_shape=jax.ShapeDtypeStruct((8, n_out), _F32),
        grid=(n_out // tn,),
        in_specs=[
            pl.BlockSpec((8, D), lambda j: (0, 0)),
            pl.BlockSpec((D, tn), lambda j: (0, j)),
            pl.BlockSpec((1, tn), lambda j: (0, j)),
        ],
        out_specs=pl.BlockSpec((8, tn), lambda j: (0, j)),
        compiler_params=pltpu.CompilerParams(
            dimension_semantics=("arbitrary",), vmem_limit_bytes=VMEM_LIMIT),
        name="ada_mod",
    )(c_all, ada_w, ada_b)


def _mixer_kernel(*refs, n, latent):
    if latent:
        (x_ref, mod_ref, nv_ref, wl_ref, wqk_ref, wvg_ref, lv_ref, wg_ref, dec_ref,
         h0_ref, s0_ref, cos_ref, sin_ref,
         mix_ref,
         xlp, gy, sg, a_f, a_b, h_f, h_b) = refs
    else:
        (x_ref, mod_ref, nv_ref, wl_ref, wqk_ref, wvg_ref, lv_ref, wg_ref, dec_ref,
         mix_ref, stl_ref, str_ref,
         xlp, gy, sg, a_f, a_b, h_f, h_b) = refs

    nt = n // ROW_TILE
    shift = mod_ref[0:1, :]
    scale = nv_ref[0:1, :] * (1.0 + mod_ref[1:2, :])

    def normed(rows):
        x = x_ref[rows, :]
        return x * lax.rsqrt(jnp.mean(x * x, axis=-1, keepdims=True) + EPS) * scale + shift

    xlp[0:SUBLANES, :] = jnp.zeros((SUBLANES, LRU_W), _F32)
    xlp[n + SUBLANES:n + 2 * SUBLANES, :] = jnp.zeros((SUBLANES, LRU_W), _F32)

    def phase_a1(r, carry):
        r0 = pl.multiple_of(r * ROW_TILE, ROW_TILE)
        rows = pl.ds(r0, ROW_TILE)
        p = _dot(normed(rows), wl_ref[...])
        xlp[pl.ds(r0 + SUBLANES, ROW_TILE), :] = p[:, 0:LRU_W]
        gy[rows, :] = _gelu_tanh(p[:, LRU_W:])
        return carry

    lax.fori_loop(0, nt, phase_a1, 0)

    half = LRU_W // 2

    def softplus_neg(lam):
        z = -lam
        return jnp.maximum(z, 0.0) + jnp.log1p(jnp.exp(-jnp.abs(z)))

    sp = (softplus_neg(lv_ref[7:8, :]), softplus_neg(lv_ref[10:11, :]))

    def phase_b(r, carry):
        r0 = pl.multiple_of(r * ROW_TILE, ROW_TILE)
        rows = pl.ds(r0, ROW_TILE)
        ext = xlp[pl.ds(r0, ROW_TILE + 2 * SUBLANES), :]
        xc = lv_ref[4:5, :]
        for tap in range(4):
            o = SUBLANES - 2 + tap
            xc = xc + ext[o:o + ROW_TILE, :] * lv_ref[tap:tap + 1, :]
        xh = 0.5 * xc
        for d, (a_ref, u_ref) in enumerate(((a_f, h_f), (a_b, h_b))):
            bah = 0.5 * lv_ref[5 + 3 * d:6 + 3 * d, :]
            bih = 0.5 * lv_ref[6 + 3 * d:7 + 3 * d, :]
            ch = (-0.5 * LRU_C) * sp[d]
            for hh in range(2):
                cs = slice(hh * half, (hh + 1) * half)
                pre = _dot(xc[:, cs], wg_ref[d, hh])
                t_r = jnp.tanh(pre[:, 0:half] + bah[:, cs])
                t_i = jnp.tanh(pre[:, half:] + bih[:, cs])
                log_a = t_r * ch[:, cs] + ch[:, cs]
                a = jnp.exp(log_a)
                a_ref[rows, cs] = a
                root = jnp.sqrt(-jnp.tanh(log_a) * (a * a + 1.0))
                u_ref[rows, cs] = root * (t_i * xh[:, cs] + xh[:, cs])
        return carry

    lax.fori_loop(0, nt, phase_b, 0)

    row_id = lax.broadcasted_iota(jnp.int32, (SUBLANES, LRU_W), 0)

    def scan_group(a, b, shift_of, mask_of):
        for s in (1, 2, 4):
            m = mask_of(s)
            a_s = jnp.where(m, pltpu.roll(a, shift_of(s), axis=0), 1.0)
            b_s = jnp.where(m, pltpu.roll(b, shift_of(s), axis=0), 0.0)
            b = a * b_s + b
            a = a * a_s
        return a, b

    if latent:
        init_f = h0_ref[0:1, :]
        init_b = h0_ref[1:2, :]
    else:
        init_f = jnp.zeros((1, LRU_W), _F32)
        init_b = jnp.zeros((1, LRU_W), _F32)
    ng = n // SUBLANES

    def fwd_body(g, carry):
        rows = pl.ds(pl.multiple_of(g * SUBLANES, SUBLANES), SUBLANES)
        a, b = scan_group(a_f[rows, :], h_f[rows, :], lambda s: s, lambda s: row_id >= s)
        h = a * carry + b
        h_f[rows, :] = h
        return h[SUBLANES - 1:SUBLANES, :]

    last_f = lax.fori_loop(0, ng, fwd_body, init_f, unroll=4)

    def bwd_body(i, carry):
        g = ng - 1 - i
        rows = pl.ds(pl.multiple_of(g * SUBLANES, SUBLANES), SUBLANES)
        a, b = scan_group(a_b[rows, :], h_b[rows, :], lambda s: SUBLANES - s,
                          lambda s: row_id < SUBLANES - s)
        h = a * carry + b
        h_b[rows, :] = h
        return h[0:1, :]

    first_b = lax.fori_loop(0, ng, bwd_body, init_b, unroll=4)

    if not latent:
        stl_ref[0:1, :] = last_f
        stl_ref[1:2, :] = first_b

    def lru_out(r, carry):
        rows = pl.ds(pl.multiple_of(r * ROW_TILE, ROW_TILE), ROW_TILE)
        mix_ref[rows, 0:LRU_W] = (h_f[rows, :] + h_b[rows, :]) * gy[rows, :]
        return carry

    lax.fori_loop(0, nt, lru_out, 0)

    q_f, q_b, k_f, k_b, v_s = a_f, a_b, h_f, h_b, gy

    def phase_a2(r, carry):
        r0 = pl.multiple_of(r * ROW_TILE, ROW_TILE)
        rows = pl.ds(r0, ROW_TILE)
        h = normed(rows)
        pqk = _dot(h, wqk_ref[...])
        pvg = _dot(h, wvg_ref[...])
        sg[rows, :] = _silu(pvg[:, RET_W:])
        tpos = (r0 + lax.broadcasted_iota(jnp.int32, (ROW_TILE, DK), 0)).astype(_F32)
        if latent:
            cos2 = cos_ref[rows, :]
            sin2 = sin_ref[rows, :]
        for hd in range(RET_H):
            cols = slice(hd * DK, (hd + 1) * DK)
            q = pqk[:, cols] * (DK ** -0.5)
            k = pqk[:, RET_W + hd * DK:RET_W + (hd + 1) * DK]
            if latent:
                q = q * cos2 + pltpu.roll(q, DK // 2, axis=1) * sin2
                k = k * cos2 + pltpu.roll(k, DK // 2, axis=1) * sin2
            lgf = dec_ref[hd * DEC_ROWS + DEC_LOG_F:hd * DEC_ROWS + DEC_LOG_F + 1, :]
            lgb = dec_ref[hd * DEC_ROWS + DEC_LOG_B:hd * DEC_ROWS + DEC_LOG_B + 1, :]
            q_f[rows, cols] = q * jnp.exp(tpos * lgf)
            k_f[rows, cols] = k * jnp.exp(tpos * (-lgf))
            q_b[rows, cols] = q * jnp.exp(tpos * (-lgb))
            k_b[rows, cols] = k * jnp.exp(tpos * lgb)
        v_s[rows, :] = pvg[:, 0:RET_W]
        return carry

    lax.fori_loop(0, nt, phase_a2, 0)

    lower = (lax.broadcasted_iota(jnp.int32, (ROW_TILE, ROW_TILE), 0)
             >= lax.broadcasted_iota(jnp.int32, (ROW_TILE, ROW_TILE), 1))
    blocks = [slice(r * ROW_TILE, (r + 1) * ROW_TILE) for r in range(nt)]

    for hd in range(RET_H):
        cols = slice(hd * DK, (hd + 1) * DK)
        dec = lambda row, hd=hd: dec_ref[hd * DEC_ROWS + row:hd * DEC_ROWS + row + 1, :]
        kv_f = [_dot_tn(k_f[rows, cols], v_s[rows, cols]) if (r < nt - 1 or not latent) else 0.0
                for r, rows in enumerate(blocks)]
        kv_b = [_dot_tn(k_b[rows, cols], v_s[rows, cols]) if (r > 0 or not latent) else 0.0
                for r, rows in enumerate(blocks)]
        if latent:
            run_f = s0_ref[0, hd] * dec(DEC_G_F)
            run_b = s0_ref[1, hd] * dec(DEC_GN_B)
        else:
            run_f = run_b = None
        before = []
        for r in range(nt):
            before.append(run_f)
            run_f = kv_f[r] if run_f is None else run_f + kv_f[r]
        after = [None] * nt
        for r in reversed(range(nt)):
            after[r] = run_b
            run_b = kv_b[r] if run_b is None else run_b + kv_b[r]

        for r, rows in enumerate(blocks):
            qf = q_f[rows, cols]
            qb = q_b[rows, cols]
            s = jnp.where(lower, _dot_nt(qf, k_f[rows, cols]), _dot_nt(qb, k_b[rows, cols]))
            o = _dot(s, v_s[rows, cols])
            if before[r] is not None:
                o = o + _dot(qf, before[r])
            if after[r] is not None:
                o = o + _dot(qb, after[r])
            o = o * lax.rsqrt(jnp.mean(o * o, axis=-1, keepdims=True) + EPS)
            mix_ref[rows, LRU_W + cols.start:LRU_W + cols.stop] = o * sg[rows, cols]
        if not latent:
            str_ref[0, hd] = run_f * dec(DEC_GN1_F)
            str_ref[1, hd] = run_b


def _mixer_call(x, mod, nvec, w_in, lvec, wg, dec, latent, extra=()):
    b, n, _ = x.shape
    const2 = lambda i: (0, 0)
    mod_idx = (lambda i: (i + 1, 0, 0)) if latent else (lambda i: (0, 0, 0))
    in_specs = [
        pl.BlockSpec((None, n, D), lambda i: (i, 0, 0)),
        pl.BlockSpec((None, N_MOD, D), mod_idx),
        pl.BlockSpec((8, D), const2),
        pl.BlockSpec((D, D), lambda i: (0, 0)),
        pl.BlockSpec((D, D), lambda i: (0, 1)),
        pl.BlockSpec((D, D), lambda i: (0, 2)),
        pl.BlockSpec((16, LRU_W), const2),
        pl.BlockSpec((2, 2, LRU_W // 2, LRU_W), lambda i: (0, 0, 0, 0)),
        pl.BlockSpec((RET_H * DEC_ROWS + 4, LANES), const2),
    ]
    out_shape = [jax.ShapeDtypeStruct((b, n, D), _F32)]
    out_specs = [pl.BlockSpec((None, n, D), lambda i: (i, 0, 0))]
    if latent:
        in_specs += [
            pl.BlockSpec((None, 2, LRU_W), lambda i: (i, 0, 0)),
            pl.BlockSpec((None, 2, RET_H, DK, DK), lambda i: (i, 0, 0, 0, 0)),
            pl.BlockSpec((n, DK), const2),
            pl.BlockSpec((n, DK), const2),
        ]
    else:
        out_shape += [
            jax.ShapeDtypeStruct((b, 1, 2, LRU_W), _F32),
            jax.ShapeDtypeStruct((b, 1, 2, RET_H, DK, DK), _F32),
        ]
        out_specs += [
            pl.BlockSpec((None, None, 2, LRU_W), lambda i: (i, 0, 0, 0)),
            pl.BlockSpec((None, None, 2, RET_H, DK, DK), lambda i: (i, 0, 0, 0, 0, 0)),
        ]
    f32s = lambda shape: pltpu.VMEM(shape, _F32)
    scratch = [
        f32s((n + 2 * SUBLANES, LRU_W)),
        f32s((n, LRU_W)), f32s((n, LRU_W)),
        f32s((n, LRU_W)), f32s((n, LRU_W)),
        f32s((n, LRU_W)), f32s((n, LRU_W)),
    ]
    return pl.pallas_call(
        functools.partial(_mixer_kernel, n=n, latent=latent),
        out_shape=out_shape,
        grid=(b,),
        in_specs=in_specs,
        out_specs=out_specs,
        scratch_shapes=scratch,
        compiler_params=pltpu.CompilerParams(
            dimension_semantics=("arbitrary",), vmem_limit_bytes=VMEM_LIMIT),
        name="mixer_latent" if latent else "mixer_context",
    )(x, mod, nvec, w_in, w_in, w_in, lvec, wg, dec, *extra)


def _post_kernel(x_ref, mix_ref, mod_ref, nv_ref, wout_ref, rw_ref, x1_ref, h2_ref, lt_ref, *, g, nb):
    gain1 = mod_ref[2:3, :] * nv_ref[1:2, :]
    gain2 = nv_ref[2:3, :] * (1.0 + mod_ref[4:5, :])
    sh2 = mod_ref[3:4, :]
    sub = min(nb, ROW_TILE)
    for j in range(g):
        for t in range(nb // sub):
            rows = slice(t * sub, (t + 1) * sub)
            mix = _dot(mix_ref[j, rows, :], wout_ref[...])
            x1 = x_ref[j, rows, :] + mix * lax.rsqrt(jnp.mean(mix * mix, axis=-1, keepdims=True) + EPS) * gain1
            x1_ref[j, rows, :] = x1
            h2 = x1 * lax.rsqrt(jnp.mean(x1 * x1, axis=-1, keepdims=True) + EPS) * gain2 + sh2
            h2_ref[j, rows, :] = h2.astype(_BF)
            lt_ref[j, :, rows] = _dot(h2, rw_ref[...]).T[0:N_EXP, :]


def _post_call(x, mix, mod, nvec, w_out, rw, latent):
    b, n, _ = x.shape
    nb = min(n, POST_TILE)
    g = POST_TILE // nb
    mod_idx = (lambda i, j: (i + 1, 0, 0)) if latent else (lambda i, j: (0, 0, 0))
    tok = pl.BlockSpec((g, nb, D), lambda i, j: (i, j, 0))
    return pl.pallas_call(
        functools.partial(_post_kernel, g=g, nb=nb),
        out_shape=[
            jax.ShapeDtypeStruct((b, n, D), _F32),
            jax.ShapeDtypeStruct((b, n, D), _BF),
            jax.ShapeDtypeStruct((b, N_EXP, n), _F32),
        ],
        grid=(b // g, n // nb),
        in_specs=[
            tok, tok,
            pl.BlockSpec((None, N_MOD, D), mod_idx),
            pl.BlockSpec((8, D), lambda i, j: (0, 0)),
            pl.BlockSpec((D, D), lambda i, j: (0, 0)),
            pl.BlockSpec((D, LANES), lambda i, j: (0, 0)),
        ],
        out_specs=[tok, tok, pl.BlockSpec((g, N_EXP, nb), lambda i, j: (i, 0, j))],
        compiler_params=pltpu.CompilerParams(
            dimension_semantics=("arbitrary", "arbitrary"), vmem_limit_bytes=VMEM_LIMIT),
        name="post_latent" if latent else "post_context",
    )(x, mix, mod, nvec, w_out, rw)


def _select(l3, cap):
    bsz, _, n = l3.shape
    rows = bsz * N_EXP
    m = jnp.max(l3, axis=1, keepdims=True)
    e = jnp.exp(l3 - m)
    p = (e / jnp.sum(e, axis=1, keepdims=True)).reshape(rows, n)
    bits = pltpu.bitcast(p, jnp.int32)
    capf = float(cap)

    def count(mask):
        return jnp.sum(jnp.where(mask, 1.0, 0.0), axis=-1, keepdims=True)

    def val_body(i, t):
        cand = t | (jnp.int32(1) << (30 - i))
        return jnp.where(count(bits >= cand) >= capf, cand, t)

    thr = lax.fori_loop(0, 31, val_body, jnp.zeros((rows, 1), jnp.int32))
    gt = bits > thr
    eq = bits == thr
    need = capf - count(gt)
    idx = lax.broadcasted_iota(jnp.int32, (rows, n), 1)
    nbits = int(math.log2(n))

    def idx_body(i, j):
        cand = j | (jnp.int32(1) << (nbits - 1 - i))
        return jnp.where(count(eq & (idx < cand)) < need, cand, j)

    jlast = lax.fori_loop(0, nbits, idx_body, jnp.zeros((rows, 1), jnp.int32))
    sel = gt | (eq & (idx <= jlast))
    before = (lax.broadcasted_iota(jnp.int32, (n, n), 0)
              < lax.broadcasted_iota(jnp.int32, (n, n), 1))
    pos = _dot(jnp.where(sel, 1.0, 0.0).astype(_BF), jnp.where(before, 1.0, 0.0).astype(_BF))
    return jnp.where(sel, pos, -1.0), jnp.where(sel, p, 0.0)


def _route_kernel(lp_ref, ls_ref, pp_ref, gp_ref, ps_ref, gs_ref, *, cap_p, cap_s):
    pos, gate = _select(lp_ref[...], cap_p)
    pp_ref[...] = pos
    gp_ref[...] = gate
    pos, gate = _select(ls_ref[...], cap_s)
    ps_ref[...] = pos
    gs_ref[...] = gate


def _route_call(lt_p, lt_s, cap_p, cap_s):
    bp, _, n_p = lt_p.shape
    bs, _, n_s = lt_s.shape
    shapes = [
        jax.ShapeDtypeStruct((bp * N_EXP, n_p), _F32), jax.ShapeDtypeStruct((bp * N_EXP, n_p), _F32),
        jax.ShapeDtypeStruct((bs * N_EXP, n_s), _F32), jax.ShapeDtypeStruct((bs * N_EXP, n_s), _F32),
    ]
    return pl.pallas_call(
        functools.partial(_route_kernel, cap_p=cap_p, cap_s=cap_s),
        out_shape=shapes,
        compiler_params=pltpu.CompilerParams(vmem_limit_bytes=VMEM_LIMIT),
        name="route_select",
    )(lt_p, lt_s)


def _dispatch_kernel(pos_ref, gate_ref, h_ref, xs_ref, gs_ref, *, n, cap):
    slot = lax.broadcasted_iota(jnp.int32, (cap, n), 0).astype(_F32)
    parts = []
    for e in range(N_EXP):
        hit = pos_ref[e:e + 1, :] == slot
        parts.append(jnp.where(hit, 1.0, 0.0).astype(_BF))
        gs_ref[e] = jnp.sum(jnp.where(hit, gate_ref[e:e + 1, :], 0.0), axis=-1, keepdims=True)
    onehot = jnp.concatenate(parts, axis=0)
    xs = _dot(onehot, h_ref[...]).astype(_BF)
    for e in range(N_EXP):
        xs_ref[e] = xs[e * cap:(e + 1) * cap, :]


def _dispatch_call(pos, gate, h2, cap):
    b, n, _ = h2.shape
    return pl.pallas_call(
        functools.partial(_dispatch_kernel, n=n, cap=cap),
        out_shape=[
            jax.ShapeDtypeStruct((N_EXP, b * cap, D), _BF),
            jax.ShapeDtypeStruct((N_EXP, b * cap, 1), _F32),
        ],
        grid=(b,),
        in_specs=[
            pl.BlockSpec((None, N_EXP, n), lambda i: (i, 0, 0)),
            pl.BlockSpec((None, N_EXP, n), lambda i: (i, 0, 0)),
            pl.BlockSpec((None, n, D), lambda i: (i, 0, 0)),
        ],
        out_specs=[
            pl.BlockSpec((N_EXP, cap, D), lambda i: (0, i, 0)),
            pl.BlockSpec((N_EXP, cap, 1), lambda i: (0, i, 0)),
        ],
        compiler_params=pltpu.CompilerParams(
            dimension_semantics=("arbitrary",), vmem_limit_bytes=VMEM_LIMIT),
        name="dispatch",
    )(pos, gate, h2)


def _expert_kernel(xp_ref, xs_ref, gp_ref, gs_ref, wg_ref, wu_ref, wd_ref, y_ref, xcat, acc,
                   *, sp, nf, tf):
    f = pl.program_id(1)

    @pl.when(f == 0)
    def _():
        xcat[0:sp, :] = xp_ref[...]
        xcat[sp:, :] = xs_ref[...]
        acc[...] = jnp.zeros_like(acc)

    x = xcat[...]
    for c in range(tf // FF_CHUNK):
        cs = slice(c * FF_CHUNK, (c + 1) * FF_CHUNK)
        hg = _dot(x, wg_ref[:, cs].astype(_BF))
        hu = _dot(x, wu_ref[:, cs].astype(_BF))
        hid = (_silu(hg) * hu).astype(_BF)
        acc[...] += _dot(hid, wd_ref[cs, :].astype(_BF))

    @pl.when(f == nf - 1)
    def _():
        y_ref[0:sp, :] = (acc[0:sp, :] * gp_ref[...]).astype(_BF)
        y_ref[sp:, :] = (acc[sp:, :] * gs_ref[...]).astype(_BF)


def _expert_call(xs_p, xs_s, g_p, g_s, w_gate, w_up, w_down):
    tf = 1024
    sp = xs_p.shape[1]
    ss = xs_s.shape[1]
    nf = FF // tf
    return pl.pallas_call(
        functools.partial(_expert_kernel, sp=sp, nf=nf, tf=tf),
        out_shape=jax.ShapeDtypeStruct((N_EXP, sp + ss, D), _BF),
        grid=(N_EXP, nf),
        in_specs=[
            pl.BlockSpec((None, sp, D), lambda e, f: (e, 0, 0)),
            pl.BlockSpec((None, ss, D), lambda e, f: (e, 0, 0)),
            pl.BlockSpec((None, sp, 1), lambda e, f: (e, 0, 0)),
            pl.BlockSpec((None, ss, 1), lambda e, f: (e, 0, 0)),
            pl.BlockSpec((None, D, tf), lambda e, f: (e, 0, f)),
            pl.BlockSpec((None, D, tf), lambda e, f: (e, 0, f)),
            pl.BlockSpec((None, tf, D), lambda e, f: (e, f, 0)),
        ],
        out_specs=pl.BlockSpec((None, sp + ss, D), lambda e, f: (e, 0, 0)),
        scratch_shapes=[pltpu.VMEM((sp + ss, D), _BF), pltpu.VMEM((sp + ss, D), _F32)],
        compiler_params=pltpu.CompilerParams(
            dimension_semantics=("arbitrary", "arbitrary"), vmem_limit_bytes=VMEM_LIMIT),
        name="expert_ffn",
    )(xs_p, xs_s, g_p, g_s, w_gate, w_up, w_down)


def _combine_kernel(post_ref, ye_ref, x1_ref, mod_ref, nv_ref, y_ref, *, n, cap):
    width = N_EXP * cap
    lane = lax.broadcasted_iota(jnp.int32, (N_EXP, width), 1)
    expand = jnp.where(lane // cap == lax.broadcasted_iota(jnp.int32, (N_EXP, width), 0),
                       1.0, 0.0).astype(_BF)
    tile = min(n, COMBINE_TILE)
    slot = (lax.broadcasted_iota(jnp.int32, (tile, width), 1) % cap).astype(_F32)
    ye = ye_ref[...].reshape(width, D)
    g2 = mod_ref[5:6, :]

    def body(r, carry):
        rows = pl.ds(pl.multiple_of(r * tile, tile), tile)
        pos_e = _dot(post_ref[rows, :], expand)
        onehot = jnp.where(pos_e == slot, 1.0, 0.0).astype(_BF)
        f = _dot(onehot, ye)
        y_ref[rows, :] = x1_ref[rows, :] + g2 * _rms(f, nv_ref[3:4, :])
        return carry

    lax.fori_loop(0, n // tile, body, 0)


def _combine_call(post, ye, x1, mod, nvec, cap, slot_off, latent):
    b, n, _ = x1.shape
    blk_off = slot_off // cap
    mod_idx = (lambda i: (i + 1, 0, 0)) if latent else (lambda i: (0, 0, 0))
    return pl.pallas_call(
        functools.partial(_combine_kernel, n=n, cap=cap),
        out_shape=jax.ShapeDtypeStruct((b, n, D), _F32),
        grid=(b,),
        in_specs=[
            pl.BlockSpec((None, n, N_EXP), lambda i: (i, 0, 0)),
            pl.BlockSpec((N_EXP, cap, D), lambda i: (0, i + blk_off, 0)),
            pl.BlockSpec((None, n, D), lambda i: (i, 0, 0)),
            pl.BlockSpec((None, N_MOD, D), mod_idx),
            pl.BlockSpec((8, D), lambda i: (0, 0)),
        ],
        out_specs=pl.BlockSpec((None, n, D), lambda i: (i, 0, 0)),
        compiler_params=pltpu.CompilerParams(
            dimension_semantics=("arbitrary",), vmem_limit_bytes=VMEM_LIMIT),
        name="combine_latent" if latent else "combine_context",
    )(post, ye, x1, mod, nvec)


def _block_diag_gates(wa, wi):
    per_half = LRU_HEADS // 2
    eye = jnp.eye(per_half, dtype=wa.dtype)

    def bd(w):
        w = w.reshape(2, 2, per_half, LRU_HD, LRU_HD)
        full = jnp.einsum('dghij,hk->dghikj', w, eye)
        return full.reshape(2, 2, per_half * LRU_HD, per_half * LRU_HD)

    return 0.5 * jnp.concatenate([bd(wa), bd(wi)], axis=-1)


def _rope_tables(n):
    rows = n // GRID_W
    row = jnp.repeat(jnp.arange(rows, dtype=_F32), GRID_W)
    col = jnp.tile(jnp.arange(GRID_W, dtype=_F32), rows)
    nf = DK // 4
    freqs = ROPE_BASE ** (-jnp.arange(nf, dtype=_F32) / nf)
    ang = jnp.concatenate([row[:, None] * freqs, col[:, None] * freqs], axis=-1)
    cos = jnp.cos(ang)
    sin = jnp.sin(ang)
    return jnp.concatenate([cos, cos], axis=-1), jnp.concatenate([-sin, sin], axis=-1)


def _decay_table(n):
    heads = jnp.arange(RET_H, dtype=_F32)
    lgf = jnp.log1p(-jnp.exp2(-(RET_DECAY_OFFSET_FWD + heads)))
    lgb = jnp.log1p(-jnp.exp2(-(RET_DECAY_OFFSET_BWD + heads)))
    tab = jnp.stack([lgf, lgb, jnp.exp(lgf), jnp.exp(n * lgb), jnp.exp((n - 1) * lgf)], axis=1)
    tab = jnp.concatenate([tab.reshape(RET_H * DEC_ROWS), jnp.zeros((4,), _F32)])
    return jnp.broadcast_to(tab[:, None], (RET_H * DEC_ROWS + 4, LANES))


def kernel(x_prompt, x_sample, c, state_lru, state_ret, c_ctx, ada_w, ada_b, norm_mix_pre, norm_mix_post, norm_ffn_pre, norm_ffn_post, w_in, conv_w, conv_b, lru_wa, lru_ba, lru_wi, lru_bi, lru_lambda, w_out, router_w, exp_w_gate, exp_w_up, exp_w_down):
    bp, n_p, _ = x_prompt.shape
    bs, n_s, _ = x_sample.shape
    cap_p = 2 * n_p // N_EXP
    cap_s = 2 * n_s // N_EXP
    l = 0

    c_all = jnp.concatenate([c_ctx[None, :], c, jnp.zeros((8 - 1 - bs, D), _F32)], axis=0)
    mod = _ada_call(c_all, ada_w[l], ada_b[l][None, :]).reshape(8, N_MOD, D)

    nvec = jnp.concatenate([norm_mix_pre[l][None], norm_mix_post[l][None], norm_ffn_pre[l][None],
                            norm_ffn_post[l][None], jnp.zeros((4, D), _F32)], axis=0)
    lvec = jnp.concatenate([
        conv_w[l], conv_b[l][None],
        lru_ba[l, 0][None], lru_bi[l, 0][None], lru_lambda[l, 0][None],
        lru_ba[l, 1][None], lru_bi[l, 1][None], lru_lambda[l, 1][None],
        jnp.zeros((5, LRU_W), _F32)], axis=0)
    wg = _block_diag_gates(lru_wa[l], lru_wi[l])
    rw = jnp.pad(router_w[l], ((0, 0), (0, LANES - N_EXP)))
    cos2, sin2 = _rope_tables(n_s)

    mix_p, st_lru, st_ret = _mixer_call(
        x_prompt, mod, nvec, w_in[l], lvec, wg, _decay_table(n_p), latent=False)
    mix_s, = _mixer_call(
        x_sample, mod, nvec, w_in[l], lvec, wg, _decay_table(n_s), latent=True,
        extra=(state_lru[:, l], state_ret[:, l], cos2, sin2))
    x1_p, h2_p, lt_p = _post_call(x_prompt, mix_p, mod, nvec, w_out[l], rw, latent=False)
    x1_s, h2_s, lt_s = _post_call(x_sample, mix_s, mod, nvec, w_out[l], rw, latent=True)

    pos_p, gate_p, pos_s, gate_s = _route_call(lt_p, lt_s, cap_p, cap_s)
    pos_p = pos_p.reshape(bp, N_EXP, n_p)
    gate_p = gate_p.reshape(bp, N_EXP, n_p)
    pos_s = pos_s.reshape(bs, N_EXP, n_s)
    gate_s = gate_s.reshape(bs, N_EXP, n_s)

    xs_p, gsl_p = _dispatch_call(pos_p, gate_p, h2_p, cap_p)
    xs_s, gsl_s = _dispatch_call(pos_s, gate_s, h2_s, cap_s)
    ye = _expert_call(xs_p, xs_s, gsl_p, gsl_s, exp_w_gate[l], exp_w_up[l], exp_w_down[l])

    post_p = jnp.swapaxes(pos_p, 1, 2).astype(_BF)
    post_s = jnp.swapaxes(pos_s, 1, 2).astype(_BF)
    y_p = _combine_call(post_p, ye, x1_p, mod, nvec, cap_p, 0, latent=False)
    y_s = _combine_call(post_s, ye, x1_s, mod, nvec, cap_s, bp * cap_p, latent=True)
    return (y_p, y_s, st_lru, st_ret)
```

```python
import functools
import math

import jax
import jax.numpy as jnp
from jax import lax
from jax.experimental import pallas as pl
from jax.experimental.pallas import tpu as pltpu

D = 1024
LRU_W = 512
LRU_HEADS = 8
LRU_HD = 64
LRU_C = 8.0
RET_W = 512
RET_H = 4
DK = 128
N_EXP = 16
FF = 2048
N_MOD = 6
EPS = 1e-6
GRID_W = 64
ROPE_BASE = 10000.0
RET_DECAY_OFFSET_FWD = 5.0
RET_DECAY_OFFSET_BWD = 5.5

ROW_TILE = 256
FF_CHUNK = 512
COMBINE_TILE = 512
SUBLANES = 8
LANES = 128
VMEM_LIMIT = 60 * 1024 * 1024

DEC_LOG_F, DEC_LOG_B, DEC_G_F, DEC_GN_B, DEC_GN1_F, DEC_ROWS = 0, 1, 2, 3, 4, 5

_BF = jnp.bfloat16
_F32 = jnp.float32


def _sigmoid(x):
    return 0.5 * jnp.tanh(0.5 * x) + 0.5


def _silu(x):
    return x * _sigmoid(x)


def _gelu_tanh(x):
    c = math.sqrt(2.0 / math.pi)
    return 0.5 * x * (1.0 + jnp.tanh(c * (x + 0.044715 * (x * x * x))))


def _rms(x, gain):
    return x * lax.rsqrt(jnp.mean(x * x, axis=-1, keepdims=True) + EPS) * gain


def _dot(a, b):
    return jnp.dot(a, b, preferred_element_type=_F32)


def _dot_nt(a, b):
    return lax.dot_general(a, b, (((1,), (1,)), ((), ())), preferred_element_type=_F32)


def _dot_tn(a, b):
    return lax.dot_general(a, b, (((0,), (0,)), ((), ())), preferred_element_type=_F32)


def _ada_kernel(c_ref, w_ref, b_ref, o_ref):
    o_ref[...] = _dot(_silu(c_ref[...]), w_ref[...]) + b_ref[...]


def _ada_call(c_all, ada_w, ada_b):
    tn = 1536
    n_out = ada_w.shape[1]
    return pl.pallas_call(
        _ada_kernel,
        out_shape=jax.ShapeDtypeStruct((8, n_out), _F32),
        grid=(n_out // tn,),
        in_specs=[
            pl.BlockSpec((8, D), lambda j: (0, 0)),
            pl.BlockSpec((D, tn), lambda j: (0, j)),
            pl.BlockSpec((1, tn), lambda j: (0, j)),
        ],
        out_specs=pl.BlockSpec((8, tn), lambda j: (0, j)),
        compiler_params=pltpu.CompilerParams(
            dimension_semantics=("arbitrary",), vmem_limit_bytes=VMEM_LIMIT),
        name="ada_mod",
    )(c_all, ada_w, ada_b)


def _mixer_kernel(*refs, n, latent):
    if latent:
        (x_ref, mod_ref, nv_ref, wl_ref, wqk_ref, wvg_ref, lv_ref, wg_ref, dec_ref, wout_ref, rw_ref,
         h0_ref, s0_ref, cos_ref, sin_ref,
         x1_ref, h2_ref, lt_ref,
         xlp, gy, a_f, a_b, h_f, h_b) = refs
    else:
        (x_ref, mod_ref, nv_ref, wl_ref, wqk_ref, wvg_ref, lv_ref, wg_ref, dec_ref, wout_ref, rw_ref,
         x1_ref, h2_ref, lt_ref, stl_ref, str_ref,
         xlp, gy, a_f, a_b, h_f, h_b) = refs
    mix_ref = x1_ref
    sg = xlp

    nt = n // ROW_TILE
    shift = mod_ref[0:1, :]
    scale = nv_ref[0:1, :] * (1.0 + mod_ref[1:2, :])

    def normed(rows):
        x = x_ref[rows, :]
        return x * lax.rsqrt(jnp.mean(x * x, axis=-1, keepdims=True) + EPS) * scale + shift

    xlp[0:SUBLANES, :] = jnp.zeros((SUBLANES, LRU_W), _F32)
    xlp[n + SUBLANES:n + 2 * SUBLANES, :] = jnp.zeros((SUBLANES, LRU_W), _F32)

    def phase_a1(r, carry):
        r0 = pl.multiple_of(r * ROW_TILE, ROW_TILE)
        rows = pl.ds(r0, ROW_TILE)
        p = _dot(normed(rows), wl_ref[...])
        xlp[pl.ds(r0 + SUBLANES, ROW_TILE), :] = p[:, 0:LRU_W]
        gy[rows, :] = _gelu_tanh(p[:, LRU_W:])
        return carry

    lax.fori_loop(0, nt, phase_a1, 0)

    half = LRU_W // 2

    def softplus_neg(lam):
        z = -lam
        return jnp.maximum(z, 0.0) + jnp.log1p(jnp.exp(-jnp.abs(z)))

    sp = (softplus_neg(lv_ref[7:8, :]), softplus_neg(lv_ref[10:11, :]))

    def phase_b(r, carry):
        r0 = pl.multiple_of(r * ROW_TILE, ROW_TILE)
        rows = pl.ds(r0, ROW_TILE)
        ext = xlp[pl.ds(r0, ROW_TILE + 2 * SUBLANES), :]
        xc = lv_ref[4:5, :]
        for tap in range(4):
            o = SUBLANES - 2 + tap
            xc = xc + ext[o:o + ROW_TILE, :] * lv_ref[tap:tap + 1, :]
        xh = 0.5 * xc
        for d, (a_ref, u_ref) in enumerate(((a_f, h_f), (a_b, h_b))):
            bah = 0.5 * lv_ref[5 + 3 * d:6 + 3 * d, :]
            bih = 0.5 * lv_ref[6 + 3 * d:7 + 3 * d, :]
            ch = (-0.5 * LRU_C) * sp[d]
            for hh in range(2):
                cs = slice(hh * half, (hh + 1) * half)
                pre = _dot(xc[:, cs], wg_ref[d, hh])
                t_r = jnp.tanh(pre[:, 0:half] + bah[:, cs])
                t_i = jnp.tanh(pre[:, half:] + bih[:, cs])
                log_a = t_r * ch[:, cs] + ch[:, cs]
                a = jnp.exp(log_a)
                a_ref[rows, cs] = a
                om = -jnp.tanh(log_a) * (a * a + 1.0)
                root = jnp.where(om > 0.0, om * lax.rsqrt(om), 0.0)
                u_ref[rows, cs] = root * (t_i * xh[:, cs] + xh[:, cs])
        return carry

    lax.fori_loop(0, nt, phase_b, 0)

    row_id = lax.broadcasted_iota(jnp.int32, (SUBLANES, LRU_W), 0)

    def scan_group(a, b, shift_of, mask_of):
        for s in (1, 2, 4):
            m = mask_of(s)
            a_s = jnp.where(m, pltpu.roll(a, shift_of(s), axis=0), 1.0)
            b_s = jnp.where(m, pltpu.roll(b, shift_of(s), axis=0), 0.0)
            b = a * b_s + b
            a = a * a_s
        return a, b

    if latent:
        init_f = h0_ref[0:1, :]
        init_b = h0_ref[1:2, :]
    else:
        init_f = jnp.zeros((1, LRU_W), _F32)
        init_b = jnp.zeros((1, LRU_W), _F32)
    ng = n // SUBLANES

    def fwd_body(g, carry):
        rows = pl.ds(pl.multiple_of(g * SUBLANES, SUBLANES), SUBLANES)
        a, b = scan_group(a_f[rows, :], h_f[rows, :], lambda s: s, lambda s: row_id >= s)
        h = a * carry + b
        h_f[rows, :] = h
        return h[SUBLANES - 1:SUBLANES, :]

    last_f = lax.fori_loop(0, ng, fwd_body, init_f, unroll=4)

    def bwd_body(i, carry):
        g = ng - 1 - i
        rows = pl.ds(pl.multiple_of(g * SUBLANES, SUBLANES), SUBLANES)
        a, b = scan_group(a_b[rows, :], h_b[rows, :], lambda s: SUBLANES - s,
                          lambda s: row_id < SUBLANES - s)
        h = a * carry + b
        h_b[rows, :] = h
        return h[0:1, :]

    first_b = lax.fori_loop(0, ng, bwd_body, init_b, unroll=4)

    if not latent:
        stl_ref[0:1, :] = last_f
        stl_ref[1:2, :] = first_b

    def lru_out(r, carry):
        rows = pl.ds(pl.multiple_of(r * ROW_TILE, ROW_TILE), ROW_TILE)
        mix_ref[rows, 0:LRU_W] = (h_f[rows, :] + h_b[rows, :]) * gy[rows, :]
        return carry

    lax.fori_loop(0, nt, lru_out, 0)

    q_f, q_b, k_f, k_b, v_s = a_f, a_b, h_f, h_b, gy

    def phase_a2(r, carry):
        r0 = pl.multiple_of(r * ROW_TILE, ROW_TILE)
        rows = pl.ds(r0, ROW_TILE)
        h = normed(rows)
        pqk = _dot(h, wqk_ref[...])
        pvg = _dot(h, wvg_ref[...])
        sg[rows, :] = _silu(pvg[:, RET_W:])
        tpos = (r0 + lax.broadcasted_iota(jnp.int32, (ROW_TILE, DK), 0)).astype(_F32)
        if latent:
            cos2 = cos_ref[rows, :]
            sin2 = sin_ref[rows, :]
        for hd in range(RET_H):
            cols = slice(hd * DK, (hd + 1) * DK)
            q = pqk[:, cols] * (DK ** -0.5)
            k = pqk[:, RET_W + hd * DK:RET_W + (hd + 1) * DK]
            if latent:
                q = q * cos2 + pltpu.roll(q, DK // 2, axis=1) * sin2
                k = k * cos2 + pltpu.roll(k, DK // 2, axis=1) * sin2
            lgf = dec_ref[hd * DEC_ROWS + DEC_LOG_F:hd * DEC_ROWS + DEC_LOG_F + 1, :]
            lgb = dec_ref[hd * DEC_ROWS + DEC_LOG_B:hd * DEC_ROWS + DEC_LOG_B + 1, :]
            q_f[rows, cols] = q * jnp.exp(tpos * lgf)
            k_f[rows, cols] = k * jnp.exp(tpos * (-lgf))
            q_b[rows, cols] = q * jnp.exp(tpos * (-lgb))
            k_b[rows, cols] = k * jnp.exp(tpos * lgb)
        v_s[rows, :] = pvg[:, 0:RET_W]
        return carry

    lax.fori_loop(0, nt, phase_a2, 0)

    lower = (lax.broadcasted_iota(jnp.int32, (ROW_TILE, ROW_TILE), 0)
             >= lax.broadcasted_iota(jnp.int32, (ROW_TILE, ROW_TILE), 1))
    blocks = [slice(r * ROW_TILE, (r + 1) * ROW_TILE) for r in range(nt)]

    for hd in range(RET_H):
        cols = slice(hd * DK, (hd + 1) * DK)
        dec = lambda row, hd=hd: dec_ref[hd * DEC_ROWS + row:hd * DEC_ROWS + row + 1, :]
        kv_f = [_dot_tn(k_f[rows, cols], v_s[rows, cols]) if (r < nt - 1 or not latent) else 0.0
                for r, rows in enumerate(blocks)]
        kv_b = [_dot_tn(k_b[rows, cols], v_s[rows, cols]) if (r > 0 or not latent) else 0.0
                for r, rows in enumerate(blocks)]
        if latent:
            run_f = s0_ref[0, hd] * dec(DEC_G_F)
            run_b = s0_ref[1, hd] * dec(DEC_GN_B)
        else:
            run_f = run_b = None
        before = []
        for r in range(nt):
            before.append(run_f)
            run_f = kv_f[r] if run_f is None else run_f + kv_f[r]
        after = [None] * nt
        for r in reversed(range(nt)):
            after[r] = run_b
            run_b = kv_b[r] if run_b is None else run_b + kv_b[r]

        for r, rows in enumerate(blocks):
            qf = q_f[rows, cols]
            qb = q_b[rows, cols]
            s = jnp.where(lower, _dot_nt(qf, k_f[rows, cols]), _dot_nt(qb, k_b[rows, cols]))
            o = _dot(s, v_s[rows, cols])
            if before[r] is not None:
                o = o + _dot(qf, before[r])
            if after[r] is not None:
                o = o + _dot(qb, after[r])
            o = o * lax.rsqrt(jnp.mean(o * o, axis=-1, keepdims=True) + EPS)
            mix_ref[rows, LRU_W + cols.start:LRU_W + cols.stop] = o * sg[rows, cols]
        if not latent:
            str_ref[0, hd] = run_f * dec(DEC_GN1_F)
            str_ref[1, hd] = run_b

    gain1 = mod_ref[2:3, :] * nv_ref[1:2, :]
    gain2 = nv_ref[2:3, :] * (1.0 + mod_ref[4:5, :])
    sh2 = mod_ref[3:4, :]

    def phase_d(r, carry):
        rows = pl.ds(pl.multiple_of(r * ROW_TILE, ROW_TILE), ROW_TILE)
        mix = _dot(mix_ref[rows, :], wout_ref[...])
        x1 = x_ref[rows, :] + mix * lax.rsqrt(jnp.mean(mix * mix, axis=-1, keepdims=True) + EPS) * gain1
        x1_ref[rows, :] = x1
        h2 = x1 * lax.rsqrt(jnp.mean(x1 * x1, axis=-1, keepdims=True) + EPS) * gain2 + sh2
        h2_ref[rows, :] = h2.astype(_BF)
        lt_ref[:, rows] = _dot(h2, rw_ref[...]).T[0:N_EXP, :]
        return carry

    lax.fori_loop(0, nt, phase_d, 0, unroll=min(nt, 2))


def _mixer_call(x, mod, nvec, w_in, lvec, wg, dec, w_out, rw, latent, extra=()):
    b, n, _ = x.shape
    const2 = lambda i: (0, 0)
    mod_idx = (lambda i: (i + 1, 0, 0)) if latent else (lambda i: (0, 0, 0))
    in_specs = [
        pl.BlockSpec((None, n, D), lambda i: (i, 0, 0)),
        pl.BlockSpec((None, N_MOD, D), mod_idx),
        pl.BlockSpec((8, D), const2),
        pl.BlockSpec((D, D), lambda i: (0, 0)),
        pl.BlockSpec((D, D), lambda i: (0, 1)),
        pl.BlockSpec((D, D), lambda i: (0, 2)),
        pl.BlockSpec((16, LRU_W), const2),
        pl.BlockSpec((2, 2, LRU_W // 2, LRU_W), lambda i: (0, 0, 0, 0)),
        pl.BlockSpec((RET_H * DEC_ROWS + 4, LANES), const2),
        pl.BlockSpec((D, D), const2),
        pl.BlockSpec((D, LANES), const2),
    ]
    out_shape = [
        jax.ShapeDtypeStruct((b, n, D), _F32),
        jax.ShapeDtypeStruct((b, n, D), _BF),
        jax.ShapeDtypeStruct((b, N_EXP, n), _F32),
    ]
    out_specs = [
        pl.BlockSpec((None, n, D), lambda i: (i, 0, 0)),
        pl.BlockSpec((None, n, D), lambda i: (i, 0, 0)),
        pl.BlockSpec((None, N_EXP, n), lambda i: (i, 0, 0)),
    ]
    if latent:
        in_specs += [
            pl.BlockSpec((None, 2, LRU_W), lambda i: (i, 0, 0)),
            pl.BlockSpec((None, 2, RET_H, DK, DK), lambda i: (i, 0, 0, 0, 0)),
            pl.BlockSpec((n, DK), const2),
            pl.BlockSpec((n, DK), const2),
        ]
    else:
        out_shape += [
            jax.ShapeDtypeStruct((b, 1, 2, LRU_W), _F32),
            jax.ShapeDtypeStruct((b, 1, 2, RET_H, DK, DK), _F32),
        ]
        out_specs += [
            pl.BlockSpec((None, None, 2, LRU_W), lambda i: (i, 0, 0, 0)),
            pl.BlockSpec((None, None, 2, RET_H, DK, DK), lambda i: (i, 0, 0, 0, 0, 0)),
        ]
    f32s = lambda shape: pltpu.VMEM(shape, _F32)
    scratch = [
        f32s((n + 2 * SUBLANES, LRU_W)),
        f32s((n, LRU_W)),
        f32s((n, LRU_W)), f32s((n, LRU_W)),
        f32s((n, LRU_W)), f32s((n, LRU_W)),
    ]
    return pl.pallas_call(
        functools.partial(_mixer_kernel, n=n, latent=latent),
        out_shape=out_shape,
        grid=(b,),
        in_specs=in_specs,
        out_specs=out_specs,
        scratch_shapes=scratch,
        compiler_params=pltpu.CompilerParams(
            dimension_semantics=("arbitrary",), vmem_limit_bytes=VMEM_LIMIT),
        name="mixer_latent" if latent else "mixer_context",
    )(x, mod, nvec, w_in, w_in, w_in, lvec, wg, dec, w_out, rw, *extra)


def _count(mask):
    return jnp.sum(jnp.where(mask, 1.0, 0.0), axis=-1, keepdims=True)


def _probs(l3):
    bsz, _, n = l3.shape
    m = jnp.max(l3, axis=1, keepdims=True)
    e = jnp.exp(l3 - m)
    return (e / jnp.sum(e, axis=1, keepdims=True)).reshape(bsz * N_EXP, n)


def _finish_select(p, bits, thr, cap):
    rows, n = p.shape
    gt = bits > thr
    eq = bits == thr
    need = float(cap) - _count(gt)
    idx = lax.broadcasted_iota(jnp.int32, (rows, n), 1)
    nbits = int(math.log2(n))

    def idx_body(i, j):
        cand = j | (jnp.int32(1) << (nbits - 1 - i))
        return jnp.where(_count(eq & (idx < cand)) < need, cand, j)

    jlast = lax.fori_loop(0, nbits, idx_body, jnp.zeros((rows, 1), jnp.int32))
    sel = gt | (eq & (idx <= jlast))
    before = (lax.broadcasted_iota(jnp.int32, (n, n), 0)
              < lax.broadcasted_iota(jnp.int32, (n, n), 1))
    pos = _dot(jnp.where(sel, 1.0, 0.0).astype(_BF), jnp.where(before, 1.0, 0.0).astype(_BF))
    return jnp.where(sel, pos, -1.0), jnp.where(sel, p, 0.0)


def _route_kernel(lp_ref, ls_ref, pp_ref, gp_ref, ps_ref, gs_ref, *, cap_p, cap_s):
    groups = ((_probs(lp_ref[...]), float(cap_p)), (_probs(ls_ref[...]), float(cap_s)))
    bits = [pltpu.bitcast(p, jnp.int32) for p, _ in groups]

    def val_body(i, thr):
        out = []
        for b, t, (_, capf) in zip(bits, thr, groups):
            cand = t | (jnp.int32(1) << (30 - i))
            out.append(jnp.where(_count(b >= cand) >= capf, cand, t))
        return tuple(out)

    thr = lax.fori_loop(0, 31, val_body,
                        tuple(jnp.zeros((b.shape[0], 1), jnp.int32) for b in bits))
    pos, gate = _finish_select(groups[0][0], bits[0], thr[0], cap_p)
    pp_ref[...] = pos
    gp_ref[...] = gate
    pos, gate = _finish_select(groups[1][0], bits[1], thr[1], cap_s)
    ps_ref[...] = pos
    gs_ref[...] = gate


def _route_call(lt_p, lt_s, cap_p, cap_s):
    bp, _, n_p = lt_p.shape
    bs, _, n_s = lt_s.shape
    shapes = [
        jax.ShapeDtypeStruct((bp * N_EXP, n_p), _F32), jax.ShapeDtypeStruct((bp * N_EXP, n_p), _F32),
        jax.ShapeDtypeStruct((bs * N_EXP, n_s), _F32), jax.ShapeDtypeStruct((bs * N_EXP, n_s), _F32),
    ]
    return pl.pallas_call(
        functools.partial(_route_kernel, cap_p=cap_p, cap_s=cap_s),
        out_shape=shapes,
        compiler_params=pltpu.CompilerParams(vmem_limit_bytes=VMEM_LIMIT),
        name="route_select",
    )(lt_p, lt_s)


def _dispatch_kernel(pos_ref, gate_ref, h_ref, xs_ref, gs_ref, *, n, cap):
    slot = lax.broadcasted_iota(jnp.int32, (cap, n), 0).astype(_F32)
    parts = []
    for e in range(N_EXP):
        hit = pos_ref[e:e + 1, :] == slot
        parts.append(jnp.where(hit, 1.0, 0.0).astype(_BF))
        gs_ref[e] = jnp.sum(jnp.where(hit, gate_ref[e:e + 1, :], 0.0), axis=-1, keepdims=True)
    onehot = jnp.concatenate(parts, axis=0)
    xs = _dot(onehot, h_ref[...]).astype(_BF)
    for e in range(N_EXP):
        xs_ref[e] = xs[e * cap:(e + 1) * cap, :]


def _dispatch_call(pos, gate, h2, cap):
    b, n, _ = h2.shape
    return pl.pallas_call(
        functools.partial(_dispatch_kernel, n=n, cap=cap),
        out_shape=[
            jax.ShapeDtypeStruct((N_EXP, b * cap, D), _BF),
            jax.ShapeDtypeStruct((N_EXP, b * cap, 1), _F32),
        ],
        grid=(b,),
        in_specs=[
            pl.BlockSpec((None, N_EXP, n), lambda i: (i, 0, 0)),
            pl.BlockSpec((None, N_EXP, n), lambda i: (i, 0, 0)),
            pl.BlockSpec((None, n, D), lambda i: (i, 0, 0)),
        ],
        out_specs=[
            pl.BlockSpec((N_EXP, cap, D), lambda i: (0, i, 0)),
            pl.BlockSpec((N_EXP, cap, 1), lambda i: (0, i, 0)),
        ],
        compiler_params=pltpu.CompilerParams(
            dimension_semantics=("arbitrary",), vmem_limit_bytes=VMEM_LIMIT),
        name="dispatch",
    )(pos, gate, h2)


def _expert_kernel(xp_ref, xs_ref, gp_ref, gs_ref, wg_ref, wu_ref, wd_ref, y_ref, xcat, acc,
                   *, sp, nf, tf):
    f = pl.program_id(1)

    @pl.when(f == 0)
    def _():
        xcat[0:sp, :] = xp_ref[...]
        xcat[sp:, :] = xs_ref[...]
        acc[...] = jnp.zeros_like(acc)

    x = xcat[...]
    for c in range(tf // FF_CHUNK):
        cs = slice(c * FF_CHUNK, (c + 1) * FF_CHUNK)
        hg = _dot(x, wg_ref[:, cs].astype(_BF))
        hu = _dot(x, wu_ref[:, cs].astype(_BF))
        hid = (_silu(hg) * hu).astype(_BF)
        acc[...] += _dot(hid, wd_ref[cs, :].astype(_BF))

    @pl.when(f == nf - 1)
    def _():
        y_ref[0:sp, :] = (acc[0:sp, :] * gp_ref[...]).astype(_BF)
        y_ref[sp:, :] = (acc[sp:, :] * gs_ref[...]).astype(_BF)


def _expert_call(xs_p, xs_s, g_p, g_s, w_gate, w_up, w_down):
    tf = 1024
    sp = xs_p.shape[1]
    ss = xs_s.shape[1]
    nf = FF // tf
    return pl.pallas_call(
        functools.partial(_expert_kernel, sp=sp, nf=nf, tf=tf),
        out_shape=jax.ShapeDtypeStruct((N_EXP, sp + ss, D), _BF),
        grid=(N_EXP, nf),
        in_specs=[
            pl.BlockSpec((None, sp, D), lambda e, f: (e, 0, 0)),
            pl.BlockSpec((None, ss, D), lambda e, f: (e, 0, 0)),
            pl.BlockSpec((None, sp, 1), lambda e, f: (e, 0, 0)),
            pl.BlockSpec((None, ss, 1), lambda e, f: (e, 0, 0)),
            pl.BlockSpec((None, D, tf), lambda e, f: (e, 0, f)),
            pl.BlockSpec((None, D, tf), lambda e, f: (e, 0, f)),
            pl.BlockSpec((None, tf, D), lambda e, f: (e, f, 0)),
        ],
        out_specs=pl.BlockSpec((None, sp + ss, D), lambda e, f: (e, 0, 0)),
        scratch_shapes=[pltpu.VMEM((sp + ss, D), _BF), pltpu.VMEM((sp + ss, D), _F32)],
        compiler_params=pltpu.CompilerParams(
            dimension_semantics=("arbitrary", "arbitrary"), vmem_limit_bytes=VMEM_LIMIT),
        name="expert_ffn",
    )(xs_p, xs_s, g_p, g_s, w_gate, w_up, w_down)


def _combine_kernel(post_ref, ye_ref, x1_ref, mod_ref, nv_ref, y_ref, *, n, cap):
    width = N_EXP * cap
    lane = lax.broadcasted_iota(jnp.int32, (N_EXP, width), 1)
    expand = jnp.where(lane // cap == lax.broadcasted_iota(jnp.int32, (N_EXP, width), 0),
                       1.0, 0.0).astype(_BF)
    tile = min(n, COMBINE_TILE)
    slot = (lax.broadcasted_iota(jnp.int32, (tile, width), 1) % cap).astype(_F32)
    ye = ye_ref[...].reshape(width, D)
    g2 = mod_ref[5:6, :]

    def body(r, carry):
        rows = pl.ds(pl.multiple_of(r * tile, tile), tile)
        pos_e = _dot(post_ref[rows, :], expand)
        onehot = jnp.where(pos_e == slot, 1.0, 0.0).astype(_BF)
        f = _dot(onehot, ye)
        y_ref[rows, :] = x1_ref[rows, :] + g2 * _rms(f, nv_ref[3:4, :])
        return carry

    lax.fori_loop(0, n // tile, body, 0)


def _combine_call(post, ye, x1, mod, nvec, cap, slot_off, latent):
    b, n, _ = x1.shape
    blk_off = slot_off // cap
    mod_idx = (lambda i: (i + 1, 0, 0)) if latent else (lambda i: (0, 0, 0))
    return pl.pallas_call(
        functools.partial(_combine_kernel, n=n, cap=cap),
        out_shape=jax.ShapeDtypeStruct((b, n, D), _F32),
        grid=(b,),
        in_specs=[
            pl.BlockSpec((None, n, N_EXP), lambda i: (i, 0, 0)),
            pl.BlockSpec((N_EXP, cap, D), lambda i: (0, i + blk_off, 0)),
            pl.BlockSpec((None, n, D), lambda i: (i, 0, 0)),
            pl.BlockSpec((None, N_MOD, D), mod_idx),
            pl.BlockSpec((8, D), lambda i: (0, 0)),
        ],
        out_specs=pl.BlockSpec((None, n, D), lambda i: (i, 0, 0)),
        compiler_params=pltpu.CompilerParams(
            dimension_semantics=("arbitrary",), vmem_limit_bytes=VMEM_LIMIT),
        name="combine_latent" if latent else "combine_context",
    )(post, ye, x1, mod, nvec)


def _block_diag_gates(wa, wi):
    per_half = LRU_HEADS // 2
    eye = jnp.eye(per_half, dtype=wa.dtype)

    def bd(w):
        w = w.reshape(2, 2, per_half, LRU_HD, LRU_HD)
        full = jnp.einsum('dghij,hk->dghikj', w, eye)
        return full.reshape(2, 2, per_half * LRU_HD, per_half * LRU_HD)

    return 0.5 * jnp.concatenate([bd(wa), bd(wi)], axis=-1)


def _rope_tables(n):
    rows = n // GRID_W
    row = jnp.repeat(jnp.arange(rows, dtype=_F32), GRID_W)
    col = jnp.tile(jnp.arange(GRID_W, dtype=_F32), rows)
    nf = DK // 4
    freqs = ROPE_BASE ** (-jnp.arange(nf, dtype=_F32) / nf)
    ang = jnp.concatenate([row[:, None] * freqs, col[:, None] * freqs], axis=-1)
    cos = jnp.cos(ang)
    sin = jnp.sin(ang)
    return jnp.concatenate([cos, cos], axis=-1), jnp.concatenate([-sin, sin], axis=-1)


def _decay_table(n):
    heads = jnp.arange(RET_H, dtype=_F32)
    lgf = jnp.log1p(-jnp.exp2(-(RET_DECAY_OFFSET_FWD + heads)))
    lgb = jnp.log1p(-jnp.exp2(-(RET_DECAY_OFFSET_BWD + heads)))
    tab = jnp.stack([lgf, lgb, jnp.exp(lgf), jnp.exp(n * lgb), jnp.exp((n - 1) * lgf)], axis=1)
    tab = jnp.concatenate([tab.reshape(RET_H * DEC_ROWS), jnp.zeros((4,), _F32)])
    return jnp.broadcast_to(tab[:, None], (RET_H * DEC_ROWS + 4, LANES))


def kernel(x_prompt, x_sample, c, state_lru, state_ret, c_ctx, ada_w, ada_b, norm_mix_pre, norm_mix_post, norm_ffn_pre, norm_ffn_post, w_in, conv_w, conv_b, lru_wa, lru_ba, lru_wi, lru_bi, lru_lambda, w_out, router_w, exp_w_gate, exp_w_up, exp_w_down):
    bp, n_p, _ = x_prompt.shape
    bs, n_s, _ = x_sample.shape
    cap_p = 2 * n_p // N_EXP
    cap_s = 2 * n_s // N_EXP
    l = 0

    c_all = jnp.concatenate([c_ctx[None, :], c, jnp.zeros((8 - 1 - bs, D), _F32)], axis=0)
    mod = _ada_call(c_all, ada_w[l], ada_b[l][None, :]).reshape(8, N_MOD, D)

    nvec = jnp.concatenate([norm_mix_pre[l][None], norm_mix_post[l][None], norm_ffn_pre[l][None],
                            norm_ffn_post[l][None], jnp.zeros((4, D), _F32)], axis=0)
    lvec = jnp.concatenate([
        conv_w[l], conv_b[l][None],
        lru_ba[l, 0][None], lru_bi[l, 0][None], lru_lambda[l, 0][None],
        lru_ba[l, 1][None], lru_bi[l, 1][None], lru_lambda[l, 1][None],
        jnp.zeros((5, LRU_W), _F32)], axis=0)
    wg = _block_diag_gates(lru_wa[l], lru_wi[l])
    rw = jnp.pad(router_w[l], ((0, 0), (0, LANES - N_EXP)))
    cos2, sin2 = _rope_tables(n_s)

    x1_p, h2_p, lt_p, st_lru, st_ret = _mixer_call(
        x_prompt, mod, nvec, w_in[l], lvec, wg, _decay_table(n_p), w_out[l], rw, latent=False)
    x1_s, h2_s, lt_s = _mixer_call(
        x_sample, mod, nvec, w_in[l], lvec, wg, _decay_table(n_s), w_out[l], rw, latent=True,
        extra=(state_lru[:, l], state_ret[:, l], cos2, sin2))

    pos_p, gate_p, pos_s, gate_s = _route_call(lt_p, lt_s, cap_p, cap_s)
    pos_p = pos_p.reshape(bp, N_EXP, n_p)
    gate_p = gate_p.reshape(bp, N_EXP, n_p)
    pos_s = pos_s.reshape(bs, N_EXP, n_s)
    gate_s = gate_s.reshape(bs, N_EXP, n_s)

    xs_p, gsl_p = _dispatch_call(pos_p, gate_p, h2_p, cap_p)
    xs_s, gsl_s = _dispatch_call(pos_s, gate_s, h2_s, cap_s)
    ye = _expert_call(xs_p, xs_s, gsl_p, gsl_s, exp_w_gate[l], exp_w_up[l], exp_w_down[l])

    post_p = jnp.swapaxes(pos_p, 1, 2).astype(_BF)
    post_s = jnp.swapaxes(pos_s, 1, 2).astype(_BF)
    y_p = _combine_call(post_p, ye, x1_p, mod, nvec, cap_p, 0, latent=False)
    y_s = _combine_call(post_s, ye, x1_s, mod, nvec, cap_s, bp * cap_p, latent=True)
    return (y_p, y_s, st_lru, st_ret)
```

```python
import functools
import math

import jax
import jax.numpy as jnp
from jax import lax
from jax.experimental import pallas as pl
from jax.experimental.pallas import tpu as pltpu

D = 1024
LRU_W = 512
LRU_HEADS = 8
LRU_HD = 64
LRU_C = 8.0
RET_W = 512
RET_H = 4
DK = 128
N_EXP = 16
FF = 2048
N_MOD = 6
EPS = 1e-6
GRID_W = 64
ROPE_BASE = 10000.0
RET_DECAY_OFFSET_FWD = 5.0
RET_DECAY_OFFSET_BWD = 5.5

ROW_TILE = 256
FF_CHUNK = 512
COMBINE_TILE = 512
SUBLANES = 8
LANES = 128
VMEM_LIMIT = 60 * 1024 * 1024

DEC_LOG_F, DEC_LOG_B, DEC_G_F, DEC_GN_B, DEC_GN1_F, DEC_ROWS = 0, 1, 2, 3, 4, 5

_BF = jnp.bfloat16
_F32 = jnp.float32


def _sigmoid(x):
    return 0.5 * jnp.tanh(0.5 * x) + 0.5


def _silu(x):
    return x * _sigmoid(x)


def _gelu_tanh(x):
    c = math.sqrt(2.0 / math.pi)
    return 0.5 * x * (1.0 + jnp.tanh(c * (x + 0.044715 * (x * x * x))))


def _rms(x, gain):
    return x * lax.rsqrt(jnp.mean(x * x, axis=-1, keepdims=True) + EPS) * gain


def _dot(a, b):
    return jnp.dot(a, b, preferred_element_type=_F32)


def _dot_nt(a, b):
    return lax.dot_general(a, b, (((1,), (1,)), ((), ())), preferred_element_type=_F32)


def _dot_tn(a, b):
    return lax.dot_general(a, b, (((0,), (0,)), ((), ())), preferred_element_type=_F32)


def _ada_kernel(c_ref, w_ref, b_ref, o_ref):
    o_ref[...] = _dot(_silu(c_ref[...]), w_ref[...]) + b_ref[...]


def _ada_call(c_all, ada_w, ada_b):
    tn = 1536
    n_out = ada_w.shape[1]
    return pl.pallas_call(
        _ada_kernel,
        out_shape=jax.ShapeDtypeStruct((8, n_out), _F32),
        grid=(n_out // tn,),
        in_specs=[
            pl.BlockSpec((8, D), lambda j: (0, 0)),
            pl.BlockSpec((D, tn), lambda j: (0, j)),
            pl.BlockSpec((1, tn), lambda j: (0, j)),
        ],
        out_specs=pl.BlockSpec((8, tn), lambda j: (0, j)),
        compiler_params=pltpu.CompilerParams(
            dimension_semantics=("arbitrary",), vmem_limit_bytes=VMEM_LIMIT),
        name="ada_mod",
    )(c_all, ada_w, ada_b)


def _mixer_kernel(*refs, n, latent):
    if latent:
        (x_ref, mod_ref, nv_ref, wl_ref, wqk_ref, wvg_ref, lv_ref, wg_ref, dec_ref, wout_ref, rw_ref,
         h0_ref, s0_ref, cos_ref, sin_ref,
         x1_ref, h2_ref, lt_ref,
         xlp, gy, a_f, a_b, h_f, h_b) = refs
    else:
        (x_ref, mod_ref, nv_ref, wl_ref, wqk_ref, wvg_ref, lv_ref, wg_ref, dec_ref, wout_ref, rw_ref,
         x1_ref, h2_ref, lt_ref, stl_ref, str_ref,
         xlp, gy, a_f, a_b, h_f, h_b) = refs
    mix_ref = x1_ref
    sg = xlp

    nt = n // ROW_TILE
    shift = mod_ref[0:1, :]
    scale = nv_ref[0:1, :] * (1.0 + mod_ref[1:2, :])

    def normed(rows):
        x = x_ref[rows, :]
        return x * lax.rsqrt(jnp.mean(x * x, axis=-1, keepdims=True) + EPS) * scale + shift

    xlp[0:SUBLANES, :] = jnp.zeros((SUBLANES, LRU_W), _F32)
    xlp[n + SUBLANES:n + 2 * SUBLANES, :] = jnp.zeros((SUBLANES, LRU_W), _F32)

    def phase_a1(r, carry):
        r0 = pl.multiple_of(r * ROW_TILE, ROW_TILE)
        rows = pl.ds(r0, ROW_TILE)
        p = _dot(normed(rows), wl_ref[...])
        xlp[pl.ds(r0 + SUBLANES, ROW_TILE), :] = p[:, 0:LRU_W]
        gy[rows, :] = _gelu_tanh(p[:, LRU_W:])
        return carry

    lax.fori_loop(0, nt, phase_a1, 0)

    half = LRU_W // 2

    def softplus_neg(lam):
        z = -lam
        return jnp.maximum(z, 0.0) + jnp.log1p(jnp.exp(-jnp.abs(z)))

    sp = (softplus_neg(lv_ref[7:8, :]), softplus_neg(lv_ref[10:11, :]))

    def phase_b(r, carry):
        r0 = pl.multiple_of(r * ROW_TILE, ROW_TILE)
        rows = pl.ds(r0, ROW_TILE)
        ext = xlp[pl.ds(r0, ROW_TILE + 2 * SUBLANES), :]
        xc = lv_ref[4:5, :]
        for tap in range(4):
            o = SUBLANES - 2 + tap
            xc = xc + ext[o:o + ROW_TILE, :] * lv_ref[tap:tap + 1, :]
        xh = 0.5 * xc
        for d, (a_ref, u_ref) in enumerate(((a_f, h_f), (a_b, h_b))):
            bah = 0.5 * lv_ref[5 + 3 * d:6 + 3 * d, :]
            bih = 0.5 * lv_ref[6 + 3 * d:7 + 3 * d, :]
            ch = (-0.5 * LRU_C) * sp[d]
            for hh in range(2):
                cs = slice(hh * half, (hh + 1) * half)
                pre = _dot(xc[:, cs], wg_ref[d, hh])
                t_r = jnp.tanh(pre[:, 0:half] + bah[:, cs])
                t_i = jnp.tanh(pre[:, half:] + bih[:, cs])
                log_a = t_r * ch[:, cs] + ch[:, cs]
                a = jnp.exp(log_a)
                a_ref[rows, cs] = a
                om = -jnp.tanh(log_a) * (a * a + 1.0)
                root = jnp.where(om > 0.0, om * lax.rsqrt(om), 0.0)
                u_ref[rows, cs] = root * (t_i * xh[:, cs] + xh[:, cs])
        return carry

    lax.fori_loop(0, nt, phase_b, 0)

    row_id = lax.broadcasted_iota(jnp.int32, (SUBLANES, LRU_W), 0)

    def scan_group(a, b, shift_of, mask_of):
        for s in (1, 2, 4):
            m = mask_of(s)
            a_s = jnp.where(m, pltpu.roll(a, shift_of(s), axis=0), 1.0)
            b_s = jnp.where(m, pltpu.roll(b, shift_of(s), axis=0), 0.0)
            b = a * b_s + b
            a = a * a_s
        return a, b

    if latent:
        init_f = h0_ref[0:1, :]
        init_b = h0_ref[1:2, :]
    else:
        init_f = jnp.zeros((1, LRU_W), _F32)
        init_b = jnp.zeros((1, LRU_W), _F32)
    ng = n // SUBLANES

    def fwd_body(g, carry):
        rows = pl.ds(pl.multiple_of(g * SUBLANES, SUBLANES), SUBLANES)
        a, b = scan_group(a_f[rows, :], h_f[rows, :], lambda s: s, lambda s: row_id >= s)
        h = a * carry + b
        h_f[rows, :] = h
        return h[SUBLANES - 1:SUBLANES, :]

    last_f = lax.fori_loop(0, ng, fwd_body, init_f, unroll=4)

    def bwd_body(i, carry):
        g = ng - 1 - i
        rows = pl.ds(pl.multiple_of(g * SUBLANES, SUBLANES), SUBLANES)
        a, b = scan_group(a_b[rows, :], h_b[rows, :], lambda s: SUBLANES - s,
                          lambda s: row_id < SUBLANES - s)
        h = a * carry + b
        h_b[rows, :] = h
        return h[0:1, :]

    first_b = lax.fori_loop(0, ng, bwd_body, init_b, unroll=4)

    if not latent:
        stl_ref[0:1, :] = last_f
        stl_ref[1:2, :] = first_b

    def lru_out(r, carry):
        rows = pl.ds(pl.multiple_of(r * ROW_TILE, ROW_TILE), ROW_TILE)
        mix_ref[rows, 0:LRU_W] = (h_f[rows, :] + h_b[rows, :]) * gy[rows, :]
        return carry

    lax.fori_loop(0, nt, lru_out, 0)

    q_f, q_b, k_f, k_b, v_s = a_f, a_b, h_f, h_b, gy

    def phase_a2(r, carry):
        r0 = pl.multiple_of(r * ROW_TILE, ROW_TILE)
        rows = pl.ds(r0, ROW_TILE)
        h = normed(rows)
        pqk = _dot(h, wqk_ref[...])
        pvg = _dot(h, wvg_ref[...])
        sg[rows, :] = _silu(pvg[:, RET_W:])
        tpos = (r0 + lax.broadcasted_iota(jnp.int32, (ROW_TILE, DK), 0)).astype(_F32)
        if latent:
            cos2 = cos_ref[rows, :]
            sin2 = sin_ref[rows, :]
        for hd in range(RET_H):
            cols = slice(hd * DK, (hd + 1) * DK)
            q = pqk[:, cols] * (DK ** -0.5)
            k = pqk[:, RET_W + hd * DK:RET_W + (hd + 1) * DK]
            if latent:
                q = q * cos2 + pltpu.roll(q, DK // 2, axis=1) * sin2
                k = k * cos2 + pltpu.roll(k, DK // 2, axis=1) * sin2
            lgf = dec_ref[hd * DEC_ROWS + DEC_LOG_F:hd * DEC_ROWS + DEC_LOG_F + 1, :]
            lgb = dec_ref[hd * DEC_ROWS + DEC_LOG_B:hd * DEC_ROWS + DEC_LOG_B + 1, :]
            q_f[rows, cols] = q * jnp.exp(tpos * lgf)
            k_f[rows, cols] = k * jnp.exp(tpos * (-lgf))
            q_b[rows, cols] = q * jnp.exp(tpos * (-lgb))
            k_b[rows, cols] = k * jnp.exp(tpos * lgb)
        v_s[rows, :] = pvg[:, 0:RET_W]
        return carry

    lax.fori_loop(0, nt, phase_a2, 0)

    lower = (lax.broadcasted_iota(jnp.int32, (ROW_TILE, ROW_TILE), 0)
             >= lax.broadcasted_iota(jnp.int32, (ROW_TILE, ROW_TILE), 1))
    blocks = [slice(r * ROW_TILE, (r + 1) * ROW_TILE) for r in range(nt)]

    for hd in range(RET_H):
        cols = slice(hd * DK, (hd + 1) * DK)
        dec = lambda row, hd=hd: dec_ref[hd * DEC_ROWS + row:hd * DEC_ROWS + row + 1, :]
        kv_f = [_dot_tn(k_f[rows, cols], v_s[rows, cols]) if (r < nt - 1 or not latent) else 0.0
                for r, rows in enumerate(blocks)]
        kv_b = [_dot_tn(k_b[rows, cols], v_s[rows, cols]) if (r > 0 or not latent) else 0.0
                for r, rows in enumerate(blocks)]
        if latent:
            run_f = s0_ref[0, hd] * dec(DEC_G_F)
            run_b = s0_ref[1, hd] * dec(DEC_GN_B)
        else:
            run_f = run_b = None
        before = []
        for r in range(nt):
            before.append(run_f)
            run_f = kv_f[r] if run_f is None else run_f + kv_f[r]
        after = [None] * nt
        for r in reversed(range(nt)):
            after[r] = run_b
            run_b = kv_b[r] if run_b is None else run_b + kv_b[r]

        for r, rows in enumerate(blocks):
            qf = q_f[rows, cols]
            qb = q_b[rows, cols]
            s = jnp.where(lower, _dot_nt(qf, k_f[rows, cols]), _dot_nt(qb, k_b[rows, cols]))
            o = _dot(s, v_s[rows, cols])
            if before[r] is not None:
                o = o + _dot(qf, before[r])
            if after[r] is not None:
                o = o + _dot(qb, after[r])
            o = o * lax.rsqrt(jnp.mean(o * o, axis=-1, keepdims=True) + EPS)
            mix_ref[rows, LRU_W + cols.start:LRU_W + cols.stop] = o * sg[rows, cols]
        if not latent:
            str_ref[0, hd] = run_f * dec(DEC_GN1_F)
            str_ref[1, hd] = run_b

    gain1 = mod_ref[2:3, :] * nv_ref[1:2, :]
    gain2 = nv_ref[2:3, :] * (1.0 + mod_ref[4:5, :])
    sh2 = mod_ref[3:4, :]

    def phase_d(r, carry):
        rows = pl.ds(pl.multiple_of(r * ROW_TILE, ROW_TILE), ROW_TILE)
        mix = _dot(mix_ref[rows, :], wout_ref[...])
        x1 = x_ref[rows, :] + mix * lax.rsqrt(jnp.mean(mix * mix, axis=-1, keepdims=True) + EPS) * gain1
        x1_ref[rows, :] = x1
        h2 = x1 * lax.rsqrt(jnp.mean(x1 * x1, axis=-1, keepdims=True) + EPS) * gain2 + sh2
        h2_ref[rows, :] = h2.astype(_BF)
        lt_ref[:, rows] = _dot(h2, rw_ref[...]).T[0:N_EXP, :]
        return carry

    lax.fori_loop(0, nt, phase_d, 0, unroll=min(nt, 2))


def _mixer_call(x, mod, nvec, w_in, lvec, wg, dec, w_out, rw, latent, extra=()):
    b, n, _ = x.shape
    const2 = lambda i: (0, 0)
    mod_idx = (lambda i: (i + 1, 0, 0)) if latent else (lambda i: (0, 0, 0))
    in_specs = [
        pl.BlockSpec((None, n, D), lambda i: (i, 0, 0)),
        pl.BlockSpec((None, N_MOD, D), mod_idx),
        pl.BlockSpec((8, D), const2),
        pl.BlockSpec((D, D), lambda i: (0, 0)),
        pl.BlockSpec((D, D), lambda i: (0, 1)),
        pl.BlockSpec((D, D), lambda i: (0, 2)),
        pl.BlockSpec((16, LRU_W), const2),
        pl.BlockSpec((2, 2, LRU_W // 2, LRU_W), lambda i: (0, 0, 0, 0)),
        pl.BlockSpec((RET_H * DEC_ROWS + 4, LANES), const2),
        pl.BlockSpec((D, D), const2),
        pl.BlockSpec((D, LANES), const2),
    ]
    out_shape = [
        jax.ShapeDtypeStruct((b, n, D), _F32),
        jax.ShapeDtypeStruct((b, n, D), _BF),
        jax.ShapeDtypeStruct((b, N_EXP, n), _F32),
    ]
    out_specs = [
        pl.BlockSpec((None, n, D), lambda i: (i, 0, 0)),
        pl.BlockSpec((None, n, D), lambda i: (i, 0, 0)),
        pl.BlockSpec((None, N_EXP, n), lambda i: (i, 0, 0)),
    ]
    if latent:
        in_specs += [
            pl.BlockSpec((None, 2, LRU_W), lambda i: (i, 0, 0)),
            pl.BlockSpec((None, 2, RET_H, DK, DK), lambda i: (i, 0, 0, 0, 0)),
            pl.BlockSpec((n, DK), const2),
            pl.BlockSpec((n, DK), const2),
        ]
    else:
        out_shape += [
            jax.ShapeDtypeStruct((b, 1, 2, LRU_W), _F32),
            jax.ShapeDtypeStruct((b, 1, 2, RET_H, DK, DK), _F32),
        ]
        out_specs += [
            pl.BlockSpec((None, None, 2, LRU_W), lambda i: (i, 0, 0, 0)),
            pl.BlockSpec((None, None, 2, RET_H, DK, DK), lambda i: (i, 0, 0, 0, 0, 0)),
        ]
    f32s = lambda shape: pltpu.VMEM(shape, _F32)
    scratch = [
        f32s((n + 2 * SUBLANES, LRU_W)),
        f32s((n, LRU_W)),
        f32s((n, LRU_W)), f32s((n, LRU_W)),
        f32s((n, LRU_W)), f32s((n, LRU_W)),
    ]
    return pl.pallas_call(
        functools.partial(_mixer_kernel, n=n, latent=latent),
        out_shape=out_shape,
        grid=(b,),
        in_specs=in_specs,
        out_specs=out_specs,
        scratch_shapes=scratch,
        compiler_params=pltpu.CompilerParams(
            dimension_semantics=("arbitrary",), vmem_limit_bytes=VMEM_LIMIT),
        name="mixer_latent" if latent else "mixer_context",
    )(x, mod, nvec, w_in, w_in, w_in, lvec, wg, dec, w_out, rw, *extra)


def _count(mask):
    return jnp.sum(jnp.where(mask, 1.0, 0.0), axis=-1, keepdims=True)


def _probs(l3):
    bsz, _, n = l3.shape
    m = jnp.max(l3, axis=1, keepdims=True)
    e = jnp.exp(l3 - m)
    return (e / jnp.sum(e, axis=1, keepdims=True)).reshape(bsz * N_EXP, n)


def _finish_select(p, bits, thr, cap):
    rows, n = p.shape
    gt = bits > thr
    eq = bits == thr
    need = float(cap) - _count(gt)
    idx = lax.broadcasted_iota(jnp.int32, (rows, n), 1)
    nbits = int(math.log2(n))

    def idx_body(i, j):
        cand = j | (jnp.int32(1) << (nbits - 1 - i))
        return jnp.where(_count(eq & (idx < cand)) < need, cand, j)

    jlast = lax.fori_loop(0, nbits, idx_body, jnp.zeros((rows, 1), jnp.int32))
    sel = gt | (eq & (idx <= jlast))
    before = (lax.broadcasted_iota(jnp.int32, (n, n), 0)
              < lax.broadcasted_iota(jnp.int32, (n, n), 1))
    pos = _dot(jnp.where(sel, 1.0, 0.0).astype(_BF), jnp.where(before, 1.0, 0.0).astype(_BF))
    return jnp.where(sel, pos, -1.0), jnp.where(sel, p, 0.0)


def _route_kernel(lp_ref, ls_ref, pp_ref, gp_ref, ps_ref, gs_ref, *, cap_p, cap_s):
    groups = ((_probs(lp_ref[...]), float(cap_p)), (_probs(ls_ref[...]), float(cap_s)))
    bits = [pltpu.bitcast(p, jnp.int32) for p, _ in groups]

    def val_body(i, thr):
        out = []
        for b, t, (_, capf) in zip(bits, thr, groups):
            cand = t | (jnp.int32(1) << (30 - i))
            out.append(jnp.where(_count(b >= cand) >= capf, cand, t))
        return tuple(out)

    thr = lax.fori_loop(0, 31, val_body,
                        tuple(jnp.zeros((b.shape[0], 1), jnp.int32) for b in bits))
    pos, gate = _finish_select(groups[0][0], bits[0], thr[0], cap_p)
    pp_ref[...] = pos
    gp_ref[...] = gate
    pos, gate = _finish_select(groups[1][0], bits[1], thr[1], cap_s)
    ps_ref[...] = pos
    gs_ref[...] = gate


def _route_call(lt_p, lt_s, cap_p, cap_s):
    bp, _, n_p = lt_p.shape
    bs, _, n_s = lt_s.shape
    shapes = [
        jax.ShapeDtypeStruct((bp * N_EXP, n_p), _F32), jax.ShapeDtypeStruct((bp * N_EXP, n_p), _F32),
        jax.ShapeDtypeStruct((bs * N_EXP, n_s), _F32), jax.ShapeDtypeStruct((bs * N_EXP, n_s), _F32),
    ]
    return pl.pallas_call(
        functools.partial(_route_kernel, cap_p=cap_p, cap_s=cap_s),
        out_shape=shapes,
        compiler_params=pltpu.CompilerParams(vmem_limit_bytes=VMEM_LIMIT),
        name="route_select",
    )(lt_p, lt_s)


def _dispatch_kernel(pos_ref, gate_ref, h_ref, xs_ref, gs_ref, *, n, cap):
    slot = lax.broadcasted_iota(jnp.int32, (cap, n), 0).astype(_F32)
    parts = []
    for e in range(N_EXP):
        hit = pos_ref[e:e + 1, :] == slot
        parts.append(jnp.where(hit, 1.0, 0.0).astype(_BF))
        gs_ref[e] = jnp.sum(jnp.where(hit, gate_ref[e:e + 1, :], 0.0), axis=-1, keepdims=True)
    onehot = jnp.concatenate(parts, axis=0)
    xs = _dot(onehot, h_ref[...]).astype(_BF)
    for e in range(N_EXP):
        xs_ref[e] = xs[e * cap:(e + 1) * cap, :]


def _dispatch_call(pos, gate, h2, cap):
    b, n, _ = h2.shape
    return pl.pallas_call(
        functools.partial(_dispatch_kernel, n=n, cap=cap),
        out_shape=[
            jax.ShapeDtypeStruct((N_EXP, b * cap, D), _BF),
            jax.ShapeDtypeStruct((N_EXP, b * cap, 1), _F32),
        ],
        grid=(b,),
        in_specs=[
            pl.BlockSpec((None, N_EXP, n), lambda i: (i, 0, 0)),
            pl.BlockSpec((None, N_EXP, n), lambda i: (i, 0, 0)),
            pl.BlockSpec((None, n, D), lambda i: (i, 0, 0)),
        ],
        out_specs=[
            pl.BlockSpec((N_EXP, cap, D), lambda i: (0, i, 0)),
            pl.BlockSpec((N_EXP, cap, 1), lambda i: (0, i, 0)),
        ],
        compiler_params=pltpu.CompilerParams(
            dimension_semantics=("arbitrary",), vmem_limit_bytes=VMEM_LIMIT),
        name="dispatch",
    )(pos, gate, h2)


def _expert_kernel(xp_ref, xs_ref, gp_ref, gs_ref, wg_ref, wu_ref, wd_ref, y_ref, xcat, acc,
                   *, sp, nf, tf):
    f = pl.program_id(1)
    xcat[0:sp, :] = xp_ref[...]
    xcat[sp:, :] = xs_ref[...]
    x = xcat[...]
    total = jnp.where(f == 0, 0.0, acc[...])
    for c in range(tf // FF_CHUNK):
        cs = slice(c * FF_CHUNK, (c + 1) * FF_CHUNK)
        hg = _dot(x, wg_ref[:, cs].astype(_BF))
        hu = _dot(x, wu_ref[:, cs].astype(_BF))
        hid = (_silu(hg) * hu).astype(_BF)
        total = total + _dot(hid, wd_ref[cs, :].astype(_BF))
    acc[...] = total
    y_ref[0:sp, :] = (total[0:sp, :] * gp_ref[...]).astype(_BF)
    y_ref[sp:, :] = (total[sp:, :] * gs_ref[...]).astype(_BF)


def _expert_call(xs_p, xs_s, g_p, g_s, w_gate, w_up, w_down):
    tf = 1024
    sp = xs_p.shape[1]
    ss = xs_s.shape[1]
    nf = FF // tf
    return pl.pallas_call(
        functools.partial(_expert_kernel, sp=sp, nf=nf, tf=tf),
        out_shape=jax.ShapeDtypeStruct((N_EXP, sp + ss, D), _BF),
        grid=(N_EXP, nf),
        in_specs=[
            pl.BlockSpec((None, sp, D), lambda e, f: (e, 0, 0)),
            pl.BlockSpec((None, ss, D), lambda e, f: (e, 0, 0)),
            pl.BlockSpec((None, sp, 1), lambda e, f: (e, 0, 0)),
            pl.BlockSpec((None, ss, 1), lambda e, f: (e, 0, 0)),
            pl.BlockSpec((None, D, tf), lambda e, f: (e, 0, f)),
            pl.BlockSpec((None, D, tf), lambda e, f: (e, 0, f)),
            pl.BlockSpec((None, tf, D), lambda e, f: (e, f, 0)),
        ],
        out_specs=pl.BlockSpec((None, sp + ss, D), lambda e, f: (e, 0, 0)),
        scratch_shapes=[pltpu.VMEM((sp + ss, D), _BF), pltpu.VMEM((sp + ss, D), _F32)],
        compiler_params=pltpu.CompilerParams(
            dimension_semantics=("arbitrary", "arbitrary"), vmem_limit_bytes=VMEM_LIMIT),
        name="expert_ffn",
    )(xs_p, xs_s, g_p, g_s, w_gate, w_up, w_down)


def _combine_kernel(post_ref, ye_ref, x1_ref, mod_ref, nv_ref, y_ref, *, n, cap):
    width = N_EXP * cap
    lane = lax.broadcasted_iota(jnp.int32, (N_EXP, width), 1)
    expand = jnp.where(lane // cap == lax.broadcasted_iota(jnp.int32, (N_EXP, width), 0),
                       1.0, 0.0).astype(_BF)
    tile = min(n, COMBINE_TILE)
    slot = (lax.broadcasted_iota(jnp.int32, (tile, width), 1) % cap).astype(_F32)
    ye = ye_ref[...].reshape(width, D)
    g2 = mod_ref[5:6, :]

    def body(r, carry):
        rows = pl.ds(pl.multiple_of(r * tile, tile), tile)
        pos_e = _dot(post_ref[rows, :], expand)
        onehot = jnp.where(pos_e == slot, 1.0, 0.0).astype(_BF)
        f = _dot(onehot, ye)
        y_ref[rows, :] = x1_ref[rows, :] + g2 * _rms(f, nv_ref[3:4, :])
        return carry

    lax.fori_loop(0, n // tile, body, 0)


def _combine_call(post, ye, x1, mod, nvec, cap, slot_off, latent):
    b, n, _ = x1.shape
    blk_off = slot_off // cap
    mod_idx = (lambda i: (i + 1, 0, 0)) if latent else (lambda i: (0, 0, 0))
    return pl.pallas_call(
        functools.partial(_combine_kernel, n=n, cap=cap),
        out_shape=jax.ShapeDtypeStruct((b, n, D), _F32),
        grid=(b,),
        in_specs=[
            pl.BlockSpec((None, n, N_EXP), lambda i: (i, 0, 0)),
            pl.BlockSpec((N_EXP, cap, D), lambda i: (0, i + blk_off, 0)),
            pl.BlockSpec((None, n, D), lambda i: (i, 0, 0)),
            pl.BlockSpec((None, N_MOD, D), mod_idx),
            pl.BlockSpec((8, D), lambda i: (0, 0)),
        ],
        out_specs=pl.BlockSpec((None, n, D), lambda i: (i, 0, 0)),
        compiler_params=pltpu.CompilerParams(
            dimension_semantics=("arbitrary",), vmem_limit_bytes=VMEM_LIMIT),
        name="combine_latent" if latent else "combine_context",
    )(post, ye, x1, mod, nvec)


def _block_diag_gates(wa, wi):
    per_half = LRU_HEADS // 2
    eye = jnp.eye(per_half, dtype=wa.dtype)

    def bd(w):
        w = w.reshape(2, 2, per_half, LRU_HD, LRU_HD)
        full = jnp.einsum('dghij,hk->dghikj', w, eye)
        return full.reshape(2, 2, per_half * LRU_HD, per_half * LRU_HD)

    return 0.5 * jnp.concatenate([bd(wa), bd(wi)], axis=-1)


def _rope_tables(n):
    rows = n // GRID_W
    row = jnp.repeat(jnp.arange(rows, dtype=_F32), GRID_W)
    col = jnp.tile(jnp.arange(GRID_W, dtype=_F32), rows)
    nf = DK // 4
    freqs = ROPE_BASE ** (-jnp.arange(nf, dtype=_F32) / nf)
    ang = jnp.concatenate([row[:, None] * freqs, col[:, None] * freqs], axis=-1)
    cos = jnp.cos(ang)
    sin = jnp.sin(ang)
    return jnp.concatenate([cos, cos], axis=-1), jnp.concatenate([-sin, sin], axis=-1)


def _decay_table(n):
    heads = jnp.arange(RET_H, dtype=_F32)
    lgf = jnp.log1p(-jnp.exp2(-(RET_DECAY_OFFSET_FWD + heads)))
    lgb = jnp.log1p(-jnp.exp2(-(RET_DECAY_OFFSET_BWD + heads)))
    tab = jnp.stack([lgf, lgb, jnp.exp(lgf), jnp.exp(n * lgb), jnp.exp((n - 1) * lgf)], axis=1)
    tab = jnp.concatenate([tab.reshape(RET_H * DEC_ROWS), jnp.zeros((4,), _F32)])
    return jnp.broadcast_to(tab[:, None], (RET_H * DEC_ROWS + 4, LANES))


def kernel(x_prompt, x_sample, c, state_lru, state_ret, c_ctx, ada_w, ada_b, norm_mix_pre, norm_mix_post, norm_ffn_pre, norm_ffn_post, w_in, conv_w, conv_b, lru_wa, lru_ba, lru_wi, lru_bi, lru_lambda, w_out, router_w, exp_w_gate, exp_w_up, exp_w_down):
    bp, n_p, _ = x_prompt.shape
    bs, n_s, _ = x_sample.shape
    cap_p = 2 * n_p // N_EXP
    cap_s = 2 * n_s // N_EXP
    l = 0

    c_all = jnp.concatenate([c_ctx[None, :], c, jnp.zeros((8 - 1 - bs, D), _F32)], axis=0)
    mod = _ada_call(c_all, ada_w[l], ada_b[l][None, :]).reshape(8, N_MOD, D)

    nvec = jnp.concatenate([norm_mix_pre[l][None], norm_mix_post[l][None], norm_ffn_pre[l][None],
                            norm_ffn_post[l][None], jnp.zeros((4, D), _F32)], axis=0)
    lvec = jnp.concatenate([
        conv_w[l], conv_b[l][None],
        lru_ba[l, 0][None], lru_bi[l, 0][None], lru_lambda[l, 0][None],
        lru_ba[l, 1][None], lru_bi[l, 1][None], lru_lambda[l, 1][None],
        jnp.zeros((5, LRU_W), _F32)], axis=0)
    wg = _block_diag_gates(lru_wa[l], lru_wi[l])
    rw = jnp.pad(router_w[l], ((0, 0), (0, LANES - N_EXP)))
    cos2, sin2 = _rope_tables(n_s)

    x1_p, h2_p, lt_p, st_lru, st_ret = _mixer_call(
        x_prompt, mod, nvec, w_in[l], lvec, wg, _decay_table(n_p), w_out[l], rw, latent=False)
    x1_s, h2_s, lt_s = _mixer_call(
        x_sample, mod, nvec, w_in[l], lvec, wg, _decay_table(n_s), w_out[l], rw, latent=True,
        extra=(state_lru[:, l], state_ret[:, l], cos2, sin2))

    pos_p, gate_p, pos_s, gate_s = _route_call(lt_p, lt_s, cap_p, cap_s)
    pos_p = pos_p.reshape(bp, N_EXP, n_p)
    gate_p = gate_p.reshape(bp, N_EXP, n_p)
    pos_s = pos_s.reshape(bs, N_EXP, n_s)
    gate_s = gate_s.reshape(bs, N_EXP, n_s)

    xs_p, gsl_p = _dispatch_call(pos_p, gate_p, h2_p, cap_p)
    xs_s, gsl_s = _dispatch_call(pos_s, gate_s, h2_s, cap_s)
    ye = _expert_call(xs_p, xs_s, gsl_p, gsl_s, exp_w_gate[l], exp_w_up[l], exp_w_down[l])

    post_p = jnp.swapaxes(pos_p, 1, 2).astype(_BF)
    post_s = jnp.swapaxes(pos_s, 1, 2).astype(_BF)
    y_p = _combine_call(post_p, ye, x1_p, mod, nvec, cap_p, 0, latent=False)
    y_s = _combine_call(post_s, ye, x1_s, mod, nvec, cap_s, bp * cap_p, latent=True)
    return (y_p, y_s, st_lru, st_ret)
```

```python
import functools
import math
import types

import jax
import jax.numpy as jnp
from jax import lax
from jax.experimental import pallas as pl
from jax.experimental.pallas import tpu as pltpu

D = 1024
LRU_W = 512
LRU_HEADS = 8
LRU_HD = 64
LRU_C = 8.0
RET_W = 512
RET_H = 4
DK = 128
N_EXP = 16
FF = 2048
N_MOD = 6
EPS = 1e-6
GRID_W = 64
ROPE_BASE = 10000.0
RET_DECAY_OFFSET_FWD = 5.0
RET_DECAY_OFFSET_BWD = 5.5

ROW_TILE = 256
FF_CHUNK = 512
COMBINE_TILE = 512
CONTEXT_GROUP = 2
SUBLANES = 8
LANES = 128
VMEM_LIMIT = 60 * 1024 * 1024

DEC_LOG_F, DEC_LOG_B, DEC_G_F, DEC_GN_B, DEC_GN1_F, DEC_ROWS = 0, 1, 2, 3, 4, 5

_BF = jnp.bfloat16
_F32 = jnp.float32


def _sigmoid(x):
    return 0.5 * jnp.tanh(0.5 * x) + 0.5


def _silu(x):
    return x * _sigmoid(x)


def _gelu_tanh(x):
    c = math.sqrt(2.0 / math.pi)
    return 0.5 * x * (1.0 + jnp.tanh(c * (x + 0.044715 * (x * x * x))))


def _rms(x, gain):
    return x * lax.rsqrt(jnp.mean(x * x, axis=-1, keepdims=True) + EPS) * gain


def _dot(a, b):
    return jnp.dot(a, b, preferred_element_type=_F32)


def _dot_nt(a, b):
    return lax.dot_general(a, b, (((1,), (1,)), ((), ())), preferred_element_type=_F32)


def _dot_tn(a, b):
    return lax.dot_general(a, b, (((0,), (0,)), ((), ())), preferred_element_type=_F32)


def _ada_kernel(c_ref, w_ref, b_ref, o_ref):
    o_ref[...] = _dot(_silu(c_ref[...]), w_ref[...]) + b_ref[...]


def _ada_call(c_all, ada_w, ada_b):
    tn = 1536
    n_out = ada_w.shape[1]
    return pl.pallas_call(
        _ada_kernel,
        out_shape=jax.ShapeDtypeStruct((8, n_out), _F32),
        grid=(n_out // tn,),
        in_specs=[
            pl.BlockSpec((8, D), lambda j: (0, 0)),
            pl.BlockSpec((D, tn), lambda j: (0, j)),
            pl.BlockSpec((1, tn), lambda j: (0, j)),
        ],
        out_specs=pl.BlockSpec((8, tn), lambda j: (0, j)),
        compiler_params=pltpu.CompilerParams(
            dimension_semantics=("arbitrary",), vmem_limit_bytes=VMEM_LIMIT),
        name="ada_mod",
    )(c_all, ada_w, ada_b)


def _tile_loop(nt, body, unroll=1):
    if nt == 1:
        body(0, 0)
    else:
        lax.fori_loop(0, nt, body, 0, unroll=unroll)


def _mixer_kernel(*refs, n, g, latent):
    if latent:
        (x_ref, mod_ref, nv_ref, wl_ref, wqk_ref, wvg_ref, lv_ref, wg_ref, dec_ref, wout_ref, rw_ref,
         h0_ref, s0_ref, cos_ref, sin_ref,
         x1_ref, h2_ref, lt_ref,
         xlp_g, gy_g, af_g, ab_g, hf_g, hb_g) = refs
    else:
        (x_ref, mod_ref, nv_ref, wl_ref, wqk_ref, wvg_ref, lv_ref, wg_ref, dec_ref, wout_ref, rw_ref,
         x1_ref, h2_ref, lt_ref, stl_ref, str_ref,
         xlp_g, gy_g, af_g, ab_g, hf_g, hb_g) = refs

    seqs = []
    for s in range(g):
        q = types.SimpleNamespace(
            x=x_ref.at[s], x1=x1_ref.at[s], h2=h2_ref.at[s], lt=lt_ref.at[s], mix=x1_ref.at[s],
            xlp=xlp_g.at[s], sg=xlp_g.at[s], gy=gy_g.at[s], v=gy_g.at[s],
            a_f=af_g.at[s], a_b=ab_g.at[s], h_f=hf_g.at[s], h_b=hb_g.at[s],
            q_f=af_g.at[s], q_b=ab_g.at[s], k_f=hf_g.at[s], k_b=hb_g.at[s])
        if latent:
            q.h0, q.s0 = h0_ref.at[s], s0_ref.at[s]
        else:
            q.stl, q.str = stl_ref.at[s], str_ref.at[s]
        seqs.append(q)

    nt = n // ROW_TILE
    shift = mod_ref[0:1, :]
    scale = nv_ref[0:1, :] * (1.0 + mod_ref[1:2, :])

    def normed(q, rows):
        x = q.x[rows, :]
        return x * lax.rsqrt(jnp.mean(x * x, axis=-1, keepdims=True) + EPS) * scale + shift

    def phase_a1(q):
        def body(r, carry):
            r0 = pl.multiple_of(r * ROW_TILE, ROW_TILE)
            rows = pl.ds(r0, ROW_TILE)
            p = _dot(normed(q, rows), wl_ref[...])
            q.xlp[pl.ds(r0 + SUBLANES, ROW_TILE), :] = p[:, 0:LRU_W]
            q.gy[rows, :] = _gelu_tanh(p[:, LRU_W:])
            return carry
        return body

    for q in seqs:
        q.xlp[0:SUBLANES, :] = jnp.zeros((SUBLANES, LRU_W), _F32)
        q.xlp[n + SUBLANES:n + 2 * SUBLANES, :] = jnp.zeros((SUBLANES, LRU_W), _F32)
        _tile_loop(nt, phase_a1(q))

    half = LRU_W // 2

    def softplus_neg(lam):
        z = -lam
        return jnp.maximum(z, 0.0) + jnp.log1p(jnp.exp(-jnp.abs(z)))

    sp = (softplus_neg(lv_ref[7:8, :]), softplus_neg(lv_ref[10:11, :]))

    def phase_b(q):
        def body(r, carry):
            r0 = pl.multiple_of(r * ROW_TILE, ROW_TILE)
            rows = pl.ds(r0, ROW_TILE)
            ext = q.xlp[pl.ds(r0, ROW_TILE + 2 * SUBLANES), :]
            xc = lv_ref[4:5, :]
            for tap in range(4):
                o = SUBLANES - 2 + tap
                xc = xc + ext[o:o + ROW_TILE, :] * lv_ref[tap:tap + 1, :]
            xh = 0.5 * xc
            for d, (a_ref, u_ref) in enumerate(((q.a_f, q.h_f), (q.a_b, q.h_b))):
                bah = 0.5 * lv_ref[5 + 3 * d:6 + 3 * d, :]
                bih = 0.5 * lv_ref[6 + 3 * d:7 + 3 * d, :]
                ch = (-0.5 * LRU_C) * sp[d]
                for hh in range(2):
                    cs = slice(hh * half, (hh + 1) * half)
                    pre = _dot(xc[:, cs], wg_ref[d, hh])
                    t_r = jnp.tanh(pre[:, 0:half] + bah[:, cs])
                    t_i = jnp.tanh(pre[:, half:] + bih[:, cs])
                    log_a = t_r * ch[:, cs] + ch[:, cs]
                    a = jnp.exp(log_a)
                    a_ref[rows, cs] = a
                    om = -jnp.tanh(log_a) * (a * a + 1.0)
                    root = jnp.where(om > 0.0, om * lax.rsqrt(om), 0.0)
                    u_ref[rows, cs] = root * (t_i * xh[:, cs] + xh[:, cs])
            return carry
        return body

    for q in seqs:
        _tile_loop(nt, phase_b(q))

    row_id = lax.broadcasted_iota(jnp.int32, (SUBLANES, LRU_W), 0)

    def scan_group(a, b, shift_of, mask_of):
        for s in (1, 2, 4):
            m = mask_of(s)
            a_s = jnp.where(m, pltpu.roll(a, shift_of(s), axis=0), 1.0)
            b_s = jnp.where(m, pltpu.roll(b, shift_of(s), axis=0), 0.0)
            b = a * b_s + b
            a = a * a_s
        return a, b

    zero_row = jnp.zeros((1, LRU_W), _F32)
    init_f = tuple(q.h0[0:1, :] if latent else zero_row for q in seqs)
    init_b = tuple(q.h0[1:2, :] if latent else zero_row for q in seqs)
    ng = n // SUBLANES

    def fwd_body(i, carry):
        rows = pl.ds(pl.multiple_of(i * SUBLANES, SUBLANES), SUBLANES)
        out = []
        for q, c in zip(seqs, carry):
            a, b = scan_group(q.a_f[rows, :], q.h_f[rows, :], lambda s: s, lambda s: row_id >= s)
            h = a * c + b
            q.h_f[rows, :] = h
            out.append(h[SUBLANES - 1:SUBLANES, :])
        return tuple(out)

    last_f = lax.fori_loop(0, ng, fwd_body, init_f, unroll=4)

    def bwd_body(i, carry):
        rows = pl.ds(pl.multiple_of((ng - 1 - i) * SUBLANES, SUBLANES), SUBLANES)
        out = []
        for q, c in zip(seqs, carry):
            a, b = scan_group(q.a_b[rows, :], q.h_b[rows, :], lambda s: SUBLANES - s,
                              lambda s: row_id < SUBLANES - s)
            h = a * c + b
            q.h_b[rows, :] = h
            out.append(h[0:1, :])
        return tuple(out)

    first_b = lax.fori_loop(0, ng, bwd_body, init_b, unroll=4)

    def lru_out(q):
        def body(r, carry):
            rows = pl.ds(pl.multiple_of(r * ROW_TILE, ROW_TILE), ROW_TILE)
            q.mix[rows, 0:LRU_W] = (q.h_f[rows, :] + q.h_b[rows, :]) * q.gy[rows, :]
            return carry
        return body

    for s, q in enumerate(seqs):
        if not latent:
            q.stl[0:1, :] = last_f[s]
            q.stl[1:2, :] = first_b[s]
        _tile_loop(nt, lru_out(q))

    def phase_a2(q):
        def body(r, carry):
            r0 = pl.multiple_of(r * ROW_TILE, ROW_TILE)
            rows = pl.ds(r0, ROW_TILE)
            h = normed(q, rows)
            pqk = _dot(h, wqk_ref[...])
            pvg = _dot(h, wvg_ref[...])
            q.sg[rows, :] = _silu(pvg[:, RET_W:])
            tpos = (r0 + lax.broadcasted_iota(jnp.int32, (ROW_TILE, DK), 0)).astype(_F32)
            if latent:
                cos2 = cos_ref[rows, :]
                sin2 = sin_ref[rows, :]
            for hd in range(RET_H):
                cols = slice(hd * DK, (hd + 1) * DK)
                qh = pqk[:, cols] * (DK ** -0.5)
                kh = pqk[:, RET_W + hd * DK:RET_W + (hd + 1) * DK]
                if latent:
                    qh = qh * cos2 + pltpu.roll(qh, DK // 2, axis=1) * sin2
                    kh = kh * cos2 + pltpu.roll(kh, DK // 2, axis=1) * sin2
                lgf = dec_ref[hd * DEC_ROWS + DEC_LOG_F:hd * DEC_ROWS + DEC_LOG_F + 1, :]
                lgb = dec_ref[hd * DEC_ROWS + DEC_LOG_B:hd * DEC_ROWS + DEC_LOG_B + 1, :]
                q.q_f[rows, cols] = qh * jnp.exp(tpos * lgf)
                q.k_f[rows, cols] = kh * jnp.exp(tpos * (-lgf))
                q.q_b[rows, cols] = qh * jnp.exp(tpos * (-lgb))
                q.k_b[rows, cols] = kh * jnp.exp(tpos * lgb)
            q.v[rows, :] = pvg[:, 0:RET_W]
            return carry
        return body

    for q in seqs:
        _tile_loop(nt, phase_a2(q))

    lower = (lax.broadcasted_iota(jnp.int32, (ROW_TILE, ROW_TILE), 0)
             >= lax.broadcasted_iota(jnp.int32, (ROW_TILE, ROW_TILE), 1))
    blocks = [slice(r * ROW_TILE, (r + 1) * ROW_TILE) for r in range(nt)]

    def phase_c(q, hd):
        cols = slice(hd * DK, (hd + 1) * DK)
        dec = lambda row: dec_ref[hd * DEC_ROWS + row:hd * DEC_ROWS + row + 1, :]
        kv_f = [_dot_tn(q.k_f[rows, cols], q.v[rows, cols]) if (r < nt - 1 or not latent) else 0.0
                for r, rows in enumerate(blocks)]
        kv_b = [_dot_tn(q.k_b[rows, cols], q.v[rows, cols]) if (r > 0 or not latent) else 0.0
                for r, rows in enumerate(blocks)]
        if latent:
            run_f = q.s0[0, hd] * dec(DEC_G_F)
            run_b = q.s0[1, hd] * dec(DEC_GN_B)
        else:
            run_f = run_b = None
        before = []
        for r in range(nt):
            before.append(run_f)
            run_f = kv_f[r] if run_f is None else run_f + kv_f[r]
        after = [None] * nt
        for r in reversed(range(nt)):
            after[r] = run_b
            run_b = kv_b[r] if run_b is None else run_b + kv_b[r]

        for r, rows in enumerate(blocks):
            qf = q.q_f[rows, cols]
            qb = q.q_b[rows, cols]
            s = jnp.where(lower, _dot_nt(qf, q.k_f[rows, cols]), _dot_nt(qb, q.k_b[rows, cols]))
            o = _dot(s, q.v[rows, cols])
            if before[r] is not None:
                o = o + _dot(qf, before[r])
            if after[r] is not None:
                o = o + _dot(qb, after[r])
            o = o * lax.rsqrt(jnp.mean(o * o, axis=-1, keepdims=True) + EPS)
            q.mix[rows, LRU_W + cols.start:LRU_W + cols.stop] = o * q.sg[rows, cols]
        if not latent:
            q.str[0, hd] = run_f * dec(DEC_GN1_F)
            q.str[1, hd] = run_b

    for hd in range(RET_H):
        for q in seqs:
            phase_c(q, hd)

    gain1 = mod_ref[2:3, :] * nv_ref[1:2, :]
    gain2 = nv_ref[2:3, :] * (1.0 + mod_ref[4:5, :])
    sh2 = mod_ref[3:4, :]

    def phase_d(q):
        def body(r, carry):
            rows = pl.ds(pl.multiple_of(r * ROW_TILE, ROW_TILE), ROW_TILE)
            mix = _dot(q.mix[rows, :], wout_ref[...])
            x1 = q.x[rows, :] + mix * lax.rsqrt(jnp.mean(mix * mix, axis=-1, keepdims=True) + EPS) * gain1
            q.x1[rows, :] = x1
            h2 = x1 * lax.rsqrt(jnp.mean(x1 * x1, axis=-1, keepdims=True) + EPS) * gain2 + sh2
            q.h2[rows, :] = h2.astype(_BF)
            q.lt[:, rows] = _dot(h2, rw_ref[...]).T[0:N_EXP, :]
            return carry
        return body

    for q in seqs:
        _tile_loop(nt, phase_d(q), unroll=2)


def _mixer_call(x, mod, nvec, w_in, lvec, wg, dec, w_out, rw, latent, extra=()):
    b, n, _ = x.shape
    g = 1 if latent else CONTEXT_GROUP
    const2 = lambda i: (0, 0)
    mod_idx = (lambda i: (i + 1, 0, 0)) if latent else (lambda i: (0, 0, 0))
    in_specs = [
        pl.BlockSpec((g, n, D), lambda i: (i, 0, 0)),
        pl.BlockSpec((None, N_MOD, D), mod_idx),
        pl.BlockSpec((8, D), const2),
        pl.BlockSpec((D, D), lambda i: (0, 0)),
        pl.BlockSpec((D, D), lambda i: (0, 1)),
        pl.BlockSpec((D, D), lambda i: (0, 2)),
        pl.BlockSpec((16, LRU_W), const2),
        pl.BlockSpec((2, 2, LRU_W // 2, LRU_W), lambda i: (0, 0, 0, 0)),
        pl.BlockSpec((RET_H * DEC_ROWS + 4, LANES), const2),
        pl.BlockSpec((D, D), const2),
        pl.BlockSpec((D, LANES), const2),
    ]
    out_shape = [
        jax.ShapeDtypeStruct((b, n, D), _F32),
        jax.ShapeDtypeStruct((b, n, D), _BF),
        jax.ShapeDtypeStruct((b, N_EXP, n), _F32),
    ]
    out_specs = [
        pl.BlockSpec((g, n, D), lambda i: (i, 0, 0)),
        pl.BlockSpec((g, n, D), lambda i: (i, 0, 0)),
        pl.BlockSpec((g, N_EXP, n), lambda i: (i, 0, 0)),
    ]
    if latent:
        in_specs += [
            pl.BlockSpec((g, 2, LRU_W), lambda i: (i, 0, 0)),
            pl.BlockSpec((g, 2, RET_H, DK, DK), lambda i: (i, 0, 0, 0, 0)),
            pl.BlockSpec((n, DK), const2),
            pl.BlockSpec((n, DK), const2),
        ]
    else:
        out_shape += [
            jax.ShapeDtypeStruct((b, 1, 2, LRU_W), _F32),
            jax.ShapeDtypeStruct((b, 1, 2, RET_H, DK, DK), _F32),
        ]
        out_specs += [
            pl.BlockSpec((g, None, 2, LRU_W), lambda i: (i, 0, 0, 0)),
            pl.BlockSpec((g, None, 2, RET_H, DK, DK), lambda i: (i, 0, 0, 0, 0, 0)),
        ]
    f32s = lambda shape: pltpu.VMEM((g,) + shape, _F32)
    scratch = [
        f32s((n + 2 * SUBLANES, LRU_W)),
        f32s((n, LRU_W)),
        f32s((n, LRU_W)), f32s((n, LRU_W)),
        f32s((n, LRU_W)), f32s((n, LRU_W)),
    ]
    return pl.pallas_call(
        functools.partial(_mixer_kernel, n=n, g=g, latent=latent),
        out_shape=out_shape,
        grid=(b // g,),
        in_specs=in_specs,
        out_specs=out_specs,
        scratch_shapes=scratch,
        compiler_params=pltpu.CompilerParams(
            dimension_semantics=("arbitrary",), vmem_limit_bytes=VMEM_LIMIT),
        name="mixer_latent" if latent else "mixer_context",
    )(x, mod, nvec, w_in, w_in, w_in, lvec, wg, dec, w_out, rw, *extra)


def _count(mask):
    return jnp.sum(jnp.where(mask, 1.0, 0.0), axis=-1, keepdims=True)


def _probs(l3):
    bsz, _, n = l3.shape
    m = jnp.max(l3, axis=1, keepdims=True)
    e = jnp.exp(l3 - m)
    return (e / jnp.sum(e, axis=1, keepdims=True)).reshape(bsz * N_EXP, n)


def _finish_select(p, bits, thr, cap):
    rows, n = p.shape
    gt = bits > thr
    eq = bits == thr
    need = float(cap) - _count(gt)
    idx = lax.broadcasted_iota(jnp.int32, (rows, n), 1)
    nbits = int(math.log2(n))

    def idx_body(i, j):
        cand = j | (jnp.int32(1) << (nbits - 1 - i))
        return jnp.where(_count(eq & (idx < cand)) < need, cand, j)

    jlast = lax.fori_loop(0, nbits, idx_body, jnp.zeros((rows, 1), jnp.int32))
    sel = gt | (eq & (idx <= jlast))
    before = (lax.broadcasted_iota(jnp.int32, (n, n), 0)
              < lax.broadcasted_iota(jnp.int32, (n, n), 1))
    pos = _dot(jnp.where(sel, 1.0, 0.0).astype(_BF), jnp.where(before, 1.0, 0.0).astype(_BF))
    return jnp.where(sel, pos, -1.0), jnp.where(sel, p, 0.0)


def _route_kernel(lp_ref, ls_ref, pp_ref, gp_ref, ps_ref, gs_ref, *, cap_p, cap_s):
    groups = ((_probs(lp_ref[...]), float(cap_p)), (_probs(ls_ref[...]), float(cap_s)))
    bits = [pltpu.bitcast(p, jnp.int32) for p, _ in groups]

    def val_body(i, thr):
        out = []
        for b, t, (_, capf) in zip(bits, thr, groups):
            cand = t | (jnp.int32(1) << (30 - i))
            out.append(jnp.where(_count(b >= cand) >= capf, cand, t))
        return tuple(out)

    thr = lax.fori_loop(0, 31, val_body,
                        tuple(jnp.zeros((b.shape[0], 1), jnp.int32) for b in bits))
    pos, gate = _finish_select(groups[0][0], bits[0], thr[0], cap_p)
    pp_ref[...] = pos
    gp_ref[...] = gate
    pos, gate = _finish_select(groups[1][0], bits[1], thr[1], cap_s)
    ps_ref[...] = pos
    gs_ref[...] = gate


def _route_call(lt_p, lt_s, cap_p, cap_s):
    bp, _, n_p = lt_p.shape
    bs, _, n_s = lt_s.shape
    shapes = [
        jax.ShapeDtypeStruct((bp * N_EXP, n_p), _F32), jax.ShapeDtypeStruct((bp * N_EXP, n_p), _F32),
        jax.ShapeDtypeStruct((bs * N_EXP, n_s), _F32), jax.ShapeDtypeStruct((bs * N_EXP, n_s), _F32),
    ]
    return pl.pallas_call(
        functools.partial(_route_kernel, cap_p=cap_p, cap_s=cap_s),
        out_shape=shapes,
        compiler_params=pltpu.CompilerParams(vmem_limit_bytes=VMEM_LIMIT),
        name="route_select",
    )(lt_p, lt_s)


def _dispatch_kernel(pos_ref, gate_ref, h_ref, xs_ref, gs_ref, *, n, cap):
    slot = lax.broadcasted_iota(jnp.int32, (cap, n), 0).astype(_F32)
    parts = []
    for e in range(N_EXP):
        hit = pos_ref[e:e + 1, :] == slot
        parts.append(jnp.where(hit, 1.0, 0.0).astype(_BF))
        gs_ref[e] = jnp.sum(jnp.where(hit, gate_ref[e:e + 1, :], 0.0), axis=-1, keepdims=True)
    onehot = jnp.concatenate(parts, axis=0)
    xs = _dot(onehot, h_ref[...]).astype(_BF)
    for e in range(N_EXP):
        xs_ref[e] = xs[e * cap:(e + 1) * cap, :]


def _dispatch_call(pos, gate, h2, cap):
    b, n, _ = h2.shape
    return pl.pallas_call(
        functools.partial(_dispatch_kernel, n=n, cap=cap),
        out_shape=[
            jax.ShapeDtypeStruct((N_EXP, b * cap, D), _BF),
            jax.ShapeDtypeStruct((N_EXP, b * cap, 1), _F32),
        ],
        grid=(b,),
        in_specs=[
            pl.BlockSpec((None, N_EXP, n), lambda i: (i, 0, 0)),
            pl.BlockSpec((None, N_EXP, n), lambda i: (i, 0, 0)),
            pl.BlockSpec((None, n, D), lambda i: (i, 0, 0)),
        ],
        out_specs=[
            pl.BlockSpec((N_EXP, cap, D), lambda i: (0, i, 0)),
            pl.BlockSpec((N_EXP, cap, 1), lambda i: (0, i, 0)),
        ],
        compiler_params=pltpu.CompilerParams(
            dimension_semantics=("arbitrary",), vmem_limit_bytes=VMEM_LIMIT),
        name="dispatch",
    )(pos, gate, h2)


def _expert_kernel(xp_ref, xs_ref, gp_ref, gs_ref, wg_ref, wu_ref, wd_ref, y_ref, xcat, acc,
                   *, sp, nf, tf):
    f = pl.program_id(1)
    xcat[0:sp, :] = xp_ref[...]
    xcat[sp:, :] = xs_ref[...]
    x = xcat[...]
    total = jnp.where(f == 0, 0.0, acc[...])
    for c in range(tf // FF_CHUNK):
        cs = slice(c * FF_CHUNK, (c + 1) * FF_CHUNK)
        hg = _dot(x, wg_ref[:, cs].astype(_BF))
        hu = _dot(x, wu_ref[:, cs].astype(_BF))
        hid = (_silu(hg) * hu).astype(_BF)
        total = total + _dot(hid, wd_ref[cs, :].astype(_BF))
    acc[...] = total
    y_ref[0:sp, :] = (total[0:sp, :] * gp_ref[...]).astype(_BF)
    y_ref[sp:, :] = (total[sp:, :] * gs_ref[...]).astype(_BF)


def _expert_call(xs_p, xs_s, g_p, g_s, w_gate, w_up, w_down):
    tf = 1024
    sp = xs_p.shape[1]
    ss = xs_s.shape[1]
    nf = FF // tf
    return pl.pallas_call(
        functools.partial(_expert_kernel, sp=sp, nf=nf, tf=tf),
        out_shape=jax.ShapeDtypeStruct((N_EXP, sp + ss, D), _BF),
        grid=(N_EXP, nf),
        in_specs=[
            pl.BlockSpec((None, sp, D), lambda e, f: (e, 0, 0)),
            pl.BlockSpec((None, ss, D), lambda e, f: (e, 0, 0)),
            pl.BlockSpec((None, sp, 1), lambda e, f: (e, 0, 0)),
            pl.BlockSpec((None, ss, 1), lambda e, f: (e, 0, 0)),
            pl.BlockSpec((None, D, tf), lambda e, f: (e, 0, f)),
            pl.BlockSpec((None, D, tf), lambda e, f: (e, 0, f)),
            pl.BlockSpec((None, tf, D), lambda e, f: (e, f, 0)),
        ],
        out_specs=pl.BlockSpec((None, sp + ss, D), lambda e, f: (e, 0, 0)),
        scratch_shapes=[pltpu.VMEM((sp + ss, D), _BF), pltpu.VMEM((sp + ss, D), _F32)],
        compiler_params=pltpu.CompilerParams(
            dimension_semantics=("arbitrary", "arbitrary"), vmem_limit_bytes=VMEM_LIMIT),
        name="expert_ffn",
    )(xs_p, xs_s, g_p, g_s, w_gate, w_up, w_down)


def _combine_kernel(post_ref, ye_ref, x1_ref, mod_ref, nv_ref, y_ref, *, n, cap):
    width = N_EXP * cap
    lane = lax.broadcasted_iota(jnp.int32, (N_EXP, width), 1)
    expand = jnp.where(lane // cap == lax.broadcasted_iota(jnp.int32, (N_EXP, width), 0),
                       1.0, 0.0).astype(_BF)
    tile = min(n, COMBINE_TILE)
    slot = (lax.broadcasted_iota(jnp.int32, (tile, width), 1) % cap).astype(_F32)
    ye = ye_ref[...].reshape(width, D)
    g2 = mod_ref[5:6, :]

    def body(r, carry):
        rows = pl.ds(pl.multiple_of(r * tile, tile), tile)
        pos_e = _dot(post_ref[rows, :], expand)
        onehot = jnp.where(pos_e == slot, 1.0, 0.0).astype(_BF)
        f = _dot(onehot, ye)
        y_ref[rows, :] = x1_ref[rows, :] + g2 * _rms(f, nv_ref[3:4, :])
        return carry

    lax.fori_loop(0, n // tile, body, 0)


def _combine_call(post, ye, x1, mod, nvec, cap, slot_off, latent):
    b, n, _ = x1.shape
    blk_off = slot_off // cap
    mod_idx = (lambda i: (i + 1, 0, 0)) if latent else (lambda i: (0, 0, 0))
    return pl.pallas_call(
        functools.partial(_combine_kernel, n=n, cap=cap),
        out_shape=jax.ShapeDtypeStruct((b, n, D), _F32),
        grid=(b,),
        in_specs=[
            pl.BlockSpec((None, n, N_EXP), lambda i: (i, 0, 0)),
            pl.BlockSpec((N_EXP, cap, D), lambda i: (0, i + blk_off, 0)),
            pl.BlockSpec((None, n, D), lambda i: (i, 0, 0)),
            pl.BlockSpec((None, N_MOD, D), mod_idx),
            pl.BlockSpec((8, D), lambda i: (0, 0)),
        ],
        out_specs=pl.BlockSpec((None, n, D), lambda i: (i, 0, 0)),
        compiler_params=pltpu.CompilerParams(
            dimension_semantics=("arbitrary",), vmem_limit_bytes=VMEM_LIMIT),
        name="combine_latent" if latent else "combine_context",
    )(post, ye, x1, mod, nvec)


def _block_diag_gates(wa, wi):
    per_half = LRU_HEADS // 2
    eye = jnp.eye(per_half, dtype=wa.dtype)

    def bd(w):
        w = w.reshape(2, 2, per_half, LRU_HD, LRU_HD)
        full = jnp.einsum('dghij,hk->dghikj', w, eye)
        return full.reshape(2, 2, per_half * LRU_HD, per_half * LRU_HD)

    return 0.5 * jnp.concatenate([bd(wa), bd(wi)], axis=-1)


def _rope_tables(n):
    rows = n // GRID_W
    row = jnp.repeat(jnp.arange(rows, dtype=_F32), GRID_W)
    col = jnp.tile(jnp.arange(GRID_W, dtype=_F32), rows)
    nf = DK // 4
    freqs = ROPE_BASE ** (-jnp.arange(nf, dtype=_F32) / nf)
    ang = jnp.concatenate([row[:, None] * freqs, col[:, None] * freqs], axis=-1)
    cos = jnp.cos(ang)
    sin = jnp.sin(ang)
    return jnp.concatenate([cos, cos], axis=-1), jnp.concatenate([-sin, sin], axis=-1)


def _decay_table(n):
    heads = jnp.arange(RET_H, dtype=_F32)
    lgf = jnp.log1p(-jnp.exp2(-(RET_DECAY_OFFSET_FWD + heads)))
    lgb = jnp.log1p(-jnp.exp2(-(RET_DECAY_OFFSET_BWD + heads)))
    tab = jnp.stack([lgf, lgb, jnp.exp(lgf), jnp.exp(n * lgb), jnp.exp((n - 1) * lgf)], axis=1)
    tab = jnp.concatenate([tab.reshape(RET_H * DEC_ROWS), jnp.zeros((4,), _F32)])
    return jnp.broadcast_to(tab[:, None], (RET_H * DEC_ROWS + 4, LANES))


def kernel(x_prompt, x_sample, c, state_lru, state_ret, c_ctx, ada_w, ada_b, norm_mix_pre, norm_mix_post, norm_ffn_pre, norm_ffn_post, w_in, conv_w, conv_b, lru_wa, lru_ba, lru_wi, lru_bi, lru_lambda, w_out, router_w, exp_w_gate, exp_w_up, exp_w_down):
    bp, n_p, _ = x_prompt.shape
    bs, n_s, _ = x_sample.shape
    cap_p = 2 * n_p // N_EXP
    cap_s = 2 * n_s // N_EXP
    l = 0

    c_all = jnp.concatenate([c_ctx[None, :], c, jnp.zeros((8 - 1 - bs, D), _F32)], axis=0)
    mod = _ada_call(c_all, ada_w[l], ada_b[l][None, :]).reshape(8, N_MOD, D)

    nvec = jnp.concatenate([norm_mix_pre[l][None], norm_mix_post[l][None], norm_ffn_pre[l][None],
                            norm_ffn_post[l][None], jnp.zeros((4, D), _F32)], axis=0)
    lvec = jnp.concatenate([
        conv_w[l], conv_b[l][None],
        lru_ba[l, 0][None], lru_bi[l, 0][None], lru_lambda[l, 0][None],
        lru_ba[l, 1][None], lru_bi[l, 1][None], lru_lambda[l, 1][None],
        jnp.zeros((5, LRU_W), _F32)], axis=0)
    wg = _block_diag_gates(lru_wa[l], lru_wi[l])
    rw = jnp.pad(router_w[l], ((0, 0), (0, LANES - N_EXP)))
    cos2, sin2 = _rope_tables(n_s)

    x1_p, h2_p, lt_p, st_lru, st_ret = _mixer_call(
        x_prompt, mod, nvec, w_in[l], lvec, wg, _decay_table(n_p), w_out[l], rw, latent=False)
    x1_s, h2_s, lt_s = _mixer_call(
        x_sample, mod, nvec, w_in[l], lvec, wg, _decay_table(n_s), w_out[l], rw, latent=True,
        extra=(state_lru[:, l], state_ret[:, l], cos2, sin2))

    pos_p, gate_p, pos_s, gate_s = _route_call(lt_p, lt_s, cap_p, cap_s)
    pos_p = pos_p.reshape(bp, N_EXP, n_p)
    gate_p = gate_p.reshape(bp, N_EXP, n_p)
    pos_s = pos_s.reshape(bs, N_EXP, n_s)
    gate_s = gate_s.reshape(bs, N_EXP, n_s)

    xs_p, gsl_p = _dispatch_call(pos_p, gate_p, h2_p, cap_p)
    xs_s, gsl_s = _dispatch_call(pos_s, gate_s, h2_s, cap_s)
    ye = _expert_call(xs_p, xs_s, gsl_p, gsl_s, exp_w_gate[l], exp_w_up[l], exp_w_down[l])

    post_p = jnp.swapaxes(pos_p, 1, 2).astype(_BF)
    post_s = jnp.swapaxes(pos_s, 1, 2).astype(_BF)
    y_p = _combine_call(post_p, ye, x1_p, mod, nvec, cap_p, 0, latent=False)
    y_s = _combine_call(post_s, ye, x1_s, mod, nvec, cap_s, bp * cap_p, latent=True)
    return (y_p, y_s, st_lru, st_ret)
```

```python
import functools
import math
import types

import jax
import jax.numpy as jnp
from jax import lax
from jax.experimental import pallas as pl
from jax.experimental.pallas import tpu as pltpu

D = 1024
LRU_W = 512
LRU_HEADS = 8
LRU_HD = 64
LRU_C = 8.0
RET_W = 512
RET_H = 4
DK = 128
N_EXP = 16
FF = 2048
N_MOD = 6
EPS = 1e-6
GRID_W = 64
ROPE_BASE = 10000.0
RET_DECAY_OFFSET_FWD = 5.0
RET_DECAY_OFFSET_BWD = 5.5

ROW_TILE = 256
FF_CHUNK = 512
COMBINE_TILE = 512
CONTEXT_GROUP = 2
SMALL_STEP_TOKENS = 1024
SUBLANES = 8
LANES = 128
VMEM_LIMIT = 60 * 1024 * 1024

DEC_LOG_F, DEC_LOG_B, DEC_G_F, DEC_GN_B, DEC_GN1_F, DEC_ROWS = 0, 1, 2, 3, 4, 5

_BF = jnp.bfloat16
_F32 = jnp.float32


def _sigmoid(x):
    return 0.5 * jnp.tanh(0.5 * x) + 0.5


def _silu(x):
    return x * _sigmoid(x)


def _gelu_tanh(x):
    c = math.sqrt(2.0 / math.pi)
    return 0.5 * x * (1.0 + jnp.tanh(c * (x + 0.044715 * (x * x * x))))


def _rms(x, gain):
    return x * lax.rsqrt(jnp.mean(x * x, axis=-1, keepdims=True) + EPS) * gain


def _dot(a, b):
    return jnp.dot(a, b, preferred_element_type=_F32)


def _dot_nt(a, b):
    return lax.dot_general(a, b, (((1,), (1,)), ((), ())), preferred_element_type=_F32)


def _dot_tn(a, b):
    return lax.dot_general(a, b, (((0,), (0,)), ((), ())), preferred_element_type=_F32)


def _ada_kernel(c_ref, w_ref, b_ref, o_ref):
    o_ref[...] = _dot(_silu(c_ref[...]), w_ref[...]) + b_ref[...]


def _ada_call(c_all, ada_w, ada_b):
    tn = 1536
    n_out = ada_w.shape[1]
    return pl.pallas_call(
        _ada_kernel,
        out_shape=jax.ShapeDtypeStruct((8, n_out), _F32),
        grid=(n_out // tn,),
        in_specs=[
            pl.BlockSpec((8, D), lambda j: (0, 0)),
            pl.BlockSpec((D, tn), lambda j: (0, j)),
            pl.BlockSpec((1, tn), lambda j: (0, j)),
        ],
        out_specs=pl.BlockSpec((8, tn), lambda j: (0, j)),
        compiler_params=pltpu.CompilerParams(
            dimension_semantics=("arbitrary",), vmem_limit_bytes=VMEM_LIMIT),
        name="ada_mod",
    )(c_all, ada_w, ada_b)


def _tile_loop(nt, body, unroll=1):
    if nt == 1:
        body(0, 0)
    else:
        lax.fori_loop(0, nt, body, 0, unroll=unroll)


def _mixer_kernel(*refs, n, g, latent):
    if latent:
        (x_ref, mod_ref, nv_ref, wl_ref, wqk_ref, wvg_ref, lv_ref, wg_ref, dec_ref, wout_ref, rw_ref,
         h0_ref, s0_ref, cos_ref, sin_ref,
         x1_ref, h2_ref, lt_ref,
         xlp_g, gy_g, af_g, ab_g, hf_g, hb_g) = refs
        sg_g, qf_g, qb_g, kf_g, kb_g, v_g = xlp_g, af_g, ab_g, hf_g, hb_g, gy_g
    else:
        (x_ref, mod_ref, nv_ref, wl_ref, wqk_ref, wvg_ref, lv_ref, wg_ref, dec_ref, wout_ref, rw_ref,
         x1_ref, h2_ref, lt_ref, stl_ref, str_ref,
         xlp_g, gy_g, af_g, ab_g, hf_g, hb_g, sg_g, qf_g, qb_g, kf_g, kb_g, v_g) = refs
    early_ret = not latent

    seqs = []
    for s in range(g):
        q = types.SimpleNamespace(
            x=x_ref.at[s], x1=x1_ref.at[s], h2=h2_ref.at[s], lt=lt_ref.at[s], mix=x1_ref.at[s],
            xlp=xlp_g.at[s], sg=sg_g.at[s], gy=gy_g.at[s], v=v_g.at[s],
            a_f=af_g.at[s], a_b=ab_g.at[s], h_f=hf_g.at[s], h_b=hb_g.at[s],
            q_f=qf_g.at[s], q_b=qb_g.at[s], k_f=kf_g.at[s], k_b=kb_g.at[s])
        if latent:
            q.h0, q.s0 = h0_ref.at[s], s0_ref.at[s]
        else:
            q.stl, q.str = stl_ref.at[s], str_ref.at[s]
        seqs.append(q)

    nt = n // ROW_TILE
    shift = mod_ref[0:1, :]
    scale = nv_ref[0:1, :] * (1.0 + mod_ref[1:2, :])

    def normed(q, rows):
        x = q.x[rows, :]
        return x * lax.rsqrt(jnp.mean(x * x, axis=-1, keepdims=True) + EPS) * scale + shift

    def project_retention(q, r0, rows, h):
        pqk = _dot(h, wqk_ref[...])
        pvg = _dot(h, wvg_ref[...])
        q.sg[rows, :] = _silu(pvg[:, RET_W:])
        tpos = (r0 + lax.broadcasted_iota(jnp.int32, (ROW_TILE, DK), 0)).astype(_F32)
        if latent:
            cos2 = cos_ref[rows, :]
            sin2 = sin_ref[rows, :]
        for hd in range(RET_H):
            cols = slice(hd * DK, (hd + 1) * DK)
            qh = pqk[:, cols] * (DK ** -0.5)
            kh = pqk[:, RET_W + hd * DK:RET_W + (hd + 1) * DK]
            if latent:
                qh = qh * cos2 + pltpu.roll(qh, DK // 2, axis=1) * sin2
                kh = kh * cos2 + pltpu.roll(kh, DK // 2, axis=1) * sin2
            lgf = dec_ref[hd * DEC_ROWS + DEC_LOG_F:hd * DEC_ROWS + DEC_LOG_F + 1, :]
            lgb = dec_ref[hd * DEC_ROWS + DEC_LOG_B:hd * DEC_ROWS + DEC_LOG_B + 1, :]
            q.q_f[rows, cols] = qh * jnp.exp(tpos * lgf)
            q.k_f[rows, cols] = kh * jnp.exp(tpos * (-lgf))
            q.q_b[rows, cols] = qh * jnp.exp(tpos * (-lgb))
            q.k_b[rows, cols] = kh * jnp.exp(tpos * lgb)
        q.v[rows, :] = pvg[:, 0:RET_W]

    def phase_a1(q):
        def body(r, carry):
            r0 = pl.multiple_of(r * ROW_TILE, ROW_TILE)
            rows = pl.ds(r0, ROW_TILE)
            h = normed(q, rows)
            p = _dot(h, wl_ref[...])
            q.xlp[pl.ds(r0 + SUBLANES, ROW_TILE), :] = p[:, 0:LRU_W]
            q.gy[rows, :] = _gelu_tanh(p[:, LRU_W:])
            if early_ret:
                project_retention(q, r0, rows, h)
            return carry
        return body

    for q in seqs:
        q.xlp[0:SUBLANES, :] = jnp.zeros((SUBLANES, LRU_W), _F32)
        q.xlp[n + SUBLANES:n + 2 * SUBLANES, :] = jnp.zeros((SUBLANES, LRU_W), _F32)
        _tile_loop(nt, phase_a1(q))

    half = LRU_W // 2

    def softplus_neg(lam):
        z = -lam
        return jnp.maximum(z, 0.0) + jnp.log1p(jnp.exp(-jnp.abs(z)))

    sp = (softplus_neg(lv_ref[7:8, :]), softplus_neg(lv_ref[10:11, :]))

    def phase_b(q):
        def body(r, carry):
            r0 = pl.multiple_of(r * ROW_TILE, ROW_TILE)
            rows = pl.ds(r0, ROW_TILE)
            ext = q.xlp[pl.ds(r0, ROW_TILE + 2 * SUBLANES), :]
            xc = lv_ref[4:5, :]
            for tap in range(4):
                o = SUBLANES - 2 + tap
                xc = xc + ext[o:o + ROW_TILE, :] * lv_ref[tap:tap + 1, :]
            xh = 0.5 * xc
            for d, (a_ref, u_ref) in enumerate(((q.a_f, q.h_f), (q.a_b, q.h_b))):
                bah = 0.5 * lv_ref[5 + 3 * d:6 + 3 * d, :]
                bih = 0.5 * lv_ref[6 + 3 * d:7 + 3 * d, :]
                ch = (-0.5 * LRU_C) * sp[d]
                for hh in range(2):
                    cs = slice(hh * half, (hh + 1) * half)
                    pre = _dot(xc[:, cs], wg_ref[d, hh])
                    t_r = jnp.tanh(pre[:, 0:half] + bah[:, cs])
                    t_i = jnp.tanh(pre[:, half:] + bih[:, cs])
                    log_a = t_r * ch[:, cs] + ch[:, cs]
                    a = jnp.exp(log_a)
                    a_ref[rows, cs] = a
                    om = -jnp.tanh(log_a) * (a * a + 1.0)
                    root = jnp.where(om > 0.0, om * lax.rsqrt(om), 0.0)
                    u_ref[rows, cs] = root * (t_i * xh[:, cs] + xh[:, cs])
            return carry
        return body

    for q in seqs:
        _tile_loop(nt, phase_b(q))

    row_id = lax.broadcasted_iota(jnp.int32, (SUBLANES, LRU_W), 0)

    def scan_group(a, b, shift_of, mask_of):
        for s in (1, 2, 4):
            m = mask_of(s)
            a_s = jnp.where(m, pltpu.roll(a, shift_of(s), axis=0), 1.0)
            b_s = jnp.where(m, pltpu.roll(b, shift_of(s), axis=0), 0.0)
            b = a * b_s + b
            a = a * a_s
        return a, b

    zero_row = jnp.zeros((1, LRU_W), _F32)
    init_f = tuple(q.h0[0:1, :] if latent else zero_row for q in seqs)
    init_b = tuple(q.h0[1:2, :] if latent else zero_row for q in seqs)
    ng = n // SUBLANES

    def fwd_body(i, carry):
        rows = pl.ds(pl.multiple_of(i * SUBLANES, SUBLANES), SUBLANES)
        out = []
        for q, c in zip(seqs, carry):
            a, b = scan_group(q.a_f[rows, :], q.h_f[rows, :], lambda s: s, lambda s: row_id >= s)
            h = a * c + b
            q.h_f[rows, :] = h
            out.append(h[SUBLANES - 1:SUBLANES, :])
        return tuple(out)

    last_f = lax.fori_loop(0, ng, fwd_body, init_f, unroll=4)

    def bwd_body(i, carry):
        rows = pl.ds(pl.multiple_of((ng - 1 - i) * SUBLANES, SUBLANES), SUBLANES)
        out = []
        for q, c in zip(seqs, carry):
            a, b = scan_group(q.a_b[rows, :], q.h_b[rows, :], lambda s: SUBLANES - s,
                              lambda s: row_id < SUBLANES - s)
            h = a * c + b
            q.h_b[rows, :] = h
            out.append(h[0:1, :])
        return tuple(out)

    first_b = lax.fori_loop(0, ng, bwd_body, init_b, unroll=4)

    def lru_out(q):
        def body(r, carry):
            rows = pl.ds(pl.multiple_of(r * ROW_TILE, ROW_TILE), ROW_TILE)
            q.mix[rows, 0:LRU_W] = (q.h_f[rows, :] + q.h_b[rows, :]) * q.gy[rows, :]
            return carry
        return body

    for s, q in enumerate(seqs):
        if not latent:
            q.stl[0:1, :] = last_f[s]
            q.stl[1:2, :] = first_b[s]
        _tile_loop(nt, lru_out(q))

    def phase_a2(q):
        def body(r, carry):
            r0 = pl.multiple_of(r * ROW_TILE, ROW_TILE)
            rows = pl.ds(r0, ROW_TILE)
            project_retention(q, r0, rows, normed(q, rows))
            return carry
        return body

    if not early_ret:
        for q in seqs:
            _tile_loop(nt, phase_a2(q))

    lower = (lax.broadcasted_iota(jnp.int32, (ROW_TILE, ROW_TILE), 0)
             >= lax.broadcasted_iota(jnp.int32, (ROW_TILE, ROW_TILE), 1))
    blocks = [slice(r * ROW_TILE, (r + 1) * ROW_TILE) for r in range(nt)]

    def phase_c(q, hd):
        cols = slice(hd * DK, (hd + 1) * DK)
        dec = lambda row: dec_ref[hd * DEC_ROWS + row:hd * DEC_ROWS + row + 1, :]
        kv_f = [_dot_tn(q.k_f[rows, cols], q.v[rows, cols]) if (r < nt - 1 or not latent) else 0.0
                for r, rows in enumerate(blocks)]
        kv_b = [_dot_tn(q.k_b[rows, cols], q.v[rows, cols]) if (r > 0 or not latent) else 0.0
                for r, rows in enumerate(blocks)]
        if latent:
            run_f = q.s0[0, hd] * dec(DEC_G_F)
            run_b = q.s0[1, hd] * dec(DEC_GN_B)
        else:
            run_f = run_b = None
        before = []
        for r in range(nt):
            before.append(run_f)
            run_f = kv_f[r] if run_f is None else run_f + kv_f[r]
        after = [None] * nt
        for r in reversed(range(nt)):
            after[r] = run_b
            run_b = kv_b[r] if run_b is None else run_b + kv_b[r]

        for r, rows in enumerate(blocks):
            qf = q.q_f[rows, cols]
            qb = q.q_b[rows, cols]
            s = jnp.where(lower, _dot_nt(qf, q.k_f[rows, cols]), _dot_nt(qb, q.k_b[rows, cols]))
            o = _dot(s, q.v[rows, cols])
            if before[r] is not None:
                o = o + _dot(qf, before[r])
            if after[r] is not None:
                o = o + _dot(qb, after[r])
            o = o * lax.rsqrt(jnp.mean(o * o, axis=-1, keepdims=True) + EPS)
            q.mix[rows, LRU_W + cols.start:LRU_W + cols.stop] = o * q.sg[rows, cols]
        if not latent:
            q.str[0, hd] = run_f * dec(DEC_GN1_F)
            q.str[1, hd] = run_b

    for hd in range(RET_H):
        for q in seqs:
            phase_c(q, hd)

    gain1 = mod_ref[2:3, :] * nv_ref[1:2, :]
    gain2 = nv_ref[2:3, :] * (1.0 + mod_ref[4:5, :])
    sh2 = mod_ref[3:4, :]

    def phase_d(q):
        def body(r, carry):
            rows = pl.ds(pl.multiple_of(r * ROW_TILE, ROW_TILE), ROW_TILE)
            mix = _dot(q.mix[rows, :], wout_ref[...])
            x1 = q.x[rows, :] + mix * lax.rsqrt(jnp.mean(mix * mix, axis=-1, keepdims=True) + EPS) * gain1
            q.x1[rows, :] = x1
            h2 = x1 * lax.rsqrt(jnp.mean(x1 * x1, axis=-1, keepdims=True) + EPS) * gain2 + sh2
            q.h2[rows, :] = h2.astype(_BF)
            q.lt[:, rows] = _dot(h2, rw_ref[...]).T[0:N_EXP, :]
            return carry
        return body

    for q in seqs:
        _tile_loop(nt, phase_d(q), unroll=2)


def _mixer_call(x, mod, nvec, w_in, lvec, wg, dec, w_out, rw, latent, extra=()):
    b, n, _ = x.shape
    g = 1 if latent else CONTEXT_GROUP
    const2 = lambda i: (0, 0)
    mod_idx = (lambda i: (i + 1, 0, 0)) if latent else (lambda i: (0, 0, 0))
    in_specs = [
        pl.BlockSpec((g, n, D), lambda i: (i, 0, 0)),
        pl.BlockSpec((None, N_MOD, D), mod_idx),
        pl.BlockSpec((8, D), const2),
        pl.BlockSpec((D, D), lambda i: (0, 0)),
        pl.BlockSpec((D, D), lambda i: (0, 1)),
        pl.BlockSpec((D, D), lambda i: (0, 2)),
        pl.BlockSpec((16, LRU_W), const2),
        pl.BlockSpec((2, 2, LRU_W // 2, LRU_W), lambda i: (0, 0, 0, 0)),
        pl.BlockSpec((RET_H * DEC_ROWS + 4, LANES), const2),
        pl.BlockSpec((D, D), const2),
        pl.BlockSpec((D, LANES), const2),
    ]
    out_shape = [
        jax.ShapeDtypeStruct((b, n, D), _F32),
        jax.ShapeDtypeStruct((b, n, D), _BF),
        jax.ShapeDtypeStruct((b, N_EXP, n), _F32),
    ]
    out_specs = [
        pl.BlockSpec((g, n, D), lambda i: (i, 0, 0)),
        pl.BlockSpec((g, n, D), lambda i: (i, 0, 0)),
        pl.BlockSpec((g, N_EXP, n), lambda i: (i, 0, 0)),
    ]
    if latent:
        in_specs += [
            pl.BlockSpec((g, 2, LRU_W), lambda i: (i, 0, 0)),
            pl.BlockSpec((g, 2, RET_H, DK, DK), lambda i: (i, 0, 0, 0, 0)),
            pl.BlockSpec((n, DK), const2),
            pl.BlockSpec((n, DK), const2),
        ]
    else:
        out_shape += [
            jax.ShapeDtypeStruct((b, 1, 2, LRU_W), _F32),
            jax.ShapeDtypeStruct((b, 1, 2, RET_H, DK, DK), _F32),
        ]
        out_specs += [
            pl.BlockSpec((g, None, 2, LRU_W), lambda i: (i, 0, 0, 0)),
            pl.BlockSpec((g, None, 2, RET_H, DK, DK), lambda i: (i, 0, 0, 0, 0, 0)),
        ]
    f32s = lambda shape: pltpu.VMEM((g,) + shape, _F32)
    scratch = [
        f32s((n + 2 * SUBLANES, LRU_W)),
        f32s((n, LRU_W)),
        f32s((n, LRU_W)), f32s((n, LRU_W)),
        f32s((n, LRU_W)), f32s((n, LRU_W)),
    ]
    if not latent:
        scratch += [f32s((n, RET_W)) for _ in range(6)]
    return pl.pallas_call(
        functools.partial(_mixer_kernel, n=n, g=g, latent=latent),
        out_shape=out_shape,
        grid=(b // g,),
        in_specs=in_specs,
        out_specs=out_specs,
        scratch_shapes=scratch,
        compiler_params=pltpu.CompilerParams(
            dimension_semantics=("arbitrary",), vmem_limit_bytes=VMEM_LIMIT),
        name="mixer_latent" if latent else "mixer_context",
    )(x, mod, nvec, w_in, w_in, w_in, lvec, wg, dec, w_out, rw, *extra)


def _count(mask):
    return jnp.sum(jnp.where(mask, 1.0, 0.0), axis=-1, keepdims=True)


def _probs(l3):
    bsz, _, n = l3.shape
    m = jnp.max(l3, axis=1, keepdims=True)
    e = jnp.exp(l3 - m)
    return (e / jnp.sum(e, axis=1, keepdims=True)).reshape(bsz * N_EXP, n)


def _finish_select(p, bits, thr, cap):
    rows, n = p.shape
    gt = bits > thr
    eq = bits == thr
    need = float(cap) - _count(gt)
    idx = lax.broadcasted_iota(jnp.int32, (rows, n), 1)
    nbits = int(math.log2(n))

    def idx_body(i, j):
        cand = j | (jnp.int32(1) << (nbits - 1 - i))
        return jnp.where(_count(eq & (idx < cand)) < need, cand, j)

    jlast = lax.fori_loop(0, nbits, idx_body, jnp.zeros((rows, 1), jnp.int32))
    sel = gt | (eq & (idx <= jlast))
    before = (lax.broadcasted_iota(jnp.int32, (n, n), 0)
              < lax.broadcasted_iota(jnp.int32, (n, n), 1))
    pos = _dot(jnp.where(sel, 1.0, 0.0).astype(_BF), jnp.where(before, 1.0, 0.0).astype(_BF))
    return jnp.where(sel, pos, -1.0), jnp.where(sel, p, 0.0)


def _route_kernel(lp_ref, ls_ref, pp_ref, gp_ref, ps_ref, gs_ref, *, cap_p, cap_s):
    groups = ((_probs(lp_ref[...]), float(cap_p)), (_probs(ls_ref[...]), float(cap_s)))
    bits = [pltpu.bitcast(p, jnp.int32) for p, _ in groups]

    def val_body(i, thr):
        out = []
        for b, t, (_, capf) in zip(bits, thr, groups):
            cand = t | (jnp.int32(1) << (30 - i))
            out.append(jnp.where(_count(b >= cand) >= capf, cand, t))
        return tuple(out)

    thr = lax.fori_loop(0, 31, val_body,
                        tuple(jnp.zeros((b.shape[0], 1), jnp.int32) for b in bits))
    pos, gate = _finish_select(groups[0][0], bits[0], thr[0], cap_p)
    pp_ref[...] = pos
    gp_ref[...] = gate
    pos, gate = _finish_select(groups[1][0], bits[1], thr[1], cap_s)
    ps_ref[...] = pos
    gs_ref[...] = gate


def _route_call(lt_p, lt_s, cap_p, cap_s):
    bp, _, n_p = lt_p.shape
    bs, _, n_s = lt_s.shape
    shapes = [
        jax.ShapeDtypeStruct((bp * N_EXP, n_p), _F32), jax.ShapeDtypeStruct((bp * N_EXP, n_p), _F32),
        jax.ShapeDtypeStruct((bs * N_EXP, n_s), _F32), jax.ShapeDtypeStruct((bs * N_EXP, n_s), _F32),
    ]
    return pl.pallas_call(
        functools.partial(_route_kernel, cap_p=cap_p, cap_s=cap_s),
        out_shape=shapes,
        compiler_params=pltpu.CompilerParams(vmem_limit_bytes=VMEM_LIMIT),
        name="route_select",
    )(lt_p, lt_s)


def _dispatch_kernel(pos_ref, gate_ref, h_ref, xs_ref, gs_ref, *, n, cap, g):
    slot = lax.broadcasted_iota(jnp.int32, (cap, n), 0).astype(_F32)
    for j in range(g):
        slots = slice(j * cap, (j + 1) * cap)
        parts = []
        for e in range(N_EXP):
            hit = pos_ref[j, e:e + 1, :] == slot
            parts.append(jnp.where(hit, 1.0, 0.0).astype(_BF))
            gs_ref[e, slots, :] = jnp.sum(jnp.where(hit, gate_ref[j, e:e + 1, :], 0.0),
                                          axis=-1, keepdims=True)
        onehot = jnp.concatenate(parts, axis=0)
        xs = _dot(onehot, h_ref[j]).astype(_BF)
        for e in range(N_EXP):
            xs_ref[e, slots, :] = xs[e * cap:(e + 1) * cap, :]


def _dispatch_call(pos, gate, h2, cap):
    b, n, _ = h2.shape
    g = max(1, SMALL_STEP_TOKENS // n)
    return pl.pallas_call(
        functools.partial(_dispatch_kernel, n=n, cap=cap, g=g),
        out_shape=[
            jax.ShapeDtypeStruct((N_EXP, b * cap, D), _BF),
            jax.ShapeDtypeStruct((N_EXP, b * cap, 1), _F32),
        ],
        grid=(b // g,),
        in_specs=[
            pl.BlockSpec((g, N_EXP, n), lambda i: (i, 0, 0)),
            pl.BlockSpec((g, N_EXP, n), lambda i: (i, 0, 0)),
            pl.BlockSpec((g, n, D), lambda i: (i, 0, 0)),
        ],
        out_specs=[
            pl.BlockSpec((N_EXP, g * cap, D), lambda i: (0, i, 0)),
            pl.BlockSpec((N_EXP, g * cap, 1), lambda i: (0, i, 0)),
        ],
        compiler_params=pltpu.CompilerParams(
            dimension_semantics=("arbitrary",), vmem_limit_bytes=VMEM_LIMIT),
        name="dispatch",
    )(pos, gate, h2)


def _expert_kernel(xp_ref, xs_ref, gp_ref, gs_ref, wg_ref, wu_ref, wd_ref, y_ref, xcat, acc,
                   *, sp, nf, tf):
    f = pl.program_id(1)
    xcat[0:sp, :] = xp_ref[...]
    xcat[sp:, :] = xs_ref[...]
    x = xcat[...]
    total = jnp.where(f == 0, 0.0, acc[...])
    for c in range(tf // FF_CHUNK):
        cs = slice(c * FF_CHUNK, (c + 1) * FF_CHUNK)
        hg = _dot(x, wg_ref[:, cs].astype(_BF))
        hu = _dot(x, wu_ref[:, cs].astype(_BF))
        hid = (_silu(hg) * hu).astype(_BF)
        total = total + _dot(hid, wd_ref[cs, :].astype(_BF))
    acc[...] = total
    y_ref[0:sp, :] = (total[0:sp, :] * gp_ref[...]).astype(_BF)
    y_ref[sp:, :] = (total[sp:, :] * gs_ref[...]).astype(_BF)


def _expert_call(xs_p, xs_s, g_p, g_s, w_gate, w_up, w_down):
    tf = 1024
    sp = xs_p.shape[1]
    ss = xs_s.shape[1]
    nf = FF // tf
    return pl.pallas_call(
        functools.partial(_expert_kernel, sp=sp, nf=nf, tf=tf),
        out_shape=jax.ShapeDtypeStruct((N_EXP, sp + ss, D), _BF),
        grid=(N_EXP, nf),
        in_specs=[
            pl.BlockSpec((None, sp, D), lambda e, f: (e, 0, 0)),
            pl.BlockSpec((None, ss, D), lambda e, f: (e, 0, 0)),
            pl.BlockSpec((None, sp, 1), lambda e, f: (e, 0, 0)),
            pl.BlockSpec((None, ss, 1), lambda e, f: (e, 0, 0)),
            pl.BlockSpec((None, D, tf), lambda e, f: (e, 0, f)),
            pl.BlockSpec((None, D, tf), lambda e, f: (e, 0, f)),
            pl.BlockSpec((None, tf, D), lambda e, f: (e, f, 0)),
        ],
        out_specs=pl.BlockSpec((None, sp + ss, D), lambda e, f: (e, 0, 0)),
        scratch_shapes=[pltpu.VMEM((sp + ss, D), _BF), pltpu.VMEM((sp + ss, D), _F32)],
        compiler_params=pltpu.CompilerParams(
            dimension_semantics=("arbitrary", "arbitrary"), vmem_limit_bytes=VMEM_LIMIT),
        name="expert_ffn",
    )(xs_p, xs_s, g_p, g_s, w_gate, w_up, w_down)


def _combine_kernel(post_ref, ye_ref, x1_ref, mod_ref, nv_ref, y_ref, *, n, cap, g):
    width = N_EXP * cap
    lane = lax.broadcasted_iota(jnp.int32, (N_EXP, width), 1)
    expand = jnp.where(lane // cap == lax.broadcasted_iota(jnp.int32, (N_EXP, width), 0),
                       1.0, 0.0).astype(_BF)
    tile = min(n, COMBINE_TILE)
    slot = (lax.broadcasted_iota(jnp.int32, (tile, width), 1) % cap).astype(_F32)
    gain = mod_ref[5:6, :] * nv_ref[3:4, :]

    for j in range(g):
        ye = ye_ref[:, j * cap:(j + 1) * cap, :].reshape(width, D)

        def body(r, carry, j=j, ye=ye):
            rows = pl.ds(pl.multiple_of(r * tile, tile), tile)
            pos_e = _dot(post_ref[j, rows, :], expand)
            onehot = jnp.where(pos_e == slot, 1.0, 0.0).astype(_BF)
            f = _dot(onehot, ye)
            y_ref[j, rows, :] = x1_ref[j, rows, :] + (
                f * lax.rsqrt(jnp.mean(f * f, axis=-1, keepdims=True) + EPS) * gain)
            return carry

        _tile_loop(n // tile, body)


def _combine_call(post, ye, x1, mod, nvec, cap, slot_off, latent):
    b, n, _ = x1.shape
    g = 1 if latent else max(1, SMALL_STEP_TOKENS // n)
    blk_off = slot_off // (g * cap)
    mod_idx = (lambda i: (i + 1, 0, 0)) if latent else (lambda i: (0, 0, 0))
    return pl.pallas_call(
        functools.partial(_combine_kernel, n=n, cap=cap, g=g),
        out_shape=jax.ShapeDtypeStruct((b, n, D), _F32),
        grid=(b // g,),
        in_specs=[
            pl.BlockSpec((g, n, N_EXP), lambda i: (i, 0, 0)),
            pl.BlockSpec((N_EXP, g * cap, D), lambda i: (0, i + blk_off, 0)),
            pl.BlockSpec((g, n, D), lambda i: (i, 0, 0)),
            pl.BlockSpec((None, N_MOD, D), mod_idx),
            pl.BlockSpec((8, D), lambda i: (0, 0)),
        ],
        out_specs=pl.BlockSpec((g, n, D), lambda i: (i, 0, 0)),
        compiler_params=pltpu.CompilerParams(
            dimension_semantics=("arbitrary",), vmem_limit_bytes=VMEM_LIMIT),
        name="combine_latent" if latent else "combine_context",
    )(post, ye, x1, mod, nvec)


def _block_diag_gates(wa, wi):
    per_half = LRU_HEADS // 2
    eye = jnp.eye(per_half, dtype=wa.dtype)

    def bd(w):
        w = w.reshape(2, 2, per_half, LRU_HD, LRU_HD)
        full = jnp.einsum('dghij,hk->dghikj', w, eye)
        return full.reshape(2, 2, per_half * LRU_HD, per_half * LRU_HD)

    return 0.5 * jnp.concatenate([bd(wa), bd(wi)], axis=-1)


def _rope_tables(n):
    rows = n // GRID_W
    row = jnp.repeat(jnp.arange(rows, dtype=_F32), GRID_W)
    col = jnp.tile(jnp.arange(GRID_W, dtype=_F32), rows)
    nf = DK // 4
    freqs = ROPE_BASE ** (-jnp.arange(nf, dtype=_F32) / nf)
    ang = jnp.concatenate([row[:, None] * freqs, col[:, None] * freqs], axis=-1)
    cos = jnp.cos(ang)
    sin = jnp.sin(ang)
    return jnp.concatenate([cos, cos], axis=-1), jnp.concatenate([-sin, sin], axis=-1)


def _decay_table(n):
    heads = jnp.arange(RET_H, dtype=_F32)
    lgf = jnp.log1p(-jnp.exp2(-(RET_DECAY_OFFSET_FWD + heads)))
    lgb = jnp.log1p(-jnp.exp2(-(RET_DECAY_OFFSET_BWD + heads)))
    tab = jnp.stack([lgf, lgb, jnp.exp(lgf), jnp.exp(n * lgb), jnp.exp((n - 1) * lgf)], axis=1)
    tab = jnp.concatenate([tab.reshape(RET_H * DEC_ROWS), jnp.zeros((4,), _F32)])
    return jnp.broadcast_to(tab[:, None], (RET_H * DEC_ROWS + 4, LANES))


def kernel(x_prompt, x_sample, c, state_lru, state_ret, c_ctx, ada_w, ada_b, norm_mix_pre, norm_mix_post, norm_ffn_pre, norm_ffn_post, w_in, conv_w, conv_b, lru_wa, lru_ba, lru_wi, lru_bi, lru_lambda, w_out, router_w, exp_w_gate, exp_w_up, exp_w_down):
    bp, n_p, _ = x_prompt.shape
    bs, n_s, _ = x_sample.shape
    cap_p = 2 * n_p // N_EXP
    cap_s = 2 * n_s // N_EXP
    l = 0

    c_all = jnp.concatenate([c_ctx[None, :], c, jnp.zeros((8 - 1 - bs, D), _F32)], axis=0)
    mod = _ada_call(c_all, ada_w[l], ada_b[l][None, :]).reshape(8, N_MOD, D)

    nvec = jnp.concatenate([norm_mix_pre[l][None], norm_mix_post[l][None], norm_ffn_pre[l][None],
                            norm_ffn_post[l][None], jnp.zeros((4, D), _F32)], axis=0)
    lvec = jnp.concatenate([
        conv_w[l], conv_b[l][None],
        lru_ba[l, 0][None], lru_bi[l, 0][None], lru_lambda[l, 0][None],
        lru_ba[l, 1][None], lru_bi[l, 1][None], lru_lambda[l, 1][None],
        jnp.zeros((5, LRU_W), _F32)], axis=0)
    wg = _block_diag_gates(lru_wa[l], lru_wi[l])
    rw = jnp.pad(router_w[l], ((0, 0), (0, LANES - N_EXP)))
    cos2, sin2 = _rope_tables(n_s)

    x1_p, h2_p, lt_p, st_lru, st_ret = _mixer_call(
        x_prompt, mod, nvec, w_in[l], lvec, wg, _decay_table(n_p), w_out[l], rw, latent=False)
    x1_s, h2_s, lt_s = _mixer_call(
        x_sample, mod, nvec, w_in[l], lvec, wg, _decay_table(n_s), w_out[l], rw, latent=True,
        extra=(state_lru[:, l], state_ret[:, l], cos2, sin2))

    pos_p, gate_p, pos_s, gate_s = _route_call(lt_p, lt_s, cap_p, cap_s)
    pos_p = pos_p.reshape(bp, N_EXP, n_p)
    gate_p = gate_p.reshape(bp, N_EXP, n_p)
    pos_s = pos_s.reshape(bs, N_EXP, n_s)
    gate_s = gate_s.reshape(bs, N_EXP, n_s)

    xs_p, gsl_p = _dispatch_call(pos_p, gate_p, h2_p, cap_p)
    xs_s, gsl_s = _dispatch_call(pos_s, gate_s, h2_s, cap_s)
    ye = _expert_call(xs_p, xs_s, gsl_p, gsl_s, exp_w_gate[l], exp_w_up[l], exp_w_down[l])

    post_p = jnp.swapaxes(pos_p, 1, 2).astype(_BF)
    post_s = jnp.swapaxes(pos_s, 1, 2).astype(_BF)
    y_p = _combine_call(post_p, ye, x1_p, mod, nvec, cap_p, 0, latent=False)
    y_s = _combine_call(post_s, ye, x1_s, mod, nvec, cap_s, bp * cap_p, latent=True)
    return (y_p, y_s, st_lru, st_ret)
```

```python
import functools
import math
import types

import jax
import jax.numpy as jnp
from jax import lax
from jax.experimental import pallas as pl
from jax.experimental.pallas import tpu as pltpu

D = 1024
LRU_W = 512
LRU_HEADS = 8
LRU_HD = 64
LRU_C = 8.0
RET_W = 512
RET_H = 4
DK = 128
N_EXP = 16
FF = 2048
N_MOD = 6
EPS = 1e-6
GRID_W = 64
ROPE_BASE = 10000.0
RET_DECAY_OFFSET_FWD = 5.0
RET_DECAY_OFFSET_BWD = 5.5

ROW_TILE = 256
FF_CHUNK = 512
COMBINE_TILE = 512
CONTEXT_GROUP = 2
SMALL_STEP_TOKENS = 1024
SUBLANES = 8
LANES = 128
VMEM_LIMIT = 60 * 1024 * 1024

DEC_LOG_F, DEC_LOG_B, DEC_G_F, DEC_GN_B, DEC_GN1_F, DEC_ROWS = 0, 1, 2, 3, 4, 5

_BF = jnp.bfloat16
_F32 = jnp.float32


def _sigmoid(x):
    return 0.5 * jnp.tanh(0.5 * x) + 0.5


def _silu(x):
    return x * _sigmoid(x)


def _gelu_tanh(x):
    c = math.sqrt(2.0 / math.pi)
    return 0.5 * x * (1.0 + jnp.tanh(c * (x + 0.044715 * (x * x * x))))


def _rms(x, gain):
    return x * lax.rsqrt(jnp.mean(x * x, axis=-1, keepdims=True) + EPS) * gain


def _dot(a, b):
    return jnp.dot(a, b, preferred_element_type=_F32)


def _dot_nt(a, b):
    return lax.dot_general(a, b, (((1,), (1,)), ((), ())), preferred_element_type=_F32)


def _dot_tn(a, b):
    return lax.dot_general(a, b, (((0,), (0,)), ((), ())), preferred_element_type=_F32)


def _ada_kernel(c_ref, w_ref, b_ref, o_ref):
    o_ref[...] = _dot(_silu(c_ref[...]), w_ref[...]) + b_ref[...]


def _ada_call(c_all, ada_w, ada_b):
    tn = 1536
    n_out = ada_w.shape[1]
    return pl.pallas_call(
        _ada_kernel,
        out_shape=jax.ShapeDtypeStruct((8, n_out), _F32),
        grid=(n_out // tn,),
        in_specs=[
            pl.BlockSpec((8, D), lambda j: (0, 0)),
            pl.BlockSpec((D, tn), lambda j: (0, j)),
            pl.BlockSpec((1, tn), lambda j: (0, j)),
        ],
        out_specs=pl.BlockSpec((8, tn), lambda j: (0, j)),
        compiler_params=pltpu.CompilerParams(
            dimension_semantics=("arbitrary",), vmem_limit_bytes=VMEM_LIMIT),
        name="ada_mod",
    )(c_all, ada_w, ada_b)


def _tile_loop(nt, body, unroll=1):
    if nt == 1:
        body(0, 0)
    else:
        lax.fori_loop(0, nt, body, 0, unroll=unroll)


def _mixer_kernel(*refs, n, g, latent):
    if latent:
        (x_ref, mod_ref, n1_ref, n2_ref, n3_ref, wl_ref, wqk_ref, wvg_ref,
         cw_ref, cb_ref, ba_ref, bi_ref, lam_ref, wg_ref, dec_ref, wout_ref, rw_ref,
         h0_ref, s0_ref, cos_ref, sin_ref,
         x1_ref, h2_ref, lt_ref,
         xlp_g, gy_g, af_g, ab_g, hf_g, hb_g) = refs
        sg_g, qf_g, qb_g, kf_g, kb_g, v_g = xlp_g, af_g, ab_g, hf_g, hb_g, gy_g
    else:
        (x_ref, mod_ref, n1_ref, n2_ref, n3_ref, wl_ref, wqk_ref, wvg_ref,
         cw_ref, cb_ref, ba_ref, bi_ref, lam_ref, wg_ref, dec_ref, wout_ref, rw_ref,
         x1_ref, h2_ref, lt_ref, stl_ref, str_ref,
         xlp_g, gy_g, af_g, ab_g, hf_g, hb_g, sg_g, qf_g, qb_g, kf_g, kb_g, v_g) = refs
    early_ret = not latent

    seqs = []
    for s in range(g):
        q = types.SimpleNamespace(
            x=x_ref.at[s], x1=x1_ref.at[s], h2=h2_ref.at[s], lt=lt_ref.at[s], mix=x1_ref.at[s],
            xlp=xlp_g.at[s], sg=sg_g.at[s], gy=gy_g.at[s], v=v_g.at[s],
            a_f=af_g.at[s], a_b=ab_g.at[s], h_f=hf_g.at[s], h_b=hb_g.at[s],
            q_f=qf_g.at[s], q_b=qb_g.at[s], k_f=kf_g.at[s], k_b=kb_g.at[s])
        if latent:
            q.h0, q.s0 = h0_ref.at[s], s0_ref.at[s]
        else:
            q.stl, q.str = stl_ref.at[s], str_ref.at[s]
        seqs.append(q)

    nt = n // ROW_TILE
    shift = mod_ref[0:1, :]
    scale = n1_ref[...] * (1.0 + mod_ref[1:2, :])

    def normed(q, rows):
        x = q.x[rows, :]
        return x * lax.rsqrt(jnp.mean(x * x, axis=-1, keepdims=True) + EPS) * scale + shift

    def project_retention(q, r0, rows, h):
        pqk = _dot(h, wqk_ref[...])
        pvg = _dot(h, wvg_ref[...])
        q.sg[rows, :] = _silu(pvg[:, RET_W:])
        tpos = (r0 + lax.broadcasted_iota(jnp.int32, (ROW_TILE, DK), 0)).astype(_F32)
        if latent:
            cos2 = cos_ref[rows, :]
            sin2 = sin_ref[rows, :]
        for hd in range(RET_H):
            cols = slice(hd * DK, (hd + 1) * DK)
            qh = pqk[:, cols] * (DK ** -0.5)
            kh = pqk[:, RET_W + hd * DK:RET_W + (hd + 1) * DK]
            if latent:
                qh = qh * cos2 + pltpu.roll(qh, DK // 2, axis=1) * sin2
                kh = kh * cos2 + pltpu.roll(kh, DK // 2, axis=1) * sin2
            lgf = dec_ref[hd * DEC_ROWS + DEC_LOG_F:hd * DEC_ROWS + DEC_LOG_F + 1, :]
            lgb = dec_ref[hd * DEC_ROWS + DEC_LOG_B:hd * DEC_ROWS + DEC_LOG_B + 1, :]
            q.q_f[rows, cols] = qh * jnp.exp(tpos * lgf)
            q.k_f[rows, cols] = kh * jnp.exp(tpos * (-lgf))
            q.q_b[rows, cols] = qh * jnp.exp(tpos * (-lgb))
            q.k_b[rows, cols] = kh * jnp.exp(tpos * lgb)
        q.v[rows, :] = pvg[:, 0:RET_W]

    def phase_a1(q):
        def body(r, carry):
            r0 = pl.multiple_of(r * ROW_TILE, ROW_TILE)
            rows = pl.ds(r0, ROW_TILE)
            h = normed(q, rows)
            p = _dot(h, wl_ref[...])
            q.xlp[pl.ds(r0 + SUBLANES, ROW_TILE), :] = p[:, 0:LRU_W]
            q.gy[rows, :] = _gelu_tanh(p[:, LRU_W:])
            if early_ret:
                project_retention(q, r0, rows, h)
            return carry
        return body

    for q in seqs:
        q.xlp[0:SUBLANES, :] = jnp.zeros((SUBLANES, LRU_W), _F32)
        q.xlp[n + SUBLANES:n + 2 * SUBLANES, :] = jnp.zeros((SUBLANES, LRU_W), _F32)

    half = LRU_W // 2

    def softplus_neg(lam):
        z = -lam
        return jnp.maximum(z, 0.0) + jnp.log1p(jnp.exp(-jnp.abs(z)))

    sp = (softplus_neg(lam_ref[0:1, :]), softplus_neg(lam_ref[1:2, :]))

    def phase_b(q):
        def body(r, carry):
            r0 = pl.multiple_of(r * ROW_TILE, ROW_TILE)
            rows = pl.ds(r0, ROW_TILE)
            ext = q.xlp[pl.ds(r0, ROW_TILE + 2 * SUBLANES), :]
            xc = cb_ref[...]
            for tap in range(4):
                o = SUBLANES - 2 + tap
                xc = xc + ext[o:o + ROW_TILE, :] * cw_ref[tap:tap + 1, :]
            xh = 0.5 * xc
            for d, (a_ref, u_ref) in enumerate(((q.a_f, q.h_f), (q.a_b, q.h_b))):
                bah = 0.5 * ba_ref[d:d + 1, :]
                bih = 0.5 * bi_ref[d:d + 1, :]
                ch = (-0.5 * LRU_C) * sp[d]
                for hh in range(2):
                    cs = slice(hh * half, (hh + 1) * half)
                    pre = _dot(xc[:, cs], wg_ref[d, hh])
                    t_r = jnp.tanh(pre[:, 0:half] + bah[:, cs])
                    t_i = jnp.tanh(pre[:, half:] + bih[:, cs])
                    log_a = t_r * ch[:, cs] + ch[:, cs]
                    a = jnp.exp(log_a)
                    a_ref[rows, cs] = a
                    om = -jnp.tanh(log_a) * (a * a + 1.0)
                    root = jnp.where(om > 0.0, om * lax.rsqrt(om), 0.0)
                    u_ref[rows, cs] = root * (t_i * xh[:, cs] + xh[:, cs])
            return carry
        return body

    for q in seqs:
        _tile_loop(nt, phase_a1(q))
    for q in seqs:
        _tile_loop(nt, phase_b(q))

    row_id = lax.broadcasted_iota(jnp.int32, (SUBLANES, LRU_W), 0)

    def scan_group(a, b, shift_of, mask_of):
        for s in (1, 2, 4):
            m = mask_of(s)
            a_s = jnp.where(m, pltpu.roll(a, shift_of(s), axis=0), 1.0)
            b_s = jnp.where(m, pltpu.roll(b, shift_of(s), axis=0), 0.0)
            b = a * b_s + b
            a = a * a_s
        return a, b

    zero_row = jnp.zeros((1, LRU_W), _F32)
    init_f = tuple(q.h0[0:1, :] if latent else zero_row for q in seqs)
    init_b = tuple(q.h0[1:2, :] if latent else zero_row for q in seqs)
    ng = n // SUBLANES

    def fwd_body(i, carry):
        rows = pl.ds(pl.multiple_of(i * SUBLANES, SUBLANES), SUBLANES)
        out = []
        for q, c in zip(seqs, carry):
            a, b = scan_group(q.a_f[rows, :], q.h_f[rows, :], lambda s: s, lambda s: row_id >= s)
            h = a * c + b
            q.h_f[rows, :] = h
            out.append(h[SUBLANES - 1:SUBLANES, :])
        return tuple(out)

    last_f = lax.fori_loop(0, ng, fwd_body, init_f, unroll=4)

    def bwd_body(i, carry):
        rows = pl.ds(pl.multiple_of((ng - 1 - i) * SUBLANES, SUBLANES), SUBLANES)
        out = []
        for q, c in zip(seqs, carry):
            a, b = scan_group(q.a_b[rows, :], q.h_b[rows, :], lambda s: SUBLANES - s,
                              lambda s: row_id < SUBLANES - s)
            h = a * c + b
            q.h_b[rows, :] = h
            out.append(h[0:1, :])
        return tuple(out)

    first_b = lax.fori_loop(0, ng, bwd_body, init_b, unroll=4)

    def lru_out(q):
        def body(r, carry):
            rows = pl.ds(pl.multiple_of(r * ROW_TILE, ROW_TILE), ROW_TILE)
            q.mix[rows, 0:LRU_W] = (q.h_f[rows, :] + q.h_b[rows, :]) * q.gy[rows, :]
            return carry
        return body

    for s, q in enumerate(seqs):
        if not latent:
            q.stl[0:1, :] = last_f[s]
            q.stl[1:2, :] = first_b[s]
        _tile_loop(nt, lru_out(q))

    def phase_a2(q):
        def body(r, carry):
            r0 = pl.multiple_of(r * ROW_TILE, ROW_TILE)
            rows = pl.ds(r0, ROW_TILE)
            project_retention(q, r0, rows, normed(q, rows))
            return carry
        return body

    if not early_ret:
        for q in seqs:
            _tile_loop(nt, phase_a2(q))

    lower = (lax.broadcasted_iota(jnp.int32, (ROW_TILE, ROW_TILE), 0)
             >= lax.broadcasted_iota(jnp.int32, (ROW_TILE, ROW_TILE), 1))
    blocks = [slice(r * ROW_TILE, (r + 1) * ROW_TILE) for r in range(nt)]

    def phase_c(q, hd):
        cols = slice(hd * DK, (hd + 1) * DK)
        dec = lambda row: dec_ref[hd * DEC_ROWS + row:hd * DEC_ROWS + row + 1, :]
        kv_f = [_dot_tn(q.k_f[rows, cols], q.v[rows, cols]) if (r < nt - 1 or not latent) else 0.0
                for r, rows in enumerate(blocks)]
        kv_b = [_dot_tn(q.k_b[rows, cols], q.v[rows, cols]) if (r > 0 or not latent) else 0.0
                for r, rows in enumerate(blocks)]
        if latent:
            run_f = q.s0[0, hd] * dec(DEC_G_F)
            run_b = q.s0[1, hd] * dec(DEC_GN_B)
        else:
            run_f = run_b = None
        before = []
        for r in range(nt):
            before.append(run_f)
            run_f = kv_f[r] if run_f is None else run_f + kv_f[r]
        after = [None] * nt
        for r in reversed(range(nt)):
            after[r] = run_b
            run_b = kv_b[r] if run_b is None else run_b + kv_b[r]

        for r, rows in enumerate(blocks):
            qf = q.q_f[rows, cols]
            qb = q.q_b[rows, cols]
            s = jnp.where(lower, _dot_nt(qf, q.k_f[rows, cols]), _dot_nt(qb, q.k_b[rows, cols]))
            o = _dot(s, q.v[rows, cols])
            if before[r] is not None:
                o = o + _dot(qf, before[r])
            if after[r] is not None:
                o = o + _dot(qb, after[r])
            o = o * lax.rsqrt(jnp.mean(o * o, axis=-1, keepdims=True) + EPS)
            q.mix[rows, LRU_W + cols.start:LRU_W + cols.stop] = o * q.sg[rows, cols]
        if not latent:
            q.str[0, hd] = run_f * dec(DEC_GN1_F)
            q.str[1, hd] = run_b

    for hd in range(RET_H):
        for q in seqs:
            phase_c(q, hd)

    gain1 = mod_ref[2:3, :] * n2_ref[...]
    gain2 = n3_ref[...] * (1.0 + mod_ref[4:5, :])
    sh2 = mod_ref[3:4, :]

    def phase_d(q):
        def body(r, carry):
            rows = pl.ds(pl.multiple_of(r * ROW_TILE, ROW_TILE), ROW_TILE)
            mix = _dot(q.mix[rows, :], wout_ref[...])
            x1 = q.x[rows, :] + mix * lax.rsqrt(jnp.mean(mix * mix, axis=-1, keepdims=True) + EPS) * gain1
            q.x1[rows, :] = x1
            h2 = x1 * lax.rsqrt(jnp.mean(x1 * x1, axis=-1, keepdims=True) + EPS) * gain2 + sh2
            q.h2[rows, :] = h2.astype(_BF)
            q.lt[:, rows] = _dot(h2, rw_ref[...]).T[0:N_EXP, :]
            return carry
        return body

    for q in seqs:
        _tile_loop(nt, phase_d(q), unroll=2)


def _mixer_call(x, mod, norms, w_in, lru, wg, dec, w_out, rw, latent, extra=()):
    b, n, _ = x.shape
    g = 1 if latent else CONTEXT_GROUP
    const2 = lambda i: (0, 0)
    mod_idx = (lambda i: (i + 1, 0, 0)) if latent else (lambda i: (0, 0, 0))
    in_specs = [
        pl.BlockSpec((g, n, D), lambda i: (i, 0, 0)),
        pl.BlockSpec((None, N_MOD, D), mod_idx),
        pl.BlockSpec((1, D), const2), pl.BlockSpec((1, D), const2), pl.BlockSpec((1, D), const2),
        pl.BlockSpec((D, D), lambda i: (0, 0)),
        pl.BlockSpec((D, D), lambda i: (0, 1)),
        pl.BlockSpec((D, D), lambda i: (0, 2)),
        pl.BlockSpec((4, LRU_W), const2), pl.BlockSpec((1, LRU_W), const2),
        pl.BlockSpec((2, LRU_W), const2), pl.BlockSpec((2, LRU_W), const2),
        pl.BlockSpec((2, LRU_W), const2),
        pl.BlockSpec((2, 2, LRU_W // 2, LRU_W), lambda i: (0, 0, 0, 0)),
        pl.BlockSpec((RET_H * DEC_ROWS + 4, LANES), const2),
        pl.BlockSpec((D, D), const2),
        pl.BlockSpec((D, LANES), const2),
    ]
    out_shape = [
        jax.ShapeDtypeStruct((b, n, D), _F32),
        jax.ShapeDtypeStruct((b, n, D), _BF),
        jax.ShapeDtypeStruct((b, N_EXP, n), _F32),
    ]
    out_specs = [
        pl.BlockSpec((g, n, D), lambda i: (i, 0, 0)),
        pl.BlockSpec((g, n, D), lambda i: (i, 0, 0)),
        pl.BlockSpec((g, N_EXP, n), lambda i: (i, 0, 0)),
    ]
    if latent:
        in_specs += [
            pl.BlockSpec((g, 2, LRU_W), lambda i: (i, 0, 0)),
            pl.BlockSpec((g, 2, RET_H, DK, DK), lambda i: (i, 0, 0, 0, 0)),
            pl.BlockSpec((n, DK), const2),
            pl.BlockSpec((n, DK), const2),
        ]
    else:
        out_shape += [
            jax.ShapeDtypeStruct((b, 1, 2, LRU_W), _F32),
            jax.ShapeDtypeStruct((b, 1, 2, RET_H, DK, DK), _F32),
        ]
        out_specs += [
            pl.BlockSpec((g, None, 2, LRU_W), lambda i: (i, 0, 0, 0)),
            pl.BlockSpec((g, None, 2, RET_H, DK, DK), lambda i: (i, 0, 0, 0, 0, 0)),
        ]
    f32s = lambda shape: pltpu.VMEM((g,) + shape, _F32)
    scratch = [
        f32s((n + 2 * SUBLANES, LRU_W)),
        f32s((n, LRU_W)),
        f32s((n, LRU_W)), f32s((n, LRU_W)),
        f32s((n, LRU_W)), f32s((n, LRU_W)),
    ]
    if not latent:
        scratch += [f32s((n, RET_W)) for _ in range(6)]
    return pl.pallas_call(
        functools.partial(_mixer_kernel, n=n, g=g, latent=latent),
        out_shape=out_shape,
        grid=(b // g,),
        in_specs=in_specs,
        out_specs=out_specs,
        scratch_shapes=scratch,
        compiler_params=pltpu.CompilerParams(
            dimension_semantics=("arbitrary",), vmem_limit_bytes=VMEM_LIMIT),
        name="mixer_latent" if latent else "mixer_context",
    )(x, mod, *norms, w_in, w_in, w_in, *lru, wg, dec, w_out, rw, *extra)


def _count(mask):
    return jnp.sum(jnp.where(mask, 1.0, 0.0), axis=-1, keepdims=True)


def _probs(l3):
    bsz, _, n = l3.shape
    m = jnp.max(l3, axis=1, keepdims=True)
    e = jnp.exp(l3 - m)
    return (e / jnp.sum(e, axis=1, keepdims=True)).reshape(bsz * N_EXP, n)


def _finish_select(p, bits, thr, cap):
    rows, n = p.shape
    gt = bits > thr
    eq = bits == thr
    need = float(cap) - _count(gt)
    idx = lax.broadcasted_iota(jnp.int32, (rows, n), 1)
    nbits = int(math.log2(n))

    def idx_body(i, j):
        cand = j | (jnp.int32(1) << (nbits - 1 - i))
        return jnp.where(_count(eq & (idx < cand)) < need, cand, j)

    jlast = lax.fori_loop(0, nbits, idx_body, jnp.zeros((rows, 1), jnp.int32))
    sel = gt | (eq & (idx <= jlast))
    before = (lax.broadcasted_iota(jnp.int32, (n, n), 0)
              < lax.broadcasted_iota(jnp.int32, (n, n), 1))
    pos = _dot(jnp.where(sel, 1.0, 0.0).astype(_BF), jnp.where(before, 1.0, 0.0).astype(_BF))
    return jnp.where(sel, pos, -1.0), jnp.where(sel, p, 0.0)


def _route_kernel(lp_ref, ls_ref, pp_ref, gp_ref, ps_ref, gs_ref, *, cap_p, cap_s):
    groups = ((_probs(lp_ref[...]), float(cap_p)), (_probs(ls_ref[...]), float(cap_s)))
    bits = [pltpu.bitcast(p, jnp.int32) for p, _ in groups]

    def val_body(i, thr):
        out = []
        for b, t, (_, capf) in zip(bits, thr, groups):
            cand = t | (jnp.int32(1) << (30 - i))
            out.append(jnp.where(_count(b >= cand) >= capf, cand, t))
        return tuple(out)

    thr = lax.fori_loop(0, 31, val_body,
                        tuple(jnp.zeros((b.shape[0], 1), jnp.int32) for b in bits))
    pos, gate = _finish_select(groups[0][0], bits[0], thr[0], cap_p)
    pp_ref[...] = pos.reshape(pp_ref.shape)
    gp_ref[...] = gate.reshape(gp_ref.shape)
    pos, gate = _finish_select(groups[1][0], bits[1], thr[1], cap_s)
    ps_ref[...] = pos.reshape(ps_ref.shape)
    gs_ref[...] = gate.reshape(gs_ref.shape)


def _route_call(lt_p, lt_s, cap_p, cap_s):
    shapes = [
        jax.ShapeDtypeStruct(lt_p.shape, _F32), jax.ShapeDtypeStruct(lt_p.shape, _F32),
        jax.ShapeDtypeStruct(lt_s.shape, _F32), jax.ShapeDtypeStruct(lt_s.shape, _F32),
    ]
    return pl.pallas_call(
        functools.partial(_route_kernel, cap_p=cap_p, cap_s=cap_s),
        out_shape=shapes,
        compiler_params=pltpu.CompilerParams(vmem_limit_bytes=VMEM_LIMIT),
        name="route_select",
    )(lt_p, lt_s)


def _dispatch_kernel(pos_ref, gate_ref, h_ref, xs_ref, gs_ref, *, n, cap, g):
    slot = lax.broadcasted_iota(jnp.int32, (cap, n), 0).astype(_F32)
    for j in range(g):
        slots = slice(j * cap, (j + 1) * cap)
        parts = []
        for e in range(N_EXP):
            hit = pos_ref[j, e:e + 1, :] == slot
            parts.append(jnp.where(hit, 1.0, 0.0).astype(_BF))
            gs_ref[e, slots, :] = jnp.sum(jnp.where(hit, gate_ref[j, e:e + 1, :], 0.0),
                                          axis=-1, keepdims=True)
        onehot = jnp.concatenate(parts, axis=0)
        xs = _dot(onehot, h_ref[j]).astype(_BF)
        for e in range(N_EXP):
            xs_ref[e, slots, :] = xs[e * cap:(e + 1) * cap, :]


def _dispatch_call(pos, gate, h2, cap):
    b, n, _ = h2.shape
    g = max(1, SMALL_STEP_TOKENS // n)
    return pl.pallas_call(
        functools.partial(_dispatch_kernel, n=n, cap=cap, g=g),
        out_shape=[
            jax.ShapeDtypeStruct((N_EXP, b * cap, D), _BF),
            jax.ShapeDtypeStruct((N_EXP, b * cap, 1), _F32),
        ],
        grid=(b // g,),
        in_specs=[
            pl.BlockSpec((g, N_EXP, n), lambda i: (i, 0, 0)),
            pl.BlockSpec((g, N_EXP, n), lambda i: (i, 0, 0)),
            pl.BlockSpec((g, n, D), lambda i: (i, 0, 0)),
        ],
        out_specs=[
            pl.BlockSpec((N_EXP, g * cap, D), lambda i: (0, i, 0)),
            pl.BlockSpec((N_EXP, g * cap, 1), lambda i: (0, i, 0)),
        ],
        compiler_params=pltpu.CompilerParams(
            dimension_semantics=("arbitrary",), vmem_limit_bytes=VMEM_LIMIT),
        name="dispatch",
    )(pos, gate, h2)


def _expert_kernel(xp_ref, xs_ref, gp_ref, gs_ref, wg_ref, wu_ref, wd_ref, y_ref, xcat, acc,
                   *, sp, nf, tf):
    f = pl.program_id(1)
    xcat[0:sp, :] = xp_ref[...]
    xcat[sp:, :] = xs_ref[...]
    x = xcat[...]
    total = jnp.where(f == 0, 0.0, acc[...])
    for c in range(tf // FF_CHUNK):
        cs = slice(c * FF_CHUNK, (c + 1) * FF_CHUNK)
        hg = _dot(x, wg_ref[:, cs].astype(_BF))
        hu = _dot(x, wu_ref[:, cs].astype(_BF))
        hid = (_silu(hg) * hu).astype(_BF)
        total = total + _dot(hid, wd_ref[cs, :].astype(_BF))
    acc[...] = total
    y_ref[0:sp, :] = (total[0:sp, :] * gp_ref[...]).astype(_BF)
    y_ref[sp:, :] = (total[sp:, :] * gs_ref[...]).astype(_BF)


def _expert_call(xs_p, xs_s, g_p, g_s, w_gate, w_up, w_down):
    tf = 1024
    sp = xs_p.shape[1]
    ss = xs_s.shape[1]
    nf = FF // tf
    return pl.pallas_call(
        functools.partial(_expert_kernel, sp=sp, nf=nf, tf=tf),
        out_shape=jax.ShapeDtypeStruct((N_EXP, sp + ss, D), _BF),
        grid=(N_EXP, nf),
        in_specs=[
            pl.BlockSpec((None, sp, D), lambda e, f: (e, 0, 0)),
            pl.BlockSpec((None, ss, D), lambda e, f: (e, 0, 0)),
            pl.BlockSpec((None, sp, 1), lambda e, f: (e, 0, 0)),
            pl.BlockSpec((None, ss, 1), lambda e, f: (e, 0, 0)),
            pl.BlockSpec((None, D, tf), lambda e, f: (e, 0, f)),
            pl.BlockSpec((None, D, tf), lambda e, f: (e, 0, f)),
            pl.BlockSpec((None, tf, D), lambda e, f: (e, f, 0)),
        ],
        out_specs=pl.BlockSpec((None, sp + ss, D), lambda e, f: (e, 0, 0)),
        scratch_shapes=[pltpu.VMEM((sp + ss, D), _BF), pltpu.VMEM((sp + ss, D), _F32)],
        compiler_params=pltpu.CompilerParams(
            dimension_semantics=("arbitrary", "arbitrary"), vmem_limit_bytes=VMEM_LIMIT),
        name="expert_ffn",
    )(xs_p, xs_s, g_p, g_s, w_gate, w_up, w_down)


def _combine_kernel(pos_ref, ye_ref, x1_ref, mod_ref, n4_ref, y_ref, *, n, cap, g):
    width = N_EXP * cap
    lane = lax.broadcasted_iota(jnp.int32, (N_EXP, width), 1)
    expand = jnp.where(lane // cap == lax.broadcasted_iota(jnp.int32, (N_EXP, width), 0),
                       1.0, 0.0).astype(_BF)
    tile = min(n, COMBINE_TILE)
    slot = (lax.broadcasted_iota(jnp.int32, (tile, width), 1) % cap).astype(_F32)
    gain = mod_ref[5:6, :] * n4_ref[...]

    for j in range(g):
        ye = ye_ref[:, j * cap:(j + 1) * cap, :].reshape(width, D)

        def body(r, carry, j=j, ye=ye):
            rows = pl.ds(pl.multiple_of(r * tile, tile), tile)
            pos_e = _dot_tn(pos_ref[j, :, rows].astype(_BF), expand)
            onehot = jnp.where(pos_e == slot, 1.0, 0.0).astype(_BF)
            f = _dot(onehot, ye)
            y_ref[j, rows, :] = x1_ref[j, rows, :] + (
                f * lax.rsqrt(jnp.mean(f * f, axis=-1, keepdims=True) + EPS) * gain)
            return carry

        _tile_loop(n // tile, body)


def _combine_call(pos, ye, x1, mod, norm_post, cap, slot_off, latent):
    b, n, _ = x1.shape
    g = max(1, SMALL_STEP_TOKENS // n)
    blk_off = slot_off // (g * cap)
    mod_idx = (lambda i: (i + 1, 0, 0)) if latent else (lambda i: (0, 0, 0))
    return pl.pallas_call(
        functools.partial(_combine_kernel, n=n, cap=cap, g=g),
        out_shape=jax.ShapeDtypeStruct((b, n, D), _F32),
        grid=(b // g,),
        in_specs=[
            pl.BlockSpec((g, N_EXP, n), lambda i: (i, 0, 0)),
            pl.BlockSpec((N_EXP, g * cap, D), lambda i: (0, i + blk_off, 0)),
            pl.BlockSpec((g, n, D), lambda i: (i, 0, 0)),
            pl.BlockSpec((None, N_MOD, D), mod_idx),
            pl.BlockSpec((1, D), lambda i: (0, 0)),
        ],
        out_specs=pl.BlockSpec((g, n, D), lambda i: (i, 0, 0)),
        compiler_params=pltpu.CompilerParams(
            dimension_semantics=("arbitrary",), vmem_limit_bytes=VMEM_LIMIT),
        name="combine_latent" if latent else "combine_context",
    )(pos, ye, x1, mod, norm_post)


def _block_diag_gates(wa, wi):
    per_half = LRU_HEADS // 2
    eye = jnp.eye(per_half, dtype=wa.dtype)

    def bd(w):
        w = w.reshape(2, 2, per_half, LRU_HD, LRU_HD)
        full = jnp.einsum('dghij,hk->dghikj', w, eye)
        return full.reshape(2, 2, per_half * LRU_HD, per_half * LRU_HD)

    return 0.5 * jnp.concatenate([bd(wa), bd(wi)], axis=-1)


def _rope_tables(n):
    rows = n // GRID_W
    row = jnp.repeat(jnp.arange(rows, dtype=_F32), GRID_W)
    col = jnp.tile(jnp.arange(GRID_W, dtype=_F32), rows)
    nf = DK // 4
    freqs = ROPE_BASE ** (-jnp.arange(nf, dtype=_F32) / nf)
    ang = jnp.concatenate([row[:, None] * freqs, col[:, None] * freqs], axis=-1)
    cos = jnp.cos(ang)
    sin = jnp.sin(ang)
    return jnp.concatenate([cos, cos], axis=-1), jnp.concatenate([-sin, sin], axis=-1)


def _decay_table(n):
    heads = jnp.arange(RET_H, dtype=_F32)
    lgf = jnp.log1p(-jnp.exp2(-(RET_DECAY_OFFSET_FWD + heads)))
    lgb = jnp.log1p(-jnp.exp2(-(RET_DECAY_OFFSET_BWD + heads)))
    tab = jnp.stack([lgf, lgb, jnp.exp(lgf), jnp.exp(n * lgb), jnp.exp((n - 1) * lgf)], axis=1)
    tab = jnp.concatenate([tab.reshape(RET_H * DEC_ROWS), jnp.zeros((4,), _F32)])
    return jnp.broadcast_to(tab[:, None], (RET_H * DEC_ROWS + 4, LANES))


def kernel(x_prompt, x_sample, c, state_lru, state_ret, c_ctx, ada_w, ada_b, norm_mix_pre, norm_mix_post, norm_ffn_pre, norm_ffn_post, w_in, conv_w, conv_b, lru_wa, lru_ba, lru_wi, lru_bi, lru_lambda, w_out, router_w, exp_w_gate, exp_w_up, exp_w_down):
    bp, n_p, _ = x_prompt.shape
    bs, n_s, _ = x_sample.shape
    cap_p = 2 * n_p // N_EXP
    cap_s = 2 * n_s // N_EXP
    l = 0

    c_all = jnp.concatenate([c_ctx[None, :], c, jnp.zeros((8 - 1 - bs, D), _F32)], axis=0)
    mod = _ada_call(c_all, ada_w[l], ada_b[l][None, :]).reshape(8, N_MOD, D)

    norms = (norm_mix_pre[l][None], norm_mix_post[l][None], norm_ffn_pre[l][None])
    lru = (conv_w[l], conv_b[l][None], lru_ba[l], lru_bi[l], lru_lambda[l])
    wg = _block_diag_gates(lru_wa[l], lru_wi[l])
    rw = jnp.pad(router_w[l], ((0, 0), (0, LANES - N_EXP)))
    cos2, sin2 = _rope_tables(n_s)

    x1_p, h2_p, lt_p, st_lru, st_ret = _mixer_call(
        x_prompt, mod, norms, w_in[l], lru, wg, _decay_table(n_p), w_out[l], rw, latent=False)
    x1_s, h2_s, lt_s = _mixer_call(
        x_sample, mod, norms, w_in[l], lru, wg, _decay_table(n_s), w_out[l], rw, latent=True,
        extra=(state_lru[:, l], state_ret[:, l], cos2, sin2))

    pos_p, gate_p, pos_s, gate_s = _route_call(lt_p, lt_s, cap_p, cap_s)
    xs_p, gsl_p = _dispatch_call(pos_p, gate_p, h2_p, cap_p)
    xs_s, gsl_s = _dispatch_call(pos_s, gate_s, h2_s, cap_s)
    ye = _expert_call(xs_p, xs_s, gsl_p, gsl_s, exp_w_gate[l], exp_w_up[l], exp_w_down[l])

    norm_post = norm_ffn_post[l][None]
    y_p = _combine_call(pos_p, ye, x1_p, mod, norm_post, cap_p, 0, latent=False)
    y_s = _combine_call(pos_s, ye, x1_s, mod, norm_post, cap_s, bp * cap_p, latent=True)
    return (y_p, y_s, st_lru, st_ret)
```

```python
import functools
import math
import types

import jax
import jax.numpy as jnp
import numpy as np
from jax import lax
from jax.experimental import pallas as pl
from jax.experimental.pallas import tpu as pltpu

D = 1024
LRU_W = 512
LRU_HEADS = 8
LRU_HD = 64
LRU_C = 8.0
RET_W = 512
RET_H = 4
DK = 128
N_EXP = 16
FF = 2048
N_MOD = 6
EPS = 1e-6
GRID_W = 64
ROPE_BASE = 10000.0
RET_DECAY_OFFSET_FWD = 5.0
RET_DECAY_OFFSET_BWD = 5.5

ROW_TILE = 256
FF_CHUNK = 512
COMBINE_TILE = 512
CONTEXT_GROUP = 2
SMALL_STEP_TOKENS = 1024
SUBLANES = 8
LANES = 128
VMEM_LIMIT = 60 * 1024 * 1024

DEC_LOG_F, DEC_LOG_B, DEC_G_F, DEC_GN_B, DEC_GN1_F, DEC_ROWS = 0, 1, 2, 3, 4, 5

_BF = jnp.bfloat16
_F32 = jnp.float32


def _sigmoid(x):
    return 0.5 * jnp.tanh(0.5 * x) + 0.5


def _silu(x):
    return x * _sigmoid(x)


def _gelu_tanh(x):
    c = math.sqrt(2.0 / math.pi)
    return 0.5 * x * (1.0 + jnp.tanh(c * (x + 0.044715 * (x * x * x))))


def _rms(x, gain):
    return x * lax.rsqrt(jnp.mean(x * x, axis=-1, keepdims=True) + EPS) * gain


def _dot(a, b):
    return jnp.dot(a, b, preferred_element_type=_F32)


def _dot_nt(a, b):
    return lax.dot_general(a, b, (((1,), (1,)), ((), ())), preferred_element_type=_F32)


def _dot_tn(a, b):
    return lax.dot_general(a, b, (((0,), (0,)), ((), ())), preferred_element_type=_F32)


def _ada_kernel(c_ref, w_ref, b_ref, o_ref):
    o_ref[...] = _dot(_silu(c_ref[...]), w_ref[...]) + b_ref[...]


def _ada_call(c_all, ada_w, ada_b):
    return pl.pallas_call(
        _ada_kernel,
        out_shape=jax.ShapeDtypeStruct((N_MOD, 8, D), _F32),
        grid=(N_MOD,),
        in_specs=[
            pl.BlockSpec((8, D), lambda j: (0, 0)),
            pl.BlockSpec((D, D), lambda j: (0, j)),
            pl.BlockSpec((1, D), lambda j: (0, j)),
        ],
        out_specs=pl.BlockSpec((None, 8, D), lambda j: (j, 0, 0)),
        compiler_params=pltpu.CompilerParams(
            dimension_semantics=("arbitrary",), vmem_limit_bytes=VMEM_LIMIT),
        name="ada_mod",
    )(c_all, ada_w, ada_b)


def _tile_loop(nt, body, unroll=1):
    if nt == 1:
        body(0, 0)
    else:
        lax.fori_loop(0, nt, body, 0, unroll=unroll)


def _mixer_kernel(*refs, n, g, latent):
    if latent:
        (x_ref, mod_ref, n1_ref, n2_ref, n3_ref, wl_ref, wqk_ref, wvg_ref,
         cw_ref, cb_ref, ba_ref, bi_ref, lam_ref, wg_ref, dec_ref, wout_ref, rw_ref,
         h0_ref, s0_ref, cos_ref, sin_ref,
         x1_ref, h2_ref, lt_ref,
         xlp_g, gy_g, af_g, ab_g, hf_g, hb_g) = refs
        sg_g, qf_g, qb_g, kf_g, kb_g, v_g = xlp_g, af_g, ab_g, hf_g, hb_g, gy_g
    else:
        (x_ref, mod_ref, n1_ref, n2_ref, n3_ref, wl_ref, wqk_ref, wvg_ref,
         cw_ref, cb_ref, ba_ref, bi_ref, lam_ref, wg_ref, dec_ref, wout_ref, rw_ref,
         x1_ref, h2_ref, lt_ref, stl_ref, str_ref,
         xlp_g, gy_g, af_g, ab_g, hf_g, hb_g, sg_g, qf_g, qb_g, kf_g, kb_g, v_g) = refs
    early_ret = not latent

    seqs = []
    for s in range(g):
        q = types.SimpleNamespace(
            x=x_ref.at[s], x1=x1_ref.at[s], h2=h2_ref.at[s], lt=lt_ref.at[s], mix=x1_ref.at[s],
            xlp=xlp_g.at[s], sg=sg_g.at[s], gy=gy_g.at[s], v=v_g.at[s],
            a_f=af_g.at[s], a_b=ab_g.at[s], h_f=hf_g.at[s], h_b=hb_g.at[s],
            q_f=qf_g.at[s], q_b=qb_g.at[s], k_f=kf_g.at[s], k_b=kb_g.at[s])
        if latent:
            q.h0, q.s0 = h0_ref.at[s], s0_ref.at[s]
        else:
            q.stl, q.str = stl_ref.at[s], str_ref.at[s]
        seqs.append(q)

    nt = n // ROW_TILE
    mod_row = (pl.program_id(0) + 1) if latent else 0
    mod = lambda k: mod_ref[k, pl.ds(mod_row, 1), :]
    shift = mod(0)
    scale = n1_ref[...] * (1.0 + mod(1))

    def normed(q, rows):
        x = q.x[rows, :]
        return x * lax.rsqrt(jnp.mean(x * x, axis=-1, keepdims=True) + EPS) * scale + shift

    def project_retention(q, r0, rows, h):
        pqk = _dot(h, wqk_ref[...])
        pvg = _dot(h, wvg_ref[...])
        q.sg[rows, :] = _silu(pvg[:, RET_W:])
        tpos = (r0 + lax.broadcasted_iota(jnp.int32, (ROW_TILE, DK), 0)).astype(_F32)
        if latent:
            cos2 = cos_ref[rows, :]
            sin2 = sin_ref[rows, :]
        for hd in range(RET_H):
            cols = slice(hd * DK, (hd + 1) * DK)
            qh = pqk[:, cols] * (DK ** -0.5)
            kh = pqk[:, RET_W + hd * DK:RET_W + (hd + 1) * DK]
            if latent:
                qh = qh * cos2 + pltpu.roll(qh, DK // 2, axis=1) * sin2
                kh = kh * cos2 + pltpu.roll(kh, DK // 2, axis=1) * sin2
            lgf = dec_ref[hd * DEC_ROWS + DEC_LOG_F:hd * DEC_ROWS + DEC_LOG_F + 1, :]
            lgb = dec_ref[hd * DEC_ROWS + DEC_LOG_B:hd * DEC_ROWS + DEC_LOG_B + 1, :]
            q.q_f[rows, cols] = qh * jnp.exp(tpos * lgf)
            q.k_f[rows, cols] = kh * jnp.exp(tpos * (-lgf))
            q.q_b[rows, cols] = qh * jnp.exp(tpos * (-lgb))
            q.k_b[rows, cols] = kh * jnp.exp(tpos * lgb)
        q.v[rows, :] = pvg[:, 0:RET_W]

    def phase_a1(q):
        def body(r, carry):
            r0 = pl.multiple_of(r * ROW_TILE, ROW_TILE)
            rows = pl.ds(r0, ROW_TILE)
            h = normed(q, rows)
            p = _dot(h, wl_ref[...])
            q.xlp[pl.ds(r0 + SUBLANES, ROW_TILE), :] = p[:, 0:LRU_W]
            q.gy[rows, :] = _gelu_tanh(p[:, LRU_W:])
            if early_ret:
                project_retention(q, r0, rows, h)
            return carry
        return body

    for q in seqs:
        q.xlp[0:SUBLANES, :] = jnp.zeros((SUBLANES, LRU_W), _F32)
        q.xlp[n + SUBLANES:n + 2 * SUBLANES, :] = jnp.zeros((SUBLANES, LRU_W), _F32)

    half = LRU_W // 2

    def softplus_neg(lam):
        z = -lam
        return jnp.maximum(z, 0.0) + jnp.log1p(jnp.exp(-jnp.abs(z)))

    sp = (softplus_neg(lam_ref[0:1, :]), softplus_neg(lam_ref[1:2, :]))

    def phase_b(q):
        def body(r, carry):
            r0 = pl.multiple_of(r * ROW_TILE, ROW_TILE)
            rows = pl.ds(r0, ROW_TILE)
            ext = q.xlp[pl.ds(r0, ROW_TILE + 2 * SUBLANES), :]
            xc = cb_ref[...]
            for tap in range(4):
                o = SUBLANES - 2 + tap
                xc = xc + ext[o:o + ROW_TILE, :] * cw_ref[tap:tap + 1, :]
            xh = 0.5 * xc
            for d, (a_ref, u_ref) in enumerate(((q.a_f, q.h_f), (q.a_b, q.h_b))):
                bah = 0.5 * ba_ref[d:d + 1, :]
                bih = 0.5 * bi_ref[d:d + 1, :]
                ch = (-0.5 * LRU_C) * sp[d]
                for hh in range(2):
                    cs = slice(hh * half, (hh + 1) * half)
                    pre = _dot(xc[:, cs], wg_ref[d, hh])
                    t_r = jnp.tanh(pre[:, 0:half] + bah[:, cs])
                    t_i = jnp.tanh(pre[:, half:] + bih[:, cs])
                    log_a = t_r * ch[:, cs] + ch[:, cs]
                    a = jnp.exp(log_a)
                    a_ref[rows, cs] = a
                    om = -jnp.tanh(log_a) * (a * a + 1.0)
                    root = jnp.where(om > 0.0, om * lax.rsqrt(om), 0.0)
                    u_ref[rows, cs] = root * (t_i * xh[:, cs] + xh[:, cs])
            return carry
        return body

    for q in seqs:
        _tile_loop(nt, phase_a1(q))
    for q in seqs:
        _tile_loop(nt, phase_b(q))

    row_id = lax.broadcasted_iota(jnp.int32, (SUBLANES, LRU_W), 0)

    def scan_group(a, b, shift_of, mask_of):
        for s in (1, 2, 4):
            m = mask_of(s)
            a_s = jnp.where(m, pltpu.roll(a, shift_of(s), axis=0), 1.0)
            b_s = jnp.where(m, pltpu.roll(b, shift_of(s), axis=0), 0.0)
            b = a * b_s + b
            a = a * a_s
        return a, b

    zero_row = jnp.zeros((1, LRU_W), _F32)
    init_f = tuple(q.h0[0:1, :] if latent else zero_row for q in seqs)
    init_b = tuple(q.h0[1:2, :] if latent else zero_row for q in seqs)
    ng = n // SUBLANES

    def fwd_body(i, carry):
        rows = pl.ds(pl.multiple_of(i * SUBLANES, SUBLANES), SUBLANES)
        out = []
        for q, c in zip(seqs, carry):
            a, b = scan_group(q.a_f[rows, :], q.h_f[rows, :], lambda s: s, lambda s: row_id >= s)
            h = a * c + b
            q.h_f[rows, :] = h
            out.append(h[SUBLANES - 1:SUBLANES, :])
        return tuple(out)

    last_f = lax.fori_loop(0, ng, fwd_body, init_f, unroll=4)

    def bwd_body(i, carry):
        rows = pl.ds(pl.multiple_of((ng - 1 - i) * SUBLANES, SUBLANES), SUBLANES)
        out = []
        for q, c in zip(seqs, carry):
            a, b = scan_group(q.a_b[rows, :], q.h_b[rows, :], lambda s: SUBLANES - s,
                              lambda s: row_id < SUBLANES - s)
            h = a * c + b
            q.h_b[rows, :] = h
            out.append(h[0:1, :])
        return tuple(out)

    first_b = lax.fori_loop(0, ng, bwd_body, init_b, unroll=4)

    def lru_out(q):
        def body(r, carry):
            rows = pl.ds(pl.multiple_of(r * ROW_TILE, ROW_TILE), ROW_TILE)
            q.mix[rows, 0:LRU_W] = (q.h_f[rows, :] + q.h_b[rows, :]) * q.gy[rows, :]
            return carry
        return body

    for s, q in enumerate(seqs):
        if not latent:
            q.stl[0:1, :] = last_f[s]
            q.stl[1:2, :] = first_b[s]
        _tile_loop(nt, lru_out(q))

    def phase_a2(q):
        def body(r, carry):
            r0 = pl.multiple_of(r * ROW_TILE, ROW_TILE)
            rows = pl.ds(r0, ROW_TILE)
            project_retention(q, r0, rows, normed(q, rows))
            return carry
        return body

    if not early_ret:
        for q in seqs:
            _tile_loop(nt, phase_a2(q))

    lower = (lax.broadcasted_iota(jnp.int32, (ROW_TILE, ROW_TILE), 0)
             >= lax.broadcasted_iota(jnp.int32, (ROW_TILE, ROW_TILE), 1))
    blocks = [slice(r * ROW_TILE, (r + 1) * ROW_TILE) for r in range(nt)]

    def phase_c(q, hd):
        cols = slice(hd * DK, (hd + 1) * DK)
        dec = lambda row: dec_ref[hd * DEC_ROWS + row:hd * DEC_ROWS + row + 1, :]
        kv_f = [_dot_tn(q.k_f[rows, cols], q.v[rows, cols]) if (r < nt - 1 or not latent) else 0.0
                for r, rows in enumerate(blocks)]
        kv_b = [_dot_tn(q.k_b[rows, cols], q.v[rows, cols]) if (r > 0 or not latent) else 0.0
                for r, rows in enumerate(blocks)]
        if latent:
            run_f = q.s0[0, hd] * dec(DEC_G_F)
            run_b = q.s0[1, hd] * dec(DEC_GN_B)
        else:
            run_f = run_b = None
        before = []
        for r in range(nt):
            before.append(run_f)
            run_f = kv_f[r] if run_f is None else run_f + kv_f[r]
        after = [None] * nt
        for r in reversed(range(nt)):
            after[r] = run_b
            run_b = kv_b[r] if run_b is None else run_b + kv_b[r]

        for r, rows in enumerate(blocks):
            qf = q.q_f[rows, cols]
            qb = q.q_b[rows, cols]
            s = jnp.where(lower, _dot_nt(qf, q.k_f[rows, cols]), _dot_nt(qb, q.k_b[rows, cols]))
            o = _dot(s, q.v[rows, cols])
            if before[r] is not None:
                o = o + _dot(qf, before[r])
            if after[r] is not None:
                o = o + _dot(qb, after[r])
            o = o * lax.rsqrt(jnp.mean(o * o, axis=-1, keepdims=True) + EPS)
            q.mix[rows, LRU_W + cols.start:LRU_W + cols.stop] = o * q.sg[rows, cols]
        if not latent:
            q.str[0, hd] = run_f * dec(DEC_GN1_F)
            q.str[1, hd] = run_b

    for hd in range(RET_H):
        for q in seqs:
            phase_c(q, hd)

    gain1 = mod(2) * n2_ref[...]
    gain2 = n3_ref[...] * (1.0 + mod(4))
    sh2 = mod(3)

    def phase_d(q):
        def body(r, carry):
            rows = pl.ds(pl.multiple_of(r * ROW_TILE, ROW_TILE), ROW_TILE)
            mix = _dot(q.mix[rows, :], wout_ref[...])
            x1 = q.x[rows, :] + mix * lax.rsqrt(jnp.mean(mix * mix, axis=-1, keepdims=True) + EPS) * gain1
            q.x1[rows, :] = x1
            h2 = x1 * lax.rsqrt(jnp.mean(x1 * x1, axis=-1, keepdims=True) + EPS) * gain2 + sh2
            q.h2[rows, :] = h2.astype(_BF)
            q.lt[:, rows] = _dot(h2, rw_ref[...]).T[0:N_EXP, :]
            return carry
        return body

    for q in seqs:
        _tile_loop(nt, phase_d(q), unroll=2)


def _mixer_call(x, mod, norms, w_in, lru, wg, dec, w_out, rw, latent, extra=(), layer=0):
    b, n, _ = x.shape
    g = 1 if latent else CONTEXT_GROUP
    const2 = lambda i: (0, 0)
    in_specs = [
        pl.BlockSpec((g, n, D), lambda i: (i, 0, 0)),
        pl.BlockSpec((N_MOD, 8, D), lambda i: (0, 0, 0)),
        pl.BlockSpec((1, D), const2), pl.BlockSpec((1, D), const2), pl.BlockSpec((1, D), const2),
        pl.BlockSpec((D, D), lambda i: (0, 0)),
        pl.BlockSpec((D, D), lambda i: (0, 1)),
        pl.BlockSpec((D, D), lambda i: (0, 2)),
        pl.BlockSpec((4, LRU_W), const2), pl.BlockSpec((1, LRU_W), const2),
        pl.BlockSpec((2, LRU_W), const2), pl.BlockSpec((2, LRU_W), const2),
        pl.BlockSpec((2, LRU_W), const2),
        pl.BlockSpec((2, 2, LRU_W // 2, LRU_W), lambda i: (0, 0, 0, 0)),
        pl.BlockSpec((RET_H * DEC_ROWS + 4, LANES), const2),
        pl.BlockSpec((D, D), const2),
        pl.BlockSpec((D, LANES), const2),
    ]
    out_shape = [
        jax.ShapeDtypeStruct((b, n, D), _F32),
        jax.ShapeDtypeStruct((b, n, D), _BF),
        jax.ShapeDtypeStruct((b, N_EXP, n), _F32),
    ]
    out_specs = [
        pl.BlockSpec((g, n, D), lambda i: (i, 0, 0)),
        pl.BlockSpec((g, n, D), lambda i: (i, 0, 0)),
        pl.BlockSpec((g, N_EXP, n), lambda i: (i, 0, 0)),
    ]
    if latent:
        in_specs += [
            pl.BlockSpec((g, None, 2, LRU_W), lambda i: (i, layer, 0, 0)),
            pl.BlockSpec((g, None, 2, RET_H, DK, DK), lambda i: (i, layer, 0, 0, 0, 0)),
            pl.BlockSpec((n, DK), const2),
            pl.BlockSpec((n, DK), const2),
        ]
    else:
        out_shape += [
            jax.ShapeDtypeStruct((b, 1, 2, LRU_W), _F32),
            jax.ShapeDtypeStruct((b, 1, 2, RET_H, DK, DK), _F32),
        ]
        out_specs += [
            pl.BlockSpec((g, None, 2, LRU_W), lambda i: (i, 0, 0, 0)),
            pl.BlockSpec((g, None, 2, RET_H, DK, DK), lambda i: (i, 0, 0, 0, 0, 0)),
        ]
    f32s = lambda shape: pltpu.VMEM((g,) + shape, _F32)
    scratch = [
        f32s((n + 2 * SUBLANES, LRU_W)),
        f32s((n, LRU_W)),
        f32s((n, LRU_W)), f32s((n, LRU_W)),
        f32s((n, LRU_W)), f32s((n, LRU_W)),
    ]
    if not latent:
        scratch += [f32s((n, RET_W)) for _ in range(6)]
    return pl.pallas_call(
        functools.partial(_mixer_kernel, n=n, g=g, latent=latent),
        out_shape=out_shape,
        grid=(b // g,),
        in_specs=in_specs,
        out_specs=out_specs,
        scratch_shapes=scratch,
        compiler_params=pltpu.CompilerParams(
            dimension_semantics=("arbitrary",), vmem_limit_bytes=VMEM_LIMIT),
        name="mixer_latent" if latent else "mixer_context",
    )(x, mod, *norms, w_in, w_in, w_in, *lru, wg, dec, w_out, rw, *extra)


def _count(mask):
    return jnp.sum(jnp.where(mask, 1.0, 0.0), axis=-1, keepdims=True)


def _probs(l3):
    bsz, _, n = l3.shape
    m = jnp.max(l3, axis=1, keepdims=True)
    e = jnp.exp(l3 - m)
    return (e / jnp.sum(e, axis=1, keepdims=True)).reshape(bsz * N_EXP, n)


def _finish_select(p, bits, thr, cap):
    rows, n = p.shape
    gt = bits > thr
    eq = bits == thr
    need = float(cap) - _count(gt)
    idx = lax.broadcasted_iota(jnp.int32, (rows, n), 1)
    nbits = int(math.log2(n))

    def idx_body(i, j):
        cand = j | (jnp.int32(1) << (nbits - 1 - i))
        return jnp.where(_count(eq & (idx < cand)) < need, cand, j)

    jlast = lax.fori_loop(0, nbits, idx_body, jnp.zeros((rows, 1), jnp.int32))
    sel = gt | (eq & (idx <= jlast))
    before = (lax.broadcasted_iota(jnp.int32, (n, n), 0)
              < lax.broadcasted_iota(jnp.int32, (n, n), 1))
    pos = _dot(jnp.where(sel, 1.0, 0.0).astype(_BF), jnp.where(before, 1.0, 0.0).astype(_BF))
    return jnp.where(sel, pos, -1.0), jnp.where(sel, p, 0.0)


def _route_kernel(lp_ref, ls_ref, pp_ref, gp_ref, ps_ref, gs_ref, *, cap_p, cap_s):
    groups = ((_probs(lp_ref[...]), float(cap_p)), (_probs(ls_ref[...]), float(cap_s)))
    bits = [pltpu.bitcast(p, jnp.int32) for p, _ in groups]

    def val_body(i, thr):
        out = []
        for b, t, (_, capf) in zip(bits, thr, groups):
            cand = t | (jnp.int32(1) << (30 - i))
            out.append(jnp.where(_count(b >= cand) >= capf, cand, t))
        return tuple(out)

    thr = lax.fori_loop(0, 31, val_body,
                        tuple(jnp.zeros((b.shape[0], 1), jnp.int32) for b in bits))
    pos, gate = _finish_select(groups[0][0], bits[0], thr[0], cap_p)
    pp_ref[...] = pos.reshape(pp_ref.shape)
    gp_ref[...] = gate.reshape(gp_ref.shape)
    pos, gate = _finish_select(groups[1][0], bits[1], thr[1], cap_s)
    ps_ref[...] = pos.reshape(ps_ref.shape)
    gs_ref[...] = gate.reshape(gs_ref.shape)


def _route_call(lt_p, lt_s, cap_p, cap_s):
    shapes = [
        jax.ShapeDtypeStruct(lt_p.shape, _F32), jax.ShapeDtypeStruct(lt_p.shape, _F32),
        jax.ShapeDtypeStruct(lt_s.shape, _F32), jax.ShapeDtypeStruct(lt_s.shape, _F32),
    ]
    return pl.pallas_call(
        functools.partial(_route_kernel, cap_p=cap_p, cap_s=cap_s),
        out_shape=shapes,
        compiler_params=pltpu.CompilerParams(vmem_limit_bytes=VMEM_LIMIT),
        name="route_select",
    )(lt_p, lt_s)


def _dispatch_kernel(pos_ref, gate_ref, h_ref, xs_ref, gs_ref, *, n, cap, g):
    slot = lax.broadcasted_iota(jnp.int32, (cap, n), 0).astype(_F32)
    for j in range(g):
        slots = slice(j * cap, (j + 1) * cap)
        parts = []
        for e in range(N_EXP):
            hit = pos_ref[j, e:e + 1, :] == slot
            parts.append(jnp.where(hit, 1.0, 0.0).astype(_BF))
            gs_ref[e, slots, :] = jnp.sum(jnp.where(hit, gate_ref[j, e:e + 1, :], 0.0),
                                          axis=-1, keepdims=True)
        onehot = jnp.concatenate(parts, axis=0)
        xs = _dot(onehot, h_ref[j]).astype(_BF)
        for e in range(N_EXP):
            xs_ref[e, slots, :] = xs[e * cap:(e + 1) * cap, :]


def _dispatch_call(pos, gate, h2, cap):
    b, n, _ = h2.shape
    g = max(1, SMALL_STEP_TOKENS // n)
    return pl.pallas_call(
        functools.partial(_dispatch_kernel, n=n, cap=cap, g=g),
        out_shape=[
            jax.ShapeDtypeStruct((N_EXP, b * cap, D), _BF),
            jax.ShapeDtypeStruct((N_EXP, b * cap, 1), _F32),
        ],
        grid=(b // g,),
        in_specs=[
            pl.BlockSpec((g, N_EXP, n), lambda i: (i, 0, 0)),
            pl.BlockSpec((g, N_EXP, n), lambda i: (i, 0, 0)),
            pl.BlockSpec((g, n, D), lambda i: (i, 0, 0)),
        ],
        out_specs=[
            pl.BlockSpec((N_EXP, g * cap, D), lambda i: (0, i, 0)),
            pl.BlockSpec((N_EXP, g * cap, 1), lambda i: (0, i, 0)),
        ],
        compiler_params=pltpu.CompilerParams(
            dimension_semantics=("arbitrary",), vmem_limit_bytes=VMEM_LIMIT),
        name="dispatch",
    )(pos, gate, h2)


def _expert_kernel(xp_ref, xs_ref, gp_ref, gs_ref, wg_ref, wu_ref, wd_ref, y_ref, xcat, acc,
                   *, sp, nf, tf):
    f = pl.program_id(1)
    xcat[0:sp, :] = xp_ref[...]
    xcat[sp:, :] = xs_ref[...]
    x = xcat[...]
    total = jnp.where(f == 0, 0.0, acc[...])
    for c in range(tf // FF_CHUNK):
        cs = slice(c * FF_CHUNK, (c + 1) * FF_CHUNK)
        hg = _dot(x, wg_ref[:, cs].astype(_BF))
        hu = _dot(x, wu_ref[:, cs].astype(_BF))
        hid = (_silu(hg) * hu).astype(_BF)
        total = total + _dot(hid, wd_ref[cs, :].astype(_BF))
    acc[...] = total
    y_ref[0:sp, :] = (total[0:sp, :] * gp_ref[...]).astype(_BF)
    y_ref[sp:, :] = (total[sp:, :] * gs_ref[...]).astype(_BF)


def _expert_call(xs_p, xs_s, g_p, g_s, w_gate, w_up, w_down):
    tf = 1024
    sp = xs_p.shape[1]
    ss = xs_s.shape[1]
    nf = FF // tf
    return pl.pallas_call(
        functools.partial(_expert_kernel, sp=sp, nf=nf, tf=tf),
        out_shape=jax.ShapeDtypeStruct((N_EXP, sp + ss, D), _BF),
        grid=(N_EXP, nf),
        in_specs=[
            pl.BlockSpec((None, sp, D), lambda e, f: (e, 0, 0)),
            pl.BlockSpec((None, ss, D), lambda e, f: (e, 0, 0)),
            pl.BlockSpec((None, sp, 1), lambda e, f: (e, 0, 0)),
            pl.BlockSpec((None, ss, 1), lambda e, f: (e, 0, 0)),
            pl.BlockSpec((None, D, tf), lambda e, f: (e, 0, f)),
            pl.BlockSpec((None, D, tf), lambda e, f: (e, 0, f)),
            pl.BlockSpec((None, tf, D), lambda e, f: (e, f, 0)),
        ],
        out_specs=pl.BlockSpec((None, sp + ss, D), lambda e, f: (e, 0, 0)),
        scratch_shapes=[pltpu.VMEM((sp + ss, D), _BF), pltpu.VMEM((sp + ss, D), _F32)],
        compiler_params=pltpu.CompilerParams(
            dimension_semantics=("arbitrary", "arbitrary"), vmem_limit_bytes=VMEM_LIMIT),
        name="expert_ffn",
    )(xs_p, xs_s, g_p, g_s, w_gate, w_up, w_down)


def _combine_kernel(pos_ref, ye_ref, x1_ref, mod_ref, n4_ref, y_ref, *, n, cap, g, latent):
    width = N_EXP * cap
    lane = lax.broadcasted_iota(jnp.int32, (N_EXP, width), 1)
    expand = jnp.where(lane // cap == lax.broadcasted_iota(jnp.int32, (N_EXP, width), 0),
                       1.0, 0.0).astype(_BF)
    tile = min(n, COMBINE_TILE)
    slot = (lax.broadcasted_iota(jnp.int32, (tile, width), 1) % cap).astype(_F32)
    mod_row = (pl.program_id(0) + 1) if latent else 0
    gain = mod_ref[N_MOD - 1, pl.ds(mod_row, 1), :] * n4_ref[...]

    for j in range(g):
        ye = ye_ref[:, j * cap:(j + 1) * cap, :].reshape(width, D)

        def body(r, carry, j=j, ye=ye):
            rows = pl.ds(pl.multiple_of(r * tile, tile), tile)
            pos_e = _dot_tn(pos_ref[j, :, rows].astype(_BF), expand)
            onehot = jnp.where(pos_e == slot, 1.0, 0.0).astype(_BF)
            f = _dot(onehot, ye)
            y_ref[j, rows, :] = x1_ref[j, rows, :] + (
                f * lax.rsqrt(jnp.mean(f * f, axis=-1, keepdims=True) + EPS) * gain)
            return carry

        _tile_loop(n // tile, body)


def _combine_call(pos, ye, x1, mod, norm_post, cap, slot_off, latent):
    b, n, _ = x1.shape
    g = max(1, SMALL_STEP_TOKENS // n)
    assert g == 1 or not latent
    blk_off = slot_off // (g * cap)
    return pl.pallas_call(
        functools.partial(_combine_kernel, n=n, cap=cap, g=g, latent=latent),
        out_shape=jax.ShapeDtypeStruct((b, n, D), _F32),
        grid=(b // g,),
        in_specs=[
            pl.BlockSpec((g, N_EXP, n), lambda i: (i, 0, 0)),
            pl.BlockSpec((N_EXP, g * cap, D), lambda i: (0, i + blk_off, 0)),
            pl.BlockSpec((g, n, D), lambda i: (i, 0, 0)),
            pl.BlockSpec((N_MOD, 8, D), lambda i: (0, 0, 0)),
            pl.BlockSpec((1, D), lambda i: (0, 0)),
        ],
        out_specs=pl.BlockSpec((g, n, D), lambda i: (i, 0, 0)),
        compiler_params=pltpu.CompilerParams(
            dimension_semantics=("arbitrary",), vmem_limit_bytes=VMEM_LIMIT),
        name="combine_latent" if latent else "combine_context",
    )(pos, ye, x1, mod, norm_post)


def _block_diag_gates(wa, wi):
    per_half = LRU_HEADS // 2
    eye = jnp.eye(per_half, dtype=wa.dtype)

    def bd(w):
        w = w.reshape(2, 2, per_half, LRU_HD, LRU_HD)
        full = jnp.einsum('dghij,hk->dghikj', w, eye)
        return full.reshape(2, 2, per_half * LRU_HD, per_half * LRU_HD)

    return 0.5 * jnp.concatenate([bd(wa), bd(wi)], axis=-1)


def _rope_tables(n):
    rows = n // GRID_W
    row = np.repeat(np.arange(rows, dtype=np.float32), GRID_W)
    col = np.tile(np.arange(GRID_W, dtype=np.float32), rows)
    nf = DK // 4
    freqs = np.float32(ROPE_BASE) ** (-np.arange(nf, dtype=np.float32) / np.float32(nf))
    ang = np.concatenate([row[:, None] * freqs, col[:, None] * freqs], axis=-1).astype(np.float32)
    cos = np.cos(ang)
    sin = np.sin(ang)
    return (jnp.asarray(np.concatenate([cos, cos], axis=-1), _F32),
            jnp.asarray(np.concatenate([-sin, sin], axis=-1), _F32))


def _decay_table(n):
    heads = jnp.arange(RET_H, dtype=_F32)
    lgf = jnp.log1p(-jnp.exp2(-(RET_DECAY_OFFSET_FWD + heads)))
    lgb = jnp.log1p(-jnp.exp2(-(RET_DECAY_OFFSET_BWD + heads)))
    tab = jnp.stack([lgf, lgb, jnp.exp(lgf), jnp.exp(n * lgb), jnp.exp((n - 1) * lgf)], axis=1)
    tab = jnp.concatenate([tab.reshape(RET_H * DEC_ROWS), jnp.zeros((4,), _F32)])
    return jnp.broadcast_to(tab[:, None], (RET_H * DEC_ROWS + 4, LANES))


def kernel(x_prompt, x_sample, c, state_lru, state_ret, c_ctx, ada_w, ada_b, norm_mix_pre, norm_mix_post, norm_ffn_pre, norm_ffn_post, w_in, conv_w, conv_b, lru_wa, lru_ba, lru_wi, lru_bi, lru_lambda, w_out, router_w, exp_w_gate, exp_w_up, exp_w_down):
    bp, n_p, _ = x_prompt.shape
    bs, n_s, _ = x_sample.shape
    cap_p = 2 * n_p // N_EXP
    cap_s = 2 * n_s // N_EXP
    l = 0

    c_all = jnp.concatenate([c_ctx[None, :], c, jnp.zeros((8 - 1 - bs, D), _F32)], axis=0)
    mod = _ada_call(c_all, ada_w[l], ada_b[l][None, :])

    norms = (norm_mix_pre[l][None], norm_mix_post[l][None], norm_ffn_pre[l][None])
    lru = (conv_w[l], conv_b[l][None], lru_ba[l], lru_bi[l], lru_lambda[l])
    wg = _block_diag_gates(lru_wa[l], lru_wi[l])
    rw = jnp.pad(router_w[l], ((0, 0), (0, LANES - N_EXP)))
    cos2, sin2 = _rope_tables(n_s)

    x1_p, h2_p, lt_p, st_lru, st_ret = _mixer_call(
        x_prompt, mod, norms, w_in[l], lru, wg, _decay_table(n_p), w_out[l], rw, latent=False)
    x1_s, h2_s, lt_s = _mixer_call(
        x_sample, mod, norms, w_in[l], lru, wg, _decay_table(n_s), w_out[l], rw, latent=True,
        extra=(state_lru, state_ret, cos2, sin2), layer=l)

    pos_p, gate_p, pos_s, gate_s = _route_call(lt_p, lt_s, cap_p, cap_s)
    xs_p, gsl_p = _dispatch_call(pos_p, gate_p, h2_p, cap_p)
    xs_s, gsl_s = _dispatch_call(pos_s, gate_s, h2_s, cap_s)
    ye = _expert_call(xs_p, xs_s, gsl_p, gsl_s, exp_w_gate[l], exp_w_up[l], exp_w_down[l])

    norm_post = norm_ffn_post[l][None]
    y_p = _combine_call(pos_p, ye, x1_p, mod, norm_post, cap_p, 0, latent=False)
    y_s = _combine_call(pos_s, ye, x1_s, mod, norm_post, cap_s, bp * cap_p, latent=True)
    return (y_p, y_s, st_lru, st_ret)
```

```python
import functools
import math
import types

import jax
import jax.numpy as jnp
import numpy as np
from jax import lax
from jax.experimental import pallas as pl
from jax.experimental.pallas import tpu as pltpu

D = 1024
LRU_W = 512
LRU_HEADS = 8
LRU_HD = 64
LRU_C = 8.0
RET_W = 512
RET_H = 4
DK = 128
N_EXP = 16
FF = 2048
N_MOD = 6
EPS = 1e-6
GRID_W = 64
ROPE_BASE = 10000.0
RET_DECAY_OFFSET_FWD = 5.0
RET_DECAY_OFFSET_BWD = 5.5

ROW_TILE = 256
FF_CHUNK = 512
COMBINE_TILE = 512
CONTEXT_GROUP = 2
SMALL_STEP_TOKENS = 1024
SUBLANES = 8
LANES = 128
VMEM_LIMIT = 60 * 1024 * 1024

DEC_LOG_F, DEC_LOG_B, DEC_G_F, DEC_GN_B, DEC_GN1_F, DEC_ROWS = 0, 1, 2, 3, 4, 5

_BF = jnp.bfloat16
_F32 = jnp.float32


def _sigmoid(x):
    return 0.5 * jnp.tanh(0.5 * x) + 0.5


def _silu(x):
    return x * _sigmoid(x)


def _gelu_tanh(x):
    c = math.sqrt(2.0 / math.pi)
    return 0.5 * x * (1.0 + jnp.tanh(c * (x + 0.044715 * (x * x * x))))


def _rms(x, gain):
    return x * lax.rsqrt(jnp.mean(x * x, axis=-1, keepdims=True) + EPS) * gain


def _dot(a, b):
    return jnp.dot(a, b, preferred_element_type=_F32)


def _dot_nt(a, b):
    return lax.dot_general(a, b, (((1,), (1,)), ((), ())), preferred_element_type=_F32)


def _dot_tn(a, b):
    return lax.dot_general(a, b, (((0,), (0,)), ((), ())), preferred_element_type=_F32)


def _ada_kernel(c_ref, w_ref, b_ref, o_ref):
    o_ref[...] = _dot(_silu(c_ref[...]), w_ref[...]) + b_ref[...]


def _ada_call(c_all, ada_w, ada_b):
    return pl.pallas_call(
        _ada_kernel,
        out_shape=jax.ShapeDtypeStruct((N_MOD, 8, D), _F32),
        grid=(N_MOD,),
        in_specs=[
            pl.BlockSpec((8, D), lambda j: (0, 0)),
            pl.BlockSpec((D, D), lambda j: (0, j)),
            pl.BlockSpec((1, D), lambda j: (0, j)),
        ],
        out_specs=pl.BlockSpec((None, 8, D), lambda j: (j, 0, 0)),
        compiler_params=pltpu.CompilerParams(
            dimension_semantics=("arbitrary",), vmem_limit_bytes=VMEM_LIMIT),
        name="ada_mod",
    )(c_all, ada_w, ada_b)


def _tile_loop(nt, body, unroll=1):
    if nt == 1:
        body(0, 0)
    else:
        lax.fori_loop(0, nt, body, 0, unroll=unroll)


def _mixer_kernel(*refs, n, g, latent):
    if latent:
        (x_ref, mod_ref, n1_ref, n2_ref, n3_ref, wl_ref, wqk_ref, wvg_ref,
         cw_ref, cb_ref, ba_ref, bi_ref, lam_ref, wg_ref, dec_ref, wout_ref, rw_ref,
         h0_ref, s0_ref, cos_ref, sin_ref,
         x1_ref, h2_ref, lt_ref,
         xlp_g, gy_g, af_g, ab_g, hf_g, hb_g) = refs
        sg_g, qf_g, qb_g, kf_g, kb_g, v_g = xlp_g, af_g, ab_g, hf_g, hb_g, gy_g
    else:
        (x_ref, mod_ref, n1_ref, n2_ref, n3_ref, wl_ref, wqk_ref, wvg_ref,
         cw_ref, cb_ref, ba_ref, bi_ref, lam_ref, wg_ref, dec_ref, wout_ref, rw_ref,
         x1_ref, h2_ref, lt_ref, stl_ref, str_ref,
         xlp_g, gy_g, af_g, ab_g, hf_g, hb_g, sg_g, qf_g, qb_g, kf_g, kb_g, v_g) = refs
    early_ret = not latent

    seqs = []
    for s in range(g):
        q = types.SimpleNamespace(
            x=x_ref.at[s], x1=x1_ref.at[s], h2=h2_ref.at[s], lt=lt_ref.at[s], mix=x1_ref.at[s],
            xlp=xlp_g.at[s], sg=sg_g.at[s], gy=gy_g.at[s], v=v_g.at[s],
            a_f=af_g.at[s], a_b=ab_g.at[s], h_f=hf_g.at[s], h_b=hb_g.at[s],
            q_f=qf_g.at[s], q_b=qb_g.at[s], k_f=kf_g.at[s], k_b=kb_g.at[s])
        if latent:
            q.h0, q.s0 = h0_ref.at[s], s0_ref.at[s]
        else:
            q.stl, q.str = stl_ref.at[s], str_ref.at[s]
        seqs.append(q)

    nt = n // ROW_TILE
    mod_row = (pl.program_id(0) + 1) if latent else 0
    mod = lambda k: mod_ref[k, pl.ds(mod_row, 1), :]
    shift = mod(0)
    scale = n1_ref[...] * (1.0 + mod(1))

    def normed(q, rows):
        x = q.x[rows, :]
        return x * lax.rsqrt(jnp.mean(x * x, axis=-1, keepdims=True) + EPS) * scale + shift

    def project_retention(q, r0, rows, h):
        pqk = _dot(h, wqk_ref[...])
        pvg = _dot(h, wvg_ref[...])
        q.sg[rows, :] = _silu(pvg[:, RET_W:])
        tpos = (r0 + lax.broadcasted_iota(jnp.int32, (ROW_TILE, DK), 0)).astype(_F32)
        if latent:
            cos2 = cos_ref[rows, :]
            sin2 = sin_ref[rows, :]
        for hd in range(RET_H):
            cols = slice(hd * DK, (hd + 1) * DK)
            qh = pqk[:, cols] * (DK ** -0.5)
            kh = pqk[:, RET_W + hd * DK:RET_W + (hd + 1) * DK]
            if latent:
                qh = qh * cos2 + pltpu.roll(qh, DK // 2, axis=1) * sin2
                kh = kh * cos2 + pltpu.roll(kh, DK // 2, axis=1) * sin2
            lgf = dec_ref[hd * DEC_ROWS + DEC_LOG_F:hd * DEC_ROWS + DEC_LOG_F + 1, :]
            lgb = dec_ref[hd * DEC_ROWS + DEC_LOG_B:hd * DEC_ROWS + DEC_LOG_B + 1, :]
            q.q_f[hd, rows, :] = qh * jnp.exp(tpos * lgf)
            q.k_f[hd, rows, :] = kh * jnp.exp(tpos * (-lgf))
            q.q_b[hd, rows, :] = qh * jnp.exp(tpos * (-lgb))
            q.k_b[hd, rows, :] = kh * jnp.exp(tpos * lgb)
        q.v[rows, :] = pvg[:, 0:RET_W]

    def phase_a1(q):
        def body(r, carry):
            r0 = pl.multiple_of(r * ROW_TILE, ROW_TILE)
            rows = pl.ds(r0, ROW_TILE)
            h = normed(q, rows)
            p = _dot(h, wl_ref[...])
            q.xlp[pl.ds(r0 + SUBLANES, ROW_TILE), :] = p[:, 0:LRU_W]
            q.gy[rows, :] = _gelu_tanh(p[:, LRU_W:])
            if early_ret:
                project_retention(q, r0, rows, h)
            return carry
        return body

    for q in seqs:
        q.xlp[0:SUBLANES, :] = jnp.zeros((SUBLANES, LRU_W), _F32)
        q.xlp[n + SUBLANES:n + 2 * SUBLANES, :] = jnp.zeros((SUBLANES, LRU_W), _F32)

    half = LRU_W // 2

    def softplus_neg(lam):
        z = -lam
        return jnp.maximum(z, 0.0) + jnp.log1p(jnp.exp(-jnp.abs(z)))

    sp = (softplus_neg(lam_ref[0:1, :]), softplus_neg(lam_ref[1:2, :]))

    def phase_b(q):
        def body(r, carry):
            r0 = pl.multiple_of(r * ROW_TILE, ROW_TILE)
            rows = pl.ds(r0, ROW_TILE)
            ext = q.xlp[pl.ds(r0, ROW_TILE + 2 * SUBLANES), :]
            xc = cb_ref[...]
            for tap in range(4):
                o = SUBLANES - 2 + tap
                xc = xc + ext[o:o + ROW_TILE, :] * cw_ref[tap:tap + 1, :]
            xh = 0.5 * xc
            for d, (a_ref, u_ref) in enumerate(((q.a_f, q.h_f), (q.a_b, q.h_b))):
                bah = 0.5 * ba_ref[d:d + 1, :]
                bih = 0.5 * bi_ref[d:d + 1, :]
                ch = (-0.5 * LRU_C) * sp[d]
                for hh in range(2):
                    cs = slice(hh * half, (hh + 1) * half)
                    pre = _dot(xc[:, cs], wg_ref[d, hh])
                    t_r = jnp.tanh(pre[:, 0:half] + bah[:, cs])
                    t_i = jnp.tanh(pre[:, half:] + bih[:, cs])
                    log_a = t_r * ch[:, cs] + ch[:, cs]
                    a = jnp.exp(log_a)
                    om = -jnp.tanh(log_a) * (a * a + 1.0)
                    root = jnp.where(om > 0.0, om * lax.rsqrt(om), 0.0)
                    u = root * (t_i * xh[:, cs] + xh[:, cs])
                    for j in range(half // LANES):
                        lanes = slice(j * LANES, (j + 1) * LANES)
                        a_ref[hh * (half // LANES) + j, rows, :] = a[:, lanes]
                        u_ref[hh * (half // LANES) + j, rows, :] = u[:, lanes]
            return carry
        return body

    for q in seqs:
        _tile_loop(nt, phase_a1(q))
    for q in seqs:
        _tile_loop(nt, phase_b(q))

    row8 = lax.broadcasted_iota(jnp.int32, (SUBLANES, LANES), 0)
    block = SUBLANES * SUBLANES
    n_blocks = n // block
    n_slabs = LRU_W // LANES

    def across_groups(a, b, reverse):
        for s in (1, 2, 4):
            m = (row8 < SUBLANES - s) if reverse else (row8 >= s)
            shift = SUBLANES - s if reverse else s
            a_s = jnp.where(m, pltpu.roll(a, shift, axis=0), 1.0)
            b_s = jnp.where(m, pltpu.roll(b, shift, axis=0), 0.0)
            b = a * b_s + b
            a = a * a_s
        return a, b

    def scan_block(a_ref, h_ref, base, carry, reverse):
        rows = [pl.ds(base + k, SUBLANES, stride=SUBLANES) for k in range(SUBLANES)]
        order = list(reversed(range(SUBLANES))) if reverse else list(range(SUBLANES))
        prod, local = {}, {}
        prev = None
        for k in order:
            a, u = a_ref[rows[k], :], h_ref[rows[k], :]
            prod[k] = a if prev is None else a * prod[prev]
            local[k] = u if prev is None else a * local[prev] + u
            prev = k
        p_all, h_all = across_groups(prod[prev], local[prev], reverse)
        inner = (row8 < SUBLANES - 1) if reverse else (row8 >= 1)
        shift = SUBLANES - 1 if reverse else 1
        enter = (jnp.where(inner, pltpu.roll(p_all, shift, axis=0), 1.0) * carry
                 + jnp.where(inner, pltpu.roll(h_all, shift, axis=0), 0.0))
        for k in order:
            h_ref[rows[k], :] = prod[k] * enter + local[k]
        leave = p_all * carry + h_all
        return leave[0:1, :] if reverse else leave[SUBLANES - 1:SUBLANES, :]

    def initial(q, d, s):
        if latent:
            return q.h0[d:d + 1, s * LANES:(s + 1) * LANES]
        return jnp.zeros((1, LANES), _F32)

    chains = [(q, s) for q in seqs for s in range(n_slabs)]

    def scan_body(i, carry):
        fwd_base = pl.multiple_of(i * block, block)
        bwd_base = pl.multiple_of((n_blocks - 1 - i) * block, block)
        out = []
        for (q, s), (cf, cb) in zip(chains, carry):
            out.append((scan_block(q.a_f.at[s], q.h_f.at[s], fwd_base, cf, False),
                        scan_block(q.a_b.at[s], q.h_b.at[s], bwd_base, cb, True)))
        return tuple(out)

    ends = lax.fori_loop(0, n_blocks, scan_body,
                         tuple((initial(q, 0, s), initial(q, 1, s)) for q, s in chains), unroll=2)

    def lru_out(q):
        def body(r, carry):
            rows = pl.ds(pl.multiple_of(r * ROW_TILE, ROW_TILE), ROW_TILE)
            for s in range(n_slabs):
                lanes = slice(s * LANES, (s + 1) * LANES)
                q.mix[rows, lanes] = (q.h_f[s, rows, :] + q.h_b[s, rows, :]) * q.gy[rows, lanes]
            return carry
        return body

    for (q, s), (last_f, first_b) in zip(chains, ends):
        if not latent:
            q.stl[0:1, s * LANES:(s + 1) * LANES] = last_f
            q.stl[1:2, s * LANES:(s + 1) * LANES] = first_b
    for q in seqs:
        _tile_loop(nt, lru_out(q))

    def phase_a2(q):
        def body(r, carry):
            r0 = pl.multiple_of(r * ROW_TILE, ROW_TILE)
            rows = pl.ds(r0, ROW_TILE)
            project_retention(q, r0, rows, normed(q, rows))
            return carry
        return body

    if not early_ret:
        for q in seqs:
            _tile_loop(nt, phase_a2(q))

    lower = (lax.broadcasted_iota(jnp.int32, (ROW_TILE, ROW_TILE), 0)
             >= lax.broadcasted_iota(jnp.int32, (ROW_TILE, ROW_TILE), 1))
    blocks = [slice(r * ROW_TILE, (r + 1) * ROW_TILE) for r in range(nt)]

    def phase_c(q, hd):
        cols = slice(hd * DK, (hd + 1) * DK)
        dec = lambda row: dec_ref[hd * DEC_ROWS + row:hd * DEC_ROWS + row + 1, :]
        kv_f = [_dot_tn(q.k_f[hd, rows, :], q.v[rows, cols]) if (r < nt - 1 or not latent) else 0.0
                for r, rows in enumerate(blocks)]
        kv_b = [_dot_tn(q.k_b[hd, rows, :], q.v[rows, cols]) if (r > 0 or not latent) else 0.0
                for r, rows in enumerate(blocks)]
        if latent:
            run_f = q.s0[0, hd] * dec(DEC_G_F)
            run_b = q.s0[1, hd] * dec(DEC_GN_B)
        else:
            run_f = run_b = None
        before = []
        for r in range(nt):
            before.append(run_f)
            run_f = kv_f[r] if run_f is None else run_f + kv_f[r]
        after = [None] * nt
        for r in reversed(range(nt)):
            after[r] = run_b
            run_b = kv_b[r] if run_b is None else run_b + kv_b[r]

        for r, rows in enumerate(blocks):
            qf = q.q_f[hd, rows, :]
            qb = q.q_b[hd, rows, :]
            s = jnp.where(lower, _dot_nt(qf, q.k_f[hd, rows, :]), _dot_nt(qb, q.k_b[hd, rows, :]))
            o = _dot(s, q.v[rows, cols])
            if before[r] is not None:
                o = o + _dot(qf, before[r])
            if after[r] is not None:
                o = o + _dot(qb, after[r])
            o = o * lax.rsqrt(jnp.mean(o * o, axis=-1, keepdims=True) + EPS)
            q.mix[rows, LRU_W + cols.start:LRU_W + cols.stop] = o * q.sg[rows, cols]
        if not latent:
            q.str[0, hd] = run_f * dec(DEC_GN1_F)
            q.str[1, hd] = run_b

    for hd in range(RET_H):
        for q in seqs:
            phase_c(q, hd)

    gain1 = mod(2) * n2_ref[...]
    gain2 = n3_ref[...] * (1.0 + mod(4))
    sh2 = mod(3)

    def phase_d(q):
        def body(r, carry):
            rows = pl.ds(pl.multiple_of(r * ROW_TILE, ROW_TILE), ROW_TILE)
            mix = _dot(q.mix[rows, :], wout_ref[...])
            x1 = q.x[rows, :] + mix * lax.rsqrt(jnp.mean(mix * mix, axis=-1, keepdims=True) + EPS) * gain1
            q.x1[rows, :] = x1
            h2 = x1 * lax.rsqrt(jnp.mean(x1 * x1, axis=-1, keepdims=True) + EPS) * gain2 + sh2
            q.h2[rows, :] = h2.astype(_BF)
            q.lt[:, rows] = _dot(h2, rw_ref[...]).T[0:N_EXP, :]
            return carry
        return body

    for q in seqs:
        _tile_loop(nt, phase_d(q), unroll=2)


def _mixer_call(x, mod, norms, w_in, lru, wg, dec, w_out, rw, latent, extra=(), layer=0):
    b, n, _ = x.shape
    g = 1 if latent else CONTEXT_GROUP
    const2 = lambda i: (0, 0)
    in_specs = [
        pl.BlockSpec((g, n, D), lambda i: (i, 0, 0)),
        pl.BlockSpec((N_MOD, 8, D), lambda i: (0, 0, 0)),
        pl.BlockSpec((1, D), const2), pl.BlockSpec((1, D), const2), pl.BlockSpec((1, D), const2),
        pl.BlockSpec((D, D), lambda i: (0, 0)),
        pl.BlockSpec((D, D), lambda i: (0, 1)),
        pl.BlockSpec((D, D), lambda i: (0, 2)),
        pl.BlockSpec((4, LRU_W), const2), pl.BlockSpec((1, LRU_W), const2),
        pl.BlockSpec((2, LRU_W), const2), pl.BlockSpec((2, LRU_W), const2),
        pl.BlockSpec((2, LRU_W), const2),
        pl.BlockSpec((2, 2, LRU_W // 2, LRU_W), lambda i: (0, 0, 0, 0)),
        pl.BlockSpec((RET_H * DEC_ROWS + 4, LANES), const2),
        pl.BlockSpec((D, D), const2),
        pl.BlockSpec((D, LANES), const2),
    ]
    out_shape = [
        jax.ShapeDtypeStruct((b, n, D), _F32),
        jax.ShapeDtypeStruct((b, n, D), _BF),
        jax.ShapeDtypeStruct((b, N_EXP, n), _F32),
    ]
    out_specs = [
        pl.BlockSpec((g, n, D), lambda i: (i, 0, 0)),
        pl.BlockSpec((g, n, D), lambda i: (i, 0, 0)),
        pl.BlockSpec((g, N_EXP, n), lambda i: (i, 0, 0)),
    ]
    if latent:
        in_specs += [
            pl.BlockSpec((g, None, 2, LRU_W), lambda i: (i, layer, 0, 0)),
            pl.BlockSpec((g, None, 2, RET_H, DK, DK), lambda i: (i, layer, 0, 0, 0, 0)),
            pl.BlockSpec((n, DK), const2),
            pl.BlockSpec((n, DK), const2),
        ]
    else:
        out_shape += [
            jax.ShapeDtypeStruct((b, 1, 2, LRU_W), _F32),
            jax.ShapeDtypeStruct((b, 1, 2, RET_H, DK, DK), _F32),
        ]
        out_specs += [
            pl.BlockSpec((g, None, 2, LRU_W), lambda i: (i, 0, 0, 0)),
            pl.BlockSpec((g, None, 2, RET_H, DK, DK), lambda i: (i, 0, 0, 0, 0, 0)),
        ]
    f32s = lambda shape: pltpu.VMEM((g,) + shape, _F32)
    slabs = (LRU_W // LANES, n, LANES)
    scratch = [
        f32s((n + 2 * SUBLANES, LRU_W)),
        f32s((n, LRU_W)),
        f32s(slabs), f32s(slabs),
        f32s(slabs), f32s(slabs),
    ]
    if not latent:
        scratch += [f32s((n, RET_W))] + [f32s(slabs) for _ in range(4)] + [f32s((n, RET_W))]
    return pl.pallas_call(
        functools.partial(_mixer_kernel, n=n, g=g, latent=latent),
        out_shape=out_shape,
        grid=(b // g,),
        in_specs=in_specs,
        out_specs=out_specs,
        scratch_shapes=scratch,
        compiler_params=pltpu.CompilerParams(
            dimension_semantics=("arbitrary",), vmem_limit_bytes=VMEM_LIMIT),
        name="mixer_latent" if latent else "mixer_context",
    )(x, mod, *norms, w_in, w_in, w_in, *lru, wg, dec, w_out, rw, *extra)


def _count(mask):
    return jnp.sum(jnp.where(mask, 1.0, 0.0), axis=-1, keepdims=True)


def _probs(l3):
    bsz, _, n = l3.shape
    m = jnp.max(l3, axis=1, keepdims=True)
    e = jnp.exp(l3 - m)
    return (e / jnp.sum(e, axis=1, keepdims=True)).reshape(bsz * N_EXP, n)


def _finish_select(p, bits, thr, cap):
    rows, n = p.shape
    gt = bits > thr
    eq = bits == thr
    need = float(cap) - _count(gt)
    idx = lax.broadcasted_iota(jnp.int32, (rows, n), 1)
    nbits = int(math.log2(n))

    def idx_body(i, j):
        cand = j | (jnp.int32(1) << (nbits - 1 - i))
        return jnp.where(_count(eq & (idx < cand)) < need, cand, j)

    jlast = lax.fori_loop(0, nbits, idx_body, jnp.zeros((rows, 1), jnp.int32))
    sel = gt | (eq & (idx <= jlast))
    before = (lax.broadcasted_iota(jnp.int32, (n, n), 0)
              < lax.broadcasted_iota(jnp.int32, (n, n), 1))
    pos = _dot(jnp.where(sel, 1.0, 0.0).astype(_BF), jnp.where(before, 1.0, 0.0).astype(_BF))
    return jnp.where(sel, pos, -1.0), jnp.where(sel, p, 0.0)


def _route_kernel(lp_ref, ls_ref, pp_ref, gp_ref, ps_ref, gs_ref, *, cap_p, cap_s):
    groups = ((_probs(lp_ref[...]), float(cap_p)), (_probs(ls_ref[...]), float(cap_s)))
    bits = [pltpu.bitcast(p, jnp.int32) for p, _ in groups]

    def val_body(i, thr):
        out = []
        for b, t, (_, capf) in zip(bits, thr, groups):
            cand = t | (jnp.int32(1) << (30 - i))
            out.append(jnp.where(_count(b >= cand) >= capf, cand, t))
        return tuple(out)

    thr = lax.fori_loop(0, 31, val_body,
                        tuple(jnp.zeros((b.shape[0], 1), jnp.int32) for b in bits))
    pos, gate = _finish_select(groups[0][0], bits[0], thr[0], cap_p)
    pp_ref[...] = pos.reshape(pp_ref.shape)
    gp_ref[...] = gate.reshape(gp_ref.shape)
    pos, gate = _finish_select(groups[1][0], bits[1], thr[1], cap_s)
    ps_ref[...] = pos.reshape(ps_ref.shape)
    gs_ref[...] = gate.reshape(gs_ref.shape)


def _route_call(lt_p, lt_s, cap_p, cap_s):
    shapes = [
        jax.ShapeDtypeStruct(lt_p.shape, _F32), jax.ShapeDtypeStruct(lt_p.shape, _F32),
        jax.ShapeDtypeStruct(lt_s.shape, _F32), jax.ShapeDtypeStruct(lt_s.shape, _F32),
    ]
    return pl.pallas_call(
        functools.partial(_route_kernel, cap_p=cap_p, cap_s=cap_s),
        out_shape=shapes,
        compiler_params=pltpu.CompilerParams(vmem_limit_bytes=VMEM_LIMIT),
        name="route_select",
    )(lt_p, lt_s)


def _dispatch_kernel(pos_ref, gate_ref, h_ref, xs_ref, gs_ref, *, n, cap, g):
    slot = lax.broadcasted_iota(jnp.int32, (cap, n), 0).astype(_F32)
    for j in range(g):
        slots = slice(j * cap, (j + 1) * cap)
        parts = []
        for e in range(N_EXP):
            hit = pos_ref[j, e:e + 1, :] == slot
            parts.append(jnp.where(hit, 1.0, 0.0).astype(_BF))
            gs_ref[e, slots, :] = jnp.sum(jnp.where(hit, gate_ref[j, e:e + 1, :], 0.0),
                                          axis=-1, keepdims=True)
        onehot = jnp.concatenate(parts, axis=0)
        xs = _dot(onehot, h_ref[j]).astype(_BF)
        for e in range(N_EXP):
            xs_ref[e, slots, :] = xs[e * cap:(e + 1) * cap, :]


def _dispatch_call(pos, gate, h2, cap):
    b, n, _ = h2.shape
    g = max(1, SMALL_STEP_TOKENS // n)
    return pl.pallas_call(
        functools.partial(_dispatch_kernel, n=n, cap=cap, g=g),
        out_shape=[
            jax.ShapeDtypeStruct((N_EXP, b * cap, D), _BF),
            jax.ShapeDtypeStruct((N_EXP, b * cap, 1), _F32),
        ],
        grid=(b // g,),
        in_specs=[
            pl.BlockSpec((g, N_EXP, n), lambda i: (i, 0, 0)),
            pl.BlockSpec((g, N_EXP, n), lambda i: (i, 0, 0)),
            pl.BlockSpec((g, n, D), lambda i: (i, 0, 0)),
        ],
        out_specs=[
            pl.BlockSpec((N_EXP, g * cap, D), lambda i: (0, i, 0)),
            pl.BlockSpec((N_EXP, g * cap, 1), lambda i: (0, i, 0)),
        ],
        compiler_params=pltpu.CompilerParams(
            dimension_semantics=("arbitrary",), vmem_limit_bytes=VMEM_LIMIT),
        name="dispatch",
    )(pos, gate, h2)


def _expert_kernel(xp_ref, xs_ref, gp_ref, gs_ref, wg_ref, wu_ref, wd_ref, y_ref, xcat, acc,
                   *, sp, nf, tf):
    f = pl.program_id(1)
    xcat[0:sp, :] = xp_ref[...]
    xcat[sp:, :] = xs_ref[...]
    x = xcat[...]
    total = jnp.where(f == 0, 0.0, acc[...])
    for c in range(tf // FF_CHUNK):
        cs = slice(c * FF_CHUNK, (c + 1) * FF_CHUNK)
        hg = _dot(x, wg_ref[:, cs].astype(_BF))
        hu = _dot(x, wu_ref[:, cs].astype(_BF))
        hid = (_silu(hg) * hu).astype(_BF)
        total = total + _dot(hid, wd_ref[cs, :].astype(_BF))
    acc[...] = total
    y_ref[0:sp, :] = (total[0:sp, :] * gp_ref[...]).astype(_BF)
    y_ref[sp:, :] = (total[sp:, :] * gs_ref[...]).astype(_BF)


def _expert_call(xs_p, xs_s, g_p, g_s, w_gate, w_up, w_down):
    tf = 1024
    sp = xs_p.shape[1]
    ss = xs_s.shape[1]
    nf = FF // tf
    return pl.pallas_call(
        functools.partial(_expert_kernel, sp=sp, nf=nf, tf=tf),
        out_shape=jax.ShapeDtypeStruct((N_EXP, sp + ss, D), _BF),
        grid=(N_EXP, nf),
        in_specs=[
            pl.BlockSpec((None, sp, D), lambda e, f: (e, 0, 0)),
            pl.BlockSpec((None, ss, D), lambda e, f: (e, 0, 0)),
            pl.BlockSpec((None, sp, 1), lambda e, f: (e, 0, 0)),
            pl.BlockSpec((None, ss, 1), lambda e, f: (e, 0, 0)),
            pl.BlockSpec((None, D, tf), lambda e, f: (e, 0, f)),
            pl.BlockSpec((None, D, tf), lambda e, f: (e, 0, f)),
            pl.BlockSpec((None, tf, D), lambda e, f: (e, f, 0)),
        ],
        out_specs=pl.BlockSpec((None, sp + ss, D), lambda e, f: (e, 0, 0)),
        scratch_shapes=[pltpu.VMEM((sp + ss, D), _BF), pltpu.VMEM((sp + ss, D), _F32)],
        compiler_params=pltpu.CompilerParams(
            dimension_semantics=("arbitrary", "arbitrary"), vmem_limit_bytes=VMEM_LIMIT),
        name="expert_ffn",
    )(xs_p, xs_s, g_p, g_s, w_gate, w_up, w_down)


def _combine_kernel(pos_ref, ye_ref, x1_ref, mod_ref, n4_ref, y_ref, *, n, cap, g, latent):
    width = N_EXP * cap
    lane = lax.broadcasted_iota(jnp.int32, (N_EXP, width), 1)
    expand = jnp.where(lane // cap == lax.broadcasted_iota(jnp.int32, (N_EXP, width), 0),
                       1.0, 0.0).astype(_BF)
    tile = min(n, COMBINE_TILE)
    slot = (lax.broadcasted_iota(jnp.int32, (tile, width), 1) % cap).astype(_F32)
    mod_row = (pl.program_id(0) + 1) if latent else 0
    gain = mod_ref[N_MOD - 1, pl.ds(mod_row, 1), :] * n4_ref[...]

    for j in range(g):
        ye = ye_ref[:, j * cap:(j + 1) * cap, :].reshape(width, D)

        def body(r, carry, j=j, ye=ye):
            rows = pl.ds(pl.multiple_of(r * tile, tile), tile)
            pos_e = _dot_tn(pos_ref[j, :, rows].astype(_BF), expand)
            onehot = jnp.where(pos_e == slot, 1.0, 0.0).astype(_BF)
            f = _dot(onehot, ye)
            y_ref[j, rows, :] = x1_ref[j, rows, :] + (
                f * lax.rsqrt(jnp.mean(f * f, axis=-1, keepdims=True) + EPS) * gain)
            return carry

        _tile_loop(n // tile, body)


def _combine_call(pos, ye, x1, mod, norm_post, cap, slot_off, latent):
    b, n, _ = x1.shape
    g = max(1, SMALL_STEP_TOKENS // n)
    assert g == 1 or not latent
    blk_off = slot_off // (g * cap)
    return pl.pallas_call(
        functools.partial(_combine_kernel, n=n, cap=cap, g=g, latent=latent),
        out_shape=jax.ShapeDtypeStruct((b, n, D), _F32),
        grid=(b // g,),
        in_specs=[
            pl.BlockSpec((g, N_EXP, n), lambda i: (i, 0, 0)),
            pl.BlockSpec((N_EXP, g * cap, D), lambda i: (0, i + blk_off, 0)),
            pl.BlockSpec((g, n, D), lambda i: (i, 0, 0)),
            pl.BlockSpec((N_MOD, 8, D), lambda i: (0, 0, 0)),
            pl.BlockSpec((1, D), lambda i: (0, 0)),
        ],
        out_specs=pl.BlockSpec((g, n, D), lambda i: (i, 0, 0)),
        compiler_params=pltpu.CompilerParams(
            dimension_semantics=("arbitrary",), vmem_limit_bytes=VMEM_LIMIT),
        name="combine_latent" if latent else "combine_context",
    )(pos, ye, x1, mod, norm_post)


def _block_diag_gates(wa, wi):
    per_half = LRU_HEADS // 2
    side = per_half * LRU_HD
    on_diag = (np.arange(side)[:, None] // LRU_HD) == (np.arange(side)[None, :] // LRU_HD)

    def bd(w):
        rows = w.reshape(2, 2, side, LRU_HD)
        return jnp.where(on_diag, jnp.tile(rows, (1, 1, 1, per_half)), 0.0)

    return 0.5 * jnp.concatenate([bd(wa), bd(wi)], axis=-1)


def _rope_tables(n):
    rows = n // GRID_W
    row = np.repeat(np.arange(rows, dtype=np.float32), GRID_W)
    col = np.tile(np.arange(GRID_W, dtype=np.float32), rows)
    nf = DK // 4
    freqs = np.float32(ROPE_BASE) ** (-np.arange(nf, dtype=np.float32) / np.float32(nf))
    ang = np.concatenate([row[:, None] * freqs, col[:, None] * freqs], axis=-1).astype(np.float32)
    cos = np.cos(ang)
    sin = np.sin(ang)
    return (jnp.asarray(np.concatenate([cos, cos], axis=-1), _F32),
            jnp.asarray(np.concatenate([-sin, sin], axis=-1), _F32))


def _decay_table(n):
    heads = np.arange(RET_H, dtype=np.float32)
    f32 = np.float32
    lgf = np.log1p(-np.exp2(-(f32(RET_DECAY_OFFSET_FWD) + heads))).astype(f32)
    lgb = np.log1p(-np.exp2(-(f32(RET_DECAY_OFFSET_BWD) + heads))).astype(f32)
    tab = np.stack([lgf, lgb, np.exp(lgf), np.exp(f32(n) * lgb), np.exp(f32(n - 1) * lgf)], axis=1)
    tab = np.concatenate([tab.reshape(RET_H * DEC_ROWS).astype(f32), np.zeros((4,), f32)])
    return jnp.asarray(np.broadcast_to(tab[:, None], (RET_H * DEC_ROWS + 4, LANES)), _F32)


def kernel(x_prompt, x_sample, c, state_lru, state_ret, c_ctx, ada_w, ada_b, norm_mix_pre, norm_mix_post, norm_ffn_pre, norm_ffn_post, w_in, conv_w, conv_b, lru_wa, lru_ba, lru_wi, lru_bi, lru_lambda, w_out, router_w, exp_w_gate, exp_w_up, exp_w_down):
    bp, n_p, _ = x_prompt.shape
    bs, n_s, _ = x_sample.shape
    cap_p = 2 * n_p // N_EXP
    cap_s = 2 * n_s // N_EXP
    l = 0

    c_all = jnp.concatenate([c_ctx[None, :], c, jnp.zeros((8 - 1 - bs, D), _F32)], axis=0)
    mod = _ada_call(c_all, ada_w[l], ada_b[l][None, :])

    norms = (norm_mix_pre[l][None], norm_mix_post[l][None], norm_ffn_pre[l][None])
    lru = (conv_w[l], conv_b[l][None], lru_ba[l], lru_bi[l], lru_lambda[l])
    wg = _block_diag_gates(lru_wa[l], lru_wi[l])
    rw = jnp.pad(router_w[l], ((0, 0), (0, LANES - N_EXP)))
    cos2, sin2 = _rope_tables(n_s)

    x1_p, h2_p, lt_p, st_lru, st_ret = _mixer_call(
        x_prompt, mod, norms, w_in[l], lru, wg, _decay_table(n_p), w_out[l], rw, latent=False)
    x1_s, h2_s, lt_s = _mixer_call(
        x_sample, mod, norms, w_in[l], lru, wg, _decay_table(n_s), w_out[l], rw, latent=True,
        extra=(state_lru, state_ret, cos2, sin2), layer=l)

    pos_p, gate_p, pos_s, gate_s = _route_call(lt_p, lt_s, cap_p, cap_s)
    xs_p, gsl_p = _dispatch_call(pos_p, gate_p, h2_p, cap_p)
    xs_s, gsl_s = _dispatch_call(pos_s, gate_s, h2_s, cap_s)
    ye = _expert_call(xs_p, xs_s, gsl_p, gsl_s, exp_w_gate[l], exp_w_up[l], exp_w_down[l])

    norm_post = norm_ffn_post[l][None]
    y_p = _combine_call(pos_p, ye, x1_p, mod, norm_post, cap_p, 0, latent=False)
    y_s = _combine_call(pos_s, ye, x1_s, mod, norm_post, cap_s, bp * cap_p, latent=True)
    return (y_p, y_s, st_lru, st_ret)
```

```python
import functools
import math
import types

import jax
import jax.numpy as jnp
import numpy as np
from jax import lax
from jax.experimental import pallas as pl
from jax.experimental.pallas import tpu as pltpu

D = 1024
LRU_W = 512
LRU_HEADS = 8
LRU_HD = 64
LRU_C = 8.0
RET_W = 512
RET_H = 4
DK = 128
N_EXP = 16
FF = 2048
N_MOD = 6
EPS = 1e-6
GRID_W = 64
ROPE_BASE = 10000.0
RET_DECAY_OFFSET_FWD = 5.0
RET_DECAY_OFFSET_BWD = 5.5

ROW_TILE = 256
FF_CHUNK = 512
COMBINE_TILE = 512
CONTEXT_GROUP = 2
SMALL_STEP_TOKENS = 1024
SUBLANES = 8
LANES = 128
VMEM_LIMIT = 60 * 1024 * 1024

DEC_LOG_F, DEC_LOG_B, DEC_G_F, DEC_GN_B, DEC_GN1_F = 0, 1, 2, 3, 4

_BF = jnp.bfloat16
_F32 = jnp.float32


def _sigmoid(x):
    return 0.5 * jnp.tanh(0.5 * x) + 0.5


def _silu(x):
    return x * _sigmoid(x)


def _gelu_tanh(x):
    c = math.sqrt(2.0 / math.pi)
    return 0.5 * x * (1.0 + jnp.tanh(c * (x + 0.044715 * (x * x * x))))


def _rms(x, gain):
    return x * lax.rsqrt(jnp.mean(x * x, axis=-1, keepdims=True) + EPS) * gain


def _dot(a, b):
    return jnp.dot(a, b, preferred_element_type=_F32)


def _dot_nt(a, b):
    return lax.dot_general(a, b, (((1,), (1,)), ((), ())), preferred_element_type=_F32)


def _dot_tn(a, b):
    return lax.dot_general(a, b, (((0,), (0,)), ((), ())), preferred_element_type=_F32)


def _ada_kernel(cc_ref, c_ref, w_ref, b_ref, o_ref, s_ref):
    nb = c_ref.shape[0]
    s_ref[...] = jnp.zeros_like(s_ref)
    s_ref[0:1, :] = _silu(cc_ref[...])
    s_ref[1:1 + nb, :] = _silu(c_ref[...])
    o_ref[...] = _dot(s_ref[...], w_ref[...]) + b_ref[...]


def _ada_call(c_ctx, c, ada_w, ada_b):
    nb = c.shape[0]
    assert nb + 1 <= SUBLANES
    return pl.pallas_call(
        _ada_kernel,
        out_shape=jax.ShapeDtypeStruct((N_MOD, SUBLANES, D), _F32),
        grid=(N_MOD,),
        in_specs=[
            pl.BlockSpec((1, D), lambda j: (0, 0)),
            pl.BlockSpec((nb, D), lambda j: (0, 0)),
            pl.BlockSpec((D, D), lambda j: (0, j)),
            pl.BlockSpec((1, D), lambda j: (0, j)),
        ],
        out_specs=pl.BlockSpec((None, SUBLANES, D), lambda j: (j, 0, 0)),
        scratch_shapes=[pltpu.VMEM((SUBLANES, D), _F32)],
        compiler_params=pltpu.CompilerParams(
            dimension_semantics=("arbitrary",), vmem_limit_bytes=VMEM_LIMIT),
        name="ada_mod",
    )(c_ctx, c, ada_w, ada_b)


def _tile_loop(nt, body, unroll=1):
    if nt == 1:
        body(0, 0)
    else:
        lax.fori_loop(0, nt, body, 0, unroll=unroll)


def _mixer_kernel(*refs, n, g, latent):
    if latent:
        (x_ref, mod_ref, n1_ref, n2_ref, n3_ref, wl_ref, wqk_ref, wvg_ref,
         cw_ref, cb_ref, ba_ref, bi_ref, lam_ref, wg_ref, wout_ref, rw_ref,
         h0_ref, s0_ref, cos_ref, sin_ref,
         x1_ref, h2_ref, lt_ref,
         xlp_g, gy_g, af_g, ab_g, hf_g, hb_g) = refs
        sg_g, qf_g, qb_g, kf_g, kb_g, v_g = xlp_g, af_g, ab_g, hf_g, hb_g, gy_g
    else:
        (x_ref, mod_ref, n1_ref, n2_ref, n3_ref, wl_ref, wqk_ref, wvg_ref,
         cw_ref, cb_ref, ba_ref, bi_ref, lam_ref, wg_ref, wout_ref, rw_ref,
         x1_ref, h2_ref, lt_ref, stl_ref, str_ref,
         xlp_g, gy_g, af_g, ab_g, hf_g, hb_g, sg_g, qf_g, qb_g, kf_g, kb_g, v_g) = refs
    early_ret = not latent

    seqs = []
    for s in range(g):
        q = types.SimpleNamespace(
            x=x_ref.at[s], x1=x1_ref.at[s], h2=h2_ref.at[s], lt=lt_ref.at[s], mix=x1_ref.at[s],
            xlp=xlp_g.at[s], sg=sg_g.at[s], gy=gy_g.at[s], v=v_g.at[s],
            a_f=af_g.at[s], a_b=ab_g.at[s], h_f=hf_g.at[s], h_b=hb_g.at[s],
            q_f=qf_g.at[s], q_b=qb_g.at[s], k_f=kf_g.at[s], k_b=kb_g.at[s])
        if latent:
            q.h0, q.s0 = h0_ref.at[s], s0_ref.at[s]
        else:
            q.stl, q.str = stl_ref.at[s], str_ref.at[s]
        seqs.append(q)

    nt = n // ROW_TILE
    decay = _decay_consts(n)
    mod_row = (pl.program_id(0) + 1) if latent else 0
    mod = lambda k: mod_ref[k, pl.ds(mod_row, 1), :]
    shift = mod(0)
    scale = n1_ref[...] * (1.0 + mod(1))

    def normed(q, rows):
        x = q.x[rows, :]
        return x * lax.rsqrt(jnp.mean(x * x, axis=-1, keepdims=True) + EPS) * scale + shift

    def project_retention(q, r0, rows, h):
        pqk = _dot(h, wqk_ref[...])
        pvg = _dot(h, wvg_ref[...])
        q.sg[rows, :] = _silu(pvg[:, RET_W:])
        tpos = (r0 + lax.broadcasted_iota(jnp.int32, (ROW_TILE, DK), 0)).astype(_F32)
        if latent:
            cos2 = cos_ref[rows, :]
            sin2 = sin_ref[rows, :]
        for hd in range(RET_H):
            cols = slice(hd * DK, (hd + 1) * DK)
            qh = pqk[:, cols] * (DK ** -0.5)
            kh = pqk[:, RET_W + hd * DK:RET_W + (hd + 1) * DK]
            if latent:
                qh = qh * cos2 + pltpu.roll(qh, DK // 2, axis=1) * sin2
                kh = kh * cos2 + pltpu.roll(kh, DK // 2, axis=1) * sin2
            lgf = decay[hd][DEC_LOG_F]
            lgb = decay[hd][DEC_LOG_B]
            q.q_f[hd, rows, :] = qh * jnp.exp(tpos * lgf)
            q.k_f[hd, rows, :] = kh * jnp.exp(tpos * (-lgf))
            q.q_b[hd, rows, :] = qh * jnp.exp(tpos * (-lgb))
            q.k_b[hd, rows, :] = kh * jnp.exp(tpos * lgb)
        q.v[rows, :] = pvg[:, 0:RET_W]

    def phase_a1(q):
        def body(r, carry):
            r0 = pl.multiple_of(r * ROW_TILE, ROW_TILE)
            rows = pl.ds(r0, ROW_TILE)
            h = normed(q, rows)
            p = _dot(h, wl_ref[...])
            q.xlp[pl.ds(r0 + SUBLANES, ROW_TILE), :] = p[:, 0:LRU_W]
            q.gy[rows, :] = _gelu_tanh(p[:, LRU_W:])
            if early_ret:
                project_retention(q, r0, rows, h)
            return carry
        return body

    for q in seqs:
        q.xlp[0:SUBLANES, :] = jnp.zeros((SUBLANES, LRU_W), _F32)
        q.xlp[n + SUBLANES:n + 2 * SUBLANES, :] = jnp.zeros((SUBLANES, LRU_W), _F32)

    half = LRU_W // 2

    def softplus_neg(lam):
        z = -lam
        return jnp.maximum(z, 0.0) + jnp.log1p(jnp.exp(-jnp.abs(z)))

    sp = (softplus_neg(lam_ref[0:1, :]), softplus_neg(lam_ref[1:2, :]))

    def phase_b(q):
        def body(r, carry):
            r0 = pl.multiple_of(r * ROW_TILE, ROW_TILE)
            rows = pl.ds(r0, ROW_TILE)
            ext = q.xlp[pl.ds(r0, ROW_TILE + 2 * SUBLANES), :]
            xc = cb_ref[...]
            for tap in range(4):
                o = SUBLANES - 2 + tap
                xc = xc + ext[o:o + ROW_TILE, :] * cw_ref[tap:tap + 1, :]
            xh = 0.5 * xc
            for d, (a_ref, u_ref) in enumerate(((q.a_f, q.h_f), (q.a_b, q.h_b))):
                bah = 0.5 * ba_ref[d:d + 1, :]
                bih = 0.5 * bi_ref[d:d + 1, :]
                ch = (-0.5 * LRU_C) * sp[d]
                for hh in range(2):
                    cs = slice(hh * half, (hh + 1) * half)
                    pre = _dot(xc[:, cs], wg_ref[d, hh])
                    t_r = jnp.tanh(pre[:, 0:half] + bah[:, cs])
                    t_i = jnp.tanh(pre[:, half:] + bih[:, cs])
                    log_a = t_r * ch[:, cs] + ch[:, cs]
                    a = jnp.exp(log_a)
                    om = -jnp.tanh(log_a) * (a * a + 1.0)
                    root = jnp.where(om > 0.0, om * lax.rsqrt(om), 0.0)
                    u = root * (t_i * xh[:, cs] + xh[:, cs])
                    for j in range(half // LANES):
                        lanes = slice(j * LANES, (j + 1) * LANES)
                        a_ref[hh * (half // LANES) + j, rows, :] = a[:, lanes]
                        u_ref[hh * (half // LANES) + j, rows, :] = u[:, lanes]
            return carry
        return body

    for q in seqs:
        _tile_loop(nt, phase_a1(q))
    for q in seqs:
        _tile_loop(nt, phase_b(q))

    row8 = lax.broadcasted_iota(jnp.int32, (SUBLANES, LANES), 0)
    block = SUBLANES * SUBLANES
    n_blocks = n // block
    n_slabs = LRU_W // LANES

    def across_groups(a, b, reverse):
        for s in (1, 2, 4):
            m = (row8 < SUBLANES - s) if reverse else (row8 >= s)
            shift = SUBLANES - s if reverse else s
            a_s = jnp.where(m, pltpu.roll(a, shift, axis=0), 1.0)
            b_s = jnp.where(m, pltpu.roll(b, shift, axis=0), 0.0)
            b = a * b_s + b
            a = a * a_s
        return a, b

    def scan_block(a_ref, h_ref, base, carry, reverse):
        rows = [pl.ds(base + k, SUBLANES, stride=SUBLANES) for k in range(SUBLANES)]
        order = list(reversed(range(SUBLANES))) if reverse else list(range(SUBLANES))
        prod, local = {}, {}
        prev = None
        for k in order:
            a, u = a_ref[rows[k], :], h_ref[rows[k], :]
            prod[k] = a if prev is None else a * prod[prev]
            local[k] = u if prev is None else a * local[prev] + u
            prev = k
        p_all, h_all = across_groups(prod[prev], local[prev], reverse)
        inner = (row8 < SUBLANES - 1) if reverse else (row8 >= 1)
        shift = SUBLANES - 1 if reverse else 1
        enter = (jnp.where(inner, pltpu.roll(p_all, shift, axis=0), 1.0) * carry
                 + jnp.where(inner, pltpu.roll(h_all, shift, axis=0), 0.0))
        for k in order:
            h_ref[rows[k], :] = prod[k] * enter + local[k]
        leave = p_all * carry + h_all
        return leave[0:1, :] if reverse else leave[SUBLANES - 1:SUBLANES, :]

    def initial(q, d, s):
        if latent:
            return q.h0[d:d + 1, s * LANES:(s + 1) * LANES]
        return jnp.zeros((1, LANES), _F32)

    chains = [(q, s) for q in seqs for s in range(n_slabs)]

    def scan_body(i, carry):
        fwd_base = pl.multiple_of(i * block, block)
        bwd_base = pl.multiple_of((n_blocks - 1 - i) * block, block)
        out = []
        for (q, s), (cf, cb) in zip(chains, carry):
            out.append((scan_block(q.a_f.at[s], q.h_f.at[s], fwd_base, cf, False),
                        scan_block(q.a_b.at[s], q.h_b.at[s], bwd_base, cb, True)))
        return tuple(out)

    ends = lax.fori_loop(0, n_blocks, scan_body,
                         tuple((initial(q, 0, s), initial(q, 1, s)) for q, s in chains), unroll=2)

    def lru_out(q):
        def body(r, carry):
            rows = pl.ds(pl.multiple_of(r * ROW_TILE, ROW_TILE), ROW_TILE)
            for s in range(n_slabs):
                lanes = slice(s * LANES, (s + 1) * LANES)
                q.mix[rows, lanes] = (q.h_f[s, rows, :] + q.h_b[s, rows, :]) * q.gy[rows, lanes]
            return carry
        return body

    for (q, s), (last_f, first_b) in zip(chains, ends):
        if not latent:
            q.stl[0:1, s * LANES:(s + 1) * LANES] = last_f
            q.stl[1:2, s * LANES:(s + 1) * LANES] = first_b
    for q in seqs:
        _tile_loop(nt, lru_out(q))

    def phase_a2(q):
        def body(r, carry):
            r0 = pl.multiple_of(r * ROW_TILE, ROW_TILE)
            rows = pl.ds(r0, ROW_TILE)
            project_retention(q, r0, rows, normed(q, rows))
            return carry
        return body

    if not early_ret:
        for q in seqs:
            _tile_loop(nt, phase_a2(q))

    lower = (lax.broadcasted_iota(jnp.int32, (ROW_TILE, ROW_TILE), 0)
             >= lax.broadcasted_iota(jnp.int32, (ROW_TILE, ROW_TILE), 1))
    blocks = [slice(r * ROW_TILE, (r + 1) * ROW_TILE) for r in range(nt)]

    def phase_c(q, hd):
        cols = slice(hd * DK, (hd + 1) * DK)
        dec = lambda row: decay[hd][row]
        kv_f = [_dot_tn(q.k_f[hd, rows, :], q.v[rows, cols]) if (r < nt - 1 or not latent) else 0.0
                for r, rows in enumerate(blocks)]
        kv_b = [_dot_tn(q.k_b[hd, rows, :], q.v[rows, cols]) if (r > 0 or not latent) else 0.0
                for r, rows in enumerate(blocks)]
        if latent:
            run_f = q.s0[0, hd] * dec(DEC_G_F)
            run_b = q.s0[1, hd] * dec(DEC_GN_B)
        else:
            run_f = run_b = None
        before = []
        for r in range(nt):
            before.append(run_f)
            run_f = kv_f[r] if run_f is None else run_f + kv_f[r]
        after = [None] * nt
        for r in reversed(range(nt)):
            after[r] = run_b
            run_b = kv_b[r] if run_b is None else run_b + kv_b[r]

        for r, rows in enumerate(blocks):
            qf = q.q_f[hd, rows, :]
            qb = q.q_b[hd, rows, :]
            s = jnp.where(lower, _dot_nt(qf, q.k_f[hd, rows, :]), _dot_nt(qb, q.k_b[hd, rows, :]))
            o = _dot(s, q.v[rows, cols])
            if before[r] is not None:
                o = o + _dot(qf, before[r])
            if after[r] is not None:
                o = o + _dot(qb, after[r])
            o = o * lax.rsqrt(jnp.mean(o * o, axis=-1, keepdims=True) + EPS)
            q.mix[rows, LRU_W + cols.start:LRU_W + cols.stop] = o * q.sg[rows, cols]
        if not latent:
            q.str[0, hd] = run_f * dec(DEC_GN1_F)
            q.str[1, hd] = run_b

    for hd in range(RET_H):
        for q in seqs:
            phase_c(q, hd)

    gain1 = mod(2) * n2_ref[...]
    gain2 = n3_ref[...] * (1.0 + mod(4))
    sh2 = mod(3)

    def phase_d(q):
        def body(r, carry):
            rows = pl.ds(pl.multiple_of(r * ROW_TILE, ROW_TILE), ROW_TILE)
            mix = _dot(q.mix[rows, :], wout_ref[...])
            x1 = q.x[rows, :] + mix * lax.rsqrt(jnp.mean(mix * mix, axis=-1, keepdims=True) + EPS) * gain1
            q.x1[rows, :] = x1
            h2 = x1 * lax.rsqrt(jnp.mean(x1 * x1, axis=-1, keepdims=True) + EPS) * gain2 + sh2
            q.h2[rows, :] = h2.astype(_BF)
            q.lt[:, rows] = _dot(h2, rw_ref[...]).T[0:N_EXP, :]
            return carry
        return body

    for q in seqs:
        _tile_loop(nt, phase_d(q), unroll=2)


def _mixer_call(x, mod, norms, w_in, lru, wg, w_out, rw, latent, extra=(), layer=0):
    b, n, _ = x.shape
    g = 1 if latent else CONTEXT_GROUP
    const2 = lambda i: (0, 0)
    in_specs = [
        pl.BlockSpec((g, n, D), lambda i: (i, 0, 0)),
        pl.BlockSpec((N_MOD, SUBLANES, D), lambda i: (0, 0, 0)),
        pl.BlockSpec((1, D), const2), pl.BlockSpec((1, D), const2), pl.BlockSpec((1, D), const2),
        pl.BlockSpec((D, D), lambda i: (0, 0)),
        pl.BlockSpec((D, D), lambda i: (0, 1)),
        pl.BlockSpec((D, D), lambda i: (0, 2)),
        pl.BlockSpec((4, LRU_W), const2), pl.BlockSpec((1, LRU_W), const2),
        pl.BlockSpec((2, LRU_W), const2), pl.BlockSpec((2, LRU_W), const2),
        pl.BlockSpec((2, LRU_W), const2),
        pl.BlockSpec((2, 2, LRU_W // 2, LRU_W), lambda i: (0, 0, 0, 0)),
        pl.BlockSpec((D, D), const2),
        pl.BlockSpec((D, LANES), const2),
    ]
    out_shape = [
        jax.ShapeDtypeStruct((b, n, D), _F32),
        jax.ShapeDtypeStruct((b, n, D), _BF),
        jax.ShapeDtypeStruct((b, N_EXP, n), _F32),
    ]
    out_specs = [
        pl.BlockSpec((g, n, D), lambda i: (i, 0, 0)),
        pl.BlockSpec((g, n, D), lambda i: (i, 0, 0)),
        pl.BlockSpec((g, N_EXP, n), lambda i: (i, 0, 0)),
    ]
    if latent:
        in_specs += [
            pl.BlockSpec((g, None, 2, LRU_W), lambda i: (i, layer, 0, 0)),
            pl.BlockSpec((g, None, 2, RET_H, DK, DK), lambda i: (i, layer, 0, 0, 0, 0)),
            pl.BlockSpec((n, DK), const2),
            pl.BlockSpec((n, DK), const2),
        ]
    else:
        out_shape += [
            jax.ShapeDtypeStruct((b, 1, 2, LRU_W), _F32),
            jax.ShapeDtypeStruct((b, 1, 2, RET_H, DK, DK), _F32),
        ]
        out_specs += [
            pl.BlockSpec((g, None, 2, LRU_W), lambda i: (i, 0, 0, 0)),
            pl.BlockSpec((g, None, 2, RET_H, DK, DK), lambda i: (i, 0, 0, 0, 0, 0)),
        ]
    f32s = lambda shape: pltpu.VMEM((g,) + shape, _F32)
    slabs = (LRU_W // LANES, n, LANES)
    scratch = [
        f32s((n + 2 * SUBLANES, LRU_W)),
        f32s((n, LRU_W)),
        f32s(slabs), f32s(slabs),
        f32s(slabs), f32s(slabs),
    ]
    if not latent:
        scratch += [f32s((n, RET_W))] + [f32s(slabs) for _ in range(4)] + [f32s((n, RET_W))]
    return pl.pallas_call(
        functools.partial(_mixer_kernel, n=n, g=g, latent=latent),
        out_shape=out_shape,
        grid=(b // g,),
        in_specs=in_specs,
        out_specs=out_specs,
        scratch_shapes=scratch,
        compiler_params=pltpu.CompilerParams(
            dimension_semantics=("arbitrary",), vmem_limit_bytes=VMEM_LIMIT),
        name="mixer_latent" if latent else "mixer_context",
    )(x, mod, *norms, w_in, w_in, w_in, *lru, wg, w_out, rw, *extra)


def _count(mask):
    return jnp.sum(jnp.where(mask, 1.0, 0.0), axis=-1, keepdims=True)


def _probs(l3):
    bsz, _, n = l3.shape
    m = jnp.max(l3, axis=1, keepdims=True)
    e = jnp.exp(l3 - m)
    return (e / jnp.sum(e, axis=1, keepdims=True)).reshape(bsz * N_EXP, n)


def _finish_select(p, bits, thr, cap):
    rows, n = p.shape
    gt = bits > thr
    eq = bits == thr
    need = float(cap) - _count(gt)
    idx = lax.broadcasted_iota(jnp.int32, (rows, n), 1)
    nbits = int(math.log2(n))

    def idx_body(i, j):
        cand = j | (jnp.int32(1) << (nbits - 1 - i))
        return jnp.where(_count(eq & (idx < cand)) < need, cand, j)

    jlast = lax.fori_loop(0, nbits, idx_body, jnp.zeros((rows, 1), jnp.int32))
    sel = gt | (eq & (idx <= jlast))
    before = (lax.broadcasted_iota(jnp.int32, (n, n), 0)
              < lax.broadcasted_iota(jnp.int32, (n, n), 1))
    pos = _dot(jnp.where(sel, 1.0, 0.0).astype(_BF), jnp.where(before, 1.0, 0.0).astype(_BF))
    return jnp.where(sel, pos, -1.0), jnp.where(sel, p, 0.0)


def _route_kernel(lp_ref, ls_ref, pp_ref, gp_ref, ps_ref, gs_ref, *, cap_p, cap_s):
    groups = ((_probs(lp_ref[...]), float(cap_p)), (_probs(ls_ref[...]), float(cap_s)))
    bits = [pltpu.bitcast(p, jnp.int32) for p, _ in groups]

    def val_body(i, thr):
        out = []
        for b, t, (_, capf) in zip(bits, thr, groups):
            cand = t | (jnp.int32(1) << (30 - i))
            out.append(jnp.where(_count(b >= cand) >= capf, cand, t))
        return tuple(out)

    thr = lax.fori_loop(0, 31, val_body,
                        tuple(jnp.zeros((b.shape[0], 1), jnp.int32) for b in bits))
    pos, gate = _finish_select(groups[0][0], bits[0], thr[0], cap_p)
    pp_ref[...] = pos.reshape(pp_ref.shape)
    gp_ref[...] = gate.reshape(gp_ref.shape)
    pos, gate = _finish_select(groups[1][0], bits[1], thr[1], cap_s)
    ps_ref[...] = pos.reshape(ps_ref.shape)
    gs_ref[...] = gate.reshape(gs_ref.shape)


def _route_call(lt_p, lt_s, cap_p, cap_s):
    shapes = [
        jax.ShapeDtypeStruct(lt_p.shape, _F32), jax.ShapeDtypeStruct(lt_p.shape, _F32),
        jax.ShapeDtypeStruct(lt_s.shape, _F32), jax.ShapeDtypeStruct(lt_s.shape, _F32),
    ]
    return pl.pallas_call(
        functools.partial(_route_kernel, cap_p=cap_p, cap_s=cap_s),
        out_shape=shapes,
        compiler_params=pltpu.CompilerParams(vmem_limit_bytes=VMEM_LIMIT),
        name="route_select",
    )(lt_p, lt_s)


def _dispatch_kernel(pos_ref, gate_ref, h_ref, xs_ref, gs_ref, *, n, cap, g):
    slot = lax.broadcasted_iota(jnp.int32, (cap, n), 0).astype(_F32)
    for j in range(g):
        slots = slice(j * cap, (j + 1) * cap)
        parts = []
        for e in range(N_EXP):
            hit = pos_ref[j, e:e + 1, :] == slot
            parts.append(jnp.where(hit, 1.0, 0.0).astype(_BF))
            gs_ref[e, slots, :] = jnp.sum(jnp.where(hit, gate_ref[j, e:e + 1, :], 0.0),
                                          axis=-1, keepdims=True)
        onehot = jnp.concatenate(parts, axis=0)
        xs = _dot(onehot, h_ref[j]).astype(_BF)
        for e in range(N_EXP):
            xs_ref[e, slots, :] = xs[e * cap:(e + 1) * cap, :]


def _dispatch_call(pos, gate, h2, cap):
    b, n, _ = h2.shape
    g = max(1, SMALL_STEP_TOKENS // n)
    return pl.pallas_call(
        functools.partial(_dispatch_kernel, n=n, cap=cap, g=g),
        out_shape=[
            jax.ShapeDtypeStruct((N_EXP, b * cap, D), _BF),
            jax.ShapeDtypeStruct((N_EXP, b * cap, 1), _F32),
        ],
        grid=(b // g,),
        in_specs=[
            pl.BlockSpec((g, N_EXP, n), lambda i: (i, 0, 0)),
            pl.BlockSpec((g, N_EXP, n), lambda i: (i, 0, 0)),
            pl.BlockSpec((g, n, D), lambda i: (i, 0, 0)),
        ],
        out_specs=[
            pl.BlockSpec((N_EXP, g * cap, D), lambda i: (0, i, 0)),
            pl.BlockSpec((N_EXP, g * cap, 1), lambda i: (0, i, 0)),
        ],
        compiler_params=pltpu.CompilerParams(
            dimension_semantics=("arbitrary",), vmem_limit_bytes=VMEM_LIMIT),
        name="dispatch",
    )(pos, gate, h2)


def _expert_kernel(xp_ref, xs_ref, gp_ref, gs_ref, wg_ref, wu_ref, wd_ref, y_ref, xcat, acc,
                   *, sp, nf, tf):
    f = pl.program_id(1)
    xcat[0:sp, :] = xp_ref[...]
    xcat[sp:, :] = xs_ref[...]
    x = xcat[...]
    total = jnp.where(f == 0, 0.0, acc[...])
    for c in range(tf // FF_CHUNK):
        cs = slice(c * FF_CHUNK, (c + 1) * FF_CHUNK)
        hg = _dot(x, wg_ref[:, cs].astype(_BF))
        hu = _dot(x, wu_ref[:, cs].astype(_BF))
        hid = (_silu(hg) * hu).astype(_BF)
        total = total + _dot(hid, wd_ref[cs, :].astype(_BF))
    acc[...] = total
    y_ref[0:sp, :] = (total[0:sp, :] * gp_ref[...]).astype(_BF)
    y_ref[sp:, :] = (total[sp:, :] * gs_ref[...]).astype(_BF)


def _expert_call(xs_p, xs_s, g_p, g_s, w_gate, w_up, w_down):
    tf = 1024
    sp = xs_p.shape[1]
    ss = xs_s.shape[1]
    nf = FF // tf
    return pl.pallas_call(
        functools.partial(_expert_kernel, sp=sp, nf=nf, tf=tf),
        out_shape=jax.ShapeDtypeStruct((N_EXP, sp + ss, D), _BF),
        grid=(N_EXP, nf),
        in_specs=[
            pl.BlockSpec((None, sp, D), lambda e, f: (e, 0, 0)),
            pl.BlockSpec((None, ss, D), lambda e, f: (e, 0, 0)),
            pl.BlockSpec((None, sp, 1), lambda e, f: (e, 0, 0)),
            pl.BlockSpec((None, ss, 1), lambda e, f: (e, 0, 0)),
            pl.BlockSpec((None, D, tf), lambda e, f: (e, 0, f)),
            pl.BlockSpec((None, D, tf), lambda e, f: (e, 0, f)),
            pl.BlockSpec((None, tf, D), lambda e, f: (e, f, 0)),
        ],
        out_specs=pl.BlockSpec((None, sp + ss, D), lambda e, f: (e, 0, 0)),
        scratch_shapes=[pltpu.VMEM((sp + ss, D), _BF), pltpu.VMEM((sp + ss, D), _F32)],
        compiler_params=pltpu.CompilerParams(
            dimension_semantics=("arbitrary", "arbitrary"), vmem_limit_bytes=VMEM_LIMIT),
        name="expert_ffn",
    )(xs_p, xs_s, g_p, g_s, w_gate, w_up, w_down)


def _combine_kernel(pos_ref, ye_ref, x1_ref, mod_ref, n4_ref, y_ref, *, n, cap, g, latent):
    width = N_EXP * cap
    lane = lax.broadcasted_iota(jnp.int32, (N_EXP, width), 1)
    expand = jnp.where(lane // cap == lax.broadcasted_iota(jnp.int32, (N_EXP, width), 0),
                       1.0, 0.0).astype(_BF)
    tile = min(n, COMBINE_TILE)
    slot = (lax.broadcasted_iota(jnp.int32, (tile, width), 1) % cap).astype(_F32)
    mod_row = (pl.program_id(0) + 1) if latent else 0
    gain = mod_ref[N_MOD - 1, pl.ds(mod_row, 1), :] * n4_ref[...]

    for j in range(g):
        ye = ye_ref[:, j * cap:(j + 1) * cap, :].reshape(width, D)

        def body(r, carry, j=j, ye=ye):
            rows = pl.ds(pl.multiple_of(r * tile, tile), tile)
            pos_e = _dot_tn(pos_ref[j, :, rows].astype(_BF), expand)
            onehot = jnp.where(pos_e == slot, 1.0, 0.0).astype(_BF)
            f = _dot(onehot, ye)
            y_ref[j, rows, :] = x1_ref[j, rows, :] + (
                f * lax.rsqrt(jnp.mean(f * f, axis=-1, keepdims=True) + EPS) * gain)
            return carry

        _tile_loop(n // tile, body)


def _combine_call(pos, ye, x1, mod, norm_post, cap, slot_off, latent):
    b, n, _ = x1.shape
    g = max(1, SMALL_STEP_TOKENS // n)
    assert g == 1 or not latent
    blk_off = slot_off // (g * cap)
    return pl.pallas_call(
        functools.partial(_combine_kernel, n=n, cap=cap, g=g, latent=latent),
        out_shape=jax.ShapeDtypeStruct((b, n, D), _F32),
        grid=(b // g,),
        in_specs=[
            pl.BlockSpec((g, N_EXP, n), lambda i: (i, 0, 0)),
            pl.BlockSpec((N_EXP, g * cap, D), lambda i: (0, i + blk_off, 0)),
            pl.BlockSpec((g, n, D), lambda i: (i, 0, 0)),
            pl.BlockSpec((N_MOD, SUBLANES, D), lambda i: (0, 0, 0)),
            pl.BlockSpec((1, D), lambda i: (0, 0)),
        ],
        out_specs=pl.BlockSpec((g, n, D), lambda i: (i, 0, 0)),
        compiler_params=pltpu.CompilerParams(
            dimension_semantics=("arbitrary",), vmem_limit_bytes=VMEM_LIMIT),
        name="combine_latent" if latent else "combine_context",
    )(pos, ye, x1, mod, norm_post)


def _block_diag_gates(wa, wi):
    per_half = LRU_HEADS // 2
    side = per_half * LRU_HD
    on_diag = (np.arange(side)[:, None] // LRU_HD) == (np.arange(side)[None, :] // LRU_HD)

    def bd(w):
        rows = w.reshape(2, 2, side, LRU_HD)
        return jnp.where(on_diag, jnp.tile(rows, (1, 1, 1, per_half)), 0.0)

    return 0.5 * jnp.concatenate([bd(wa), bd(wi)], axis=-1)


def _rope_tables(n):
    rows = n // GRID_W
    row = np.repeat(np.arange(rows, dtype=np.float32), GRID_W)
    col = np.tile(np.arange(GRID_W, dtype=np.float32), rows)
    nf = DK // 4
    freqs = np.float32(ROPE_BASE) ** (-np.arange(nf, dtype=np.float32) / np.float32(nf))
    ang = np.concatenate([row[:, None] * freqs, col[:, None] * freqs], axis=-1).astype(np.float32)
    cos = np.cos(ang)
    sin = np.sin(ang)
    return (jnp.asarray(np.concatenate([cos, cos], axis=-1), _F32),
            jnp.asarray(np.concatenate([-sin, sin], axis=-1), _F32))


def _decay_consts(n):
    heads = np.arange(RET_H, dtype=np.float32)
    f32 = np.float32
    lgf = np.log1p(-np.exp2(-(f32(RET_DECAY_OFFSET_FWD) + heads))).astype(f32)
    lgb = np.log1p(-np.exp2(-(f32(RET_DECAY_OFFSET_BWD) + heads))).astype(f32)
    tab = np.stack([lgf, lgb, np.exp(lgf), np.exp(f32(n) * lgb), np.exp(f32(n - 1) * lgf)], axis=1)
    return [[float(v) for v in row] for row in tab.astype(f32)]


def kernel(x_prompt, x_sample, c, state_lru, state_ret, c_ctx, ada_w, ada_b, norm_mix_pre, norm_mix_post, norm_ffn_pre, norm_ffn_post, w_in, conv_w, conv_b, lru_wa, lru_ba, lru_wi, lru_bi, lru_lambda, w_out, router_w, exp_w_gate, exp_w_up, exp_w_down):
    bp, n_p, _ = x_prompt.shape
    bs, n_s, _ = x_sample.shape
    cap_p = 2 * n_p // N_EXP
    cap_s = 2 * n_s // N_EXP
    l = 0

    mod = _ada_call(c_ctx[None, :], c, ada_w[l], ada_b[l][None, :])

    norms = (norm_mix_pre[l][None], norm_mix_post[l][None], norm_ffn_pre[l][None])
    lru = (conv_w[l], conv_b[l][None], lru_ba[l], lru_bi[l], lru_lambda[l])
    wg = _block_diag_gates(lru_wa[l], lru_wi[l])
    rw = jnp.pad(router_w[l], ((0, 0), (0, LANES - N_EXP)))
    cos2, sin2 = _rope_tables(n_s)

    x1_p, h2_p, lt_p, st_lru, st_ret = _mixer_call(
        x_prompt, mod, norms, w_in[l], lru, wg, w_out[l], rw, latent=False)
    x1_s, h2_s, lt_s = _mixer_call(
        x_sample, mod, norms, w_in[l], lru, wg, w_out[l], rw, latent=True,
        extra=(state_lru, state_ret, cos2, sin2), layer=l)

    pos_p, gate_p, pos_s, gate_s = _route_call(lt_p, lt_s, cap_p, cap_s)
    xs_p, gsl_p = _dispatch_call(pos_p, gate_p, h2_p, cap_p)
    xs_s, gsl_s = _dispatch_call(pos_s, gate_s, h2_s, cap_s)
    ye = _expert_call(xs_p, xs_s, gsl_p, gsl_s, exp_w_gate[l], exp_w_up[l], exp_w_down[l])

    norm_post = norm_ffn_post[l][None]
    y_p = _combine_call(pos_p, ye, x1_p, mod, norm_post, cap_p, 0, latent=False)
    y_s = _combine_call(pos_s, ye, x1_s, mod, norm_post, cap_s, bp * cap_p, latent=True)
    return (y_p, y_s, st_lru, st_ret)
```

```python
import functools
import math
import types

import jax
import jax.numpy as jnp
import numpy as np
from jax import lax
from jax.experimental import pallas as pl
from jax.experimental.pallas import tpu as pltpu

D = 1024
LRU_W = 512
LRU_HEADS = 8
LRU_HD = 64
LRU_C = 8.0
RET_W = 512
RET_H = 4
DK = 128
N_EXP = 16
FF = 2048
N_MOD = 6
EPS = 1e-6
GRID_W = 64
ROPE_BASE = 10000.0
RET_DECAY_OFFSET_FWD = 5.0
RET_DECAY_OFFSET_BWD = 5.5

ROW_TILE = 256
FF_CHUNK = 512
COMBINE_TILE = 512
CONTEXT_GROUP = 2
SMALL_STEP_TOKENS = 1024
SUBLANES = 8
LANES = 128
VMEM_LIMIT = 60 * 1024 * 1024

DEC_LOG_F, DEC_LOG_B, DEC_G_F, DEC_GN_B, DEC_GN1_F = 0, 1, 2, 3, 4

_BF = jnp.bfloat16
_F32 = jnp.float32


def _sigmoid(x):
    return 0.5 * jnp.tanh(0.5 * x) + 0.5


def _silu(x):
    return x * _sigmoid(x)


def _gelu_tanh(x):
    c = math.sqrt(2.0 / math.pi)
    return 0.5 * x * (1.0 + jnp.tanh(c * (x + 0.044715 * (x * x * x))))


def _rms(x, gain):
    return x * lax.rsqrt(jnp.mean(x * x, axis=-1, keepdims=True) + EPS) * gain


def _dot(a, b):
    return jnp.dot(a, b, preferred_element_type=_F32)


def _dot_nt(a, b):
    return lax.dot_general(a, b, (((1,), (1,)), ((), ())), preferred_element_type=_F32)


def _dot_tn(a, b):
    return lax.dot_general(a, b, (((0,), (0,)), ((), ())), preferred_element_type=_F32)


def _ada_kernel(cc_ref, c_ref, wa_ref, wb_ref, b_ref, o_ref, s_ref):
    nb = c_ref.shape[0]
    s_ref[...] = jnp.zeros_like(s_ref)
    s_ref[0:1, :] = _silu(cc_ref[...])
    s_ref[1:1 + nb, :] = _silu(c_ref[...])
    o_ref[...] = (_dot(s_ref[:, 0:D // 2], wa_ref[...]) + _dot(s_ref[:, D // 2:], wb_ref[...])
                  + b_ref[...])


def _ada_call(c_ctx, c, ada_w, ada_b):
    nb = c.shape[0]
    assert nb + 1 <= SUBLANES
    return pl.pallas_call(
        _ada_kernel,
        out_shape=jax.ShapeDtypeStruct((N_MOD, SUBLANES, D), _F32),
        grid=(N_MOD,),
        in_specs=[
            pl.BlockSpec((1, D), lambda j: (0, 0)),
            pl.BlockSpec((nb, D), lambda j: (0, 0)),
            pl.BlockSpec((D // 2, D), lambda j: (0, j)),
            pl.BlockSpec((D // 2, D), lambda j: (1, j)),
            pl.BlockSpec((1, D), lambda j: (0, j)),
        ],
        out_specs=pl.BlockSpec((None, SUBLANES, D), lambda j: (j, 0, 0)),
        scratch_shapes=[pltpu.VMEM((SUBLANES, D), _F32)],
        compiler_params=pltpu.CompilerParams(
            dimension_semantics=("arbitrary",), vmem_limit_bytes=VMEM_LIMIT),
        name="ada_mod",
    )(c_ctx, c, ada_w, ada_w, ada_b)


def _tile_loop(nt, body, unroll=1):
    if nt == 1:
        body(0, 0)
    else:
        lax.fori_loop(0, nt, body, 0, unroll=unroll)


def _mixer_kernel(*refs, n, g, latent):
    if latent:
        (x_ref, mod_ref, n1_ref, n2_ref, n3_ref, wl_ref, wqk_ref, wvg_ref,
         cw_ref, cb_ref, ba_ref, bi_ref, lam_ref, wg_ref, wout_ref, rw_ref,
         h0_ref, s0_ref, cos_ref, sin_ref,
         x1_ref, h2_ref, lt_ref,
         xlp_g, gy_g, af_g, ab_g, hf_g, hb_g) = refs
        sg_g, qf_g, qb_g, kf_g, kb_g, v_g = xlp_g, af_g, ab_g, hf_g, hb_g, gy_g
    else:
        (x_ref, mod_ref, n1_ref, n2_ref, n3_ref, wl_ref, wqk_ref, wvg_ref,
         cw_ref, cb_ref, ba_ref, bi_ref, lam_ref, wg_ref, wout_ref, rw_ref,
         x1_ref, h2_ref, lt_ref, stl_ref, str_ref,
         xlp_g, gy_g, af_g, ab_g, hf_g, hb_g, sg_g, qf_g, qb_g, kf_g, kb_g, v_g) = refs
    early_ret = not latent

    seqs = []
    for s in range(g):
        q = types.SimpleNamespace(
            x=x_ref.at[s], x1=x1_ref.at[s], h2=h2_ref.at[s], lt=lt_ref.at[s], mix=x1_ref.at[s],
            xlp=xlp_g.at[s], sg=sg_g.at[s], gy=gy_g.at[s], v=v_g.at[s],
            a_f=af_g.at[s], a_b=ab_g.at[s], h_f=hf_g.at[s], h_b=hb_g.at[s],
            q_f=qf_g.at[s], q_b=qb_g.at[s], k_f=kf_g.at[s], k_b=kb_g.at[s])
        if latent:
            q.h0, q.s0 = h0_ref.at[s], s0_ref.at[s]
        else:
            q.stl, q.str = stl_ref.at[s], str_ref.at[s]
        seqs.append(q)

    nt = n // ROW_TILE
    decay = _decay_consts(n)
    mod_row = (pl.program_id(0) + 1) if latent else 0
    mod = lambda k: mod_ref[k, pl.ds(mod_row, 1), :]
    shift = mod(0)
    scale = n1_ref[...] * (1.0 + mod(1))

    def normed(q, rows):
        x = q.x[rows, :]
        return x * lax.rsqrt(jnp.mean(x * x, axis=-1, keepdims=True) + EPS) * scale + shift

    def project_retention(q, r0, rows, h):
        pqk = _dot(h, wqk_ref[...])
        pvg = _dot(h, wvg_ref[...])
        q.sg[rows, :] = _silu(pvg[:, RET_W:])
        tpos = (r0 + lax.broadcasted_iota(jnp.int32, (ROW_TILE, DK), 0)).astype(_F32)
        if latent:
            cos2 = cos_ref[rows, :]
            sin2 = sin_ref[rows, :]
        for hd in range(RET_H):
            cols = slice(hd * DK, (hd + 1) * DK)
            qh = pqk[:, cols] * (DK ** -0.5)
            kh = pqk[:, RET_W + hd * DK:RET_W + (hd + 1) * DK]
            if latent:
                qh = qh * cos2 + pltpu.roll(qh, DK // 2, axis=1) * sin2
                kh = kh * cos2 + pltpu.roll(kh, DK // 2, axis=1) * sin2
            lgf = decay[hd][DEC_LOG_F]
            lgb = decay[hd][DEC_LOG_B]
            q.q_f[hd, rows, :] = qh * jnp.exp(tpos * lgf)
            q.k_f[hd, rows, :] = kh * jnp.exp(tpos * (-lgf))
            q.q_b[hd, rows, :] = qh * jnp.exp(tpos * (-lgb))
            q.k_b[hd, rows, :] = kh * jnp.exp(tpos * lgb)
        q.v[rows, :] = pvg[:, 0:RET_W]

    def phase_a1(q):
        def body(r, carry):
            r0 = pl.multiple_of(r * ROW_TILE, ROW_TILE)
            rows = pl.ds(r0, ROW_TILE)
            h = normed(q, rows)
            p = _dot(h, wl_ref[...])
            q.xlp[pl.ds(r0 + SUBLANES, ROW_TILE), :] = p[:, 0:LRU_W]
            q.gy[rows, :] = _gelu_tanh(p[:, LRU_W:])
            if early_ret:
                project_retention(q, r0, rows, h)
            return carry
        return body

    for q in seqs:
        q.xlp[0:SUBLANES, :] = jnp.zeros((SUBLANES, LRU_W), _F32)
        q.xlp[n + SUBLANES:n + 2 * SUBLANES, :] = jnp.zeros((SUBLANES, LRU_W), _F32)

    half = LRU_W // 2

    def softplus_neg(lam):
        z = -lam
        return jnp.maximum(z, 0.0) + jnp.log1p(jnp.exp(-jnp.abs(z)))

    sp = (softplus_neg(lam_ref[0:1, :]), softplus_neg(lam_ref[1:2, :]))

    def phase_b(q):
        def body(r, carry):
            r0 = pl.multiple_of(r * ROW_TILE, ROW_TILE)
            rows = pl.ds(r0, ROW_TILE)
            ext = q.xlp[pl.ds(r0, ROW_TILE + 2 * SUBLANES), :]
            xc = cb_ref[...]
            for tap in range(4):
                o = SUBLANES - 2 + tap
                xc = xc + ext[o:o + ROW_TILE, :] * cw_ref[tap:tap + 1, :]
            xh = 0.5 * xc
            for d, (a_ref, u_ref) in enumerate(((q.a_f, q.h_f), (q.a_b, q.h_b))):
                bah = 0.5 * ba_ref[d:d + 1, :]
                bih = 0.5 * bi_ref[d:d + 1, :]
                ch = (-0.5 * LRU_C) * sp[d]
                for hh in range(2):
                    cs = slice(hh * half, (hh + 1) * half)
                    pre = _dot(xc[:, cs], wg_ref[d, hh])
                    t_r = jnp.tanh(pre[:, 0:half] + bah[:, cs])
                    t_i = jnp.tanh(pre[:, half:] + bih[:, cs])
                    log_a = t_r * ch[:, cs] + ch[:, cs]
                    a = jnp.exp(log_a)
                    om = -jnp.tanh(log_a) * (a * a + 1.0)
                    root = jnp.where(om > 0.0, om * lax.rsqrt(om), 0.0)
                    u = root * (t_i * xh[:, cs] + xh[:, cs])
                    for j in range(half // LANES):
                        lanes = slice(j * LANES, (j + 1) * LANES)
                        a_ref[hh * (half // LANES) + j, rows, :] = a[:, lanes]
                        u_ref[hh * (half // LANES) + j, rows, :] = u[:, lanes]
            return carry
        return body

    for q in seqs:
        _tile_loop(nt, phase_a1(q))
    for q in seqs:
        _tile_loop(nt, phase_b(q))

    row8 = lax.broadcasted_iota(jnp.int32, (SUBLANES, LANES), 0)
    block = SUBLANES * SUBLANES
    n_blocks = n // block
    n_slabs = LRU_W // LANES

    def across_groups(a, b, reverse):
        for s in (1, 2, 4):
            m = (row8 < SUBLANES - s) if reverse else (row8 >= s)
            shift = SUBLANES - s if reverse else s
            a_s = jnp.where(m, pltpu.roll(a, shift, axis=0), 1.0)
            b_s = jnp.where(m, pltpu.roll(b, shift, axis=0), 0.0)
            b = a * b_s + b
            a = a * a_s
        return a, b

    def scan_block(a_ref, h_ref, base, carry, reverse):
        rows = [pl.ds(base + k, SUBLANES, stride=SUBLANES) for k in range(SUBLANES)]
        order = list(reversed(range(SUBLANES))) if reverse else list(range(SUBLANES))
        prod, local = {}, {}
        prev = None
        for k in order:
            a, u = a_ref[rows[k], :], h_ref[rows[k], :]
            prod[k] = a if prev is None else a * prod[prev]
            local[k] = u if prev is None else a * local[prev] + u
            prev = k
        p_all, h_all = across_groups(prod[prev], local[prev], reverse)
        inner = (row8 < SUBLANES - 1) if reverse else (row8 >= 1)
        shift = SUBLANES - 1 if reverse else 1
        enter = (jnp.where(inner, pltpu.roll(p_all, shift, axis=0), 1.0) * carry
                 + jnp.where(inner, pltpu.roll(h_all, shift, axis=0), 0.0))
        for k in order:
            h_ref[rows[k], :] = prod[k] * enter + local[k]
        leave = p_all * carry + h_all
        return leave[0:1, :] if reverse else leave[SUBLANES - 1:SUBLANES, :]

    def initial(q, d, s):
        if latent:
            return q.h0[d:d + 1, s * LANES:(s + 1) * LANES]
        return jnp.zeros((1, LANES), _F32)

    chains = [(q, s) for q in seqs for s in range(n_slabs)]

    def scan_body(i, carry):
        fwd_base = pl.multiple_of(i * block, block)
        bwd_base = pl.multiple_of((n_blocks - 1 - i) * block, block)
        out = []
        for (q, s), (cf, cb) in zip(chains, carry):
            out.append((scan_block(q.a_f.at[s], q.h_f.at[s], fwd_base, cf, False),
                        scan_block(q.a_b.at[s], q.h_b.at[s], bwd_base, cb, True)))
        return tuple(out)

    ends = lax.fori_loop(0, n_blocks, scan_body,
                         tuple((initial(q, 0, s), initial(q, 1, s)) for q, s in chains), unroll=2)

    def lru_out(q):
        def body(r, carry):
            rows = pl.ds(pl.multiple_of(r * ROW_TILE, ROW_TILE), ROW_TILE)
            for s in range(n_slabs):
                lanes = slice(s * LANES, (s + 1) * LANES)
                q.mix[rows, lanes] = (q.h_f[s, rows, :] + q.h_b[s, rows, :]) * q.gy[rows, lanes]
            return carry
        return body

    for (q, s), (last_f, first_b) in zip(chains, ends):
        if not latent:
            q.stl[0:1, s * LANES:(s + 1) * LANES] = last_f
            q.stl[1:2, s * LANES:(s + 1) * LANES] = first_b
    for q in seqs:
        _tile_loop(nt, lru_out(q))

    def phase_a2(q):
        def body(r, carry):
            r0 = pl.multiple_of(r * ROW_TILE, ROW_TILE)
            rows = pl.ds(r0, ROW_TILE)
            project_retention(q, r0, rows, normed(q, rows))
            return carry
        return body

    if not early_ret:
        for q in seqs:
            _tile_loop(nt, phase_a2(q))

    lower = (lax.broadcasted_iota(jnp.int32, (ROW_TILE, ROW_TILE), 0)
             >= lax.broadcasted_iota(jnp.int32, (ROW_TILE, ROW_TILE), 1))
    blocks = [slice(r * ROW_TILE, (r + 1) * ROW_TILE) for r in range(nt)]

    def phase_c(q, hd):
        cols = slice(hd * DK, (hd + 1) * DK)
        dec = lambda row: decay[hd][row]
        kv_f = [_dot_tn(q.k_f[hd, rows, :], q.v[rows, cols]) if (r < nt - 1 or not latent) else 0.0
                for r, rows in enumerate(blocks)]
        kv_b = [_dot_tn(q.k_b[hd, rows, :], q.v[rows, cols]) if (r > 0 or not latent) else 0.0
                for r, rows in enumerate(blocks)]
        if latent:
            run_f = q.s0[0, hd] * dec(DEC_G_F)
            run_b = q.s0[1, hd] * dec(DEC_GN_B)
        else:
            run_f = run_b = None
        before = []
        for r in range(nt):
            before.append(run_f)
            run_f = kv_f[r] if run_f is None else run_f + kv_f[r]
        after = [None] * nt
        for r in reversed(range(nt)):
            after[r] = run_b
            run_b = kv_b[r] if run_b is None else run_b + kv_b[r]

        for r, rows in enumerate(blocks):
            qf = q.q_f[hd, rows, :]
            qb = q.q_b[hd, rows, :]
            s = jnp.where(lower, _dot_nt(qf, q.k_f[hd, rows, :]), _dot_nt(qb, q.k_b[hd, rows, :]))
            o = _dot(s, q.v[rows, cols])
            if before[r] is not None:
                o = o + _dot(qf, before[r])
            if after[r] is not None:
                o = o + _dot(qb, after[r])
            o = o * lax.rsqrt(jnp.mean(o * o, axis=-1, keepdims=True) + EPS)
            q.mix[rows, LRU_W + cols.start:LRU_W + cols.stop] = o * q.sg[rows, cols]
        if not latent:
            q.str[0, hd] = run_f * dec(DEC_GN1_F)
            q.str[1, hd] = run_b

    for hd in range(RET_H):
        for q in seqs:
            phase_c(q, hd)

    gain1 = mod(2) * n2_ref[...]
    gain2 = n3_ref[...] * (1.0 + mod(4))
    sh2 = mod(3)

    def phase_d(q):
        def body(r, carry):
            rows = pl.ds(pl.multiple_of(r * ROW_TILE, ROW_TILE), ROW_TILE)
            mix = _dot(q.mix[rows, :], wout_ref[...])
            x1 = q.x[rows, :] + mix * lax.rsqrt(jnp.mean(mix * mix, axis=-1, keepdims=True) + EPS) * gain1
            q.x1[rows, :] = x1
            h2 = x1 * lax.rsqrt(jnp.mean(x1 * x1, axis=-1, keepdims=True) + EPS) * gain2 + sh2
            q.h2[rows, :] = h2.astype(_BF)
            q.lt[:, rows] = _dot(h2, rw_ref[...]).T[0:N_EXP, :]
            return carry
        return body

    for q in seqs:
        _tile_loop(nt, phase_d(q), unroll=2)


def _mixer_call(x, mod, norms, w_in, lru, wg, w_out, rw, latent, extra=(), layer=0):
    b, n, _ = x.shape
    g = 1 if latent else CONTEXT_GROUP
    const2 = lambda i: (0, 0)
    in_specs = [
        pl.BlockSpec((g, n, D), lambda i: (i, 0, 0)),
        pl.BlockSpec((N_MOD, SUBLANES, D), lambda i: (0, 0, 0)),
        pl.BlockSpec((1, D), const2), pl.BlockSpec((1, D), const2), pl.BlockSpec((1, D), const2),
        pl.BlockSpec((D, D), lambda i: (0, 0)),
        pl.BlockSpec((D, D), lambda i: (0, 1)),
        pl.BlockSpec((D, D), lambda i: (0, 2)),
        pl.BlockSpec((4, LRU_W), const2), pl.BlockSpec((1, LRU_W), const2),
        pl.BlockSpec((2, LRU_W), const2), pl.BlockSpec((2, LRU_W), const2),
        pl.BlockSpec((2, LRU_W), const2),
        pl.BlockSpec((2, 2, LRU_W // 2, LRU_W), lambda i: (0, 0, 0, 0)),
        pl.BlockSpec((D, D), const2),
        pl.BlockSpec((D, LANES), const2),
    ]
    out_shape = [
        jax.ShapeDtypeStruct((b, n, D), _F32),
        jax.ShapeDtypeStruct((b, n, D), _BF),
        jax.ShapeDtypeStruct((b, N_EXP, n), _F32),
    ]
    out_specs = [
        pl.BlockSpec((g, n, D), lambda i: (i, 0, 0)),
        pl.BlockSpec((g, n, D), lambda i: (i, 0, 0)),
        pl.BlockSpec((g, N_EXP, n), lambda i: (i, 0, 0)),
    ]
    if latent:
        in_specs += [
            pl.BlockSpec((g, None, 2, LRU_W), lambda i: (i, layer, 0, 0)),
            pl.BlockSpec((g, None, 2, RET_H, DK, DK), lambda i: (i, layer, 0, 0, 0, 0)),
            pl.BlockSpec((n, DK), const2),
            pl.BlockSpec((n, DK), const2),
        ]
    else:
        out_shape += [
            jax.ShapeDtypeStruct((b, 1, 2, LRU_W), _F32),
            jax.ShapeDtypeStruct((b, 1, 2, RET_H, DK, DK), _F32),
        ]
        out_specs += [
            pl.BlockSpec((g, None, 2, LRU_W), lambda i: (i, 0, 0, 0)),
            pl.BlockSpec((g, None, 2, RET_H, DK, DK), lambda i: (i, 0, 0, 0, 0, 0)),
        ]
    f32s = lambda shape: pltpu.VMEM((g,) + shape, _F32)
    slabs = (LRU_W // LANES, n, LANES)
    scratch = [
        f32s((n + 2 * SUBLANES, LRU_W)),
        f32s((n, LRU_W)),
        f32s(slabs), f32s(slabs),
        f32s(slabs), f32s(slabs),
    ]
    if not latent:
        scratch += [f32s((n, RET_W))] + [f32s(slabs) for _ in range(4)] + [f32s((n, RET_W))]
    return pl.pallas_call(
        functools.partial(_mixer_kernel, n=n, g=g, latent=latent),
        out_shape=out_shape,
        grid=(b // g,),
        in_specs=in_specs,
        out_specs=out_specs,
        scratch_shapes=scratch,
        compiler_params=pltpu.CompilerParams(
            dimension_semantics=("arbitrary",), vmem_limit_bytes=VMEM_LIMIT),
        name="mixer_latent" if latent else "mixer_context",
    )(x, mod, *norms, w_in, w_in, w_in, *lru, wg, w_out, rw, *extra)


def _count(mask):
    return jnp.sum(jnp.where(mask, 1.0, 0.0), axis=-1, keepdims=True)


def _probs(l3):
    bsz, _, n = l3.shape
    m = jnp.max(l3, axis=1, keepdims=True)
    e = jnp.exp(l3 - m)
    return (e / jnp.sum(e, axis=1, keepdims=True)).reshape(bsz * N_EXP, n)


def _finish_select(p, bits, thr, cap):
    rows, n = p.shape
    gt = bits > thr
    eq = bits == thr
    need = float(cap) - _count(gt)
    idx = lax.broadcasted_iota(jnp.int32, (rows, n), 1)
    nbits = int(math.log2(n))

    def idx_body(i, j):
        cand = j | (jnp.int32(1) << (nbits - 1 - i))
        return jnp.where(_count(eq & (idx < cand)) < need, cand, j)

    jlast = lax.fori_loop(0, nbits, idx_body, jnp.zeros((rows, 1), jnp.int32))
    sel = gt | (eq & (idx <= jlast))
    before = (lax.broadcasted_iota(jnp.int32, (n, n), 0)
              < lax.broadcasted_iota(jnp.int32, (n, n), 1))
    pos = _dot(jnp.where(sel, 1.0, 0.0).astype(_BF), jnp.where(before, 1.0, 0.0).astype(_BF))
    return jnp.where(sel, pos, -1.0), jnp.where(sel, p, 0.0)


def _route_kernel(lp_ref, ls_ref, pp_ref, gp_ref, ps_ref, gs_ref, *, cap_p, cap_s):
    groups = ((_probs(lp_ref[...]), float(cap_p)), (_probs(ls_ref[...]), float(cap_s)))
    bits = [pltpu.bitcast(p, jnp.int32) for p, _ in groups]

    def val_body(i, thr):
        out = []
        for b, t, (_, capf) in zip(bits, thr, groups):
            cand = t | (jnp.int32(1) << (30 - i))
            out.append(jnp.where(_count(b >= cand) >= capf, cand, t))
        return tuple(out)

    thr = lax.fori_loop(0, 31, val_body,
                        tuple(jnp.zeros((b.shape[0], 1), jnp.int32) for b in bits))
    pos, gate = _finish_select(groups[0][0], bits[0], thr[0], cap_p)
    pp_ref[...] = pos.reshape(pp_ref.shape)
    gp_ref[...] = gate.reshape(gp_ref.shape)
    pos, gate = _finish_select(groups[1][0], bits[1], thr[1], cap_s)
    ps_ref[...] = pos.reshape(ps_ref.shape)
    gs_ref[...] = gate.reshape(gs_ref.shape)


def _route_call(lt_p, lt_s, cap_p, cap_s):
    shapes = [
        jax.ShapeDtypeStruct(lt_p.shape, _F32), jax.ShapeDtypeStruct(lt_p.shape, _F32),
        jax.ShapeDtypeStruct(lt_s.shape, _F32), jax.ShapeDtypeStruct(lt_s.shape, _F32),
    ]
    return pl.pallas_call(
        functools.partial(_route_kernel, cap_p=cap_p, cap_s=cap_s),
        out_shape=shapes,
        compiler_params=pltpu.CompilerParams(vmem_limit_bytes=VMEM_LIMIT),
        name="route_select",
    )(lt_p, lt_s)


def _dispatch_kernel(pos_ref, gate_ref, h_ref, xs_ref, gs_ref, *, n, cap, g):
    slot = lax.broadcasted_iota(jnp.int32, (cap, n), 0).astype(_F32)
    for j in range(g):
        slots = slice(j * cap, (j + 1) * cap)
        parts = []
        for e in range(N_EXP):
            hit = pos_ref[j, e:e + 1, :] == slot
            parts.append(jnp.where(hit, 1.0, 0.0).astype(_BF))
            gs_ref[e, slots, :] = jnp.sum(jnp.where(hit, gate_ref[j, e:e + 1, :], 0.0),
                                          axis=-1, keepdims=True)
        onehot = jnp.concatenate(parts, axis=0)
        xs = _dot(onehot, h_ref[j]).astype(_BF)
        for e in range(N_EXP):
            xs_ref[e, slots, :] = xs[e * cap:(e + 1) * cap, :]


def _dispatch_call(pos, gate, h2, cap):
    b, n, _ = h2.shape
    g = max(1, SMALL_STEP_TOKENS // n)
    return pl.pallas_call(
        functools.partial(_dispatch_kernel, n=n, cap=cap, g=g),
        out_shape=[
            jax.ShapeDtypeStruct((N_EXP, b * cap, D), _BF),
            jax.ShapeDtypeStruct((N_EXP, b * cap, 1), _F32),
        ],
        grid=(b // g,),
        in_specs=[
            pl.BlockSpec((g, N_EXP, n), lambda i: (i, 0, 0)),
            pl.BlockSpec((g, N_EXP, n), lambda i: (i, 0, 0)),
            pl.BlockSpec((g, n, D), lambda i: (i, 0, 0)),
        ],
        out_specs=[
            pl.BlockSpec((N_EXP, g * cap, D), lambda i: (0, i, 0)),
            pl.BlockSpec((N_EXP, g * cap, 1), lambda i: (0, i, 0)),
        ],
        compiler_params=pltpu.CompilerParams(
            dimension_semantics=("arbitrary",), vmem_limit_bytes=VMEM_LIMIT),
        name="dispatch",
    )(pos, gate, h2)


def _expert_kernel(xp_ref, xs_ref, gp_ref, gs_ref, wg_ref, wu_ref, wd_ref, y_ref, xcat, acc,
                   *, sp, nf, tf):
    f = pl.program_id(1)
    xcat[0:sp, :] = xp_ref[...]
    xcat[sp:, :] = xs_ref[...]
    x = xcat[...]
    total = jnp.where(f == 0, 0.0, acc[...])
    for c in range(tf // FF_CHUNK):
        cs = slice(c * FF_CHUNK, (c + 1) * FF_CHUNK)
        hg = _dot(x, wg_ref[:, cs].astype(_BF))
        hu = _dot(x, wu_ref[:, cs].astype(_BF))
        hid = (_silu(hg) * hu).astype(_BF)
        total = total + _dot(hid, wd_ref[cs, :].astype(_BF))
    acc[...] = total
    y_ref[0:sp, :] = (total[0:sp, :] * gp_ref[...]).astype(_BF)
    y_ref[sp:, :] = (total[sp:, :] * gs_ref[...]).astype(_BF)


def _expert_call(xs_p, xs_s, g_p, g_s, w_gate, w_up, w_down):
    tf = 1024
    sp = xs_p.shape[1]
    ss = xs_s.shape[1]
    nf = FF // tf
    return pl.pallas_call(
        functools.partial(_expert_kernel, sp=sp, nf=nf, tf=tf),
        out_shape=jax.ShapeDtypeStruct((N_EXP, sp + ss, D), _BF),
        grid=(N_EXP, nf),
        in_specs=[
            pl.BlockSpec((None, sp, D), lambda e, f: (e, 0, 0)),
            pl.BlockSpec((None, ss, D), lambda e, f: (e, 0, 0)),
            pl.BlockSpec((None, sp, 1), lambda e, f: (e, 0, 0)),
            pl.BlockSpec((None, ss, 1), lambda e, f: (e, 0, 0)),
            pl.BlockSpec((None, D, tf), lambda e, f: (e, 0, f)),
            pl.BlockSpec((None, D, tf), lambda e, f: (e, 0, f)),
            pl.BlockSpec((None, tf, D), lambda e, f: (e, f, 0)),
        ],
        out_specs=pl.BlockSpec((None, sp + ss, D), lambda e, f: (e, 0, 0)),
        scratch_shapes=[pltpu.VMEM((sp + ss, D), _BF), pltpu.VMEM((sp + ss, D), _F32)],
        compiler_params=pltpu.CompilerParams(
            dimension_semantics=("arbitrary", "arbitrary"), vmem_limit_bytes=VMEM_LIMIT),
        name="expert_ffn",
    )(xs_p, xs_s, g_p, g_s, w_gate, w_up, w_down)


def _combine_kernel(pos_ref, ye_ref, x1_ref, mod_ref, n4_ref, y_ref, *, n, cap, g, latent):
    width = N_EXP * cap
    lane = lax.broadcasted_iota(jnp.int32, (N_EXP, width), 1)
    expand = jnp.where(lane // cap == lax.broadcasted_iota(jnp.int32, (N_EXP, width), 0),
                       1.0, 0.0).astype(_BF)
    tile = min(n, COMBINE_TILE)
    slot = (lax.broadcasted_iota(jnp.int32, (tile, width), 1) % cap).astype(_F32)
    mod_row = (pl.program_id(0) + 1) if latent else 0
    gain = mod_ref[N_MOD - 1, pl.ds(mod_row, 1), :] * n4_ref[...]

    for j in range(g):
        ye = ye_ref[:, j * cap:(j + 1) * cap, :].reshape(width, D)

        def body(r, carry, j=j, ye=ye):
            rows = pl.ds(pl.multiple_of(r * tile, tile), tile)
            pos_e = _dot_tn(pos_ref[j, :, rows].astype(_BF), expand)
            onehot = jnp.where(pos_e == slot, 1.0, 0.0).astype(_BF)
            f = _dot(onehot, ye)
            y_ref[j, rows, :] = x1_ref[j, rows, :] + (
                f * lax.rsqrt(jnp.mean(f * f, axis=-1, keepdims=True) + EPS) * gain)
            return carry

        _tile_loop(n // tile, body)


def _combine_call(pos, ye, x1, mod, norm_post, cap, slot_off, latent):
    b, n, _ = x1.shape
    g = max(1, SMALL_STEP_TOKENS // n)
    assert g == 1 or not latent
    blk_off = slot_off // (g * cap)
    return pl.pallas_call(
        functools.partial(_combine_kernel, n=n, cap=cap, g=g, latent=latent),
        out_shape=jax.ShapeDtypeStruct((b, n, D), _F32),
        grid=(b // g,),
        in_specs=[
            pl.BlockSpec((g, N_EXP, n), lambda i: (i, 0, 0)),
            pl.BlockSpec((N_EXP, g * cap, D), lambda i: (0, i + blk_off, 0)),
            pl.BlockSpec((g, n, D), lambda i: (i, 0, 0)),
            pl.BlockSpec((N_MOD, SUBLANES, D), lambda i: (0, 0, 0)),
            pl.BlockSpec((1, D), lambda i: (0, 0)),
        ],
        out_specs=pl.BlockSpec((g, n, D), lambda i: (i, 0, 0)),
        compiler_params=pltpu.CompilerParams(
            dimension_semantics=("arbitrary",), vmem_limit_bytes=VMEM_LIMIT),
        name="combine_latent" if latent else "combine_context",
    )(pos, ye, x1, mod, norm_post)


def _block_diag_gates(wa, wi):
    per_half = LRU_HEADS // 2
    side = per_half * LRU_HD
    on_diag = (np.arange(side)[:, None] // LRU_HD) == (np.arange(side)[None, :] // LRU_HD)

    def bd(w):
        rows = w.reshape(2, 2, side, LRU_HD)
        return jnp.where(on_diag, jnp.tile(rows, (1, 1, 1, per_half)), 0.0)

    return 0.5 * jnp.concatenate([bd(wa), bd(wi)], axis=-1)


def _rope_tables(n):
    rows = n // GRID_W
    row = np.repeat(np.arange(rows, dtype=np.float32), GRID_W)
    col = np.tile(np.arange(GRID_W, dtype=np.float32), rows)
    nf = DK // 4
    freqs = np.float32(ROPE_BASE) ** (-np.arange(nf, dtype=np.float32) / np.float32(nf))
    ang = np.concatenate([row[:, None] * freqs, col[:, None] * freqs], axis=-1).astype(np.float32)
    cos = np.cos(ang)
    sin = np.sin(ang)
    return (jnp.asarray(np.concatenate([cos, cos], axis=-1), _F32),
            jnp.asarray(np.concatenate([-sin, sin], axis=-1), _F32))


def _decay_consts(n):
    heads = np.arange(RET_H, dtype=np.float32)
    f32 = np.float32
    lgf = np.log1p(-np.exp2(-(f32(RET_DECAY_OFFSET_FWD) + heads))).astype(f32)
    lgb = np.log1p(-np.exp2(-(f32(RET_DECAY_OFFSET_BWD) + heads))).astype(f32)
    tab = np.stack([lgf, lgb, np.exp(lgf), np.exp(f32(n) * lgb), np.exp(f32(n - 1) * lgf)], axis=1)
    return [[float(v) for v in row] for row in tab.astype(f32)]


def kernel(x_prompt, x_sample, c, state_lru, state_ret, c_ctx, ada_w, ada_b, norm_mix_pre, norm_mix_post, norm_ffn_pre, norm_ffn_post, w_in, conv_w, conv_b, lru_wa, lru_ba, lru_wi, lru_bi, lru_lambda, w_out, router_w, exp_w_gate, exp_w_up, exp_w_down):
    bp, n_p, _ = x_prompt.shape
    bs, n_s, _ = x_sample.shape
    cap_p = 2 * n_p // N_EXP
    cap_s = 2 * n_s // N_EXP
    l = 0

    mod = _ada_call(c_ctx[None, :], c, ada_w[l], ada_b[l][None, :])

    norms = (norm_mix_pre[l][None], norm_mix_post[l][None], norm_ffn_pre[l][None])
    lru = (conv_w[l], conv_b[l][None], lru_ba[l], lru_bi[l], lru_lambda[l])
    wg = _block_diag_gates(lru_wa[l], lru_wi[l])
    rw = jnp.pad(router_w[l], ((0, 0), (0, LANES - N_EXP)))
    cos2, sin2 = _rope_tables(n_s)

    x1_p, h2_p, lt_p, st_lru, st_ret = _mixer_call(
        x_prompt, mod, norms, w_in[l], lru, wg, w_out[l], rw, latent=False)
    x1_s, h2_s, lt_s = _mixer_call(
        x_sample, mod, norms, w_in[l], lru, wg, w_out[l], rw, latent=True,
        extra=(state_lru, state_ret, cos2, sin2), layer=l)

    pos_p, gate_p, pos_s, gate_s = _route_call(lt_p, lt_s, cap_p, cap_s)
    xs_p, gsl_p = _dispatch_call(pos_p, gate_p, h2_p, cap_p)
    xs_s, gsl_s = _dispatch_call(pos_s, gate_s, h2_s, cap_s)
    ye = _expert_call(xs_p, xs_s, gsl_p, gsl_s, exp_w_gate[l], exp_w_up[l], exp_w_down[l])

    norm_post = norm_ffn_post[l][None]
    y_p = _combine_call(pos_p, ye, x1_p, mod, norm_post, cap_p, 0, latent=False)
    y_s = _combine_call(pos_s, ye, x1_s, mod, norm_post, cap_s, bp * cap_p, latent=True)
    return (y_p, y_s, st_lru, st_ret)
```

```python
import functools
import math
import types

import jax
import jax.numpy as jnp
import numpy as np
from jax import lax
from jax.experimental import pallas as pl
from jax.experimental.pallas import tpu as pltpu

D = 1024
LRU_W = 512
LRU_HEADS = 8
LRU_HD = 64
LRU_C = 8.0
RET_W = 512
RET_H = 4
DK = 128
N_EXP = 16
FF = 2048
N_MOD = 6
EPS = 1e-6
GRID_W = 64
ROPE_BASE = 10000.0
RET_DECAY_OFFSET_FWD = 5.0
RET_DECAY_OFFSET_BWD = 5.5

ROW_TILE = 256
FF_CHUNK = 512
COMBINE_TILE = 512
CONTEXT_GROUP = 2
SMALL_STEP_TOKENS = 1024
SUBLANES = 8
LANES = 128
VMEM_LIMIT = 60 * 1024 * 1024

DEC_LOG_F, DEC_LOG_B, DEC_G_F, DEC_GN_B, DEC_GN1_F = 0, 1, 2, 3, 4

_BF = jnp.bfloat16
_F32 = jnp.float32


def _sigmoid(x):
    return 0.5 * jnp.tanh(0.5 * x) + 0.5


def _silu(x):
    return x * _sigmoid(x)


def _gelu_tanh(x):
    c = math.sqrt(2.0 / math.pi)
    return 0.5 * x * (1.0 + jnp.tanh(c * (x + 0.044715 * (x * x * x))))


def _rms(x, gain):
    return x * lax.rsqrt(jnp.mean(x * x, axis=-1, keepdims=True) + EPS) * gain


def _dot(a, b):
    return jnp.dot(a, b, preferred_element_type=_F32)


def _dot_nt(a, b):
    return lax.dot_general(a, b, (((1,), (1,)), ((), ())), preferred_element_type=_F32)


def _dot_tn(a, b):
    return lax.dot_general(a, b, (((0,), (0,)), ((), ())), preferred_element_type=_F32)


def _ada_kernel(cc_ref, c_ref, w_ref, b_ref, o_ref, s_ref):
    nb = c_ref.shape[0]
    s_ref[...] = jnp.zeros_like(s_ref)
    s_ref[0:1, :] = _silu(cc_ref[...])
    s_ref[1:1 + nb, :] = _silu(c_ref[...])
    o_ref[...] = _dot(s_ref[...], w_ref[...]) + b_ref[...]


def _ada_call(c_ctx, c, ada_w, ada_b):
    nb = c.shape[0]
    assert nb + 1 <= SUBLANES
    return pl.pallas_call(
        _ada_kernel,
        out_shape=jax.ShapeDtypeStruct((N_MOD, SUBLANES, D), _F32),
        grid=(N_MOD,),
        in_specs=[
            pl.BlockSpec((1, D), lambda j: (0, 0)),
            pl.BlockSpec((nb, D), lambda j: (0, 0)),
            pl.BlockSpec((D, D), lambda j: (0, j)),
            pl.BlockSpec((1, D), lambda j: (0, j)),
        ],
        out_specs=pl.BlockSpec((None, SUBLANES, D), lambda j: (j, 0, 0)),
        scratch_shapes=[pltpu.VMEM((SUBLANES, D), _F32)],
        compiler_params=pltpu.CompilerParams(
            dimension_semantics=("arbitrary",), vmem_limit_bytes=VMEM_LIMIT),
        name="ada_mod",
    )(c_ctx, c, ada_w, ada_b)


def _tile_loop(nt, body, unroll=1):
    if nt == 1:
        body(0, 0)
    else:
        lax.fori_loop(0, nt, body, 0, unroll=unroll)


def _mixer_kernel(*refs, n, g, latent):
    if latent:
        (x_ref, mod_ref, n1_ref, n2_ref, n3_ref, wl_ref, win_hbm,
         cw_ref, cb_ref, ba_ref, bi_ref, lam_ref, wg_ref, wout_hbm, rw_ref,
         h0_ref, s0_ref, cos_ref, sin_ref,
         x1_ref, h2_ref, lt_ref,
         xlp_g, gy_g, af_g, ab_g, hf_g, hb_g, wret_v, wout_ref, late_sem) = refs
        sg_g, qf_g, qb_g, kf_g, kb_g, v_g = xlp_g, af_g, ab_g, hf_g, hb_g, gy_g
        wqk_ref, wvg_ref = wret_v.at[:, 0:D], wret_v.at[:, D:2 * D]
        first_step = pl.program_id(0) == 0
        late_ret = pltpu.make_async_copy(win_hbm.at[:, pl.ds(D, 2 * D)], wret_v, late_sem.at[0])
        late_out = pltpu.make_async_copy(wout_hbm, wout_ref, late_sem.at[1])

        @pl.when(first_step)
        def _():
            late_ret.start()
            late_out.start()
    else:
        (x_ref, mod_ref, n1_ref, n2_ref, n3_ref, wl_ref, wqk_ref, wvg_ref,
         cw_ref, cb_ref, ba_ref, bi_ref, lam_ref, wg_ref, wout_ref, rw_ref,
         x1_ref, h2_ref, lt_ref, stl_ref, str_ref,
         xlp_g, gy_g, af_g, ab_g, hf_g, hb_g, sg_g, qf_g, qb_g, kf_g, kb_g, v_g) = refs
    early_ret = not latent

    seqs = []
    for s in range(g):
        q = types.SimpleNamespace(
            x=x_ref.at[s], x1=x1_ref.at[s], h2=h2_ref.at[s], lt=lt_ref.at[s], mix=x1_ref.at[s],
            xlp=xlp_g.at[s], sg=sg_g.at[s], gy=gy_g.at[s], v=v_g.at[s],
            a_f=af_g.at[s], a_b=ab_g.at[s], h_f=hf_g.at[s], h_b=hb_g.at[s],
            q_f=qf_g.at[s], q_b=qb_g.at[s], k_f=kf_g.at[s], k_b=kb_g.at[s])
        if latent:
            q.h0, q.s0 = h0_ref.at[s], s0_ref.at[s]
        else:
            q.stl, q.str = stl_ref.at[s], str_ref.at[s]
        seqs.append(q)

    nt = n // ROW_TILE
    decay = _decay_consts(n)
    mod_row = (pl.program_id(0) + 1) if latent else 0
    mod = lambda k: mod_ref[k, pl.ds(mod_row, 1), :]
    shift = mod(0)
    scale = n1_ref[...] * (1.0 + mod(1))

    def normed(q, rows):
        return _rms(q.x[rows, :], scale) + shift

    def project_retention(q, r0, rows, h):
        pqk = _dot(h, wqk_ref[...])
        pvg = _dot(h, wvg_ref[...])
        q.sg[rows, :] = _silu(pvg[:, RET_W:])
        tpos = (r0 + lax.broadcasted_iota(jnp.int32, (ROW_TILE, DK), 0)).astype(_F32)
        if latent:
            cos2 = cos_ref[rows, :]
            sin2 = sin_ref[rows, :]
        for hd in range(RET_H):
            cols = slice(hd * DK, (hd + 1) * DK)
            qh = pqk[:, cols] * (DK ** -0.5)
            kh = pqk[:, RET_W + hd * DK:RET_W + (hd + 1) * DK]
            if latent:
                qh = qh * cos2 + pltpu.roll(qh, DK // 2, axis=1) * sin2
                kh = kh * cos2 + pltpu.roll(kh, DK // 2, axis=1) * sin2
            lgf = decay[hd][DEC_LOG_F]
            lgb = decay[hd][DEC_LOG_B]
            q.q_f[hd, rows, :] = qh * jnp.exp(tpos * lgf)
            q.k_f[hd, rows, :] = kh * jnp.exp(tpos * (-lgf))
            q.q_b[hd, rows, :] = qh * jnp.exp(tpos * (-lgb))
            q.k_b[hd, rows, :] = kh * jnp.exp(tpos * lgb)
        q.v[rows, :] = pvg[:, 0:RET_W]

    def phase_a1(q):
        def body(r, carry):
            r0 = pl.multiple_of(r * ROW_TILE, ROW_TILE)
            rows = pl.ds(r0, ROW_TILE)
            h = normed(q, rows)
            p = _dot(h, wl_ref[...])
            q.xlp[pl.ds(r0 + SUBLANES, ROW_TILE), :] = p[:, 0:LRU_W]
            q.gy[rows, :] = _gelu_tanh(p[:, LRU_W:])
            if early_ret:
                project_retention(q, r0, rows, h)
            return carry
        return body

    for q in seqs:
        q.xlp[0:SUBLANES, :] = jnp.zeros((SUBLANES, LRU_W), _F32)
        q.xlp[n + SUBLANES:n + 2 * SUBLANES, :] = jnp.zeros((SUBLANES, LRU_W), _F32)

    half = LRU_W // 2

    def softplus_neg(lam):
        z = -lam
        return jnp.maximum(z, 0.0) + jnp.log1p(jnp.exp(-jnp.abs(z)))

    sp = (softplus_neg(lam_ref[0:1, :]), softplus_neg(lam_ref[1:2, :]))

    def phase_b(q):
        def body(r, carry):
            r0 = pl.multiple_of(r * ROW_TILE, ROW_TILE)
            rows = pl.ds(r0, ROW_TILE)
            ext = q.xlp[pl.ds(r0, ROW_TILE + 2 * SUBLANES), :]
            xc = cb_ref[...]
            for tap in range(4):
                o = SUBLANES - 2 + tap
                xc = xc + ext[o:o + ROW_TILE, :] * cw_ref[tap:tap + 1, :]
            xh = 0.5 * xc
            for d, (a_ref, u_ref) in enumerate(((q.a_f, q.h_f), (q.a_b, q.h_b))):
                bah = 0.5 * ba_ref[d:d + 1, :]
                bih = 0.5 * bi_ref[d:d + 1, :]
                ch = (-0.5 * LRU_C) * sp[d]
                for hh in range(2):
                    cs = slice(hh * half, (hh + 1) * half)
                    pre = _dot(xc[:, cs], wg_ref[d, hh])
                    t_r = jnp.tanh(pre[:, 0:half] + bah[:, cs])
                    t_i = jnp.tanh(pre[:, half:] + bih[:, cs])
                    log_a = t_r * ch[:, cs] + ch[:, cs]
                    a = jnp.exp(log_a)
                    om = -jnp.tanh(log_a) * (a * a + 1.0)
                    root = jnp.where(om > 0.0, om * lax.rsqrt(om), 0.0)
                    u = root * (t_i * xh[:, cs] + xh[:, cs])
                    for j in range(half // LANES):
                        lanes = slice(j * LANES, (j + 1) * LANES)
                        a_ref[hh * (half // LANES) + j, rows, :] = a[:, lanes]
                        u_ref[hh * (half // LANES) + j, rows, :] = u[:, lanes]
            return carry
        return body

    for q in seqs:
        _tile_loop(nt, phase_a1(q))
    for q in seqs:
        _tile_loop(nt, phase_b(q))

    row8 = lax.broadcasted_iota(jnp.int32, (SUBLANES, LANES), 0)
    block = SUBLANES * SUBLANES
    n_blocks = n // block
    n_slabs = LRU_W // LANES

    def across_groups(a, b, reverse):
        for s in (1, 2, 4):
            m = (row8 < SUBLANES - s) if reverse else (row8 >= s)
            shift = SUBLANES - s if reverse else s
            a_s = jnp.where(m, pltpu.roll(a, shift, axis=0), 1.0)
            b_s = jnp.where(m, pltpu.roll(b, shift, axis=0), 0.0)
            b = a * b_s + b
            a = a * a_s
        return a, b

    def scan_block(a_ref, h_ref, base, carry, reverse):
        rows = [pl.ds(base + k, SUBLANES, stride=SUBLANES) for k in range(SUBLANES)]
        order = list(reversed(range(SUBLANES))) if reverse else list(range(SUBLANES))
        prod, local = {}, {}
        prev = None
        for k in order:
            a, u = a_ref[rows[k], :], h_ref[rows[k], :]
            prod[k] = a if prev is None else a * prod[prev]
            local[k] = u if prev is None else a * local[prev] + u
            prev = k
        p_all, h_all = across_groups(prod[prev], local[prev], reverse)
        inner = (row8 < SUBLANES - 1) if reverse else (row8 >= 1)
        shift = SUBLANES - 1 if reverse else 1
        enter = (jnp.where(inner, pltpu.roll(p_all, shift, axis=0), 1.0) * carry
                 + jnp.where(inner, pltpu.roll(h_all, shift, axis=0), 0.0))
        for k in order:
            h_ref[rows[k], :] = prod[k] * enter + local[k]
        leave = p_all * carry + h_all
        return leave[0:1, :] if reverse else leave[SUBLANES - 1:SUBLANES, :]

    def initial(q, d, s):
        if latent:
            return q.h0[d:d + 1, s * LANES:(s + 1) * LANES]
        return jnp.zeros((1, LANES), _F32)

    chains = [(q, s) for q in seqs for s in range(n_slabs)]

    def scan_body(i, carry):
        fwd_base = pl.multiple_of(i * block, block)
        bwd_base = pl.multiple_of((n_blocks - 1 - i) * block, block)
        out = []
        for (q, s), (cf, cb) in zip(chains, carry):
            out.append((scan_block(q.a_f.at[s], q.h_f.at[s], fwd_base, cf, False),
                        scan_block(q.a_b.at[s], q.h_b.at[s], bwd_base, cb, True)))
        return tuple(out)

    ends = lax.fori_loop(0, n_blocks, scan_body,
                         tuple((initial(q, 0, s), initial(q, 1, s)) for q, s in chains), unroll=2)

    def lru_out(q):
        def body(r, carry):
            rows = pl.ds(pl.multiple_of(r * ROW_TILE, ROW_TILE), ROW_TILE)
            for s in range(n_slabs):
                lanes = slice(s * LANES, (s + 1) * LANES)
                q.mix[rows, lanes] = (q.h_f[s, rows, :] + q.h_b[s, rows, :]) * q.gy[rows, lanes]
            return carry
        return body

    for (q, s), (last_f, first_b) in zip(chains, ends):
        if not latent:
            q.stl[0:1, s * LANES:(s + 1) * LANES] = last_f
            q.stl[1:2, s * LANES:(s + 1) * LANES] = first_b
    for q in seqs:
        _tile_loop(nt, lru_out(q))

    def phase_a2(q):
        def body(r, carry):
            r0 = pl.multiple_of(r * ROW_TILE, ROW_TILE)
            rows = pl.ds(r0, ROW_TILE)
            project_retention(q, r0, rows, normed(q, rows))
            return carry
        return body

    if not early_ret:
        @pl.when(first_step)
        def _():
            late_ret.wait()

        for q in seqs:
            _tile_loop(nt, phase_a2(q))

    lower = (lax.broadcasted_iota(jnp.int32, (ROW_TILE, ROW_TILE), 0)
             >= lax.broadcasted_iota(jnp.int32, (ROW_TILE, ROW_TILE), 1))
    blocks = [slice(r * ROW_TILE, (r + 1) * ROW_TILE) for r in range(nt)]

    def phase_c(q, hd):
        cols = slice(hd * DK, (hd + 1) * DK)
        dec = lambda row: decay[hd][row]
        kv_f = [_dot_tn(q.k_f[hd, rows, :], q.v[rows, cols]) if (r < nt - 1 or not latent) else 0.0
                for r, rows in enumerate(blocks)]
        kv_b = [_dot_tn(q.k_b[hd, rows, :], q.v[rows, cols]) if (r > 0 or not latent) else 0.0
                for r, rows in enumerate(blocks)]
        if latent:
            run_f = q.s0[0, hd] * dec(DEC_G_F)
            run_b = q.s0[1, hd] * dec(DEC_GN_B)
        else:
            run_f = run_b = None
        before = []
        for r in range(nt):
            before.append(run_f)
            run_f = kv_f[r] if run_f is None else run_f + kv_f[r]
        after = [None] * nt
        for r in reversed(range(nt)):
            after[r] = run_b
            run_b = kv_b[r] if run_b is None else run_b + kv_b[r]

        for r, rows in enumerate(blocks):
            qf = q.q_f[hd, rows, :]
            qb = q.q_b[hd, rows, :]
            s = jnp.where(lower, _dot_nt(qf, q.k_f[hd, rows, :]), _dot_nt(qb, q.k_b[hd, rows, :]))
            o = _dot(s, q.v[rows, cols])
            if before[r] is not None:
                o = o + _dot(qf, before[r])
            if after[r] is not None:
                o = o + _dot(qb, after[r])
            o = o * lax.rsqrt(jnp.mean(o * o, axis=-1, keepdims=True) + EPS)
            q.mix[rows, LRU_W + cols.start:LRU_W + cols.stop] = o * q.sg[rows, cols]
        if not latent:
            q.str[0, hd] = run_f * dec(DEC_GN1_F)
            q.str[1, hd] = run_b

    for hd in range(RET_H):
        for q in seqs:
            phase_c(q, hd)

    gain1 = mod(2) * n2_ref[...]
    gain2 = n3_ref[...] * (1.0 + mod(4))
    sh2 = mod(3)

    def phase_d(q):
        def body(r, carry):
            rows = pl.ds(pl.multiple_of(r * ROW_TILE, ROW_TILE), ROW_TILE)
            mix = _dot(q.mix[rows, :], wout_ref[...])
            x1 = q.x[rows, :] + _rms(mix, gain1)
            q.x1[rows, :] = x1
            h2 = _rms(x1, gain2) + sh2
            q.h2[rows, :] = h2.astype(_BF)
            q.lt[:, rows] = _dot(h2, rw_ref[...]).T[0:N_EXP, :]
            return carry
        return body

    if latent:
        @pl.when(first_step)
        def _():
            late_out.wait()

    for q in seqs:
        _tile_loop(nt, phase_d(q), unroll=2)


def _mixer_call(x, mod, norms, w_in, lru, wg, w_out, rw, latent, extra=(), layer=0):
    b, n, _ = x.shape
    g = 1 if latent else CONTEXT_GROUP
    const2 = lambda i: (0, 0)
    whole = pl.BlockSpec(memory_space=pl.ANY)
    in_specs = [
        pl.BlockSpec((g, n, D), lambda i: (i, 0, 0)),
        pl.BlockSpec((N_MOD, SUBLANES, D), lambda i: (0, 0, 0)),
        pl.BlockSpec((1, D), const2), pl.BlockSpec((1, D), const2), pl.BlockSpec((1, D), const2),
        pl.BlockSpec((D, D), lambda i: (0, 0)),
    ]
    if latent:
        in_specs += [whole]
    else:
        in_specs += [pl.BlockSpec((D, D), lambda i: (0, 1)),
                     pl.BlockSpec((D, D), lambda i: (0, 2))]
    in_specs += [
        pl.BlockSpec((4, LRU_W), const2), pl.BlockSpec((1, LRU_W), const2),
        pl.BlockSpec((2, LRU_W), const2), pl.BlockSpec((2, LRU_W), const2),
        pl.BlockSpec((2, LRU_W), const2),
        pl.BlockSpec((2, 2, LRU_W // 2, LRU_W), lambda i: (0, 0, 0, 0)),
        whole if latent else pl.BlockSpec((D, D), const2),
        pl.BlockSpec((D, LANES), const2),
    ]
    out_shape = [
        jax.ShapeDtypeStruct((b, n, D), _F32),
        jax.ShapeDtypeStruct((b, n, D), _BF),
        jax.ShapeDtypeStruct((b, N_EXP, n), _F32),
    ]
    out_specs = [
        pl.BlockSpec((g, n, D), lambda i: (i, 0, 0)),
        pl.BlockSpec((g, n, D), lambda i: (i, 0, 0)),
        pl.BlockSpec((g, N_EXP, n), lambda i: (i, 0, 0)),
    ]
    if latent:
        in_specs += [
            pl.BlockSpec((g, None, 2, LRU_W), lambda i: (i, layer, 0, 0)),
            pl.BlockSpec((g, None, 2, RET_H, DK, DK), lambda i: (i, layer, 0, 0, 0, 0)),
            pl.BlockSpec((n, DK), const2),
            pl.BlockSpec((n, DK), const2),
        ]
    else:
        out_shape += [
            jax.ShapeDtypeStruct((b, 1, 2, LRU_W), _F32),
            jax.ShapeDtypeStruct((b, 1, 2, RET_H, DK, DK), _F32),
        ]
        out_specs += [
            pl.BlockSpec((g, None, 2, LRU_W), lambda i: (i, 0, 0, 0)),
            pl.BlockSpec((g, None, 2, RET_H, DK, DK), lambda i: (i, 0, 0, 0, 0, 0)),
        ]
    f32s = lambda shape: pltpu.VMEM((g,) + shape, _F32)
    slabs = (LRU_W // LANES, n, LANES)
    scratch = [
        f32s((n + 2 * SUBLANES, LRU_W)),
        f32s((n, LRU_W)),
        f32s(slabs), f32s(slabs),
        f32s(slabs), f32s(slabs),
    ]
    if latent:
        scratch += [pltpu.VMEM((D, 2 * D), _F32), pltpu.VMEM((D, D), _F32),
                    pltpu.SemaphoreType.DMA((2,))]
        w_in_args = (w_in, w_in)
    else:
        scratch += [f32s((n, RET_W))] + [f32s(slabs) for _ in range(4)] + [f32s((n, RET_W))]
        w_in_args = (w_in, w_in, w_in)
    return pl.pallas_call(
        functools.partial(_mixer_kernel, n=n, g=g, latent=latent),
        out_shape=out_shape,
        grid=(b // g,),
        in_specs=in_specs,
        out_specs=out_specs,
        scratch_shapes=scratch,
        compiler_params=pltpu.CompilerParams(
            dimension_semantics=("arbitrary",), vmem_limit_bytes=VMEM_LIMIT),
        name="mixer_latent" if latent else "mixer_context",
    )(x, mod, *norms, *w_in_args, *lru, wg, w_out, rw, *extra)


def _count(mask):
    return jnp.sum(jnp.where(mask, 1.0, 0.0), axis=-1, keepdims=True)


def _probs(l3):
    bsz, _, n = l3.shape
    m = jnp.max(l3, axis=1, keepdims=True)
    e = jnp.exp(l3 - m)
    return (e / jnp.sum(e, axis=1, keepdims=True)).reshape(bsz * N_EXP, n)


def _finish_select(p, bits, thr, cap):
    rows, n = p.shape
    gt = bits > thr
    eq = bits == thr
    need = float(cap) - _count(gt)
    idx = lax.broadcasted_iota(jnp.int32, (rows, n), 1)
    nbits = int(math.log2(n))

    def idx_body(i, j):
        cand = j | (jnp.int32(1) << (nbits - 1 - i))
        return jnp.where(_count(eq & (idx < cand)) < need, cand, j)

    jlast = lax.fori_loop(0, nbits, idx_body, jnp.zeros((rows, 1), jnp.int32))
    sel = gt | (eq & (idx <= jlast))
    before = (lax.broadcasted_iota(jnp.int32, (n, n), 0)
              < lax.broadcasted_iota(jnp.int32, (n, n), 1))
    pos = _dot(jnp.where(sel, 1.0, 0.0).astype(_BF), jnp.where(before, 1.0, 0.0).astype(_BF))
    return jnp.where(sel, pos, -1.0), jnp.where(sel, p, 0.0)


def _route_kernel(lp_ref, ls_ref, pp_ref, gp_ref, ps_ref, gs_ref, *, cap_p, cap_s):
    groups = ((_probs(lp_ref[...]), float(cap_p)), (_probs(ls_ref[...]), float(cap_s)))
    bits = [pltpu.bitcast(p, jnp.int32) for p, _ in groups]

    def val_body(i, thr):
        out = []
        for b, t, (_, capf) in zip(bits, thr, groups):
            cand = t | (jnp.int32(1) << (30 - i))
            out.append(jnp.where(_count(b >= cand) >= capf, cand, t))
        return tuple(out)

    thr = lax.fori_loop(0, 31, val_body,
                        tuple(jnp.zeros((b.shape[0], 1), jnp.int32) for b in bits))
    pos, gate = _finish_select(groups[0][0], bits[0], thr[0], cap_p)
    pp_ref[...] = pos.reshape(pp_ref.shape)
    gp_ref[...] = gate.reshape(gp_ref.shape)
    pos, gate = _finish_select(groups[1][0], bits[1], thr[1], cap_s)
    ps_ref[...] = pos.reshape(ps_ref.shape)
    gs_ref[...] = gate.reshape(gs_ref.shape)


def _route_call(lt_p, lt_s, cap_p, cap_s):
    shapes = [
        jax.ShapeDtypeStruct(lt_p.shape, _F32), jax.ShapeDtypeStruct(lt_p.shape, _F32),
        jax.ShapeDtypeStruct(lt_s.shape, _F32), jax.ShapeDtypeStruct(lt_s.shape, _F32),
    ]
    return pl.pallas_call(
        functools.partial(_route_kernel, cap_p=cap_p, cap_s=cap_s),
        out_shape=shapes,
        compiler_params=pltpu.CompilerParams(vmem_limit_bytes=VMEM_LIMIT),
        name="route_select",
    )(lt_p, lt_s)


def _dispatch_kernel(pos_ref, gate_ref, h_ref, xs_ref, gs_ref, *, n, cap, g):
    slot = lax.broadcasted_iota(jnp.int32, (cap, n), 0).astype(_F32)
    for j in range(g):
        slots = slice(j * cap, (j + 1) * cap)
        parts = []
        for e in range(N_EXP):
            hit = pos_ref[j, e:e + 1, :] == slot
            parts.append(jnp.where(hit, 1.0, 0.0).astype(_BF))
            gs_ref[e, slots, :] = jnp.sum(jnp.where(hit, gate_ref[j, e:e + 1, :], 0.0),
                                          axis=-1, keepdims=True)
        onehot = jnp.concatenate(parts, axis=0)
        xs = _dot(onehot, h_ref[j]).astype(_BF)
        for e in range(N_EXP):
            xs_ref[e, slots, :] = xs[e * cap:(e + 1) * cap, :]


def _dispatch_call(pos, gate, h2, cap):
    b, n, _ = h2.shape
    g = max(1, SMALL_STEP_TOKENS // n)
    return pl.pallas_call(
        functools.partial(_dispatch_kernel, n=n, cap=cap, g=g),
        out_shape=[
            jax.ShapeDtypeStruct((N_EXP, b * cap, D), _BF),
            jax.ShapeDtypeStruct((N_EXP, b * cap, 1), _F32),
        ],
        grid=(b // g,),
        in_specs=[
            pl.BlockSpec((g, N_EXP, n), lambda i: (i, 0, 0)),
            pl.BlockSpec((g, N_EXP, n), lambda i: (i, 0, 0)),
            pl.BlockSpec((g, n, D), lambda i: (i, 0, 0)),
        ],
        out_specs=[
            pl.BlockSpec((N_EXP, g * cap, D), lambda i: (0, i, 0)),
            pl.BlockSpec((N_EXP, g * cap, 1), lambda i: (0, i, 0)),
        ],
        compiler_params=pltpu.CompilerParams(
            dimension_semantics=("arbitrary",), vmem_limit_bytes=VMEM_LIMIT),
        name="dispatch",
    )(pos, gate, h2)


def _expert_kernel(xp_ref, xs_ref, gp_ref, gs_ref, wg_ref, wu_ref, wd_ref, y_ref, xcat, acc,
                   *, sp, nf, tf):
    f = pl.program_id(1)
    xcat[0:sp, :] = xp_ref[...]
    xcat[sp:, :] = xs_ref[...]
    x = xcat[...]
    total = jnp.where(f == 0, 0.0, acc[...])
    for c in range(tf // FF_CHUNK):
        cs = slice(c * FF_CHUNK, (c + 1) * FF_CHUNK)
        hg = _dot(x, wg_ref[:, cs].astype(_BF))
        hu = _dot(x, wu_ref[:, cs].astype(_BF))
        hid = (_silu(hg) * hu).astype(_BF)
        total = total + _dot(hid, wd_ref[cs, :].astype(_BF))
    acc[...] = total
    y_ref[0:sp, :] = (total[0:sp, :] * gp_ref[...]).astype(_BF)
    y_ref[sp:, :] = (total[sp:, :] * gs_ref[...]).astype(_BF)


def _expert_call(xs_p, xs_s, g_p, g_s, w_gate, w_up, w_down):
    tf = 1024
    sp = xs_p.shape[1]
    ss = xs_s.shape[1]
    nf = FF // tf
    return pl.pallas_call(
        functools.partial(_expert_kernel, sp=sp, nf=nf, tf=tf),
        out_shape=jax.ShapeDtypeStruct((N_EXP, sp + ss, D), _BF),
        grid=(N_EXP, nf),
        in_specs=[
            pl.BlockSpec((None, sp, D), lambda e, f: (e, 0, 0)),
            pl.BlockSpec((None, ss, D), lambda e, f: (e, 0, 0)),
            pl.BlockSpec((None, sp, 1), lambda e, f: (e, 0, 0)),
            pl.BlockSpec((None, ss, 1), lambda e, f: (e, 0, 0)),
            pl.BlockSpec((None, D, tf), lambda e, f: (e, 0, f)),
            pl.BlockSpec((None, D, tf), lambda e, f: (e, 0, f)),
            pl.BlockSpec((None, tf, D), lambda e, f: (e, f, 0)),
        ],
        out_specs=pl.BlockSpec((None, sp + ss, D), lambda e, f: (e, 0, 0)),
        scratch_shapes=[pltpu.VMEM((sp + ss, D), _BF), pltpu.VMEM((sp + ss, D), _F32)],
        compiler_params=pltpu.CompilerParams(
            dimension_semantics=("arbitrary", "arbitrary"), vmem_limit_bytes=VMEM_LIMIT),
        name="expert_ffn",
    )(xs_p, xs_s, g_p, g_s, w_gate, w_up, w_down)


def _combine_kernel(pos_ref, ye_ref, x1_ref, mod_ref, n4_ref, y_ref, *, n, cap, g, latent):
    width = N_EXP * cap
    lane = lax.broadcasted_iota(jnp.int32, (N_EXP, width), 1)
    expand = jnp.where(lane // cap == lax.broadcasted_iota(jnp.int32, (N_EXP, width), 0),
                       1.0, 0.0).astype(_BF)
    tile = min(n, COMBINE_TILE)
    slot = (lax.broadcasted_iota(jnp.int32, (tile, width), 1) % cap).astype(_F32)
    mod_row = (pl.program_id(0) + 1) if latent else 0
    gain = mod_ref[N_MOD - 1, pl.ds(mod_row, 1), :] * n4_ref[...]

    for j in range(g):
        ye = ye_ref[:, j * cap:(j + 1) * cap, :].reshape(width, D)

        def body(r, carry, j=j, ye=ye):
            rows = pl.ds(pl.multiple_of(r * tile, tile), tile)
            pos_e = _dot_tn(pos_ref[j, :, rows].astype(_BF), expand)
            onehot = jnp.where(pos_e == slot, 1.0, 0.0).astype(_BF)
            f = _dot(onehot, ye)
            y_ref[j, rows, :] = x1_ref[j, rows, :] + _rms(f, gain)
            return carry

        _tile_loop(n // tile, body)


def _combine_call(pos, ye, x1, mod, norm_post, cap, slot_off, latent):
    b, n, _ = x1.shape
    g = max(1, SMALL_STEP_TOKENS // n)
    assert g == 1 or not latent
    blk_off = slot_off // (g * cap)
    return pl.pallas_call(
        functools.partial(_combine_kernel, n=n, cap=cap, g=g, latent=latent),
        out_shape=jax.ShapeDtypeStruct((b, n, D), _F32),
        grid=(b // g,),
        in_specs=[
            pl.BlockSpec((g, N_EXP, n), lambda i: (i, 0, 0)),
            pl.BlockSpec((N_EXP, g * cap, D), lambda i: (0, i + blk_off, 0)),
            pl.BlockSpec((g, n, D), lambda i: (i, 0, 0)),
            pl.BlockSpec((N_MOD, SUBLANES, D), lambda i: (0, 0, 0)),
            pl.BlockSpec((1, D), lambda i: (0, 0)),
        ],
        out_specs=pl.BlockSpec((g, n, D), lambda i: (i, 0, 0)),
        compiler_params=pltpu.CompilerParams(
            dimension_semantics=("arbitrary",), vmem_limit_bytes=VMEM_LIMIT),
        name="combine_latent" if latent else "combine_context",
    )(pos, ye, x1, mod, norm_post)


def _block_diag_gates(wa, wi):
    per_half = LRU_HEADS // 2
    side = per_half * LRU_HD
    on_diag = (np.arange(side)[:, None] // LRU_HD) == (np.arange(side)[None, :] // LRU_HD)

    def bd(w):
        rows = w.reshape(2, 2, side, LRU_HD)
        return jnp.where(on_diag, jnp.tile(rows, (1, 1, 1, per_half)), 0.0)

    return 0.5 * jnp.concatenate([bd(wa), bd(wi)], axis=-1)


def _rope_tables(n):
    rows = n // GRID_W
    row = np.repeat(np.arange(rows, dtype=np.float32), GRID_W)
    col = np.tile(np.arange(GRID_W, dtype=np.float32), rows)
    nf = DK // 4
    freqs = np.float32(ROPE_BASE) ** (-np.arange(nf, dtype=np.float32) / np.float32(nf))
    ang = np.concatenate([row[:, None] * freqs, col[:, None] * freqs], axis=-1).astype(np.float32)
    cos = np.cos(ang)
    sin = np.sin(ang)
    return (jnp.asarray(np.concatenate([cos, cos], axis=-1), _F32),
            jnp.asarray(np.concatenate([-sin, sin], axis=-1), _F32))


def _decay_consts(n):
    heads = np.arange(RET_H, dtype=np.float32)
    f32 = np.float32
    lgf = np.log1p(-np.exp2(-(f32(RET_DECAY_OFFSET_FWD) + heads))).astype(f32)
    lgb = np.log1p(-np.exp2(-(f32(RET_DECAY_OFFSET_BWD) + heads))).astype(f32)
    tab = np.stack([lgf, lgb, np.exp(lgf), np.exp(f32(n) * lgb), np.exp(f32(n - 1) * lgf)], axis=1)
    return [[float(v) for v in row] for row in tab.astype(f32)]


def kernel(x_prompt, x_sample, c, state_lru, state_ret, c_ctx, ada_w, ada_b, norm_mix_pre, norm_mix_post, norm_ffn_pre, norm_ffn_post, w_in, conv_w, conv_b, lru_wa, lru_ba, lru_wi, lru_bi, lru_lambda, w_out, router_w, exp_w_gate, exp_w_up, exp_w_down):
    bp, n_p, _ = x_prompt.shape
    bs, n_s, _ = x_sample.shape
    cap_p = 2 * n_p // N_EXP
    cap_s = 2 * n_s // N_EXP
    l = 0

    mod = _ada_call(c_ctx[None, :], c, ada_w[l], ada_b[l][None, :])

    norms = (norm_mix_pre[l][None], norm_mix_post[l][None], norm_ffn_pre[l][None])
    lru = (conv_w[l], conv_b[l][None], lru_ba[l], lru_bi[l], lru_lambda[l])
    wg = _block_diag_gates(lru_wa[l], lru_wi[l])
    rw = jnp.pad(router_w[l], ((0, 0), (0, LANES - N_EXP)))
    cos2, sin2 = _rope_tables(n_s)

    x1_p, h2_p, lt_p, st_lru, st_ret = _mixer_call(
        x_prompt, mod, norms, w_in[l], lru, wg, w_out[l], rw, latent=False)
    x1_s, h2_s, lt_s = _mixer_call(
        x_sample, mod, norms, w_in[l], lru, wg, w_out[l], rw, latent=True,
        extra=(state_lru, state_ret, cos2, sin2), layer=l)

    pos_p, gate_p, pos_s, gate_s = _route_call(lt_p, lt_s, cap_p, cap_s)
    xs_p, gsl_p = _dispatch_call(pos_p, gate_p, h2_p, cap_p)
    xs_s, gsl_s = _dispatch_call(pos_s, gate_s, h2_s, cap_s)
    ye = _expert_call(xs_p, xs_s, gsl_p, gsl_s, exp_w_gate[l], exp_w_up[l], exp_w_down[l])

    norm_post = norm_ffn_post[l][None]
    y_p = _combine_call(pos_p, ye, x1_p, mod, norm_post, cap_p, 0, latent=False)
    y_s = _combine_call(pos_s, ye, x1_s, mod, norm_post, cap_s, bp * cap_p, latent=True)
    return (y_p, y_s, st_lru, st_ret)
```

```python
import functools
import math
import types

import jax
import jax.numpy as jnp
import numpy as np
from jax import lax
from jax.experimental import pallas as pl
from jax.experimental.pallas import tpu as pltpu

D = 1024
LRU_W = 512
LRU_HEADS = 8
LRU_HD = 64
LRU_C = 8.0
RET_W = 512
RET_H = 4
DK = 128
N_EXP = 16
FF = 2048
N_MOD = 6
EPS = 1e-6
GRID_W = 64
ROPE_BASE = 10000.0
RET_DECAY_OFFSET_FWD = 5.0
RET_DECAY_OFFSET_BWD = 5.5

ROW_TILE = 256
FF_CHUNK = 512
COMBINE_TILE = 512
CONTEXT_GROUP = 2
SMALL_STEP_TOKENS = 1024
SUBLANES = 8
LANES = 128
VMEM_LIMIT = 60 * 1024 * 1024

DEC_LOG_F, DEC_LOG_B, DEC_G_F, DEC_GN_B, DEC_GN1_F = 0, 1, 2, 3, 4

_BF = jnp.bfloat16
_F32 = jnp.float32


def _sigmoid(x):
    return 0.5 * jnp.tanh(0.5 * x) + 0.5


def _silu(x):
    return x * _sigmoid(x)


def _gelu_tanh(x):
    c = math.sqrt(2.0 / math.pi)
    return 0.5 * x * (1.0 + jnp.tanh(c * (x + 0.044715 * (x * x * x))))


def _rms(x, gain):
    return x * lax.rsqrt(jnp.mean(x * x, axis=-1, keepdims=True) + EPS) * gain


def _dot(a, b):
    return jnp.dot(a, b, preferred_element_type=_F32)


def _dot_nt(a, b):
    return lax.dot_general(a, b, (((1,), (1,)), ((), ())), preferred_element_type=_F32)


def _dot_tn(a, b):
    return lax.dot_general(a, b, (((0,), (0,)), ((), ())), preferred_element_type=_F32)


def _ada_kernel(cc_ref, c_ref, w_ref, b_ref, o_ref, s_ref):
    nb = c_ref.shape[0]
    s_ref[...] = jnp.zeros_like(s_ref)
    s_ref[0:1, :] = _silu(cc_ref[...])
    s_ref[1:1 + nb, :] = _silu(c_ref[...])
    o_ref[...] = _dot(s_ref[...], w_ref[...]) + b_ref[...]


def _ada_call(c_ctx, c, ada_w, ada_b):
    nb = c.shape[0]
    assert nb + 1 <= SUBLANES
    return pl.pallas_call(
        _ada_kernel,
        out_shape=jax.ShapeDtypeStruct((N_MOD, SUBLANES, D), _F32),
        grid=(N_MOD,),
        in_specs=[
            pl.BlockSpec((1, D), lambda j: (0, 0)),
            pl.BlockSpec((nb, D), lambda j: (0, 0)),
            pl.BlockSpec((D, D), lambda j: (0, j)),
            pl.BlockSpec((1, D), lambda j: (0, j)),
        ],
        out_specs=pl.BlockSpec((None, SUBLANES, D), lambda j: (j, 0, 0)),
        scratch_shapes=[pltpu.VMEM((SUBLANES, D), _F32)],
        compiler_params=pltpu.CompilerParams(
            dimension_semantics=("arbitrary",), vmem_limit_bytes=VMEM_LIMIT),
        name="ada_mod",
    )(c_ctx, c, ada_w, ada_b)


def _tile_loop(nt, body, unroll=1):
    if nt == 1:
        body(0, 0)
    else:
        lax.fori_loop(0, nt, body, 0, unroll=unroll)


def _mixer_kernel(*refs, n, g, latent):
    if latent:
        (x_ref, mod_ref, n1_ref, n2_ref, n3_ref, wl_ref, win_hbm,
         cw_ref, cb_ref, ba_ref, bi_ref, lam_ref, wg_ref, wout_hbm, rw_ref,
         h0_ref, s0_ref, cos_ref, sin_ref,
         x1_ref, h2_ref, lt_ref,
         xlp_g, gy_g, af_g, ab_g, hf_g, hb_g, wret_v, wout_ref, late_sem) = refs
        sg_g, qf_g, qb_g, kf_g, kb_g, v_g = xlp_g, af_g, ab_g, hf_g, hb_g, gy_g
        wqk_ref, wvg_ref = wret_v.at[:, 0:D], wret_v.at[:, D:2 * D]
    else:
        (x_ref, mod_ref, n1_ref, n2_ref, n3_ref, wl_ref, wqk_ref, wvg_ref,
         cw_ref, cb_ref, ba_ref, bi_ref, lam_ref, wg_ref, wout_hbm, rw_ref,
         x1_ref, h2_ref, lt_ref, stl_ref, str_ref,
         xlp_g, gy_g, af_g, ab_g, hf_g, hb_g, sg_g, qf_g, qb_g, kf_g, kb_g, v_g,
         wout_ref, late_sem) = refs
    early_ret = not latent

    first_step = pl.program_id(0) == 0
    late_out = pltpu.make_async_copy(wout_hbm, wout_ref, late_sem.at[0])
    if latent:
        late_ret = pltpu.make_async_copy(win_hbm.at[:, pl.ds(D, 2 * D)], wret_v, late_sem.at[1])

    @pl.when(first_step)
    def _():
        if latent:
            late_ret.start()
        late_out.start()

    seqs = []
    for s in range(g):
        q = types.SimpleNamespace(
            x=x_ref.at[s], x1=x1_ref.at[s], h2=h2_ref.at[s], lt=lt_ref.at[s], mix=x1_ref.at[s],
            xlp=xlp_g.at[s], sg=sg_g.at[s], gy=gy_g.at[s], v=v_g.at[s],
            a_f=af_g.at[s], a_b=ab_g.at[s], h_f=hf_g.at[s], h_b=hb_g.at[s],
            q_f=qf_g.at[s], q_b=qb_g.at[s], k_f=kf_g.at[s], k_b=kb_g.at[s])
        if latent:
            q.h0, q.s0 = h0_ref.at[s], s0_ref.at[s]
        else:
            q.stl, q.str = stl_ref.at[s], str_ref.at[s]
        seqs.append(q)

    nt = n // ROW_TILE
    decay = _decay_consts(n)
    mod_row = (pl.program_id(0) + 1) if latent else 0
    mod = lambda k: mod_ref[k, pl.ds(mod_row, 1), :]
    shift = mod(0)
    scale = n1_ref[...] * (1.0 + mod(1))

    def normed(q, rows):
        return _rms(q.x[rows, :], scale) + shift

    def project_retention(q, r0, rows, h):
        pqk = _dot(h, wqk_ref[...])
        pvg = _dot(h, wvg_ref[...])
        q.sg[rows, :] = _silu(pvg[:, RET_W:])
        tpos = (r0 + lax.broadcasted_iota(jnp.int32, (ROW_TILE, DK), 0)).astype(_F32)
        if latent:
            cos2 = cos_ref[rows, :]
            sin2 = sin_ref[rows, :]
        for hd in range(RET_H):
            cols = slice(hd * DK, (hd + 1) * DK)
            qh = pqk[:, cols] * (DK ** -0.5)
            kh = pqk[:, RET_W + hd * DK:RET_W + (hd + 1) * DK]
            if latent:
                qh = qh * cos2 + pltpu.roll(qh, DK // 2, axis=1) * sin2
                kh = kh * cos2 + pltpu.roll(kh, DK // 2, axis=1) * sin2
            lgf = decay[hd][DEC_LOG_F]
            lgb = decay[hd][DEC_LOG_B]
            q.q_f[hd, rows, :] = qh * jnp.exp(tpos * lgf)
            q.k_f[hd, rows, :] = kh * jnp.exp(tpos * (-lgf))
            q.q_b[hd, rows, :] = qh * jnp.exp(tpos * (-lgb))
            q.k_b[hd, rows, :] = kh * jnp.exp(tpos * lgb)
        q.v[rows, :] = pvg[:, 0:RET_W]

    def phase_a1(q):
        def body(r, carry):
            r0 = pl.multiple_of(r * ROW_TILE, ROW_TILE)
            rows = pl.ds(r0, ROW_TILE)
            h = normed(q, rows)
            p = _dot(h, wl_ref[...])
            q.xlp[pl.ds(r0 + SUBLANES, ROW_TILE), :] = p[:, 0:LRU_W]
            q.gy[rows, :] = _gelu_tanh(p[:, LRU_W:])
            if early_ret:
                project_retention(q, r0, rows, h)
            return carry
        return body

    for q in seqs:
        q.xlp[0:SUBLANES, :] = jnp.zeros((SUBLANES, LRU_W), _F32)
        q.xlp[n + SUBLANES:n + 2 * SUBLANES, :] = jnp.zeros((SUBLANES, LRU_W), _F32)

    half = LRU_W // 2

    def softplus_neg(lam):
        z = -lam
        return jnp.maximum(z, 0.0) + jnp.log1p(jnp.exp(-jnp.abs(z)))

    sp = (softplus_neg(lam_ref[0:1, :]), softplus_neg(lam_ref[1:2, :]))

    def phase_b(q):
        def body(r, carry):
            r0 = pl.multiple_of(r * ROW_TILE, ROW_TILE)
            rows = pl.ds(r0, ROW_TILE)
            ext = q.xlp[pl.ds(r0, ROW_TILE + 2 * SUBLANES), :]
            xc = cb_ref[...]
            for tap in range(4):
                o = SUBLANES - 2 + tap
                xc = xc + ext[o:o + ROW_TILE, :] * cw_ref[tap:tap + 1, :]
            xh = 0.5 * xc
            for d, (a_ref, u_ref) in enumerate(((q.a_f, q.h_f), (q.a_b, q.h_b))):
                bah = 0.5 * ba_ref[d:d + 1, :]
                bih = 0.5 * bi_ref[d:d + 1, :]
                ch = (-0.5 * LRU_C) * sp[d]
                for hh in range(2):
                    cs = slice(hh * half, (hh + 1) * half)
                    pre = _dot(xc[:, cs], wg_ref[d, hh])
                    t_r = jnp.tanh(pre[:, 0:half] + bah[:, cs])
                    t_i = jnp.tanh(pre[:, half:] + bih[:, cs])
                    log_a = t_r * ch[:, cs] + ch[:, cs]
                    a = jnp.exp(log_a)
                    om = -jnp.tanh(log_a) * (a * a + 1.0)
                    root = jnp.where(om > 0.0, om * lax.rsqrt(om), 0.0)
                    u = root * (t_i * xh[:, cs] + xh[:, cs])
                    for j in range(half // LANES):
                        lanes = slice(j * LANES, (j + 1) * LANES)
                        a_ref[hh * (half // LANES) + j, rows, :] = a[:, lanes]
                        u_ref[hh * (half // LANES) + j, rows, :] = u[:, lanes]
            return carry
        return body

    for q in seqs:
        _tile_loop(nt, phase_a1(q))
    for q in seqs:
        _tile_loop(nt, phase_b(q))

    row8 = lax.broadcasted_iota(jnp.int32, (SUBLANES, LANES), 0)
    block = SUBLANES * SUBLANES
    n_blocks = n // block
    n_slabs = LRU_W // LANES

    def across_groups(a, b, reverse):
        for s in (1, 2, 4):
            m = (row8 < SUBLANES - s) if reverse else (row8 >= s)
            shift = SUBLANES - s if reverse else s
            a_s = jnp.where(m, pltpu.roll(a, shift, axis=0), 1.0)
            b_s = jnp.where(m, pltpu.roll(b, shift, axis=0), 0.0)
            b = a * b_s + b
            a = a * a_s
        return a, b

    def scan_block(a_ref, h_ref, base, carry, reverse):
        rows = [pl.ds(base + k, SUBLANES, stride=SUBLANES) for k in range(SUBLANES)]
        order = list(reversed(range(SUBLANES))) if reverse else list(range(SUBLANES))
        prod, local = {}, {}
        prev = None
        for k in order:
            a, u = a_ref[rows[k], :], h_ref[rows[k], :]
            prod[k] = a if prev is None else a * prod[prev]
            local[k] = u if prev is None else a * local[prev] + u
            prev = k
        p_all, h_all = across_groups(prod[prev], local[prev], reverse)
        inner = (row8 < SUBLANES - 1) if reverse else (row8 >= 1)
        shift = SUBLANES - 1 if reverse else 1
        enter = (jnp.where(inner, pltpu.roll(p_all, shift, axis=0), 1.0) * carry
                 + jnp.where(inner, pltpu.roll(h_all, shift, axis=0), 0.0))
        for k in order:
            h_ref[rows[k], :] = prod[k] * enter + local[k]
        leave = p_all * carry + h_all
        return leave[0:1, :] if reverse else leave[SUBLANES - 1:SUBLANES, :]

    def initial(q, d, s):
        if latent:
            return q.h0[d:d + 1, s * LANES:(s + 1) * LANES]
        return jnp.zeros((1, LANES), _F32)

    chains = [(q, s) for q in seqs for s in range(n_slabs)]

    def scan_body(i, carry):
        fwd_base = pl.multiple_of(i * block, block)
        bwd_base = pl.multiple_of((n_blocks - 1 - i) * block, block)
        out = []
        for (q, s), (cf, cb) in zip(chains, carry):
            out.append((scan_block(q.a_f.at[s], q.h_f.at[s], fwd_base, cf, False),
                        scan_block(q.a_b.at[s], q.h_b.at[s], bwd_base, cb, True)))
        return tuple(out)

    ends = lax.fori_loop(0, n_blocks, scan_body,
                         tuple((initial(q, 0, s), initial(q, 1, s)) for q, s in chains), unroll=2)

    def lru_out(q):
        def body(r, carry):
            rows = pl.ds(pl.multiple_of(r * ROW_TILE, ROW_TILE), ROW_TILE)
            for s in range(n_slabs):
                lanes = slice(s * LANES, (s + 1) * LANES)
                q.mix[rows, lanes] = (q.h_f[s, rows, :] + q.h_b[s, rows, :]) * q.gy[rows, lanes]
            return carry
        return body

    for (q, s), (last_f, first_b) in zip(chains, ends):
        if not latent:
            q.stl[0:1, s * LANES:(s + 1) * LANES] = last_f
            q.stl[1:2, s * LANES:(s + 1) * LANES] = first_b
    for q in seqs:
        _tile_loop(nt, lru_out(q))

    def phase_a2(q):
        def body(r, carry):
            r0 = pl.multiple_of(r * ROW_TILE, ROW_TILE)
            rows = pl.ds(r0, ROW_TILE)
            project_retention(q, r0, rows, normed(q, rows))
            return carry
        return body

    if not early_ret:
        @pl.when(first_step)
        def _():
            late_ret.wait()

        for q in seqs:
            _tile_loop(nt, phase_a2(q))

    lower = (lax.broadcasted_iota(jnp.int32, (ROW_TILE, ROW_TILE), 0)
             >= lax.broadcasted_iota(jnp.int32, (ROW_TILE, ROW_TILE), 1))
    blocks = [slice(r * ROW_TILE, (r + 1) * ROW_TILE) for r in range(nt)]

    def phase_c(q, hd):
        cols = slice(hd * DK, (hd + 1) * DK)
        dec = lambda row: decay[hd][row]
        kv_f = [_dot_tn(q.k_f[hd, rows, :], q.v[rows, cols]) if (r < nt - 1 or not latent) else 0.0
                for r, rows in enumerate(blocks)]
        kv_b = [_dot_tn(q.k_b[hd, rows, :], q.v[rows, cols]) if (r > 0 or not latent) else 0.0
                for r, rows in enumerate(blocks)]
        if latent:
            run_f = q.s0[0, hd] * dec(DEC_G_F)
            run_b = q.s0[1, hd] * dec(DEC_GN_B)
        else:
            run_f = run_b = None
        before = []
        for r in range(nt):
            before.append(run_f)
            run_f = kv_f[r] if run_f is None else run_f + kv_f[r]
        after = [None] * nt
        for r in reversed(range(nt)):
            after[r] = run_b
            run_b = kv_b[r] if run_b is None else run_b + kv_b[r]

        for r, rows in enumerate(blocks):
            qf = q.q_f[hd, rows, :]
            qb = q.q_b[hd, rows, :]
            s = jnp.where(lower, _dot_nt(qf, q.k_f[hd, rows, :]), _dot_nt(qb, q.k_b[hd, rows, :]))
            o = _dot(s, q.v[rows, cols])
            if before[r] is not None:
                o = o + _dot(qf, before[r])
            if after[r] is not None:
                o = o + _dot(qb, after[r])
            o = o * lax.rsqrt(jnp.mean(o * o, axis=-1, keepdims=True) + EPS)
            q.mix[rows, LRU_W + cols.start:LRU_W + cols.stop] = o * q.sg[rows, cols]
        if not latent:
            q.str[0, hd] = run_f * dec(DEC_GN1_F)
            q.str[1, hd] = run_b

    for hd in range(RET_H):
        for q in seqs:
            phase_c(q, hd)

    gain1 = mod(2) * n2_ref[...]
    gain2 = n3_ref[...] * (1.0 + mod(4))
    sh2 = mod(3)

    def phase_d(q):
        def body(r, carry):
            rows = pl.ds(pl.multiple_of(r * ROW_TILE, ROW_TILE), ROW_TILE)
            mix = _dot(q.mix[rows, :], wout_ref[...])
            x1 = q.x[rows, :] + _rms(mix, gain1)
            q.x1[rows, :] = x1
            h2 = _rms(x1, gain2) + sh2
            q.h2[rows, :] = h2.astype(_BF)
            q.lt[:, rows] = _dot(h2, rw_ref[...]).T[0:N_EXP, :]
            return carry
        return body

    @pl.when(first_step)
    def _():
        late_out.wait()

    for q in seqs:
        _tile_loop(nt, phase_d(q), unroll=2)


def _mixer_call(x, mod, norms, w_in, lru, wg, w_out, rw, latent, extra=(), layer=0):
    b, n, _ = x.shape
    g = 1 if latent else CONTEXT_GROUP
    const2 = lambda i: (0, 0)
    whole = pl.BlockSpec(memory_space=pl.ANY)
    in_specs = [
        pl.BlockSpec((g, n, D), lambda i: (i, 0, 0)),
        pl.BlockSpec((N_MOD, SUBLANES, D), lambda i: (0, 0, 0)),
        pl.BlockSpec((1, D), const2), pl.BlockSpec((1, D), const2), pl.BlockSpec((1, D), const2),
        pl.BlockSpec((D, D), lambda i: (0, 0)),
    ]
    if latent:
        in_specs += [whole]
    else:
        in_specs += [pl.BlockSpec((D, D), lambda i: (0, 1)),
                     pl.BlockSpec((D, D), lambda i: (0, 2))]
    in_specs += [
        pl.BlockSpec((4, LRU_W), const2), pl.BlockSpec((1, LRU_W), const2),
        pl.BlockSpec((2, LRU_W), const2), pl.BlockSpec((2, LRU_W), const2),
        pl.BlockSpec((2, LRU_W), const2),
        pl.BlockSpec((2, 2, LRU_W // 2, LRU_W), lambda i: (0, 0, 0, 0)),
        whole,
        pl.BlockSpec((D, LANES), const2),
    ]
    out_shape = [
        jax.ShapeDtypeStruct((b, n, D), _F32),
        jax.ShapeDtypeStruct((b, n, D), _BF),
        jax.ShapeDtypeStruct((b, N_EXP, n), _F32),
    ]
    out_specs = [
        pl.BlockSpec((g, n, D), lambda i: (i, 0, 0)),
        pl.BlockSpec((g, n, D), lambda i: (i, 0, 0)),
        pl.BlockSpec((g, N_EXP, n), lambda i: (i, 0, 0)),
    ]
    if latent:
        in_specs += [
            pl.BlockSpec((g, None, 2, LRU_W), lambda i: (i, layer, 0, 0)),
            pl.BlockSpec((g, None, 2, RET_H, DK, DK), lambda i: (i, layer, 0, 0, 0, 0)),
            pl.BlockSpec((n, DK), const2),
            pl.BlockSpec((n, DK), const2),
        ]
    else:
        out_shape += [
            jax.ShapeDtypeStruct((b, 1, 2, LRU_W), _F32),
            jax.ShapeDtypeStruct((b, 1, 2, RET_H, DK, DK), _F32),
        ]
        out_specs += [
            pl.BlockSpec((g, None, 2, LRU_W), lambda i: (i, 0, 0, 0)),
            pl.BlockSpec((g, None, 2, RET_H, DK, DK), lambda i: (i, 0, 0, 0, 0, 0)),
        ]
    f32s = lambda shape: pltpu.VMEM((g,) + shape, _F32)
    slabs = (LRU_W // LANES, n, LANES)
    scratch = [
        f32s((n + 2 * SUBLANES, LRU_W)),
        f32s((n, LRU_W)),
        f32s(slabs), f32s(slabs),
        f32s(slabs), f32s(slabs),
    ]
    if latent:
        scratch += [pltpu.VMEM((D, 2 * D), _F32)]
        w_in_args = (w_in, w_in)
    else:
        scratch += [f32s((n, RET_W))] + [f32s(slabs) for _ in range(4)] + [f32s((n, RET_W))]
        w_in_args = (w_in, w_in, w_in)
    scratch += [pltpu.VMEM((D, D), _F32), pltpu.SemaphoreType.DMA((2 if latent else 1,))]
    return pl.pallas_call(
        functools.partial(_mixer_kernel, n=n, g=g, latent=latent),
        out_shape=out_shape,
        grid=(b // g,),
        in_specs=in_specs,
        out_specs=out_specs,
        scratch_shapes=scratch,
        compiler_params=pltpu.CompilerParams(
            dimension_semantics=("arbitrary",), vmem_limit_bytes=VMEM_LIMIT),
        name="mixer_latent" if latent else "mixer_context",
    )(x, mod, *norms, *w_in_args, *lru, wg, w_out, rw, *extra)


def _count(mask):
    return jnp.sum(jnp.where(mask, 1.0, 0.0), axis=-1, keepdims=True)


def _probs(l3):
    bsz, _, n = l3.shape
    m = jnp.max(l3, axis=1, keepdims=True)
    e = jnp.exp(l3 - m)
    return (e / jnp.sum(e, axis=1, keepdims=True)).reshape(bsz * N_EXP, n)


def _finish_select(p, bits, thr, cap):
    rows, n = p.shape
    gt = bits > thr
    eq = bits == thr
    need = float(cap) - _count(gt)
    idx = lax.broadcasted_iota(jnp.int32, (rows, n), 1)
    nbits = int(math.log2(n))

    def idx_body(i, j):
        cand = j | (jnp.int32(1) << (nbits - 1 - i))
        return jnp.where(_count(eq & (idx < cand)) < need, cand, j)

    jlast = lax.fori_loop(0, nbits, idx_body, jnp.zeros((rows, 1), jnp.int32))
    sel = gt | (eq & (idx <= jlast))
    before = (lax.broadcasted_iota(jnp.int32, (n, n), 0)
              < lax.broadcasted_iota(jnp.int32, (n, n), 1))
    pos = _dot(jnp.where(sel, 1.0, 0.0).astype(_BF), jnp.where(before, 1.0, 0.0).astype(_BF))
    return jnp.where(sel, pos, -1.0), jnp.where(sel, p, 0.0)


def _route_kernel(lp_ref, ls_ref, pp_ref, gp_ref, ps_ref, gs_ref, *, cap_p, cap_s):
    groups = ((_probs(lp_ref[...]), float(cap_p)), (_probs(ls_ref[...]), float(cap_s)))
    bits = [pltpu.bitcast(p, jnp.int32) for p, _ in groups]

    def val_body(i, thr):
        out = []
        for b, t, (_, capf) in zip(bits, thr, groups):
            cand = t | (jnp.int32(1) << (30 - i))
            out.append(jnp.where(_count(b >= cand) >= capf, cand, t))
        return tuple(out)

    thr = lax.fori_loop(0, 31, val_body,
                        tuple(jnp.zeros((b.shape[0], 1), jnp.int32) for b in bits))
    pos, gate = _finish_select(groups[0][0], bits[0], thr[0], cap_p)
    pp_ref[...] = pos.reshape(pp_ref.shape)
    gp_ref[...] = gate.reshape(gp_ref.shape)
    pos, gate = _finish_select(groups[1][0], bits[1], thr[1], cap_s)
    ps_ref[...] = pos.reshape(ps_ref.shape)
    gs_ref[...] = gate.reshape(gs_ref.shape)


def _route_call(lt_p, lt_s, cap_p, cap_s):
    shapes = [
        jax.ShapeDtypeStruct(lt_p.shape, _F32), jax.ShapeDtypeStruct(lt_p.shape, _F32),
        jax.ShapeDtypeStruct(lt_s.shape, _F32), jax.ShapeDtypeStruct(lt_s.shape, _F32),
    ]
    return pl.pallas_call(
        functools.partial(_route_kernel, cap_p=cap_p, cap_s=cap_s),
        out_shape=shapes,
        compiler_params=pltpu.CompilerParams(vmem_limit_bytes=VMEM_LIMIT),
        name="route_select",
    )(lt_p, lt_s)


def _dispatch_kernel(pos_ref, gate_ref, h_ref, xs_ref, gs_ref, *, n, cap, g):
    slot = lax.broadcasted_iota(jnp.int32, (cap, n), 0).astype(_F32)
    for j in range(g):
        slots = slice(j * cap, (j + 1) * cap)
        parts = []
        for e in range(N_EXP):
            hit = pos_ref[j, e:e + 1, :] == slot
            parts.append(jnp.where(hit, 1.0, 0.0).astype(_BF))
            gs_ref[e, slots, :] = jnp.sum(jnp.where(hit, gate_ref[j, e:e + 1, :], 0.0),
                                          axis=-1, keepdims=True)
        onehot = jnp.concatenate(parts, axis=0)
        xs = _dot(onehot, h_ref[j]).astype(_BF)
        for e in range(N_EXP):
            xs_ref[e, slots, :] = xs[e * cap:(e + 1) * cap, :]


def _dispatch_call(pos, gate, h2, cap):
    b, n, _ = h2.shape
    g = max(1, SMALL_STEP_TOKENS // n)
    return pl.pallas_call(
        functools.partial(_dispatch_kernel, n=n, cap=cap, g=g),
        out_shape=[
            jax.ShapeDtypeStruct((N_EXP, b * cap, D), _BF),
            jax.ShapeDtypeStruct((N_EXP, b * cap, 1), _F32),
        ],
        grid=(b // g,),
        in_specs=[
            pl.BlockSpec((g, N_EXP, n), lambda i: (i, 0, 0)),
            pl.BlockSpec((g, N_EXP, n), lambda i: (i, 0, 0)),
            pl.BlockSpec((g, n, D), lambda i: (i, 0, 0)),
        ],
        out_specs=[
            pl.BlockSpec((N_EXP, g * cap, D), lambda i: (0, i, 0)),
            pl.BlockSpec((N_EXP, g * cap, 1), lambda i: (0, i, 0)),
        ],
        compiler_params=pltpu.CompilerParams(
            dimension_semantics=("arbitrary",), vmem_limit_bytes=VMEM_LIMIT),
        name="dispatch",
    )(pos, gate, h2)


def _expert_kernel(xp_ref, xs_ref, gp_ref, gs_ref, wg_ref, wu_ref, wd_ref, y_ref, xcat, acc,
                   *, sp, nf, tf):
    f = pl.program_id(1)
    xcat[0:sp, :] = xp_ref[...]
    xcat[sp:, :] = xs_ref[...]
    x = xcat[...]
    total = jnp.where(f == 0, 0.0, acc[...])
    for c in range(tf // FF_CHUNK):
        cs = slice(c * FF_CHUNK, (c + 1) * FF_CHUNK)
        hg = _dot(x, wg_ref[:, cs].astype(_BF))
        hu = _dot(x, wu_ref[:, cs].astype(_BF))
        hid = (_silu(hg) * hu).astype(_BF)
        total = total + _dot(hid, wd_ref[cs, :].astype(_BF))
    acc[...] = total
    y_ref[0:sp, :] = (total[0:sp, :] * gp_ref[...]).astype(_BF)
    y_ref[sp:, :] = (total[sp:, :] * gs_ref[...]).astype(_BF)


def _expert_call(xs_p, xs_s, g_p, g_s, w_gate, w_up, w_down):
    tf = 1024
    sp = xs_p.shape[1]
    ss = xs_s.shape[1]
    nf = FF // tf
    return pl.pallas_call(
        functools.partial(_expert_kernel, sp=sp, nf=nf, tf=tf),
        out_shape=jax.ShapeDtypeStruct((N_EXP, sp + ss, D), _BF),
        grid=(N_EXP, nf),
        in_specs=[
            pl.BlockSpec((None, sp, D), lambda e, f: (e, 0, 0)),
            pl.BlockSpec((None, ss, D), lambda e, f: (e, 0, 0)),
            pl.BlockSpec((None, sp, 1), lambda e, f: (e, 0, 0)),
            pl.BlockSpec((None, ss, 1), lambda e, f: (e, 0, 0)),
            pl.BlockSpec((None, D, tf), lambda e, f: (e, 0, f)),
            pl.BlockSpec((None, D, tf), lambda e, f: (e, 0, f)),
            pl.BlockSpec((None, tf, D), lambda e, f: (e, f, 0)),
        ],
        out_specs=pl.BlockSpec((None, sp + ss, D), lambda e, f: (e, 0, 0)),
        scratch_shapes=[pltpu.VMEM((sp + ss, D), _BF), pltpu.VMEM((sp + ss, D), _F32)],
        compiler_params=pltpu.CompilerParams(
            dimension_semantics=("arbitrary", "arbitrary"), vmem_limit_bytes=VMEM_LIMIT),
        name="expert_ffn",
    )(xs_p, xs_s, g_p, g_s, w_gate, w_up, w_down)


def _combine_kernel(pos_ref, ye_ref, x1_ref, mod_ref, n4_ref, y_ref, *, n, cap, g, latent):
    width = N_EXP * cap
    lane = lax.broadcasted_iota(jnp.int32, (N_EXP, width), 1)
    expand = jnp.where(lane // cap == lax.broadcasted_iota(jnp.int32, (N_EXP, width), 0),
                       1.0, 0.0).astype(_BF)
    tile = min(n, COMBINE_TILE)
    slot = (lax.broadcasted_iota(jnp.int32, (tile, width), 1) % cap).astype(_F32)
    mod_row = (pl.program_id(0) + 1) if latent else 0
    gain = mod_ref[N_MOD - 1, pl.ds(mod_row, 1), :] * n4_ref[...]

    for j in range(g):
        ye = ye_ref[:, j * cap:(j + 1) * cap, :].reshape(width, D)

        def body(r, carry, j=j, ye=ye):
            rows = pl.ds(pl.multiple_of(r * tile, tile), tile)
            pos_e = _dot_tn(pos_ref[j, :, rows].astype(_BF), expand)
            onehot = jnp.where(pos_e == slot, 1.0, 0.0).astype(_BF)
            f = _dot(onehot, ye)
            y_ref[j, rows, :] = x1_ref[j, rows, :] + _rms(f, gain)
            return carry

        _tile_loop(n // tile, body)


def _combine_call(pos, ye, x1, mod, norm_post, cap, slot_off, latent):
    b, n, _ = x1.shape
    g = max(1, SMALL_STEP_TOKENS // n)
    assert g == 1 or not latent
    blk_off = slot_off // (g * cap)
    return pl.pallas_call(
        functools.partial(_combine_kernel, n=n, cap=cap, g=g, latent=latent),
        out_shape=jax.ShapeDtypeStruct((b, n, D), _F32),
        grid=(b // g,),
        in_specs=[
            pl.BlockSpec((g, N_EXP, n), lambda i: (i, 0, 0)),
            pl.BlockSpec((N_EXP, g * cap, D), lambda i: (0, i + blk_off, 0)),
            pl.BlockSpec((g, n, D), lambda i: (i, 0, 0)),
            pl.BlockSpec((N_MOD, SUBLANES, D), lambda i: (0, 0, 0)),
            pl.BlockSpec((1, D), lambda i: (0, 0)),
        ],
        out_specs=pl.BlockSpec((g, n, D), lambda i: (i, 0, 0)),
        compiler_params=pltpu.CompilerParams(
            dimension_semantics=("arbitrary",), vmem_limit_bytes=VMEM_LIMIT),
        name="combine_latent" if latent else "combine_context",
    )(pos, ye, x1, mod, norm_post)


def _block_diag_gates(wa, wi):
    per_half = LRU_HEADS // 2
    side = per_half * LRU_HD
    on_diag = (np.arange(side)[:, None] // LRU_HD) == (np.arange(side)[None, :] // LRU_HD)

    def bd(w):
        rows = w.reshape(2, 2, side, LRU_HD)
        return jnp.where(on_diag, jnp.tile(rows, (1, 1, 1, per_half)), 0.0)

    return 0.5 * jnp.concatenate([bd(wa), bd(wi)], axis=-1)


def _rope_tables(n):
    rows = n // GRID_W
    row = np.repeat(np.arange(rows, dtype=np.float32), GRID_W)
    col = np.tile(np.arange(GRID_W, dtype=np.float32), rows)
    nf = DK // 4
    freqs = np.float32(ROPE_BASE) ** (-np.arange(nf, dtype=np.float32) / np.float32(nf))
    ang = np.concatenate([row[:, None] * freqs, col[:, None] * freqs], axis=-1).astype(np.float32)
    cos = np.cos(ang)
    sin = np.sin(ang)
    return (jnp.asarray(np.concatenate([cos, cos], axis=-1), _F32),
            jnp.asarray(np.concatenate([-sin, sin], axis=-1), _F32))


def _decay_consts(n):
    heads = np.arange(RET_H, dtype=np.float32)
    f32 = np.float32
    lgf = np.log1p(-np.exp2(-(f32(RET_DECAY_OFFSET_FWD) + heads))).astype(f32)
    lgb = np.log1p(-np.exp2(-(f32(RET_DECAY_OFFSET_BWD) + heads))).astype(f32)
    tab = np.stack([lgf, lgb, np.exp(lgf), np.exp(f32(n) * lgb), np.exp(f32(n - 1) * lgf)], axis=1)
    return [[float(v) for v in row] for row in tab.astype(f32)]


def kernel(x_prompt, x_sample, c, state_lru, state_ret, c_ctx, ada_w, ada_b, norm_mix_pre, norm_mix_post, norm_ffn_pre, norm_ffn_post, w_in, conv_w, conv_b, lru_wa, lru_ba, lru_wi, lru_bi, lru_lambda, w_out, router_w, exp_w_gate, exp_w_up, exp_w_down):
    bp, n_p, _ = x_prompt.shape
    bs, n_s, _ = x_sample.shape
    cap_p = 2 * n_p // N_EXP
    cap_s = 2 * n_s // N_EXP
    l = 0

    mod = _ada_call(c_ctx[None, :], c, ada_w[l], ada_b[l][None, :])

    norms = (norm_mix_pre[l][None], norm_mix_post[l][None], norm_ffn_pre[l][None])
    lru = (conv_w[l], conv_b[l][None], lru_ba[l], lru_bi[l], lru_lambda[l])
    wg = _block_diag_gates(lru_wa[l], lru_wi[l])
    rw = jnp.pad(router_w[l], ((0, 0), (0, LANES - N_EXP)))
    cos2, sin2 = _rope_tables(n_s)

    x1_p, h2_p, lt_p, st_lru, st_ret = _mixer_call(
        x_prompt, mod, norms, w_in[l], lru, wg, w_out[l], rw, latent=False)
    x1_s, h2_s, lt_s = _mixer_call(
        x_sample, mod, norms, w_in[l], lru, wg, w_out[l], rw, latent=True,
        extra=(state_lru, state_ret, cos2, sin2), layer=l)

    pos_p, gate_p, pos_s, gate_s = _route_call(lt_p, lt_s, cap_p, cap_s)
    xs_p, gsl_p = _dispatch_call(pos_p, gate_p, h2_p, cap_p)
    xs_s, gsl_s = _dispatch_call(pos_s, gate_s, h2_s, cap_s)
    ye = _expert_call(xs_p, xs_s, gsl_p, gsl_s, exp_w_gate[l], exp_w_up[l], exp_w_down[l])

    norm_post = norm_ffn_post[l][None]
    y_p = _combine_call(pos_p, ye, x1_p, mod, norm_post, cap_p, 0, latent=False)
    y_s = _combine_call(pos_s, ye, x1_s, mod, norm_post, cap_s, bp * cap_p, latent=True)
    return (y_p, y_s, st_lru, st_ret)
```

```python
import functools
import math
import types

import jax
import jax.numpy as jnp
import numpy as np
from jax import lax
from jax.experimental import pallas as pl
from jax.experimental.pallas import tpu as pltpu

D = 1024
LRU_W = 512
LRU_HEADS = 8
LRU_HD = 64
LRU_C = 8.0
RET_W = 512
RET_H = 4
DK = 128
N_EXP = 16
FF = 2048
N_MOD = 6
EPS = 1e-6
GRID_W = 64
ROPE_BASE = 10000.0
RET_DECAY_OFFSET_FWD = 5.0
RET_DECAY_OFFSET_BWD = 5.5

ROW_TILE = 256
WIDE_TILE = 512
FF_CHUNK = 512
COMBINE_TILE = 512
CONTEXT_GROUP = 2
SMALL_STEP_TOKENS = 1024
SUBLANES = 8
LANES = 128
VMEM_LIMIT = 60 * 1024 * 1024

DEC_LOG_F, DEC_LOG_B, DEC_G_F, DEC_GN_B, DEC_GN1_F = 0, 1, 2, 3, 4

_BF = jnp.bfloat16
_F32 = jnp.float32


def _sigmoid(x):
    return 0.5 * jnp.tanh(0.5 * x) + 0.5


def _silu(x):
    return x * _sigmoid(x)


def _gelu_tanh(x):
    c = math.sqrt(2.0 / math.pi)
    return 0.5 * x * (1.0 + jnp.tanh(c * (x + 0.044715 * (x * x * x))))


def _rms(x, gain):
    return x * lax.rsqrt(jnp.mean(x * x, axis=-1, keepdims=True) + EPS) * gain


def _dot(a, b):
    return jnp.dot(a, b, preferred_element_type=_F32)


def _dot_nt(a, b):
    return lax.dot_general(a, b, (((1,), (1,)), ((), ())), preferred_element_type=_F32)


def _dot_tn(a, b):
    return lax.dot_general(a, b, (((0,), (0,)), ((), ())), preferred_element_type=_F32)


def _ada_kernel(cc_ref, c_ref, w_ref, b_ref, o_ref, s_ref):
    nb = c_ref.shape[0]
    s_ref[...] = jnp.zeros_like(s_ref)
    s_ref[0:1, :] = _silu(cc_ref[...])
    s_ref[1:1 + nb, :] = _silu(c_ref[...])
    o_ref[...] = _dot(s_ref[...], w_ref[...]) + b_ref[...]


def _ada_call(c_ctx, c, ada_w, ada_b):
    nb = c.shape[0]
    assert nb + 1 <= SUBLANES
    return pl.pallas_call(
        _ada_kernel,
        out_shape=jax.ShapeDtypeStruct((N_MOD, SUBLANES, D), _F32),
        grid=(N_MOD,),
        in_specs=[
            pl.BlockSpec((1, D), lambda j: (0, 0)),
            pl.BlockSpec((nb, D), lambda j: (0, 0)),
            pl.BlockSpec((D, D), lambda j: (0, j)),
            pl.BlockSpec((1, D), lambda j: (0, j)),
        ],
        out_specs=pl.BlockSpec((None, SUBLANES, D), lambda j: (j, 0, 0)),
        scratch_shapes=[pltpu.VMEM((SUBLANES, D), _F32)],
        compiler_params=pltpu.CompilerParams(
            dimension_semantics=("arbitrary",), vmem_limit_bytes=VMEM_LIMIT),
        name="ada_mod",
    )(c_ctx, c, ada_w, ada_b)


def _tile_loop(nt, body, unroll=1):
    if nt == 1:
        body(0, 0)
    else:
        lax.fori_loop(0, nt, body, 0, unroll=unroll)


def _mixer_kernel(*refs, n, g, latent):
    if latent:
        (x_ref, mod_ref, n1_ref, n2_ref, n3_ref, wl_ref, win_hbm,
         cw_ref, cb_ref, ba_ref, bi_ref, lam_ref, wg_ref, wout_hbm, rw_ref,
         h0_ref, s0_ref, cos_ref, sin_ref,
         x1_ref, h2_ref, lt_ref,
         xlp_g, gy_g, af_g, ab_g, hf_g, hb_g, wret_v, wout_ref, late_sem) = refs
        sg_g, qf_g, qb_g, kf_g, kb_g, v_g = xlp_g, af_g, ab_g, hf_g, hb_g, gy_g
        wqk_ref, wvg_ref = wret_v.at[:, 0:D], wret_v.at[:, D:2 * D]
        first_step = pl.program_id(0) == 0
        late_ret = pltpu.make_async_copy(win_hbm.at[:, pl.ds(D, 2 * D)], wret_v, late_sem.at[0])
        late_out = pltpu.make_async_copy(wout_hbm, wout_ref, late_sem.at[1])

        @pl.when(first_step)
        def _():
            late_ret.start()
            late_out.start()
    else:
        (x_ref, mod_ref, n1_ref, n2_ref, n3_ref, wl_ref, wqk_ref, wvg_ref,
         cw_ref, cb_ref, ba_ref, bi_ref, lam_ref, wg_ref, wout_ref, rw_ref,
         x1_ref, h2_ref, lt_ref, stl_ref, str_ref,
         xlp_g, gy_g, af_g, ab_g, hf_g, hb_g, sg_g, qf_g, qb_g, kf_g, kb_g, v_g) = refs
    early_ret = not latent

    seqs = []
    for s in range(g):
        q = types.SimpleNamespace(
            x=x_ref.at[s], x1=x1_ref.at[s], h2=h2_ref.at[s], lt=lt_ref.at[s], mix=x1_ref.at[s],
            xlp=xlp_g.at[s], sg=sg_g.at[s], gy=gy_g.at[s], v=v_g.at[s],
            a_f=af_g.at[s], a_b=ab_g.at[s], h_f=hf_g.at[s], h_b=hb_g.at[s],
            q_f=qf_g.at[s], q_b=qb_g.at[s], k_f=kf_g.at[s], k_b=kb_g.at[s])
        if latent:
            q.h0, q.s0 = h0_ref.at[s], s0_ref.at[s]
        else:
            q.stl, q.str = stl_ref.at[s], str_ref.at[s]
        seqs.append(q)

    nt = n // ROW_TILE
    decay = _decay_consts(n)
    mod_row = (pl.program_id(0) + 1) if latent else 0
    mod = lambda k: mod_ref[k, pl.ds(mod_row, 1), :]
    shift = mod(0)
    scale = n1_ref[...] * (1.0 + mod(1))

    def normed(q, rows):
        return _rms(q.x[rows, :], scale) + shift

    def project_retention(q, r0, rows, h):
        pqk = _dot(h, wqk_ref[...])
        pvg = _dot(h, wvg_ref[...])
        q.sg[rows, :] = _silu(pvg[:, RET_W:])
        tpos = (r0 + lax.broadcasted_iota(jnp.int32, (h.shape[0], DK), 0)).astype(_F32)
        if latent:
            cos2 = cos_ref[rows, :]
            sin2 = sin_ref[rows, :]
        for hd in range(RET_H):
            cols = slice(hd * DK, (hd + 1) * DK)
            qh = pqk[:, cols] * (DK ** -0.5)
            kh = pqk[:, RET_W + hd * DK:RET_W + (hd + 1) * DK]
            if latent:
                qh = qh * cos2 + pltpu.roll(qh, DK // 2, axis=1) * sin2
                kh = kh * cos2 + pltpu.roll(kh, DK // 2, axis=1) * sin2
            lgf = decay[hd][DEC_LOG_F]
            lgb = decay[hd][DEC_LOG_B]
            q.q_f[hd, rows, :] = qh * jnp.exp(tpos * lgf)
            q.k_f[hd, rows, :] = kh * jnp.exp(tpos * (-lgf))
            q.q_b[hd, rows, :] = qh * jnp.exp(tpos * (-lgb))
            q.k_b[hd, rows, :] = kh * jnp.exp(tpos * lgb)
        q.v[rows, :] = pvg[:, 0:RET_W]

    def phase_a1(q):
        def body(r, carry):
            r0 = pl.multiple_of(r * ROW_TILE, ROW_TILE)
            rows = pl.ds(r0, ROW_TILE)
            h = normed(q, rows)
            p = _dot(h, wl_ref[...])
            q.xlp[pl.ds(r0 + SUBLANES, ROW_TILE), :] = p[:, 0:LRU_W]
            q.gy[rows, :] = _gelu_tanh(p[:, LRU_W:])
            if early_ret:
                project_retention(q, r0, rows, h)
            return carry
        return body

    for q in seqs:
        q.xlp[0:SUBLANES, :] = jnp.zeros((SUBLANES, LRU_W), _F32)
        q.xlp[n + SUBLANES:n + 2 * SUBLANES, :] = jnp.zeros((SUBLANES, LRU_W), _F32)

    half = LRU_W // 2

    def softplus_neg(lam):
        z = -lam
        return jnp.maximum(z, 0.0) + jnp.log1p(jnp.exp(-jnp.abs(z)))

    sp = (softplus_neg(lam_ref[0:1, :]), softplus_neg(lam_ref[1:2, :]))

    def phase_b(q):
        def body(r, carry):
            r0 = pl.multiple_of(r * ROW_TILE, ROW_TILE)
            rows = pl.ds(r0, ROW_TILE)
            ext = q.xlp[pl.ds(r0, ROW_TILE + 2 * SUBLANES), :]
            xc = cb_ref[...]
            for tap in range(4):
                o = SUBLANES - 2 + tap
                xc = xc + ext[o:o + ROW_TILE, :] * cw_ref[tap:tap + 1, :]
            xh = 0.5 * xc
            for d, (a_ref, u_ref) in enumerate(((q.a_f, q.h_f), (q.a_b, q.h_b))):
                bah = 0.5 * ba_ref[d:d + 1, :]
                bih = 0.5 * bi_ref[d:d + 1, :]
                ch = (-0.5 * LRU_C) * sp[d]
                for hh in range(2):
                    cs = slice(hh * half, (hh + 1) * half)
                    pre = _dot(xc[:, cs], wg_ref[d, hh])
                    t_r = jnp.tanh(pre[:, 0:half] + bah[:, cs])
                    t_i = jnp.tanh(pre[:, half:] + bih[:, cs])
                    log_a = t_r * ch[:, cs] + ch[:, cs]
                    a = jnp.exp(log_a)
                    om = -jnp.tanh(log_a) * (a * a + 1.0)
                    root = jnp.where(om > 0.0, om * lax.rsqrt(om), 0.0)
                    u = root * (t_i * xh[:, cs] + xh[:, cs])
                    for j in range(half // LANES):
                        lanes = slice(j * LANES, (j + 1) * LANES)
                        a_ref[hh * (half // LANES) + j, rows, :] = a[:, lanes]
                        u_ref[hh * (half // LANES) + j, rows, :] = u[:, lanes]
            return carry
        return body

    for q in seqs:
        _tile_loop(nt, phase_a1(q))
    for q in seqs:
        _tile_loop(nt, phase_b(q))

    row8 = lax.broadcasted_iota(jnp.int32, (SUBLANES, LANES), 0)
    block = SUBLANES * SUBLANES
    n_blocks = n // block
    n_slabs = LRU_W // LANES

    def across_groups(a, b, reverse):
        for s in (1, 2, 4):
            m = (row8 < SUBLANES - s) if reverse else (row8 >= s)
            shift = SUBLANES - s if reverse else s
            a_s = jnp.where(m, pltpu.roll(a, shift, axis=0), 1.0)
            b_s = jnp.where(m, pltpu.roll(b, shift, axis=0), 0.0)
            b = a * b_s + b
            a = a * a_s
        return a, b

    def scan_block(a_ref, h_ref, base, carry, reverse):
        rows = [pl.ds(base + k, SUBLANES, stride=SUBLANES) for k in range(SUBLANES)]
        order = list(reversed(range(SUBLANES))) if reverse else list(range(SUBLANES))
        prod, local = {}, {}
        prev = None
        for k in order:
            a, u = a_ref[rows[k], :], h_ref[rows[k], :]
            prod[k] = a if prev is None else a * prod[prev]
            local[k] = u if prev is None else a * local[prev] + u
            prev = k
        p_all, h_all = across_groups(prod[prev], local[prev], reverse)
        inner = (row8 < SUBLANES - 1) if reverse else (row8 >= 1)
        shift = SUBLANES - 1 if reverse else 1
        enter = (jnp.where(inner, pltpu.roll(p_all, shift, axis=0), 1.0) * carry
                 + jnp.where(inner, pltpu.roll(h_all, shift, axis=0), 0.0))
        for k in order:
            h_ref[rows[k], :] = prod[k] * enter + local[k]
        leave = p_all * carry + h_all
        return leave[0:1, :] if reverse else leave[SUBLANES - 1:SUBLANES, :]

    def initial(q, d, s):
        if latent:
            return q.h0[d:d + 1, s * LANES:(s + 1) * LANES]
        return jnp.zeros((1, LANES), _F32)

    chains = [(q, s) for q in seqs for s in range(n_slabs)]

    def scan_body(i, carry):
        fwd_base = pl.multiple_of(i * block, block)
        bwd_base = pl.multiple_of((n_blocks - 1 - i) * block, block)
        out = []
        for (q, s), (cf, cb) in zip(chains, carry):
            out.append((scan_block(q.a_f.at[s], q.h_f.at[s], fwd_base, cf, False),
                        scan_block(q.a_b.at[s], q.h_b.at[s], bwd_base, cb, True)))
        return tuple(out)

    ends = lax.fori_loop(0, n_blocks, scan_body,
                         tuple((initial(q, 0, s), initial(q, 1, s)) for q, s in chains), unroll=2)

    def lru_out(q):
        def body(r, carry):
            rows = pl.ds(pl.multiple_of(r * ROW_TILE, ROW_TILE), ROW_TILE)
            for s in range(n_slabs):
                lanes = slice(s * LANES, (s + 1) * LANES)
                q.mix[rows, lanes] = (q.h_f[s, rows, :] + q.h_b[s, rows, :]) * q.gy[rows, lanes]
            return carry
        return body

    for (q, s), (last_f, first_b) in zip(chains, ends):
        if not latent:
            q.stl[0:1, s * LANES:(s + 1) * LANES] = last_f
            q.stl[1:2, s * LANES:(s + 1) * LANES] = first_b
    for q in seqs:
        _tile_loop(nt, lru_out(q))

    wide = min(n, WIDE_TILE)

    def phase_a2(q):
        def body(r, carry):
            r0 = pl.multiple_of(r * wide, wide)
            rows = pl.ds(r0, wide)
            project_retention(q, r0, rows, normed(q, rows))
            return carry
        return body

    if not early_ret:
        @pl.when(first_step)
        def _():
            late_ret.wait()

        for q in seqs:
            _tile_loop(n // wide, phase_a2(q))

    lower = (lax.broadcasted_iota(jnp.int32, (ROW_TILE, ROW_TILE), 0)
             >= lax.broadcasted_iota(jnp.int32, (ROW_TILE, ROW_TILE), 1))
    blocks = [slice(r * ROW_TILE, (r + 1) * ROW_TILE) for r in range(nt)]

    def phase_c(q, hd):
        cols = slice(hd * DK, (hd + 1) * DK)
        dec = lambda row: decay[hd][row]
        kv_f = [_dot_tn(q.k_f[hd, rows, :], q.v[rows, cols]) if (r < nt - 1 or not latent) else 0.0
                for r, rows in enumerate(blocks)]
        kv_b = [_dot_tn(q.k_b[hd, rows, :], q.v[rows, cols]) if (r > 0 or not latent) else 0.0
                for r, rows in enumerate(blocks)]
        if latent:
            run_f = q.s0[0, hd] * dec(DEC_G_F)
            run_b = q.s0[1, hd] * dec(DEC_GN_B)
        else:
            run_f = run_b = None
        before = []
        for r in range(nt):
            before.append(run_f)
            run_f = kv_f[r] if run_f is None else run_f + kv_f[r]
        after = [None] * nt
        for r in reversed(range(nt)):
            after[r] = run_b
            run_b = kv_b[r] if run_b is None else run_b + kv_b[r]

        for r, rows in enumerate(blocks):
            qf = q.q_f[hd, rows, :]
            qb = q.q_b[hd, rows, :]
            s = jnp.where(lower, _dot_nt(qf, q.k_f[hd, rows, :]), _dot_nt(qb, q.k_b[hd, rows, :]))
            o = _dot(s, q.v[rows, cols])
            if before[r] is not None:
                o = o + _dot(qf, before[r])
            if after[r] is not None:
                o = o + _dot(qb, after[r])
            o = o * lax.rsqrt(jnp.mean(o * o, axis=-1, keepdims=True) + EPS)
            q.mix[rows, LRU_W + cols.start:LRU_W + cols.stop] = o * q.sg[rows, cols]
        if not latent:
            q.str[0, hd] = run_f * dec(DEC_GN1_F)
            q.str[1, hd] = run_b

    for hd in range(RET_H):
        for q in seqs:
            phase_c(q, hd)

    gain1 = mod(2) * n2_ref[...]
    gain2 = n3_ref[...] * (1.0 + mod(4))
    sh2 = mod(3)

    def phase_d(q):
        def body(r, carry):
            rows = pl.ds(pl.multiple_of(r * wide, wide), wide)
            mix = _dot(q.mix[rows, :], wout_ref[...])
            x1 = q.x[rows, :] + _rms(mix, gain1)
            q.x1[rows, :] = x1
            h2 = _rms(x1, gain2) + sh2
            q.h2[rows, :] = h2.astype(_BF)
            q.lt[:, rows] = _dot(h2, rw_ref[...]).T[0:N_EXP, :]
            return carry
        return body

    if latent:
        @pl.when(first_step)
        def _():
            late_out.wait()

    for q in seqs:
        _tile_loop(n // wide, phase_d(q), unroll=2)


def _mixer_call(x, mod, norms, w_in, lru, wg, w_out, rw, latent, extra=(), layer=0):
    b, n, _ = x.shape
    g = 1 if latent else CONTEXT_GROUP
    const2 = lambda i: (0, 0)
    whole = pl.BlockSpec(memory_space=pl.ANY)
    in_specs = [
        pl.BlockSpec((g, n, D), lambda i: (i, 0, 0)),
        pl.BlockSpec((N_MOD, SUBLANES, D), lambda i: (0, 0, 0)),
        pl.BlockSpec((1, D), const2), pl.BlockSpec((1, D), const2), pl.BlockSpec((1, D), const2),
        pl.BlockSpec((D, D), lambda i: (0, 0)),
    ]
    if latent:
        in_specs += [whole]
    else:
        in_specs += [pl.BlockSpec((D, D), lambda i: (0, 1)),
                     pl.BlockSpec((D, D), lambda i: (0, 2))]
    in_specs += [
        pl.BlockSpec((4, LRU_W), const2), pl.BlockSpec((1, LRU_W), const2),
        pl.BlockSpec((2, LRU_W), const2), pl.BlockSpec((2, LRU_W), const2),
        pl.BlockSpec((2, LRU_W), const2),
        pl.BlockSpec((2, 2, LRU_W // 2, LRU_W), lambda i: (0, 0, 0, 0)),
        whole if latent else pl.BlockSpec((D, D), const2),
        pl.BlockSpec((D, LANES), const2),
    ]
    out_shape = [
        jax.ShapeDtypeStruct((b, n, D), _F32),
        jax.ShapeDtypeStruct((b, n, D), _BF),
        jax.ShapeDtypeStruct((b, N_EXP, n), _F32),
    ]
    out_specs = [
        pl.BlockSpec((g, n, D), lambda i: (i, 0, 0)),
        pl.BlockSpec((g, n, D), lambda i: (i, 0, 0)),
        pl.BlockSpec((g, N_EXP, n), lambda i: (i, 0, 0)),
    ]
    if latent:
        in_specs += [
            pl.BlockSpec((g, None, 2, LRU_W), lambda i: (i, layer, 0, 0)),
            pl.BlockSpec((g, None, 2, RET_H, DK, DK), lambda i: (i, layer, 0, 0, 0, 0)),
            pl.BlockSpec((n, DK), const2),
            pl.BlockSpec((n, DK), const2),
        ]
    else:
        out_shape += [
            jax.ShapeDtypeStruct((b, 1, 2, LRU_W), _F32),
            jax.ShapeDtypeStruct((b, 1, 2, RET_H, DK, DK), _F32),
        ]
        out_specs += [
            pl.BlockSpec((g, None, 2, LRU_W), lambda i: (i, 0, 0, 0)),
            pl.BlockSpec((g, None, 2, RET_H, DK, DK), lambda i: (i, 0, 0, 0, 0, 0)),
        ]
    f32s = lambda shape: pltpu.VMEM((g,) + shape, _F32)
    slabs = (LRU_W // LANES, n, LANES)
    scratch = [
        f32s((n + 2 * SUBLANES, LRU_W)),
        f32s((n, LRU_W)),
        f32s(slabs), f32s(slabs),
        f32s(slabs), f32s(slabs),
    ]
    if latent:
        scratch += [pltpu.VMEM((D, 2 * D), _F32), pltpu.VMEM((D, D), _F32),
                    pltpu.SemaphoreType.DMA((2,))]
        w_in_args = (w_in, w_in)
    else:
        scratch += [f32s((n, RET_W))] + [f32s(slabs) for _ in range(4)] + [f32s((n, RET_W))]
        w_in_args = (w_in, w_in, w_in)
    return pl.pallas_call(
        functools.partial(_mixer_kernel, n=n, g=g, latent=latent),
        out_shape=out_shape,
        grid=(b // g,),
        in_specs=in_specs,
        out_specs=out_specs,
        scratch_shapes=scratch,
        compiler_params=pltpu.CompilerParams(
            dimension_semantics=("arbitrary",), vmem_limit_bytes=VMEM_LIMIT),
        name="mixer_latent" if latent else "mixer_context",
    )(x, mod, *norms, *w_in_args, *lru, wg, w_out, rw, *extra)


def _count(mask):
    return jnp.sum(jnp.where(mask, 1.0, 0.0), axis=-1, keepdims=True)


def _probs(l3):
    bsz, _, n = l3.shape
    m = jnp.max(l3, axis=1, keepdims=True)
    e = jnp.exp(l3 - m)
    return (e / jnp.sum(e, axis=1, keepdims=True)).reshape(bsz * N_EXP, n)


def _finish_select(p, bits, thr, cap):
    rows, n = p.shape
    gt = bits > thr
    eq = bits == thr
    need = float(cap) - _count(gt)
    idx = lax.broadcasted_iota(jnp.int32, (rows, n), 1)
    nbits = int(math.log2(n))

    def idx_body(i, j):
        cand = j | (jnp.int32(1) << (nbits - 1 - i))
        return jnp.where(_count(eq & (idx < cand)) < need, cand, j)

    jlast = lax.fori_loop(0, nbits, idx_body, jnp.zeros((rows, 1), jnp.int32))
    sel = gt | (eq & (idx <= jlast))
    before = (lax.broadcasted_iota(jnp.int32, (n, n), 0)
              < lax.broadcasted_iota(jnp.int32, (n, n), 1))
    pos = _dot(jnp.where(sel, 1.0, 0.0).astype(_BF), jnp.where(before, 1.0, 0.0).astype(_BF))
    return jnp.where(sel, pos, -1.0), jnp.where(sel, p, 0.0)


def _route_kernel(lp_ref, ls_ref, pp_ref, gp_ref, ps_ref, gs_ref, *, cap_p, cap_s):
    groups = ((_probs(lp_ref[...]), float(cap_p)), (_probs(ls_ref[...]), float(cap_s)))
    bits = [pltpu.bitcast(p, jnp.int32) for p, _ in groups]

    def val_body(i, thr):
        out = []
        for b, t, (_, capf) in zip(bits, thr, groups):
            cand = t | (jnp.int32(1) << (30 - i))
            out.append(jnp.where(_count(b >= cand) >= capf, cand, t))
        return tuple(out)

    thr = lax.fori_loop(0, 31, val_body,
                        tuple(jnp.zeros((b.shape[0], 1), jnp.int32) for b in bits))
    pos, gate = _finish_select(groups[0][0], bits[0], thr[0], cap_p)
    pp_ref[...] = pos.reshape(pp_ref.shape)
    gp_ref[...] = gate.reshape(gp_ref.shape)
    pos, gate = _finish_select(groups[1][0], bits[1], thr[1], cap_s)
    ps_ref[...] = pos.reshape(ps_ref.shape)
    gs_ref[...] = gate.reshape(gs_ref.shape)


def _route_call(lt_p, lt_s, cap_p, cap_s):
    shapes = [
        jax.ShapeDtypeStruct(lt_p.shape, _F32), jax.ShapeDtypeStruct(lt_p.shape, _F32),
        jax.ShapeDtypeStruct(lt_s.shape, _F32), jax.ShapeDtypeStruct(lt_s.shape, _F32),
    ]
    return pl.pallas_call(
        functools.partial(_route_kernel, cap_p=cap_p, cap_s=cap_s),
        out_shape=shapes,
        compiler_params=pltpu.CompilerParams(vmem_limit_bytes=VMEM_LIMIT),
        name="route_select",
    )(lt_p, lt_s)


def _dispatch_kernel(pos_ref, gate_ref, h_ref, xs_ref, gs_ref, *, n, cap, g):
    slot = lax.broadcasted_iota(jnp.int32, (cap, n), 0).astype(_F32)
    for j in range(g):
        slots = slice(j * cap, (j + 1) * cap)
        parts = []
        for e in range(N_EXP):
            hit = pos_ref[j, e:e + 1, :] == slot
            parts.append(jnp.where(hit, 1.0, 0.0).astype(_BF))
            gs_ref[e, slots, :] = jnp.sum(jnp.where(hit, gate_ref[j, e:e + 1, :], 0.0),
                                          axis=-1, keepdims=True)
        onehot = jnp.concatenate(parts, axis=0)
        xs = _dot(onehot, h_ref[j]).astype(_BF)
        for e in range(N_EXP):
            xs_ref[e, slots, :] = xs[e * cap:(e + 1) * cap, :]


def _dispatch_call(pos, gate, h2, cap):
    b, n, _ = h2.shape
    g = max(1, SMALL_STEP_TOKENS // n)
    return pl.pallas_call(
        functools.partial(_dispatch_kernel, n=n, cap=cap, g=g),
        out_shape=[
            jax.ShapeDtypeStruct((N_EXP, b * cap, D), _BF),
            jax.ShapeDtypeStruct((N_EXP, b * cap, 1), _F32),
        ],
        grid=(b // g,),
        in_specs=[
            pl.BlockSpec((g, N_EXP, n), lambda i: (i, 0, 0)),
            pl.BlockSpec((g, N_EXP, n), lambda i: (i, 0, 0)),
            pl.BlockSpec((g, n, D), lambda i: (i, 0, 0)),
        ],
        out_specs=[
            pl.BlockSpec((N_EXP, g * cap, D), lambda i: (0, i, 0)),
            pl.BlockSpec((N_EXP, g * cap, 1), lambda i: (0, i, 0)),
        ],
        compiler_params=pltpu.CompilerParams(
            dimension_semantics=("arbitrary",), vmem_limit_bytes=VMEM_LIMIT),
        name="dispatch",
    )(pos, gate, h2)


def _expert_kernel(xp_ref, xs_ref, gp_ref, gs_ref, wg_ref, wu_ref, wd_ref, y_ref, xcat, acc,
                   *, sp, nf, tf):
    f = pl.program_id(1)
    xcat[0:sp, :] = xp_ref[...]
    xcat[sp:, :] = xs_ref[...]
    x = xcat[...]
    total = jnp.where(f == 0, 0.0, acc[...])
    for c in range(tf // FF_CHUNK):
        cs = slice(c * FF_CHUNK, (c + 1) * FF_CHUNK)
        hg = _dot(x, wg_ref[:, cs].astype(_BF))
        hu = _dot(x, wu_ref[:, cs].astype(_BF))
        hid = (_silu(hg) * hu).astype(_BF)
        total = total + _dot(hid, wd_ref[cs, :].astype(_BF))
    acc[...] = total
    y_ref[0:sp, :] = (total[0:sp, :] * gp_ref[...]).astype(_BF)
    y_ref[sp:, :] = (total[sp:, :] * gs_ref[...]).astype(_BF)


def _expert_call(xs_p, xs_s, g_p, g_s, w_gate, w_up, w_down):
    tf = 1024
    sp = xs_p.shape[1]
    ss = xs_s.shape[1]
    nf = FF // tf
    return pl.pallas_call(
        functools.partial(_expert_kernel, sp=sp, nf=nf, tf=tf),
        out_shape=jax.ShapeDtypeStruct((N_EXP, sp + ss, D), _BF),
        grid=(N_EXP, nf),
        in_specs=[
            pl.BlockSpec((None, sp, D), lambda e, f: (e, 0, 0)),
            pl.BlockSpec((None, ss, D), lambda e, f: (e, 0, 0)),
            pl.BlockSpec((None, sp, 1), lambda e, f: (e, 0, 0)),
            pl.BlockSpec((None, ss, 1), lambda e, f: (e, 0, 0)),
            pl.BlockSpec((None, D, tf), lambda e, f: (e, 0, f)),
            pl.BlockSpec((None, D, tf), lambda e, f: (e, 0, f)),
            pl.BlockSpec((None, tf, D), lambda e, f: (e, f, 0)),
        ],
        out_specs=pl.BlockSpec((None, sp + ss, D), lambda e, f: (e, 0, 0)),
        scratch_shapes=[pltpu.VMEM((sp + ss, D), _BF), pltpu.VMEM((sp + ss, D), _F32)],
        compiler_params=pltpu.CompilerParams(
            dimension_semantics=("arbitrary", "arbitrary"), vmem_limit_bytes=VMEM_LIMIT),
        name="expert_ffn",
    )(xs_p, xs_s, g_p, g_s, w_gate, w_up, w_down)


def _combine_kernel(pos_ref, ye_ref, x1_ref, mod_ref, n4_ref, y_ref, *, n, cap, g, latent):
    width = N_EXP * cap
    lane = lax.broadcasted_iota(jnp.int32, (N_EXP, width), 1)
    expand = jnp.where(lane // cap == lax.broadcasted_iota(jnp.int32, (N_EXP, width), 0),
                       1.0, 0.0).astype(_BF)
    tile = min(n, COMBINE_TILE)
    slot = (lax.broadcasted_iota(jnp.int32, (tile, width), 1) % cap).astype(_F32)
    mod_row = (pl.program_id(0) + 1) if latent else 0
    gain = mod_ref[N_MOD - 1, pl.ds(mod_row, 1), :] * n4_ref[...]

    for j in range(g):
        ye = ye_ref[:, j * cap:(j + 1) * cap, :].reshape(width, D)

        def body(r, carry, j=j, ye=ye):
            rows = pl.ds(pl.multiple_of(r * tile, tile), tile)
            pos_e = _dot_tn(pos_ref[j, :, rows].astype(_BF), expand)
            onehot = jnp.where(pos_e == slot, 1.0, 0.0).astype(_BF)
            f = _dot(onehot, ye)
            y_ref[j, rows, :] = x1_ref[j, rows, :] + _rms(f, gain)
            return carry

        _tile_loop(n // tile, body)


def _combine_call(pos, ye, x1, mod, norm_post, cap, slot_off, latent):
    b, n, _ = x1.shape
    g = max(1, SMALL_STEP_TOKENS // n)
    assert g == 1 or not latent
    blk_off = slot_off // (g * cap)
    return pl.pallas_call(
        functools.partial(_combine_kernel, n=n, cap=cap, g=g, latent=latent),
        out_shape=jax.ShapeDtypeStruct((b, n, D), _F32),
        grid=(b // g,),
        in_specs=[
            pl.BlockSpec((g, N_EXP, n), lambda i: (i, 0, 0)),
            pl.BlockSpec((N_EXP, g * cap, D), lambda i: (0, i + blk_off, 0)),
            pl.BlockSpec((g, n, D), lambda i: (i, 0, 0)),
            pl.BlockSpec((N_MOD, SUBLANES, D), lambda i: (0, 0, 0)),
            pl.BlockSpec((1, D), lambda i: (0, 0)),
        ],
        out_specs=pl.BlockSpec((g, n, D), lambda i: (i, 0, 0)),
        compiler_params=pltpu.CompilerParams(
            dimension_semantics=("arbitrary",), vmem_limit_bytes=VMEM_LIMIT),
        name="combine_latent" if latent else "combine_context",
    )(pos, ye, x1, mod, norm_post)


def _block_diag_gates(wa, wi):
    per_half = LRU_HEADS // 2
    side = per_half * LRU_HD
    on_diag = (np.arange(side)[:, None] // LRU_HD) == (np.arange(side)[None, :] // LRU_HD)

    def bd(w):
        rows = w.reshape(2, 2, side, LRU_HD)
        return jnp.where(on_diag, jnp.tile(rows, (1, 1, 1, per_half)), 0.0)

    return 0.5 * jnp.concatenate([bd(wa), bd(wi)], axis=-1)


def _rope_tables(n):
    rows = n // GRID_W
    row = np.repeat(np.arange(rows, dtype=np.float32), GRID_W)
    col = np.tile(np.arange(GRID_W, dtype=np.float32), rows)
    nf = DK // 4
    freqs = np.float32(ROPE_BASE) ** (-np.arange(nf, dtype=np.float32) / np.float32(nf))
    ang = np.concatenate([row[:, None] * freqs, col[:, None] * freqs], axis=-1).astype(np.float32)
    cos = np.cos(ang)
    sin = np.sin(ang)
    return (jnp.asarray(np.concatenate([cos, cos], axis=-1), _F32),
            jnp.asarray(np.concatenate([-sin, sin], axis=-1), _F32))


def _decay_consts(n):
    heads = np.arange(RET_H, dtype=np.float32)
    f32 = np.float32
    lgf = np.log1p(-np.exp2(-(f32(RET_DECAY_OFFSET_FWD) + heads))).astype(f32)
    lgb = np.log1p(-np.exp2(-(f32(RET_DECAY_OFFSET_BWD) + heads))).astype(f32)
    tab = np.stack([lgf, lgb, np.exp(lgf), np.exp(f32(n) * lgb), np.exp(f32(n - 1) * lgf)], axis=1)
    return [[float(v) for v in row] for row in tab.astype(f32)]


def kernel(x_prompt, x_sample, c, state_lru, state_ret, c_ctx, ada_w, ada_b, norm_mix_pre, norm_mix_post, norm_ffn_pre, norm_ffn_post, w_in, conv_w, conv_b, lru_wa, lru_ba, lru_wi, lru_bi, lru_lambda, w_out, router_w, exp_w_gate, exp_w_up, exp_w_down):
    bp, n_p, _ = x_prompt.shape
    bs, n_s, _ = x_sample.shape
    cap_p = 2 * n_p // N_EXP
    cap_s = 2 * n_s // N_EXP
    l = 0

    mod = _ada_call(c_ctx[None, :], c, ada_w[l], ada_b[l][None, :])

    norms = (norm_mix_pre[l][None], norm_mix_post[l][None], norm_ffn_pre[l][None])
    lru = (conv_w[l], conv_b[l][None], lru_ba[l], lru_bi[l], lru_lambda[l])
    wg = _block_diag_gates(lru_wa[l], lru_wi[l])
    rw = jnp.pad(router_w[l], ((0, 0), (0, LANES - N_EXP)))
    cos2, sin2 = _rope_tables(n_s)

    x1_p, h2_p, lt_p, st_lru, st_ret = _mixer_call(
        x_prompt, mod, norms, w_in[l], lru, wg, w_out[l], rw, latent=False)
    x1_s, h2_s, lt_s = _mixer_call(
        x_sample, mod, norms, w_in[l], lru, wg, w_out[l], rw, latent=True,
        extra=(state_lru, state_ret, cos2, sin2), layer=l)

    pos_p, gate_p, pos_s, gate_s = _route_call(lt_p, lt_s, cap_p, cap_s)
    xs_p, gsl_p = _dispatch_call(pos_p, gate_p, h2_p, cap_p)
    xs_s, gsl_s = _dispatch_call(pos_s, gate_s, h2_s, cap_s)
    ye = _expert_call(xs_p, xs_s, gsl_p, gsl_s, exp_w_gate[l], exp_w_up[l], exp_w_down[l])

    norm_post = norm_ffn_post[l][None]
    y_p = _combine_call(pos_p, ye, x1_p, mod, norm_post, cap_p, 0, latent=False)
    y_s = _combine_call(pos_s, ye, x1_s, mod, norm_post, cap_s, bp * cap_p, latent=True)
    return (y_p, y_s, st_lru, st_ret)
```

```python
import functools
import math
import types

import jax
import jax.numpy as jnp
import numpy as np
from jax import lax
from jax.experimental import pallas as pl
from jax.experimental.pallas import tpu as pltpu

D = 1024
LRU_W = 512
LRU_HEADS = 8
LRU_HD = 64
LRU_C = 8.0
RET_W = 512
RET_H = 4
DK = 128
N_EXP = 16
FF = 2048
N_MOD = 6
EPS = 1e-6
GRID_W = 64
ROPE_BASE = 10000.0
RET_DECAY_OFFSET_FWD = 5.0
RET_DECAY_OFFSET_BWD = 5.5

ROW_TILE = 256
WIDE_TILE = 512
FF_CHUNK = 512
COMBINE_TILE = 512
CONTEXT_GROUP = 2
SMALL_STEP_TOKENS = 1024
SUBLANES = 8
LANES = 128
VMEM_LIMIT = 60 * 1024 * 1024

DEC_LOG_F, DEC_LOG_B, DEC_G_F, DEC_GN_B, DEC_GN1_F = 0, 1, 2, 3, 4

_BF = jnp.bfloat16
_F32 = jnp.float32


def _sigmoid(x):
    return 0.5 * jnp.tanh(0.5 * x) + 0.5


def _silu(x):
    return x * _sigmoid(x)


def _gelu_tanh(x):
    c = math.sqrt(2.0 / math.pi)
    return 0.5 * x * (1.0 + jnp.tanh(c * (x + 0.044715 * (x * x * x))))


def _rms(x, gain):
    return x * lax.rsqrt(jnp.mean(x * x, axis=-1, keepdims=True) + EPS) * gain


def _dot(a, b):
    return jnp.dot(a, b, preferred_element_type=_F32)


def _dot_nt(a, b):
    return lax.dot_general(a, b, (((1,), (1,)), ((), ())), preferred_element_type=_F32)


def _dot_tn(a, b):
    return lax.dot_general(a, b, (((0,), (0,)), ((), ())), preferred_element_type=_F32)


def _ada_kernel(cc_ref, c_ref, w_ref, b_ref, o_ref, s_ref):
    nb = c_ref.shape[0]
    s_ref[...] = jnp.zeros_like(s_ref)
    s_ref[0:1, :] = _silu(cc_ref[...])
    s_ref[1:1 + nb, :] = _silu(c_ref[...])
    o_ref[...] = _dot(s_ref[...], w_ref[...]) + b_ref[...]


def _ada_call(c_ctx, c, ada_w, ada_b):
    nb = c.shape[0]
    assert nb + 1 <= SUBLANES
    return pl.pallas_call(
        _ada_kernel,
        out_shape=jax.ShapeDtypeStruct((N_MOD, SUBLANES, D), _F32),
        grid=(N_MOD,),
        in_specs=[
            pl.BlockSpec((1, D), lambda j: (0, 0)),
            pl.BlockSpec((nb, D), lambda j: (0, 0)),
            pl.BlockSpec((D, D), lambda j: (0, j)),
            pl.BlockSpec((1, D), lambda j: (0, j)),
        ],
        out_specs=pl.BlockSpec((None, SUBLANES, D), lambda j: (j, 0, 0)),
        scratch_shapes=[pltpu.VMEM((SUBLANES, D), _F32)],
        compiler_params=pltpu.CompilerParams(
            dimension_semantics=("arbitrary",), vmem_limit_bytes=VMEM_LIMIT),
        name="ada_mod",
    )(c_ctx, c, ada_w, ada_b)


def _tile_loop(nt, body, unroll=1):
    if nt == 1:
        body(0, 0)
    else:
        lax.fori_loop(0, nt, body, 0, unroll=unroll)


def _mixer_kernel(*refs, n, g, latent):
    if latent:
        (x_ref, mod_ref, n1_ref, n2_ref, n3_ref, wl_ref, win_hbm,
         cw_ref, cb_ref, ba_ref, bi_ref, lam_ref, wg_ref, wout_hbm, rw_ref,
         h0_ref, s0_ref, cos_ref, sin_ref,
         x1_ref, h2_ref, lt_ref,
         xlp_g, gy_g, af_g, ab_g, hf_g, hb_g, wret_v, wout_ref, late_sem) = refs
        sg_g, qf_g, qb_g, kf_g, kb_g, v_g = xlp_g, af_g, ab_g, hf_g, hb_g, gy_g
        wqk_ref, wvg_ref = wret_v.at[:, 0:D], wret_v.at[:, D:2 * D]
        first_step = pl.program_id(0) == 0
        late_ret = pltpu.make_async_copy(win_hbm.at[:, pl.ds(D, 2 * D)], wret_v, late_sem.at[0])
        late_out = pltpu.make_async_copy(wout_hbm, wout_ref, late_sem.at[1])

        @pl.when(first_step)
        def _():
            late_ret.start()
            late_out.start()
    else:
        (x_ref, mod_ref, n1_ref, n2_ref, n3_ref, wl_ref, wqk_ref, wvg_ref,
         cw_ref, cb_ref, ba_ref, bi_ref, lam_ref, wg_ref, wout_ref, rw_ref,
         x1_ref, h2_ref, lt_ref, stl_ref, str_ref,
         xlp_g, gy_g, af_g, ab_g, hf_g, hb_g, sg_g, qf_g, qb_g, kf_g, kb_g, v_g) = refs
    early_ret = not latent

    seqs = []
    for s in range(g):
        q = types.SimpleNamespace(
            x=x_ref.at[s], x1=x1_ref.at[s], h2=h2_ref.at[s], lt=lt_ref.at[s], mix=x1_ref.at[s],
            xlp=xlp_g.at[s], sg=sg_g.at[s], gy=gy_g.at[s], v=v_g.at[s],
            a_f=af_g.at[s], a_b=ab_g.at[s], h_f=hf_g.at[s], h_b=hb_g.at[s],
            q_f=qf_g.at[s], q_b=qb_g.at[s], k_f=kf_g.at[s], k_b=kb_g.at[s])
        if latent:
            q.h0, q.s0 = h0_ref.at[s], s0_ref.at[s]
        else:
            q.stl, q.str = stl_ref.at[s], str_ref.at[s]
        seqs.append(q)

    nt = n // ROW_TILE
    decay = _decay_consts(n)
    mod_row = (pl.program_id(0) + 1) if latent else 0
    mod = lambda k: mod_ref[k, pl.ds(mod_row, 1), :]
    shift = mod(0)
    scale = n1_ref[...] * (1.0 + mod(1))

    def normed(q, rows):
        return _rms(q.x[rows, :], scale) + shift

    wide = min(n, WIDE_TILE)
    a_tile = wide if latent else ROW_TILE

    def stacked(parts):
        return parts[0] if len(parts) == 1 else jnp.concatenate(parts, axis=0)

    def store_retention(q, r0, rows, pqk, pvg):
        q.sg[rows, :] = _silu(pvg[:, RET_W:])
        tpos = (r0 + lax.broadcasted_iota(jnp.int32, (pqk.shape[0], DK), 0)).astype(_F32)
        if latent:
            cos2 = cos_ref[rows, :]
            sin2 = sin_ref[rows, :]
        for hd in range(RET_H):
            cols = slice(hd * DK, (hd + 1) * DK)
            qh = pqk[:, cols] * (DK ** -0.5)
            kh = pqk[:, RET_W + hd * DK:RET_W + (hd + 1) * DK]
            if latent:
                qh = qh * cos2 + pltpu.roll(qh, DK // 2, axis=1) * sin2
                kh = kh * cos2 + pltpu.roll(kh, DK // 2, axis=1) * sin2
            lgf = decay[hd][DEC_LOG_F]
            lgb = decay[hd][DEC_LOG_B]
            q.q_f[hd, rows, :] = qh * jnp.exp(tpos * lgf)
            q.k_f[hd, rows, :] = kh * jnp.exp(tpos * (-lgf))
            q.q_b[hd, rows, :] = qh * jnp.exp(tpos * (-lgb))
            q.k_b[hd, rows, :] = kh * jnp.exp(tpos * lgb)
        q.v[rows, :] = pvg[:, 0:RET_W]

    def phase_a1(r, carry):
        r0 = pl.multiple_of(r * a_tile, a_tile)
        rows = pl.ds(r0, a_tile)
        h = stacked([normed(q, rows) for q in seqs])
        p_all = _dot(h, wl_ref[...])
        if early_ret:
            pqk_all = _dot(h, wqk_ref[...])
            pvg_all = _dot(h, wvg_ref[...])
        for s, q in enumerate(seqs):
            part = slice(s * a_tile, (s + 1) * a_tile)
            q.xlp[pl.ds(r0 + SUBLANES, a_tile), :] = p_all[part, 0:LRU_W]
            q.gy[rows, :] = _gelu_tanh(p_all[part, LRU_W:])
            if early_ret:
                store_retention(q, r0, rows, pqk_all[part], pvg_all[part])
        return carry

    for q in seqs:
        q.xlp[0:SUBLANES, :] = jnp.zeros((SUBLANES, LRU_W), _F32)
        q.xlp[n + SUBLANES:n + 2 * SUBLANES, :] = jnp.zeros((SUBLANES, LRU_W), _F32)

    half = LRU_W // 2

    def softplus_neg(lam):
        z = -lam
        return jnp.maximum(z, 0.0) + jnp.log1p(jnp.exp(-jnp.abs(z)))

    sp = (softplus_neg(lam_ref[0:1, :]), softplus_neg(lam_ref[1:2, :]))

    def phase_b(q):
        def body(r, carry):
            r0 = pl.multiple_of(r * ROW_TILE, ROW_TILE)
            rows = pl.ds(r0, ROW_TILE)
            ext = q.xlp[pl.ds(r0, ROW_TILE + 2 * SUBLANES), :]
            xc = cb_ref[...]
            for tap in range(4):
                o = SUBLANES - 2 + tap
                xc = xc + ext[o:o + ROW_TILE, :] * cw_ref[tap:tap + 1, :]
            xh = 0.5 * xc
            for d, (a_ref, u_ref) in enumerate(((q.a_f, q.h_f), (q.a_b, q.h_b))):
                bah = 0.5 * ba_ref[d:d + 1, :]
                bih = 0.5 * bi_ref[d:d + 1, :]
                ch = (-0.5 * LRU_C) * sp[d]
                for hh in range(2):
                    cs = slice(hh * half, (hh + 1) * half)
                    pre = _dot(xc[:, cs], wg_ref[d, hh])
                    t_r = jnp.tanh(pre[:, 0:half] + bah[:, cs])
                    t_i = jnp.tanh(pre[:, half:] + bih[:, cs])
                    log_a = t_r * ch[:, cs] + ch[:, cs]
                    a = jnp.exp(log_a)
                    om = -jnp.tanh(log_a) * (a * a + 1.0)
                    root = jnp.where(om > 0.0, om * lax.rsqrt(om), 0.0)
                    u = root * (t_i * xh[:, cs] + xh[:, cs])
                    for j in range(half // LANES):
                        lanes = slice(j * LANES, (j + 1) * LANES)
                        a_ref[hh * (half // LANES) + j, rows, :] = a[:, lanes]
                        u_ref[hh * (half // LANES) + j, rows, :] = u[:, lanes]
            return carry
        return body

    _tile_loop(n // a_tile, phase_a1)
    for q in seqs:
        _tile_loop(nt, phase_b(q))

    row8 = lax.broadcasted_iota(jnp.int32, (SUBLANES, LANES), 0)
    block = SUBLANES * SUBLANES
    n_blocks = n // block
    n_slabs = LRU_W // LANES

    def across_groups(a, b, reverse):
        for s in (1, 2, 4):
            m = (row8 < SUBLANES - s) if reverse else (row8 >= s)
            shift = SUBLANES - s if reverse else s
            a_s = jnp.where(m, pltpu.roll(a, shift, axis=0), 1.0)
            b_s = jnp.where(m, pltpu.roll(b, shift, axis=0), 0.0)
            b = a * b_s + b
            a = a * a_s
        return a, b

    def scan_block(a_ref, h_ref, base, carry, reverse):
        rows = [pl.ds(base + k, SUBLANES, stride=SUBLANES) for k in range(SUBLANES)]
        order = list(reversed(range(SUBLANES))) if reverse else list(range(SUBLANES))
        prod, local = {}, {}
        prev = None
        for k in order:
            a, u = a_ref[rows[k], :], h_ref[rows[k], :]
            prod[k] = a if prev is None else a * prod[prev]
            local[k] = u if prev is None else a * local[prev] + u
            prev = k
        p_all, h_all = across_groups(prod[prev], local[prev], reverse)
        inner = (row8 < SUBLANES - 1) if reverse else (row8 >= 1)
        shift = SUBLANES - 1 if reverse else 1
        enter = (jnp.where(inner, pltpu.roll(p_all, shift, axis=0), 1.0) * carry
                 + jnp.where(inner, pltpu.roll(h_all, shift, axis=0), 0.0))
        for k in order:
            h_ref[rows[k], :] = prod[k] * enter + local[k]
        leave = p_all * carry + h_all
        return leave[0:1, :] if reverse else leave[SUBLANES - 1:SUBLANES, :]

    def initial(q, d, s):
        if latent:
            return q.h0[d:d + 1, s * LANES:(s + 1) * LANES]
        return jnp.zeros((1, LANES), _F32)

    chains = [(q, s) for q in seqs for s in range(n_slabs)]

    def scan_body(i, carry):
        fwd_base = pl.multiple_of(i * block, block)
        bwd_base = pl.multiple_of((n_blocks - 1 - i) * block, block)
        out = []
        for (q, s), (cf, cb) in zip(chains, carry):
            out.append((scan_block(q.a_f.at[s], q.h_f.at[s], fwd_base, cf, False),
                        scan_block(q.a_b.at[s], q.h_b.at[s], bwd_base, cb, True)))
        return tuple(out)

    ends = lax.fori_loop(0, n_blocks, scan_body,
                         tuple((initial(q, 0, s), initial(q, 1, s)) for q, s in chains), unroll=2)

    def lru_out(q):
        def body(r, carry):
            rows = pl.ds(pl.multiple_of(r * ROW_TILE, ROW_TILE), ROW_TILE)
            for s in range(n_slabs):
                lanes = slice(s * LANES, (s + 1) * LANES)
                q.mix[rows, lanes] = (q.h_f[s, rows, :] + q.h_b[s, rows, :]) * q.gy[rows, lanes]
            return carry
        return body

    for (q, s), (last_f, first_b) in zip(chains, ends):
        if not latent:
            q.stl[0:1, s * LANES:(s + 1) * LANES] = last_f
            q.stl[1:2, s * LANES:(s + 1) * LANES] = first_b
    for q in seqs:
        _tile_loop(nt, lru_out(q))

    def phase_a2(r, carry):
        r0 = pl.multiple_of(r * wide, wide)
        rows = pl.ds(r0, wide)
        h = stacked([normed(q, rows) for q in seqs])
        pqk_all = _dot(h, wqk_ref[...])
        pvg_all = _dot(h, wvg_ref[...])
        for s, q in enumerate(seqs):
            part = slice(s * wide, (s + 1) * wide)
            store_retention(q, r0, rows, pqk_all[part], pvg_all[part])
        return carry

    if not early_ret:
        @pl.when(first_step)
        def _():
            late_ret.wait()

        _tile_loop(n // wide, phase_a2)

    lower = (lax.broadcasted_iota(jnp.int32, (ROW_TILE, ROW_TILE), 0)
             >= lax.broadcasted_iota(jnp.int32, (ROW_TILE, ROW_TILE), 1))
    blocks = [slice(r * ROW_TILE, (r + 1) * ROW_TILE) for r in range(nt)]

    def phase_c(q, hd):
        cols = slice(hd * DK, (hd + 1) * DK)
        dec = lambda row: decay[hd][row]
        kv_f = [_dot_tn(q.k_f[hd, rows, :], q.v[rows, cols]) if (r < nt - 1 or not latent) else 0.0
                for r, rows in enumerate(blocks)]
        kv_b = [_dot_tn(q.k_b[hd, rows, :], q.v[rows, cols]) if (r > 0 or not latent) else 0.0
                for r, rows in enumerate(blocks)]
        if latent:
            run_f = q.s0[0, hd] * dec(DEC_G_F)
            run_b = q.s0[1, hd] * dec(DEC_GN_B)
        else:
            run_f = run_b = None
        before = []
        for r in range(nt):
            before.append(run_f)
            run_f = kv_f[r] if run_f is None else run_f + kv_f[r]
        after = [None] * nt
        for r in reversed(range(nt)):
            after[r] = run_b
            run_b = kv_b[r] if run_b is None else run_b + kv_b[r]

        for r, rows in enumerate(blocks):
            qf = q.q_f[hd, rows, :]
            qb = q.q_b[hd, rows, :]
            s = jnp.where(lower, _dot_nt(qf, q.k_f[hd, rows, :]), _dot_nt(qb, q.k_b[hd, rows, :]))
            o = _dot(s, q.v[rows, cols])
            if before[r] is not None:
                o = o + _dot(qf, before[r])
            if after[r] is not None:
                o = o + _dot(qb, after[r])
            o = o * lax.rsqrt(jnp.mean(o * o, axis=-1, keepdims=True) + EPS)
            q.mix[rows, LRU_W + cols.start:LRU_W + cols.stop] = o * q.sg[rows, cols]
        if not latent:
            q.str[0, hd] = run_f * dec(DEC_GN1_F)
            q.str[1, hd] = run_b

    for hd in range(RET_H):
        for q in seqs:
            phase_c(q, hd)

    gain1 = mod(2) * n2_ref[...]
    gain2 = n3_ref[...] * (1.0 + mod(4))
    sh2 = mod(3)

    def phase_d(r, carry):
        rows = pl.ds(pl.multiple_of(r * wide, wide), wide)
        mix_all = _dot(stacked([q.mix[rows, :] for q in seqs]), wout_ref[...])
        h2s = []
        for s, q in enumerate(seqs):
            x1 = q.x[rows, :] + _rms(mix_all[s * wide:(s + 1) * wide], gain1)
            q.x1[rows, :] = x1
            h2s.append(_rms(x1, gain2) + sh2)
            q.h2[rows, :] = h2s[-1].astype(_BF)
        logits_t = _dot(stacked(h2s), rw_ref[...]).T
        for s, q in enumerate(seqs):
            q.lt[:, rows] = logits_t[0:N_EXP, s * wide:(s + 1) * wide]
        return carry

    if latent:
        @pl.when(first_step)
        def _():
            late_out.wait()

    _tile_loop(n // wide, phase_d, unroll=2)


def _mixer_call(x, mod, norms, w_in, lru, wg, w_out, rw, latent, extra=(), layer=0):
    b, n, _ = x.shape
    g = 1 if latent else CONTEXT_GROUP
    const2 = lambda i: (0, 0)
    whole = pl.BlockSpec(memory_space=pl.ANY)
    in_specs = [
        pl.BlockSpec((g, n, D), lambda i: (i, 0, 0)),
        pl.BlockSpec((N_MOD, SUBLANES, D), lambda i: (0, 0, 0)),
        pl.BlockSpec((1, D), const2), pl.BlockSpec((1, D), const2), pl.BlockSpec((1, D), const2),
        pl.BlockSpec((D, D), lambda i: (0, 0)),
    ]
    if latent:
        in_specs += [whole]
    else:
        in_specs += [pl.BlockSpec((D, D), lambda i: (0, 1)),
                     pl.BlockSpec((D, D), lambda i: (0, 2))]
    in_specs += [
        pl.BlockSpec((4, LRU_W), const2), pl.BlockSpec((1, LRU_W), const2),
        pl.BlockSpec((2, LRU_W), const2), pl.BlockSpec((2, LRU_W), const2),
        pl.BlockSpec((2, LRU_W), const2),
        pl.BlockSpec((2, 2, LRU_W // 2, LRU_W), lambda i: (0, 0, 0, 0)),
        whole if latent else pl.BlockSpec((D, D), const2),
        pl.BlockSpec((D, LANES), const2),
    ]
    out_shape = [
        jax.ShapeDtypeStruct((b, n, D), _F32),
        jax.ShapeDtypeStruct((b, n, D), _BF),
        jax.ShapeDtypeStruct((b, N_EXP, n), _F32),
    ]
    out_specs = [
        pl.BlockSpec((g, n, D), lambda i: (i, 0, 0)),
        pl.BlockSpec((g, n, D), lambda i: (i, 0, 0)),
        pl.BlockSpec((g, N_EXP, n), lambda i: (i, 0, 0)),
    ]
    if latent:
        in_specs += [
            pl.BlockSpec((g, None, 2, LRU_W), lambda i: (i, layer, 0, 0)),
            pl.BlockSpec((g, None, 2, RET_H, DK, DK), lambda i: (i, layer, 0, 0, 0, 0)),
            pl.BlockSpec((n, DK), const2),
            pl.BlockSpec((n, DK), const2),
        ]
    else:
        out_shape += [
            jax.ShapeDtypeStruct((b, 1, 2, LRU_W), _F32),
            jax.ShapeDtypeStruct((b, 1, 2, RET_H, DK, DK), _F32),
        ]
        out_specs += [
            pl.BlockSpec((g, None, 2, LRU_W), lambda i: (i, 0, 0, 0)),
            pl.BlockSpec((g, None, 2, RET_H, DK, DK), lambda i: (i, 0, 0, 0, 0, 0)),
        ]
    f32s = lambda shape: pltpu.VMEM((g,) + shape, _F32)
    slabs = (LRU_W // LANES, n, LANES)
    scratch = [
        f32s((n + 2 * SUBLANES, LRU_W)),
        f32s((n, LRU_W)),
        f32s(slabs), f32s(slabs),
        f32s(slabs), f32s(slabs),
    ]
    if latent:
        scratch += [pltpu.VMEM((D, 2 * D), _F32), pltpu.VMEM((D, D), _F32),
                    pltpu.SemaphoreType.DMA((2,))]
        w_in_args = (w_in, w_in)
    else:
        scratch += [f32s((n, RET_W))] + [f32s(slabs) for _ in range(4)] + [f32s((n, RET_W))]
        w_in_args = (w_in, w_in, w_in)
    return pl.pallas_call(
        functools.partial(_mixer_kernel, n=n, g=g, latent=latent),
        out_shape=out_shape,
        grid=(b // g,),
        in_specs=in_specs,
        out_specs=out_specs,
        scratch_shapes=scratch,
        compiler_params=pltpu.CompilerParams(
            dimension_semantics=("arbitrary",), vmem_limit_bytes=VMEM_LIMIT),
        name="mixer_latent" if latent else "mixer_context",
    )(x, mod, *norms, *w_in_args, *lru, wg, w_out, rw, *extra)


def _count(mask):
    return jnp.sum(jnp.where(mask, 1.0, 0.0), axis=-1, keepdims=True)


def _probs(l3):
    bsz, _, n = l3.shape
    m = jnp.max(l3, axis=1, keepdims=True)
    e = jnp.exp(l3 - m)
    return (e / jnp.sum(e, axis=1, keepdims=True)).reshape(bsz * N_EXP, n)


def _finish_select(p, bits, thr, cap):
    rows, n = p.shape
    gt = bits > thr
    eq = bits == thr
    need = float(cap) - _count(gt)
    idx = lax.broadcasted_iota(jnp.int32, (rows, n), 1)
    nbits = int(math.log2(n))

    def idx_body(i, j):
        cand = j | (jnp.int32(1) << (nbits - 1 - i))
        return jnp.where(_count(eq & (idx < cand)) < need, cand, j)

    jlast = lax.fori_loop(0, nbits, idx_body, jnp.zeros((rows, 1), jnp.int32))
    sel = gt | (eq & (idx <= jlast))
    before = (lax.broadcasted_iota(jnp.int32, (n, n), 0)
              < lax.broadcasted_iota(jnp.int32, (n, n), 1))
    pos = _dot(jnp.where(sel, 1.0, 0.0).astype(_BF), jnp.where(before, 1.0, 0.0).astype(_BF))
    return jnp.where(sel, pos, -1.0), jnp.where(sel, p, 0.0)


def _route_kernel(lp_ref, ls_ref, pp_ref, gp_ref, ps_ref, gs_ref, *, cap_p, cap_s):
    groups = ((_probs(lp_ref[...]), float(cap_p)), (_probs(ls_ref[...]), float(cap_s)))
    bits = [pltpu.bitcast(p, jnp.int32) for p, _ in groups]

    def val_body(i, thr):
        out = []
        for b, t, (_, capf) in zip(bits, thr, groups):
            cand = t | (jnp.int32(1) << (30 - i))
            out.append(jnp.where(_count(b >= cand) >= capf, cand, t))
        return tuple(out)

    thr = lax.fori_loop(0, 31, val_body,
                        tuple(jnp.zeros((b.shape[0], 1), jnp.int32) for b in bits))
    pos, gate = _finish_select(groups[0][0], bits[0], thr[0], cap_p)
    pp_ref[...] = pos.reshape(pp_ref.shape)
    gp_ref[...] = gate.reshape(gp_ref.shape)
    pos, gate = _finish_select(groups[1][0], bits[1], thr[1], cap_s)
    ps_ref[...] = pos.reshape(ps_ref.shape)
    gs_ref[...] = gate.reshape(gs_ref.shape)


def _route_call(lt_p, lt_s, cap_p, cap_s):
    shapes = [
        jax.ShapeDtypeStruct(lt_p.shape, _F32), jax.ShapeDtypeStruct(lt_p.shape, _F32),
        jax.ShapeDtypeStruct(lt_s.shape, _F32), jax.ShapeDtypeStruct(lt_s.shape, _F32),
    ]
    return pl.pallas_call(
        functools.partial(_route_kernel, cap_p=cap_p, cap_s=cap_s),
        out_shape=shapes,
        compiler_params=pltpu.CompilerParams(vmem_limit_bytes=VMEM_LIMIT),
        name="route_select",
    )(lt_p, lt_s)


def _dispatch_kernel(pos_ref, gate_ref, h_ref, xs_ref, gs_ref, *, n, cap, g):
    slot = lax.broadcasted_iota(jnp.int32, (cap, n), 0).astype(_F32)
    for j in range(g):
        slots = slice(j * cap, (j + 1) * cap)
        parts = []
        for e in range(N_EXP):
            hit = pos_ref[j, e:e + 1, :] == slot
            parts.append(jnp.where(hit, 1.0, 0.0).astype(_BF))
            gs_ref[e, slots, :] = jnp.sum(jnp.where(hit, gate_ref[j, e:e + 1, :], 0.0),
                                          axis=-1, keepdims=True)
        onehot = jnp.concatenate(parts, axis=0)
        xs = _dot(onehot, h_ref[j]).astype(_BF)
        for e in range(N_EXP):
            xs_ref[e, slots, :] = xs[e * cap:(e + 1) * cap, :]


def _dispatch_call(pos, gate, h2, cap):
    b, n, _ = h2.shape
    g = max(1, SMALL_STEP_TOKENS // n)
    return pl.pallas_call(
        functools.partial(_dispatch_kernel, n=n, cap=cap, g=g),
        out_shape=[
            jax.ShapeDtypeStruct((N_EXP, b * cap, D), _BF),
            jax.ShapeDtypeStruct((N_EXP, b * cap, 1), _F32),
        ],
        grid=(b // g,),
        in_specs=[
            pl.BlockSpec((g, N_EXP, n), lambda i: (i, 0, 0)),
            pl.BlockSpec((g, N_EXP, n), lambda i: (i, 0, 0)),
            pl.BlockSpec((g, n, D), lambda i: (i, 0, 0)),
        ],
        out_specs=[
            pl.BlockSpec((N_EXP, g * cap, D), lambda i: (0, i, 0)),
            pl.BlockSpec((N_EXP, g * cap, 1), lambda i: (0, i, 0)),
        ],
        compiler_params=pltpu.CompilerParams(
            dimension_semantics=("arbitrary",), vmem_limit_bytes=VMEM_LIMIT),
        name="dispatch",
    )(pos, gate, h2)


def _expert_kernel(xp_ref, xs_ref, gp_ref, gs_ref, wg_ref, wu_ref, wd_ref, y_ref, xcat, acc,
                   *, sp, nf, tf):
    f = pl.program_id(1)
    xcat[0:sp, :] = xp_ref[...]
    xcat[sp:, :] = xs_ref[...]
    x = xcat[...]
    total = jnp.where(f == 0, 0.0, acc[...])
    for c in range(tf // FF_CHUNK):
        cs = slice(c * FF_CHUNK, (c + 1) * FF_CHUNK)
        hg = _dot(x, wg_ref[:, cs].astype(_BF))
        hu = _dot(x, wu_ref[:, cs].astype(_BF))
        hid = (_silu(hg) * hu).astype(_BF)
        total = total + _dot(hid, wd_ref[cs, :].astype(_BF))
    acc[...] = total
    y_ref[0:sp, :] = (total[0:sp, :] * gp_ref[...]).astype(_BF)
    y_ref[sp:, :] = (total[sp:, :] * gs_ref[...]).astype(_BF)


def _expert_call(xs_p, xs_s, g_p, g_s, w_gate, w_up, w_down):
    tf = 1024
    sp = xs_p.shape[1]
    ss = xs_s.shape[1]
    nf = FF // tf
    return pl.pallas_call(
        functools.partial(_expert_kernel, sp=sp, nf=nf, tf=tf),
        out_shape=jax.ShapeDtypeStruct((N_EXP, sp + ss, D), _BF),
        grid=(N_EXP, nf),
        in_specs=[
            pl.BlockSpec((None, sp, D), lambda e, f: (e, 0, 0)),
            pl.BlockSpec((None, ss, D), lambda e, f: (e, 0, 0)),
            pl.BlockSpec((None, sp, 1), lambda e, f: (e, 0, 0)),
            pl.BlockSpec((None, ss, 1), lambda e, f: (e, 0, 0)),
            pl.BlockSpec((None, D, tf), lambda e, f: (e, 0, f)),
            pl.BlockSpec((None, D, tf), lambda e, f: (e, 0, f)),
            pl.BlockSpec((None, tf, D), lambda e, f: (e, f, 0)),
        ],
        out_specs=pl.BlockSpec((None, sp + ss, D), lambda e, f: (e, 0, 0)),
        scratch_shapes=[pltpu.VMEM((sp + ss, D), _BF), pltpu.VMEM((sp + ss, D), _F32)],
        compiler_params=pltpu.CompilerParams(
            dimension_semantics=("arbitrary", "arbitrary"), vmem_limit_bytes=VMEM_LIMIT),
        name="expert_ffn",
    )(xs_p, xs_s, g_p, g_s, w_gate, w_up, w_down)


def _combine_kernel(pos_ref, ye_ref, x1_ref, mod_ref, n4_ref, y_ref, *, n, cap, g, latent):
    width = N_EXP * cap
    lane = lax.broadcasted_iota(jnp.int32, (N_EXP, width), 1)
    expand = jnp.where(lane // cap == lax.broadcasted_iota(jnp.int32, (N_EXP, width), 0),
                       1.0, 0.0).astype(_BF)
    tile = min(n, COMBINE_TILE)
    slot = (lax.broadcasted_iota(jnp.int32, (tile, width), 1) % cap).astype(_F32)
    mod_row = (pl.program_id(0) + 1) if latent else 0
    gain = mod_ref[N_MOD - 1, pl.ds(mod_row, 1), :] * n4_ref[...]

    for j in range(g):
        ye = ye_ref[:, j * cap:(j + 1) * cap, :].reshape(width, D)

        def body(r, carry, j=j, ye=ye):
            rows = pl.ds(pl.multiple_of(r * tile, tile), tile)
            pos_e = _dot_tn(pos_ref[j, :, rows].astype(_BF), expand)
            onehot = jnp.where(pos_e == slot, 1.0, 0.0).astype(_BF)
            f = _dot(onehot, ye)
            y_ref[j, rows, :] = x1_ref[j, rows, :] + _rms(f, gain)
            return carry

        _tile_loop(n // tile, body)


def _combine_call(pos, ye, x1, mod, norm_post, cap, slot_off, latent):
    b, n, _ = x1.shape
    g = max(1, SMALL_STEP_TOKENS // n)
    assert g == 1 or not latent
    blk_off = slot_off // (g * cap)
    return pl.pallas_call(
        functools.partial(_combine_kernel, n=n, cap=cap, g=g, latent=latent),
        out_shape=jax.ShapeDtypeStruct((b, n, D), _F32),
        grid=(b // g,),
        in_specs=[
            pl.BlockSpec((g, N_EXP, n), lambda i: (i, 0, 0)),
            pl.BlockSpec((N_EXP, g * cap, D), lambda i: (0, i + blk_off, 0)),
            pl.BlockSpec((g, n, D), lambda i: (i, 0, 0)),
            pl.BlockSpec((N_MOD, SUBLANES, D), lambda i: (0, 0, 0)),
            pl.BlockSpec((1, D), lambda i: (0, 0)),
        ],
        out_specs=pl.BlockSpec((g, n, D), lambda i: (i, 0, 0)),
        compiler_params=pltpu.CompilerParams(
            dimension_semantics=("arbitrary",), vmem_limit_bytes=VMEM_LIMIT),
        name="combine_latent" if latent else "combine_context",
    )(pos, ye, x1, mod, norm_post)


def _block_diag_gates(wa, wi):
    per_half = LRU_HEADS // 2
    side = per_half * LRU_HD
    on_diag = (np.arange(side)[:, None] // LRU_HD) == (np.arange(side)[None, :] // LRU_HD)

    def bd(w):
        rows = w.reshape(2, 2, side, LRU_HD)
        return jnp.where(on_diag, jnp.tile(rows, (1, 1, 1, per_half)), 0.0)

    return 0.5 * jnp.concatenate([bd(wa), bd(wi)], axis=-1)


def _rope_tables(n):
    rows = n // GRID_W
    row = np.repeat(np.arange(rows, dtype=np.float32), GRID_W)
    col = np.tile(np.arange(GRID_W, dtype=np.float32), rows)
    nf = DK // 4
    freqs = np.float32(ROPE_BASE) ** (-np.arange(nf, dtype=np.float32) / np.float32(nf))
    ang = np.concatenate([row[:, None] * freqs, col[:, None] * freqs], axis=-1).astype(np.float32)
    cos = np.cos(ang)
    sin = np.sin(ang)
    return (jnp.asarray(np.concatenate([cos, cos], axis=-1), _F32),
            jnp.asarray(np.concatenate([-sin, sin], axis=-1), _F32))


def _decay_consts(n):
    heads = np.arange(RET_H, dtype=np.float32)
    f32 = np.float32
    lgf = np.log1p(-np.exp2(-(f32(RET_DECAY_OFFSET_FWD) + heads))).astype(f32)
    lgb = np.log1p(-np.exp2(-(f32(RET_DECAY_OFFSET_BWD) + heads))).astype(f32)
    tab = np.stack([lgf, lgb, np.exp(lgf), np.exp(f32(n) * lgb), np.exp(f32(n - 1) * lgf)], axis=1)
    return [[float(v) for v in row] for row in tab.astype(f32)]


def kernel(x_prompt, x_sample, c, state_lru, state_ret, c_ctx, ada_w, ada_b, norm_mix_pre, norm_mix_post, norm_ffn_pre, norm_ffn_post, w_in, conv_w, conv_b, lru_wa, lru_ba, lru_wi, lru_bi, lru_lambda, w_out, router_w, exp_w_gate, exp_w_up, exp_w_down):
    bp, n_p, _ = x_prompt.shape
    bs, n_s, _ = x_sample.shape
    cap_p = 2 * n_p // N_EXP
    cap_s = 2 * n_s // N_EXP
    l = 0

    mod = _ada_call(c_ctx[None, :], c, ada_w[l], ada_b[l][None, :])

    norms = (norm_mix_pre[l][None], norm_mix_post[l][None], norm_ffn_pre[l][None])
    lru = (conv_w[l], conv_b[l][None], lru_ba[l], lru_bi[l], lru_lambda[l])
    wg = _block_diag_gates(lru_wa[l], lru_wi[l])
    rw = jnp.pad(router_w[l], ((0, 0), (0, LANES - N_EXP)))
    cos2, sin2 = _rope_tables(n_s)

    x1_p, h2_p, lt_p, st_lru, st_ret = _mixer_call(
        x_prompt, mod, norms, w_in[l], lru, wg, w_out[l], rw, latent=False)
    x1_s, h2_s, lt_s = _mixer_call(
        x_sample, mod, norms, w_in[l], lru, wg, w_out[l], rw, latent=True,
        extra=(state_lru, state_ret, cos2, sin2), layer=l)

    pos_p, gate_p, pos_s, gate_s = _route_call(lt_p, lt_s, cap_p, cap_s)
    xs_p, gsl_p = _dispatch_call(pos_p, gate_p, h2_p, cap_p)
    xs_s, gsl_s = _dispatch_call(pos_s, gate_s, h2_s, cap_s)
    ye = _expert_call(xs_p, xs_s, gsl_p, gsl_s, exp_w_gate[l], exp_w_up[l], exp_w_down[l])

    norm_post = norm_ffn_post[l][None]
    y_p = _combine_call(pos_p, ye, x1_p, mod, norm_post, cap_p, 0, latent=False)
    y_s = _combine_call(pos_s, ye, x1_s, mod, norm_post, cap_s, bp * cap_p, latent=True)
    return (y_p, y_s, st_lru, st_ret)
```

```python
import functools
import math
import types

import jax
import jax.numpy as jnp
import numpy as np
from jax import lax
from jax.experimental import pallas as pl
from jax.experimental.pallas import tpu as pltpu

D = 1024
LRU_W = 512
LRU_HEADS = 8
LRU_HD = 64
LRU_C = 8.0
RET_W = 512
RET_H = 4
DK = 128
N_EXP = 16
FF = 2048
N_MOD = 6
EPS = 1e-6
GRID_W = 64
ROPE_BASE = 10000.0
RET_DECAY_OFFSET_FWD = 5.0
RET_DECAY_OFFSET_BWD = 5.5

ROW_TILE = 256
WIDE_TILE = 512
FF_CHUNK = 512
COMBINE_TILE = 1024
CONTEXT_GROUP = 2
SMALL_STEP_TOKENS = 1024
SUBLANES = 8
LANES = 128
VMEM_LIMIT = 60 * 1024 * 1024

DEC_LOG_F, DEC_LOG_B, DEC_G_F, DEC_GN_B, DEC_GN1_F = 0, 1, 2, 3, 4

_BF = jnp.bfloat16
_F32 = jnp.float32


def _sigmoid(x):
    return 0.5 * jnp.tanh(0.5 * x) + 0.5


def _silu(x):
    return x * _sigmoid(x)


def _gelu_tanh(x):
    c = math.sqrt(2.0 / math.pi)
    return 0.5 * x * (1.0 + jnp.tanh(c * (x + 0.044715 * (x * x * x))))


def _rms(x, gain):
    return x * lax.rsqrt(jnp.mean(x * x, axis=-1, keepdims=True) + EPS) * gain


def _dot(a, b):
    return jnp.dot(a, b, preferred_element_type=_F32)


def _dot_nt(a, b):
    return lax.dot_general(a, b, (((1,), (1,)), ((), ())), preferred_element_type=_F32)


def _dot_tn(a, b):
    return lax.dot_general(a, b, (((0,), (0,)), ((), ())), preferred_element_type=_F32)


def _ada_kernel(cc_ref, c_ref, w_ref, b_ref, o_ref, s_ref):
    nb = c_ref.shape[0]
    s_ref[...] = jnp.zeros_like(s_ref)
    s_ref[0:1, :] = _silu(cc_ref[...])
    s_ref[1:1 + nb, :] = _silu(c_ref[...])
    o_ref[...] = _dot(s_ref[...], w_ref[...]) + b_ref[...]


def _ada_call(c_ctx, c, ada_w, ada_b):
    nb = c.shape[0]
    assert nb + 1 <= SUBLANES
    return pl.pallas_call(
        _ada_kernel,
        out_shape=jax.ShapeDtypeStruct((N_MOD, SUBLANES, D), _F32),
        grid=(N_MOD,),
        in_specs=[
            pl.BlockSpec((1, D), lambda j: (0, 0)),
            pl.BlockSpec((nb, D), lambda j: (0, 0)),
            pl.BlockSpec((D, D), lambda j: (0, j)),
            pl.BlockSpec((1, D), lambda j: (0, j)),
        ],
        out_specs=pl.BlockSpec((None, SUBLANES, D), lambda j: (j, 0, 0)),
        scratch_shapes=[pltpu.VMEM((SUBLANES, D), _F32)],
        compiler_params=pltpu.CompilerParams(
            dimension_semantics=("arbitrary",), vmem_limit_bytes=VMEM_LIMIT),
        name="ada_mod",
    )(c_ctx, c, ada_w, ada_b)


def _tile_loop(nt, body, unroll=1):
    if nt == 1:
        body(0, 0)
    else:
        lax.fori_loop(0, nt, body, 0, unroll=unroll)


def _mixer_kernel(*refs, n, g, latent):
    if latent:
        (x_ref, mod_ref, n1_ref, n2_ref, n3_ref, wl_ref, win_hbm,
         cw_ref, cb_ref, ba_ref, bi_ref, lam_ref, wg_ref, wout_hbm, rw_ref,
         h0_ref, s0_ref, cos_ref, sin_ref,
         x1_ref, h2_ref, lt_ref,
         xlp_g, gy_g, af_g, ab_g, hf_g, hb_g, wret_v, wout_ref, late_sem) = refs
        sg_g, qf_g, qb_g, kf_g, kb_g, v_g = xlp_g, af_g, ab_g, hf_g, hb_g, gy_g
        wqk_ref, wvg_ref = wret_v.at[:, 0:D], wret_v.at[:, D:2 * D]
        first_step = pl.program_id(0) == 0
        late_ret = pltpu.make_async_copy(win_hbm.at[:, pl.ds(D, 2 * D)], wret_v, late_sem.at[0])
        late_out = pltpu.make_async_copy(wout_hbm, wout_ref, late_sem.at[1])

        @pl.when(first_step)
        def _():
            late_ret.start()
            late_out.start()
    else:
        (x_ref, mod_ref, n1_ref, n2_ref, n3_ref, wl_ref, wqk_ref, wvg_ref,
         cw_ref, cb_ref, ba_ref, bi_ref, lam_ref, wg_ref, wout_ref, rw_ref,
         x1_ref, h2_ref, lt_ref, stl_ref, str_ref,
         xlp_g, gy_g, af_g, ab_g, hf_g, hb_g, sg_g, qf_g, qb_g, kf_g, kb_g, v_g) = refs
    early_ret = not latent

    seqs = []
    for s in range(g):
        q = types.SimpleNamespace(
            x=x_ref.at[s], x1=x1_ref.at[s], h2=h2_ref.at[s], lt=lt_ref.at[s], mix=x1_ref.at[s],
            xlp=xlp_g.at[s], sg=sg_g.at[s], gy=gy_g.at[s], v=v_g.at[s],
            a_f=af_g.at[s], a_b=ab_g.at[s], h_f=hf_g.at[s], h_b=hb_g.at[s],
            q_f=qf_g.at[s], q_b=qb_g.at[s], k_f=kf_g.at[s], k_b=kb_g.at[s])
        if latent:
            q.h0, q.s0 = h0_ref.at[s], s0_ref.at[s]
        else:
            q.stl, q.str = stl_ref.at[s], str_ref.at[s]
        seqs.append(q)

    nt = n // ROW_TILE
    decay = _decay_consts(n)
    mod_row = (pl.program_id(0) + 1) if latent else 0
    mod = lambda k: mod_ref[k, pl.ds(mod_row, 1), :]
    shift = mod(0)
    scale = n1_ref[...] * (1.0 + mod(1))

    def normed(q, rows):
        return _rms(q.x[rows, :], scale) + shift

    wide = min(n, WIDE_TILE)
    a_tile = wide if latent else ROW_TILE

    def stacked(parts):
        return parts[0] if len(parts) == 1 else jnp.concatenate(parts, axis=0)

    def store_retention(q, r0, rows, pqk, pvg):
        q.sg[rows, :] = _silu(pvg[:, RET_W:])
        tpos = (r0 + lax.broadcasted_iota(jnp.int32, (pqk.shape[0], DK), 0)).astype(_F32)
        if latent:
            cos2 = cos_ref[rows, :]
            sin2 = sin_ref[rows, :]
        for hd in range(RET_H):
            cols = slice(hd * DK, (hd + 1) * DK)
            qh = pqk[:, cols] * (DK ** -0.5)
            kh = pqk[:, RET_W + hd * DK:RET_W + (hd + 1) * DK]
            if latent:
                qh = qh * cos2 + pltpu.roll(qh, DK // 2, axis=1) * sin2
                kh = kh * cos2 + pltpu.roll(kh, DK // 2, axis=1) * sin2
            lgf = decay[hd][DEC_LOG_F]
            lgb = decay[hd][DEC_LOG_B]
            q.q_f[hd, rows, :] = qh * jnp.exp(tpos * lgf)
            q.k_f[hd, rows, :] = kh * jnp.exp(tpos * (-lgf))
            q.q_b[hd, rows, :] = qh * jnp.exp(tpos * (-lgb))
            q.k_b[hd, rows, :] = kh * jnp.exp(tpos * lgb)
        q.v[rows, :] = pvg[:, 0:RET_W]

    def phase_a1(r, carry):
        r0 = pl.multiple_of(r * a_tile, a_tile)
        rows = pl.ds(r0, a_tile)
        h = stacked([normed(q, rows) for q in seqs])
        p_all = _dot(h, wl_ref[...])
        if early_ret:
            pqk_all = _dot(h, wqk_ref[...])
            pvg_all = _dot(h, wvg_ref[...])
        for s, q in enumerate(seqs):
            part = slice(s * a_tile, (s + 1) * a_tile)
            q.xlp[pl.ds(r0 + SUBLANES, a_tile), :] = p_all[part, 0:LRU_W]
            q.gy[rows, :] = _gelu_tanh(p_all[part, LRU_W:])
            if early_ret:
                store_retention(q, r0, rows, pqk_all[part], pvg_all[part])
        return carry

    for q in seqs:
        q.xlp[0:SUBLANES, :] = jnp.zeros((SUBLANES, LRU_W), _F32)
        q.xlp[n + SUBLANES:n + 2 * SUBLANES, :] = jnp.zeros((SUBLANES, LRU_W), _F32)

    half = LRU_W // 2

    def softplus_neg(lam):
        z = -lam
        return jnp.maximum(z, 0.0) + jnp.log1p(jnp.exp(-jnp.abs(z)))

    sp = (softplus_neg(lam_ref[0:1, :]), softplus_neg(lam_ref[1:2, :]))

    def phase_b(q):
        def body(r, carry):
            r0 = pl.multiple_of(r * a_tile, a_tile)
            rows = pl.ds(r0, a_tile)
            ext = q.xlp[pl.ds(r0, a_tile + 2 * SUBLANES), :]
            xc = cb_ref[...]
            for tap in range(4):
                o = SUBLANES - 2 + tap
                xc = xc + ext[o:o + a_tile, :] * cw_ref[tap:tap + 1, :]
            xh = 0.5 * xc
            for d, (a_ref, u_ref) in enumerate(((q.a_f, q.h_f), (q.a_b, q.h_b))):
                bah = 0.5 * ba_ref[d:d + 1, :]
                bih = 0.5 * bi_ref[d:d + 1, :]
                ch = (-0.5 * LRU_C) * sp[d]
                for hh in range(2):
                    cs = slice(hh * half, (hh + 1) * half)
                    pre = _dot(xc[:, cs], wg_ref[d, hh])
                    t_r = jnp.tanh(pre[:, 0:half] + bah[:, cs])
                    t_i = jnp.tanh(pre[:, half:] + bih[:, cs])
                    log_a = t_r * ch[:, cs] + ch[:, cs]
                    a = jnp.exp(log_a)
                    om = -jnp.tanh(log_a) * (a * a + 1.0)
                    root = jnp.where(om > 0.0, om * lax.rsqrt(om), 0.0)
                    u = root * (t_i * xh[:, cs] + xh[:, cs])
                    for j in range(half // LANES):
                        lanes = slice(j * LANES, (j + 1) * LANES)
                        a_ref[hh * (half // LANES) + j, rows, :] = a[:, lanes]
                        u_ref[hh * (half // LANES) + j, rows, :] = u[:, lanes]
            return carry
        return body

    _tile_loop(n // a_tile, phase_a1)
    for q in seqs:
        _tile_loop(n // a_tile, phase_b(q))

    row8 = lax.broadcasted_iota(jnp.int32, (SUBLANES, LANES), 0)
    block = SUBLANES * SUBLANES
    n_blocks = n // block
    n_slabs = LRU_W // LANES

    def across_groups(a, b, reverse):
        for s in (1, 2, 4):
            m = (row8 < SUBLANES - s) if reverse else (row8 >= s)
            shift = SUBLANES - s if reverse else s
            a_s = jnp.where(m, pltpu.roll(a, shift, axis=0), 1.0)
            b_s = jnp.where(m, pltpu.roll(b, shift, axis=0), 0.0)
            b = a * b_s + b
            a = a * a_s
        return a, b

    def scan_block(a_ref, h_ref, base, carry, reverse):
        rows = [pl.ds(base + k, SUBLANES, stride=SUBLANES) for k in range(SUBLANES)]
        order = list(reversed(range(SUBLANES))) if reverse else list(range(SUBLANES))
        prod, local = {}, {}
        prev = None
        for k in order:
            a, u = a_ref[rows[k], :], h_ref[rows[k], :]
            prod[k] = a if prev is None else a * prod[prev]
            local[k] = u if prev is None else a * local[prev] + u
            prev = k
        p_all, h_all = across_groups(prod[prev], local[prev], reverse)
        inner = (row8 < SUBLANES - 1) if reverse else (row8 >= 1)
        shift = SUBLANES - 1 if reverse else 1
        enter = (jnp.where(inner, pltpu.roll(p_all, shift, axis=0), 1.0) * carry
                 + jnp.where(inner, pltpu.roll(h_all, shift, axis=0), 0.0))
        for k in order:
            h_ref[rows[k], :] = prod[k] * enter + local[k]
        leave = p_all * carry + h_all
        return leave[0:1, :] if reverse else leave[SUBLANES - 1:SUBLANES, :]

    def initial(q, d, s):
        if latent:
            return q.h0[d:d + 1, s * LANES:(s + 1) * LANES]
        return jnp.zeros((1, LANES), _F32)

    chains = [(q, s) for q in seqs for s in range(n_slabs)]

    def scan_body(i, carry):
        fwd_base = pl.multiple_of(i * block, block)
        bwd_base = pl.multiple_of((n_blocks - 1 - i) * block, block)
        out = []
        for (q, s), (cf, cb) in zip(chains, carry):
            out.append((scan_block(q.a_f.at[s], q.h_f.at[s], fwd_base, cf, False),
                        scan_block(q.a_b.at[s], q.h_b.at[s], bwd_base, cb, True)))
        return tuple(out)

    ends = lax.fori_loop(0, n_blocks, scan_body,
                         tuple((initial(q, 0, s), initial(q, 1, s)) for q, s in chains), unroll=2)

    def lru_out(q):
        def body(r, carry):
            rows = pl.ds(pl.multiple_of(r * ROW_TILE, ROW_TILE), ROW_TILE)
            for s in range(n_slabs):
                lanes = slice(s * LANES, (s + 1) * LANES)
                q.mix[rows, lanes] = (q.h_f[s, rows, :] + q.h_b[s, rows, :]) * q.gy[rows, lanes]
            return carry
        return body

    for (q, s), (last_f, first_b) in zip(chains, ends):
        if not latent:
            q.stl[0:1, s * LANES:(s + 1) * LANES] = last_f
            q.stl[1:2, s * LANES:(s + 1) * LANES] = first_b
    for q in seqs:
        _tile_loop(nt, lru_out(q))

    def phase_a2(r, carry):
        r0 = pl.multiple_of(r * wide, wide)
        rows = pl.ds(r0, wide)
        h = stacked([normed(q, rows) for q in seqs])
        pqk_all = _dot(h, wqk_ref[...])
        pvg_all = _dot(h, wvg_ref[...])
        for s, q in enumerate(seqs):
            part = slice(s * wide, (s + 1) * wide)
            store_retention(q, r0, rows, pqk_all[part], pvg_all[part])
        return carry

    if not early_ret:
        @pl.when(first_step)
        def _():
            late_ret.wait()

        _tile_loop(n // wide, phase_a2)

    lower = (lax.broadcasted_iota(jnp.int32, (ROW_TILE, ROW_TILE), 0)
             >= lax.broadcasted_iota(jnp.int32, (ROW_TILE, ROW_TILE), 1))
    blocks = [slice(r * ROW_TILE, (r + 1) * ROW_TILE) for r in range(nt)]

    def phase_c(q, hd):
        cols = slice(hd * DK, (hd + 1) * DK)
        dec = lambda row: decay[hd][row]
        kv_f = [_dot_tn(q.k_f[hd, rows, :], q.v[rows, cols]) if (r < nt - 1 or not latent) else 0.0
                for r, rows in enumerate(blocks)]
        kv_b = [_dot_tn(q.k_b[hd, rows, :], q.v[rows, cols]) if (r > 0 or not latent) else 0.0
                for r, rows in enumerate(blocks)]
        if latent:
            run_f = q.s0[0, hd] * dec(DEC_G_F)
            run_b = q.s0[1, hd] * dec(DEC_GN_B)
        else:
            run_f = run_b = None
        before = []
        for r in range(nt):
            before.append(run_f)
            run_f = kv_f[r] if run_f is None else run_f + kv_f[r]
        after = [None] * nt
        for r in reversed(range(nt)):
            after[r] = run_b
            run_b = kv_b[r] if run_b is None else run_b + kv_b[r]

        for r, rows in enumerate(blocks):
            qf = q.q_f[hd, rows, :]
            qb = q.q_b[hd, rows, :]
            s = jnp.where(lower, _dot_nt(qf, q.k_f[hd, rows, :]), _dot_nt(qb, q.k_b[hd, rows, :]))
            o = _dot(s, q.v[rows, cols])
            if before[r] is not None:
                o = o + _dot(qf, before[r])
            if after[r] is not None:
                o = o + _dot(qb, after[r])
            o = o * lax.rsqrt(jnp.mean(o * o, axis=-1, keepdims=True) + EPS)
            q.mix[rows, LRU_W + cols.start:LRU_W + cols.stop] = o * q.sg[rows, cols]
        if not latent:
            q.str[0, hd] = run_f * dec(DEC_GN1_F)
            q.str[1, hd] = run_b

    for hd in range(RET_H):
        for q in seqs:
            phase_c(q, hd)

    gain1 = mod(2) * n2_ref[...]
    gain2 = n3_ref[...] * (1.0 + mod(4))
    sh2 = mod(3)

    def phase_d(r, carry):
        rows = pl.ds(pl.multiple_of(r * wide, wide), wide)
        mix_all = _dot(stacked([q.mix[rows, :] for q in seqs]), wout_ref[...])
        h2s = []
        for s, q in enumerate(seqs):
            x1 = q.x[rows, :] + _rms(mix_all[s * wide:(s + 1) * wide], gain1)
            q.x1[rows, :] = x1
            h2s.append(_rms(x1, gain2) + sh2)
            q.h2[rows, :] = h2s[-1].astype(_BF)
        logits_t = _dot(stacked(h2s), rw_ref[...]).T
        for s, q in enumerate(seqs):
            q.lt[:, rows] = logits_t[0:N_EXP, s * wide:(s + 1) * wide]
        return carry

    if latent:
        @pl.when(first_step)
        def _():
            late_out.wait()

    _tile_loop(n // wide, phase_d, unroll=2)


def _mixer_call(x, mod, norms, w_in, lru, wg, w_out, rw, latent, extra=(), layer=0):
    b, n, _ = x.shape
    g = 1 if latent else CONTEXT_GROUP
    const2 = lambda i: (0, 0)
    whole = pl.BlockSpec(memory_space=pl.ANY)
    in_specs = [
        pl.BlockSpec((g, n, D), lambda i: (i, 0, 0)),
        pl.BlockSpec((N_MOD, SUBLANES, D), lambda i: (0, 0, 0)),
        pl.BlockSpec((1, D), const2), pl.BlockSpec((1, D), const2), pl.BlockSpec((1, D), const2),
        pl.BlockSpec((D, D), lambda i: (0, 0)),
    ]
    if latent:
        in_specs += [whole]
    else:
        in_specs += [pl.BlockSpec((D, D), lambda i: (0, 1)),
                     pl.BlockSpec((D, D), lambda i: (0, 2))]
    in_specs += [
        pl.BlockSpec((4, LRU_W), const2), pl.BlockSpec((1, LRU_W), const2),
        pl.BlockSpec((2, LRU_W), const2), pl.BlockSpec((2, LRU_W), const2),
        pl.BlockSpec((2, LRU_W), const2),
        pl.BlockSpec((2, 2, LRU_W // 2, LRU_W), lambda i: (0, 0, 0, 0)),
        whole if latent else pl.BlockSpec((D, D), const2),
        pl.BlockSpec((D, LANES), const2),
    ]
    out_shape = [
        jax.ShapeDtypeStruct((b, n, D), _F32),
        jax.ShapeDtypeStruct((b, n, D), _BF),
        jax.ShapeDtypeStruct((b, N_EXP, n), _F32),
    ]
    out_specs = [
        pl.BlockSpec((g, n, D), lambda i: (i, 0, 0)),
        pl.BlockSpec((g, n, D), lambda i: (i, 0, 0)),
        pl.BlockSpec((g, N_EXP, n), lambda i: (i, 0, 0)),
    ]
    if latent:
        in_specs += [
            pl.BlockSpec((g, None, 2, LRU_W), lambda i: (i, layer, 0, 0)),
            pl.BlockSpec((g, None, 2, RET_H, DK, DK), lambda i: (i, layer, 0, 0, 0, 0)),
            pl.BlockSpec((n, DK), const2),
            pl.BlockSpec((n, DK), const2),
        ]
    else:
        out_shape += [
            jax.ShapeDtypeStruct((b, 1, 2, LRU_W), _F32),
            jax.ShapeDtypeStruct((b, 1, 2, RET_H, DK, DK), _F32),
        ]
        out_specs += [
            pl.BlockSpec((g, None, 2, LRU_W), lambda i: (i, 0, 0, 0)),
            pl.BlockSpec((g, None, 2, RET_H, DK, DK), lambda i: (i, 0, 0, 0, 0, 0)),
        ]
    f32s = lambda shape: pltpu.VMEM((g,) + shape, _F32)
    slabs = (LRU_W // LANES, n, LANES)
    scratch = [
        f32s((n + 2 * SUBLANES, LRU_W)),
        f32s((n, LRU_W)),
        f32s(slabs), f32s(slabs),
        f32s(slabs), f32s(slabs),
    ]
    if latent:
        scratch += [pltpu.VMEM((D, 2 * D), _F32), pltpu.VMEM((D, D), _F32),
                    pltpu.SemaphoreType.DMA((2,))]
        w_in_args = (w_in, w_in)
    else:
        scratch += [f32s((n, RET_W))] + [f32s(slabs) for _ in range(4)] + [f32s((n, RET_W))]
        w_in_args = (w_in, w_in, w_in)
    return pl.pallas_call(
        functools.partial(_mixer_kernel, n=n, g=g, latent=latent),
        out_shape=out_shape,
        grid=(b // g,),
        in_specs=in_specs,
        out_specs=out_specs,
        scratch_shapes=scratch,
        compiler_params=pltpu.CompilerParams(
            dimension_semantics=("arbitrary",), vmem_limit_bytes=VMEM_LIMIT),
        name="mixer_latent" if latent else "mixer_context",
    )(x, mod, *norms, *w_in_args, *lru, wg, w_out, rw, *extra)


def _count(mask):
    return jnp.sum(jnp.where(mask, 1.0, 0.0), axis=-1, keepdims=True)


def _probs(l3):
    bsz, _, n = l3.shape
    m = jnp.max(l3, axis=1, keepdims=True)
    e = jnp.exp(l3 - m)
    return (e / jnp.sum(e, axis=1, keepdims=True)).reshape(bsz * N_EXP, n)


def _finish_select(p, bits, thr, cap):
    rows, n = p.shape
    gt = bits > thr
    eq = bits == thr
    need = float(cap) - _count(gt)
    idx = lax.broadcasted_iota(jnp.int32, (rows, n), 1)
    nbits = int(math.log2(n))

    def idx_body(i, j):
        cand = j | (jnp.int32(1) << (nbits - 1 - i))
        return jnp.where(_count(eq & (idx < cand)) < need, cand, j)

    jlast = lax.fori_loop(0, nbits, idx_body, jnp.zeros((rows, 1), jnp.int32))
    sel = gt | (eq & (idx <= jlast))
    before = (lax.broadcasted_iota(jnp.int32, (n, n), 0)
              < lax.broadcasted_iota(jnp.int32, (n, n), 1))
    pos = _dot(jnp.where(sel, 1.0, 0.0).astype(_BF), jnp.where(before, 1.0, 0.0).astype(_BF))
    return jnp.where(sel, pos, -1.0), jnp.where(sel, p, 0.0)


def _route_kernel(lp_ref, ls_ref, pp_ref, gp_ref, ps_ref, gs_ref, *, cap_p, cap_s):
    groups = ((_probs(lp_ref[...]), float(cap_p)), (_probs(ls_ref[...]), float(cap_s)))
    bits = [pltpu.bitcast(p, jnp.int32) for p, _ in groups]

    def val_body(i, thr):
        out = []
        for b, t, (_, capf) in zip(bits, thr, groups):
            cand = t | (jnp.int32(1) << (30 - i))
            out.append(jnp.where(_count(b >= cand) >= capf, cand, t))
        return tuple(out)

    thr = lax.fori_loop(0, 31, val_body,
                        tuple(jnp.zeros((b.shape[0], 1), jnp.int32) for b in bits))
    pos, gate = _finish_select(groups[0][0], bits[0], thr[0], cap_p)
    pp_ref[...] = pos.reshape(pp_ref.shape)
    gp_ref[...] = gate.reshape(gp_ref.shape)
    pos, gate = _finish_select(groups[1][0], bits[1], thr[1], cap_s)
    ps_ref[...] = pos.reshape(ps_ref.shape)
    gs_ref[...] = gate.reshape(gs_ref.shape)


def _route_call(lt_p, lt_s, cap_p, cap_s):
    shapes = [
        jax.ShapeDtypeStruct(lt_p.shape, _F32), jax.ShapeDtypeStruct(lt_p.shape, _F32),
        jax.ShapeDtypeStruct(lt_s.shape, _F32), jax.ShapeDtypeStruct(lt_s.shape, _F32),
    ]
    return pl.pallas_call(
        functools.partial(_route_kernel, cap_p=cap_p, cap_s=cap_s),
        out_shape=shapes,
        compiler_params=pltpu.CompilerParams(vmem_limit_bytes=VMEM_LIMIT),
        name="route_select",
    )(lt_p, lt_s)


def _dispatch_kernel(pos_ref, gate_ref, h_ref, xs_ref, gs_ref, *, n, cap, g):
    slot = lax.broadcasted_iota(jnp.int32, (cap, n), 0).astype(_F32)
    for j in range(g):
        slots = slice(j * cap, (j + 1) * cap)
        parts = []
        for e in range(N_EXP):
            hit = pos_ref[j, e:e + 1, :] == slot
            parts.append(jnp.where(hit, 1.0, 0.0).astype(_BF))
            gs_ref[e, slots, :] = jnp.sum(jnp.where(hit, gate_ref[j, e:e + 1, :], 0.0),
                                          axis=-1, keepdims=True)
        onehot = jnp.concatenate(parts, axis=0)
        xs = _dot(onehot, h_ref[j]).astype(_BF)
        for e in range(N_EXP):
            xs_ref[e, slots, :] = xs[e * cap:(e + 1) * cap, :]


def _dispatch_call(pos, gate, h2, cap):
    b, n, _ = h2.shape
    g = max(1, SMALL_STEP_TOKENS // n)
    return pl.pallas_call(
        functools.partial(_dispatch_kernel, n=n, cap=cap, g=g),
        out_shape=[
            jax.ShapeDtypeStruct((N_EXP, b * cap, D), _BF),
            jax.ShapeDtypeStruct((N_EXP, b * cap, 1), _F32),
        ],
        grid=(b // g,),
        in_specs=[
            pl.BlockSpec((g, N_EXP, n), lambda i: (i, 0, 0)),
            pl.BlockSpec((g, N_EXP, n), lambda i: (i, 0, 0)),
            pl.BlockSpec((g, n, D), lambda i: (i, 0, 0)),
        ],
        out_specs=[
            pl.BlockSpec((N_EXP, g * cap, D), lambda i: (0, i, 0)),
            pl.BlockSpec((N_EXP, g * cap, 1), lambda i: (0, i, 0)),
        ],
        compiler_params=pltpu.CompilerParams(
            dimension_semantics=("arbitrary",), vmem_limit_bytes=VMEM_LIMIT),
        name="dispatch",
    )(pos, gate, h2)


def _expert_kernel(xp_ref, xs_ref, gp_ref, gs_ref, wg_ref, wu_ref, wd_ref, y_ref, xcat, acc,
                   *, sp, nf, tf):
    f = pl.program_id(1)
    xcat[0:sp, :] = xp_ref[...]
    xcat[sp:, :] = xs_ref[...]
    x = xcat[...]
    total = jnp.where(f == 0, 0.0, acc[...])
    for c in range(tf // FF_CHUNK):
        cs = slice(c * FF_CHUNK, (c + 1) * FF_CHUNK)
        hg = _dot(x, wg_ref[:, cs].astype(_BF))
        hu = _dot(x, wu_ref[:, cs].astype(_BF))
        hid = (_silu(hg) * hu).astype(_BF)
        total = total + _dot(hid, wd_ref[cs, :].astype(_BF))
    acc[...] = total
    y_ref[0:sp, :] = (total[0:sp, :] * gp_ref[...]).astype(_BF)
    y_ref[sp:, :] = (total[sp:, :] * gs_ref[...]).astype(_BF)


def _expert_call(xs_p, xs_s, g_p, g_s, w_gate, w_up, w_down):
    tf = 1024
    sp = xs_p.shape[1]
    ss = xs_s.shape[1]
    nf = FF // tf
    return pl.pallas_call(
        functools.partial(_expert_kernel, sp=sp, nf=nf, tf=tf),
        out_shape=jax.ShapeDtypeStruct((N_EXP, sp + ss, D), _BF),
        grid=(N_EXP, nf),
        in_specs=[
            pl.BlockSpec((None, sp, D), lambda e, f: (e, 0, 0)),
            pl.BlockSpec((None, ss, D), lambda e, f: (e, 0, 0)),
            pl.BlockSpec((None, sp, 1), lambda e, f: (e, 0, 0)),
            pl.BlockSpec((None, ss, 1), lambda e, f: (e, 0, 0)),
            pl.BlockSpec((None, D, tf), lambda e, f: (e, 0, f)),
            pl.BlockSpec((None, D, tf), lambda e, f: (e, 0, f)),
            pl.BlockSpec((None, tf, D), lambda e, f: (e, f, 0)),
        ],
        out_specs=pl.BlockSpec((None, sp + ss, D), lambda e, f: (e, 0, 0)),
        scratch_shapes=[pltpu.VMEM((sp + ss, D), _BF), pltpu.VMEM((sp + ss, D), _F32)],
        compiler_params=pltpu.CompilerParams(
            dimension_semantics=("arbitrary", "arbitrary"), vmem_limit_bytes=VMEM_LIMIT),
        name="expert_ffn",
    )(xs_p, xs_s, g_p, g_s, w_gate, w_up, w_down)


def _combine_kernel(pos_ref, ye_ref, x1_ref, mod_ref, n4_ref, y_ref, *, n, cap, g, latent):
    width = N_EXP * cap
    lane = lax.broadcasted_iota(jnp.int32, (N_EXP, width), 1)
    expand = jnp.where(lane // cap == lax.broadcasted_iota(jnp.int32, (N_EXP, width), 0),
                       1.0, 0.0).astype(_BF)
    tile = min(n, COMBINE_TILE)
    slot = (lax.broadcasted_iota(jnp.int32, (tile, width), 1) % cap).astype(_F32)
    mod_row = (pl.program_id(0) + 1) if latent else 0
    gain = mod_ref[N_MOD - 1, pl.ds(mod_row, 1), :] * n4_ref[...]

    for j in range(g):
        ye = ye_ref[:, j * cap:(j + 1) * cap, :].reshape(width, D)

        def body(r, carry, j=j, ye=ye):
            rows = pl.ds(pl.multiple_of(r * tile, tile), tile)
            pos_e = _dot_tn(pos_ref[j, :, rows].astype(_BF), expand)
            onehot = jnp.where(pos_e == slot, 1.0, 0.0).astype(_BF)
            f = _dot(onehot, ye)
            y_ref[j, rows, :] = x1_ref[j, rows, :] + _rms(f, gain)
            return carry

        _tile_loop(n // tile, body)


def _combine_call(pos, ye, x1, mod, norm_post, cap, slot_off, latent):
    b, n, _ = x1.shape
    g = max(1, SMALL_STEP_TOKENS // n)
    assert g == 1 or not latent
    blk_off = slot_off // (g * cap)
    return pl.pallas_call(
        functools.partial(_combine_kernel, n=n, cap=cap, g=g, latent=latent),
        out_shape=jax.ShapeDtypeStruct((b, n, D), _F32),
        grid=(b // g,),
        in_specs=[
            pl.BlockSpec((g, N_EXP, n), lambda i: (i, 0, 0)),
            pl.BlockSpec((N_EXP, g * cap, D), lambda i: (0, i + blk_off, 0)),
            pl.BlockSpec((g, n, D), lambda i: (i, 0, 0)),
            pl.BlockSpec((N_MOD, SUBLANES, D), lambda i: (0, 0, 0)),
            pl.BlockSpec((1, D), lambda i: (0, 0)),
        ],
        out_specs=pl.BlockSpec((g, n, D), lambda i: (i, 0, 0)),
        compiler_params=pltpu.CompilerParams(
            dimension_semantics=("arbitrary",), vmem_limit_bytes=VMEM_LIMIT),
        name="combine_latent" if latent else "combine_context",
    )(pos, ye, x1, mod, norm_post)


def _block_diag_gates(wa, wi):
    per_half = LRU_HEADS // 2
    side = per_half * LRU_HD
    on_diag = (np.arange(side)[:, None] // LRU_HD) == (np.arange(side)[None, :] // LRU_HD)

    def bd(w):
        rows = w.reshape(2, 2, side, LRU_HD)
        return jnp.where(on_diag, jnp.tile(rows, (1, 1, 1, per_half)), 0.0)

    return 0.5 * jnp.concatenate([bd(wa), bd(wi)], axis=-1)


def _rope_tables(n):
    rows = n // GRID_W
    row = np.repeat(np.arange(rows, dtype=np.float32), GRID_W)
    col = np.tile(np.arange(GRID_W, dtype=np.float32), rows)
    nf = DK // 4
    freqs = np.float32(ROPE_BASE) ** (-np.arange(nf, dtype=np.float32) / np.float32(nf))
    ang = np.concatenate([row[:, None] * freqs, col[:, None] * freqs], axis=-1).astype(np.float32)
    cos = np.cos(ang)
    sin = np.sin(ang)
    return (jnp.asarray(np.concatenate([cos, cos], axis=-1), _F32),
            jnp.asarray(np.concatenate([-sin, sin], axis=-1), _F32))


def _decay_consts(n):
    heads = np.arange(RET_H, dtype=np.float32)
    f32 = np.float32
    lgf = np.log1p(-np.exp2(-(f32(RET_DECAY_OFFSET_FWD) + heads))).astype(f32)
    lgb = np.log1p(-np.exp2(-(f32(RET_DECAY_OFFSET_BWD) + heads))).astype(f32)
    tab = np.stack([lgf, lgb, np.exp(lgf), np.exp(f32(n) * lgb), np.exp(f32(n - 1) * lgf)], axis=1)
    return [[float(v) for v in row] for row in tab.astype(f32)]


def kernel(x_prompt, x_sample, c, state_lru, state_ret, c_ctx, ada_w, ada_b, norm_mix_pre, norm_mix_post, norm_ffn_pre, norm_ffn_post, w_in, conv_w, conv_b, lru_wa, lru_ba, lru_wi, lru_bi, lru_lambda, w_out, router_w, exp_w_gate, exp_w_up, exp_w_down):
    bp, n_p, _ = x_prompt.shape
    bs, n_s, _ = x_sample.shape
    cap_p = 2 * n_p // N_EXP
    cap_s = 2 * n_s // N_EXP
    l = 0

    mod = _ada_call(c_ctx[None, :], c, ada_w[l], ada_b[l][None, :])

    norms = (norm_mix_pre[l][None], norm_mix_post[l][None], norm_ffn_pre[l][None])
    lru = (conv_w[l], conv_b[l][None], lru_ba[l], lru_bi[l], lru_lambda[l])
    wg = _block_diag_gates(lru_wa[l], lru_wi[l])
    rw = jnp.pad(router_w[l], ((0, 0), (0, LANES - N_EXP)))
    cos2, sin2 = _rope_tables(n_s)

    x1_p, h2_p, lt_p, st_lru, st_ret = _mixer_call(
        x_prompt, mod, norms, w_in[l], lru, wg, w_out[l], rw, latent=False)
    x1_s, h2_s, lt_s = _mixer_call(
        x_sample, mod, norms, w_in[l], lru, wg, w_out[l], rw, latent=True,
        extra=(state_lru, state_ret, cos2, sin2), layer=l)

    pos_p, gate_p, pos_s, gate_s = _route_call(lt_p, lt_s, cap_p, cap_s)
    xs_p, gsl_p = _dispatch_call(pos_p, gate_p, h2_p, cap_p)
    xs_s, gsl_s = _dispatch_call(pos_s, gate_s, h2_s, cap_s)
    ye = _expert_call(xs_p, xs_s, gsl_p, gsl_s, exp_w_gate[l], exp_w_up[l], exp_w_down[l])

    norm_post = norm_ffn_post[l][None]
    y_p = _combine_call(pos_p, ye, x1_p, mod, norm_post, cap_p, 0, latent=False)
    y_s = _combine_call(pos_s, ye, x1_s, mod, norm_post, cap_s, bp * cap_p, latent=True)
    return (y_p, y_s, st_lru, st_ret)
```

```python
import functools
import math
import types

import jax
import jax.numpy as jnp
import numpy as np
from jax import lax
from jax.experimental import pallas as pl
from jax.experimental.pallas import tpu as pltpu

D = 1024
LRU_W = 512
LRU_HEADS = 8
LRU_HD = 64
LRU_C = 8.0
RET_W = 512
RET_H = 4
DK = 128
N_EXP = 16
FF = 2048
N_MOD = 6
EPS = 1e-6
GRID_W = 64
ROPE_BASE = 10000.0
RET_DECAY_OFFSET_FWD = 5.0
RET_DECAY_OFFSET_BWD = 5.5

ROW_TILE = 256
WIDE_TILE = 512
FF_CHUNK = 512
COMBINE_TILE = 1024
CONTEXT_GROUP = 2
SMALL_STEP_TOKENS = 1024
SUBLANES = 8
LANES = 128
VMEM_LIMIT = 60 * 1024 * 1024

DEC_LOG_F, DEC_LOG_B, DEC_G_F, DEC_GN_B, DEC_GN1_F = 0, 1, 2, 3, 4

_BF = jnp.bfloat16
_F32 = jnp.float32


def _sigmoid(x):
    return 0.5 * jnp.tanh(0.5 * x) + 0.5


def _silu(x):
    return x * _sigmoid(x)


def _gelu_tanh(x):
    c = math.sqrt(2.0 / math.pi)
    return 0.5 * x * (1.0 + jnp.tanh(c * (x + 0.044715 * (x * x * x))))


def _rms(x, gain):
    return x * lax.rsqrt(jnp.mean(x * x, axis=-1, keepdims=True) + EPS) * gain


def _dot(a, b):
    return jnp.dot(a, b, preferred_element_type=_F32)


def _dot_nt(a, b):
    return lax.dot_general(a, b, (((1,), (1,)), ((), ())), preferred_element_type=_F32)


def _dot_tn(a, b):
    return lax.dot_general(a, b, (((0,), (0,)), ((), ())), preferred_element_type=_F32)


def _ada_kernel(cc_ref, c_ref, w_ref, b_ref, o_ref, s_ref):
    nb = c_ref.shape[0]
    s_ref[...] = jnp.zeros_like(s_ref)
    s_ref[0:1, :] = _silu(cc_ref[...])
    s_ref[1:1 + nb, :] = _silu(c_ref[...])
    o_ref[...] = _dot(s_ref[...], w_ref[...]) + b_ref[...]


def _ada_call(c_ctx, c, ada_w, ada_b):
    nb = c.shape[0]
    assert nb + 1 <= SUBLANES
    return pl.pallas_call(
        _ada_kernel,
        out_shape=jax.ShapeDtypeStruct((N_MOD, SUBLANES, D), _F32),
        grid=(N_MOD,),
        in_specs=[
            pl.BlockSpec((1, D), lambda j: (0, 0)),
            pl.BlockSpec((nb, D), lambda j: (0, 0)),
            pl.BlockSpec((D, D), lambda j: (0, j)),
            pl.BlockSpec((1, D), lambda j: (0, j)),
        ],
        out_specs=pl.BlockSpec((None, SUBLANES, D), lambda j: (j, 0, 0)),
        scratch_shapes=[pltpu.VMEM((SUBLANES, D), _F32)],
        compiler_params=pltpu.CompilerParams(
            dimension_semantics=("arbitrary",), vmem_limit_bytes=VMEM_LIMIT),
        name="ada_mod",
    )(c_ctx, c, ada_w, ada_b)


def _tile_loop(nt, body, unroll=1):
    if nt == 1:
        body(0, 0)
    else:
        lax.fori_loop(0, nt, body, 0, unroll=unroll)


def _mixer_kernel(*refs, n, g, latent):
    if latent:
        (x_ref, mod_ref, n1_ref, n2_ref, n3_ref, wl_ref, win_hbm,
         cw_ref, cb_ref, ba_ref, bi_ref, lam_ref, wg_ref, wout_hbm, rw_ref,
         h0_ref, s0_ref, cos_ref, sin_ref,
         x1_ref, h2_ref, lt_ref,
         xlp_g, gy_g, af_g, ab_g, hf_g, hb_g, wret_v, wout_ref, late_sem) = refs
        sg_g, qf_g, qb_g, kf_g, kb_g, v_g = xlp_g, af_g, ab_g, hf_g, hb_g, gy_g
        wqk_ref, wvg_ref = wret_v.at[:, 0:D], wret_v.at[:, D:2 * D]
        first_step = pl.program_id(0) == 0
        late_ret = pltpu.make_async_copy(win_hbm.at[:, pl.ds(D, 2 * D)], wret_v, late_sem.at[0])
        late_out = pltpu.make_async_copy(wout_hbm, wout_ref, late_sem.at[1])

        @pl.when(first_step)
        def _():
            late_ret.start()
            late_out.start()
    else:
        (x_ref, mod_ref, n1_ref, n2_ref, n3_ref, wl_ref, wqk_ref, wvg_ref,
         cw_ref, cb_ref, ba_ref, bi_ref, lam_ref, wg_ref, wout_ref, rw_ref,
         x1_ref, h2_ref, lt_ref, stl_ref, str_ref,
         xlp_g, gy_g, af_g, ab_g, hf_g, hb_g, sg_g, qf_g, qb_g, kf_g, kb_g, v_g) = refs
    early_ret = not latent

    seqs = []
    for s in range(g):
        q = types.SimpleNamespace(
            x=x_ref.at[s], x1=x1_ref.at[s], h2=h2_ref.at[s], lt=lt_ref.at[s], mix=x1_ref.at[s],
            xlp=xlp_g.at[s], sg=sg_g.at[s], gy=gy_g.at[s], v=v_g.at[s],
            a_f=af_g.at[s], a_b=ab_g.at[s], h_f=hf_g.at[s], h_b=hb_g.at[s],
            q_f=qf_g.at[s], q_b=qb_g.at[s], k_f=kf_g.at[s], k_b=kb_g.at[s])
        if latent:
            q.h0, q.s0 = h0_ref.at[s], s0_ref.at[s]
        else:
            q.stl, q.str = stl_ref.at[s], str_ref.at[s]
        seqs.append(q)

    nt = n // ROW_TILE
    decay = _decay_consts(n)
    mod_row = (pl.program_id(0) + 1) if latent else 0
    mod = lambda k: mod_ref[k, pl.ds(mod_row, 1), :]
    shift = mod(0)
    scale = n1_ref[...] * (1.0 + mod(1))

    def normed(q, rows):
        return _rms(q.x[rows, :], scale) + shift

    wide = min(n, WIDE_TILE)
    a_tile = wide if latent else ROW_TILE

    def stacked(parts):
        return parts[0] if len(parts) == 1 else jnp.concatenate(parts, axis=0)

    def store_retention(q, r0, rows, pqk, pvg):
        q.sg[rows, :] = _silu(pvg[:, RET_W:])
        tpos = (r0 + lax.broadcasted_iota(jnp.int32, (pqk.shape[0], DK), 0)).astype(_F32)
        if latent:
            cos2 = cos_ref[rows, :]
            sin2 = sin_ref[rows, :]
        for hd in range(RET_H):
            cols = slice(hd * DK, (hd + 1) * DK)
            qh = pqk[:, cols] * (DK ** -0.5)
            kh = pqk[:, RET_W + hd * DK:RET_W + (hd + 1) * DK]
            if latent:
                qh = qh * cos2 + pltpu.roll(qh, DK // 2, axis=1) * sin2
                kh = kh * cos2 + pltpu.roll(kh, DK // 2, axis=1) * sin2
            lgf = decay[hd][DEC_LOG_F]
            lgb = decay[hd][DEC_LOG_B]
            q.q_f[hd, rows, :] = qh * jnp.exp(tpos * lgf)
            q.k_f[hd, rows, :] = kh * jnp.exp(tpos * (-lgf))
            q.q_b[hd, rows, :] = qh * jnp.exp(tpos * (-lgb))
            q.k_b[hd, rows, :] = kh * jnp.exp(tpos * lgb)
        q.v[rows, :] = pvg[:, 0:RET_W]

    def phase_a1(r, carry):
        r0 = pl.multiple_of(r * a_tile, a_tile)
        rows = pl.ds(r0, a_tile)
        h = stacked([normed(q, rows) for q in seqs])
        p_all = _dot(h, wl_ref[...])
        if early_ret:
            pqk_all = _dot(h, wqk_ref[...])
            pvg_all = _dot(h, wvg_ref[...])
        for s, q in enumerate(seqs):
            part = slice(s * a_tile, (s + 1) * a_tile)
            q.xlp[pl.ds(r0 + SUBLANES, a_tile), :] = p_all[part, 0:LRU_W]
            q.gy[rows, :] = _gelu_tanh(p_all[part, LRU_W:])
            if early_ret:
                store_retention(q, r0, rows, pqk_all[part], pvg_all[part])
        return carry

    for q in seqs:
        q.xlp[0:SUBLANES, :] = jnp.zeros((SUBLANES, LRU_W), _F32)
        q.xlp[n + SUBLANES:n + 2 * SUBLANES, :] = jnp.zeros((SUBLANES, LRU_W), _F32)

    half = LRU_W // 2

    def softplus_neg(lam):
        z = -lam
        return jnp.maximum(z, 0.0) + jnp.log1p(jnp.exp(-jnp.abs(z)))

    sp = (softplus_neg(lam_ref[0:1, :]), softplus_neg(lam_ref[1:2, :]))

    def phase_b(q):
        def body(r, carry):
            r0 = pl.multiple_of(r * a_tile, a_tile)
            rows = pl.ds(r0, a_tile)
            ext = q.xlp[pl.ds(r0, a_tile + 2 * SUBLANES), :]
            xc = cb_ref[...]
            for tap in range(4):
                o = SUBLANES - 2 + tap
                xc = xc + ext[o:o + a_tile, :] * cw_ref[tap:tap + 1, :]
            xh = 0.5 * xc
            for d, (a_ref, u_ref) in enumerate(((q.a_f, q.h_f), (q.a_b, q.h_b))):
                bah = 0.5 * ba_ref[d:d + 1, :]
                bih = 0.5 * bi_ref[d:d + 1, :]
                ch = (-0.5 * LRU_C) * sp[d]
                for hh in range(2):
                    cs = slice(hh * half, (hh + 1) * half)
                    pre = _dot(xc[:, cs], wg_ref[d, hh])
                    t_r = jnp.tanh(pre[:, 0:half] + bah[:, cs])
                    t_i = jnp.tanh(pre[:, half:] + bih[:, cs])
                    log_a = t_r * ch[:, cs] + ch[:, cs]
                    a = jnp.exp(log_a)
                    om = -jnp.tanh(log_a) * (a * a + 1.0)
                    root = jnp.where(om > 0.0, om * lax.rsqrt(om), 0.0)
                    u = root * (t_i * xh[:, cs] + xh[:, cs])
                    for j in range(half // LANES):
                        lanes = slice(j * LANES, (j + 1) * LANES)
                        a_ref[hh * (half // LANES) + j, rows, :] = a[:, lanes]
                        u_ref[hh * (half // LANES) + j, rows, :] = u[:, lanes]
            return carry
        return body

    _tile_loop(n // a_tile, phase_a1)
    for q in seqs:
        _tile_loop(n // a_tile, phase_b(q))

    row8 = lax.broadcasted_iota(jnp.int32, (SUBLANES, LANES), 0)
    block = SUBLANES * SUBLANES
    n_blocks = n // block
    n_slabs = LRU_W // LANES

    def across_groups(a, b, reverse):
        for s in (1, 2, 4):
            m = (row8 < SUBLANES - s) if reverse else (row8 >= s)
            shift = SUBLANES - s if reverse else s
            a_s = jnp.where(m, pltpu.roll(a, shift, axis=0), 1.0)
            b_s = jnp.where(m, pltpu.roll(b, shift, axis=0), 0.0)
            b = a * b_s + b
            a = a * a_s
        return a, b

    def scan_block(a_ref, h_ref, base, carry, reverse):
        rows = [pl.ds(base + k, SUBLANES, stride=SUBLANES) for k in range(SUBLANES)]
        order = list(reversed(range(SUBLANES))) if reverse else list(range(SUBLANES))
        prod, local = {}, {}
        prev = None
        for k in order:
            a, u = a_ref[rows[k], :], h_ref[rows[k], :]
            prod[k] = a if prev is None else a * prod[prev]
            local[k] = u if prev is None else a * local[prev] + u
            prev = k
        p_all, h_all = across_groups(prod[prev], local[prev], reverse)
        inner = (row8 < SUBLANES - 1) if reverse else (row8 >= 1)
        shift = SUBLANES - 1 if reverse else 1
        enter = (jnp.where(inner, pltpu.roll(p_all, shift, axis=0), 1.0) * carry
                 + jnp.where(inner, pltpu.roll(h_all, shift, axis=0), 0.0))
        for k in order:
            h_ref[rows[k], :] = prod[k] * enter + local[k]
        leave = p_all * carry + h_all
        return leave[0:1, :] if reverse else leave[SUBLANES - 1:SUBLANES, :]

    def initial(q, d, s):
        if latent:
            return q.h0[d:d + 1, s * LANES:(s + 1) * LANES]
        return jnp.zeros((1, LANES), _F32)

    chains = [(q, s) for q in seqs for s in range(n_slabs)]

    def scan_body(i, carry):
        fwd_base = pl.multiple_of(i * block, block)
        bwd_base = pl.multiple_of((n_blocks - 1 - i) * block, block)
        out = []
        for (q, s), (cf, cb) in zip(chains, carry):
            out.append((scan_block(q.a_f.at[s], q.h_f.at[s], fwd_base, cf, False),
                        scan_block(q.a_b.at[s], q.h_b.at[s], bwd_base, cb, True)))
        return tuple(out)

    ends = lax.fori_loop(0, n_blocks, scan_body,
                         tuple((initial(q, 0, s), initial(q, 1, s)) for q, s in chains), unroll=2)

    def lru_out(q):
        def body(r, carry):
            rows = pl.ds(pl.multiple_of(r * ROW_TILE, ROW_TILE), ROW_TILE)
            for s in range(n_slabs):
                lanes = slice(s * LANES, (s + 1) * LANES)
                q.mix[rows, lanes] = (q.h_f[s, rows, :] + q.h_b[s, rows, :]) * q.gy[rows, lanes]
            return carry
        return body

    for (q, s), (last_f, first_b) in zip(chains, ends):
        if not latent:
            q.stl[0:1, s * LANES:(s + 1) * LANES] = last_f
            q.stl[1:2, s * LANES:(s + 1) * LANES] = first_b
    for q in seqs:
        _tile_loop(nt, lru_out(q))

    def phase_a2(r, carry):
        r0 = pl.multiple_of(r * wide, wide)
        rows = pl.ds(r0, wide)
        h = stacked([normed(q, rows) for q in seqs])
        pqk_all = _dot(h, wqk_ref[...])
        pvg_all = _dot(h, wvg_ref[...])
        for s, q in enumerate(seqs):
            part = slice(s * wide, (s + 1) * wide)
            store_retention(q, r0, rows, pqk_all[part], pvg_all[part])
        return carry

    if not early_ret:
        @pl.when(first_step)
        def _():
            late_ret.wait()

        _tile_loop(n // wide, phase_a2)

    lower = (lax.broadcasted_iota(jnp.int32, (ROW_TILE, ROW_TILE), 0)
             >= lax.broadcasted_iota(jnp.int32, (ROW_TILE, ROW_TILE), 1))
    blocks = [slice(r * ROW_TILE, (r + 1) * ROW_TILE) for r in range(nt)]

    def phase_c(q, hd):
        cols = slice(hd * DK, (hd + 1) * DK)
        dec = lambda row: decay[hd][row]
        kv_f = [_dot_tn(q.k_f[hd, rows, :], q.v[rows, cols]) if (r < nt - 1 or not latent) else 0.0
                for r, rows in enumerate(blocks)]
        kv_b = [_dot_tn(q.k_b[hd, rows, :], q.v[rows, cols]) if (r > 0 or not latent) else 0.0
                for r, rows in enumerate(blocks)]
        if latent:
            run_f = q.s0[0, hd] * dec(DEC_G_F)
            run_b = q.s0[1, hd] * dec(DEC_GN_B)
        else:
            run_f = run_b = None
        before = []
        for r in range(nt):
            before.append(run_f)
            run_f = kv_f[r] if run_f is None else run_f + kv_f[r]
        after = [None] * nt
        for r in reversed(range(nt)):
            after[r] = run_b
            run_b = kv_b[r] if run_b is None else run_b + kv_b[r]

        for r, rows in enumerate(blocks):
            qf = q.q_f[hd, rows, :]
            qb = q.q_b[hd, rows, :]
            s = jnp.where(lower, _dot_nt(qf, q.k_f[hd, rows, :]), _dot_nt(qb, q.k_b[hd, rows, :]))
            o = _dot(s, q.v[rows, cols])
            if before[r] is not None:
                o = o + _dot(qf, before[r])
            if after[r] is not None:
                o = o + _dot(qb, after[r])
            o = o * lax.rsqrt(jnp.mean(o * o, axis=-1, keepdims=True) + EPS)
            q.mix[rows, LRU_W + cols.start:LRU_W + cols.stop] = o * q.sg[rows, cols]
        if not latent:
            q.str[0, hd] = run_f * dec(DEC_GN1_F)
            q.str[1, hd] = run_b

    for hd in range(RET_H):
        for q in seqs:
            phase_c(q, hd)

    gain1 = mod(2) * n2_ref[...]
    gain2 = n3_ref[...] * (1.0 + mod(4))
    sh2 = mod(3)

    def phase_d(r, carry):
        rows = pl.ds(pl.multiple_of(r * wide, wide), wide)
        mix_all = _dot(stacked([q.mix[rows, :] for q in seqs]), wout_ref[...])
        h2s = []
        for s, q in enumerate(seqs):
            x1 = q.x[rows, :] + _rms(mix_all[s * wide:(s + 1) * wide], gain1)
            q.x1[rows, :] = x1
            h2s.append(_rms(x1, gain2) + sh2)
            q.h2[rows, :] = h2s[-1].astype(_BF)
        logits_t = _dot(stacked(h2s), rw_ref[...]).T
        for s, q in enumerate(seqs):
            q.lt[:, rows] = logits_t[0:N_EXP, s * wide:(s + 1) * wide]
        return carry

    if latent:
        @pl.when(first_step)
        def _():
            late_out.wait()

    _tile_loop(n // wide, phase_d, unroll=2)


def _mixer_call(x, mod, norms, w_in, lru, wg, w_out, rw, latent, extra=(), layer=0):
    b, n, _ = x.shape
    g = 1 if latent else CONTEXT_GROUP
    const2 = lambda i: (0, 0)
    whole = pl.BlockSpec(memory_space=pl.ANY)
    in_specs = [
        pl.BlockSpec((g, n, D), lambda i: (i, 0, 0)),
        pl.BlockSpec((N_MOD, SUBLANES, D), lambda i: (0, 0, 0)),
        pl.BlockSpec((1, D), const2), pl.BlockSpec((1, D), const2), pl.BlockSpec((1, D), const2),
        pl.BlockSpec((D, D), lambda i: (0, 0)),
    ]
    if latent:
        in_specs += [whole]
    else:
        in_specs += [pl.BlockSpec((D, D), lambda i: (0, 1)),
                     pl.BlockSpec((D, D), lambda i: (0, 2))]
    in_specs += [
        pl.BlockSpec((4, LRU_W), const2), pl.BlockSpec((1, LRU_W), const2),
        pl.BlockSpec((2, LRU_W), const2), pl.BlockSpec((2, LRU_W), const2),
        pl.BlockSpec((2, LRU_W), const2),
        pl.BlockSpec((2, 2, LRU_W // 2, LRU_W), lambda i: (0, 0, 0, 0)),
        whole if latent else pl.BlockSpec((D, D), const2),
        pl.BlockSpec((D, LANES), const2),
    ]
    out_shape = [
        jax.ShapeDtypeStruct((b, n, D), _F32),
        jax.ShapeDtypeStruct((b, n, D), _BF),
        jax.ShapeDtypeStruct((b, N_EXP, n), _F32),
    ]
    out_specs = [
        pl.BlockSpec((g, n, D), lambda i: (i, 0, 0)),
        pl.BlockSpec((g, n, D), lambda i: (i, 0, 0)),
        pl.BlockSpec((g, N_EXP, n), lambda i: (i, 0, 0)),
    ]
    if latent:
        in_specs += [
            pl.BlockSpec((g, None, 2, LRU_W), lambda i: (i, layer, 0, 0)),
            pl.BlockSpec((g, None, 2, RET_H, DK, DK), lambda i: (i, layer, 0, 0, 0, 0)),
            pl.BlockSpec((n, DK), const2),
            pl.BlockSpec((n, DK), const2),
        ]
    else:
        out_shape += [
            jax.ShapeDtypeStruct((b, 1, 2, LRU_W), _F32),
            jax.ShapeDtypeStruct((b, 1, 2, RET_H, DK, DK), _F32),
        ]
        out_specs += [
            pl.BlockSpec((g, None, 2, LRU_W), lambda i: (i, 0, 0, 0)),
            pl.BlockSpec((g, None, 2, RET_H, DK, DK), lambda i: (i, 0, 0, 0, 0, 0)),
        ]
    f32s = lambda shape: pltpu.VMEM((g,) + shape, _F32)
    slabs = (LRU_W // LANES, n, LANES)
    scratch = [
        f32s((n + 2 * SUBLANES, LRU_W)),
        f32s((n, LRU_W)),
        f32s(slabs), f32s(slabs),
        f32s(slabs), f32s(slabs),
    ]
    if latent:
        scratch += [pltpu.VMEM((D, 2 * D), _F32), pltpu.VMEM((D, D), _F32),
                    pltpu.SemaphoreType.DMA((2,))]
        w_in_args = (w_in, w_in)
    else:
        scratch += [f32s((n, RET_W))] + [f32s(slabs) for _ in range(4)] + [f32s((n, RET_W))]
        w_in_args = (w_in, w_in, w_in)
    return pl.pallas_call(
        functools.partial(_mixer_kernel, n=n, g=g, latent=latent),
        out_shape=out_shape,
        grid=(b // g,),
        in_specs=in_specs,
        out_specs=out_specs,
        scratch_shapes=scratch,
        compiler_params=pltpu.CompilerParams(
            dimension_semantics=("arbitrary",), vmem_limit_bytes=VMEM_LIMIT),
        name="mixer_latent" if latent else "mixer_context",
    )(x, mod, *norms, *w_in_args, *lru, wg, w_out, rw, *extra)


def _count(mask):
    return jnp.sum(jnp.where(mask, 1.0, 0.0), axis=-1, keepdims=True)


def _probs(l3):
    bsz, _, n = l3.shape
    m = jnp.max(l3, axis=1, keepdims=True)
    e = jnp.exp(l3 - m)
    return (e / jnp.sum(e, axis=1, keepdims=True)).reshape(bsz * N_EXP, n)


def _break_ties(parts):
    prep = []
    for bits, thr, cap in parts:
        rows, n = bits.shape
        eq = bits == thr
        need = float(cap) - _count(bits > thr)
        idx = lax.broadcasted_iota(jnp.int32, (rows, n), 1)
        prep.append((eq, need, idx, int(math.log2(n))))
    most = max(nbits for _, _, _, nbits in prep)

    def idx_body(i, last):
        out = []
        for (eq, need, idx, nbits), j in zip(prep, last):
            shift = nbits - 1 - i
            cand = j | jnp.where(shift >= 0, jnp.int32(1) << jnp.maximum(shift, 0), 0)
            out.append(jnp.where(_count(eq & (idx < cand)) < need, cand, j))
        return tuple(out)

    return lax.fori_loop(0, most, idx_body,
                         tuple(jnp.zeros((eq.shape[0], 1), jnp.int32) for eq, _, _, _ in prep))


def _slots(p, bits, thr, jlast):
    rows, n = p.shape
    idx = lax.broadcasted_iota(jnp.int32, (rows, n), 1)
    sel = (bits > thr) | ((bits == thr) & (idx <= jlast))
    before = (lax.broadcasted_iota(jnp.int32, (n, n), 0)
              < lax.broadcasted_iota(jnp.int32, (n, n), 1))
    pos = _dot(jnp.where(sel, 1.0, 0.0).astype(_BF), jnp.where(before, 1.0, 0.0).astype(_BF))
    return jnp.where(sel, pos, -1.0), jnp.where(sel, p, 0.0)


def _route_kernel(lp_ref, ls_ref, pp_ref, gp_ref, ps_ref, gs_ref, *, cap_p, cap_s):
    groups = ((_probs(lp_ref[...]), float(cap_p)), (_probs(ls_ref[...]), float(cap_s)))
    bits = [pltpu.bitcast(p, jnp.int32) for p, _ in groups]

    def settle(b, t, capf, hi, lo):
        with_hi = t | hi
        both = with_hi | lo
        with_lo = t | lo
        ok = lambda cand: _count(b >= cand) >= capf
        return jnp.where(ok(with_hi), jnp.where(ok(both), both, with_hi),
                         jnp.where(ok(with_lo), with_lo, t))

    def val_body(i, thr):
        hi = jnp.int32(1) << (29 - 2 * i)
        lo = jnp.int32(1) << (28 - 2 * i)
        return tuple(settle(b, t, capf, hi, lo) for b, t, (_, capf) in zip(bits, thr, groups))

    thr = lax.fori_loop(0, 15, val_body,
                        tuple(jnp.zeros((b.shape[0], 1), jnp.int32) for b in bits))
    jlast = _break_ties([(bits[0], thr[0], cap_p), (bits[1], thr[1], cap_s)])
    pos, gate = _slots(groups[0][0], bits[0], thr[0], jlast[0])
    pp_ref[...] = pos.reshape(pp_ref.shape)
    gp_ref[...] = gate.reshape(gp_ref.shape)
    pos, gate = _slots(groups[1][0], bits[1], thr[1], jlast[1])
    ps_ref[...] = pos.reshape(ps_ref.shape)
    gs_ref[...] = gate.reshape(gs_ref.shape)


def _route_call(lt_p, lt_s, cap_p, cap_s):
    shapes = [
        jax.ShapeDtypeStruct(lt_p.shape, _F32), jax.ShapeDtypeStruct(lt_p.shape, _F32),
        jax.ShapeDtypeStruct(lt_s.shape, _F32), jax.ShapeDtypeStruct(lt_s.shape, _F32),
    ]
    return pl.pallas_call(
        functools.partial(_route_kernel, cap_p=cap_p, cap_s=cap_s),
        out_shape=shapes,
        compiler_params=pltpu.CompilerParams(vmem_limit_bytes=VMEM_LIMIT),
        name="route_select",
    )(lt_p, lt_s)


def _dispatch_kernel(pos_ref, gate_ref, h_ref, xs_ref, gs_ref, *, n, cap, g):
    slot = lax.broadcasted_iota(jnp.int32, (cap, n), 0).astype(_F32)
    for j in range(g):
        slots = slice(j * cap, (j + 1) * cap)
        parts = []
        for e in range(N_EXP):
            hit = pos_ref[j, e:e + 1, :] == slot
            parts.append(jnp.where(hit, 1.0, 0.0).astype(_BF))
            gs_ref[e, slots, :] = jnp.sum(jnp.where(hit, gate_ref[j, e:e + 1, :], 0.0),
                                          axis=-1, keepdims=True)
        onehot = jnp.concatenate(parts, axis=0)
        xs = _dot(onehot, h_ref[j]).astype(_BF)
        for e in range(N_EXP):
            xs_ref[e, slots, :] = xs[e * cap:(e + 1) * cap, :]


def _dispatch_call(pos, gate, h2, cap):
    b, n, _ = h2.shape
    g = max(1, SMALL_STEP_TOKENS // n)
    return pl.pallas_call(
        functools.partial(_dispatch_kernel, n=n, cap=cap, g=g),
        out_shape=[
            jax.ShapeDtypeStruct((N_EXP, b * cap, D), _BF),
            jax.ShapeDtypeStruct((N_EXP, b * cap, 1), _F32),
        ],
        grid=(b // g,),
        in_specs=[
            pl.BlockSpec((g, N_EXP, n), lambda i: (i, 0, 0)),
            pl.BlockSpec((g, N_EXP, n), lambda i: (i, 0, 0)),
            pl.BlockSpec((g, n, D), lambda i: (i, 0, 0)),
        ],
        out_specs=[
            pl.BlockSpec((N_EXP, g * cap, D), lambda i: (0, i, 0)),
            pl.BlockSpec((N_EXP, g * cap, 1), lambda i: (0, i, 0)),
        ],
        compiler_params=pltpu.CompilerParams(
            dimension_semantics=("arbitrary",), vmem_limit_bytes=VMEM_LIMIT),
        name="dispatch",
    )(pos, gate, h2)


def _expert_kernel(xp_ref, xs_ref, gp_ref, gs_ref, wg_ref, wu_ref, wd_ref, y_ref, xcat, acc,
                   *, sp, nf, tf):
    f = pl.program_id(1)
    xcat[0:sp, :] = xp_ref[...]
    xcat[sp:, :] = xs_ref[...]
    x = xcat[...]
    total = jnp.where(f == 0, 0.0, acc[...])
    for c in range(tf // FF_CHUNK):
        cs = slice(c * FF_CHUNK, (c + 1) * FF_CHUNK)
        hg = _dot(x, wg_ref[:, cs].astype(_BF))
        hu = _dot(x, wu_ref[:, cs].astype(_BF))
        hid = (_silu(hg) * hu).astype(_BF)
        total = total + _dot(hid, wd_ref[cs, :].astype(_BF))
    acc[...] = total
    y_ref[0:sp, :] = (total[0:sp, :] * gp_ref[...]).astype(_BF)
    y_ref[sp:, :] = (total[sp:, :] * gs_ref[...]).astype(_BF)


def _expert_call(xs_p, xs_s, g_p, g_s, w_gate, w_up, w_down):
    tf = 1024
    sp = xs_p.shape[1]
    ss = xs_s.shape[1]
    nf = FF // tf
    return pl.pallas_call(
        functools.partial(_expert_kernel, sp=sp, nf=nf, tf=tf),
        out_shape=jax.ShapeDtypeStruct((N_EXP, sp + ss, D), _BF),
        grid=(N_EXP, nf),
        in_specs=[
            pl.BlockSpec((None, sp, D), lambda e, f: (e, 0, 0)),
            pl.BlockSpec((None, ss, D), lambda e, f: (e, 0, 0)),
            pl.BlockSpec((None, sp, 1), lambda e, f: (e, 0, 0)),
            pl.BlockSpec((None, ss, 1), lambda e, f: (e, 0, 0)),
            pl.BlockSpec((None, D, tf), lambda e, f: (e, 0, f)),
            pl.BlockSpec((None, D, tf), lambda e, f: (e, 0, f)),
            pl.BlockSpec((None, tf, D), lambda e, f: (e, f, 0)),
        ],
        out_specs=pl.BlockSpec((None, sp + ss, D), lambda e, f: (e, 0, 0)),
        scratch_shapes=[pltpu.VMEM((sp + ss, D), _BF), pltpu.VMEM((sp + ss, D), _F32)],
        compiler_params=pltpu.CompilerParams(
            dimension_semantics=("arbitrary", "arbitrary"), vmem_limit_bytes=VMEM_LIMIT),
        name="expert_ffn",
    )(xs_p, xs_s, g_p, g_s, w_gate, w_up, w_down)


def _combine_kernel(pos_ref, ye_ref, x1_ref, mod_ref, n4_ref, y_ref, *, n, cap, g, latent):
    width = N_EXP * cap
    lane = lax.broadcasted_iota(jnp.int32, (N_EXP, width), 1)
    expand = jnp.where(lane // cap == lax.broadcasted_iota(jnp.int32, (N_EXP, width), 0),
                       1.0, 0.0).astype(_BF)
    tile = min(n, COMBINE_TILE)
    slot = (lax.broadcasted_iota(jnp.int32, (tile, width), 1) % cap).astype(_F32)
    mod_row = (pl.program_id(0) + 1) if latent else 0
    gain = mod_ref[N_MOD - 1, pl.ds(mod_row, 1), :] * n4_ref[...]

    for j in range(g):
        ye = ye_ref[:, j * cap:(j + 1) * cap, :].reshape(width, D)

        def body(r, carry, j=j, ye=ye):
            rows = pl.ds(pl.multiple_of(r * tile, tile), tile)
            pos_e = _dot_tn(pos_ref[j, :, rows].astype(_BF), expand)
            onehot = jnp.where(pos_e == slot, 1.0, 0.0).astype(_BF)
            f = _dot(onehot, ye)
            y_ref[j, rows, :] = x1_ref[j, rows, :] + _rms(f, gain)
            return carry

        _tile_loop(n // tile, body)


def _combine_call(pos, ye, x1, mod, norm_post, cap, slot_off, latent):
    b, n, _ = x1.shape
    g = max(1, SMALL_STEP_TOKENS // n)
    assert g == 1 or not latent
    blk_off = slot_off // (g * cap)
    return pl.pallas_call(
        functools.partial(_combine_kernel, n=n, cap=cap, g=g, latent=latent),
        out_shape=jax.ShapeDtypeStruct((b, n, D), _F32),
        grid=(b // g,),
        in_specs=[
            pl.BlockSpec((g, N_EXP, n), lambda i: (i, 0, 0)),
            pl.BlockSpec((N_EXP, g * cap, D), lambda i: (0, i + blk_off, 0)),
            pl.BlockSpec((g, n, D), lambda i: (i, 0, 0)),
            pl.BlockSpec((N_MOD, SUBLANES, D), lambda i: (0, 0, 0)),
            pl.BlockSpec((1, D), lambda i: (0, 0)),
        ],
        out_specs=pl.BlockSpec((g, n, D), lambda i: (i, 0, 0)),
        compiler_params=pltpu.CompilerParams(
            dimension_semantics=("arbitrary",), vmem_limit_bytes=VMEM_LIMIT),
        name="combine_latent" if latent else "combine_context",
    )(pos, ye, x1, mod, norm_post)


def _block_diag_gates(wa, wi):
    per_half = LRU_HEADS // 2
    side = per_half * LRU_HD
    on_diag = (np.arange(side)[:, None] // LRU_HD) == (np.arange(side)[None, :] // LRU_HD)

    def bd(w):
        rows = w.reshape(2, 2, side, LRU_HD)
        return jnp.where(on_diag, jnp.tile(rows, (1, 1, 1, per_half)), 0.0)

    return 0.5 * jnp.concatenate([bd(wa), bd(wi)], axis=-1)


def _rope_tables(n):
    rows = n // GRID_W
    row = np.repeat(np.arange(rows, dtype=np.float32), GRID_W)
    col = np.tile(np.arange(GRID_W, dtype=np.float32), rows)
    nf = DK // 4
    freqs = np.float32(ROPE_BASE) ** (-np.arange(nf, dtype=np.float32) / np.float32(nf))
    ang = np.concatenate([row[:, None] * freqs, col[:, None] * freqs], axis=-1).astype(np.float32)
    cos = np.cos(ang)
    sin = np.sin(ang)
    return (jnp.asarray(np.concatenate([cos, cos], axis=-1), _F32),
            jnp.asarray(np.concatenate([-sin, sin], axis=-1), _F32))


def _decay_consts(n):
    heads = np.arange(RET_H, dtype=np.float32)
    f32 = np.float32
    lgf = np.log1p(-np.exp2(-(f32(RET_DECAY_OFFSET_FWD) + heads))).astype(f32)
    lgb = np.log1p(-np.exp2(-(f32(RET_DECAY_OFFSET_BWD) + heads))).astype(f32)
    tab = np.stack([lgf, lgb, np.exp(lgf), np.exp(f32(n) * lgb), np.exp(f32(n - 1) * lgf)], axis=1)
    return [[float(v) for v in row] for row in tab.astype(f32)]


def kernel(x_prompt, x_sample, c, state_lru, state_ret, c_ctx, ada_w, ada_b, norm_mix_pre, norm_mix_post, norm_ffn_pre, norm_ffn_post, w_in, conv_w, conv_b, lru_wa, lru_ba, lru_wi, lru_bi, lru_lambda, w_out, router_w, exp_w_gate, exp_w_up, exp_w_down):
    bp, n_p, _ = x_prompt.shape
    bs, n_s, _ = x_sample.shape
    cap_p = 2 * n_p // N_EXP
    cap_s = 2 * n_s // N_EXP
    l = 0

    mod = _ada_call(c_ctx[None, :], c, ada_w[l], ada_b[l][None, :])

    norms = (norm_mix_pre[l][None], norm_mix_post[l][None], norm_ffn_pre[l][None])
    lru = (conv_w[l], conv_b[l][None], lru_ba[l], lru_bi[l], lru_lambda[l])
    wg = _block_diag_gates(lru_wa[l], lru_wi[l])
    rw = jnp.pad(router_w[l], ((0, 0), (0, LANES - N_EXP)))
    cos2, sin2 = _rope_tables(n_s)

    x1_p, h2_p, lt_p, st_lru, st_ret = _mixer_call(
        x_prompt, mod, norms, w_in[l], lru, wg, w_out[l], rw, latent=False)
    x1_s, h2_s, lt_s = _mixer_call(
        x_sample, mod, norms, w_in[l], lru, wg, w_out[l], rw, latent=True,
        extra=(state_lru, state_ret, cos2, sin2), layer=l)

    pos_p, gate_p, pos_s, gate_s = _route_call(lt_p, lt_s, cap_p, cap_s)
    xs_p, gsl_p = _dispatch_call(pos_p, gate_p, h2_p, cap_p)
    xs_s, gsl_s = _dispatch_call(pos_s, gate_s, h2_s, cap_s)
    ye = _expert_call(xs_p, xs_s, gsl_p, gsl_s, exp_w_gate[l], exp_w_up[l], exp_w_down[l])

    norm_post = norm_ffn_post[l][None]
    y_p = _combine_call(pos_p, ye, x1_p, mod, norm_post, cap_p, 0, latent=False)
    y_s = _combine_call(pos_s, ye, x1_s, mod, norm_post, cap_s, bp * cap_p, latent=True)
    return (y_p, y_s, st_lru, st_ret)
```

```python
import functools
import math
import types

import jax
import jax.numpy as jnp
import numpy as np
from jax import lax
from jax.experimental import pallas as pl
from jax.experimental.pallas import tpu as pltpu

D = 1024
LRU_W = 512
LRU_HEADS = 8
LRU_HD = 64
LRU_C = 8.0
RET_W = 512
RET_H = 4
DK = 128
N_EXP = 16
FF = 2048
N_MOD = 6
EPS = 1e-6
GRID_W = 64
ROPE_BASE = 10000.0
RET_DECAY_OFFSET_FWD = 5.0
RET_DECAY_OFFSET_BWD = 5.5

ROW_TILE = 256
WIDE_TILE = 512
FF_CHUNK = 512
COMBINE_TILE = 1024
CONTEXT_GROUP = 2
SMALL_STEP_TOKENS = 1024
SUBLANES = 8
LANES = 128
VMEM_LIMIT = 60 * 1024 * 1024

DEC_LOG_F, DEC_LOG_B, DEC_G_F, DEC_GN_B, DEC_GN1_F = 0, 1, 2, 3, 4

_BF = jnp.bfloat16
_F32 = jnp.float32


def _sigmoid(x):
    return 0.5 * jnp.tanh(0.5 * x) + 0.5


def _silu(x):
    return x * _sigmoid(x)


def _gelu_tanh(x):
    c = math.sqrt(2.0 / math.pi)
    return 0.5 * x * (1.0 + jnp.tanh(c * (x + 0.044715 * (x * x * x))))


def _rms(x, gain):
    return x * lax.rsqrt(jnp.mean(x * x, axis=-1, keepdims=True) + EPS) * gain


def _dot(a, b):
    return jnp.dot(a, b, preferred_element_type=_F32)


def _dot_nt(a, b):
    return lax.dot_general(a, b, (((1,), (1,)), ((), ())), preferred_element_type=_F32)


def _dot_tn(a, b):
    return lax.dot_general(a, b, (((0,), (0,)), ((), ())), preferred_element_type=_F32)


def _ada_kernel(cc_ref, c_ref, w_ref, b_ref, o_ref, s_ref):
    nb = c_ref.shape[0]
    s_ref[...] = jnp.zeros_like(s_ref)
    s_ref[0:1, :] = _silu(cc_ref[...])
    s_ref[1:1 + nb, :] = _silu(c_ref[...])
    o_ref[...] = _dot(s_ref[...], w_ref[...]) + b_ref[...]


def _ada_call(c_ctx, c, ada_w, ada_b):
    nb = c.shape[0]
    assert nb + 1 <= SUBLANES
    return pl.pallas_call(
        _ada_kernel,
        out_shape=jax.ShapeDtypeStruct((N_MOD, SUBLANES, D), _F32),
        grid=(N_MOD,),
        in_specs=[
            pl.BlockSpec((1, D), lambda j: (0, 0)),
            pl.BlockSpec((nb, D), lambda j: (0, 0)),
            pl.BlockSpec((D, D), lambda j: (0, j)),
            pl.BlockSpec((1, D), lambda j: (0, j)),
        ],
        out_specs=pl.BlockSpec((None, SUBLANES, D), lambda j: (j, 0, 0)),
        scratch_shapes=[pltpu.VMEM((SUBLANES, D), _F32)],
        compiler_params=pltpu.CompilerParams(
            dimension_semantics=("arbitrary",), vmem_limit_bytes=VMEM_LIMIT),
        name="ada_mod",
    )(c_ctx, c, ada_w, ada_b)


def _tile_loop(nt, body, unroll=1):
    if nt == 1:
        body(0, 0)
    else:
        lax.fori_loop(0, nt, body, 0, unroll=unroll)


def _mixer_kernel(*refs, n, g, latent):
    if latent:
        (x_ref, mod_ref, n1_ref, n2_ref, n3_ref, wl_ref, win_hbm,
         cw_ref, cb_ref, ba_ref, bi_ref, lam_ref, wg_ref, wout_hbm, rw_ref,
         h0_ref, s0_ref, cos_ref, sin_ref,
         x1_ref, h2_ref, lt_ref,
         xlp_g, gy_g, af_g, ab_g, hf_g, hb_g, wret_v, wout_ref, late_sem) = refs
        sg_g, qf_g, qb_g, kf_g, kb_g, v_g = xlp_g, af_g, ab_g, hf_g, hb_g, gy_g
        wqk_ref, wvg_ref = wret_v.at[:, 0:D], wret_v.at[:, D:2 * D]
        first_step = pl.program_id(0) == 0
        late_ret = pltpu.make_async_copy(win_hbm.at[:, pl.ds(D, 2 * D)], wret_v, late_sem.at[0])
        late_out = pltpu.make_async_copy(wout_hbm, wout_ref, late_sem.at[1])

        @pl.when(first_step)
        def _():
            late_ret.start()
            late_out.start()
    else:
        (x_ref, mod_ref, n1_ref, n2_ref, n3_ref, wl_ref, wqk_ref, wvg_ref,
         cw_ref, cb_ref, ba_ref, bi_ref, lam_ref, wg_ref, wout_ref, rw_ref,
         x1_ref, h2_ref, lt_ref, stl_ref, str_ref,
         xlp_g, gy_g, af_g, ab_g, hf_g, hb_g, sg_g, qf_g, qb_g, kf_g, kb_g, v_g) = refs
    early_ret = not latent

    seqs = []
    for s in range(g):
        q = types.SimpleNamespace(
            x=x_ref.at[s], x1=x1_ref.at[s], h2=h2_ref.at[s], lt=lt_ref.at[s], mix=x1_ref.at[s],
            xlp=xlp_g.at[s], sg=sg_g.at[s], gy=gy_g.at[s], v=v_g.at[s],
            a_f=af_g.at[s], a_b=ab_g.at[s], h_f=hf_g.at[s], h_b=hb_g.at[s],
            q_f=qf_g.at[s], q_b=qb_g.at[s], k_f=kf_g.at[s], k_b=kb_g.at[s])
        if latent:
            q.h0, q.s0 = h0_ref.at[s], s0_ref.at[s]
        else:
            q.stl, q.str = stl_ref.at[s], str_ref.at[s]
        seqs.append(q)

    nt = n // ROW_TILE
    decay = _decay_consts(n)
    mod_row = (pl.program_id(0) + 1) if latent else 0
    mod = lambda k: mod_ref[k, pl.ds(mod_row, 1), :]
    shift = mod(0)
    scale = n1_ref[...] * (1.0 + mod(1))

    def normed(q, rows):
        return _rms(q.x[rows, :], scale) + shift

    wide = min(n, WIDE_TILE)
    a_tile = wide if latent else ROW_TILE

    def stacked(parts):
        return parts[0] if len(parts) == 1 else jnp.concatenate(parts, axis=0)

    def store_retention(q, r0, rows, pqk, pvg):
        q.sg[rows, :] = _silu(pvg[:, RET_W:])
        tpos = (r0 + lax.broadcasted_iota(jnp.int32, (pqk.shape[0], DK), 0)).astype(_F32)
        if latent:
            cos2 = cos_ref[rows, :]
            sin2 = sin_ref[rows, :]
        for hd in range(RET_H):
            cols = slice(hd * DK, (hd + 1) * DK)
            qh = pqk[:, cols] * (DK ** -0.5)
            kh = pqk[:, RET_W + hd * DK:RET_W + (hd + 1) * DK]
            if latent:
                qh = qh * cos2 + pltpu.roll(qh, DK // 2, axis=1) * sin2
                kh = kh * cos2 + pltpu.roll(kh, DK // 2, axis=1) * sin2
            lgf = decay[hd][DEC_LOG_F]
            lgb = decay[hd][DEC_LOG_B]
            q.q_f[hd, rows, :] = qh * jnp.exp(tpos * lgf)
            q.k_f[hd, rows, :] = kh * jnp.exp(tpos * (-lgf))
            q.q_b[hd, rows, :] = qh * jnp.exp(tpos * (-lgb))
            q.k_b[hd, rows, :] = kh * jnp.exp(tpos * lgb)
        q.v[rows, :] = pvg[:, 0:RET_W]

    def phase_a1(r, carry):
        r0 = pl.multiple_of(r * a_tile, a_tile)
        rows = pl.ds(r0, a_tile)
        h = stacked([normed(q, rows) for q in seqs])
        p_all = _dot(h, wl_ref[...])
        if early_ret:
            pqk_all = _dot(h, wqk_ref[...])
            pvg_all = _dot(h, wvg_ref[...])
        for s, q in enumerate(seqs):
            part = slice(s * a_tile, (s + 1) * a_tile)
            q.xlp[pl.ds(r0 + SUBLANES, a_tile), :] = p_all[part, 0:LRU_W]
            q.gy[rows, :] = _gelu_tanh(p_all[part, LRU_W:])
            if early_ret:
                store_retention(q, r0, rows, pqk_all[part], pvg_all[part])
        return carry

    for q in seqs:
        q.xlp[0:SUBLANES, :] = jnp.zeros((SUBLANES, LRU_W), _F32)
        q.xlp[n + SUBLANES:n + 2 * SUBLANES, :] = jnp.zeros((SUBLANES, LRU_W), _F32)

    half = LRU_W // 2

    def softplus_neg(lam):
        z = -lam
        return jnp.maximum(z, 0.0) + jnp.log1p(jnp.exp(-jnp.abs(z)))

    sp = (softplus_neg(lam_ref[0:1, :]), softplus_neg(lam_ref[1:2, :]))

    def phase_b(r, carry):
        r0 = pl.multiple_of(r * a_tile, a_tile)
        rows = pl.ds(r0, a_tile)
        xcs = []
        for q in seqs:
            ext = q.xlp[pl.ds(r0, a_tile + 2 * SUBLANES), :]
            xc = cb_ref[...]
            for tap in range(4):
                o = SUBLANES - 2 + tap
                xc = xc + ext[o:o + a_tile, :] * cw_ref[tap:tap + 1, :]
            xcs.append(xc)
        xhs = [0.5 * xc for xc in xcs]
        for d in range(2):
            bah = 0.5 * ba_ref[d:d + 1, :]
            bih = 0.5 * bi_ref[d:d + 1, :]
            ch = (-0.5 * LRU_C) * sp[d]
            for hh in range(2):
                cs = slice(hh * half, (hh + 1) * half)
                pre_all = _dot(stacked([xc[:, cs] for xc in xcs]), wg_ref[d, hh])
                for s, q in enumerate(seqs):
                    a_ref, u_ref = ((q.a_f, q.h_f), (q.a_b, q.h_b))[d]
                    pre = pre_all[s * a_tile:(s + 1) * a_tile]
                    xh = xhs[s][:, cs]
                    t_r = jnp.tanh(pre[:, 0:half] + bah[:, cs])
                    t_i = jnp.tanh(pre[:, half:] + bih[:, cs])
                    log_a = t_r * ch[:, cs] + ch[:, cs]
                    a = jnp.exp(log_a)
                    om = -jnp.tanh(log_a) * (a * a + 1.0)
                    root = jnp.where(om > 0.0, om * lax.rsqrt(om), 0.0)
                    u = root * (t_i * xh + xh)
                    for j in range(half // LANES):
                        lanes = slice(j * LANES, (j + 1) * LANES)
                        a_ref[hh * (half // LANES) + j, rows, :] = a[:, lanes]
                        u_ref[hh * (half // LANES) + j, rows, :] = u[:, lanes]
        return carry

    _tile_loop(n // a_tile, phase_a1)
    _tile_loop(n // a_tile, phase_b)

    row8 = lax.broadcasted_iota(jnp.int32, (SUBLANES, LANES), 0)
    block = SUBLANES * SUBLANES
    n_blocks = n // block
    n_slabs = LRU_W // LANES

    def across_groups(a, b, reverse):
        for s in (1, 2, 4):
            m = (row8 < SUBLANES - s) if reverse else (row8 >= s)
            shift = SUBLANES - s if reverse else s
            a_s = jnp.where(m, pltpu.roll(a, shift, axis=0), 1.0)
            b_s = jnp.where(m, pltpu.roll(b, shift, axis=0), 0.0)
            b = a * b_s + b
            a = a * a_s
        return a, b

    def scan_block(a_ref, h_ref, base, carry, reverse):
        rows = [pl.ds(base + k, SUBLANES, stride=SUBLANES) for k in range(SUBLANES)]
        order = list(reversed(range(SUBLANES))) if reverse else list(range(SUBLANES))
        prod, local = {}, {}
        prev = None
        for k in order:
            a, u = a_ref[rows[k], :], h_ref[rows[k], :]
            prod[k] = a if prev is None else a * prod[prev]
            local[k] = u if prev is None else a * local[prev] + u
            prev = k
        p_all, h_all = across_groups(prod[prev], local[prev], reverse)
        inner = (row8 < SUBLANES - 1) if reverse else (row8 >= 1)
        shift = SUBLANES - 1 if reverse else 1
        enter = (jnp.where(inner, pltpu.roll(p_all, shift, axis=0), 1.0) * carry
                 + jnp.where(inner, pltpu.roll(h_all, shift, axis=0), 0.0))
        for k in order:
            h_ref[rows[k], :] = prod[k] * enter + local[k]
        leave = p_all * carry + h_all
        return leave[0:1, :] if reverse else leave[SUBLANES - 1:SUBLANES, :]

    def initial(q, d, s):
        if latent:
            return q.h0[d:d + 1, s * LANES:(s + 1) * LANES]
        return jnp.zeros((1, LANES), _F32)

    chains = [(q, s) for q in seqs for s in range(n_slabs)]

    def scan_body(i, carry):
        fwd_base = pl.multiple_of(i * block, block)
        bwd_base = pl.multiple_of((n_blocks - 1 - i) * block, block)
        out = []
        for (q, s), (cf, cb) in zip(chains, carry):
            out.append((scan_block(q.a_f.at[s], q.h_f.at[s], fwd_base, cf, False),
                        scan_block(q.a_b.at[s], q.h_b.at[s], bwd_base, cb, True)))
        return tuple(out)

    ends = lax.fori_loop(0, n_blocks, scan_body,
                         tuple((initial(q, 0, s), initial(q, 1, s)) for q, s in chains), unroll=2)

    def lru_out(q):
        def body(r, carry):
            rows = pl.ds(pl.multiple_of(r * ROW_TILE, ROW_TILE), ROW_TILE)
            for s in range(n_slabs):
                lanes = slice(s * LANES, (s + 1) * LANES)
                q.mix[rows, lanes] = (q.h_f[s, rows, :] + q.h_b[s, rows, :]) * q.gy[rows, lanes]
            return carry
        return body

    for (q, s), (last_f, first_b) in zip(chains, ends):
        if not latent:
            q.stl[0:1, s * LANES:(s + 1) * LANES] = last_f
            q.stl[1:2, s * LANES:(s + 1) * LANES] = first_b
    for q in seqs:
        _tile_loop(nt, lru_out(q))

    def phase_a2(r, carry):
        r0 = pl.multiple_of(r * wide, wide)
        rows = pl.ds(r0, wide)
        h = stacked([normed(q, rows) for q in seqs])
        pqk_all = _dot(h, wqk_ref[...])
        pvg_all = _dot(h, wvg_ref[...])
        for s, q in enumerate(seqs):
            part = slice(s * wide, (s + 1) * wide)
            store_retention(q, r0, rows, pqk_all[part], pvg_all[part])
        return carry

    if not early_ret:
        @pl.when(first_step)
        def _():
            late_ret.wait()

        _tile_loop(n // wide, phase_a2)

    lower = (lax.broadcasted_iota(jnp.int32, (ROW_TILE, ROW_TILE), 0)
             >= lax.broadcasted_iota(jnp.int32, (ROW_TILE, ROW_TILE), 1))
    blocks = [slice(r * ROW_TILE, (r + 1) * ROW_TILE) for r in range(nt)]

    def phase_c(q, hd):
        cols = slice(hd * DK, (hd + 1) * DK)
        dec = lambda row: decay[hd][row]
        kv_f = [_dot_tn(q.k_f[hd, rows, :], q.v[rows, cols]) if (r < nt - 1 or not latent) else 0.0
                for r, rows in enumerate(blocks)]
        kv_b = [_dot_tn(q.k_b[hd, rows, :], q.v[rows, cols]) if (r > 0 or not latent) else 0.0
                for r, rows in enumerate(blocks)]
        if latent:
            run_f = q.s0[0, hd] * dec(DEC_G_F)
            run_b = q.s0[1, hd] * dec(DEC_GN_B)
        else:
            run_f = run_b = None
        before = []
        for r in range(nt):
            before.append(run_f)
            run_f = kv_f[r] if run_f is None else run_f + kv_f[r]
        after = [None] * nt
        for r in reversed(range(nt)):
            after[r] = run_b
            run_b = kv_b[r] if run_b is None else run_b + kv_b[r]

        for r, rows in enumerate(blocks):
            qf = q.q_f[hd, rows, :]
            qb = q.q_b[hd, rows, :]
            s = jnp.where(lower, _dot_nt(qf, q.k_f[hd, rows, :]), _dot_nt(qb, q.k_b[hd, rows, :]))
            o = _dot(s, q.v[rows, cols])
            if before[r] is not None:
                o = o + _dot(qf, before[r])
            if after[r] is not None:
                o = o + _dot(qb, after[r])
            o = o * lax.rsqrt(jnp.mean(o * o, axis=-1, keepdims=True) + EPS)
            q.mix[rows, LRU_W + cols.start:LRU_W + cols.stop] = o * q.sg[rows, cols]
        if not latent:
            q.str[0, hd] = run_f * dec(DEC_GN1_F)
            q.str[1, hd] = run_b

    for hd in range(RET_H):
        for q in seqs:
            phase_c(q, hd)

    gain1 = mod(2) * n2_ref[...]
    gain2 = n3_ref[...] * (1.0 + mod(4))
    sh2 = mod(3)

    def phase_d(r, carry):
        rows = pl.ds(pl.multiple_of(r * wide, wide), wide)
        mix_all = _dot(stacked([q.mix[rows, :] for q in seqs]), wout_ref[...])
        h2s = []
        for s, q in enumerate(seqs):
            x1 = q.x[rows, :] + _rms(mix_all[s * wide:(s + 1) * wide], gain1)
            q.x1[rows, :] = x1
            h2s.append(_rms(x1, gain2) + sh2)
            q.h2[rows, :] = h2s[-1].astype(_BF)
        logits_t = _dot(stacked(h2s), rw_ref[...]).T
        for s, q in enumerate(seqs):
            q.lt[:, rows] = logits_t[0:N_EXP, s * wide:(s + 1) * wide]
        return carry

    if latent:
        @pl.when(first_step)
        def _():
            late_out.wait()

    _tile_loop(n // wide, phase_d, unroll=2)


def _mixer_call(x, mod, norms, w_in, lru, wg, w_out, rw, latent, extra=(), layer=0):
    b, n, _ = x.shape
    g = 1 if latent else CONTEXT_GROUP
    const2 = lambda i: (0, 0)
    whole = pl.BlockSpec(memory_space=pl.ANY)
    in_specs = [
        pl.BlockSpec((g, n, D), lambda i: (i, 0, 0)),
        pl.BlockSpec((N_MOD, SUBLANES, D), lambda i: (0, 0, 0)),
        pl.BlockSpec((1, D), const2), pl.BlockSpec((1, D), const2), pl.BlockSpec((1, D), const2),
        pl.BlockSpec((D, D), lambda i: (0, 0)),
    ]
    if latent:
        in_specs += [whole]
    else:
        in_specs += [pl.BlockSpec((D, D), lambda i: (0, 1)),
                     pl.BlockSpec((D, D), lambda i: (0, 2))]
    in_specs += [
        pl.BlockSpec((4, LRU_W), const2), pl.BlockSpec((1, LRU_W), const2),
        pl.BlockSpec((2, LRU_W), const2), pl.BlockSpec((2, LRU_W), const2),
        pl.BlockSpec((2, LRU_W), const2),
        pl.BlockSpec((2, 2, LRU_W // 2, LRU_W), lambda i: (0, 0, 0, 0)),
        whole if latent else pl.BlockSpec((D, D), const2),
        pl.BlockSpec((D, LANES), const2),
    ]
    out_shape = [
        jax.ShapeDtypeStruct((b, n, D), _F32),
        jax.ShapeDtypeStruct((b, n, D), _BF),
        jax.ShapeDtypeStruct((b, N_EXP, n), _F32),
    ]
    out_specs = [
        pl.BlockSpec((g, n, D), lambda i: (i, 0, 0)),
        pl.BlockSpec((g, n, D), lambda i: (i, 0, 0)),
        pl.BlockSpec((g, N_EXP, n), lambda i: (i, 0, 0)),
    ]
    if latent:
        in_specs += [
            pl.BlockSpec((g, None, 2, LRU_W), lambda i: (i, layer, 0, 0)),
            pl.BlockSpec((g, None, 2, RET_H, DK, DK), lambda i: (i, layer, 0, 0, 0, 0)),
            pl.BlockSpec((n, DK), const2),
            pl.BlockSpec((n, DK), const2),
        ]
    else:
        out_shape += [
            jax.ShapeDtypeStruct((b, 1, 2, LRU_W), _F32),
            jax.ShapeDtypeStruct((b, 1, 2, RET_H, DK, DK), _F32),
        ]
        out_specs += [
            pl.BlockSpec((g, None, 2, LRU_W), lambda i: (i, 0, 0, 0)),
            pl.BlockSpec((g, None, 2, RET_H, DK, DK), lambda i: (i, 0, 0, 0, 0, 0)),
        ]
    f32s = lambda shape: pltpu.VMEM((g,) + shape, _F32)
    slabs = (LRU_W // LANES, n, LANES)
    scratch = [
        f32s((n + 2 * SUBLANES, LRU_W)),
        f32s((n, LRU_W)),
        f32s(slabs), f32s(slabs),
        f32s(slabs), f32s(slabs),
    ]
    if latent:
        scratch += [pltpu.VMEM((D, 2 * D), _F32), pltpu.VMEM((D, D), _F32),
                    pltpu.SemaphoreType.DMA((2,))]
        w_in_args = (w_in, w_in)
    else:
        scratch += [f32s((n, RET_W))] + [f32s(slabs) for _ in range(4)] + [f32s((n, RET_W))]
        w_in_args = (w_in, w_in, w_in)
    return pl.pallas_call(
        functools.partial(_mixer_kernel, n=n, g=g, latent=latent),
        out_shape=out_shape,
        grid=(b // g,),
        in_specs=in_specs,
        out_specs=out_specs,
        scratch_shapes=scratch,
        compiler_params=pltpu.CompilerParams(
            dimension_semantics=("arbitrary",), vmem_limit_bytes=VMEM_LIMIT),
        name="mixer_latent" if latent else "mixer_context",
    )(x, mod, *norms, *w_in_args, *lru, wg, w_out, rw, *extra)


def _count(mask):
    return jnp.sum(jnp.where(mask, 1.0, 0.0), axis=-1, keepdims=True)


def _probs(l3):
    bsz, _, n = l3.shape
    m = jnp.max(l3, axis=1, keepdims=True)
    e = jnp.exp(l3 - m)
    return (e / jnp.sum(e, axis=1, keepdims=True)).reshape(bsz * N_EXP, n)


def _break_ties(parts):
    prep = []
    for bits, thr, cap in parts:
        rows, n = bits.shape
        eq = bits == thr
        need = float(cap) - _count(bits > thr)
        idx = lax.broadcasted_iota(jnp.int32, (rows, n), 1)
        prep.append((eq, need, idx, int(math.log2(n))))
    most = max(nbits for _, _, _, nbits in prep)

    def idx_body(i, last):
        out = []
        for (eq, need, idx, nbits), j in zip(prep, last):
            shift = nbits - 1 - i
            cand = j | jnp.where(shift >= 0, jnp.int32(1) << jnp.maximum(shift, 0), 0)
            out.append(jnp.where(_count(eq & (idx < cand)) < need, cand, j))
        return tuple(out)

    return lax.fori_loop(0, most, idx_body,
                         tuple(jnp.zeros((eq.shape[0], 1), jnp.int32) for eq, _, _, _ in prep))


def _slots(p, bits, thr, jlast):
    rows, n = p.shape
    idx = lax.broadcasted_iota(jnp.int32, (rows, n), 1)
    sel = (bits > thr) | ((bits == thr) & (idx <= jlast))
    before = (lax.broadcasted_iota(jnp.int32, (n, n), 0)
              < lax.broadcasted_iota(jnp.int32, (n, n), 1))
    pos = _dot(jnp.where(sel, 1.0, 0.0).astype(_BF), jnp.where(before, 1.0, 0.0).astype(_BF))
    return jnp.where(sel, pos, -1.0), jnp.where(sel, p, 0.0)


def _route_kernel(lp_ref, ls_ref, pp_ref, gp_ref, ps_ref, gs_ref, *, cap_p, cap_s):
    groups = ((_probs(lp_ref[...]), float(cap_p)), (_probs(ls_ref[...]), float(cap_s)))
    bits = [pltpu.bitcast(p, jnp.int32) for p, _ in groups]

    def settle(b, t, capf, hi, lo):
        with_hi = t | hi
        both = with_hi | lo
        with_lo = t | lo
        ok = lambda cand: _count(b >= cand) >= capf
        return jnp.where(ok(with_hi), jnp.where(ok(both), both, with_hi),
                         jnp.where(ok(with_lo), with_lo, t))

    def val_body(i, thr):
        hi = jnp.int32(1) << (29 - 2 * i)
        lo = jnp.int32(1) << (28 - 2 * i)
        return tuple(settle(b, t, capf, hi, lo) for b, t, (_, capf) in zip(bits, thr, groups))

    thr = lax.fori_loop(0, 15, val_body,
                        tuple(jnp.zeros((b.shape[0], 1), jnp.int32) for b in bits))
    jlast = _break_ties([(bits[0], thr[0], cap_p), (bits[1], thr[1], cap_s)])
    pos, gate = _slots(groups[0][0], bits[0], thr[0], jlast[0])
    pp_ref[...] = pos.reshape(pp_ref.shape)
    gp_ref[...] = gate.reshape(gp_ref.shape)
    pos, gate = _slots(groups[1][0], bits[1], thr[1], jlast[1])
    ps_ref[...] = pos.reshape(ps_ref.shape)
    gs_ref[...] = gate.reshape(gs_ref.shape)


def _route_call(lt_p, lt_s, cap_p, cap_s):
    shapes = [
        jax.ShapeDtypeStruct(lt_p.shape, _F32), jax.ShapeDtypeStruct(lt_p.shape, _F32),
        jax.ShapeDtypeStruct(lt_s.shape, _F32), jax.ShapeDtypeStruct(lt_s.shape, _F32),
    ]
    return pl.pallas_call(
        functools.partial(_route_kernel, cap_p=cap_p, cap_s=cap_s),
        out_shape=shapes,
        compiler_params=pltpu.CompilerParams(vmem_limit_bytes=VMEM_LIMIT),
        name="route_select",
    )(lt_p, lt_s)


def _dispatch_kernel(pos_ref, gate_ref, h_ref, xs_ref, gs_ref, *, n, cap, g):
    slot = lax.broadcasted_iota(jnp.int32, (cap, n), 0).astype(_F32)
    for j in range(g):
        slots = slice(j * cap, (j + 1) * cap)
        parts = []
        for e in range(N_EXP):
            hit = pos_ref[j, e:e + 1, :] == slot
            parts.append(jnp.where(hit, 1.0, 0.0).astype(_BF))
            gs_ref[e, slots, :] = jnp.sum(jnp.where(hit, gate_ref[j, e:e + 1, :], 0.0),
                                          axis=-1, keepdims=True)
        onehot = jnp.concatenate(parts, axis=0)
        xs = _dot(onehot, h_ref[j]).astype(_BF)
        for e in range(N_EXP):
            xs_ref[e, slots, :] = xs[e * cap:(e + 1) * cap, :]


def _dispatch_call(pos, gate, h2, cap):
    b, n, _ = h2.shape
    g = max(1, SMALL_STEP_TOKENS // n)
    return pl.pallas_call(
        functools.partial(_dispatch_kernel, n=n, cap=cap, g=g),
        out_shape=[
            jax.ShapeDtypeStruct((N_EXP, b * cap, D), _BF),
            jax.ShapeDtypeStruct((N_EXP, b * cap, 1), _F32),
        ],
        grid=(b // g,),
        in_specs=[
            pl.BlockSpec((g, N_EXP, n), lambda i: (i, 0, 0)),
            pl.BlockSpec((g, N_EXP, n), lambda i: (i, 0, 0)),
            pl.BlockSpec((g, n, D), lambda i: (i, 0, 0)),
        ],
        out_specs=[
            pl.BlockSpec((N_EXP, g * cap, D), lambda i: (0, i, 0)),
            pl.BlockSpec((N_EXP, g * cap, 1), lambda i: (0, i, 0)),
        ],
        compiler_params=pltpu.CompilerParams(
            dimension_semantics=("arbitrary",), vmem_limit_bytes=VMEM_LIMIT),
        name="dispatch",
    )(pos, gate, h2)


def _expert_kernel(xp_ref, xs_ref, gp_ref, gs_ref, wg_ref, wu_ref, wd_ref, y_ref, xcat, acc,
                   *, sp, nf, tf):
    f = pl.program_id(1)
    xcat[0:sp, :] = xp_ref[...]
    xcat[sp:, :] = xs_ref[...]
    x = xcat[...]
    total = jnp.where(f == 0, 0.0, acc[...])
    for c in range(tf // FF_CHUNK):
        cs = slice(c * FF_CHUNK, (c + 1) * FF_CHUNK)
        hg = _dot(x, wg_ref[:, cs].astype(_BF))
        hu = _dot(x, wu_ref[:, cs].astype(_BF))
        hid = (_silu(hg) * hu).astype(_BF)
        total = total + _dot(hid, wd_ref[cs, :].astype(_BF))
    acc[...] = total
    y_ref[0:sp, :] = (total[0:sp, :] * gp_ref[...]).astype(_BF)
    y_ref[sp:, :] = (total[sp:, :] * gs_ref[...]).astype(_BF)


def _expert_call(xs_p, xs_s, g_p, g_s, w_gate, w_up, w_down):
    tf = 1024
    sp = xs_p.shape[1]
    ss = xs_s.shape[1]
    nf = FF // tf
    return pl.pallas_call(
        functools.partial(_expert_kernel, sp=sp, nf=nf, tf=tf),
        out_shape=jax.ShapeDtypeStruct((N_EXP, sp + ss, D), _BF),
        grid=(N_EXP, nf),
        in_specs=[
            pl.BlockSpec((None, sp, D), lambda e, f: (e, 0, 0)),
            pl.BlockSpec((None, ss, D), lambda e, f: (e, 0, 0)),
            pl.BlockSpec((None, sp, 1), lambda e, f: (e, 0, 0)),
            pl.BlockSpec((None, ss, 1), lambda e, f: (e, 0, 0)),
            pl.BlockSpec((None, D, tf), lambda e, f: (e, 0, f)),
            pl.BlockSpec((None, D, tf), lambda e, f: (e, 0, f)),
            pl.BlockSpec((None, tf, D), lambda e, f: (e, f, 0)),
        ],
        out_specs=pl.BlockSpec((None, sp + ss, D), lambda e, f: (e, 0, 0)),
        scratch_shapes=[pltpu.VMEM((sp + ss, D), _BF), pltpu.VMEM((sp + ss, D), _F32)],
        compiler_params=pltpu.CompilerParams(
            dimension_semantics=("arbitrary", "arbitrary"), vmem_limit_bytes=VMEM_LIMIT),
        name="expert_ffn",
    )(xs_p, xs_s, g_p, g_s, w_gate, w_up, w_down)


def _combine_kernel(pos_ref, ye_ref, x1_ref, mod_ref, n4_ref, y_ref, *, n, cap, g, latent):
    width = N_EXP * cap
    lane = lax.broadcasted_iota(jnp.int32, (N_EXP, width), 1)
    expand = jnp.where(lane // cap == lax.broadcasted_iota(jnp.int32, (N_EXP, width), 0),
                       1.0, 0.0).astype(_BF)
    tile = min(n, COMBINE_TILE)
    slot = (lax.broadcasted_iota(jnp.int32, (tile, width), 1) % cap).astype(_F32)
    mod_row = (pl.program_id(0) + 1) if latent else 0
    gain = mod_ref[N_MOD - 1, pl.ds(mod_row, 1), :] * n4_ref[...]

    for j in range(g):
        ye = ye_ref[:, j * cap:(j + 1) * cap, :].reshape(width, D)

        def body(r, carry, j=j, ye=ye):
            rows = pl.ds(pl.multiple_of(r * tile, tile), tile)
            pos_e = _dot_tn(pos_ref[j, :, rows].astype(_BF), expand)
            onehot = jnp.where(pos_e == slot, 1.0, 0.0).astype(_BF)
            f = _dot(onehot, ye)
            y_ref[j, rows, :] = x1_ref[j, rows, :] + _rms(f, gain)
            return carry

        _tile_loop(n // tile, body)


def _combine_call(pos, ye, x1, mod, norm_post, cap, slot_off, latent):
    b, n, _ = x1.shape
    g = max(1, SMALL_STEP_TOKENS // n)
    assert g == 1 or not latent
    blk_off = slot_off // (g * cap)
    return pl.pallas_call(
        functools.partial(_combine_kernel, n=n, cap=cap, g=g, latent=latent),
        out_shape=jax.ShapeDtypeStruct((b, n, D), _F32),
        grid=(b // g,),
        in_specs=[
            pl.BlockSpec((g, N_EXP, n), lambda i: (i, 0, 0)),
            pl.BlockSpec((N_EXP, g * cap, D), lambda i: (0, i + blk_off, 0)),
            pl.BlockSpec((g, n, D), lambda i: (i, 0, 0)),
            pl.BlockSpec((N_MOD, SUBLANES, D), lambda i: (0, 0, 0)),
            pl.BlockSpec((1, D), lambda i: (0, 0)),
        ],
        out_specs=pl.BlockSpec((g, n, D), lambda i: (i, 0, 0)),
        compiler_params=pltpu.CompilerParams(
            dimension_semantics=("arbitrary",), vmem_limit_bytes=VMEM_LIMIT),
        name="combine_latent" if latent else "combine_context",
    )(pos, ye, x1, mod, norm_post)


def _block_diag_gates(wa, wi):
    per_half = LRU_HEADS // 2
    side = per_half * LRU_HD
    on_diag = (np.arange(side)[:, None] // LRU_HD) == (np.arange(side)[None, :] // LRU_HD)

    def bd(w):
        rows = w.reshape(2, 2, side, LRU_HD)
        return jnp.where(on_diag, jnp.tile(rows, (1, 1, 1, per_half)), 0.0)

    return 0.5 * jnp.concatenate([bd(wa), bd(wi)], axis=-1)


def _rope_tables(n):
    rows = n // GRID_W
    row = np.repeat(np.arange(rows, dtype=np.float32), GRID_W)
    col = np.tile(np.arange(GRID_W, dtype=np.float32), rows)
    nf = DK // 4
    freqs = np.float32(ROPE_BASE) ** (-np.arange(nf, dtype=np.float32) / np.float32(nf))
    ang = np.concatenate([row[:, None] * freqs, col[:, None] * freqs], axis=-1).astype(np.float32)
    cos = np.cos(ang)
    sin = np.sin(ang)
    return (jnp.asarray(np.concatenate([cos, cos], axis=-1), _F32),
            jnp.asarray(np.concatenate([-sin, sin], axis=-1), _F32))


def _decay_consts(n):
    heads = np.arange(RET_H, dtype=np.float32)
    f32 = np.float32
    lgf = np.log1p(-np.exp2(-(f32(RET_DECAY_OFFSET_FWD) + heads))).astype(f32)
    lgb = np.log1p(-np.exp2(-(f32(RET_DECAY_OFFSET_BWD) + heads))).astype(f32)
    tab = np.stack([lgf, lgb, np.exp(lgf), np.exp(f32(n) * lgb), np.exp(f32(n - 1) * lgf)], axis=1)
    return [[float(v) for v in row] for row in tab.astype(f32)]


def kernel(x_prompt, x_sample, c, state_lru, state_ret, c_ctx, ada_w, ada_b, norm_mix_pre, norm_mix_post, norm_ffn_pre, norm_ffn_post, w_in, conv_w, conv_b, lru_wa, lru_ba, lru_wi, lru_bi, lru_lambda, w_out, router_w, exp_w_gate, exp_w_up, exp_w_down):
    bp, n_p, _ = x_prompt.shape
    bs, n_s, _ = x_sample.shape
    cap_p = 2 * n_p // N_EXP
    cap_s = 2 * n_s // N_EXP
    l = 0

    mod = _ada_call(c_ctx[None, :], c, ada_w[l], ada_b[l][None, :])

    norms = (norm_mix_pre[l][None], norm_mix_post[l][None], norm_ffn_pre[l][None])
    lru = (conv_w[l], conv_b[l][None], lru_ba[l], lru_bi[l], lru_lambda[l])
    wg = _block_diag_gates(lru_wa[l], lru_wi[l])
    rw = jnp.pad(router_w[l], ((0, 0), (0, LANES - N_EXP)))
    cos2, sin2 = _rope_tables(n_s)

    x1_p, h2_p, lt_p, st_lru, st_ret = _mixer_call(
        x_prompt, mod, norms, w_in[l], lru, wg, w_out[l], rw, latent=False)
    x1_s, h2_s, lt_s = _mixer_call(
        x_sample, mod, norms, w_in[l], lru, wg, w_out[l], rw, latent=True,
        extra=(state_lru, state_ret, cos2, sin2), layer=l)

    pos_p, gate_p, pos_s, gate_s = _route_call(lt_p, lt_s, cap_p, cap_s)
    xs_p, gsl_p = _dispatch_call(pos_p, gate_p, h2_p, cap_p)
    xs_s, gsl_s = _dispatch_call(pos_s, gate_s, h2_s, cap_s)
    ye = _expert_call(xs_p, xs_s, gsl_p, gsl_s, exp_w_gate[l], exp_w_up[l], exp_w_down[l])

    norm_post = norm_ffn_post[l][None]
    y_p = _combine_call(pos_p, ye, x1_p, mod, norm_post, cap_p, 0, latent=False)
    y_s = _combine_call(pos_s, ye, x1_s, mod, norm_post, cap_s, bp * cap_p, latent=True)
    return (y_p, y_s, st_lru, st_ret)
```

```python
import functools
import math
import types

import jax
import jax.numpy as jnp
import numpy as np
from jax import lax
from jax.experimental import pallas as pl
from jax.experimental.pallas import tpu as pltpu

D = 1024
LRU_W = 512
LRU_HEADS = 8
LRU_HD = 64
LRU_C = 8.0
RET_W = 512
RET_H = 4
DK = 128
N_EXP = 16
FF = 2048
N_MOD = 6
EPS = 1e-6
GRID_W = 64
ROPE_BASE = 10000.0
RET_DECAY_OFFSET_FWD = 5.0
RET_DECAY_OFFSET_BWD = 5.5

ADA_ROWS = 128
ROW_TILE = 256
WIDE_TILE = 512
FF_CHUNK = 512
COMBINE_TILE = 1024
CONTEXT_GROUP = 2
SMALL_STEP_TOKENS = 1024
SUBLANES = 8
LANES = 128
VMEM_LIMIT = 60 * 1024 * 1024

DEC_LOG_F, DEC_LOG_B, DEC_G_F, DEC_GN_B, DEC_GN1_F = 0, 1, 2, 3, 4

_BF = jnp.bfloat16
_F32 = jnp.float32


def _sigmoid(x):
    return 0.5 * jnp.tanh(0.5 * x) + 0.5


def _silu(x):
    return x * _sigmoid(x)


def _gelu_tanh(x):
    c = math.sqrt(2.0 / math.pi)
    return 0.5 * x * (1.0 + jnp.tanh(c * (x + 0.044715 * (x * x * x))))


def _rms(x, gain):
    return x * lax.rsqrt(jnp.mean(x * x, axis=-1, keepdims=True) + EPS) * gain


def _dot(a, b):
    return jnp.dot(a, b, preferred_element_type=_F32)


def _dot_nt(a, b):
    return lax.dot_general(a, b, (((1,), (1,)), ((), ())), preferred_element_type=_F32)


def _dot_tn(a, b):
    return lax.dot_general(a, b, (((0,), (0,)), ((), ())), preferred_element_type=_F32)


def _ada_kernel(cc_ref, c_ref, w_ref, b_ref, o_ref, s_ref):
    k = pl.program_id(0)
    nb = c_ref.shape[0]

    @pl.when(k == 0)
    def _():
        s_ref[...] = jnp.zeros_like(s_ref)
        s_ref[0:1, :] = _silu(cc_ref[...])
        s_ref[1:1 + nb, :] = _silu(c_ref[...])
        for m in range(N_MOD):
            o_ref[m] = jnp.broadcast_to(b_ref[:, m * D:(m + 1) * D], (SUBLANES, D))

    s = s_ref[:, pl.ds(pl.multiple_of(k * ADA_ROWS, ADA_ROWS), ADA_ROWS)]
    for m in range(N_MOD):
        o_ref[m] += _dot(s, w_ref[:, m * D:(m + 1) * D])


def _ada_call(c_ctx, c, ada_w, ada_b):
    nb = c.shape[0]
    assert nb + 1 <= SUBLANES
    return pl.pallas_call(
        _ada_kernel,
        out_shape=jax.ShapeDtypeStruct((N_MOD, SUBLANES, D), _F32),
        grid=(D // ADA_ROWS,),
        in_specs=[
            pl.BlockSpec((1, D), lambda k: (0, 0)),
            pl.BlockSpec((nb, D), lambda k: (0, 0)),
            pl.BlockSpec((ADA_ROWS, N_MOD * D), lambda k: (k, 0)),
            pl.BlockSpec((1, N_MOD * D), lambda k: (0, 0)),
        ],
        out_specs=pl.BlockSpec((N_MOD, SUBLANES, D), lambda k: (0, 0, 0)),
        scratch_shapes=[pltpu.VMEM((SUBLANES, D), _F32)],
        compiler_params=pltpu.CompilerParams(
            dimension_semantics=("arbitrary",), vmem_limit_bytes=VMEM_LIMIT),
        name="ada_mod",
    )(c_ctx, c, ada_w, ada_b)


def _tile_loop(nt, body, unroll=1):
    if nt == 1:
        body(0, 0)
    else:
        lax.fori_loop(0, nt, body, 0, unroll=unroll)


def _mixer_kernel(*refs, n, g, latent):
    if latent:
        (x_ref, mod_ref, n1_ref, n2_ref, n3_ref, wl_ref, win_hbm,
         cw_ref, cb_ref, ba_ref, bi_ref, lam_ref, wg_ref, wout_hbm, rw_ref,
         h0_ref, s0_ref, cos_ref, sin_ref,
         x1_ref, h2_ref, lt_ref,
         xlp_g, gy_g, af_g, ab_g, hf_g, hb_g, wret_v, wout_ref, late_sem) = refs
        sg_g, qf_g, qb_g, kf_g, kb_g, v_g = xlp_g, af_g, ab_g, hf_g, hb_g, gy_g
        wqk_ref, wvg_ref = wret_v.at[:, 0:D], wret_v.at[:, D:2 * D]
        first_step = pl.program_id(0) == 0
        late_ret = pltpu.make_async_copy(win_hbm.at[:, pl.ds(D, 2 * D)], wret_v, late_sem.at[0])
        late_out = pltpu.make_async_copy(wout_hbm, wout_ref, late_sem.at[1])

        @pl.when(first_step)
        def _():
            late_ret.start()
            late_out.start()
    else:
        (x_ref, mod_ref, n1_ref, n2_ref, n3_ref, wl_ref, wqk_ref, wvg_ref,
         cw_ref, cb_ref, ba_ref, bi_ref, lam_ref, wg_ref, wout_ref, rw_ref,
         x1_ref, h2_ref, lt_ref, stl_ref, str_ref,
         xlp_g, gy_g, af_g, ab_g, hf_g, hb_g, sg_g, qf_g, qb_g, kf_g, kb_g, v_g) = refs
    early_ret = not latent

    seqs = []
    for s in range(g):
        q = types.SimpleNamespace(
            x=x_ref.at[s], x1=x1_ref.at[s], h2=h2_ref.at[s], lt=lt_ref.at[s], mix=x1_ref.at[s],
            xlp=xlp_g.at[s], sg=sg_g.at[s], gy=gy_g.at[s], v=v_g.at[s],
            a_f=af_g.at[s], a_b=ab_g.at[s], h_f=hf_g.at[s], h_b=hb_g.at[s],
            q_f=qf_g.at[s], q_b=qb_g.at[s], k_f=kf_g.at[s], k_b=kb_g.at[s])
        if latent:
            q.h0, q.s0 = h0_ref.at[s], s0_ref.at[s]
        else:
            q.stl, q.str = stl_ref.at[s], str_ref.at[s]
        seqs.append(q)

    nt = n // ROW_TILE
    decay = _decay_consts(n)
    mod_row = (pl.program_id(0) + 1) if latent else 0
    mod = lambda k: mod_ref[k, pl.ds(mod_row, 1), :]
    shift = mod(0)
    scale = n1_ref[...] * (1.0 + mod(1))

    def normed(q, rows):
        return _rms(q.x[rows, :], scale) + shift

    wide = min(n, WIDE_TILE)
    a_tile = wide if latent else ROW_TILE

    def stacked(parts):
        return parts[0] if len(parts) == 1 else jnp.concatenate(parts, axis=0)

    def store_retention(q, r0, rows, pqk, pvg):
        q.sg[rows, :] = _silu(pvg[:, RET_W:])
        tpos = (r0 + lax.broadcasted_iota(jnp.int32, (pqk.shape[0], DK), 0)).astype(_F32)
        if latent:
            cos2 = cos_ref[rows, :]
            sin2 = sin_ref[rows, :]
        for hd in range(RET_H):
            cols = slice(hd * DK, (hd + 1) * DK)
            qh = pqk[:, cols] * (DK ** -0.5)
            kh = pqk[:, RET_W + hd * DK:RET_W + (hd + 1) * DK]
            if latent:
                qh = qh * cos2 + pltpu.roll(qh, DK // 2, axis=1) * sin2
                kh = kh * cos2 + pltpu.roll(kh, DK // 2, axis=1) * sin2
            lgf = decay[hd][DEC_LOG_F]
            lgb = decay[hd][DEC_LOG_B]
            q.q_f[hd, rows, :] = qh * jnp.exp(tpos * lgf)
            q.k_f[hd, rows, :] = kh * jnp.exp(tpos * (-lgf))
            q.q_b[hd, rows, :] = qh * jnp.exp(tpos * (-lgb))
            q.k_b[hd, rows, :] = kh * jnp.exp(tpos * lgb)
        q.v[rows, :] = pvg[:, 0:RET_W]

    def phase_a1(r, carry):
        r0 = pl.multiple_of(r * a_tile, a_tile)
        rows = pl.ds(r0, a_tile)
        h = stacked([normed(q, rows) for q in seqs])
        p_all = _dot(h, wl_ref[...])
        if early_ret:
            pqk_all = _dot(h, wqk_ref[...])
            pvg_all = _dot(h, wvg_ref[...])
        for s, q in enumerate(seqs):
            part = slice(s * a_tile, (s + 1) * a_tile)
            q.xlp[pl.ds(r0 + SUBLANES, a_tile), :] = p_all[part, 0:LRU_W]
            q.gy[rows, :] = _gelu_tanh(p_all[part, LRU_W:])
            if early_ret:
                store_retention(q, r0, rows, pqk_all[part], pvg_all[part])
        return carry

    for q in seqs:
        q.xlp[0:SUBLANES, :] = jnp.zeros((SUBLANES, LRU_W), _F32)
        q.xlp[n + SUBLANES:n + 2 * SUBLANES, :] = jnp.zeros((SUBLANES, LRU_W), _F32)

    half = LRU_W // 2

    def softplus_neg(lam):
        z = -lam
        return jnp.maximum(z, 0.0) + jnp.log1p(jnp.exp(-jnp.abs(z)))

    sp = (softplus_neg(lam_ref[0:1, :]), softplus_neg(lam_ref[1:2, :]))

    def phase_b(r, carry):
        r0 = pl.multiple_of(r * a_tile, a_tile)
        rows = pl.ds(r0, a_tile)
        xcs = []
        for q in seqs:
            ext = q.xlp[pl.ds(r0, a_tile + 2 * SUBLANES), :]
            xc = cb_ref[...]
            for tap in range(4):
                o = SUBLANES - 2 + tap
                xc = xc + ext[o:o + a_tile, :] * cw_ref[tap:tap + 1, :]
            xcs.append(xc)
        xhs = [0.5 * xc for xc in xcs]
        for d in range(2):
            bah = 0.5 * ba_ref[d:d + 1, :]
            bih = 0.5 * bi_ref[d:d + 1, :]
            ch = (-0.5 * LRU_C) * sp[d]
            for hh in range(2):
                cs = slice(hh * half, (hh + 1) * half)
                pre_all = _dot(stacked([xc[:, cs] for xc in xcs]), wg_ref[d, hh])
                for s, q in enumerate(seqs):
                    a_ref, u_ref = ((q.a_f, q.h_f), (q.a_b, q.h_b))[d]
                    pre = pre_all[s * a_tile:(s + 1) * a_tile]
                    xh = xhs[s][:, cs]
                    t_r = jnp.tanh(pre[:, 0:half] + bah[:, cs])
                    t_i = jnp.tanh(pre[:, half:] + bih[:, cs])
                    log_a = t_r * ch[:, cs] + ch[:, cs]
                    a = jnp.exp(log_a)
                    om = -jnp.tanh(log_a) * (a * a + 1.0)
                    root = jnp.where(om > 0.0, om * lax.rsqrt(om), 0.0)
                    u = root * (t_i * xh + xh)
                    for j in range(half // LANES):
                        lanes = slice(j * LANES, (j + 1) * LANES)
                        a_ref[hh * (half // LANES) + j, rows, :] = a[:, lanes]
                        u_ref[hh * (half // LANES) + j, rows, :] = u[:, lanes]
        return carry

    _tile_loop(n // a_tile, phase_a1)
    _tile_loop(n // a_tile, phase_b)

    row8 = lax.broadcasted_iota(jnp.int32, (SUBLANES, LANES), 0)
    block = SUBLANES * SUBLANES
    n_blocks = n // block
    n_slabs = LRU_W // LANES

    def across_groups(a, b, reverse):
        for s in (1, 2, 4):
            m = (row8 < SUBLANES - s) if reverse else (row8 >= s)
            shift = SUBLANES - s if reverse else s
            a_s = jnp.where(m, pltpu.roll(a, shift, axis=0), 1.0)
            b_s = jnp.where(m, pltpu.roll(b, shift, axis=0), 0.0)
            b = a * b_s + b
            a = a * a_s
        return a, b

    def scan_block(a_ref, h_ref, base, carry, reverse):
        rows = [pl.ds(base + k, SUBLANES, stride=SUBLANES) for k in range(SUBLANES)]
        order = list(reversed(range(SUBLANES))) if reverse else list(range(SUBLANES))
        prod, local = {}, {}
        prev = None
        for k in order:
            a, u = a_ref[rows[k], :], h_ref[rows[k], :]
            prod[k] = a if prev is None else a * prod[prev]
            local[k] = u if prev is None else a * local[prev] + u
            prev = k
        p_all, h_all = across_groups(prod[prev], local[prev], reverse)
        inner = (row8 < SUBLANES - 1) if reverse else (row8 >= 1)
        shift = SUBLANES - 1 if reverse else 1
        enter = (jnp.where(inner, pltpu.roll(p_all, shift, axis=0), 1.0) * carry
                 + jnp.where(inner, pltpu.roll(h_all, shift, axis=0), 0.0))
        for k in order:
            h_ref[rows[k], :] = prod[k] * enter + local[k]
        leave = p_all * carry + h_all
        return leave[0:1, :] if reverse else leave[SUBLANES - 1:SUBLANES, :]

    def initial(q, d, s):
        if latent:
            return q.h0[d:d + 1, s * LANES:(s + 1) * LANES]
        return jnp.zeros((1, LANES), _F32)

    chains = [(q, s) for q in seqs for s in range(n_slabs)]

    def scan_body(i, carry):
        fwd_base = pl.multiple_of(i * block, block)
        bwd_base = pl.multiple_of((n_blocks - 1 - i) * block, block)
        out = []
        for (q, s), (cf, cb) in zip(chains, carry):
            out.append((scan_block(q.a_f.at[s], q.h_f.at[s], fwd_base, cf, False),
                        scan_block(q.a_b.at[s], q.h_b.at[s], bwd_base, cb, True)))
        return tuple(out)

    ends = lax.fori_loop(0, n_blocks, scan_body,
                         tuple((initial(q, 0, s), initial(q, 1, s)) for q, s in chains), unroll=2)

    def lru_out(q):
        def body(r, carry):
            rows = pl.ds(pl.multiple_of(r * ROW_TILE, ROW_TILE), ROW_TILE)
            for s in range(n_slabs):
                lanes = slice(s * LANES, (s + 1) * LANES)
                q.mix[rows, lanes] = (q.h_f[s, rows, :] + q.h_b[s, rows, :]) * q.gy[rows, lanes]
            return carry
        return body

    for (q, s), (last_f, first_b) in zip(chains, ends):
        if not latent:
            q.stl[0:1, s * LANES:(s + 1) * LANES] = last_f
            q.stl[1:2, s * LANES:(s + 1) * LANES] = first_b
    for q in seqs:
        _tile_loop(nt, lru_out(q))

    def phase_a2(r, carry):
        r0 = pl.multiple_of(r * wide, wide)
        rows = pl.ds(r0, wide)
        h = stacked([normed(q, rows) for q in seqs])
        pqk_all = _dot(h, wqk_ref[...])
        pvg_all = _dot(h, wvg_ref[...])
        for s, q in enumerate(seqs):
            part = slice(s * wide, (s + 1) * wide)
            store_retention(q, r0, rows, pqk_all[part], pvg_all[part])
        return carry

    if not early_ret:
        @pl.when(first_step)
        def _():
            late_ret.wait()

        _tile_loop(n // wide, phase_a2)

    lower = (lax.broadcasted_iota(jnp.int32, (ROW_TILE, ROW_TILE), 0)
             >= lax.broadcasted_iota(jnp.int32, (ROW_TILE, ROW_TILE), 1))
    blocks = [slice(r * ROW_TILE, (r + 1) * ROW_TILE) for r in range(nt)]

    def phase_c(q, hd):
        cols = slice(hd * DK, (hd + 1) * DK)
        dec = lambda row: decay[hd][row]
        kv_f = [_dot_tn(q.k_f[hd, rows, :], q.v[rows, cols]) if (r < nt - 1 or not latent) else 0.0
                for r, rows in enumerate(blocks)]
        kv_b = [_dot_tn(q.k_b[hd, rows, :], q.v[rows, cols]) if (r > 0 or not latent) else 0.0
                for r, rows in enumerate(blocks)]
        if latent:
            run_f = q.s0[0, hd] * dec(DEC_G_F)
            run_b = q.s0[1, hd] * dec(DEC_GN_B)
        else:
            run_f = run_b = None
        before = []
        for r in range(nt):
            before.append(run_f)
            run_f = kv_f[r] if run_f is None else run_f + kv_f[r]
        after = [None] * nt
        for r in reversed(range(nt)):
            after[r] = run_b
            run_b = kv_b[r] if run_b is None else run_b + kv_b[r]

        for r, rows in enumerate(blocks):
            qf = q.q_f[hd, rows, :]
            qb = q.q_b[hd, rows, :]
            s = jnp.where(lower, _dot_nt(qf, q.k_f[hd, rows, :]), _dot_nt(qb, q.k_b[hd, rows, :]))
            o = _dot(s, q.v[rows, cols])
            if before[r] is not None:
                o = o + _dot(qf, before[r])
            if after[r] is not None:
                o = o + _dot(qb, after[r])
            o = o * lax.rsqrt(jnp.mean(o * o, axis=-1, keepdims=True) + EPS)
            q.mix[rows, LRU_W + cols.start:LRU_W + cols.stop] = o * q.sg[rows, cols]
        if not latent:
            q.str[0, hd] = run_f * dec(DEC_GN1_F)
            q.str[1, hd] = run_b

    for hd in range(RET_H):
        for q in seqs:
            phase_c(q, hd)

    gain1 = mod(2) * n2_ref[...]
    gain2 = n3_ref[...] * (1.0 + mod(4))
    sh2 = mod(3)

    def phase_d(r, carry):
        rows = pl.ds(pl.multiple_of(r * wide, wide), wide)
        mix_all = _dot(stacked([q.mix[rows, :] for q in seqs]), wout_ref[...])
        h2s = []
        for s, q in enumerate(seqs):
            x1 = q.x[rows, :] + _rms(mix_all[s * wide:(s + 1) * wide], gain1)
            q.x1[rows, :] = x1
            h2s.append(_rms(x1, gain2) + sh2)
            q.h2[rows, :] = h2s[-1].astype(_BF)
        logits_t = _dot(stacked(h2s), rw_ref[...]).T
        for s, q in enumerate(seqs):
            q.lt[:, rows] = logits_t[0:N_EXP, s * wide:(s + 1) * wide]
        return carry

    if latent:
        @pl.when(first_step)
        def _():
            late_out.wait()

    _tile_loop(n // wide, phase_d, unroll=2)


def _mixer_call(x, mod, norms, w_in, lru, wg, w_out, rw, latent, extra=(), layer=0):
    b, n, _ = x.shape
    g = 1 if latent else CONTEXT_GROUP
    const2 = lambda i: (0, 0)
    whole = pl.BlockSpec(memory_space=pl.ANY)
    in_specs = [
        pl.BlockSpec((g, n, D), lambda i: (i, 0, 0)),
        pl.BlockSpec((N_MOD, SUBLANES, D), lambda i: (0, 0, 0)),
        pl.BlockSpec((1, D), const2), pl.BlockSpec((1, D), const2), pl.BlockSpec((1, D), const2),
        pl.BlockSpec((D, D), lambda i: (0, 0)),
    ]
    if latent:
        in_specs += [whole]
    else:
        in_specs += [pl.BlockSpec((D, D), lambda i: (0, 1)),
                     pl.BlockSpec((D, D), lambda i: (0, 2))]
    in_specs += [
        pl.BlockSpec((4, LRU_W), const2), pl.BlockSpec((1, LRU_W), const2),
        pl.BlockSpec((2, LRU_W), const2), pl.BlockSpec((2, LRU_W), const2),
        pl.BlockSpec((2, LRU_W), const2),
        pl.BlockSpec((2, 2, LRU_W // 2, LRU_W), lambda i: (0, 0, 0, 0)),
        whole if latent else pl.BlockSpec((D, D), const2),
        pl.BlockSpec((D, LANES), const2),
    ]
    out_shape = [
        jax.ShapeDtypeStruct((b, n, D), _F32),
        jax.ShapeDtypeStruct((b, n, D), _BF),
        jax.ShapeDtypeStruct((b, N_EXP, n), _F32),
    ]
    out_specs = [
        pl.BlockSpec((g, n, D), lambda i: (i, 0, 0)),
        pl.BlockSpec((g, n, D), lambda i: (i, 0, 0)),
        pl.BlockSpec((g, N_EXP, n), lambda i: (i, 0, 0)),
    ]
    if latent:
        in_specs += [
            pl.BlockSpec((g, None, 2, LRU_W), lambda i: (i, layer, 0, 0)),
            pl.BlockSpec((g, None, 2, RET_H, DK, DK), lambda i: (i, layer, 0, 0, 0, 0)),
            pl.BlockSpec((n, DK), const2),
            pl.BlockSpec((n, DK), const2),
        ]
    else:
        out_shape += [
            jax.ShapeDtypeStruct((b, 1, 2, LRU_W), _F32),
            jax.ShapeDtypeStruct((b, 1, 2, RET_H, DK, DK), _F32),
        ]
        out_specs += [
            pl.BlockSpec((g, None, 2, LRU_W), lambda i: (i, 0, 0, 0)),
            pl.BlockSpec((g, None, 2, RET_H, DK, DK), lambda i: (i, 0, 0, 0, 0, 0)),
        ]
    f32s = lambda shape: pltpu.VMEM((g,) + shape, _F32)
    slabs = (LRU_W // LANES, n, LANES)
    scratch = [
        f32s((n + 2 * SUBLANES, LRU_W)),
        f32s((n, LRU_W)),
        f32s(slabs), f32s(slabs),
        f32s(slabs), f32s(slabs),
    ]
    if latent:
        scratch += [pltpu.VMEM((D, 2 * D), _F32), pltpu.VMEM((D, D), _F32),
                    pltpu.SemaphoreType.DMA((2,))]
        w_in_args = (w_in, w_in)
    else:
        scratch += [f32s((n, RET_W))] + [f32s(slabs) for _ in range(4)] + [f32s((n, RET_W))]
        w_in_args = (w_in, w_in, w_in)
    return pl.pallas_call(
        functools.partial(_mixer_kernel, n=n, g=g, latent=latent),
        out_shape=out_shape,
        grid=(b // g,),
        in_specs=in_specs,
        out_specs=out_specs,
        scratch_shapes=scratch,
        compiler_params=pltpu.CompilerParams(
            dimension_semantics=("arbitrary",), vmem_limit_bytes=VMEM_LIMIT),
        name="mixer_latent" if latent else "mixer_context",
    )(x, mod, *norms, *w_in_args, *lru, wg, w_out, rw, *extra)


def _count(mask):
    return jnp.sum(jnp.where(mask, 1.0, 0.0), axis=-1, keepdims=True)


def _probs(l3):
    bsz, _, n = l3.shape
    m = jnp.max(l3, axis=1, keepdims=True)
    e = jnp.exp(l3 - m)
    return (e / jnp.sum(e, axis=1, keepdims=True)).reshape(bsz * N_EXP, n)


def _break_ties(parts):
    prep = []
    for bits, thr, cap in parts:
        rows, n = bits.shape
        eq = bits == thr
        need = float(cap) - _count(bits > thr)
        idx = lax.broadcasted_iota(jnp.int32, (rows, n), 1)
        prep.append((eq, need, idx, int(math.log2(n))))
    most = max(nbits for _, _, _, nbits in prep)

    def idx_body(i, last):
        out = []
        for (eq, need, idx, nbits), j in zip(prep, last):
            shift = nbits - 1 - i
            cand = j | jnp.where(shift >= 0, jnp.int32(1) << jnp.maximum(shift, 0), 0)
            out.append(jnp.where(_count(eq & (idx < cand)) < need, cand, j))
        return tuple(out)

    return lax.fori_loop(0, most, idx_body,
                         tuple(jnp.zeros((eq.shape[0], 1), jnp.int32) for eq, _, _, _ in prep))


def _slots(p, bits, thr, jlast):
    rows, n = p.shape
    idx = lax.broadcasted_iota(jnp.int32, (rows, n), 1)
    sel = (bits > thr) | ((bits == thr) & (idx <= jlast))
    before = (lax.broadcasted_iota(jnp.int32, (n, n), 0)
              < lax.broadcasted_iota(jnp.int32, (n, n), 1))
    pos = _dot(jnp.where(sel, 1.0, 0.0).astype(_BF), jnp.where(before, 1.0, 0.0).astype(_BF))
    return jnp.where(sel, pos, -1.0), jnp.where(sel, p, 0.0)


def _route_kernel(lp_ref, ls_ref, pp_ref, gp_ref, ps_ref, gs_ref, *, cap_p, cap_s):
    groups = ((_probs(lp_ref[...]), float(cap_p)), (_probs(ls_ref[...]), float(cap_s)))
    bits = [pltpu.bitcast(p, jnp.int32) for p, _ in groups]

    def settle(b, t, capf, hi, lo):
        with_hi = t | hi
        both = with_hi | lo
        with_lo = t | lo
        ok = lambda cand: _count(b >= cand) >= capf
        return jnp.where(ok(with_hi), jnp.where(ok(both), both, with_hi),
                         jnp.where(ok(with_lo), with_lo, t))

    def val_body(i, thr):
        hi = jnp.int32(1) << (29 - 2 * i)
        lo = jnp.int32(1) << (28 - 2 * i)
        return tuple(settle(b, t, capf, hi, lo) for b, t, (_, capf) in zip(bits, thr, groups))

    thr = lax.fori_loop(0, 15, val_body,
                        tuple(jnp.zeros((b.shape[0], 1), jnp.int32) for b in bits))
    jlast = _break_ties([(bits[0], thr[0], cap_p), (bits[1], thr[1], cap_s)])
    pos, gate = _slots(groups[0][0], bits[0], thr[0], jlast[0])
    pp_ref[...] = pos.reshape(pp_ref.shape)
    gp_ref[...] = gate.reshape(gp_ref.shape)
    pos, gate = _slots(groups[1][0], bits[1], thr[1], jlast[1])
    ps_ref[...] = pos.reshape(ps_ref.shape)
    gs_ref[...] = gate.reshape(gs_ref.shape)


def _route_call(lt_p, lt_s, cap_p, cap_s):
    shapes = [
        jax.ShapeDtypeStruct(lt_p.shape, _F32), jax.ShapeDtypeStruct(lt_p.shape, _F32),
        jax.ShapeDtypeStruct(lt_s.shape, _F32), jax.ShapeDtypeStruct(lt_s.shape, _F32),
    ]
    return pl.pallas_call(
        functools.partial(_route_kernel, cap_p=cap_p, cap_s=cap_s),
        out_shape=shapes,
        compiler_params=pltpu.CompilerParams(vmem_limit_bytes=VMEM_LIMIT),
        name="route_select",
    )(lt_p, lt_s)


def _dispatch_kernel(pos_ref, gate_ref, h_ref, xs_ref, gs_ref, *, n, cap, g):
    slot = lax.broadcasted_iota(jnp.int32, (cap, n), 0).astype(_F32)
    for j in range(g):
        slots = slice(j * cap, (j + 1) * cap)
        parts = []
        for e in range(N_EXP):
            hit = pos_ref[j, e:e + 1, :] == slot
            parts.append(jnp.where(hit, 1.0, 0.0).astype(_BF))
            gs_ref[e, slots, :] = jnp.sum(jnp.where(hit, gate_ref[j, e:e + 1, :], 0.0),
                                          axis=-1, keepdims=True)
        onehot = jnp.concatenate(parts, axis=0)
        xs = _dot(onehot, h_ref[j]).astype(_BF)
        for e in range(N_EXP):
            xs_ref[e, slots, :] = xs[e * cap:(e + 1) * cap, :]


def _dispatch_call(pos, gate, h2, cap):
    b, n, _ = h2.shape
    g = max(1, SMALL_STEP_TOKENS // n)
    return pl.pallas_call(
        functools.partial(_dispatch_kernel, n=n, cap=cap, g=g),
        out_shape=[
            jax.ShapeDtypeStruct((N_EXP, b * cap, D), _BF),
            jax.ShapeDtypeStruct((N_EXP, b * cap, 1), _F32),
        ],
        grid=(b // g,),
        in_specs=[
            pl.BlockSpec((g, N_EXP, n), lambda i: (i, 0, 0)),
            pl.BlockSpec((g, N_EXP, n), lambda i: (i, 0, 0)),
            pl.BlockSpec((g, n, D), lambda i: (i, 0, 0)),
        ],
        out_specs=[
            pl.BlockSpec((N_EXP, g * cap, D), lambda i: (0, i, 0)),
            pl.BlockSpec((N_EXP, g * cap, 1), lambda i: (0, i, 0)),
        ],
        compiler_params=pltpu.CompilerParams(
            dimension_semantics=("arbitrary",), vmem_limit_bytes=VMEM_LIMIT),
        name="dispatch",
    )(pos, gate, h2)


def _expert_kernel(xp_ref, xs_ref, gp_ref, gs_ref, wg_ref, wu_ref, wd_ref, y_ref, xcat, acc,
                   *, sp, nf, tf):
    f = pl.program_id(1)
    xcat[0:sp, :] = xp_ref[...]
    xcat[sp:, :] = xs_ref[...]
    x = xcat[...]
    total = jnp.where(f == 0, 0.0, acc[...])
    for c in range(tf // FF_CHUNK):
        cs = slice(c * FF_CHUNK, (c + 1) * FF_CHUNK)
        hg = _dot(x, wg_ref[:, cs].astype(_BF))
        hu = _dot(x, wu_ref[:, cs].astype(_BF))
        hid = (_silu(hg) * hu).astype(_BF)
        total = total + _dot(hid, wd_ref[cs, :].astype(_BF))
    acc[...] = total
    y_ref[0:sp, :] = (total[0:sp, :] * gp_ref[...]).astype(_BF)
    y_ref[sp:, :] = (total[sp:, :] * gs_ref[...]).astype(_BF)


def _expert_call(xs_p, xs_s, g_p, g_s, w_gate, w_up, w_down):
    tf = 1024
    sp = xs_p.shape[1]
    ss = xs_s.shape[1]
    nf = FF // tf
    return pl.pallas_call(
        functools.partial(_expert_kernel, sp=sp, nf=nf, tf=tf),
        out_shape=jax.ShapeDtypeStruct((N_EXP, sp + ss, D), _BF),
        grid=(N_EXP, nf),
        in_specs=[
            pl.BlockSpec((None, sp, D), lambda e, f: (e, 0, 0)),
            pl.BlockSpec((None, ss, D), lambda e, f: (e, 0, 0)),
            pl.BlockSpec((None, sp, 1), lambda e, f: (e, 0, 0)),
            pl.BlockSpec((None, ss, 1), lambda e, f: (e, 0, 0)),
            pl.BlockSpec((None, D, tf), lambda e, f: (e, 0, f)),
            pl.BlockSpec((None, D, tf), lambda e, f: (e, 0, f)),
            pl.BlockSpec((None, tf, D), lambda e, f: (e, f, 0)),
        ],
        out_specs=pl.BlockSpec((None, sp + ss, D), lambda e, f: (e, 0, 0)),
        scratch_shapes=[pltpu.VMEM((sp + ss, D), _BF), pltpu.VMEM((sp + ss, D), _F32)],
        compiler_params=pltpu.CompilerParams(
            dimension_semantics=("arbitrary", "arbitrary"), vmem_limit_bytes=VMEM_LIMIT),
        name="expert_ffn",
    )(xs_p, xs_s, g_p, g_s, w_gate, w_up, w_down)


def _combine_kernel(pos_ref, ye_ref, x1_ref, mod_ref, n4_ref, y_ref, *, n, cap, g, latent):
    width = N_EXP * cap
    lane = lax.broadcasted_iota(jnp.int32, (N_EXP, width), 1)
    expand = jnp.where(lane // cap == lax.broadcasted_iota(jnp.int32, (N_EXP, width), 0),
                       1.0, 0.0).astype(_BF)
    tile = min(n, COMBINE_TILE)
    slot = (lax.broadcasted_iota(jnp.int32, (tile, width), 1) % cap).astype(_F32)
    mod_row = (pl.program_id(0) + 1) if latent else 0
    gain = mod_ref[N_MOD - 1, pl.ds(mod_row, 1), :] * n4_ref[...]

    for j in range(g):
        ye = ye_ref[:, j * cap:(j + 1) * cap, :].reshape(width, D)

        def body(r, carry, j=j, ye=ye):
            rows = pl.ds(pl.multiple_of(r * tile, tile), tile)
            pos_e = _dot_tn(pos_ref[j, :, rows].astype(_BF), expand)
            onehot = jnp.where(pos_e == slot, 1.0, 0.0).astype(_BF)
            f = _dot(onehot, ye)
            y_ref[j, rows, :] = x1_ref[j, rows, :] + _rms(f, gain)
            return carry

        _tile_loop(n // tile, body)


def _combine_call(pos, ye, x1, mod, norm_post, cap, slot_off, latent):
    b, n, _ = x1.shape
    g = max(1, SMALL_STEP_TOKENS // n)
    assert g == 1 or not latent
    blk_off = slot_off // (g * cap)
    return pl.pallas_call(
        functools.partial(_combine_kernel, n=n, cap=cap, g=g, latent=latent),
        out_shape=jax.ShapeDtypeStruct((b, n, D), _F32),
        grid=(b // g,),
        in_specs=[
            pl.BlockSpec((g, N_EXP, n), lambda i: (i, 0, 0)),
            pl.BlockSpec((N_EXP, g * cap, D), lambda i: (0, i + blk_off, 0)),
            pl.BlockSpec((g, n, D), lambda i: (i, 0, 0)),
            pl.BlockSpec((N_MOD, SUBLANES, D), lambda i: (0, 0, 0)),
            pl.BlockSpec((1, D), lambda i: (0, 0)),
        ],
        out_specs=pl.BlockSpec((g, n, D), lambda i: (i, 0, 0)),
        compiler_params=pltpu.CompilerParams(
            dimension_semantics=("arbitrary",), vmem_limit_bytes=VMEM_LIMIT),
        name="combine_latent" if latent else "combine_context",
    )(pos, ye, x1, mod, norm_post)


def _block_diag_gates(wa, wi):
    per_half = LRU_HEADS // 2
    side = per_half * LRU_HD
    on_diag = (np.arange(side)[:, None] // LRU_HD) == (np.arange(side)[None, :] // LRU_HD)

    def bd(w):
        rows = w.reshape(2, 2, side, LRU_HD)
        return jnp.where(on_diag, jnp.tile(rows, (1, 1, 1, per_half)), 0.0)

    return 0.5 * jnp.concatenate([bd(wa), bd(wi)], axis=-1)


def _rope_tables(n):
    rows = n // GRID_W
    row = np.repeat(np.arange(rows, dtype=np.float32), GRID_W)
    col = np.tile(np.arange(GRID_W, dtype=np.float32), rows)
    nf = DK // 4
    freqs = np.float32(ROPE_BASE) ** (-np.arange(nf, dtype=np.float32) / np.float32(nf))
    ang = np.concatenate([row[:, None] * freqs, col[:, None] * freqs], axis=-1).astype(np.float32)
    cos = np.cos(ang)
    sin = np.sin(ang)
    return (jnp.asarray(np.concatenate([cos, cos], axis=-1), _F32),
            jnp.asarray(np.concatenate([-sin, sin], axis=-1), _F32))


def _decay_consts(n):
    heads = np.arange(RET_H, dtype=np.float32)
    f32 = np.float32
    lgf = np.log1p(-np.exp2(-(f32(RET_DECAY_OFFSET_FWD) + heads))).astype(f32)
    lgb = np.log1p(-np.exp2(-(f32(RET_DECAY_OFFSET_BWD) + heads))).astype(f32)
    tab = np.stack([lgf, lgb, np.exp(lgf), np.exp(f32(n) * lgb), np.exp(f32(n - 1) * lgf)], axis=1)
    return [[float(v) for v in row] for row in tab.astype(f32)]


def kernel(x_prompt, x_sample, c, state_lru, state_ret, c_ctx, ada_w, ada_b, norm_mix_pre, norm_mix_post, norm_ffn_pre, norm_ffn_post, w_in, conv_w, conv_b, lru_wa, lru_ba, lru_wi, lru_bi, lru_lambda, w_out, router_w, exp_w_gate, exp_w_up, exp_w_down):
    bp, n_p, _ = x_prompt.shape
    bs, n_s, _ = x_sample.shape
    cap_p = 2 * n_p // N_EXP
    cap_s = 2 * n_s // N_EXP
    l = 0

    mod = _ada_call(c_ctx[None, :], c, ada_w[l], ada_b[l][None, :])

    norms = (norm_mix_pre[l][None], norm_mix_post[l][None], norm_ffn_pre[l][None])
    lru = (conv_w[l], conv_b[l][None], lru_ba[l], lru_bi[l], lru_lambda[l])
    wg = _block_diag_gates(lru_wa[l], lru_wi[l])
    rw = jnp.pad(router_w[l], ((0, 0), (0, LANES - N_EXP)))
    cos2, sin2 = _rope_tables(n_s)

    x1_p, h2_p, lt_p, st_lru, st_ret = _mixer_call(
        x_prompt, mod, norms, w_in[l], lru, wg, w_out[l], rw, latent=False)
    x1_s, h2_s, lt_s = _mixer_call(
        x_sample, mod, norms, w_in[l], lru, wg, w_out[l], rw, latent=True,
        extra=(state_lru, state_ret, cos2, sin2), layer=l)

    pos_p, gate_p, pos_s, gate_s = _route_call(lt_p, lt_s, cap_p, cap_s)
    xs_p, gsl_p = _dispatch_call(pos_p, gate_p, h2_p, cap_p)
    xs_s, gsl_s = _dispatch_call(pos_s, gate_s, h2_s, cap_s)
    ye = _expert_call(xs_p, xs_s, gsl_p, gsl_s, exp_w_gate[l], exp_w_up[l], exp_w_down[l])

    norm_post = norm_ffn_post[l][None]
    y_p = _combine_call(pos_p, ye, x1_p, mod, norm_post, cap_p, 0, latent=False)
    y_s = _combine_call(pos_s, ye, x1_s, mod, norm_post, cap_s, bp * cap_p, latent=True)
    return (y_p, y_s, st_lru, st_ret)
```

```python
import functools
import math
import types

import jax
import jax.numpy as jnp
import numpy as np
from jax import lax
from jax.experimental import pallas as pl
from jax.experimental.pallas import tpu as pltpu

D = 1024
LRU_W = 512
LRU_HEADS = 8
LRU_HD = 64
LRU_C = 8.0
RET_W = 512
RET_H = 4
DK = 128
N_EXP = 16
FF = 2048
N_MOD = 6
EPS = 1e-6
GRID_W = 64
ROPE_BASE = 10000.0
RET_DECAY_OFFSET_FWD = 5.0
RET_DECAY_OFFSET_BWD = 5.5

ROW_TILE = 256
WIDE_TILE = 512
FF_CHUNK = 512
COMBINE_TILE = 1024
CONTEXT_GROUP = 2
SMALL_STEP_TOKENS = 1024
SUBLANES = 8
LANES = 128
VMEM_LIMIT = 60 * 1024 * 1024

DEC_LOG_F, DEC_LOG_B, DEC_G_F, DEC_GN_B, DEC_GN1_F = 0, 1, 2, 3, 4

_BF = jnp.bfloat16
_F32 = jnp.float32


def _sigmoid(x):
    return 0.5 * jnp.tanh(0.5 * x) + 0.5


def _silu(x):
    return x * _sigmoid(x)


def _gelu_tanh(x):
    c = math.sqrt(2.0 / math.pi)
    return 0.5 * x * (1.0 + jnp.tanh(c * (x + 0.044715 * (x * x * x))))


def _rms(x, gain):
    return x * lax.rsqrt(jnp.mean(x * x, axis=-1, keepdims=True) + EPS) * gain


def _dot(a, b):
    return jnp.dot(a, b, preferred_element_type=_F32)


def _dot_nt(a, b):
    return lax.dot_general(a, b, (((1,), (1,)), ((), ())), preferred_element_type=_F32)


def _dot_tn(a, b):
    return lax.dot_general(a, b, (((0,), (0,)), ((), ())), preferred_element_type=_F32)


def _ada_kernel(cc_ref, c_ref, w_ref, b_ref, o_ref, s_ref):
    nb = c_ref.shape[0]
    s_ref[...] = jnp.zeros_like(s_ref)
    s_ref[0:1, :] = _silu(cc_ref[...])
    s_ref[1:1 + nb, :] = _silu(c_ref[...])
    o_ref[...] = _dot(s_ref[...], w_ref[...]) + b_ref[...]


def _ada_call(c_ctx, c, ada_w, ada_b):
    nb = c.shape[0]
    assert nb + 1 <= SUBLANES
    return pl.pallas_call(
        _ada_kernel,
        out_shape=jax.ShapeDtypeStruct((N_MOD, SUBLANES, D), _F32),
        grid=(N_MOD,),
        in_specs=[
            pl.BlockSpec((1, D), lambda j: (0, 0)),
            pl.BlockSpec((nb, D), lambda j: (0, 0)),
            pl.BlockSpec((D, D), lambda j: (0, j)),
            pl.BlockSpec((1, D), lambda j: (0, j)),
        ],
        out_specs=pl.BlockSpec((None, SUBLANES, D), lambda j: (j, 0, 0)),
        scratch_shapes=[pltpu.VMEM((SUBLANES, D), _F32)],
        compiler_params=pltpu.CompilerParams(
            dimension_semantics=("arbitrary",), vmem_limit_bytes=VMEM_LIMIT),
        name="ada_mod",
    )(c_ctx, c, ada_w, ada_b)


def _tile_loop(nt, body, unroll=1):
    if nt == 1:
        body(0, 0)
    else:
        lax.fori_loop(0, nt, body, 0, unroll=unroll)


def _mixer_kernel(*refs, n, g, latent):
    if latent:
        (x_ref, mod_ref, n1_ref, n2_ref, n3_ref, wl_ref, win_hbm,
         cw_ref, cb_ref, ba_ref, bi_ref, lam_ref, wg_ref, wout_hbm, rw_ref,
         h0_ref, s0_ref, cos_ref, sin_ref,
         x1_ref, h2_ref, lt_ref,
         xlp_g, gy_g, af_g, ab_g, hf_g, hb_g, wret_v, wout_ref, late_sem) = refs
        sg_g, qf_g, qb_g, kf_g, kb_g, v_g = xlp_g, af_g, ab_g, hf_g, hb_g, gy_g
        wqk_ref, wvg_ref = wret_v.at[:, 0:D], wret_v.at[:, D:2 * D]
        first_step = pl.program_id(0) == 0
        late_ret = pltpu.make_async_copy(win_hbm.at[:, pl.ds(D, 2 * D)], wret_v, late_sem.at[0])
        late_out = pltpu.make_async_copy(wout_hbm, wout_ref, late_sem.at[1])

        @pl.when(first_step)
        def _():
            late_ret.start()
            late_out.start()
    else:
        (x_ref, mod_ref, n1_ref, n2_ref, n3_ref, wl_ref, wqk_ref, wvg_ref,
         cw_ref, cb_ref, ba_ref, bi_ref, lam_ref, wg_ref, wout_ref, rw_ref,
         x1_ref, h2_ref, lt_ref, stl_ref, str_ref,
         xlp_g, gy_g, af_g, ab_g, hf_g, hb_g, sg_g, qf_g, qb_g, kf_g, kb_g, v_g) = refs
    early_ret = not latent

    seqs = []
    for s in range(g):
        q = types.SimpleNamespace(
            x=x_ref.at[s], x1=x1_ref.at[s], h2=h2_ref.at[s], lt=lt_ref.at[s], mix=x1_ref.at[s],
            xlp=xlp_g.at[s], sg=sg_g.at[s], gy=gy_g.at[s], v=v_g.at[s],
            a_f=af_g.at[s], a_b=ab_g.at[s], h_f=hf_g.at[s], h_b=hb_g.at[s],
            q_f=qf_g.at[s], q_b=qb_g.at[s], k_f=kf_g.at[s], k_b=kb_g.at[s])
        if latent:
            q.h0, q.s0 = h0_ref.at[s], s0_ref.at[s]
        else:
            q.stl, q.str = stl_ref.at[s], str_ref.at[s]
        seqs.append(q)

    nt = n // ROW_TILE
    decay = _decay_consts(n)
    mod_row = (pl.program_id(0) + 1) if latent else 0
    mod = lambda k: mod_ref[k, pl.ds(mod_row, 1), :]
    shift = mod(0)
    scale = n1_ref[...] * (1.0 + mod(1))

    def normed(q, rows):
        return _rms(q.x[rows, :], scale) + shift

    wide = min(n, WIDE_TILE)
    a_tile = wide if latent else ROW_TILE

    def stacked(parts):
        return parts[0] if len(parts) == 1 else jnp.concatenate(parts, axis=0)

    def store_retention(q, r0, rows, pqk, pvg):
        q.sg[rows, :] = _silu(pvg[:, RET_W:])
        tpos = (r0 + lax.broadcasted_iota(jnp.int32, (pqk.shape[0], DK), 0)).astype(_F32)
        if latent:
            cos2 = cos_ref[rows, :]
            sin2 = sin_ref[rows, :]
        for hd in range(RET_H):
            cols = slice(hd * DK, (hd + 1) * DK)
            qh = pqk[:, cols] * (DK ** -0.5)
            kh = pqk[:, RET_W + hd * DK:RET_W + (hd + 1) * DK]
            if latent:
                qh = qh * cos2 + pltpu.roll(qh, DK // 2, axis=1) * sin2
                kh = kh * cos2 + pltpu.roll(kh, DK // 2, axis=1) * sin2
            lgf = decay[hd][DEC_LOG_F]
            lgb = decay[hd][DEC_LOG_B]
            q.q_f[hd, rows, :] = qh * jnp.exp(tpos * lgf)
            q.k_f[hd, rows, :] = kh * jnp.exp(tpos * (-lgf))
            q.q_b[hd, rows, :] = qh * jnp.exp(tpos * (-lgb))
            q.k_b[hd, rows, :] = kh * jnp.exp(tpos * lgb)
        q.v[rows, :] = pvg[:, 0:RET_W]

    def phase_a1(r, carry):
        r0 = pl.multiple_of(r * a_tile, a_tile)
        rows = pl.ds(r0, a_tile)
        h = stacked([normed(q, rows) for q in seqs])
        p_all = _dot(h, wl_ref[...])
        if early_ret:
            pqk_all = _dot(h, wqk_ref[...])
            pvg_all = _dot(h, wvg_ref[...])
        for s, q in enumerate(seqs):
            part = slice(s * a_tile, (s + 1) * a_tile)
            q.xlp[pl.ds(r0 + SUBLANES, a_tile), :] = p_all[part, 0:LRU_W]
            q.gy[rows, :] = _gelu_tanh(p_all[part, LRU_W:])
            if early_ret:
                store_retention(q, r0, rows, pqk_all[part], pvg_all[part])
        return carry

    for q in seqs:
        q.xlp[0:SUBLANES, :] = jnp.zeros((SUBLANES, LRU_W), _F32)
        q.xlp[n + SUBLANES:n + 2 * SUBLANES, :] = jnp.zeros((SUBLANES, LRU_W), _F32)

    half = LRU_W // 2

    def softplus_neg(lam):
        z = -lam
        return jnp.maximum(z, 0.0) + jnp.log1p(jnp.exp(-jnp.abs(z)))

    sp = (softplus_neg(lam_ref[0:1, :]), softplus_neg(lam_ref[1:2, :]))

    def phase_b(q):
        def body(r, carry):
            r0 = pl.multiple_of(r * a_tile, a_tile)
            rows = pl.ds(r0, a_tile)
            ext = q.xlp[pl.ds(r0, a_tile + 2 * SUBLANES), :]
            xc = cb_ref[...]
            for tap in range(4):
                back = (2 - tap) % (a_tile + 2 * SUBLANES)
                win = ext if back == 0 else pltpu.roll(ext, back, axis=0)
                xc = xc + win[SUBLANES:SUBLANES + a_tile, :] * cw_ref[tap:tap + 1, :]
            xh = 0.5 * xc
            for d, (a_ref, u_ref) in enumerate(((q.a_f, q.h_f), (q.a_b, q.h_b))):
                bah = 0.5 * ba_ref[d:d + 1, :]
                bih = 0.5 * bi_ref[d:d + 1, :]
                ch = (-0.5 * LRU_C) * sp[d]
                for hh in range(2):
                    cs = slice(hh * half, (hh + 1) * half)
                    pre = _dot(xc[:, cs], wg_ref[d, hh])
                    t_r = jnp.tanh(pre[:, 0:half] + bah[:, cs])
                    t_i = jnp.tanh(pre[:, half:] + bih[:, cs])
                    log_a = t_r * ch[:, cs] + ch[:, cs]
                    a = jnp.exp(log_a)
                    om = -jnp.tanh(log_a) * (a * a + 1.0)
                    root = jnp.where(om > 0.0, om * lax.rsqrt(om), 0.0)
                    u = root * (t_i * xh[:, cs] + xh[:, cs])
                    for j in range(half // LANES):
                        lanes = slice(j * LANES, (j + 1) * LANES)
                        a_ref[hh * (half // LANES) + j, rows, :] = a[:, lanes]
                        u_ref[hh * (half // LANES) + j, rows, :] = u[:, lanes]
            return carry
        return body

    _tile_loop(n // a_tile, phase_a1)
    for q in seqs:
        _tile_loop(n // a_tile, phase_b(q))

    row8 = lax.broadcasted_iota(jnp.int32, (SUBLANES, LANES), 0)
    block = SUBLANES * SUBLANES
    n_blocks = n // block
    n_slabs = LRU_W // LANES

    def across_groups(a, b, reverse):
        for s in (1, 2, 4):
            m = (row8 < SUBLANES - s) if reverse else (row8 >= s)
            shift = SUBLANES - s if reverse else s
            a_s = jnp.where(m, pltpu.roll(a, shift, axis=0), 1.0)
            b_s = jnp.where(m, pltpu.roll(b, shift, axis=0), 0.0)
            b = a * b_s + b
            a = a * a_s
        return a, b

    def scan_block(a_ref, h_ref, base, carry, reverse):
        rows = [pl.ds(base + k, SUBLANES, stride=SUBLANES) for k in range(SUBLANES)]
        order = list(reversed(range(SUBLANES))) if reverse else list(range(SUBLANES))
        prod, local = {}, {}
        prev = None
        for k in order:
            a, u = a_ref[rows[k], :], h_ref[rows[k], :]
            prod[k] = a if prev is None else a * prod[prev]
            local[k] = u if prev is None else a * local[prev] + u
            prev = k
        p_all, h_all = across_groups(prod[prev], local[prev], reverse)
        inner = (row8 < SUBLANES - 1) if reverse else (row8 >= 1)
        shift = SUBLANES - 1 if reverse else 1
        enter = (jnp.where(inner, pltpu.roll(p_all, shift, axis=0), 1.0) * carry
                 + jnp.where(inner, pltpu.roll(h_all, shift, axis=0), 0.0))
        for k in order:
            h_ref[rows[k], :] = prod[k] * enter + local[k]
        leave = p_all * carry + h_all
        return leave[0:1, :] if reverse else leave[SUBLANES - 1:SUBLANES, :]

    def initial(q, d, s):
        if latent:
            return q.h0[d:d + 1, s * LANES:(s + 1) * LANES]
        return jnp.zeros((1, LANES), _F32)

    chains = [(q, s) for q in seqs for s in range(n_slabs)]

    def scan_body(i, carry):
        fwd_base = pl.multiple_of(i * block, block)
        bwd_base = pl.multiple_of((n_blocks - 1 - i) * block, block)
        out = []
        for (q, s), (cf, cb) in zip(chains, carry):
            out.append((scan_block(q.a_f.at[s], q.h_f.at[s], fwd_base, cf, False),
                        scan_block(q.a_b.at[s], q.h_b.at[s], bwd_base, cb, True)))
        return tuple(out)

    ends = lax.fori_loop(0, n_blocks, scan_body,
                         tuple((initial(q, 0, s), initial(q, 1, s)) for q, s in chains), unroll=2)

    def lru_out(q):
        def body(r, carry):
            rows = pl.ds(pl.multiple_of(r * ROW_TILE, ROW_TILE), ROW_TILE)
            for s in range(n_slabs):
                lanes = slice(s * LANES, (s + 1) * LANES)
                q.mix[rows, lanes] = (q.h_f[s, rows, :] + q.h_b[s, rows, :]) * q.gy[rows, lanes]
            return carry
        return body

    for (q, s), (last_f, first_b) in zip(chains, ends):
        if not latent:
            q.stl[0:1, s * LANES:(s + 1) * LANES] = last_f
            q.stl[1:2, s * LANES:(s + 1) * LANES] = first_b
    for q in seqs:
        _tile_loop(nt, lru_out(q))

    def phase_a2(r, carry):
        r0 = pl.multiple_of(r * wide, wide)
        rows = pl.ds(r0, wide)
        h = stacked([normed(q, rows) for q in seqs])
        pqk_all = _dot(h, wqk_ref[...])
        pvg_all = _dot(h, wvg_ref[...])
        for s, q in enumerate(seqs):
            part = slice(s * wide, (s + 1) * wide)
            store_retention(q, r0, rows, pqk_all[part], pvg_all[part])
        return carry

    if not early_ret:
        @pl.when(first_step)
        def _():
            late_ret.wait()

        _tile_loop(n // wide, phase_a2)

    lower = (lax.broadcasted_iota(jnp.int32, (ROW_TILE, ROW_TILE), 0)
             >= lax.broadcasted_iota(jnp.int32, (ROW_TILE, ROW_TILE), 1))
    blocks = [slice(r * ROW_TILE, (r + 1) * ROW_TILE) for r in range(nt)]

    def phase_c(q, hd):
        cols = slice(hd * DK, (hd + 1) * DK)
        dec = lambda row: decay[hd][row]
        kv_f = [_dot_tn(q.k_f[hd, rows, :], q.v[rows, cols]) if (r < nt - 1 or not latent) else 0.0
                for r, rows in enumerate(blocks)]
        kv_b = [_dot_tn(q.k_b[hd, rows, :], q.v[rows, cols]) if (r > 0 or not latent) else 0.0
                for r, rows in enumerate(blocks)]
        if latent:
            run_f = q.s0[0, hd] * dec(DEC_G_F)
            run_b = q.s0[1, hd] * dec(DEC_GN_B)
        else:
            run_f = run_b = None
        before = []
        for r in range(nt):
            before.append(run_f)
            run_f = kv_f[r] if run_f is None else run_f + kv_f[r]
        after = [None] * nt
        for r in reversed(range(nt)):
            after[r] = run_b
            run_b = kv_b[r] if run_b is None else run_b + kv_b[r]

        for r, rows in enumerate(blocks):
            qf = q.q_f[hd, rows, :]
            qb = q.q_b[hd, rows, :]
            s = jnp.where(lower, _dot_nt(qf, q.k_f[hd, rows, :]), _dot_nt(qb, q.k_b[hd, rows, :]))
            o = _dot(s, q.v[rows, cols])
            if before[r] is not None:
                o = o + _dot(qf, before[r])
            if after[r] is not None:
                o = o + _dot(qb, after[r])
            o = o * lax.rsqrt(jnp.mean(o * o, axis=-1, keepdims=True) + EPS)
            q.mix[rows, LRU_W + cols.start:LRU_W + cols.stop] = o * q.sg[rows, cols]
        if not latent:
            q.str[0, hd] = run_f * dec(DEC_GN1_F)
            q.str[1, hd] = run_b

    for hd in range(RET_H):
        for q in seqs:
            phase_c(q, hd)

    gain1 = mod(2) * n2_ref[...]
    gain2 = n3_ref[...] * (1.0 + mod(4))
    sh2 = mod(3)

    def phase_d(r, carry):
        rows = pl.ds(pl.multiple_of(r * wide, wide), wide)
        mix_all = _dot(stacked([q.mix[rows, :] for q in seqs]), wout_ref[...])
        h2s = []
        for s, q in enumerate(seqs):
            x1 = q.x[rows, :] + _rms(mix_all[s * wide:(s + 1) * wide], gain1)
            q.x1[rows, :] = x1
            h2s.append(_rms(x1, gain2) + sh2)
            q.h2[rows, :] = h2s[-1].astype(_BF)
        logits_t = _dot(stacked(h2s), rw_ref[...]).T
        for s, q in enumerate(seqs):
            q.lt[:, rows] = logits_t[0:N_EXP, s * wide:(s + 1) * wide]
        return carry

    if latent:
        @pl.when(first_step)
        def _():
            late_out.wait()

    _tile_loop(n // wide, phase_d, unroll=2)


def _mixer_call(x, mod, norms, w_in, lru, wg, w_out, rw, latent, extra=(), layer=0):
    b, n, _ = x.shape
    g = 1 if latent else CONTEXT_GROUP
    const2 = lambda i: (0, 0)
    whole = pl.BlockSpec(memory_space=pl.ANY)
    in_specs = [
        pl.BlockSpec((g, n, D), lambda i: (i, 0, 0)),
        pl.BlockSpec((N_MOD, SUBLANES, D), lambda i: (0, 0, 0)),
        pl.BlockSpec((1, D), const2), pl.BlockSpec((1, D), const2), pl.BlockSpec((1, D), const2),
        pl.BlockSpec((D, D), lambda i: (0, 0)),
    ]
    if latent:
        in_specs += [whole]
    else:
        in_specs += [pl.BlockSpec((D, D), lambda i: (0, 1)),
                     pl.BlockSpec((D, D), lambda i: (0, 2))]
    in_specs += [
        pl.BlockSpec((4, LRU_W), const2), pl.BlockSpec((1, LRU_W), const2),
        pl.BlockSpec((2, LRU_W), const2), pl.BlockSpec((2, LRU_W), const2),
        pl.BlockSpec((2, LRU_W), const2),
        pl.BlockSpec((2, 2, LRU_W // 2, LRU_W), lambda i: (0, 0, 0, 0)),
        whole if latent else pl.BlockSpec((D, D), const2),
        pl.BlockSpec((D, LANES), const2),
    ]
    out_shape = [
        jax.ShapeDtypeStruct((b, n, D), _F32),
        jax.ShapeDtypeStruct((b, n, D), _BF),
        jax.ShapeDtypeStruct((b, N_EXP, n), _F32),
    ]
    out_specs = [
        pl.BlockSpec((g, n, D), lambda i: (i, 0, 0)),
        pl.BlockSpec((g, n, D), lambda i: (i, 0, 0)),
        pl.BlockSpec((g, N_EXP, n), lambda i: (i, 0, 0)),
    ]
    if latent:
        in_specs += [
            pl.BlockSpec((g, None, 2, LRU_W), lambda i: (i, layer, 0, 0)),
            pl.BlockSpec((g, None, 2, RET_H, DK, DK), lambda i: (i, layer, 0, 0, 0, 0)),
            pl.BlockSpec((n, DK), const2),
            pl.BlockSpec((n, DK), const2),
        ]
    else:
        out_shape += [
            jax.ShapeDtypeStruct((b, 1, 2, LRU_W), _F32),
            jax.ShapeDtypeStruct((b, 1, 2, RET_H, DK, DK), _F32),
        ]
        out_specs += [
            pl.BlockSpec((g, None, 2, LRU_W), lambda i: (i, 0, 0, 0)),
            pl.BlockSpec((g, None, 2, RET_H, DK, DK), lambda i: (i, 0, 0, 0, 0, 0)),
        ]
    f32s = lambda shape: pltpu.VMEM((g,) + shape, _F32)
    slabs = (LRU_W // LANES, n, LANES)
    scratch = [
        f32s((n + 2 * SUBLANES, LRU_W)),
        f32s((n, LRU_W)),
        f32s(slabs), f32s(slabs),
        f32s(slabs), f32s(slabs),
    ]
    if latent:
        scratch += [pltpu.VMEM((D, 2 * D), _F32), pltpu.VMEM((D, D), _F32),
                    pltpu.SemaphoreType.DMA((2,))]
        w_in_args = (w_in, w_in)
    else:
        scratch += [f32s((n, RET_W))] + [f32s(slabs) for _ in range(4)] + [f32s((n, RET_W))]
        w_in_args = (w_in, w_in, w_in)
    return pl.pallas_call(
        functools.partial(_mixer_kernel, n=n, g=g, latent=latent),
        out_shape=out_shape,
        grid=(b // g,),
        in_specs=in_specs,
        out_specs=out_specs,
        scratch_shapes=scratch,
        compiler_params=pltpu.CompilerParams(
            dimension_semantics=("arbitrary",), vmem_limit_bytes=VMEM_LIMIT),
        name="mixer_latent" if latent else "mixer_context",
    )(x, mod, *norms, *w_in_args, *lru, wg, w_out, rw, *extra)


def _count(mask):
    return jnp.sum(jnp.where(mask, 1.0, 0.0), axis=-1, keepdims=True)


def _probs(l3):
    bsz, _, n = l3.shape
    m = jnp.max(l3, axis=1, keepdims=True)
    e = jnp.exp(l3 - m)
    return (e / jnp.sum(e, axis=1, keepdims=True)).reshape(bsz * N_EXP, n)


def _break_ties(parts):
    prep = []
    for bits, thr, cap in parts:
        rows, n = bits.shape
        eq = bits == thr
        need = float(cap) - _count(bits > thr)
        idx = lax.broadcasted_iota(jnp.int32, (rows, n), 1)
        prep.append((eq, need, idx, int(math.log2(n))))
    most = max(nbits for _, _, _, nbits in prep)

    def idx_body(i, last):
        out = []
        for (eq, need, idx, nbits), j in zip(prep, last):
            shift = nbits - 1 - i
            cand = j | jnp.where(shift >= 0, jnp.int32(1) << jnp.maximum(shift, 0), 0)
            out.append(jnp.where(_count(eq & (idx < cand)) < need, cand, j))
        return tuple(out)

    return lax.fori_loop(0, most, idx_body,
                         tuple(jnp.zeros((eq.shape[0], 1), jnp.int32) for eq, _, _, _ in prep))


def _slots(p, bits, thr, jlast):
    rows, n = p.shape
    idx = lax.broadcasted_iota(jnp.int32, (rows, n), 1)
    sel = (bits > thr) | ((bits == thr) & (idx <= jlast))
    before = (lax.broadcasted_iota(jnp.int32, (n, n), 0)
              < lax.broadcasted_iota(jnp.int32, (n, n), 1))
    pos = _dot(jnp.where(sel, 1.0, 0.0).astype(_BF), jnp.where(before, 1.0, 0.0).astype(_BF))
    return jnp.where(sel, pos, -1.0), jnp.where(sel, p, 0.0)


def _route_kernel(lp_ref, ls_ref, pp_ref, gp_ref, ps_ref, gs_ref, *, cap_p, cap_s):
    groups = ((_probs(lp_ref[...]), float(cap_p)), (_probs(ls_ref[...]), float(cap_s)))
    bits = [pltpu.bitcast(p, jnp.int32) for p, _ in groups]

    def settle(b, t, capf, hi, lo):
        with_hi = t | hi
        both = with_hi | lo
        with_lo = t | lo
        ok = lambda cand: _count(b >= cand) >= capf
        return jnp.where(ok(with_hi), jnp.where(ok(both), both, with_hi),
                         jnp.where(ok(with_lo), with_lo, t))

    def val_body(i, thr):
        hi = jnp.int32(1) << (29 - 2 * i)
        lo = jnp.int32(1) << (28 - 2 * i)
        return tuple(settle(b, t, capf, hi, lo) for b, t, (_, capf) in zip(bits, thr, groups))

    thr = lax.fori_loop(0, 15, val_body,
                        tuple(jnp.zeros((b.shape[0], 1), jnp.int32) for b in bits))
    jlast = _break_ties([(bits[0], thr[0], cap_p), (bits[1], thr[1], cap_s)])
    pos, gate = _slots(groups[0][0], bits[0], thr[0], jlast[0])
    pp_ref[...] = pos.reshape(pp_ref.shape)
    gp_ref[...] = gate.reshape(gp_ref.shape)
    pos, gate = _slots(groups[1][0], bits[1], thr[1], jlast[1])
    ps_ref[...] = pos.reshape(ps_ref.shape)
    gs_ref[...] = gate.reshape(gs_ref.shape)


def _route_call(lt_p, lt_s, cap_p, cap_s):
    shapes = [
        jax.ShapeDtypeStruct(lt_p.shape, _F32), jax.ShapeDtypeStruct(lt_p.shape, _F32),
        jax.ShapeDtypeStruct(lt_s.shape, _F32), jax.ShapeDtypeStruct(lt_s.shape, _F32),
    ]
    return pl.pallas_call(
        functools.partial(_route_kernel, cap_p=cap_p, cap_s=cap_s),
        out_shape=shapes,
        compiler_params=pltpu.CompilerParams(vmem_limit_bytes=VMEM_LIMIT),
        name="route_select",
    )(lt_p, lt_s)


def _dispatch_kernel(pos_ref, gate_ref, h_ref, xs_ref, gs_ref, *, n, cap, g):
    slot = lax.broadcasted_iota(jnp.int32, (cap, n), 0).astype(_F32)
    for j in range(g):
        slots = slice(j * cap, (j + 1) * cap)
        parts = []
        for e in range(N_EXP):
            hit = pos_ref[j, e:e + 1, :] == slot
            parts.append(jnp.where(hit, 1.0, 0.0).astype(_BF))
            gs_ref[e, slots, :] = jnp.sum(jnp.where(hit, gate_ref[j, e:e + 1, :], 0.0),
                                          axis=-1, keepdims=True)
        onehot = jnp.concatenate(parts, axis=0)
        xs = _dot(onehot, h_ref[j]).astype(_BF)
        for e in range(N_EXP):
            xs_ref[e, slots, :] = xs[e * cap:(e + 1) * cap, :]


def _dispatch_call(pos, gate, h2, cap):
    b, n, _ = h2.shape
    g = max(1, SMALL_STEP_TOKENS // n)
    return pl.pallas_call(
        functools.partial(_dispatch_kernel, n=n, cap=cap, g=g),
        out_shape=[
            jax.ShapeDtypeStruct((N_EXP, b * cap, D), _BF),
            jax.ShapeDtypeStruct((N_EXP, b * cap, 1), _F32),
        ],
        grid=(b // g,),
        in_specs=[
            pl.BlockSpec((g, N_EXP, n), lambda i: (i, 0, 0)),
            pl.BlockSpec((g, N_EXP, n), lambda i: (i, 0, 0)),
            pl.BlockSpec((g, n, D), lambda i: (i, 0, 0)),
        ],
        out_specs=[
            pl.BlockSpec((N_EXP, g * cap, D), lambda i: (0, i, 0)),
            pl.BlockSpec((N_EXP, g * cap, 1), lambda i: (0, i, 0)),
        ],
        compiler_params=pltpu.CompilerParams(
            dimension_semantics=("arbitrary",), vmem_limit_bytes=VMEM_LIMIT),
        name="dispatch",
    )(pos, gate, h2)


def _expert_kernel(xp_ref, xs_ref, gp_ref, gs_ref, wg_ref, wu_ref, wd_ref, y_ref, xcat, acc,
                   *, sp, nf, tf):
    f = pl.program_id(1)
    xcat[0:sp, :] = xp_ref[...]
    xcat[sp:, :] = xs_ref[...]
    x = xcat[...]
    total = jnp.where(f == 0, 0.0, acc[...])
    for c in range(tf // FF_CHUNK):
        cs = slice(c * FF_CHUNK, (c + 1) * FF_CHUNK)
        hg = _dot(x, wg_ref[:, cs].astype(_BF))
        hu = _dot(x, wu_ref[:, cs].astype(_BF))
        hid = (_silu(hg) * hu).astype(_BF)
        total = total + _dot(hid, wd_ref[cs, :].astype(_BF))
    acc[...] = total
    y_ref[0:sp, :] = (total[0:sp, :] * gp_ref[...]).astype(_BF)
    y_ref[sp:, :] = (total[sp:, :] * gs_ref[...]).astype(_BF)


def _expert_call(xs_p, xs_s, g_p, g_s, w_gate, w_up, w_down):
    tf = 1024
    sp = xs_p.shape[1]
    ss = xs_s.shape[1]
    nf = FF // tf
    return pl.pallas_call(
        functools.partial(_expert_kernel, sp=sp, nf=nf, tf=tf),
        out_shape=jax.ShapeDtypeStruct((N_EXP, sp + ss, D), _BF),
        grid=(N_EXP, nf),
        in_specs=[
            pl.BlockSpec((None, sp, D), lambda e, f: (e, 0, 0)),
            pl.BlockSpec((None, ss, D), lambda e, f: (e, 0, 0)),
            pl.BlockSpec((None, sp, 1), lambda e, f: (e, 0, 0)),
            pl.BlockSpec((None, ss, 1), lambda e, f: (e, 0, 0)),
            pl.BlockSpec((None, D, tf), lambda e, f: (e, 0, f)),
            pl.BlockSpec((None, D, tf), lambda e, f: (e, 0, f)),
            pl.BlockSpec((None, tf, D), lambda e, f: (e, f, 0)),
        ],
        out_specs=pl.BlockSpec((None, sp + ss, D), lambda e, f: (e, 0, 0)),
        scratch_shapes=[pltpu.VMEM((sp + ss, D), _BF), pltpu.VMEM((sp + ss, D), _F32)],
        compiler_params=pltpu.CompilerParams(
            dimension_semantics=("arbitrary", "arbitrary"), vmem_limit_bytes=VMEM_LIMIT),
        name="expert_ffn",
    )(xs_p, xs_s, g_p, g_s, w_gate, w_up, w_down)


def _combine_kernel(pos_ref, ye_ref, x1_ref, mod_ref, n4_ref, y_ref, *, n, cap, g, latent):
    width = N_EXP * cap
    lane = lax.broadcasted_iota(jnp.int32, (N_EXP, width), 1)
    expand = jnp.where(lane // cap == lax.broadcasted_iota(jnp.int32, (N_EXP, width), 0),
                       1.0, 0.0).astype(_BF)
    tile = min(n, COMBINE_TILE)
    slot = (lax.broadcasted_iota(jnp.int32, (tile, width), 1) % cap).astype(_F32)
    mod_row = (pl.program_id(0) + 1) if latent else 0
    gain = mod_ref[N_MOD - 1, pl.ds(mod_row, 1), :] * n4_ref[...]

    for j in range(g):
        ye = ye_ref[:, j * cap:(j + 1) * cap, :].reshape(width, D)

        def body(r, carry, j=j, ye=ye):
            rows = pl.ds(pl.multiple_of(r * tile, tile), tile)
            pos_e = _dot_tn(pos_ref[j, :, rows].astype(_BF), expand)
            onehot = jnp.where(pos_e == slot, 1.0, 0.0).astype(_BF)
            f = _dot(onehot, ye)
            y_ref[j, rows, :] = x1_ref[j, rows, :] + _rms(f, gain)
            return carry

        _tile_loop(n // tile, body)


def _combine_call(pos, ye, x1, mod, norm_post, cap, slot_off, latent):
    b, n, _ = x1.shape
    g = max(1, SMALL_STEP_TOKENS // n)
    assert g == 1 or not latent
    blk_off = slot_off // (g * cap)
    return pl.pallas_call(
        functools.partial(_combine_kernel, n=n, cap=cap, g=g, latent=latent),
        out_shape=jax.ShapeDtypeStruct((b, n, D), _F32),
        grid=(b // g,),
        in_specs=[
            pl.BlockSpec((g, N_EXP, n), lambda i: (i, 0, 0)),
            pl.BlockSpec((N_EXP, g * cap, D), lambda i: (0, i + blk_off, 0)),
            pl.BlockSpec((g, n, D), lambda i: (i, 0, 0)),
            pl.BlockSpec((N_MOD, SUBLANES, D), lambda i: (0, 0, 0)),
            pl.BlockSpec((1, D), lambda i: (0, 0)),
        ],
        out_specs=pl.BlockSpec((g, n, D), lambda i: (i, 0, 0)),
        compiler_params=pltpu.CompilerParams(
            dimension_semantics=("arbitrary",), vmem_limit_bytes=VMEM_LIMIT),
        name="combine_latent" if latent else "combine_context",
    )(pos, ye, x1, mod, norm_post)


def _block_diag_gates(wa, wi):
    per_half = LRU_HEADS // 2
    side = per_half * LRU_HD
    on_diag = (np.arange(side)[:, None] // LRU_HD) == (np.arange(side)[None, :] // LRU_HD)

    def bd(w):
        rows = w.reshape(2, 2, side, LRU_HD)
        return jnp.where(on_diag, jnp.tile(rows, (1, 1, 1, per_half)), 0.0)

    return 0.5 * jnp.concatenate([bd(wa), bd(wi)], axis=-1)


def _rope_tables(n):
    rows = n // GRID_W
    row = np.repeat(np.arange(rows, dtype=np.float32), GRID_W)
    col = np.tile(np.arange(GRID_W, dtype=np.float32), rows)
    nf = DK // 4
    freqs = np.float32(ROPE_BASE) ** (-np.arange(nf, dtype=np.float32) / np.float32(nf))
    ang = np.concatenate([row[:, None] * freqs, col[:, None] * freqs], axis=-1).astype(np.float32)
    cos = np.cos(ang)
    sin = np.sin(ang)
    return (jnp.asarray(np.concatenate([cos, cos], axis=-1), _F32),
            jnp.asarray(np.concatenate([-sin, sin], axis=-1), _F32))


def _decay_consts(n):
    heads = np.arange(RET_H, dtype=np.float32)
    f32 = np.float32
    lgf = np.log1p(-np.exp2(-(f32(RET_DECAY_OFFSET_FWD) + heads))).astype(f32)
    lgb = np.log1p(-np.exp2(-(f32(RET_DECAY_OFFSET_BWD) + heads))).astype(f32)
    tab = np.stack([lgf, lgb, np.exp(lgf), np.exp(f32(n) * lgb), np.exp(f32(n - 1) * lgf)], axis=1)
    return [[float(v) for v in row] for row in tab.astype(f32)]


def kernel(x_prompt, x_sample, c, state_lru, state_ret, c_ctx, ada_w, ada_b, norm_mix_pre, norm_mix_post, norm_ffn_pre, norm_ffn_post, w_in, conv_w, conv_b, lru_wa, lru_ba, lru_wi, lru_bi, lru_lambda, w_out, router_w, exp_w_gate, exp_w_up, exp_w_down):
    bp, n_p, _ = x_prompt.shape
    bs, n_s, _ = x_sample.shape
    cap_p = 2 * n_p // N_EXP
    cap_s = 2 * n_s // N_EXP
    l = 0

    mod = _ada_call(c_ctx[None, :], c, ada_w[l], ada_b[l][None, :])

    norms = (norm_mix_pre[l][None], norm_mix_post[l][None], norm_ffn_pre[l][None])
    lru = (conv_w[l], conv_b[l][None], lru_ba[l], lru_bi[l], lru_lambda[l])
    wg = _block_diag_gates(lru_wa[l], lru_wi[l])
    rw = jnp.pad(router_w[l], ((0, 0), (0, LANES - N_EXP)))
    cos2, sin2 = _rope_tables(n_s)

    x1_p, h2_p, lt_p, st_lru, st_ret = _mixer_call(
        x_prompt, mod, norms, w_in[l], lru, wg, w_out[l], rw, latent=False)
    x1_s, h2_s, lt_s = _mixer_call(
        x_sample, mod, norms, w_in[l], lru, wg, w_out[l], rw, latent=True,
        extra=(state_lru, state_ret, cos2, sin2), layer=l)

    pos_p, gate_p, pos_s, gate_s = _route_call(lt_p, lt_s, cap_p, cap_s)
    xs_p, gsl_p = _dispatch_call(pos_p, gate_p, h2_p, cap_p)
    xs_s, gsl_s = _dispatch_call(pos_s, gate_s, h2_s, cap_s)
    ye = _expert_call(xs_p, xs_s, gsl_p, gsl_s, exp_w_gate[l], exp_w_up[l], exp_w_down[l])

    norm_post = norm_ffn_post[l][None]
    y_p = _combine_call(pos_p, ye, x1_p, mod, norm_post, cap_p, 0, latent=False)
    y_s = _combine_call(pos_s, ye, x1_s, mod, norm_post, cap_s, bp * cap_p, latent=True)
    return (y_p, y_s, st_lru, st_ret)
```

```python
import functools
import math
import types

import jax
import jax.numpy as jnp
import numpy as np
from jax import lax
from jax.experimental import pallas as pl
from jax.experimental.pallas import tpu as pltpu

D = 1024
LRU_W = 512
LRU_HEADS = 8
LRU_HD = 64
LRU_C = 8.0
RET_W = 512
RET_H = 4
DK = 128
N_EXP = 16
FF = 2048
N_MOD = 6
EPS = 1e-6
GRID_W = 64
ROPE_BASE = 10000.0
RET_DECAY_OFFSET_FWD = 5.0
RET_DECAY_OFFSET_BWD = 5.5

ROW_TILE = 256
WIDE_TILE = 512
FF_CHUNK = 512
COMBINE_TILE = 1024
CONTEXT_GROUP = 2
SMALL_STEP_TOKENS = 1024
SUBLANES = 8
LANES = 128
VMEM_LIMIT = 60 * 1024 * 1024

DEC_LOG_F, DEC_LOG_B, DEC_G_F, DEC_GN_B, DEC_GN1_F = 0, 1, 2, 3, 4

_BF = jnp.bfloat16
_F32 = jnp.float32


def _sigmoid(x):
    return 0.5 * jnp.tanh(0.5 * x) + 0.5


def _silu(x):
    return x * _sigmoid(x)


def _gelu_tanh(x):
    c = math.sqrt(2.0 / math.pi)
    return 0.5 * x * (1.0 + jnp.tanh(c * (x + 0.044715 * (x * x * x))))


def _rms(x, gain):
    return x * lax.rsqrt(jnp.mean(x * x, axis=-1, keepdims=True) + EPS) * gain


def _dot(a, b):
    return jnp.dot(a, b, preferred_element_type=_F32)


def _dot_nt(a, b):
    return lax.dot_general(a, b, (((1,), (1,)), ((), ())), preferred_element_type=_F32)


def _dot_tn(a, b):
    return lax.dot_general(a, b, (((0,), (0,)), ((), ())), preferred_element_type=_F32)


def _ada_kernel(cc_ref, c_ref, w_ref, b_ref, o_ref, s_ref):
    nb = c_ref.shape[0]
    s_ref[...] = jnp.zeros_like(s_ref)
    s_ref[0:1, :] = _silu(cc_ref[...])
    s_ref[1:1 + nb, :] = _silu(c_ref[...])
    o_ref[...] = _dot(s_ref[...], w_ref[...]) + b_ref[...]


def _ada_call(c_ctx, c, ada_w, ada_b):
    nb = c.shape[0]
    assert nb + 1 <= SUBLANES
    return pl.pallas_call(
        _ada_kernel,
        out_shape=jax.ShapeDtypeStruct((N_MOD, SUBLANES, D), _F32),
        grid=(N_MOD,),
        in_specs=[
            pl.BlockSpec((1, D), lambda j: (0, 0)),
            pl.BlockSpec((nb, D), lambda j: (0, 0)),
            pl.BlockSpec((D, D), lambda j: (0, j)),
            pl.BlockSpec((1, D), lambda j: (0, j)),
        ],
        out_specs=pl.BlockSpec((None, SUBLANES, D), lambda j: (j, 0, 0)),
        scratch_shapes=[pltpu.VMEM((SUBLANES, D), _F32)],
        compiler_params=pltpu.CompilerParams(
            dimension_semantics=("arbitrary",), vmem_limit_bytes=VMEM_LIMIT),
        name="ada_mod",
    )(c_ctx, c, ada_w, ada_b)


def _tile_loop(nt, body, unroll=1):
    if nt == 1:
        body(0, 0)
    else:
        lax.fori_loop(0, nt, body, 0, unroll=unroll)


def _mixer_kernel(*refs, n, g, latent):
    if latent:
        (x_ref, mod_ref, n1_ref, n2_ref, n3_ref, wl_ref, win_hbm,
         cw_ref, cb_ref, ba_ref, bi_ref, lam_ref, wg_ref, wout_hbm, rw_ref,
         h0_ref, s0_ref, cos_ref, sin_ref,
         x1_ref, h2_ref, lt_ref,
         xlp_g, gy_g, af_g, ab_g, hf_g, hb_g, wret_v, wout_ref, late_sem) = refs
        sg_g, qf_g, qb_g, kf_g, kb_g, v_g = xlp_g, af_g, ab_g, hf_g, hb_g, gy_g
        wqk_ref, wvg_ref = wret_v.at[:, 0:D], wret_v.at[:, D:2 * D]
    else:
        (x_ref, mod_ref, n1_ref, n2_ref, n3_ref, wl_ref, wqk_ref, wvg_ref,
         cw_ref, cb_ref, ba_ref, bi_ref, lam_ref, wg_ref, wout_hbm, rw_ref,
         x1_ref, h2_ref, lt_ref, stl_ref, str_ref,
         xlp_g, gy_g, af_g, ab_g, hf_g, hb_g, sg_g, qf_g, qb_g, kf_g, kb_g, v_g,
         wout_ref, late_sem) = refs
    early_ret = not latent

    first_step = pl.program_id(0) == 0
    late_out = pltpu.make_async_copy(wout_hbm, wout_ref, late_sem.at[0])
    if latent:
        late_ret = pltpu.make_async_copy(win_hbm.at[:, pl.ds(D, 2 * D)], wret_v, late_sem.at[1])

    @pl.when(first_step)
    def _():
        if latent:
            late_ret.start()
        late_out.start()

    seqs = []
    for s in range(g):
        q = types.SimpleNamespace(
            x=x_ref.at[s], x1=x1_ref.at[s], h2=h2_ref.at[s], lt=lt_ref.at[s], mix=x1_ref.at[s],
            xlp=xlp_g.at[s], sg=sg_g.at[s], gy=gy_g.at[s], v=v_g.at[s],
            a_f=af_g.at[s], a_b=ab_g.at[s], h_f=hf_g.at[s], h_b=hb_g.at[s],
            q_f=qf_g.at[s], q_b=qb_g.at[s], k_f=kf_g.at[s], k_b=kb_g.at[s])
        if latent:
            q.h0, q.s0 = h0_ref.at[s], s0_ref.at[s]
        else:
            q.stl, q.str = stl_ref.at[s], str_ref.at[s]
        seqs.append(q)

    nt = n // ROW_TILE
    decay = _decay_consts(n)
    mod_row = (pl.program_id(0) + 1) if latent else 0
    mod = lambda k: mod_ref[k, pl.ds(mod_row, 1), :]
    shift = mod(0)
    scale = n1_ref[...] * (1.0 + mod(1))

    def normed(q, rows):
        return _rms(q.x[rows, :], scale) + shift

    wide = min(n, WIDE_TILE)
    a_tile = wide if latent else ROW_TILE

    def stacked(parts):
        return parts[0] if len(parts) == 1 else jnp.concatenate(parts, axis=0)

    def store_retention(q, r0, rows, pqk, pvg):
        q.sg[rows, :] = _silu(pvg[:, RET_W:])
        tpos = (r0 + lax.broadcasted_iota(jnp.int32, (pqk.shape[0], DK), 0)).astype(_F32)
        if latent:
            cos2 = cos_ref[rows, :]
            sin2 = sin_ref[rows, :]
        for hd in range(RET_H):
            cols = slice(hd * DK, (hd + 1) * DK)
            qh = pqk[:, cols] * (DK ** -0.5)
            kh = pqk[:, RET_W + hd * DK:RET_W + (hd + 1) * DK]
            if latent:
                qh = qh * cos2 + pltpu.roll(qh, DK // 2, axis=1) * sin2
                kh = kh * cos2 + pltpu.roll(kh, DK // 2, axis=1) * sin2
            lgf = decay[hd][DEC_LOG_F]
            lgb = decay[hd][DEC_LOG_B]
            q.q_f[hd, rows, :] = qh * jnp.exp(tpos * lgf)
            q.k_f[hd, rows, :] = kh * jnp.exp(tpos * (-lgf))
            q.q_b[hd, rows, :] = qh * jnp.exp(tpos * (-lgb))
            q.k_b[hd, rows, :] = kh * jnp.exp(tpos * lgb)
        q.v[rows, :] = pvg[:, 0:RET_W]

    def phase_a1(r, carry):
        r0 = pl.multiple_of(r * a_tile, a_tile)
        rows = pl.ds(r0, a_tile)
        h = stacked([normed(q, rows) for q in seqs])
        p_all = _dot(h, wl_ref[...])
        if early_ret:
            pqk_all = _dot(h, wqk_ref[...])
            pvg_all = _dot(h, wvg_ref[...])
        for s, q in enumerate(seqs):
            part = slice(s * a_tile, (s + 1) * a_tile)
            q.xlp[pl.ds(r0 + SUBLANES, a_tile), :] = p_all[part, 0:LRU_W]
            q.gy[rows, :] = _gelu_tanh(p_all[part, LRU_W:])
            if early_ret:
                store_retention(q, r0, rows, pqk_all[part], pvg_all[part])
        return carry

    for q in seqs:
        q.xlp[0:SUBLANES, :] = jnp.zeros((SUBLANES, LRU_W), _F32)
        q.xlp[n + SUBLANES:n + 2 * SUBLANES, :] = jnp.zeros((SUBLANES, LRU_W), _F32)

    half = LRU_W // 2

    def softplus_neg(lam):
        z = -lam
        return jnp.maximum(z, 0.0) + jnp.log1p(jnp.exp(-jnp.abs(z)))

    sp = (softplus_neg(lam_ref[0:1, :]), softplus_neg(lam_ref[1:2, :]))

    def phase_b(q):
        def body(r, carry):
            r0 = pl.multiple_of(r * a_tile, a_tile)
            rows = pl.ds(r0, a_tile)
            ext = q.xlp[pl.ds(r0, a_tile + 2 * SUBLANES), :]
            xc = cb_ref[...]
            for tap in range(4):
                back = (2 - tap) % (a_tile + 2 * SUBLANES)
                win = ext if back == 0 else pltpu.roll(ext, back, axis=0)
                xc = xc + win[SUBLANES:SUBLANES + a_tile, :] * cw_ref[tap:tap + 1, :]
            xh = 0.5 * xc
            for d, (a_ref, u_ref) in enumerate(((q.a_f, q.h_f), (q.a_b, q.h_b))):
                bah = 0.5 * ba_ref[d:d + 1, :]
                bih = 0.5 * bi_ref[d:d + 1, :]
                ch = (-0.5 * LRU_C) * sp[d]
                for hh in range(2):
                    cs = slice(hh * half, (hh + 1) * half)
                    pre = _dot(xc[:, cs], wg_ref[d, hh])
                    t_r = jnp.tanh(pre[:, 0:half] + bah[:, cs])
                    t_i = jnp.tanh(pre[:, half:] + bih[:, cs])
                    log_a = t_r * ch[:, cs] + ch[:, cs]
                    a = jnp.exp(log_a)
                    om = -jnp.tanh(log_a) * (a * a + 1.0)
                    root = jnp.where(om > 0.0, om * lax.rsqrt(om), 0.0)
                    u = root * (t_i * xh[:, cs] + xh[:, cs])
                    for j in range(half // LANES):
                        lanes = slice(j * LANES, (j + 1) * LANES)
                        a_ref[hh * (half // LANES) + j, rows, :] = a[:, lanes]
                        u_ref[hh * (half // LANES) + j, rows, :] = u[:, lanes]
            return carry
        return body

    _tile_loop(n // a_tile, phase_a1)
    for q in seqs:
        _tile_loop(n // a_tile, phase_b(q))

    row8 = lax.broadcasted_iota(jnp.int32, (SUBLANES, LANES), 0)
    block = SUBLANES * SUBLANES
    n_blocks = n // block
    n_slabs = LRU_W // LANES

    def across_groups(a, b, reverse):
        for s in (1, 2, 4):
            m = (row8 < SUBLANES - s) if reverse else (row8 >= s)
            shift = SUBLANES - s if reverse else s
            a_s = jnp.where(m, pltpu.roll(a, shift, axis=0), 1.0)
            b_s = jnp.where(m, pltpu.roll(b, shift, axis=0), 0.0)
            b = a * b_s + b
            a = a * a_s
        return a, b

    def scan_block(a_ref, h_ref, base, carry, reverse):
        rows = [pl.ds(base + k, SUBLANES, stride=SUBLANES) for k in range(SUBLANES)]
        order = list(reversed(range(SUBLANES))) if reverse else list(range(SUBLANES))
        prod, local = {}, {}
        prev = None
        for k in order:
            a, u = a_ref[rows[k], :], h_ref[rows[k], :]
            prod[k] = a if prev is None else a * prod[prev]
            local[k] = u if prev is None else a * local[prev] + u
            prev = k
        p_all, h_all = across_groups(prod[prev], local[prev], reverse)
        inner = (row8 < SUBLANES - 1) if reverse else (row8 >= 1)
        shift = SUBLANES - 1 if reverse else 1
        enter = (jnp.where(inner, pltpu.roll(p_all, shift, axis=0), 1.0) * carry
                 + jnp.where(inner, pltpu.roll(h_all, shift, axis=0), 0.0))
        for k in order:
            h_ref[rows[k], :] = prod[k] * enter + local[k]
        leave = p_all * carry + h_all
        return leave[0:1, :] if reverse else leave[SUBLANES - 1:SUBLANES, :]

    def initial(q, d, s):
        if latent:
            return q.h0[d:d + 1, s * LANES:(s + 1) * LANES]
        return jnp.zeros((1, LANES), _F32)

    chains = [(q, s) for q in seqs for s in range(n_slabs)]

    def scan_body(i, carry):
        fwd_base = pl.multiple_of(i * block, block)
        bwd_base = pl.multiple_of((n_blocks - 1 - i) * block, block)
        out = []
        for (q, s), (cf, cb) in zip(chains, carry):
            out.append((scan_block(q.a_f.at[s], q.h_f.at[s], fwd_base, cf, False),
                        scan_block(q.a_b.at[s], q.h_b.at[s], bwd_base, cb, True)))
        return tuple(out)

    ends = lax.fori_loop(0, n_blocks, scan_body,
                         tuple((initial(q, 0, s), initial(q, 1, s)) for q, s in chains), unroll=2)

    if not latent:
        @pl.when(first_step)
        def _():
            late_out.wait()

    def lru_out(q):
        def body(r, carry):
            rows = pl.ds(pl.multiple_of(r * ROW_TILE, ROW_TILE), ROW_TILE)
            for s in range(n_slabs):
                lanes = slice(s * LANES, (s + 1) * LANES)
                q.mix[rows, lanes] = (q.h_f[s, rows, :] + q.h_b[s, rows, :]) * q.gy[rows, lanes]
            return carry
        return body

    for (q, s), (last_f, first_b) in zip(chains, ends):
        if not latent:
            q.stl[0:1, s * LANES:(s + 1) * LANES] = last_f
            q.stl[1:2, s * LANES:(s + 1) * LANES] = first_b
    for q in seqs:
        _tile_loop(nt, lru_out(q))

    def phase_a2(r, carry):
        r0 = pl.multiple_of(r * wide, wide)
        rows = pl.ds(r0, wide)
        h = stacked([normed(q, rows) for q in seqs])
        pqk_all = _dot(h, wqk_ref[...])
        pvg_all = _dot(h, wvg_ref[...])
        for s, q in enumerate(seqs):
            part = slice(s * wide, (s + 1) * wide)
            store_retention(q, r0, rows, pqk_all[part], pvg_all[part])
        return carry

    if not early_ret:
        @pl.when(first_step)
        def _():
            late_ret.wait()

        _tile_loop(n // wide, phase_a2)

    lower = (lax.broadcasted_iota(jnp.int32, (ROW_TILE, ROW_TILE), 0)
             >= lax.broadcasted_iota(jnp.int32, (ROW_TILE, ROW_TILE), 1))
    blocks = [slice(r * ROW_TILE, (r + 1) * ROW_TILE) for r in range(nt)]

    def phase_c(q, hd):
        cols = slice(hd * DK, (hd + 1) * DK)
        dec = lambda row: decay[hd][row]
        kv_f = [_dot_tn(q.k_f[hd, rows, :], q.v[rows, cols]) if (r < nt - 1 or not latent) else 0.0
                for r, rows in enumerate(blocks)]
        kv_b = [_dot_tn(q.k_b[hd, rows, :], q.v[rows, cols]) if (r > 0 or not latent) else 0.0
                for r, rows in enumerate(blocks)]
        if latent:
            run_f = q.s0[0, hd] * dec(DEC_G_F)
            run_b = q.s0[1, hd] * dec(DEC_GN_B)
        else:
            run_f = run_b = None
        before = []
        for r in range(nt):
            before.append(run_f)
            run_f = kv_f[r] if run_f is None else run_f + kv_f[r]
        after = [None] * nt
        for r in reversed(range(nt)):
            after[r] = run_b
            run_b = kv_b[r] if run_b is None else run_b + kv_b[r]

        for r, rows in enumerate(blocks):
            qf = q.q_f[hd, rows, :]
            qb = q.q_b[hd, rows, :]
            s = jnp.where(lower, _dot_nt(qf, q.k_f[hd, rows, :]), _dot_nt(qb, q.k_b[hd, rows, :]))
            o = _dot(s, q.v[rows, cols])
            if before[r] is not None:
                o = o + _dot(qf, before[r])
            if after[r] is not None:
                o = o + _dot(qb, after[r])
            o = o * lax.rsqrt(jnp.mean(o * o, axis=-1, keepdims=True) + EPS)
            q.mix[rows, LRU_W + cols.start:LRU_W + cols.stop] = o * q.sg[rows, cols]
        if not latent:
            q.str[0, hd] = run_f * dec(DEC_GN1_F)
            q.str[1, hd] = run_b

    for hd in range(RET_H):
        for q in seqs:
            phase_c(q, hd)

    gain1 = mod(2) * n2_ref[...]
    gain2 = n3_ref[...] * (1.0 + mod(4))
    sh2 = mod(3)

    def phase_d(r, carry):
        rows = pl.ds(pl.multiple_of(r * wide, wide), wide)
        mix_all = _dot(stacked([q.mix[rows, :] for q in seqs]), wout_ref[...])
        h2s = []
        for s, q in enumerate(seqs):
            x1 = q.x[rows, :] + _rms(mix_all[s * wide:(s + 1) * wide], gain1)
            q.x1[rows, :] = x1
            h2s.append(_rms(x1, gain2) + sh2)
            q.h2[rows, :] = h2s[-1].astype(_BF)
        logits_t = _dot(stacked(h2s), rw_ref[...]).T
        for s, q in enumerate(seqs):
            q.lt[:, rows] = logits_t[0:N_EXP, s * wide:(s + 1) * wide]
        return carry

    if latent:
        @pl.when(first_step)
        def _():
            late_out.wait()

    _tile_loop(n // wide, phase_d, unroll=2)


def _mixer_call(x, mod, norms, w_in, lru, wg, w_out, rw, latent, extra=(), layer=0):
    b, n, _ = x.shape
    g = 1 if latent else CONTEXT_GROUP
    const2 = lambda i: (0, 0)
    whole = pl.BlockSpec(memory_space=pl.ANY)
    in_specs = [
        pl.BlockSpec((g, n, D), lambda i: (i, 0, 0)),
        pl.BlockSpec((N_MOD, SUBLANES, D), lambda i: (0, 0, 0)),
        pl.BlockSpec((1, D), const2), pl.BlockSpec((1, D), const2), pl.BlockSpec((1, D), const2),
        pl.BlockSpec((D, D), lambda i: (0, 0)),
    ]
    if latent:
        in_specs += [whole]
    else:
        in_specs += [pl.BlockSpec((D, D), lambda i: (0, 1)),
                     pl.BlockSpec((D, D), lambda i: (0, 2))]
    in_specs += [
        pl.BlockSpec((4, LRU_W), const2), pl.BlockSpec((1, LRU_W), const2),
        pl.BlockSpec((2, LRU_W), const2), pl.BlockSpec((2, LRU_W), const2),
        pl.BlockSpec((2, LRU_W), const2),
        pl.BlockSpec((2, 2, LRU_W // 2, LRU_W), lambda i: (0, 0, 0, 0)),
        whole,
        pl.BlockSpec((D, LANES), const2),
    ]
    out_shape = [
        jax.ShapeDtypeStruct((b, n, D), _F32),
        jax.ShapeDtypeStruct((b, n, D), _BF),
        jax.ShapeDtypeStruct((b, N_EXP, n), _F32),
    ]
    out_specs = [
        pl.BlockSpec((g, n, D), lambda i: (i, 0, 0)),
        pl.BlockSpec((g, n, D), lambda i: (i, 0, 0)),
        pl.BlockSpec((g, N_EXP, n), lambda i: (i, 0, 0)),
    ]
    if latent:
        in_specs += [
            pl.BlockSpec((g, None, 2, LRU_W), lambda i: (i, layer, 0, 0)),
            pl.BlockSpec((g, None, 2, RET_H, DK, DK), lambda i: (i, layer, 0, 0, 0, 0)),
            pl.BlockSpec((n, DK), const2),
            pl.BlockSpec((n, DK), const2),
        ]
    else:
        out_shape += [
            jax.ShapeDtypeStruct((b, 1, 2, LRU_W), _F32),
            jax.ShapeDtypeStruct((b, 1, 2, RET_H, DK, DK), _F32),
        ]
        out_specs += [
            pl.BlockSpec((g, None, 2, LRU_W), lambda i: (i, 0, 0, 0)),
            pl.BlockSpec((g, None, 2, RET_H, DK, DK), lambda i: (i, 0, 0, 0, 0, 0)),
        ]
    f32s = lambda shape: pltpu.VMEM((g,) + shape, _F32)
    slabs = (LRU_W // LANES, n, LANES)
    scratch = [
        f32s((n + 2 * SUBLANES, LRU_W)),
        f32s((n, LRU_W)),
        f32s(slabs), f32s(slabs),
        f32s(slabs), f32s(slabs),
    ]
    if latent:
        scratch += [pltpu.VMEM((D, 2 * D), _F32)]
        w_in_args = (w_in, w_in)
    else:
        scratch += [f32s((n, RET_W))] + [f32s(slabs) for _ in range(4)] + [f32s((n, RET_W))]
        w_in_args = (w_in, w_in, w_in)
    scratch += [pltpu.VMEM((D, D), _F32), pltpu.SemaphoreType.DMA((2 if latent else 1,))]
    return pl.pallas_call(
        functools.partial(_mixer_kernel, n=n, g=g, latent=latent),
        out_shape=out_shape,
        grid=(b // g,),
        in_specs=in_specs,
        out_specs=out_specs,
        scratch_shapes=scratch,
        compiler_params=pltpu.CompilerParams(
            dimension_semantics=("arbitrary",), vmem_limit_bytes=VMEM_LIMIT),
        name="mixer_latent" if latent else "mixer_context",
    )(x, mod, *norms, *w_in_args, *lru, wg, w_out, rw, *extra)


def _count(mask):
    return jnp.sum(jnp.where(mask, 1.0, 0.0), axis=-1, keepdims=True)


def _probs(l3):
    bsz, _, n = l3.shape
    m = jnp.max(l3, axis=1, keepdims=True)
    e = jnp.exp(l3 - m)
    return (e / jnp.sum(e, axis=1, keepdims=True)).reshape(bsz * N_EXP, n)


def _break_ties(parts):
    prep = []
    for bits, thr, cap in parts:
        rows, n = bits.shape
        eq = bits == thr
        need = float(cap) - _count(bits > thr)
        idx = lax.broadcasted_iota(jnp.int32, (rows, n), 1)
        prep.append((eq, need, idx, int(math.log2(n))))
    most = max(nbits for _, _, _, nbits in prep)

    def idx_body(i, last):
        out = []
        for (eq, need, idx, nbits), j in zip(prep, last):
            shift = nbits - 1 - i
            cand = j | jnp.where(shift >= 0, jnp.int32(1) << jnp.maximum(shift, 0), 0)
            out.append(jnp.where(_count(eq & (idx < cand)) < need, cand, j))
        return tuple(out)

    return lax.fori_loop(0, most, idx_body,
                         tuple(jnp.zeros((eq.shape[0], 1), jnp.int32) for eq, _, _, _ in prep))


def _slots(p, bits, thr, jlast):
    rows, n = p.shape
    idx = lax.broadcasted_iota(jnp.int32, (rows, n), 1)
    sel = (bits > thr) | ((bits == thr) & (idx <= jlast))
    before = (lax.broadcasted_iota(jnp.int32, (n, n), 0)
              < lax.broadcasted_iota(jnp.int32, (n, n), 1))
    pos = _dot(jnp.where(sel, 1.0, 0.0).astype(_BF), jnp.where(before, 1.0, 0.0).astype(_BF))
    return jnp.where(sel, pos, -1.0), jnp.where(sel, p, 0.0)


def _route_kernel(lp_ref, ls_ref, pp_ref, gp_ref, ps_ref, gs_ref, *, cap_p, cap_s):
    groups = ((_probs(lp_ref[...]), float(cap_p)), (_probs(ls_ref[...]), float(cap_s)))
    bits = [pltpu.bitcast(p, jnp.int32) for p, _ in groups]

    def settle(b, t, capf, hi, lo):
        with_hi = t | hi
        both = with_hi | lo
        with_lo = t | lo
        ok = lambda cand: _count(b >= cand) >= capf
        return jnp.where(ok(with_hi), jnp.where(ok(both), both, with_hi),
                         jnp.where(ok(with_lo), with_lo, t))

    def val_body(i, thr):
        hi = jnp.int32(1) << (29 - 2 * i)
        lo = jnp.int32(1) << (28 - 2 * i)
        return tuple(settle(b, t, capf, hi, lo) for b, t, (_, capf) in zip(bits, thr, groups))

    thr = lax.fori_loop(0, 15, val_body,
                        tuple(jnp.zeros((b.shape[0], 1), jnp.int32) for b in bits))
    jlast = _break_ties([(bits[0], thr[0], cap_p), (bits[1], thr[1], cap_s)])
    pos, gate = _slots(groups[0][0], bits[0], thr[0], jlast[0])
    pp_ref[...] = pos.reshape(pp_ref.shape)
    gp_ref[...] = gate.reshape(gp_ref.shape)
    pos, gate = _slots(groups[1][0], bits[1], thr[1], jlast[1])
    ps_ref[...] = pos.reshape(ps_ref.shape)
    gs_ref[...] = gate.reshape(gs_ref.shape)


def _route_call(lt_p, lt_s, cap_p, cap_s):
    shapes = [
        jax.ShapeDtypeStruct(lt_p.shape, _F32), jax.ShapeDtypeStruct(lt_p.shape, _F32),
        jax.ShapeDtypeStruct(lt_s.shape, _F32), jax.ShapeDtypeStruct(lt_s.shape, _F32),
    ]
    return pl.pallas_call(
        functools.partial(_route_kernel, cap_p=cap_p, cap_s=cap_s),
        out_shape=shapes,
        compiler_params=pltpu.CompilerParams(vmem_limit_bytes=VMEM_LIMIT),
        name="route_select",
    )(lt_p, lt_s)


def _dispatch_kernel(pos_ref, gate_ref, h_ref, xs_ref, gs_ref, *, n, cap, g):
    slot = lax.broadcasted_iota(jnp.int32, (cap, n), 0).astype(_F32)
    for j in range(g):
        slots = slice(j * cap, (j + 1) * cap)
        parts = []
        for e in range(N_EXP):
            hit = pos_ref[j, e:e + 1, :] == slot
            parts.append(jnp.where(hit, 1.0, 0.0).astype(_BF))
            gs_ref[e, slots, :] = jnp.sum(jnp.where(hit, gate_ref[j, e:e + 1, :], 0.0),
                                          axis=-1, keepdims=True)
        onehot = jnp.concatenate(parts, axis=0)
        xs = _dot(onehot, h_ref[j]).astype(_BF)
        for e in range(N_EXP):
            xs_ref[e, slots, :] = xs[e * cap:(e + 1) * cap, :]


def _dispatch_call(pos, gate, h2, cap):
    b, n, _ = h2.shape
    g = max(1, SMALL_STEP_TOKENS // n)
    return pl.pallas_call(
        functools.partial(_dispatch_kernel, n=n, cap=cap, g=g),
        out_shape=[
            jax.ShapeDtypeStruct((N_EXP, b * cap, D), _BF),
            jax.ShapeDtypeStruct((N_EXP, b * cap, 1), _F32),
        ],
        grid=(b // g,),
        in_specs=[
            pl.BlockSpec((g, N_EXP, n), lambda i: (i, 0, 0)),
            pl.BlockSpec((g, N_EXP, n), lambda i: (i, 0, 0)),
            pl.BlockSpec((g, n, D), lambda i: (i, 0, 0)),
        ],
        out_specs=[
            pl.BlockSpec((N_EXP, g * cap, D), lambda i: (0, i, 0)),
            pl.BlockSpec((N_EXP, g * cap, 1), lambda i: (0, i, 0)),
        ],
        compiler_params=pltpu.CompilerParams(
            dimension_semantics=("arbitrary",), vmem_limit_bytes=VMEM_LIMIT),
        name="dispatch",
    )(pos, gate, h2)


def _expert_kernel(xp_ref, xs_ref, gp_ref, gs_ref, wg_ref, wu_ref, wd_ref, y_ref, xcat, acc,
                   *, sp, nf, tf):
    f = pl.program_id(1)
    xcat[0:sp, :] = xp_ref[...]
    xcat[sp:, :] = xs_ref[...]
    x = xcat[...]
    total = jnp.where(f == 0, 0.0, acc[...])
    for c in range(tf // FF_CHUNK):
        cs = slice(c * FF_CHUNK, (c + 1) * FF_CHUNK)
        hg = _dot(x, wg_ref[:, cs].astype(_BF))
        hu = _dot(x, wu_ref[:, cs].astype(_BF))
        hid = (_silu(hg) * hu).astype(_BF)
        total = total + _dot(hid, wd_ref[cs, :].astype(_BF))
    acc[...] = total
    y_ref[0:sp, :] = (total[0:sp, :] * gp_ref[...]).astype(_BF)
    y_ref[sp:, :] = (total[sp:, :] * gs_ref[...]).astype(_BF)


def _expert_call(xs_p, xs_s, g_p, g_s, w_gate, w_up, w_down):
    tf = 1024
    sp = xs_p.shape[1]
    ss = xs_s.shape[1]
    nf = FF // tf
    return pl.pallas_call(
        functools.partial(_expert_kernel, sp=sp, nf=nf, tf=tf),
        out_shape=jax.ShapeDtypeStruct((N_EXP, sp + ss, D), _BF),
        grid=(N_EXP, nf),
        in_specs=[
            pl.BlockSpec((None, sp, D), lambda e, f: (e, 0, 0)),
            pl.BlockSpec((None, ss, D), lambda e, f: (e, 0, 0)),
            pl.BlockSpec((None, sp, 1), lambda e, f: (e, 0, 0)),
            pl.BlockSpec((None, ss, 1), lambda e, f: (e, 0, 0)),
            pl.BlockSpec((None, D, tf), lambda e, f: (e, 0, f)),
            pl.BlockSpec((None, D, tf), lambda e, f: (e, 0, f)),
            pl.BlockSpec((None, tf, D), lambda e, f: (e, f, 0)),
        ],
        out_specs=pl.BlockSpec((None, sp + ss, D), lambda e, f: (e, 0, 0)),
        scratch_shapes=[pltpu.VMEM((sp + ss, D), _BF), pltpu.VMEM((sp + ss, D), _F32)],
        compiler_params=pltpu.CompilerParams(
            dimension_semantics=("arbitrary", "arbitrary"), vmem_limit_bytes=VMEM_LIMIT),
        name="expert_ffn",
    )(xs_p, xs_s, g_p, g_s, w_gate, w_up, w_down)


def _combine_kernel(pos_ref, ye_ref, x1_ref, mod_ref, n4_ref, y_ref, *, n, cap, g, latent):
    width = N_EXP * cap
    lane = lax.broadcasted_iota(jnp.int32, (N_EXP, width), 1)
    expand = jnp.where(lane // cap == lax.broadcasted_iota(jnp.int32, (N_EXP, width), 0),
                       1.0, 0.0).astype(_BF)
    tile = min(n, COMBINE_TILE)
    slot = (lax.broadcasted_iota(jnp.int32, (tile, width), 1) % cap).astype(_F32)
    mod_row = (pl.program_id(0) + 1) if latent else 0
    gain = mod_ref[N_MOD - 1, pl.ds(mod_row, 1), :] * n4_ref[...]

    for j in range(g):
        ye = ye_ref[:, j * cap:(j + 1) * cap, :].reshape(width, D)

        def body(r, carry, j=j, ye=ye):
            rows = pl.ds(pl.multiple_of(r * tile, tile), tile)
            pos_e = _dot_tn(pos_ref[j, :, rows].astype(_BF), expand)
            onehot = jnp.where(pos_e == slot, 1.0, 0.0).astype(_BF)
            f = _dot(onehot, ye)
            y_ref[j, rows, :] = x1_ref[j, rows, :] + _rms(f, gain)
            return carry

        _tile_loop(n // tile, body)


def _combine_call(pos, ye, x1, mod, norm_post, cap, slot_off, latent):
    b, n, _ = x1.shape
    g = max(1, SMALL_STEP_TOKENS // n)
    assert g == 1 or not latent
    blk_off = slot_off // (g * cap)
    return pl.pallas_call(
        functools.partial(_combine_kernel, n=n, cap=cap, g=g, latent=latent),
        out_shape=jax.ShapeDtypeStruct((b, n, D), _F32),
        grid=(b // g,),
        in_specs=[
            pl.BlockSpec((g, N_EXP, n), lambda i: (i, 0, 0)),
            pl.BlockSpec((N_EXP, g * cap, D), lambda i: (0, i + blk_off, 0)),
            pl.BlockSpec((g, n, D), lambda i: (i, 0, 0)),
            pl.BlockSpec((N_MOD, SUBLANES, D), lambda i: (0, 0, 0)),
            pl.BlockSpec((1, D), lambda i: (0, 0)),
        ],
        out_specs=pl.BlockSpec((g, n, D), lambda i: (i, 0, 0)),
        compiler_params=pltpu.CompilerParams(
            dimension_semantics=("arbitrary",), vmem_limit_bytes=VMEM_LIMIT),
        name="combine_latent" if latent else "combine_context",
    )(pos, ye, x1, mod, norm_post)


def _block_diag_gates(wa, wi):
    per_half = LRU_HEADS // 2
    side = per_half * LRU_HD
    on_diag = (np.arange(side)[:, None] // LRU_HD) == (np.arange(side)[None, :] // LRU_HD)

    def bd(w):
        rows = w.reshape(2, 2, side, LRU_HD)
        return jnp.where(on_diag, jnp.tile(rows, (1, 1, 1, per_half)), 0.0)

    return 0.5 * jnp.concatenate([bd(wa), bd(wi)], axis=-1)


def _rope_tables(n):
    rows = n // GRID_W
    row = np.repeat(np.arange(rows, dtype=np.float32), GRID_W)
    col = np.tile(np.arange(GRID_W, dtype=np.float32), rows)
    nf = DK // 4
    freqs = np.float32(ROPE_BASE) ** (-np.arange(nf, dtype=np.float32) / np.float32(nf))
    ang = np.concatenate([row[:, None] * freqs, col[:, None] * freqs], axis=-1).astype(np.float32)
    cos = np.cos(ang)
    sin = np.sin(ang)
    return (jnp.asarray(np.concatenate([cos, cos], axis=-1), _F32),
            jnp.asarray(np.concatenate([-sin, sin], axis=-1), _F32))


def _decay_consts(n):
    heads = np.arange(RET_H, dtype=np.float32)
    f32 = np.float32
    lgf = np.log1p(-np.exp2(-(f32(RET_DECAY_OFFSET_FWD) + heads))).astype(f32)
    lgb = np.log1p(-np.exp2(-(f32(RET_DECAY_OFFSET_BWD) + heads))).astype(f32)
    tab = np.stack([lgf, lgb, np.exp(lgf), np.exp(f32(n) * lgb), np.exp(f32(n - 1) * lgf)], axis=1)
    return [[float(v) for v in row] for row in tab.astype(f32)]


def kernel(x_prompt, x_sample, c, state_lru, state_ret, c_ctx, ada_w, ada_b, norm_mix_pre, norm_mix_post, norm_ffn_pre, norm_ffn_post, w_in, conv_w, conv_b, lru_wa, lru_ba, lru_wi, lru_bi, lru_lambda, w_out, router_w, exp_w_gate, exp_w_up, exp_w_down):
    bp, n_p, _ = x_prompt.shape
    bs, n_s, _ = x_sample.shape
    cap_p = 2 * n_p // N_EXP
    cap_s = 2 * n_s // N_EXP
    l = 0

    mod = _ada_call(c_ctx[None, :], c, ada_w[l], ada_b[l][None, :])

    norms = (norm_mix_pre[l][None], norm_mix_post[l][None], norm_ffn_pre[l][None])
    lru = (conv_w[l], conv_b[l][None], lru_ba[l], lru_bi[l], lru_lambda[l])
    wg = _block_diag_gates(lru_wa[l], lru_wi[l])
    rw = jnp.pad(router_w[l], ((0, 0), (0, LANES - N_EXP)))
    cos2, sin2 = _rope_tables(n_s)

    x1_p, h2_p, lt_p, st_lru, st_ret = _mixer_call(
        x_prompt, mod, norms, w_in[l], lru, wg, w_out[l], rw, latent=False)
    x1_s, h2_s, lt_s = _mixer_call(
        x_sample, mod, norms, w_in[l], lru, wg, w_out[l], rw, latent=True,
        extra=(state_lru, state_ret, cos2, sin2), layer=l)

    pos_p, gate_p, pos_s, gate_s = _route_call(lt_p, lt_s, cap_p, cap_s)
    xs_p, gsl_p = _dispatch_call(pos_p, gate_p, h2_p, cap_p)
    xs_s, gsl_s = _dispatch_call(pos_s, gate_s, h2_s, cap_s)
    ye = _expert_call(xs_p, xs_s, gsl_p, gsl_s, exp_w_gate[l], exp_w_up[l], exp_w_down[l])

    norm_post = norm_ffn_post[l][None]
    y_p = _combine_call(pos_p, ye, x1_p, mod, norm_post, cap_p, 0, latent=False)
    y_s = _combine_call(pos_s, ye, x1_s, mod, norm_post, cap_s, bp * cap_p, latent=True)
    return (y_p, y_s, st_lru, st_ret)
```

```python
import functools
import math
import types

import jax
import jax.numpy as jnp
import numpy as np
from jax import lax
from jax.experimental import pallas as pl
from jax.experimental.pallas import tpu as pltpu

D = 1024
LRU_W = 512
LRU_HEADS = 8
LRU_HD = 64
LRU_C = 8.0
RET_W = 512
RET_H = 4
DK = 128
N_EXP = 16
FF = 2048
N_MOD = 6
EPS = 1e-6
GRID_W = 64
ROPE_BASE = 10000.0
RET_DECAY_OFFSET_FWD = 5.0
RET_DECAY_OFFSET_BWD = 5.5

ROW_TILE = 256
WIDE_TILE = 512
FF_CHUNK = 512
COMBINE_TILE = 1024
CONTEXT_GROUP = 2
SMALL_STEP_TOKENS = 1024
SUBLANES = 8
LANES = 128
VMEM_LIMIT = 60 * 1024 * 1024

DEC_LOG_F, DEC_LOG_B, DEC_G_F, DEC_GN_B, DEC_GN1_F = 0, 1, 2, 3, 4

_BF = jnp.bfloat16
_F32 = jnp.float32


def _sigmoid(x):
    return 0.5 * jnp.tanh(0.5 * x) + 0.5


def _silu(x):
    return x * _sigmoid(x)


def _gelu_tanh(x):
    c = math.sqrt(2.0 / math.pi)
    return 0.5 * x * (1.0 + jnp.tanh(c * (x + 0.044715 * (x * x * x))))


def _rms(x, gain):
    return x * lax.rsqrt(jnp.mean(x * x, axis=-1, keepdims=True) + EPS) * gain


def _dot(a, b):
    return jnp.dot(a, b, preferred_element_type=_F32)


def _dot_nt(a, b):
    return lax.dot_general(a, b, (((1,), (1,)), ((), ())), preferred_element_type=_F32)


def _dot_tn(a, b):
    return lax.dot_general(a, b, (((0,), (0,)), ((), ())), preferred_element_type=_F32)


def _ada_kernel(cc_ref, c_ref, w_ref, b_ref, o_ref, s_ref):
    nb = c_ref.shape[0]
    s_ref[...] = jnp.zeros_like(s_ref)
    s_ref[0:1, :] = _silu(cc_ref[...])
    s_ref[1:1 + nb, :] = _silu(c_ref[...])
    o_ref[...] = _dot(s_ref[...], w_ref[...]) + b_ref[...]


def _ada_call(c_ctx, c, ada_w, ada_b):
    nb = c.shape[0]
    assert nb + 1 <= SUBLANES
    return pl.pallas_call(
        _ada_kernel,
        out_shape=jax.ShapeDtypeStruct((N_MOD, SUBLANES, D), _F32),
        grid=(N_MOD,),
        in_specs=[
            pl.BlockSpec((1, D), lambda j: (0, 0)),
            pl.BlockSpec((nb, D), lambda j: (0, 0)),
            pl.BlockSpec((D, D), lambda j: (0, j)),
            pl.BlockSpec((1, D), lambda j: (0, j)),
        ],
        out_specs=pl.BlockSpec((None, SUBLANES, D), lambda j: (j, 0, 0)),
        scratch_shapes=[pltpu.VMEM((SUBLANES, D), _F32)],
        compiler_params=pltpu.CompilerParams(
            dimension_semantics=("arbitrary",), vmem_limit_bytes=VMEM_LIMIT),
        name="ada_mod",
    )(c_ctx, c, ada_w, ada_b)


def _tile_loop(nt, body, unroll=1):
    if nt == 1:
        body(0, 0)
    else:
        lax.fori_loop(0, nt, body, 0, unroll=unroll)


def _mixer_kernel(*refs, n, g, latent):
    if latent:
        (x_ref, mod_ref, n1_ref, n2_ref, n3_ref, wl_ref, win_hbm,
         cw_ref, cb_ref, ba_ref, bi_ref, lam_ref, wg_hbm, wout_hbm, rw_ref,
         h0_ref, s0_ref, cos_hbm, sin_hbm,
         x1_ref, h2_ref, lt_ref,
         xlp_g, gy_g, af_g, ab_g, hf_g, hb_g, wret_v, wg_ref, cos_ref, sin_ref,
         wout_ref, late_sem) = refs
        sg_g, qf_g, qb_g, kf_g, kb_g, v_g = xlp_g, af_g, ab_g, hf_g, hb_g, gy_g
        wqk_ref, wvg_ref = wret_v.at[:, 0:D], wret_v.at[:, D:2 * D]
    else:
        (x_ref, mod_ref, n1_ref, n2_ref, n3_ref, wl_ref, wqk_ref, wvg_ref,
         cw_ref, cb_ref, ba_ref, bi_ref, lam_ref, wg_ref, wout_hbm, rw_ref,
         x1_ref, h2_ref, lt_ref, stl_ref, str_ref,
         xlp_g, gy_g, af_g, ab_g, hf_g, hb_g, sg_g, qf_g, qb_g, kf_g, kb_g, v_g,
         wout_ref, late_sem) = refs
    early_ret = not latent

    first_step = pl.program_id(0) == 0
    late_out = pltpu.make_async_copy(wout_hbm, wout_ref, late_sem.at[0])
    late_gate, late_ret = [], []
    if latent:
        late_gate = [pltpu.make_async_copy(wg_hbm, wg_ref, late_sem.at[1])]
        late_ret = [
            pltpu.make_async_copy(win_hbm.at[:, pl.ds(D, 2 * D)], wret_v, late_sem.at[2]),
            pltpu.make_async_copy(cos_hbm, cos_ref, late_sem.at[3]),
            pltpu.make_async_copy(sin_hbm, sin_ref, late_sem.at[4]),
        ]

    @pl.when(first_step)
    def _():
        for copy in late_gate + late_ret + [late_out]:
            copy.start()

    seqs = []
    for s in range(g):
        q = types.SimpleNamespace(
            x=x_ref.at[s], x1=x1_ref.at[s], h2=h2_ref.at[s], lt=lt_ref.at[s], mix=x1_ref.at[s],
            xlp=xlp_g.at[s], sg=sg_g.at[s], gy=gy_g.at[s], v=v_g.at[s],
            a_f=af_g.at[s], a_b=ab_g.at[s], h_f=hf_g.at[s], h_b=hb_g.at[s],
            q_f=qf_g.at[s], q_b=qb_g.at[s], k_f=kf_g.at[s], k_b=kb_g.at[s])
        if latent:
            q.h0, q.s0 = h0_ref.at[s], s0_ref.at[s]
        else:
            q.stl, q.str = stl_ref.at[s], str_ref.at[s]
        seqs.append(q)

    nt = n // ROW_TILE
    decay = _decay_consts(n)
    mod_row = (pl.program_id(0) + 1) if latent else 0
    mod = lambda k: mod_ref[k, pl.ds(mod_row, 1), :]
    shift = mod(0)
    scale = n1_ref[...] * (1.0 + mod(1))

    def normed(q, rows):
        return _rms(q.x[rows, :], scale) + shift

    wide = min(n, WIDE_TILE)
    a_tile = wide if latent else ROW_TILE

    def stacked(parts):
        return parts[0] if len(parts) == 1 else jnp.concatenate(parts, axis=0)

    def store_retention(q, r0, rows, pqk, pvg):
        q.sg[rows, :] = _silu(pvg[:, RET_W:])
        tpos = (r0 + lax.broadcasted_iota(jnp.int32, (pqk.shape[0], DK), 0)).astype(_F32)
        if latent:
            cos2 = cos_ref[rows, :]
            sin2 = sin_ref[rows, :]
        for hd in range(RET_H):
            cols = slice(hd * DK, (hd + 1) * DK)
            qh = pqk[:, cols] * (DK ** -0.5)
            kh = pqk[:, RET_W + hd * DK:RET_W + (hd + 1) * DK]
            if latent:
                qh = qh * cos2 + pltpu.roll(qh, DK // 2, axis=1) * sin2
                kh = kh * cos2 + pltpu.roll(kh, DK // 2, axis=1) * sin2
            lgf = decay[hd][DEC_LOG_F]
            lgb = decay[hd][DEC_LOG_B]
            q.q_f[hd, rows, :] = qh * jnp.exp(tpos * lgf)
            q.k_f[hd, rows, :] = kh * jnp.exp(tpos * (-lgf))
            q.q_b[hd, rows, :] = qh * jnp.exp(tpos * (-lgb))
            q.k_b[hd, rows, :] = kh * jnp.exp(tpos * lgb)
        q.v[rows, :] = pvg[:, 0:RET_W]

    def phase_a1(r, carry):
        r0 = pl.multiple_of(r * a_tile, a_tile)
        rows = pl.ds(r0, a_tile)
        h = stacked([normed(q, rows) for q in seqs])
        p_all = _dot(h, wl_ref[...])
        if early_ret:
            pqk_all = _dot(h, wqk_ref[...])
            pvg_all = _dot(h, wvg_ref[...])
        for s, q in enumerate(seqs):
            part = slice(s * a_tile, (s + 1) * a_tile)
            q.xlp[pl.ds(r0 + SUBLANES, a_tile), :] = p_all[part, 0:LRU_W]
            q.gy[rows, :] = _gelu_tanh(p_all[part, LRU_W:])
            if early_ret:
                store_retention(q, r0, rows, pqk_all[part], pvg_all[part])
        return carry

    for q in seqs:
        q.xlp[0:SUBLANES, :] = jnp.zeros((SUBLANES, LRU_W), _F32)
        q.xlp[n + SUBLANES:n + 2 * SUBLANES, :] = jnp.zeros((SUBLANES, LRU_W), _F32)

    half = LRU_W // 2

    def softplus_neg(lam):
        z = -lam
        return jnp.maximum(z, 0.0) + jnp.log1p(jnp.exp(-jnp.abs(z)))

    sp = (softplus_neg(lam_ref[0:1, :]), softplus_neg(lam_ref[1:2, :]))

    def phase_b(q):
        def body(r, carry):
            r0 = pl.multiple_of(r * a_tile, a_tile)
            rows = pl.ds(r0, a_tile)
            ext = q.xlp[pl.ds(r0, a_tile + 2 * SUBLANES), :]
            xc = cb_ref[...]
            for tap in range(4):
                back = (2 - tap) % (a_tile + 2 * SUBLANES)
                win = ext if back == 0 else pltpu.roll(ext, back, axis=0)
                xc = xc + win[SUBLANES:SUBLANES + a_tile, :] * cw_ref[tap:tap + 1, :]
            xh = 0.5 * xc
            for d, (a_ref, u_ref) in enumerate(((q.a_f, q.h_f), (q.a_b, q.h_b))):
                bah = 0.5 * ba_ref[d:d + 1, :]
                bih = 0.5 * bi_ref[d:d + 1, :]
                ch = (-0.5 * LRU_C) * sp[d]
                for hh in range(2):
                    cs = slice(hh * half, (hh + 1) * half)
                    pre = _dot(xc[:, cs], wg_ref[d, hh])
                    t_r = jnp.tanh(pre[:, 0:half] + bah[:, cs])
                    t_i = jnp.tanh(pre[:, half:] + bih[:, cs])
                    log_a = t_r * ch[:, cs] + ch[:, cs]
                    a = jnp.exp(log_a)
                    om = -jnp.tanh(log_a) * (a * a + 1.0)
                    root = jnp.where(om > 0.0, om * lax.rsqrt(om), 0.0)
                    u = root * (t_i * xh[:, cs] + xh[:, cs])
                    for j in range(half // LANES):
                        lanes = slice(j * LANES, (j + 1) * LANES)
                        a_ref[hh * (half // LANES) + j, rows, :] = a[:, lanes]
                        u_ref[hh * (half // LANES) + j, rows, :] = u[:, lanes]
            return carry
        return body

    _tile_loop(n // a_tile, phase_a1)
    if late_gate:
        @pl.when(first_step)
        def _():
            for copy in late_gate:
                copy.wait()

    for q in seqs:
        _tile_loop(n // a_tile, phase_b(q))

    row8 = lax.broadcasted_iota(jnp.int32, (SUBLANES, LANES), 0)
    block = SUBLANES * SUBLANES
    n_blocks = n // block
    n_slabs = LRU_W // LANES

    def across_groups(a, b, reverse):
        for s in (1, 2, 4):
            m = (row8 < SUBLANES - s) if reverse else (row8 >= s)
            shift = SUBLANES - s if reverse else s
            a_s = jnp.where(m, pltpu.roll(a, shift, axis=0), 1.0)
            b_s = jnp.where(m, pltpu.roll(b, shift, axis=0), 0.0)
            b = a * b_s + b
            a = a * a_s
        return a, b

    def scan_block(a_ref, h_ref, base, carry, reverse):
        rows = [pl.ds(base + k, SUBLANES, stride=SUBLANES) for k in range(SUBLANES)]
        order = list(reversed(range(SUBLANES))) if reverse else list(range(SUBLANES))
        prod, local = {}, {}
        prev = None
        for k in order:
            a, u = a_ref[rows[k], :], h_ref[rows[k], :]
            prod[k] = a if prev is None else a * prod[prev]
            local[k] = u if prev is None else a * local[prev] + u
            prev = k
        p_all, h_all = across_groups(prod[prev], local[prev], reverse)
        inner = (row8 < SUBLANES - 1) if reverse else (row8 >= 1)
        shift = SUBLANES - 1 if reverse else 1
        enter = (jnp.where(inner, pltpu.roll(p_all, shift, axis=0), 1.0) * carry
                 + jnp.where(inner, pltpu.roll(h_all, shift, axis=0), 0.0))
        for k in order:
            h_ref[rows[k], :] = prod[k] * enter + local[k]
        leave = p_all * carry + h_all
        return leave[0:1, :] if reverse else leave[SUBLANES - 1:SUBLANES, :]

    def initial(q, d, s):
        if latent:
            return q.h0[d:d + 1, s * LANES:(s + 1) * LANES]
        return jnp.zeros((1, LANES), _F32)

    chains = [(q, s) for q in seqs for s in range(n_slabs)]

    def scan_body(i, carry):
        fwd_base = pl.multiple_of(i * block, block)
        bwd_base = pl.multiple_of((n_blocks - 1 - i) * block, block)
        out = []
        for (q, s), (cf, cb) in zip(chains, carry):
            out.append((scan_block(q.a_f.at[s], q.h_f.at[s], fwd_base, cf, False),
                        scan_block(q.a_b.at[s], q.h_b.at[s], bwd_base, cb, True)))
        return tuple(out)

    ends = lax.fori_loop(0, n_blocks, scan_body,
                         tuple((initial(q, 0, s), initial(q, 1, s)) for q, s in chains), unroll=2)

    if not latent:
        @pl.when(first_step)
        def _():
            late_out.wait()

    def lru_out(q):
        def body(r, carry):
            rows = pl.ds(pl.multiple_of(r * ROW_TILE, ROW_TILE), ROW_TILE)
            for s in range(n_slabs):
                lanes = slice(s * LANES, (s + 1) * LANES)
                q.mix[rows, lanes] = (q.h_f[s, rows, :] + q.h_b[s, rows, :]) * q.gy[rows, lanes]
            return carry
        return body

    for (q, s), (last_f, first_b) in zip(chains, ends):
        if not latent:
            q.stl[0:1, s * LANES:(s + 1) * LANES] = last_f
            q.stl[1:2, s * LANES:(s + 1) * LANES] = first_b
    for q in seqs:
        _tile_loop(nt, lru_out(q))

    def phase_a2(r, carry):
        r0 = pl.multiple_of(r * wide, wide)
        rows = pl.ds(r0, wide)
        h = stacked([normed(q, rows) for q in seqs])
        pqk_all = _dot(h, wqk_ref[...])
        pvg_all = _dot(h, wvg_ref[...])
        for s, q in enumerate(seqs):
            part = slice(s * wide, (s + 1) * wide)
            store_retention(q, r0, rows, pqk_all[part], pvg_all[part])
        return carry

    if not early_ret:
        @pl.when(first_step)
        def _():
            for copy in late_ret:
                copy.wait()

        _tile_loop(n // wide, phase_a2)

    lower = (lax.broadcasted_iota(jnp.int32, (ROW_TILE, ROW_TILE), 0)
             >= lax.broadcasted_iota(jnp.int32, (ROW_TILE, ROW_TILE), 1))
    blocks = [slice(r * ROW_TILE, (r + 1) * ROW_TILE) for r in range(nt)]

    def phase_c(q, hd):
        cols = slice(hd * DK, (hd + 1) * DK)
        dec = lambda row: decay[hd][row]
        kv_f = [_dot_tn(q.k_f[hd, rows, :], q.v[rows, cols]) if (r < nt - 1 or not latent) else 0.0
                for r, rows in enumerate(blocks)]
        kv_b = [_dot_tn(q.k_b[hd, rows, :], q.v[rows, cols]) if (r > 0 or not latent) else 0.0
                for r, rows in enumerate(blocks)]
        if latent:
            run_f = q.s0[0, hd] * dec(DEC_G_F)
            run_b = q.s0[1, hd] * dec(DEC_GN_B)
        else:
            run_f = run_b = None
        before = []
        for r in range(nt):
            before.append(run_f)
            run_f = kv_f[r] if run_f is None else run_f + kv_f[r]
        after = [None] * nt
        for r in reversed(range(nt)):
            after[r] = run_b
            run_b = kv_b[r] if run_b is None else run_b + kv_b[r]

        for r, rows in enumerate(blocks):
            qf = q.q_f[hd, rows, :]
            qb = q.q_b[hd, rows, :]
            s = jnp.where(lower, _dot_nt(qf, q.k_f[hd, rows, :]), _dot_nt(qb, q.k_b[hd, rows, :]))
            o = _dot(s, q.v[rows, cols])
            if before[r] is not None:
                o = o + _dot(qf, before[r])
            if after[r] is not None:
                o = o + _dot(qb, after[r])
            o = o * lax.rsqrt(jnp.mean(o * o, axis=-1, keepdims=True) + EPS)
            q.mix[rows, LRU_W + cols.start:LRU_W + cols.stop] = o * q.sg[rows, cols]
        if not latent:
            q.str[0, hd] = run_f * dec(DEC_GN1_F)
            q.str[1, hd] = run_b

    for hd in range(RET_H):
        for q in seqs:
            phase_c(q, hd)

    gain1 = mod(2) * n2_ref[...]
    gain2 = n3_ref[...] * (1.0 + mod(4))
    sh2 = mod(3)

    def phase_d(r, carry):
        rows = pl.ds(pl.multiple_of(r * wide, wide), wide)
        mix_all = _dot(stacked([q.mix[rows, :] for q in seqs]), wout_ref[...])
        h2s = []
        for s, q in enumerate(seqs):
            x1 = q.x[rows, :] + _rms(mix_all[s * wide:(s + 1) * wide], gain1)
            q.x1[rows, :] = x1
            h2s.append(_rms(x1, gain2) + sh2)
            q.h2[rows, :] = h2s[-1].astype(_BF)
        logits_t = _dot(stacked(h2s), rw_ref[...]).T
        for s, q in enumerate(seqs):
            q.lt[:, rows] = logits_t[0:N_EXP, s * wide:(s + 1) * wide]
        return carry

    if latent:
        @pl.when(first_step)
        def _():
            late_out.wait()

    _tile_loop(n // wide, phase_d, unroll=2)


def _mixer_call(x, mod, norms, w_in, lru, wg, w_out, rw, latent, extra=(), layer=0):
    b, n, _ = x.shape
    g = 1 if latent else CONTEXT_GROUP
    const2 = lambda i: (0, 0)
    whole = pl.BlockSpec(memory_space=pl.ANY)
    in_specs = [
        pl.BlockSpec((g, n, D), lambda i: (i, 0, 0)),
        pl.BlockSpec((N_MOD, SUBLANES, D), lambda i: (0, 0, 0)),
        pl.BlockSpec((1, D), const2), pl.BlockSpec((1, D), const2), pl.BlockSpec((1, D), const2),
        pl.BlockSpec((D, D), lambda i: (0, 0)),
    ]
    if latent:
        in_specs += [whole]
    else:
        in_specs += [pl.BlockSpec((D, D), lambda i: (0, 1)),
                     pl.BlockSpec((D, D), lambda i: (0, 2))]
    in_specs += [
        pl.BlockSpec((4, LRU_W), const2), pl.BlockSpec((1, LRU_W), const2),
        pl.BlockSpec((2, LRU_W), const2), pl.BlockSpec((2, LRU_W), const2),
        pl.BlockSpec((2, LRU_W), const2),
        whole if latent else pl.BlockSpec((2, 2, LRU_W // 2, LRU_W), lambda i: (0, 0, 0, 0)),
        whole,
        pl.BlockSpec((D, LANES), const2),
    ]
    out_shape = [
        jax.ShapeDtypeStruct((b, n, D), _F32),
        jax.ShapeDtypeStruct((b, n, D), _BF),
        jax.ShapeDtypeStruct((b, N_EXP, n), _F32),
    ]
    out_specs = [
        pl.BlockSpec((g, n, D), lambda i: (i, 0, 0)),
        pl.BlockSpec((g, n, D), lambda i: (i, 0, 0)),
        pl.BlockSpec((g, N_EXP, n), lambda i: (i, 0, 0)),
    ]
    if latent:
        in_specs += [
            pl.BlockSpec((g, None, 2, LRU_W), lambda i: (i, layer, 0, 0)),
            pl.BlockSpec((g, None, 2, RET_H, DK, DK), lambda i: (i, layer, 0, 0, 0, 0)),
            whole, whole,
        ]
    else:
        out_shape += [
            jax.ShapeDtypeStruct((b, 1, 2, LRU_W), _F32),
            jax.ShapeDtypeStruct((b, 1, 2, RET_H, DK, DK), _F32),
        ]
        out_specs += [
            pl.BlockSpec((g, None, 2, LRU_W), lambda i: (i, 0, 0, 0)),
            pl.BlockSpec((g, None, 2, RET_H, DK, DK), lambda i: (i, 0, 0, 0, 0, 0)),
        ]
    f32s = lambda shape: pltpu.VMEM((g,) + shape, _F32)
    slabs = (LRU_W // LANES, n, LANES)
    scratch = [
        f32s((n + 2 * SUBLANES, LRU_W)),
        f32s((n, LRU_W)),
        f32s(slabs), f32s(slabs),
        f32s(slabs), f32s(slabs),
    ]
    if latent:
        scratch += [pltpu.VMEM((D, 2 * D), _F32), pltpu.VMEM(wg.shape, _F32),
                    pltpu.VMEM((n, DK), _F32), pltpu.VMEM((n, DK), _F32)]
        w_in_args = (w_in, w_in)
    else:
        scratch += [f32s((n, RET_W))] + [f32s(slabs) for _ in range(4)] + [f32s((n, RET_W))]
        w_in_args = (w_in, w_in, w_in)
    scratch += [pltpu.VMEM((D, D), _F32), pltpu.SemaphoreType.DMA((5 if latent else 1,))]
    return pl.pallas_call(
        functools.partial(_mixer_kernel, n=n, g=g, latent=latent),
        out_shape=out_shape,
        grid=(b // g,),
        in_specs=in_specs,
        out_specs=out_specs,
        scratch_shapes=scratch,
        compiler_params=pltpu.CompilerParams(
            dimension_semantics=("arbitrary",), vmem_limit_bytes=VMEM_LIMIT),
        name="mixer_latent" if latent else "mixer_context",
    )(x, mod, *norms, *w_in_args, *lru, wg, w_out, rw, *extra)


def _count(mask):
    return jnp.sum(jnp.where(mask, 1.0, 0.0), axis=-1, keepdims=True)


def _probs(l3):
    bsz, _, n = l3.shape
    m = jnp.max(l3, axis=1, keepdims=True)
    e = jnp.exp(l3 - m)
    return (e / jnp.sum(e, axis=1, keepdims=True)).reshape(bsz * N_EXP, n)


def _break_ties(parts):
    prep = []
    for bits, thr, cap in parts:
        rows, n = bits.shape
        eq = bits == thr
        need = float(cap) - _count(bits > thr)
        idx = lax.broadcasted_iota(jnp.int32, (rows, n), 1)
        prep.append((eq, need, idx, int(math.log2(n))))
    most = max(nbits for _, _, _, nbits in prep)

    def idx_body(i, last):
        out = []
        for (eq, need, idx, nbits), j in zip(prep, last):
            shift = nbits - 1 - i
            cand = j | jnp.where(shift >= 0, jnp.int32(1) << jnp.maximum(shift, 0), 0)
            out.append(jnp.where(_count(eq & (idx < cand)) < need, cand, j))
        return tuple(out)

    return lax.fori_loop(0, most, idx_body,
                         tuple(jnp.zeros((eq.shape[0], 1), jnp.int32) for eq, _, _, _ in prep))


def _slots(p, bits, thr, jlast):
    rows, n = p.shape
    idx = lax.broadcasted_iota(jnp.int32, (rows, n), 1)
    sel = (bits > thr) | ((bits == thr) & (idx <= jlast))
    before = (lax.broadcasted_iota(jnp.int32, (n, n), 0)
              < lax.broadcasted_iota(jnp.int32, (n, n), 1))
    pos = _dot(jnp.where(sel, 1.0, 0.0).astype(_BF), jnp.where(before, 1.0, 0.0).astype(_BF))
    return jnp.where(sel, pos, -1.0), jnp.where(sel, p, 0.0)


def _route_kernel(lp_ref, ls_ref, pp_ref, gp_ref, ps_ref, gs_ref, *, cap_p, cap_s):
    groups = ((_probs(lp_ref[...]), float(cap_p)), (_probs(ls_ref[...]), float(cap_s)))
    bits = [pltpu.bitcast(p, jnp.int32) for p, _ in groups]

    def settle(b, t, capf, hi, lo):
        with_hi = t | hi
        both = with_hi | lo
        with_lo = t | lo
        ok = lambda cand: _count(b >= cand) >= capf
        return jnp.where(ok(with_hi), jnp.where(ok(both), both, with_hi),
                         jnp.where(ok(with_lo), with_lo, t))

    def val_body(i, thr):
        hi = jnp.int32(1) << (29 - 2 * i)
        lo = jnp.int32(1) << (28 - 2 * i)
        return tuple(settle(b, t, capf, hi, lo) for b, t, (_, capf) in zip(bits, thr, groups))

    thr = lax.fori_loop(0, 15, val_body,
                        tuple(jnp.zeros((b.shape[0], 1), jnp.int32) for b in bits))
    jlast = _break_ties([(bits[0], thr[0], cap_p), (bits[1], thr[1], cap_s)])
    pos, gate = _slots(groups[0][0], bits[0], thr[0], jlast[0])
    pp_ref[...] = pos.reshape(pp_ref.shape)
    gp_ref[...] = gate.reshape(gp_ref.shape)
    pos, gate = _slots(groups[1][0], bits[1], thr[1], jlast[1])
    ps_ref[...] = pos.reshape(ps_ref.shape)
    gs_ref[...] = gate.reshape(gs_ref.shape)


def _route_call(lt_p, lt_s, cap_p, cap_s):
    shapes = [
        jax.ShapeDtypeStruct(lt_p.shape, _F32), jax.ShapeDtypeStruct(lt_p.shape, _F32),
        jax.ShapeDtypeStruct(lt_s.shape, _F32), jax.ShapeDtypeStruct(lt_s.shape, _F32),
    ]
    return pl.pallas_call(
        functools.partial(_route_kernel, cap_p=cap_p, cap_s=cap_s),
        out_shape=shapes,
        compiler_params=pltpu.CompilerParams(vmem_limit_bytes=VMEM_LIMIT),
        name="route_select",
    )(lt_p, lt_s)


def _dispatch_kernel(pos_ref, gate_ref, h_ref, xs_ref, gs_ref, *, n, cap, g):
    slot = lax.broadcasted_iota(jnp.int32, (cap, n), 0).astype(_F32)
    for j in range(g):
        slots = slice(j * cap, (j + 1) * cap)
        parts = []
        for e in range(N_EXP):
            hit = pos_ref[j, e:e + 1, :] == slot
            parts.append(jnp.where(hit, 1.0, 0.0).astype(_BF))
            gs_ref[e, slots, :] = jnp.sum(jnp.where(hit, gate_ref[j, e:e + 1, :], 0.0),
                                          axis=-1, keepdims=True)
        onehot = jnp.concatenate(parts, axis=0)
        xs = _dot(onehot, h_ref[j]).astype(_BF)
        for e in range(N_EXP):
            xs_ref[e, slots, :] = xs[e * cap:(e + 1) * cap, :]


def _dispatch_call(pos, gate, h2, cap):
    b, n, _ = h2.shape
    g = max(1, SMALL_STEP_TOKENS // n)
    return pl.pallas_call(
        functools.partial(_dispatch_kernel, n=n, cap=cap, g=g),
        out_shape=[
            jax.ShapeDtypeStruct((N_EXP, b * cap, D), _BF),
            jax.ShapeDtypeStruct((N_EXP, b * cap, 1), _F32),
        ],
        grid=(b // g,),
        in_specs=[
            pl.BlockSpec((g, N_EXP, n), lambda i: (i, 0, 0)),
            pl.BlockSpec((g, N_EXP, n), lambda i: (i, 0, 0)),
            pl.BlockSpec((g, n, D), lambda i: (i, 0, 0)),
        ],
        out_specs=[
            pl.BlockSpec((N_EXP, g * cap, D), lambda i: (0, i, 0)),
            pl.BlockSpec((N_EXP, g * cap, 1), lambda i: (0, i, 0)),
        ],
        compiler_params=pltpu.CompilerParams(
            dimension_semantics=("arbitrary",), vmem_limit_bytes=VMEM_LIMIT),
        name="dispatch",
    )(pos, gate, h2)


def _expert_kernel(xp_ref, xs_ref, gp_ref, gs_ref, wg_ref, wu_ref, wd_ref, y_ref, xcat, acc,
                   *, sp, nf, tf):
    f = pl.program_id(1)
    xcat[0:sp, :] = xp_ref[...]
    xcat[sp:, :] = xs_ref[...]
    x = xcat[...]
    total = jnp.where(f == 0, 0.0, acc[...])
    for c in range(tf // FF_CHUNK):
        cs = slice(c * FF_CHUNK, (c + 1) * FF_CHUNK)
        hg = _dot(x, wg_ref[:, cs].astype(_BF))
        hu = _dot(x, wu_ref[:, cs].astype(_BF))
        hid = (_silu(hg) * hu).astype(_BF)
        total = total + _dot(hid, wd_ref[cs, :].astype(_BF))
    acc[...] = total
    y_ref[0:sp, :] = (total[0:sp, :] * gp_ref[...]).astype(_BF)
    y_ref[sp:, :] = (total[sp:, :] * gs_ref[...]).astype(_BF)


def _expert_call(xs_p, xs_s, g_p, g_s, w_gate, w_up, w_down):
    tf = 1024
    sp = xs_p.shape[1]
    ss = xs_s.shape[1]
    nf = FF // tf
    return pl.pallas_call(
        functools.partial(_expert_kernel, sp=sp, nf=nf, tf=tf),
        out_shape=jax.ShapeDtypeStruct((N_EXP, sp + ss, D), _BF),
        grid=(N_EXP, nf),
        in_specs=[
            pl.BlockSpec((None, sp, D), lambda e, f: (e, 0, 0)),
            pl.BlockSpec((None, ss, D), lambda e, f: (e, 0, 0)),
            pl.BlockSpec((None, sp, 1), lambda e, f: (e, 0, 0)),
            pl.BlockSpec((None, ss, 1), lambda e, f: (e, 0, 0)),
            pl.BlockSpec((None, D, tf), lambda e, f: (e, 0, f)),
            pl.BlockSpec((None, D, tf), lambda e, f: (e, 0, f)),
            pl.BlockSpec((None, tf, D), lambda e, f: (e, f, 0)),
        ],
        out_specs=pl.BlockSpec((None, sp + ss, D), lambda e, f: (e, 0, 0)),
        scratch_shapes=[pltpu.VMEM((sp + ss, D), _BF), pltpu.VMEM((sp + ss, D), _F32)],
        compiler_params=pltpu.CompilerParams(
            dimension_semantics=("arbitrary", "arbitrary"), vmem_limit_bytes=VMEM_LIMIT),
        name="expert_ffn",
    )(xs_p, xs_s, g_p, g_s, w_gate, w_up, w_down)


def _combine_kernel(pos_ref, ye_ref, x1_ref, mod_ref, n4_ref, y_ref, *, n, cap, g, latent):
    width = N_EXP * cap
    lane = lax.broadcasted_iota(jnp.int32, (N_EXP, width), 1)
    expand = jnp.where(lane // cap == lax.broadcasted_iota(jnp.int32, (N_EXP, width), 0),
                       1.0, 0.0).astype(_BF)
    tile = min(n, COMBINE_TILE)
    slot = (lax.broadcasted_iota(jnp.int32, (tile, width), 1) % cap).astype(_F32)
    mod_row = (pl.program_id(0) + 1) if latent else 0
    gain = mod_ref[N_MOD - 1, pl.ds(mod_row, 1), :] * n4_ref[...]

    for j in range(g):
        ye = ye_ref[:, j * cap:(j + 1) * cap, :].reshape(width, D)

        def body(r, carry, j=j, ye=ye):
            rows = pl.ds(pl.multiple_of(r * tile, tile), tile)
            pos_e = _dot_tn(pos_ref[j, :, rows].astype(_BF), expand)
            onehot = jnp.where(pos_e == slot, 1.0, 0.0).astype(_BF)
            f = _dot(onehot, ye)
            y_ref[j, rows, :] = x1_ref[j, rows, :] + _rms(f, gain)
            return carry

        _tile_loop(n // tile, body)


def _combine_call(pos, ye, x1, mod, norm_post, cap, slot_off, latent):
    b, n, _ = x1.shape
    g = max(1, SMALL_STEP_TOKENS // n)
    assert g == 1 or not latent
    blk_off = slot_off // (g * cap)
    return pl.pallas_call(
        functools.partial(_combine_kernel, n=n, cap=cap, g=g, latent=latent),
        out_shape=jax.ShapeDtypeStruct((b, n, D), _F32),
        grid=(b // g,),
        in_specs=[
            pl.BlockSpec((g, N_EXP, n), lambda i: (i, 0, 0)),
            pl.BlockSpec((N_EXP, g * cap, D), lambda i: (0, i + blk_off, 0)),
            pl.BlockSpec((g, n, D), lambda i: (i, 0, 0)),
            pl.BlockSpec((N_MOD, SUBLANES, D), lambda i: (0, 0, 0)),
            pl.BlockSpec((1, D), lambda i: (0, 0)),
        ],
        out_specs=pl.BlockSpec((g, n, D), lambda i: (i, 0, 0)),
        compiler_params=pltpu.CompilerParams(
            dimension_semantics=("arbitrary",), vmem_limit_bytes=VMEM_LIMIT),
        name="combine_latent" if latent else "combine_context",
    )(pos, ye, x1, mod, norm_post)


def _block_diag_gates(wa, wi):
    per_half = LRU_HEADS // 2
    side = per_half * LRU_HD
    on_diag = (np.arange(side)[:, None] // LRU_HD) == (np.arange(side)[None, :] // LRU_HD)

    def bd(w):
        rows = w.reshape(2, 2, side, LRU_HD)
        return jnp.where(on_diag, jnp.tile(rows, (1, 1, 1, per_half)), 0.0)

    return 0.5 * jnp.concatenate([bd(wa), bd(wi)], axis=-1)


def _rope_tables(n):
    rows = n // GRID_W
    row = np.repeat(np.arange(rows, dtype=np.float32), GRID_W)
    col = np.tile(np.arange(GRID_W, dtype=np.float32), rows)
    nf = DK // 4
    freqs = np.float32(ROPE_BASE) ** (-np.arange(nf, dtype=np.float32) / np.float32(nf))
    ang = np.concatenate([row[:, None] * freqs, col[:, None] * freqs], axis=-1).astype(np.float32)
    cos = np.cos(ang)
    sin = np.sin(ang)
    return (jnp.asarray(np.concatenate([cos, cos], axis=-1), _F32),
            jnp.asarray(np.concatenate([-sin, sin], axis=-1), _F32))


def _decay_consts(n):
    heads = np.arange(RET_H, dtype=np.float32)
    f32 = np.float32
    lgf = np.log1p(-np.exp2(-(f32(RET_DECAY_OFFSET_FWD) + heads))).astype(f32)
    lgb = np.log1p(-np.exp2(-(f32(RET_DECAY_OFFSET_BWD) + heads))).astype(f32)
    tab = np.stack([lgf, lgb, np.exp(lgf), np.exp(f32(n) * lgb), np.exp(f32(n - 1) * lgf)], axis=1)
    return [[float(v) for v in row] for row in tab.astype(f32)]


def kernel(x_prompt, x_sample, c, state_lru, state_ret, c_ctx, ada_w, ada_b, norm_mix_pre, norm_mix_post, norm_ffn_pre, norm_ffn_post, w_in, conv_w, conv_b, lru_wa, lru_ba, lru_wi, lru_bi, lru_lambda, w_out, router_w, exp_w_gate, exp_w_up, exp_w_down):
    bp, n_p, _ = x_prompt.shape
    bs, n_s, _ = x_sample.shape
    cap_p = 2 * n_p // N_EXP
    cap_s = 2 * n_s // N_EXP
    l = 0

    mod = _ada_call(c_ctx[None, :], c, ada_w[l], ada_b[l][None, :])

    norms = (norm_mix_pre[l][None], norm_mix_post[l][None], norm_ffn_pre[l][None])
    lru = (conv_w[l], conv_b[l][None], lru_ba[l], lru_bi[l], lru_lambda[l])
    wg = _block_diag_gates(lru_wa[l], lru_wi[l])
    rw = jnp.pad(router_w[l], ((0, 0), (0, LANES - N_EXP)))
    cos2, sin2 = _rope_tables(n_s)

    x1_p, h2_p, lt_p, st_lru, st_ret = _mixer_call(
        x_prompt, mod, norms, w_in[l], lru, wg, w_out[l], rw, latent=False)
    x1_s, h2_s, lt_s = _mixer_call(
        x_sample, mod, norms, w_in[l], lru, wg, w_out[l], rw, latent=True,
        extra=(state_lru, state_ret, cos2, sin2), layer=l)

    pos_p, gate_p, pos_s, gate_s = _route_call(lt_p, lt_s, cap_p, cap_s)
    xs_p, gsl_p = _dispatch_call(pos_p, gate_p, h2_p, cap_p)
    xs_s, gsl_s = _dispatch_call(pos_s, gate_s, h2_s, cap_s)
    ye = _expert_call(xs_p, xs_s, gsl_p, gsl_s, exp_w_gate[l], exp_w_up[l], exp_w_down[l])

    norm_post = norm_ffn_post[l][None]
    y_p = _combine_call(pos_p, ye, x1_p, mod, norm_post, cap_p, 0, latent=False)
    y_s = _combine_call(pos_s, ye, x1_s, mod, norm_post, cap_s, bp * cap_p, latent=True)
    return (y_p, y_s, st_lru, st_ret)
```

```python
import functools
import math
import types

import jax
import jax.numpy as jnp
import numpy as np
from jax import lax
from jax.experimental import pallas as pl
from jax.experimental.pallas import tpu as pltpu

D = 1024
LRU_W = 512
LRU_HEADS = 8
LRU_HD = 64
LRU_C = 8.0
RET_W = 512
RET_H = 4
DK = 128
N_EXP = 16
FF = 2048
N_MOD = 6
EPS = 1e-6
GRID_W = 64
ROPE_BASE = 10000.0
RET_DECAY_OFFSET_FWD = 5.0
RET_DECAY_OFFSET_BWD = 5.5

ROW_TILE = 256
WIDE_TILE = 512
FF_CHUNK = 512
CONTEXT_GROUP = 2
SMALL_STEP_TOKENS = 1024
SUBLANES = 8
LANES = 128
VMEM_LIMIT = 60 * 1024 * 1024

DEC_LOG_F, DEC_LOG_B, DEC_G_F, DEC_GN_B, DEC_GN1_F = 0, 1, 2, 3, 4

_BF = jnp.bfloat16
_F32 = jnp.float32


def _sigmoid(x):
    return 0.5 * jnp.tanh(0.5 * x) + 0.5


def _silu(x):
    return x * _sigmoid(x)


def _gelu_tanh(x):
    c = math.sqrt(2.0 / math.pi)
    return 0.5 * x * (1.0 + jnp.tanh(c * (x + 0.044715 * (x * x * x))))


def _rms(x, gain):
    return x * lax.rsqrt(jnp.mean(x * x, axis=-1, keepdims=True) + EPS) * gain


def _dot(a, b):
    return jnp.dot(a, b, preferred_element_type=_F32)


def _dot_nt(a, b):
    return lax.dot_general(a, b, (((1,), (1,)), ((), ())), preferred_element_type=_F32)


def _dot_tn(a, b):
    return lax.dot_general(a, b, (((0,), (0,)), ((), ())), preferred_element_type=_F32)


def _ada_kernel(cc_ref, c_ref, w_ref, b_ref, o_ref, s_ref):
    nb = c_ref.shape[0]
    s_ref[...] = jnp.zeros_like(s_ref)
    s_ref[0:1, :] = _silu(cc_ref[...])
    s_ref[1:1 + nb, :] = _silu(c_ref[...])
    o_ref[...] = _dot(s_ref[...], w_ref[...]) + b_ref[...]


def _ada_call(c_ctx, c, ada_w, ada_b):
    nb = c.shape[0]
    assert nb + 1 <= SUBLANES
    return pl.pallas_call(
        _ada_kernel,
        out_shape=jax.ShapeDtypeStruct((N_MOD, SUBLANES, D), _F32),
        grid=(N_MOD,),
        in_specs=[
            pl.BlockSpec((1, D), lambda j: (0, 0)),
            pl.BlockSpec((nb, D), lambda j: (0, 0)),
            pl.BlockSpec((D, D), lambda j: (0, j)),
            pl.BlockSpec((1, D), lambda j: (0, j)),
        ],
        out_specs=pl.BlockSpec((None, SUBLANES, D), lambda j: (j, 0, 0)),
        scratch_shapes=[pltpu.VMEM((SUBLANES, D), _F32)],
        compiler_params=pltpu.CompilerParams(
            dimension_semantics=("arbitrary",), vmem_limit_bytes=VMEM_LIMIT),
        name="ada_mod",
    )(c_ctx, c, ada_w, ada_b)


def _tile_loop(nt, body, unroll=1):
    if nt == 1:
        body(0, 0)
    else:
        lax.fori_loop(0, nt, body, 0, unroll=unroll)


def _mixer_kernel(*refs, n, g, latent):
    if latent:
        (x_ref, mod_ref, n1_ref, n2_ref, n3_ref, wl_ref, win_hbm,
         cw_ref, cb_ref, ba_ref, bi_ref, lam_ref, wg_hbm, wout_hbm, rw_ref,
         h0_ref, s0_ref, cos_hbm, sin_hbm,
         x1_ref, h2_ref, lt_ref,
         xlp_g, gy_g, af_g, ab_g, hf_g, hb_g, wret_v, wg_ref, cos_ref, sin_ref,
         wout_ref, late_sem) = refs
        sg_g, qf_g, qb_g, kf_g, kb_g, v_g = xlp_g, af_g, ab_g, hf_g, hb_g, gy_g
        wqk_ref, wvg_ref = wret_v.at[:, 0:D], wret_v.at[:, D:2 * D]
    else:
        (x_ref, mod_ref, n1_ref, n2_ref, n3_ref, wl_ref, wqk_ref, wvg_ref,
         cw_ref, cb_ref, ba_ref, bi_ref, lam_ref, wg_ref, wout_hbm, rw_ref,
         x1_ref, h2_ref, lt_ref, stl_ref, str_ref,
         xlp_g, gy_g, af_g, ab_g, hf_g, hb_g, sg_g, qf_g, qb_g, kf_g, kb_g, v_g,
         wout_ref, late_sem) = refs
    early_ret = not latent

    first_step = pl.program_id(0) == 0
    late_out = pltpu.make_async_copy(wout_hbm, wout_ref, late_sem.at[0])
    late_gate, late_ret = [], []
    if latent:
        late_gate = [pltpu.make_async_copy(wg_hbm, wg_ref, late_sem.at[1])]
        late_ret = [
            pltpu.make_async_copy(win_hbm.at[:, pl.ds(D, 2 * D)], wret_v, late_sem.at[2]),
            pltpu.make_async_copy(cos_hbm, cos_ref, late_sem.at[3]),
            pltpu.make_async_copy(sin_hbm, sin_ref, late_sem.at[4]),
        ]

    @pl.when(first_step)
    def _():
        for copy in late_gate + late_ret + [late_out]:
            copy.start()

    seqs = []
    for s in range(g):
        q = types.SimpleNamespace(
            x=x_ref.at[s], x1=x1_ref.at[s], h2=h2_ref.at[s], lt=lt_ref.at[s], mix=x1_ref.at[s],
            xlp=xlp_g.at[s], sg=sg_g.at[s], gy=gy_g.at[s], v=v_g.at[s],
            a_f=af_g.at[s], a_b=ab_g.at[s], h_f=hf_g.at[s], h_b=hb_g.at[s],
            q_f=qf_g.at[s], q_b=qb_g.at[s], k_f=kf_g.at[s], k_b=kb_g.at[s])
        if latent:
            q.h0, q.s0 = h0_ref.at[s], s0_ref.at[s]
        else:
            q.stl, q.str = stl_ref.at[s], str_ref.at[s]
        seqs.append(q)

    nt = n // ROW_TILE
    decay = _decay_consts(n)
    mod_row = (pl.program_id(0) + 1) if latent else 0
    mod = lambda k: mod_ref[k, pl.ds(mod_row, 1), :]
    shift = mod(0)
    scale = n1_ref[...] * (1.0 + mod(1))

    def normed(q, rows):
        return _rms(q.x[rows, :], scale) + shift

    wide = min(n, WIDE_TILE)
    a_tile = wide if latent else ROW_TILE

    def stacked(parts):
        return parts[0] if len(parts) == 1 else jnp.concatenate(parts, axis=0)

    def store_retention(q, r0, rows, pqk, pvg):
        q.sg[rows, :] = _silu(pvg[:, RET_W:])
        tpos = (r0 + lax.broadcasted_iota(jnp.int32, (pqk.shape[0], DK), 0)).astype(_F32)
        if latent:
            cos2 = cos_ref[rows, :]
            sin2 = sin_ref[rows, :]
        for hd in range(RET_H):
            cols = slice(hd * DK, (hd + 1) * DK)
            qh = pqk[:, cols] * (DK ** -0.5)
            kh = pqk[:, RET_W + hd * DK:RET_W + (hd + 1) * DK]
            if latent:
                qh = qh * cos2 + pltpu.roll(qh, DK // 2, axis=1) * sin2
                kh = kh * cos2 + pltpu.roll(kh, DK // 2, axis=1) * sin2
            lgf = decay[hd][DEC_LOG_F]
            lgb = decay[hd][DEC_LOG_B]
            q.q_f[hd, rows, :] = qh * jnp.exp(tpos * lgf)
            q.k_f[hd, rows, :] = kh * jnp.exp(tpos * (-lgf))
            q.q_b[hd, rows, :] = qh * jnp.exp(tpos * (-lgb))
            q.k_b[hd, rows, :] = kh * jnp.exp(tpos * lgb)
        q.v[rows, :] = pvg[:, 0:RET_W]

    def phase_a1(r, carry):
        r0 = pl.multiple_of(r * a_tile, a_tile)
        rows = pl.ds(r0, a_tile)
        h = stacked([normed(q, rows) for q in seqs])
        p_all = _dot(h, wl_ref[...])
        if early_ret:
            pqk_all = _dot(h, wqk_ref[...])
            pvg_all = _dot(h, wvg_ref[...])
        for s, q in enumerate(seqs):
            part = slice(s * a_tile, (s + 1) * a_tile)
            q.xlp[pl.ds(r0 + SUBLANES, a_tile), :] = p_all[part, 0:LRU_W]
            q.gy[rows, :] = _gelu_tanh(p_all[part, LRU_W:])
            if early_ret:
                store_retention(q, r0, rows, pqk_all[part], pvg_all[part])
        return carry

    for q in seqs:
        q.xlp[0:SUBLANES, :] = jnp.zeros((SUBLANES, LRU_W), _F32)
        q.xlp[n + SUBLANES:n + 2 * SUBLANES, :] = jnp.zeros((SUBLANES, LRU_W), _F32)

    half = LRU_W // 2

    def softplus_neg(lam):
        z = -lam
        return jnp.maximum(z, 0.0) + jnp.log1p(jnp.exp(-jnp.abs(z)))

    sp = (softplus_neg(lam_ref[0:1, :]), softplus_neg(lam_ref[1:2, :]))

    def phase_b(q):
        def body(r, carry):
            r0 = pl.multiple_of(r * a_tile, a_tile)
            rows = pl.ds(r0, a_tile)
            ext = q.xlp[pl.ds(r0, a_tile + 2 * SUBLANES), :]
            xc = cb_ref[...]
            for tap in range(4):
                back = (2 - tap) % (a_tile + 2 * SUBLANES)
                win = ext if back == 0 else pltpu.roll(ext, back, axis=0)
                xc = xc + win[SUBLANES:SUBLANES + a_tile, :] * cw_ref[tap:tap + 1, :]
            xh = 0.5 * xc
            for d, (a_ref, u_ref) in enumerate(((q.a_f, q.h_f), (q.a_b, q.h_b))):
                bah = 0.5 * ba_ref[d:d + 1, :]
                bih = 0.5 * bi_ref[d:d + 1, :]
                ch = (-0.5 * LRU_C) * sp[d]
                for hh in range(2):
                    cs = slice(hh * half, (hh + 1) * half)
                    pre = _dot(xc[:, cs], wg_ref[d, hh])
                    t_r = jnp.tanh(pre[:, 0:half] + bah[:, cs])
                    t_i = jnp.tanh(pre[:, half:] + bih[:, cs])
                    log_a = t_r * ch[:, cs] + ch[:, cs]
                    a = jnp.exp(log_a)
                    om = -jnp.tanh(log_a) * (a * a + 1.0)
                    root = jnp.where(om > 0.0, om * lax.rsqrt(om), 0.0)
                    u = root * (t_i * xh[:, cs] + xh[:, cs])
                    for j in range(half // LANES):
                        lanes = slice(j * LANES, (j + 1) * LANES)
                        a_ref[hh * (half // LANES) + j, rows, :] = a[:, lanes]
                        u_ref[hh * (half // LANES) + j, rows, :] = u[:, lanes]
            return carry
        return body

    _tile_loop(n // a_tile, phase_a1)
    if late_gate:
        @pl.when(first_step)
        def _():
            for copy in late_gate:
                copy.wait()

    for q in seqs:
        _tile_loop(n // a_tile, phase_b(q))

    row8 = lax.broadcasted_iota(jnp.int32, (SUBLANES, LANES), 0)
    block = SUBLANES * SUBLANES
    n_blocks = n // block
    n_slabs = LRU_W // LANES

    def across_groups(a, b, reverse):
        for s in (1, 2, 4):
            m = (row8 < SUBLANES - s) if reverse else (row8 >= s)
            shift = SUBLANES - s if reverse else s
            a_s = jnp.where(m, pltpu.roll(a, shift, axis=0), 1.0)
            b_s = jnp.where(m, pltpu.roll(b, shift, axis=0), 0.0)
            b = a * b_s + b
            a = a * a_s
        return a, b

    def scan_block(a_ref, h_ref, base, carry, reverse):
        rows = [pl.ds(base + k, SUBLANES, stride=SUBLANES) for k in range(SUBLANES)]
        order = list(reversed(range(SUBLANES))) if reverse else list(range(SUBLANES))
        prod, local = {}, {}
        prev = None
        for k in order:
            a, u = a_ref[rows[k], :], h_ref[rows[k], :]
            prod[k] = a if prev is None else a * prod[prev]
            local[k] = u if prev is None else a * local[prev] + u
            prev = k
        p_all, h_all = across_groups(prod[prev], local[prev], reverse)
        inner = (row8 < SUBLANES - 1) if reverse else (row8 >= 1)
        shift = SUBLANES - 1 if reverse else 1
        enter = (jnp.where(inner, pltpu.roll(p_all, shift, axis=0), 1.0) * carry
                 + jnp.where(inner, pltpu.roll(h_all, shift, axis=0), 0.0))
        for k in order:
            h_ref[rows[k], :] = prod[k] * enter + local[k]
        leave = p_all * carry + h_all
        return leave[0:1, :] if reverse else leave[SUBLANES - 1:SUBLANES, :]

    def initial(q, d, s):
        if latent:
            return q.h0[d:d + 1, s * LANES:(s + 1) * LANES]
        return jnp.zeros((1, LANES), _F32)

    chains = [(q, s) for q in seqs for s in range(n_slabs)]

    def scan_body(i, carry):
        fwd_base = pl.multiple_of(i * block, block)
        bwd_base = pl.multiple_of((n_blocks - 1 - i) * block, block)
        out = []
        for (q, s), (cf, cb) in zip(chains, carry):
            out.append((scan_block(q.a_f.at[s], q.h_f.at[s], fwd_base, cf, False),
                        scan_block(q.a_b.at[s], q.h_b.at[s], bwd_base, cb, True)))
        return tuple(out)

    ends = lax.fori_loop(0, n_blocks, scan_body,
                         tuple((initial(q, 0, s), initial(q, 1, s)) for q, s in chains), unroll=2)

    if not latent:
        @pl.when(first_step)
        def _():
            late_out.wait()

    def lru_out(q):
        def body(r, carry):
            rows = pl.ds(pl.multiple_of(r * ROW_TILE, ROW_TILE), ROW_TILE)
            for s in range(n_slabs):
                lanes = slice(s * LANES, (s + 1) * LANES)
                q.mix[rows, lanes] = (q.h_f[s, rows, :] + q.h_b[s, rows, :]) * q.gy[rows, lanes]
            return carry
        return body

    for (q, s), (last_f, first_b) in zip(chains, ends):
        if not latent:
            q.stl[0:1, s * LANES:(s + 1) * LANES] = last_f
            q.stl[1:2, s * LANES:(s + 1) * LANES] = first_b
    for q in seqs:
        _tile_loop(nt, lru_out(q))

    def phase_a2(r, carry):
        r0 = pl.multiple_of(r * wide, wide)
        rows = pl.ds(r0, wide)
        h = stacked([normed(q, rows) for q in seqs])
        pqk_all = _dot(h, wqk_ref[...])
        pvg_all = _dot(h, wvg_ref[...])
        for s, q in enumerate(seqs):
            part = slice(s * wide, (s + 1) * wide)
            store_retention(q, r0, rows, pqk_all[part], pvg_all[part])
        return carry

    if not early_ret:
        @pl.when(first_step)
        def _():
            for copy in late_ret:
                copy.wait()

        _tile_loop(n // wide, phase_a2)

    lower = (lax.broadcasted_iota(jnp.int32, (ROW_TILE, ROW_TILE), 0)
             >= lax.broadcasted_iota(jnp.int32, (ROW_TILE, ROW_TILE), 1))
    blocks = [slice(r * ROW_TILE, (r + 1) * ROW_TILE) for r in range(nt)]

    def phase_c(q, hd):
        cols = slice(hd * DK, (hd + 1) * DK)
        dec = lambda row: decay[hd][row]
        kv_f = [_dot_tn(q.k_f[hd, rows, :], q.v[rows, cols]) if (r < nt - 1 or not latent) else 0.0
                for r, rows in enumerate(blocks)]
        kv_b = [_dot_tn(q.k_b[hd, rows, :], q.v[rows, cols]) if (r > 0 or not latent) else 0.0
                for r, rows in enumerate(blocks)]
        if latent:
            run_f = q.s0[0, hd] * dec(DEC_G_F)
            run_b = q.s0[1, hd] * dec(DEC_GN_B)
        else:
            run_f = run_b = None
        before = []
        for r in range(nt):
            before.append(run_f)
            run_f = kv_f[r] if run_f is None else run_f + kv_f[r]
        after = [None] * nt
        for r in reversed(range(nt)):
            after[r] = run_b
            run_b = kv_b[r] if run_b is None else run_b + kv_b[r]

        for r, rows in enumerate(blocks):
            qf = q.q_f[hd, rows, :]
            qb = q.q_b[hd, rows, :]
            s = jnp.where(lower, _dot_nt(qf, q.k_f[hd, rows, :]), _dot_nt(qb, q.k_b[hd, rows, :]))
            o = _dot(s, q.v[rows, cols])
            if before[r] is not None:
                o = o + _dot(qf, before[r])
            if after[r] is not None:
                o = o + _dot(qb, after[r])
            o = o * lax.rsqrt(jnp.mean(o * o, axis=-1, keepdims=True) + EPS)
            q.mix[rows, LRU_W + cols.start:LRU_W + cols.stop] = o * q.sg[rows, cols]
        if not latent:
            q.str[0, hd] = run_f * dec(DEC_GN1_F)
            q.str[1, hd] = run_b

    for hd in range(RET_H):
        for q in seqs:
            phase_c(q, hd)

    gain1 = mod(2) * n2_ref[...]
    gain2 = n3_ref[...] * (1.0 + mod(4))
    sh2 = mod(3)

    def phase_d(r, carry):
        rows = pl.ds(pl.multiple_of(r * wide, wide), wide)
        mix_all = _dot(stacked([q.mix[rows, :] for q in seqs]), wout_ref[...])
        h2s = []
        for s, q in enumerate(seqs):
            x1 = q.x[rows, :] + _rms(mix_all[s * wide:(s + 1) * wide], gain1)
            q.x1[rows, :] = x1
            h2s.append(_rms(x1, gain2) + sh2)
            q.h2[rows, :] = h2s[-1].astype(_BF)
        logits_t = _dot(stacked(h2s), rw_ref[...]).T
        for s, q in enumerate(seqs):
            q.lt[:, rows] = logits_t[0:N_EXP, s * wide:(s + 1) * wide]
        return carry

    if latent:
        @pl.when(first_step)
        def _():
            late_out.wait()

    _tile_loop(n // wide, phase_d, unroll=2)


def _mixer_call(x, mod, norms, w_in, lru, wg, w_out, rw, latent, extra=(), layer=0):
    b, n, _ = x.shape
    g = 1 if latent else CONTEXT_GROUP
    const2 = lambda i: (0, 0)
    whole = pl.BlockSpec(memory_space=pl.ANY)
    in_specs = [
        pl.BlockSpec((g, n, D), lambda i: (i, 0, 0)),
        pl.BlockSpec((N_MOD, SUBLANES, D), lambda i: (0, 0, 0)),
        pl.BlockSpec((1, D), const2), pl.BlockSpec((1, D), const2), pl.BlockSpec((1, D), const2),
        pl.BlockSpec((D, D), lambda i: (0, 0)),
    ]
    if latent:
        in_specs += [whole]
    else:
        in_specs += [pl.BlockSpec((D, D), lambda i: (0, 1)),
                     pl.BlockSpec((D, D), lambda i: (0, 2))]
    in_specs += [
        pl.BlockSpec((4, LRU_W), const2), pl.BlockSpec((1, LRU_W), const2),
        pl.BlockSpec((2, LRU_W), const2), pl.BlockSpec((2, LRU_W), const2),
        pl.BlockSpec((2, LRU_W), const2),
        whole if latent else pl.BlockSpec((2, 2, LRU_W // 2, LRU_W), lambda i: (0, 0, 0, 0)),
        whole,
        pl.BlockSpec((D, LANES), const2),
    ]
    out_shape = [
        jax.ShapeDtypeStruct((b, n, D), _F32),
        jax.ShapeDtypeStruct((b, n, D), _BF),
        jax.ShapeDtypeStruct((b, N_EXP, n), _F32),
    ]
    out_specs = [
        pl.BlockSpec((g, n, D), lambda i: (i, 0, 0)),
        pl.BlockSpec((g, n, D), lambda i: (i, 0, 0)),
        pl.BlockSpec((g, N_EXP, n), lambda i: (i, 0, 0)),
    ]
    if latent:
        in_specs += [
            pl.BlockSpec((g, None, 2, LRU_W), lambda i: (i, layer, 0, 0)),
            pl.BlockSpec((g, None, 2, RET_H, DK, DK), lambda i: (i, layer, 0, 0, 0, 0)),
            whole, whole,
        ]
    else:
        out_shape += [
            jax.ShapeDtypeStruct((b, 1, 2, LRU_W), _F32),
            jax.ShapeDtypeStruct((b, 1, 2, RET_H, DK, DK), _F32),
        ]
        out_specs += [
            pl.BlockSpec((g, None, 2, LRU_W), lambda i: (i, 0, 0, 0)),
            pl.BlockSpec((g, None, 2, RET_H, DK, DK), lambda i: (i, 0, 0, 0, 0, 0)),
        ]
    f32s = lambda shape: pltpu.VMEM((g,) + shape, _F32)
    slabs = (LRU_W // LANES, n, LANES)
    scratch = [
        f32s((n + 2 * SUBLANES, LRU_W)),
        f32s((n, LRU_W)),
        f32s(slabs), f32s(slabs),
        f32s(slabs), f32s(slabs),
    ]
    if latent:
        scratch += [pltpu.VMEM((D, 2 * D), _F32), pltpu.VMEM(wg.shape, _F32),
                    pltpu.VMEM((n, DK), _F32), pltpu.VMEM((n, DK), _F32)]
        w_in_args = (w_in, w_in)
    else:
        scratch += [f32s((n, RET_W))] + [f32s(slabs) for _ in range(4)] + [f32s((n, RET_W))]
        w_in_args = (w_in, w_in, w_in)
    scratch += [pltpu.VMEM((D, D), _F32), pltpu.SemaphoreType.DMA((5 if latent else 1,))]
    return pl.pallas_call(
        functools.partial(_mixer_kernel, n=n, g=g, latent=latent),
        out_shape=out_shape,
        grid=(b // g,),
        in_specs=in_specs,
        out_specs=out_specs,
        scratch_shapes=scratch,
        compiler_params=pltpu.CompilerParams(
            dimension_semantics=("arbitrary",), vmem_limit_bytes=VMEM_LIMIT),
        name="mixer_latent" if latent else "mixer_context",
    )(x, mod, *norms, *w_in_args, *lru, wg, w_out, rw, *extra)


def _count(mask):
    return jnp.sum(jnp.where(mask, 1.0, 0.0), axis=-1, keepdims=True)


def _probs(l3):
    bsz, _, n = l3.shape
    m = jnp.max(l3, axis=1, keepdims=True)
    e = jnp.exp(l3 - m)
    return (e / jnp.sum(e, axis=1, keepdims=True)).reshape(bsz * N_EXP, n)


def _break_ties(parts):
    prep = []
    for bits, thr, cap in parts:
        rows, n = bits.shape
        eq = bits == thr
        need = float(cap) - _count(bits > thr)
        idx = lax.broadcasted_iota(jnp.int32, (rows, n), 1)
        prep.append((eq, need, idx, int(math.log2(n))))
    most = max(nbits for _, _, _, nbits in prep)

    def idx_body(i, last):
        out = []
        for (eq, need, idx, nbits), j in zip(prep, last):
            shift = nbits - 1 - i
            cand = j | jnp.where(shift >= 0, jnp.int32(1) << jnp.maximum(shift, 0), 0)
            out.append(jnp.where(_count(eq & (idx < cand)) < need, cand, j))
        return tuple(out)

    return lax.fori_loop(0, most, idx_body,
                         tuple(jnp.zeros((eq.shape[0], 1), jnp.int32) for eq, _, _, _ in prep))


def _slots(p, bits, thr, jlast):
    rows, n = p.shape
    idx = lax.broadcasted_iota(jnp.int32, (rows, n), 1)
    sel = (bits > thr) | ((bits == thr) & (idx <= jlast))
    before = (lax.broadcasted_iota(jnp.int32, (n, n), 0)
              < lax.broadcasted_iota(jnp.int32, (n, n), 1))
    pos = _dot(jnp.where(sel, 1.0, 0.0).astype(_BF), jnp.where(before, 1.0, 0.0).astype(_BF))
    return jnp.where(sel, pos, -1.0), jnp.where(sel, p, 0.0)


def _route_kernel(lp_ref, ls_ref, pp_ref, gp_ref, ps_ref, gs_ref, *, cap_p, cap_s):
    groups = ((_probs(lp_ref[...]), float(cap_p)), (_probs(ls_ref[...]), float(cap_s)))
    bits = [pltpu.bitcast(p, jnp.int32) for p, _ in groups]

    def settle(b, t, capf, hi, lo):
        with_hi = t | hi
        both = with_hi | lo
        with_lo = t | lo
        ok = lambda cand: _count(b >= cand) >= capf
        return jnp.where(ok(with_hi), jnp.where(ok(both), both, with_hi),
                         jnp.where(ok(with_lo), with_lo, t))

    def val_body(i, thr):
        hi = jnp.int32(1) << (29 - 2 * i)
        lo = jnp.int32(1) << (28 - 2 * i)
        return tuple(settle(b, t, capf, hi, lo) for b, t, (_, capf) in zip(bits, thr, groups))

    thr = lax.fori_loop(0, 15, val_body,
                        tuple(jnp.zeros((b.shape[0], 1), jnp.int32) for b in bits))
    jlast = _break_ties([(bits[0], thr[0], cap_p), (bits[1], thr[1], cap_s)])
    pos, gate = _slots(groups[0][0], bits[0], thr[0], jlast[0])
    pp_ref[...] = pos.reshape(pp_ref.shape)
    gp_ref[...] = gate.reshape(gp_ref.shape)
    pos, gate = _slots(groups[1][0], bits[1], thr[1], jlast[1])
    ps_ref[...] = pos.reshape(ps_ref.shape)
    gs_ref[...] = gate.reshape(gs_ref.shape)


def _route_call(lt_p, lt_s, cap_p, cap_s):
    shapes = [
        jax.ShapeDtypeStruct(lt_p.shape, _F32), jax.ShapeDtypeStruct(lt_p.shape, _F32),
        jax.ShapeDtypeStruct(lt_s.shape, _F32), jax.ShapeDtypeStruct(lt_s.shape, _F32),
    ]
    return pl.pallas_call(
        functools.partial(_route_kernel, cap_p=cap_p, cap_s=cap_s),
        out_shape=shapes,
        compiler_params=pltpu.CompilerParams(vmem_limit_bytes=VMEM_LIMIT),
        name="route_select",
    )(lt_p, lt_s)


def _dispatch_kernel(pos_ref, gate_ref, h_ref, xs_ref, gs_ref, *, n, cap, g):
    slot = lax.broadcasted_iota(jnp.int32, (cap, n), 0).astype(_F32)
    for j in range(g):
        slots = slice(j * cap, (j + 1) * cap)
        parts = []
        for e in range(N_EXP):
            hit = pos_ref[j, e:e + 1, :] == slot
            parts.append(jnp.where(hit, 1.0, 0.0).astype(_BF))
            gs_ref[e, slots, :] = jnp.sum(jnp.where(hit, gate_ref[j, e:e + 1, :], 0.0),
                                          axis=-1, keepdims=True)
        onehot = jnp.concatenate(parts, axis=0)
        xs = _dot(onehot, h_ref[j]).astype(_BF)
        for e in range(N_EXP):
            xs_ref[e, slots, :] = xs[e * cap:(e + 1) * cap, :]


def _dispatch_call(pos, gate, h2, cap):
    b, n, _ = h2.shape
    g = max(1, SMALL_STEP_TOKENS // n)
    return pl.pallas_call(
        functools.partial(_dispatch_kernel, n=n, cap=cap, g=g),
        out_shape=[
            jax.ShapeDtypeStruct((N_EXP, b * cap, D), _BF),
            jax.ShapeDtypeStruct((N_EXP, b * cap, 1), _F32),
        ],
        grid=(b // g,),
        in_specs=[
            pl.BlockSpec((g, N_EXP, n), lambda i: (i, 0, 0)),
            pl.BlockSpec((g, N_EXP, n), lambda i: (i, 0, 0)),
            pl.BlockSpec((g, n, D), lambda i: (i, 0, 0)),
        ],
        out_specs=[
            pl.BlockSpec((N_EXP, g * cap, D), lambda i: (0, i, 0)),
            pl.BlockSpec((N_EXP, g * cap, 1), lambda i: (0, i, 0)),
        ],
        compiler_params=pltpu.CompilerParams(
            dimension_semantics=("arbitrary",), vmem_limit_bytes=VMEM_LIMIT),
        name="dispatch",
    )(pos, gate, h2)


def _expert_kernel(xp_ref, xs_ref, gp_ref, gs_ref, wg_ref, wu_ref, wd_ref, y_ref, xcat, acc,
                   *, sp, nf, tf):
    f = pl.program_id(1)
    xcat[0:sp, :] = xp_ref[...]
    xcat[sp:, :] = xs_ref[...]
    x = xcat[...]
    total = jnp.where(f == 0, 0.0, acc[...])
    for c in range(tf // FF_CHUNK):
        cs = slice(c * FF_CHUNK, (c + 1) * FF_CHUNK)
        hg = _dot(x, wg_ref[:, cs].astype(_BF))
        hu = _dot(x, wu_ref[:, cs].astype(_BF))
        hid = (_silu(hg) * hu).astype(_BF)
        total = total + _dot(hid, wd_ref[cs, :].astype(_BF))
    acc[...] = total
    y_ref[0:sp, :] = (total[0:sp, :] * gp_ref[...]).astype(_BF)
    y_ref[sp:, :] = (total[sp:, :] * gs_ref[...]).astype(_BF)


def _expert_call(xs_p, xs_s, g_p, g_s, w_gate, w_up, w_down):
    tf = 1024
    sp = xs_p.shape[1]
    ss = xs_s.shape[1]
    nf = FF // tf
    return pl.pallas_call(
        functools.partial(_expert_kernel, sp=sp, nf=nf, tf=tf),
        out_shape=jax.ShapeDtypeStruct((N_EXP, sp + ss, D), _BF),
        grid=(N_EXP, nf),
        in_specs=[
            pl.BlockSpec((None, sp, D), lambda e, f: (e, 0, 0)),
            pl.BlockSpec((None, ss, D), lambda e, f: (e, 0, 0)),
            pl.BlockSpec((None, sp, 1), lambda e, f: (e, 0, 0)),
            pl.BlockSpec((None, ss, 1), lambda e, f: (e, 0, 0)),
            pl.BlockSpec((None, D, tf), lambda e, f: (e, 0, f)),
            pl.BlockSpec((None, D, tf), lambda e, f: (e, 0, f)),
            pl.BlockSpec((None, tf, D), lambda e, f: (e, f, 0)),
        ],
        out_specs=pl.BlockSpec((None, sp + ss, D), lambda e, f: (e, 0, 0)),
        scratch_shapes=[pltpu.VMEM((sp + ss, D), _BF), pltpu.VMEM((sp + ss, D), _F32)],
        compiler_params=pltpu.CompilerParams(
            dimension_semantics=("arbitrary", "arbitrary"), vmem_limit_bytes=VMEM_LIMIT),
        name="expert_ffn",
    )(xs_p, xs_s, g_p, g_s, w_gate, w_up, w_down)


def _combine_kernel(pos_ref, ye_ref, x1_hbm, mod_ref, n4_ref, y_ref, x1_buf, sem, *, n, cap, g, latent):
    step = pl.program_id(0)
    fetch = pltpu.make_async_copy(x1_hbm.at[pl.ds(step * g, g)], x1_buf, sem.at[0])
    fetch.start()

    width = N_EXP * cap
    lane = lax.broadcasted_iota(jnp.int32, (N_EXP, width), 1)
    expand = jnp.where(lane // cap == lax.broadcasted_iota(jnp.int32, (N_EXP, width), 0),
                       1.0, 0.0).astype(_BF)
    slot = (lax.broadcasted_iota(jnp.int32, (n, width), 1) % cap).astype(_F32)
    mod_row = (step + 1) if latent else 0
    gain = mod_ref[N_MOD - 1, pl.ds(mod_row, 1), :] * n4_ref[...]

    normed = []
    for j in range(g):
        ye = ye_ref[:, j * cap:(j + 1) * cap, :].reshape(width, D)
        pos_e = _dot_tn(pos_ref[j].astype(_BF), expand)
        onehot = jnp.where(pos_e == slot, 1.0, 0.0).astype(_BF)
        normed.append(_rms(_dot(onehot, ye), gain))
    fetch.wait()
    for j in range(g):
        y_ref[j] = x1_buf[j] + normed[j]


def _combine_call(pos, ye, x1, mod, norm_post, cap, slot_off, latent):
    b, n, _ = x1.shape
    g = max(1, SMALL_STEP_TOKENS // n)
    assert g == 1 or not latent
    blk_off = slot_off // (g * cap)
    return pl.pallas_call(
        functools.partial(_combine_kernel, n=n, cap=cap, g=g, latent=latent),
        out_shape=jax.ShapeDtypeStruct((b, n, D), _F32),
        grid=(b // g,),
        in_specs=[
            pl.BlockSpec((g, N_EXP, n), lambda i: (i, 0, 0)),
            pl.BlockSpec((N_EXP, g * cap, D), lambda i: (0, i + blk_off, 0)),
            pl.BlockSpec(memory_space=pl.ANY),
            pl.BlockSpec((N_MOD, SUBLANES, D), lambda i: (0, 0, 0)),
            pl.BlockSpec((1, D), lambda i: (0, 0)),
        ],
        out_specs=pl.BlockSpec((g, n, D), lambda i: (i, 0, 0)),
        scratch_shapes=[pltpu.VMEM((g, n, D), _F32), pltpu.SemaphoreType.DMA((1,))],
        compiler_params=pltpu.CompilerParams(
            dimension_semantics=("arbitrary",), vmem_limit_bytes=VMEM_LIMIT),
        name="combine_latent" if latent else "combine_context",
    )(pos, ye, x1, mod, norm_post)


def _block_diag_gates(wa, wi):
    per_half = LRU_HEADS // 2
    side = per_half * LRU_HD
    on_diag = (np.arange(side)[:, None] // LRU_HD) == (np.arange(side)[None, :] // LRU_HD)

    def bd(w):
        rows = w.reshape(2, 2, side, LRU_HD)
        return jnp.where(on_diag, jnp.tile(rows, (1, 1, 1, per_half)), 0.0)

    return 0.5 * jnp.concatenate([bd(wa), bd(wi)], axis=-1)


def _rope_tables(n):
    rows = n // GRID_W
    row = np.repeat(np.arange(rows, dtype=np.float32), GRID_W)
    col = np.tile(np.arange(GRID_W, dtype=np.float32), rows)
    nf = DK // 4
    freqs = np.float32(ROPE_BASE) ** (-np.arange(nf, dtype=np.float32) / np.float32(nf))
    ang = np.concatenate([row[:, None] * freqs, col[:, None] * freqs], axis=-1).astype(np.float32)
    cos = np.cos(ang)
    sin = np.sin(ang)
    return (jnp.asarray(np.concatenate([cos, cos], axis=-1), _F32),
            jnp.asarray(np.concatenate([-sin, sin], axis=-1), _F32))


def _decay_consts(n):
    heads = np.arange(RET_H, dtype=np.float32)
    f32 = np.float32
    lgf = np.log1p(-np.exp2(-(f32(RET_DECAY_OFFSET_FWD) + heads))).astype(f32)
    lgb = np.log1p(-np.exp2(-(f32(RET_DECAY_OFFSET_BWD) + heads))).astype(f32)
    tab = np.stack([lgf, lgb, np.exp(lgf), np.exp(f32(n) * lgb), np.exp(f32(n - 1) * lgf)], axis=1)
    return [[float(v) for v in row] for row in tab.astype(f32)]


def kernel(x_prompt, x_sample, c, state_lru, state_ret, c_ctx, ada_w, ada_b, norm_mix_pre, norm_mix_post, norm_ffn_pre, norm_ffn_post, w_in, conv_w, conv_b, lru_wa, lru_ba, lru_wi, lru_bi, lru_lambda, w_out, router_w, exp_w_gate, exp_w_up, exp_w_down):
    bp, n_p, _ = x_prompt.shape
    bs, n_s, _ = x_sample.shape
    cap_p = 2 * n_p // N_EXP
    cap_s = 2 * n_s // N_EXP
    l = 0

    mod = _ada_call(c_ctx[None, :], c, ada_w[l], ada_b[l][None, :])

    norms = (norm_mix_pre[l][None], norm_mix_post[l][None], norm_ffn_pre[l][None])
    lru = (conv_w[l], conv_b[l][None], lru_ba[l], lru_bi[l], lru_lambda[l])
    wg = _block_diag_gates(lru_wa[l], lru_wi[l])
    rw = jnp.pad(router_w[l], ((0, 0), (0, LANES - N_EXP)))
    cos2, sin2 = _rope_tables(n_s)

    x1_p, h2_p, lt_p, st_lru, st_ret = _mixer_call(
        x_prompt, mod, norms, w_in[l], lru, wg, w_out[l], rw, latent=False)
    x1_s, h2_s, lt_s = _mixer_call(
        x_sample, mod, norms, w_in[l], lru, wg, w_out[l], rw, latent=True,
        extra=(state_lru, state_ret, cos2, sin2), layer=l)

    pos_p, gate_p, pos_s, gate_s = _route_call(lt_p, lt_s, cap_p, cap_s)
    xs_p, gsl_p = _dispatch_call(pos_p, gate_p, h2_p, cap_p)
    xs_s, gsl_s = _dispatch_call(pos_s, gate_s, h2_s, cap_s)
    ye = _expert_call(xs_p, xs_s, gsl_p, gsl_s, exp_w_gate[l], exp_w_up[l], exp_w_down[l])

    norm_post = norm_ffn_post[l][None]
    y_p = _combine_call(pos_p, ye, x1_p, mod, norm_post, cap_p, 0, latent=False)
    y_s = _combine_call(pos_s, ye, x1_s, mod, norm_post, cap_s, bp * cap_p, latent=True)
    return (y_p, y_s, st_lru, st_ret)
```

```python
import functools
import math
import types

import jax
import jax.numpy as jnp
import numpy as np
from jax import lax
from jax.experimental import pallas as pl
from jax.experimental.pallas import tpu as pltpu

D = 1024
LRU_W = 512
LRU_HEADS = 8
LRU_HD = 64
LRU_C = 8.0
RET_W = 512
RET_H = 4
DK = 128
N_EXP = 16
FF = 2048
N_MOD = 6
EPS = 1e-6
GRID_W = 64
ROPE_BASE = 10000.0
RET_DECAY_OFFSET_FWD = 5.0
RET_DECAY_OFFSET_BWD = 5.5

ROW_TILE = 256
WIDE_TILE = 512
FF_CHUNK = 512
COMBINE_TILE = 1024
CONTEXT_GROUP = 2
SMALL_STEP_TOKENS = 1024
SUBLANES = 8
LANES = 128
VMEM_LIMIT = 60 * 1024 * 1024

DEC_LOG_F, DEC_LOG_B, DEC_G_F, DEC_GN_B, DEC_GN1_F = 0, 1, 2, 3, 4

_BF = jnp.bfloat16
_F32 = jnp.float32


def _sigmoid(x):
    return 0.5 * jnp.tanh(0.5 * x) + 0.5


def _silu(x):
    return x * _sigmoid(x)


def _gelu_tanh(x):
    c = math.sqrt(2.0 / math.pi)
    return 0.5 * x * (1.0 + jnp.tanh(c * (x + 0.044715 * (x * x * x))))


def _rms(x, gain):
    return x * lax.rsqrt(jnp.mean(x * x, axis=-1, keepdims=True) + EPS) * gain


def _dot(a, b):
    return jnp.dot(a, b, preferred_element_type=_F32)


def _dot_nt(a, b):
    return lax.dot_general(a, b, (((1,), (1,)), ((), ())), preferred_element_type=_F32)


def _dot_tn(a, b):
    return lax.dot_general(a, b, (((0,), (0,)), ((), ())), preferred_element_type=_F32)


def _ada_kernel(cc_ref, c_ref, w_ref, b_ref, o_ref, s_ref):
    nb = c_ref.shape[0]
    s_ref[...] = jnp.zeros_like(s_ref)
    s_ref[0:1, :] = _silu(cc_ref[...])
    s_ref[1:1 + nb, :] = _silu(c_ref[...])
    o_ref[...] = _dot(s_ref[...], w_ref[...]) + b_ref[...]


def _ada_call(c_ctx, c, ada_w, ada_b):
    nb = c.shape[0]
    assert nb + 1 <= SUBLANES
    return pl.pallas_call(
        _ada_kernel,
        out_shape=jax.ShapeDtypeStruct((N_MOD, SUBLANES, D), _F32),
        grid=(N_MOD,),
        in_specs=[
            pl.BlockSpec((1, D), lambda j: (0, 0)),
            pl.BlockSpec((nb, D), lambda j: (0, 0)),
            pl.BlockSpec((D, D), lambda j: (0, j)),
            pl.BlockSpec((1, D), lambda j: (0, j)),
        ],
        out_specs=pl.BlockSpec((None, SUBLANES, D), lambda j: (j, 0, 0)),
        scratch_shapes=[pltpu.VMEM((SUBLANES, D), _F32)],
        compiler_params=pltpu.CompilerParams(
            dimension_semantics=("arbitrary",), vmem_limit_bytes=VMEM_LIMIT),
        name="ada_mod",
    )(c_ctx, c, ada_w, ada_b)


def _tile_loop(nt, body, unroll=1):
    if nt == 1:
        body(0, 0)
    else:
        lax.fori_loop(0, nt, body, 0, unroll=unroll)


def _mixer_kernel(*refs, n, g, latent):
    if latent:
        (x_ref, mod_ref, n1_ref, n2_ref, n3_ref, wl_ref, win_hbm,
         cw_ref, cb_ref, ba_ref, bi_ref, lam_ref, wg_hbm, wout_hbm, rw_ref,
         h0_ref, s0_ref, cos_hbm, sin_hbm,
         x1_ref, h2_ref, lt_ref,
         xlp_g, gy_g, af_g, ab_g, hf_g, hb_g, wret_v, wg_ref, cos_ref, sin_ref,
         wout_ref, late_sem) = refs
        sg_g, qf_g, qb_g, kf_g, kb_g, v_g = xlp_g, af_g, ab_g, hf_g, hb_g, gy_g
        wqk_ref, wvg_ref = wret_v.at[:, 0:D], wret_v.at[:, D:2 * D]
    else:
        (x_ref, mod_ref, n1_ref, n2_ref, n3_ref, wl_ref, wqk_ref, wvg_ref,
         cw_ref, cb_ref, ba_ref, bi_ref, lam_ref, wg_ref, wout_hbm, rw_ref,
         x1_ref, h2_ref, lt_ref, stl_ref, str_ref,
         xlp_g, gy_g, af_g, ab_g, hf_g, hb_g, sg_g, qf_g, qb_g, kf_g, kb_g, v_g,
         wout_ref, late_sem) = refs
    early_ret = not latent

    first_step = pl.program_id(0) == 0
    late_out = pltpu.make_async_copy(wout_hbm, wout_ref, late_sem.at[0])
    late_gate, late_ret = [], []
    if latent:
        late_gate = [pltpu.make_async_copy(wg_hbm, wg_ref, late_sem.at[1])]
        late_ret = [
            pltpu.make_async_copy(win_hbm.at[:, pl.ds(D, 2 * D)], wret_v, late_sem.at[2]),
            pltpu.make_async_copy(cos_hbm, cos_ref, late_sem.at[3]),
            pltpu.make_async_copy(sin_hbm, sin_ref, late_sem.at[4]),
        ]

    @pl.when(first_step)
    def _():
        for copy in late_gate + late_ret + [late_out]:
            copy.start()

    seqs = []
    for s in range(g):
        q = types.SimpleNamespace(
            x=x_ref.at[s], x1=x1_ref.at[s], h2=h2_ref.at[s], lt=lt_ref.at[s], mix=x1_ref.at[s],
            xlp=xlp_g.at[s], sg=sg_g.at[s], gy=gy_g.at[s], v=v_g.at[s],
            a_f=af_g.at[s], a_b=ab_g.at[s], h_f=hf_g.at[s], h_b=hb_g.at[s],
            q_f=qf_g.at[s], q_b=qb_g.at[s], k_f=kf_g.at[s], k_b=kb_g.at[s])
        if latent:
            q.h0, q.s0 = h0_ref.at[s], s0_ref.at[s]
        else:
            q.stl, q.str = stl_ref.at[s], str_ref.at[s]
        seqs.append(q)

    nt = n // ROW_TILE
    decay = _decay_consts(n)
    mod_row = (pl.program_id(0) + 1) if latent else 0
    mod = lambda k: mod_ref[k, pl.ds(mod_row, 1), :]
    shift = mod(0)
    scale = n1_ref[...] * (1.0 + mod(1))

    def normed(q, rows):
        return _rms(q.x[rows, :], scale) + shift

    wide = min(n, WIDE_TILE)
    a_tile = wide if latent else ROW_TILE

    def stacked(parts):
        return parts[0] if len(parts) == 1 else jnp.concatenate(parts, axis=0)

    def store_retention(q, r0, rows, pqk, pvg):
        q.sg[rows, :] = _silu(pvg[:, RET_W:])
        tpos = (r0 + lax.broadcasted_iota(jnp.int32, (pqk.shape[0], DK), 0)).astype(_F32)
        if latent:
            cos2 = cos_ref[rows, :]
            sin2 = sin_ref[rows, :]
        for hd in range(RET_H):
            cols = slice(hd * DK, (hd + 1) * DK)
            qh = pqk[:, cols] * (DK ** -0.5)
            kh = pqk[:, RET_W + hd * DK:RET_W + (hd + 1) * DK]
            if latent:
                qh = qh * cos2 + pltpu.roll(qh, DK // 2, axis=1) * sin2
                kh = kh * cos2 + pltpu.roll(kh, DK // 2, axis=1) * sin2
            lgf = decay[hd][DEC_LOG_F]
            lgb = decay[hd][DEC_LOG_B]
            q.q_f[hd, rows, :] = qh * jnp.exp(tpos * lgf)
            q.k_f[hd, rows, :] = kh * jnp.exp(tpos * (-lgf))
            q.q_b[hd, rows, :] = qh * jnp.exp(tpos * (-lgb))
            q.k_b[hd, rows, :] = kh * jnp.exp(tpos * lgb)
        q.v[rows, :] = pvg[:, 0:RET_W]

    def phase_a1(r, carry):
        r0 = pl.multiple_of(r * a_tile, a_tile)
        rows = pl.ds(r0, a_tile)
        h = stacked([normed(q, rows) for q in seqs])
        p_all = _dot(h, wl_ref[...])
        if early_ret:
            pqk_all = _dot(h, wqk_ref[...])
            pvg_all = _dot(h, wvg_ref[...])
        for s, q in enumerate(seqs):
            part = slice(s * a_tile, (s + 1) * a_tile)
            q.xlp[pl.ds(r0 + SUBLANES, a_tile), :] = p_all[part, 0:LRU_W]
            q.gy[rows, :] = _gelu_tanh(p_all[part, LRU_W:])
            if early_ret:
                store_retention(q, r0, rows, pqk_all[part], pvg_all[part])
        return carry

    for q in seqs:
        q.xlp[0:SUBLANES, :] = jnp.zeros((SUBLANES, LRU_W), _F32)
        q.xlp[n + SUBLANES:n + 2 * SUBLANES, :] = jnp.zeros((SUBLANES, LRU_W), _F32)

    half = LRU_W // 2

    def softplus_neg(lam):
        z = -lam
        return jnp.maximum(z, 0.0) + jnp.log1p(jnp.exp(-jnp.abs(z)))

    sp = (softplus_neg(lam_ref[0:1, :]), softplus_neg(lam_ref[1:2, :]))

    def phase_b(q):
        def body(r, carry):
            r0 = pl.multiple_of(r * a_tile, a_tile)
            rows = pl.ds(r0, a_tile)
            ext = q.xlp[pl.ds(r0, a_tile + 2 * SUBLANES), :]
            xc = cb_ref[...]
            for tap in range(4):
                back = (2 - tap) % (a_tile + 2 * SUBLANES)
                win = ext if back == 0 else pltpu.roll(ext, back, axis=0)
                xc = xc + win[SUBLANES:SUBLANES + a_tile, :] * cw_ref[tap:tap + 1, :]
            xh = 0.5 * xc
            for d, (a_ref, u_ref) in enumerate(((q.a_f, q.h_f), (q.a_b, q.h_b))):
                bah = 0.5 * ba_ref[d:d + 1, :]
                bih = 0.5 * bi_ref[d:d + 1, :]
                ch = (-0.5 * LRU_C) * sp[d]
                for hh in range(2):
                    cs = slice(hh * half, (hh + 1) * half)
                    pre = _dot(xc[:, cs], wg_ref[d, hh])
                    t_r = jnp.tanh(pre[:, 0:half] + bah[:, cs])
                    t_i = jnp.tanh(pre[:, half:] + bih[:, cs])
                    log_a = t_r * ch[:, cs] + ch[:, cs]
                    a = jnp.exp(log_a)
                    om = -jnp.tanh(log_a) * (a * a + 1.0)
                    root = jnp.where(om > 0.0, om * lax.rsqrt(om), 0.0)
                    u = root * (t_i * xh[:, cs] + xh[:, cs])
                    for j in range(half // LANES):
                        lanes = slice(j * LANES, (j + 1) * LANES)
                        a_ref[hh * (half // LANES) + j, rows, :] = a[:, lanes]
                        u_ref[hh * (half // LANES) + j, rows, :] = u[:, lanes]
            return carry
        return body

    _tile_loop(n // a_tile, phase_a1)
    if late_gate:
        @pl.when(first_step)
        def _():
            for copy in late_gate:
                copy.wait()

    for q in seqs:
        _tile_loop(n // a_tile, phase_b(q))

    row8 = lax.broadcasted_iota(jnp.int32, (SUBLANES, LANES), 0)
    block = SUBLANES * SUBLANES
    n_blocks = n // block
    n_slabs = LRU_W // LANES

    def across_groups(a, b, reverse):
        for s in (1, 2, 4):
            m = (row8 < SUBLANES - s) if reverse else (row8 >= s)
            shift = SUBLANES - s if reverse else s
            a_s = jnp.where(m, pltpu.roll(a, shift, axis=0), 1.0)
            b_s = jnp.where(m, pltpu.roll(b, shift, axis=0), 0.0)
            b = a * b_s + b
            a = a * a_s
        return a, b

    def scan_block(a_ref, h_ref, base, carry, reverse):
        rows = [pl.ds(base + k, SUBLANES, stride=SUBLANES) for k in range(SUBLANES)]
        order = list(reversed(range(SUBLANES))) if reverse else list(range(SUBLANES))
        prod, local = {}, {}
        prev = None
        for k in order:
            a, u = a_ref[rows[k], :], h_ref[rows[k], :]
            prod[k] = a if prev is None else a * prod[prev]
            local[k] = u if prev is None else a * local[prev] + u
            prev = k
        p_all, h_all = across_groups(prod[prev], local[prev], reverse)
        inner = (row8 < SUBLANES - 1) if reverse else (row8 >= 1)
        shift = SUBLANES - 1 if reverse else 1
        enter = (jnp.where(inner, pltpu.roll(p_all, shift, axis=0), 1.0) * carry
                 + jnp.where(inner, pltpu.roll(h_all, shift, axis=0), 0.0))
        for k in order:
            h_ref[rows[k], :] = prod[k] * enter + local[k]
        leave = p_all * carry + h_all
        return leave[0:1, :] if reverse else leave[SUBLANES - 1:SUBLANES, :]

    def initial(q, d, s):
        if latent:
            return q.h0[d:d + 1, s * LANES:(s + 1) * LANES]
        return jnp.zeros((1, LANES), _F32)

    chains = [(q, s) for q in seqs for s in range(n_slabs)]

    def scan_body(i, carry):
        fwd_base = pl.multiple_of(i * block, block)
        bwd_base = pl.multiple_of((n_blocks - 1 - i) * block, block)
        out = []
        for (q, s), (cf, cb) in zip(chains, carry):
            out.append((scan_block(q.a_f.at[s], q.h_f.at[s], fwd_base, cf, False),
                        scan_block(q.a_b.at[s], q.h_b.at[s], bwd_base, cb, True)))
        return tuple(out)

    ends = lax.fori_loop(0, n_blocks, scan_body,
                         tuple((initial(q, 0, s), initial(q, 1, s)) for q, s in chains), unroll=2)

    if not latent:
        @pl.when(first_step)
        def _():
            late_out.wait()

    def lru_out(q):
        def body(r, carry):
            rows = pl.ds(pl.multiple_of(r * ROW_TILE, ROW_TILE), ROW_TILE)
            for s in range(n_slabs):
                lanes = slice(s * LANES, (s + 1) * LANES)
                q.mix[rows, lanes] = (q.h_f[s, rows, :] + q.h_b[s, rows, :]) * q.gy[rows, lanes]
            return carry
        return body

    for (q, s), (last_f, first_b) in zip(chains, ends):
        if not latent:
            q.stl[0:1, s * LANES:(s + 1) * LANES] = last_f
            q.stl[1:2, s * LANES:(s + 1) * LANES] = first_b
    for q in seqs:
        _tile_loop(nt, lru_out(q))

    def phase_a2(r, carry):
        r0 = pl.multiple_of(r * wide, wide)
        rows = pl.ds(r0, wide)
        h = stacked([normed(q, rows) for q in seqs])
        pqk_all = _dot(h, wqk_ref[...])
        pvg_all = _dot(h, wvg_ref[...])
        for s, q in enumerate(seqs):
            part = slice(s * wide, (s + 1) * wide)
            store_retention(q, r0, rows, pqk_all[part], pvg_all[part])
        return carry

    if not early_ret:
        @pl.when(first_step)
        def _():
            for copy in late_ret:
                copy.wait()

        _tile_loop(n // wide, phase_a2)

    lower = (lax.broadcasted_iota(jnp.int32, (ROW_TILE, ROW_TILE), 0)
             >= lax.broadcasted_iota(jnp.int32, (ROW_TILE, ROW_TILE), 1))
    blocks = [slice(r * ROW_TILE, (r + 1) * ROW_TILE) for r in range(nt)]

    def phase_c(q, hd):
        cols = slice(hd * DK, (hd + 1) * DK)
        dec = lambda row: decay[hd][row]
        kv = [_dot_tn(jnp.concatenate([q.k_f[hd, rows, :], q.k_b[hd, rows, :]], axis=1), q.v[rows, cols])
              for rows in blocks]
        kv_f = [m[0:DK] for m in kv]
        kv_b = [m[DK:2 * DK] for m in kv]
        if latent:
            run_f = q.s0[0, hd] * dec(DEC_G_F)
            run_b = q.s0[1, hd] * dec(DEC_GN_B)
        else:
            run_f = run_b = None
        before = []
        for r in range(nt):
            before.append(run_f)
            run_f = kv_f[r] if run_f is None else run_f + kv_f[r]
        after = [None] * nt
        for r in reversed(range(nt)):
            after[r] = run_b
            run_b = kv_b[r] if run_b is None else run_b + kv_b[r]

        for r, rows in enumerate(blocks):
            qf = q.q_f[hd, rows, :]
            qb = q.q_b[hd, rows, :]
            s = jnp.where(lower, _dot_nt(qf, q.k_f[hd, rows, :]), _dot_nt(qb, q.k_b[hd, rows, :]))
            o = _dot(s, q.v[rows, cols])
            if before[r] is not None and after[r] is not None:
                o = o + _dot(jnp.concatenate([qf, qb], axis=1),
                             jnp.concatenate([before[r], after[r]], axis=0))
            elif before[r] is not None:
                o = o + _dot(qf, before[r])
            elif after[r] is not None:
                o = o + _dot(qb, after[r])
            o = o * lax.rsqrt(jnp.mean(o * o, axis=-1, keepdims=True) + EPS)
            q.mix[rows, LRU_W + cols.start:LRU_W + cols.stop] = o * q.sg[rows, cols]
        if not latent:
            q.str[0, hd] = run_f * dec(DEC_GN1_F)
            q.str[1, hd] = run_b

    for hd in range(RET_H):
        for q in seqs:
            phase_c(q, hd)

    gain1 = mod(2) * n2_ref[...]
    gain2 = n3_ref[...] * (1.0 + mod(4))
    sh2 = mod(3)

    def phase_d(r, carry):
        rows = pl.ds(pl.multiple_of(r * wide, wide), wide)
        mix_all = _dot(stacked([q.mix[rows, :] for q in seqs]), wout_ref[...])
        h2s = []
        for s, q in enumerate(seqs):
            x1 = q.x[rows, :] + _rms(mix_all[s * wide:(s + 1) * wide], gain1)
            q.x1[rows, :] = x1
            h2s.append(_rms(x1, gain2) + sh2)
            q.h2[rows, :] = h2s[-1].astype(_BF)
        logits_t = _dot(stacked(h2s), rw_ref[...]).T
        for s, q in enumerate(seqs):
            q.lt[:, rows] = logits_t[0:N_EXP, s * wide:(s + 1) * wide]
        return carry

    if latent:
        @pl.when(first_step)
        def _():
            late_out.wait()

    _tile_loop(n // wide, phase_d, unroll=2)


def _mixer_call(x, mod, norms, w_in, lru, wg, w_out, rw, latent, extra=(), layer=0):
    b, n, _ = x.shape
    g = 1 if latent else CONTEXT_GROUP
    const2 = lambda i: (0, 0)
    whole = pl.BlockSpec(memory_space=pl.ANY)
    in_specs = [
        pl.BlockSpec((g, n, D), lambda i: (i, 0, 0)),
        pl.BlockSpec((N_MOD, SUBLANES, D), lambda i: (0, 0, 0)),
        pl.BlockSpec((1, D), const2), pl.BlockSpec((1, D), const2), pl.BlockSpec((1, D), const2),
        pl.BlockSpec((D, D), lambda i: (0, 0)),
    ]
    if latent:
        in_specs += [whole]
    else:
        in_specs += [pl.BlockSpec((D, D), lambda i: (0, 1)),
                     pl.BlockSpec((D, D), lambda i: (0, 2))]
    in_specs += [
        pl.BlockSpec((4, LRU_W), const2), pl.BlockSpec((1, LRU_W), const2),
        pl.BlockSpec((2, LRU_W), const2), pl.BlockSpec((2, LRU_W), const2),
        pl.BlockSpec((2, LRU_W), const2),
        whole if latent else pl.BlockSpec((2, 2, LRU_W // 2, LRU_W), lambda i: (0, 0, 0, 0)),
        whole,
        pl.BlockSpec((D, LANES), const2),
    ]
    out_shape = [
        jax.ShapeDtypeStruct((b, n, D), _F32),
        jax.ShapeDtypeStruct((b, n, D), _BF),
        jax.ShapeDtypeStruct((b, N_EXP, n), _F32),
    ]
    out_specs = [
        pl.BlockSpec((g, n, D), lambda i: (i, 0, 0)),
        pl.BlockSpec((g, n, D), lambda i: (i, 0, 0)),
        pl.BlockSpec((g, N_EXP, n), lambda i: (i, 0, 0)),
    ]
    if latent:
        in_specs += [
            pl.BlockSpec((g, None, 2, LRU_W), lambda i: (i, layer, 0, 0)),
            pl.BlockSpec((g, None, 2, RET_H, DK, DK), lambda i: (i, layer, 0, 0, 0, 0)),
            whole, whole,
        ]
    else:
        out_shape += [
            jax.ShapeDtypeStruct((b, 1, 2, LRU_W), _F32),
            jax.ShapeDtypeStruct((b, 1, 2, RET_H, DK, DK), _F32),
        ]
        out_specs += [
            pl.BlockSpec((g, None, 2, LRU_W), lambda i: (i, 0, 0, 0)),
            pl.BlockSpec((g, None, 2, RET_H, DK, DK), lambda i: (i, 0, 0, 0, 0, 0)),
        ]
    f32s = lambda shape: pltpu.VMEM((g,) + shape, _F32)
    slabs = (LRU_W // LANES, n, LANES)
    scratch = [
        f32s((n + 2 * SUBLANES, LRU_W)),
        f32s((n, LRU_W)),
        f32s(slabs), f32s(slabs),
        f32s(slabs), f32s(slabs),
    ]
    if latent:
        scratch += [pltpu.VMEM((D, 2 * D), _F32), pltpu.VMEM(wg.shape, _F32),
                    pltpu.VMEM((n, DK), _F32), pltpu.VMEM((n, DK), _F32)]
        w_in_args = (w_in, w_in)
    else:
        scratch += [f32s((n, RET_W))] + [f32s(slabs) for _ in range(4)] + [f32s((n, RET_W))]
        w_in_args = (w_in, w_in, w_in)
    scratch += [pltpu.VMEM((D, D), _F32), pltpu.SemaphoreType.DMA((5 if latent else 1,))]
    return pl.pallas_call(
        functools.partial(_mixer_kernel, n=n, g=g, latent=latent),
        out_shape=out_shape,
        grid=(b // g,),
        in_specs=in_specs,
        out_specs=out_specs,
        scratch_shapes=scratch,
        compiler_params=pltpu.CompilerParams(
            dimension_semantics=("arbitrary",), vmem_limit_bytes=VMEM_LIMIT),
        name="mixer_latent" if latent else "mixer_context",
    )(x, mod, *norms, *w_in_args, *lru, wg, w_out, rw, *extra)


def _count(mask):
    return jnp.sum(jnp.where(mask, 1.0, 0.0), axis=-1, keepdims=True)


def _probs(l3):
    bsz, _, n = l3.shape
    m = jnp.max(l3, axis=1, keepdims=True)
    e = jnp.exp(l3 - m)
    return (e / jnp.sum(e, axis=1, keepdims=True)).reshape(bsz * N_EXP, n)


def _break_ties(parts):
    prep = []
    for bits, thr, cap in parts:
        rows, n = bits.shape
        eq = bits == thr
        need = float(cap) - _count(bits > thr)
        idx = lax.broadcasted_iota(jnp.int32, (rows, n), 1)
        prep.append((eq, need, idx, int(math.log2(n))))
    most = max(nbits for _, _, _, nbits in prep)

    def idx_body(i, last):
        out = []
        for (eq, need, idx, nbits), j in zip(prep, last):
            shift = nbits - 1 - i
            cand = j | jnp.where(shift >= 0, jnp.int32(1) << jnp.maximum(shift, 0), 0)
            out.append(jnp.where(_count(eq & (idx < cand)) < need, cand, j))
        return tuple(out)

    return lax.fori_loop(0, most, idx_body,
                         tuple(jnp.zeros((eq.shape[0], 1), jnp.int32) for eq, _, _, _ in prep))


def _slots(p, bits, thr, jlast):
    rows, n = p.shape
    idx = lax.broadcasted_iota(jnp.int32, (rows, n), 1)
    sel = (bits > thr) | ((bits == thr) & (idx <= jlast))
    before = (lax.broadcasted_iota(jnp.int32, (n, n), 0)
              < lax.broadcasted_iota(jnp.int32, (n, n), 1))
    pos = _dot(jnp.where(sel, 1.0, 0.0).astype(_BF), jnp.where(before, 1.0, 0.0).astype(_BF))
    return jnp.where(sel, pos, -1.0), jnp.where(sel, p, 0.0)


def _route_kernel(lp_ref, ls_ref, pp_ref, gp_ref, ps_ref, gs_ref, *, cap_p, cap_s):
    groups = ((_probs(lp_ref[...]), float(cap_p)), (_probs(ls_ref[...]), float(cap_s)))
    bits = [pltpu.bitcast(p, jnp.int32) for p, _ in groups]

    def settle(b, t, capf, hi, lo):
        with_hi = t | hi
        both = with_hi | lo
        with_lo = t | lo
        ok = lambda cand: _count(b >= cand) >= capf
        return jnp.where(ok(with_hi), jnp.where(ok(both), both, with_hi),
                         jnp.where(ok(with_lo), with_lo, t))

    def val_body(i, thr):
        hi = jnp.int32(1) << (29 - 2 * i)
        lo = jnp.int32(1) << (28 - 2 * i)
        return tuple(settle(b, t, capf, hi, lo) for b, t, (_, capf) in zip(bits, thr, groups))

    thr = lax.fori_loop(0, 15, val_body,
                        tuple(jnp.zeros((b.shape[0], 1), jnp.int32) for b in bits))
    jlast = _break_ties([(bits[0], thr[0], cap_p), (bits[1], thr[1], cap_s)])
    pos, gate = _slots(groups[0][0], bits[0], thr[0], jlast[0])
    pp_ref[...] = pos.reshape(pp_ref.shape)
    gp_ref[...] = gate.reshape(gp_ref.shape)
    pos, gate = _slots(groups[1][0], bits[1], thr[1], jlast[1])
    ps_ref[...] = pos.reshape(ps_ref.shape)
    gs_ref[...] = gate.reshape(gs_ref.shape)


def _route_call(lt_p, lt_s, cap_p, cap_s):
    shapes = [
        jax.ShapeDtypeStruct(lt_p.shape, _F32), jax.ShapeDtypeStruct(lt_p.shape, _F32),
        jax.ShapeDtypeStruct(lt_s.shape, _F32), jax.ShapeDtypeStruct(lt_s.shape, _F32),
    ]
    return pl.pallas_call(
        functools.partial(_route_kernel, cap_p=cap_p, cap_s=cap_s),
        out_shape=shapes,
        compiler_params=pltpu.CompilerParams(vmem_limit_bytes=VMEM_LIMIT),
        name="route_select",
    )(lt_p, lt_s)


def _dispatch_kernel(pos_ref, gate_ref, h_ref, xs_ref, gs_ref, *, n, cap, g):
    slot = lax.broadcasted_iota(jnp.int32, (cap, n), 0).astype(_F32)
    for j in range(g):
        slots = slice(j * cap, (j + 1) * cap)
        parts = []
        for e in range(N_EXP):
            hit = pos_ref[j, e:e + 1, :] == slot
            parts.append(jnp.where(hit, 1.0, 0.0).astype(_BF))
            gs_ref[e, slots, :] = jnp.sum(jnp.where(hit, gate_ref[j, e:e + 1, :], 0.0),
                                          axis=-1, keepdims=True)
        onehot = jnp.concatenate(parts, axis=0)
        xs = _dot(onehot, h_ref[j]).astype(_BF)
        for e in range(N_EXP):
            xs_ref[e, slots, :] = xs[e * cap:(e + 1) * cap, :]


def _dispatch_call(pos, gate, h2, cap):
    b, n, _ = h2.shape
    g = max(1, SMALL_STEP_TOKENS // n)
    return pl.pallas_call(
        functools.partial(_dispatch_kernel, n=n, cap=cap, g=g),
        out_shape=[
            jax.ShapeDtypeStruct((N_EXP, b * cap, D), _BF),
            jax.ShapeDtypeStruct((N_EXP, b * cap, 1), _F32),
        ],
        grid=(b // g,),
        in_specs=[
            pl.BlockSpec((g, N_EXP, n), lambda i: (i, 0, 0)),
            pl.BlockSpec((g, N_EXP, n), lambda i: (i, 0, 0)),
            pl.BlockSpec((g, n, D), lambda i: (i, 0, 0)),
        ],
        out_specs=[
            pl.BlockSpec((N_EXP, g * cap, D), lambda i: (0, i, 0)),
            pl.BlockSpec((N_EXP, g * cap, 1), lambda i: (0, i, 0)),
        ],
        compiler_params=pltpu.CompilerParams(
            dimension_semantics=("arbitrary",), vmem_limit_bytes=VMEM_LIMIT),
        name="dispatch",
    )(pos, gate, h2)


def _expert_kernel(xp_ref, xs_ref, gp_ref, gs_ref, wg_ref, wu_ref, wd_ref, y_ref, xcat, acc,
                   *, sp, nf, tf):
    f = pl.program_id(1)
    xcat[0:sp, :] = xp_ref[...]
    xcat[sp:, :] = xs_ref[...]
    x = xcat[...]
    total = jnp.where(f == 0, 0.0, acc[...])
    for c in range(tf // FF_CHUNK):
        cs = slice(c * FF_CHUNK, (c + 1) * FF_CHUNK)
        hg = _dot(x, wg_ref[:, cs].astype(_BF))
        hu = _dot(x, wu_ref[:, cs].astype(_BF))
        hid = (_silu(hg) * hu).astype(_BF)
        total = total + _dot(hid, wd_ref[cs, :].astype(_BF))
    acc[...] = total
    y_ref[0:sp, :] = (total[0:sp, :] * gp_ref[...]).astype(_BF)
    y_ref[sp:, :] = (total[sp:, :] * gs_ref[...]).astype(_BF)


def _expert_call(xs_p, xs_s, g_p, g_s, w_gate, w_up, w_down):
    tf = 1024
    sp = xs_p.shape[1]
    ss = xs_s.shape[1]
    nf = FF // tf
    return pl.pallas_call(
        functools.partial(_expert_kernel, sp=sp, nf=nf, tf=tf),
        out_shape=jax.ShapeDtypeStruct((N_EXP, sp + ss, D), _BF),
        grid=(N_EXP, nf),
        in_specs=[
            pl.BlockSpec((None, sp, D), lambda e, f: (e, 0, 0)),
            pl.BlockSpec((None, ss, D), lambda e, f: (e, 0, 0)),
            pl.BlockSpec((None, sp, 1), lambda e, f: (e, 0, 0)),
            pl.BlockSpec((None, ss, 1), lambda e, f: (e, 0, 0)),
            pl.BlockSpec((None, D, tf), lambda e, f: (e, 0, f)),
            pl.BlockSpec((None, D, tf), lambda e, f: (e, 0, f)),
            pl.BlockSpec((None, tf, D), lambda e, f: (e, f, 0)),
        ],
        out_specs=pl.BlockSpec((None, sp + ss, D), lambda e, f: (e, 0, 0)),
        scratch_shapes=[pltpu.VMEM((sp + ss, D), _BF), pltpu.VMEM((sp + ss, D), _F32)],
        compiler_params=pltpu.CompilerParams(
            dimension_semantics=("arbitrary", "arbitrary"), vmem_limit_bytes=VMEM_LIMIT),
        name="expert_ffn",
    )(xs_p, xs_s, g_p, g_s, w_gate, w_up, w_down)


def _combine_kernel(pos_ref, ye_ref, x1_ref, mod_ref, n4_ref, y_ref, *, n, cap, g, latent):
    width = N_EXP * cap
    lane = lax.broadcasted_iota(jnp.int32, (N_EXP, width), 1)
    expand = jnp.where(lane // cap == lax.broadcasted_iota(jnp.int32, (N_EXP, width), 0),
                       1.0, 0.0).astype(_BF)
    tile = min(n, COMBINE_TILE)
    slot = (lax.broadcasted_iota(jnp.int32, (tile, width), 1) % cap).astype(_F32)
    mod_row = (pl.program_id(0) + 1) if latent else 0
    gain = mod_ref[N_MOD - 1, pl.ds(mod_row, 1), :] * n4_ref[...]

    for j in range(g):
        ye = ye_ref[:, j * cap:(j + 1) * cap, :].reshape(width, D)

        def body(r, carry, j=j, ye=ye):
            rows = pl.ds(pl.multiple_of(r * tile, tile), tile)
            pos_e = _dot_tn(pos_ref[j, :, rows].astype(_BF), expand)
            onehot = jnp.where(pos_e == slot, 1.0, 0.0).astype(_BF)
            f = _dot(onehot, ye)
            y_ref[j, rows, :] = x1_ref[j, rows, :] + _rms(f, gain)
            return carry

        _tile_loop(n // tile, body)


def _combine_call(pos, ye, x1, mod, norm_post, cap, slot_off, latent):
    b, n, _ = x1.shape
    g = max(1, SMALL_STEP_TOKENS // n)
    assert g == 1 or not latent
    blk_off = slot_off // (g * cap)
    return pl.pallas_call(
        functools.partial(_combine_kernel, n=n, cap=cap, g=g, latent=latent),
        out_shape=jax.ShapeDtypeStruct((b, n, D), _F32),
        grid=(b // g,),
        in_specs=[
            pl.BlockSpec((g, N_EXP, n), lambda i: (i, 0, 0)),
            pl.BlockSpec((N_EXP, g * cap, D), lambda i: (0, i + blk_off, 0)),
            pl.BlockSpec((g, n, D), lambda i: (i, 0, 0)),
            pl.BlockSpec((N_MOD, SUBLANES, D), lambda i: (0, 0, 0)),
            pl.BlockSpec((1, D), lambda i: (0, 0)),
        ],
        out_specs=pl.BlockSpec((g, n, D), lambda i: (i, 0, 0)),
        compiler_params=pltpu.CompilerParams(
            dimension_semantics=("arbitrary",), vmem_limit_bytes=VMEM_LIMIT),
        name="combine_latent" if latent else "combine_context",
    )(pos, ye, x1, mod, norm_post)


def _block_diag_gates(wa, wi):
    per_half = LRU_HEADS // 2
    side = per_half * LRU_HD
    on_diag = (np.arange(side)[:, None] // LRU_HD) == (np.arange(side)[None, :] // LRU_HD)

    def bd(w):
        rows = w.reshape(2, 2, side, LRU_HD)
        return jnp.where(on_diag, jnp.tile(rows, (1, 1, 1, per_half)), 0.0)

    return 0.5 * jnp.concatenate([bd(wa), bd(wi)], axis=-1)


def _rope_tables(n):
    rows = n // GRID_W
    row = np.repeat(np.arange(rows, dtype=np.float32), GRID_W)
    col = np.tile(np.arange(GRID_W, dtype=np.float32), rows)
    nf = DK // 4
    freqs = np.float32(ROPE_BASE) ** (-np.arange(nf, dtype=np.float32) / np.float32(nf))
    ang = np.concatenate([row[:, None] * freqs, col[:, None] * freqs], axis=-1).astype(np.float32)
    cos = np.cos(ang)
    sin = np.sin(ang)
    return (jnp.asarray(np.concatenate([cos, cos], axis=-1), _F32),
            jnp.asarray(np.concatenate([-sin, sin], axis=-1), _F32))


def _decay_consts(n):
    heads = np.arange(RET_H, dtype=np.float32)
    f32 = np.float32
    lgf = np.log1p(-np.exp2(-(f32(RET_DECAY_OFFSET_FWD) + heads))).astype(f32)
    lgb = np.log1p(-np.exp2(-(f32(RET_DECAY_OFFSET_BWD) + heads))).astype(f32)
    tab = np.stack([lgf, lgb, np.exp(lgf), np.exp(f32(n) * lgb), np.exp(f32(n - 1) * lgf)], axis=1)
    return [[float(v) for v in row] for row in tab.astype(f32)]


def kernel(x_prompt, x_sample, c, state_lru, state_ret, c_ctx, ada_w, ada_b, norm_mix_pre, norm_mix_post, norm_ffn_pre, norm_ffn_post, w_in, conv_w, conv_b, lru_wa, lru_ba, lru_wi, lru_bi, lru_lambda, w_out, router_w, exp_w_gate, exp_w_up, exp_w_down):
    bp, n_p, _ = x_prompt.shape
    bs, n_s, _ = x_sample.shape
    cap_p = 2 * n_p // N_EXP
    cap_s = 2 * n_s // N_EXP
    l = 0

    mod = _ada_call(c_ctx[None, :], c, ada_w[l], ada_b[l][None, :])

    norms = (norm_mix_pre[l][None], norm_mix_post[l][None], norm_ffn_pre[l][None])
    lru = (conv_w[l], conv_b[l][None], lru_ba[l], lru_bi[l], lru_lambda[l])
    wg = _block_diag_gates(lru_wa[l], lru_wi[l])
    rw = jnp.pad(router_w[l], ((0, 0), (0, LANES - N_EXP)))
    cos2, sin2 = _rope_tables(n_s)

    x1_p, h2_p, lt_p, st_lru, st_ret = _mixer_call(
        x_prompt, mod, norms, w_in[l], lru, wg, w_out[l], rw, latent=False)
    x1_s, h2_s, lt_s = _mixer_call(
        x_sample, mod, norms, w_in[l], lru, wg, w_out[l], rw, latent=True,
        extra=(state_lru, state_ret, cos2, sin2), layer=l)

    pos_p, gate_p, pos_s, gate_s = _route_call(lt_p, lt_s, cap_p, cap_s)
    xs_p, gsl_p = _dispatch_call(pos_p, gate_p, h2_p, cap_p)
    xs_s, gsl_s = _dispatch_call(pos_s, gate_s, h2_s, cap_s)
    ye = _expert_call(xs_p, xs_s, gsl_p, gsl_s, exp_w_gate[l], exp_w_up[l], exp_w_down[l])

    norm_post = norm_ffn_post[l][None]
    y_p = _combine_call(pos_p, ye, x1_p, mod, norm_post, cap_p, 0, latent=False)
    y_s = _combine_call(pos_s, ye, x1_s, mod, norm_post, cap_s, bp * cap_p, latent=True)
    return (y_p, y_s, st_lru, st_ret)
```

```python
import functools
import math
import types

import jax
import jax.numpy as jnp
import numpy as np
from jax import lax
from jax.experimental import pallas as pl
from jax.experimental.pallas import tpu as pltpu

D = 1024
LRU_W = 512
LRU_HEADS = 8
LRU_HD = 64
LRU_C = 8.0
RET_W = 512
RET_H = 4
DK = 128
N_EXP = 16
FF = 2048
N_MOD = 6
EPS = 1e-6
GRID_W = 64
ROPE_BASE = 10000.0
RET_DECAY_OFFSET_FWD = 5.0
RET_DECAY_OFFSET_BWD = 5.5

ROW_TILE = 256
WIDE_TILE = 512
FF_CHUNK = 512
COMBINE_TILE = 1024
CONTEXT_GROUP = 2
SMALL_STEP_TOKENS = 1024
SUBLANES = 8
LANES = 128
VMEM_LIMIT = 60 * 1024 * 1024

DEC_LOG_F, DEC_LOG_B, DEC_G_F, DEC_GN_B, DEC_GN1_F = 0, 1, 2, 3, 4

_BF = jnp.bfloat16
_F32 = jnp.float32


def _sigmoid(x):
    return 0.5 * jnp.tanh(0.5 * x) + 0.5


def _silu(x):
    return x * _sigmoid(x)


def _gelu_tanh(x):
    c = math.sqrt(2.0 / math.pi)
    return 0.5 * x * (1.0 + jnp.tanh(c * (x + 0.044715 * (x * x * x))))


def _rms(x, gain):
    return x * lax.rsqrt(jnp.mean(x * x, axis=-1, keepdims=True) + EPS) * gain


def _dot(a, b):
    return jnp.dot(a, b, preferred_element_type=_F32)


def _dot_nt(a, b):
    return lax.dot_general(a, b, (((1,), (1,)), ((), ())), preferred_element_type=_F32)


def _dot_tn(a, b):
    return lax.dot_general(a, b, (((0,), (0,)), ((), ())), preferred_element_type=_F32)


def _ada_kernel(cc_ref, c_ref, w_ref, b_ref, o_ref, s_ref):
    nb = c_ref.shape[0]
    s_ref[...] = jnp.zeros_like(s_ref)
    s_ref[0:1, :] = _silu(cc_ref[...])
    s_ref[1:1 + nb, :] = _silu(c_ref[...])
    o_ref[...] = _dot(s_ref[...], w_ref[...]) + b_ref[...]


def _ada_call(c_ctx, c, ada_w, ada_b):
    nb = c.shape[0]
    assert nb + 1 <= SUBLANES
    return pl.pallas_call(
        _ada_kernel,
        out_shape=jax.ShapeDtypeStruct((N_MOD, SUBLANES, D), _F32),
        grid=(N_MOD,),
        in_specs=[
            pl.BlockSpec((1, D), lambda j: (0, 0)),
            pl.BlockSpec((nb, D), lambda j: (0, 0)),
            pl.BlockSpec((D, D), lambda j: (0, j)),
            pl.BlockSpec((1, D), lambda j: (0, j)),
        ],
        out_specs=pl.BlockSpec((None, SUBLANES, D), lambda j: (j, 0, 0)),
        scratch_shapes=[pltpu.VMEM((SUBLANES, D), _F32)],
        compiler_params=pltpu.CompilerParams(
            dimension_semantics=("arbitrary",), vmem_limit_bytes=VMEM_LIMIT),
        name="ada_mod",
    )(c_ctx, c, ada_w, ada_b)


def _tile_loop(nt, body, unroll=1):
    if nt == 1:
        body(0, 0)
    else:
        lax.fori_loop(0, nt, body, 0, unroll=unroll)


def _mixer_kernel(*refs, n, g, latent):
    if latent:
        (x_ref, mod_ref, n1_ref, n2_ref, n3_ref, wl_ref, win_hbm,
         cw_ref, cb_ref, ba_ref, bi_ref, lam_ref, wg_hbm, wout_hbm, rw_ref,
         h0_ref, s0_ref, cos_hbm, sin_hbm,
         x1_ref, h2_ref, lt_ref,
         xlp_g, gy_g, af_g, ab_g, hf_g, hb_g, wret_v, wg_ref, cos_ref, sin_ref,
         wout_ref, late_sem) = refs
        sg_g, qf_g, qb_g, kf_g, kb_g, v_g = xlp_g, af_g, ab_g, hf_g, hb_g, gy_g
        wqk_ref, wvg_ref = wret_v.at[:, 0:D], wret_v.at[:, D:2 * D]
    else:
        (x_ref, mod_ref, n1_ref, n2_ref, n3_ref, wl_ref, wqk_ref, wvg_ref,
         cw_ref, cb_ref, ba_ref, bi_ref, lam_ref, wg_ref, wout_hbm, rw_ref,
         x1_ref, h2_ref, lt_ref, stl_ref, str_ref,
         xlp_g, gy_g, af_g, ab_g, hf_g, hb_g, sg_g, qf_g, qb_g, kf_g, kb_g, v_g,
         wout_ref, late_sem) = refs
    early_ret = not latent

    first_step = pl.program_id(0) == 0
    late_out = pltpu.make_async_copy(wout_hbm, wout_ref, late_sem.at[0])
    late_gate, late_ret = [], []
    if latent:
        late_gate = [pltpu.make_async_copy(wg_hbm, wg_ref, late_sem.at[1])]
        late_ret = [
            pltpu.make_async_copy(win_hbm.at[:, pl.ds(D, 2 * D)], wret_v, late_sem.at[2]),
            pltpu.make_async_copy(cos_hbm, cos_ref, late_sem.at[3]),
            pltpu.make_async_copy(sin_hbm, sin_ref, late_sem.at[4]),
        ]

    @pl.when(first_step)
    def _():
        for copy in late_gate + late_ret + [late_out]:
            copy.start()

    seqs = []
    for s in range(g):
        q = types.SimpleNamespace(
            x=x_ref.at[s], x1=x1_ref.at[s], h2=h2_ref.at[s], lt=lt_ref.at[s], mix=x1_ref.at[s],
            xlp=xlp_g.at[s], sg=sg_g.at[s], gy=gy_g.at[s], v=v_g.at[s],
            a_f=af_g.at[s], a_b=ab_g.at[s], h_f=hf_g.at[s], h_b=hb_g.at[s],
            q_f=qf_g.at[s], q_b=qb_g.at[s], k_f=kf_g.at[s], k_b=kb_g.at[s])
        if latent:
            q.h0, q.s0 = h0_ref.at[s], s0_ref.at[s]
        else:
            q.stl, q.str = stl_ref.at[s], str_ref.at[s]
        seqs.append(q)

    nt = n // ROW_TILE
    decay = _decay_consts(n)
    mod_row = (pl.program_id(0) + 1) if latent else 0
    mod = lambda k: mod_ref[k, pl.ds(mod_row, 1), :]
    shift = mod(0)
    scale = n1_ref[...] * (1.0 + mod(1))

    def normed(q, rows):
        return _rms(q.x[rows, :], scale) + shift

    wide = min(n, WIDE_TILE)
    a_tile = wide if latent else ROW_TILE

    def stacked(parts):
        return parts[0] if len(parts) == 1 else jnp.concatenate(parts, axis=0)

    def store_retention(q, r0, rows, pqk, pvg):
        q.sg[rows, :] = _silu(pvg[:, RET_W:])
        tpos = (r0 + lax.broadcasted_iota(jnp.int32, (pqk.shape[0], DK), 0)).astype(_F32)
        if latent:
            cos2 = cos_ref[rows, :]
            sin2 = sin_ref[rows, :]
        for hd in range(RET_H):
            cols = slice(hd * DK, (hd + 1) * DK)
            qh = pqk[:, cols] * (DK ** -0.5)
            kh = pqk[:, RET_W + hd * DK:RET_W + (hd + 1) * DK]
            if latent:
                qh = qh * cos2 + pltpu.roll(qh, DK // 2, axis=1) * sin2
                kh = kh * cos2 + pltpu.roll(kh, DK // 2, axis=1) * sin2
            lgf = decay[hd][DEC_LOG_F]
            lgb = decay[hd][DEC_LOG_B]
            q.q_f[hd, rows, :] = qh * jnp.exp(tpos * lgf)
            q.k_f[hd, rows, :] = kh * jnp.exp(tpos * (-lgf))
            q.q_b[hd, rows, :] = qh * jnp.exp(tpos * (-lgb))
            q.k_b[hd, rows, :] = kh * jnp.exp(tpos * lgb)
        q.v[rows, :] = pvg[:, 0:RET_W]

    def phase_a1(r, carry):
        r0 = pl.multiple_of(r * a_tile, a_tile)
        rows = pl.ds(r0, a_tile)
        h = stacked([normed(q, rows) for q in seqs])
        p_all = _dot(h, wl_ref[...])
        if early_ret:
            pqk_all = _dot(h, wqk_ref[...])
            pvg_all = _dot(h, wvg_ref[...])
        for s, q in enumerate(seqs):
            part = slice(s * a_tile, (s + 1) * a_tile)
            q.xlp[pl.ds(r0 + SUBLANES, a_tile), :] = p_all[part, 0:LRU_W]
            q.gy[rows, :] = _gelu_tanh(p_all[part, LRU_W:])
            if early_ret:
                store_retention(q, r0, rows, pqk_all[part], pvg_all[part])
        return carry

    for q in seqs:
        q.xlp[0:SUBLANES, :] = jnp.zeros((SUBLANES, LRU_W), _F32)
        q.xlp[n + SUBLANES:n + 2 * SUBLANES, :] = jnp.zeros((SUBLANES, LRU_W), _F32)

    half = LRU_W // 2

    def softplus_neg(lam):
        z = -lam
        return jnp.maximum(z, 0.0) + jnp.log1p(jnp.exp(-jnp.abs(z)))

    sp = (softplus_neg(lam_ref[0:1, :]), softplus_neg(lam_ref[1:2, :]))

    def phase_b(q):
        def body(r, carry):
            r0 = pl.multiple_of(r * a_tile, a_tile)
            rows = pl.ds(r0, a_tile)
            ext = q.xlp[pl.ds(r0, a_tile + 2 * SUBLANES), :]
            xc = cb_ref[...]
            for tap in range(4):
                back = (2 - tap) % (a_tile + 2 * SUBLANES)
                win = ext if back == 0 else pltpu.roll(ext, back, axis=0)
                xc = xc + win[SUBLANES:SUBLANES + a_tile, :] * cw_ref[tap:tap + 1, :]
            xh = 0.5 * xc
            for d, (a_ref, u_ref) in enumerate(((q.a_f, q.h_f), (q.a_b, q.h_b))):
                bah = 0.5 * ba_ref[d:d + 1, :]
                bih = 0.5 * bi_ref[d:d + 1, :]
                ch = (-0.5 * LRU_C) * sp[d]
                for hh in range(2):
                    cs = slice(hh * half, (hh + 1) * half)
                    pre = _dot(xc[:, cs], wg_ref[d, hh])
                    t_r = jnp.tanh(pre[:, 0:half] + bah[:, cs])
                    t_i = jnp.tanh(pre[:, half:] + bih[:, cs])
                    log_a = t_r * ch[:, cs] + ch[:, cs]
                    a = jnp.exp(log_a)
                    om = -jnp.tanh(log_a) * (a * a + 1.0)
                    root = jnp.where(om > 0.0, om * lax.rsqrt(om), 0.0)
                    u = root * (t_i * xh[:, cs] + xh[:, cs])
                    for j in range(half // LANES):
                        lanes = slice(j * LANES, (j + 1) * LANES)
                        a_ref[hh * (half // LANES) + j, rows, :] = a[:, lanes]
                        u_ref[hh * (half // LANES) + j, rows, :] = u[:, lanes]
            return carry
        return body

    _tile_loop(n // a_tile, phase_a1)
    if late_gate:
        @pl.when(first_step)
        def _():
            for copy in late_gate:
                copy.wait()

    for q in seqs:
        _tile_loop(n // a_tile, phase_b(q))

    row8 = lax.broadcasted_iota(jnp.int32, (SUBLANES, LANES), 0)
    block = SUBLANES * SUBLANES
    n_blocks = n // block
    n_slabs = LRU_W // LANES

    def across_groups(a, b, reverse):
        for s in (1, 2, 4):
            m = (row8 < SUBLANES - s) if reverse else (row8 >= s)
            shift = SUBLANES - s if reverse else s
            a_s = jnp.where(m, pltpu.roll(a, shift, axis=0), 1.0)
            b_s = jnp.where(m, pltpu.roll(b, shift, axis=0), 0.0)
            b = a * b_s + b
            a = a * a_s
        return a, b

    def scan_block(a_ref, h_ref, base, carry, reverse):
        rows = [pl.ds(base + k, SUBLANES, stride=SUBLANES) for k in range(SUBLANES)]
        order = list(reversed(range(SUBLANES))) if reverse else list(range(SUBLANES))
        prod, local = {}, {}
        prev = None
        for k in order:
            a, u = a_ref[rows[k], :], h_ref[rows[k], :]
            prod[k] = a if prev is None else a * prod[prev]
            local[k] = u if prev is None else a * local[prev] + u
            prev = k
        p_all, h_all = across_groups(prod[prev], local[prev], reverse)
        inner = (row8 < SUBLANES - 1) if reverse else (row8 >= 1)
        shift = SUBLANES - 1 if reverse else 1
        enter = (jnp.where(inner, pltpu.roll(p_all, shift, axis=0), 1.0) * carry
                 + jnp.where(inner, pltpu.roll(h_all, shift, axis=0), 0.0))
        for k in order:
            h_ref[rows[k], :] = prod[k] * enter + local[k]
        leave = p_all * carry + h_all
        return leave[0:1, :] if reverse else leave[SUBLANES - 1:SUBLANES, :]

    def initial(q, d, s):
        if latent:
            return q.h0[d:d + 1, s * LANES:(s + 1) * LANES]
        return jnp.zeros((1, LANES), _F32)

    chains = [(q, s) for q in seqs for s in range(n_slabs)]

    def scan_body(i, carry):
        fwd_base = pl.multiple_of(i * block, block)
        bwd_base = pl.multiple_of((n_blocks - 1 - i) * block, block)
        out = []
        for (q, s), (cf, cb) in zip(chains, carry):
            out.append((scan_block(q.a_f.at[s], q.h_f.at[s], fwd_base, cf, False),
                        scan_block(q.a_b.at[s], q.h_b.at[s], bwd_base, cb, True)))
        return tuple(out)

    ends = lax.fori_loop(0, n_blocks, scan_body,
                         tuple((initial(q, 0, s), initial(q, 1, s)) for q, s in chains), unroll=2)

    if not latent:
        @pl.when(first_step)
        def _():
            late_out.wait()

    def lru_out(q):
        def body(r, carry):
            rows = pl.ds(pl.multiple_of(r * ROW_TILE, ROW_TILE), ROW_TILE)
            for s in range(n_slabs):
                lanes = slice(s * LANES, (s + 1) * LANES)
                q.mix[rows, lanes] = (q.h_f[s, rows, :] + q.h_b[s, rows, :]) * q.gy[rows, lanes]
            return carry
        return body

    for (q, s), (last_f, first_b) in zip(chains, ends):
        if not latent:
            q.stl[0:1, s * LANES:(s + 1) * LANES] = last_f
            q.stl[1:2, s * LANES:(s + 1) * LANES] = first_b
    for q in seqs:
        _tile_loop(nt, lru_out(q))

    def phase_a2(r, carry):
        r0 = pl.multiple_of(r * wide, wide)
        rows = pl.ds(r0, wide)
        h = stacked([normed(q, rows) for q in seqs])
        pqk_all = _dot(h, wqk_ref[...])
        pvg_all = _dot(h, wvg_ref[...])
        for s, q in enumerate(seqs):
            part = slice(s * wide, (s + 1) * wide)
            store_retention(q, r0, rows, pqk_all[part], pvg_all[part])
        return carry

    if not early_ret:
        @pl.when(first_step)
        def _():
            for copy in late_ret:
                copy.wait()

        _tile_loop(n // wide, phase_a2)

    lower = (lax.broadcasted_iota(jnp.int32, (ROW_TILE, ROW_TILE), 0)
             >= lax.broadcasted_iota(jnp.int32, (ROW_TILE, ROW_TILE), 1))
    blocks = [slice(r * ROW_TILE, (r + 1) * ROW_TILE) for r in range(nt)]

    def phase_c(q, hd):
        cols = slice(hd * DK, (hd + 1) * DK)
        dec = lambda row: decay[hd][row]
        if nt > 1:
            kv = [_dot_tn(jnp.concatenate([q.k_f[hd, rows, :], q.k_b[hd, rows, :]], axis=1),
                          q.v[rows, cols]) for rows in blocks]
            kv_f = [m[0:DK] for m in kv]
            kv_b = [m[DK:2 * DK] for m in kv]
        else:
            kv_f = [_dot_tn(q.k_f[hd, rows, :], q.v[rows, cols]) for rows in blocks]
            kv_b = [_dot_tn(q.k_b[hd, rows, :], q.v[rows, cols]) for rows in blocks]
        if latent:
            run_f = q.s0[0, hd] * dec(DEC_G_F)
            run_b = q.s0[1, hd] * dec(DEC_GN_B)
        else:
            run_f = run_b = None
        before = []
        for r in range(nt):
            before.append(run_f)
            run_f = kv_f[r] if run_f is None else run_f + kv_f[r]
        after = [None] * nt
        for r in reversed(range(nt)):
            after[r] = run_b
            run_b = kv_b[r] if run_b is None else run_b + kv_b[r]

        for r, rows in enumerate(blocks):
            qf = q.q_f[hd, rows, :]
            qb = q.q_b[hd, rows, :]
            s = jnp.where(lower, _dot_nt(qf, q.k_f[hd, rows, :]), _dot_nt(qb, q.k_b[hd, rows, :]))
            o = _dot(s, q.v[rows, cols])
            if before[r] is not None and after[r] is not None:
                o = o + _dot(jnp.concatenate([qf, qb], axis=1),
                             jnp.concatenate([before[r], after[r]], axis=0))
            elif before[r] is not None:
                o = o + _dot(qf, before[r])
            elif after[r] is not None:
                o = o + _dot(qb, after[r])
            o = o * lax.rsqrt(jnp.mean(o * o, axis=-1, keepdims=True) + EPS)
            q.mix[rows, LRU_W + cols.start:LRU_W + cols.stop] = o * q.sg[rows, cols]
        if not latent:
            q.str[0, hd] = run_f * dec(DEC_GN1_F)
            q.str[1, hd] = run_b

    for hd in range(RET_H):
        for q in seqs:
            phase_c(q, hd)

    gain1 = mod(2) * n2_ref[...]
    gain2 = n3_ref[...] * (1.0 + mod(4))
    sh2 = mod(3)

    def phase_d(r, carry):
        rows = pl.ds(pl.multiple_of(r * wide, wide), wide)
        mix_all = _dot(stacked([q.mix[rows, :] for q in seqs]), wout_ref[...])
        h2s = []
        for s, q in enumerate(seqs):
            x1 = q.x[rows, :] + _rms(mix_all[s * wide:(s + 1) * wide], gain1)
            q.x1[rows, :] = x1
            h2s.append(_rms(x1, gain2) + sh2)
            q.h2[rows, :] = h2s[-1].astype(_BF)
        logits_t = _dot(stacked(h2s), rw_ref[...]).T
        for s, q in enumerate(seqs):
            q.lt[:, rows] = logits_t[0:N_EXP, s * wide:(s + 1) * wide]
        return carry

    if latent:
        @pl.when(first_step)
        def _():
            late_out.wait()

    _tile_loop(n // wide, phase_d, unroll=2)


def _mixer_call(x, mod, norms, w_in, lru, wg, w_out, rw, latent, extra=(), layer=0):
    b, n, _ = x.shape
    g = 1 if latent else CONTEXT_GROUP
    const2 = lambda i: (0, 0)
    whole = pl.BlockSpec(memory_space=pl.ANY)
    in_specs = [
        pl.BlockSpec((g, n, D), lambda i: (i, 0, 0)),
        pl.BlockSpec((N_MOD, SUBLANES, D), lambda i: (0, 0, 0)),
        pl.BlockSpec((1, D), const2), pl.BlockSpec((1, D), const2), pl.BlockSpec((1, D), const2),
        pl.BlockSpec((D, D), lambda i: (0, 0)),
    ]
    if latent:
        in_specs += [whole]
    else:
        in_specs += [pl.BlockSpec((D, D), lambda i: (0, 1)),
                     pl.BlockSpec((D, D), lambda i: (0, 2))]
    in_specs += [
        pl.BlockSpec((4, LRU_W), const2), pl.BlockSpec((1, LRU_W), const2),
        pl.BlockSpec((2, LRU_W), const2), pl.BlockSpec((2, LRU_W), const2),
        pl.BlockSpec((2, LRU_W), const2),
        whole if latent else pl.BlockSpec((2, 2, LRU_W // 2, LRU_W), lambda i: (0, 0, 0, 0)),
        whole,
        pl.BlockSpec((D, LANES), const2),
    ]
    out_shape = [
        jax.ShapeDtypeStruct((b, n, D), _F32),
        jax.ShapeDtypeStruct((b, n, D), _BF),
        jax.ShapeDtypeStruct((b, N_EXP, n), _F32),
    ]
    out_specs = [
        pl.BlockSpec((g, n, D), lambda i: (i, 0, 0)),
        pl.BlockSpec((g, n, D), lambda i: (i, 0, 0)),
        pl.BlockSpec((g, N_EXP, n), lambda i: (i, 0, 0)),
    ]
    if latent:
        in_specs += [
            pl.BlockSpec((g, None, 2, LRU_W), lambda i: (i, layer, 0, 0)),
            pl.BlockSpec((g, None, 2, RET_H, DK, DK), lambda i: (i, layer, 0, 0, 0, 0)),
            whole, whole,
        ]
    else:
        out_shape += [
            jax.ShapeDtypeStruct((b, 1, 2, LRU_W), _F32),
            jax.ShapeDtypeStruct((b, 1, 2, RET_H, DK, DK), _F32),
        ]
        out_specs += [
            pl.BlockSpec((g, None, 2, LRU_W), lambda i: (i, 0, 0, 0)),
            pl.BlockSpec((g, None, 2, RET_H, DK, DK), lambda i: (i, 0, 0, 0, 0, 0)),
        ]
    f32s = lambda shape: pltpu.VMEM((g,) + shape, _F32)
    slabs = (LRU_W // LANES, n, LANES)
    scratch = [
        f32s((n + 2 * SUBLANES, LRU_W)),
        f32s((n, LRU_W)),
        f32s(slabs), f32s(slabs),
        f32s(slabs), f32s(slabs),
    ]
    if latent:
        scratch += [pltpu.VMEM((D, 2 * D), _F32), pltpu.VMEM(wg.shape, _F32),
                    pltpu.VMEM((n, DK), _F32), pltpu.VMEM((n, DK), _F32)]
        w_in_args = (w_in, w_in)
    else:
        scratch += [f32s((n, RET_W))] + [f32s(slabs) for _ in range(4)] + [f32s((n, RET_W))]
        w_in_args = (w_in, w_in, w_in)
    scratch += [pltpu.VMEM((D, D), _F32), pltpu.SemaphoreType.DMA((5 if latent else 1,))]
    return pl.pallas_call(
        functools.partial(_mixer_kernel, n=n, g=g, latent=latent),
        out_shape=out_shape,
        grid=(b // g,),
        in_specs=in_specs,
        out_specs=out_specs,
        scratch_shapes=scratch,
        compiler_params=pltpu.CompilerParams(
            dimension_semantics=("arbitrary",), vmem_limit_bytes=VMEM_LIMIT),
        name="mixer_latent" if latent else "mixer_context",
    )(x, mod, *norms, *w_in_args, *lru, wg, w_out, rw, *extra)


def _count(mask):
    return jnp.sum(jnp.where(mask, 1.0, 0.0), axis=-1, keepdims=True)


def _probs(l3):
    bsz, _, n = l3.shape
    m = jnp.max(l3, axis=1, keepdims=True)
    e = jnp.exp(l3 - m)
    return (e / jnp.sum(e, axis=1, keepdims=True)).reshape(bsz * N_EXP, n)


def _break_ties(parts):
    prep = []
    for bits, thr, cap in parts:
        rows, n = bits.shape
        eq = bits == thr
        need = float(cap) - _count(bits > thr)
        idx = lax.broadcasted_iota(jnp.int32, (rows, n), 1)
        prep.append((eq, need, idx, int(math.log2(n))))
    most = max(nbits for _, _, _, nbits in prep)

    def idx_body(i, last):
        out = []
        for (eq, need, idx, nbits), j in zip(prep, last):
            shift = nbits - 1 - i
            cand = j | jnp.where(shift >= 0, jnp.int32(1) << jnp.maximum(shift, 0), 0)
            out.append(jnp.where(_count(eq & (idx < cand)) < need, cand, j))
        return tuple(out)

    return lax.fori_loop(0, most, idx_body,
                         tuple(jnp.zeros((eq.shape[0], 1), jnp.int32) for eq, _, _, _ in prep))


def _slots(p, bits, thr, jlast):
    rows, n = p.shape
    idx = lax.broadcasted_iota(jnp.int32, (rows, n), 1)
    sel = (bits > thr) | ((bits == thr) & (idx <= jlast))
    before = (lax.broadcasted_iota(jnp.int32, (n, n), 0)
              < lax.broadcasted_iota(jnp.int32, (n, n), 1))
    pos = _dot(jnp.where(sel, 1.0, 0.0).astype(_BF), jnp.where(before, 1.0, 0.0).astype(_BF))
    return jnp.where(sel, pos, -1.0), jnp.where(sel, p, 0.0)


def _route_kernel(lp_ref, ls_ref, pp_ref, gp_ref, ps_ref, gs_ref, *, cap_p, cap_s):
    groups = ((_probs(lp_ref[...]), float(cap_p)), (_probs(ls_ref[...]), float(cap_s)))
    bits = [pltpu.bitcast(p, jnp.int32) for p, _ in groups]

    def settle(b, t, capf, hi, lo):
        with_hi = t | hi
        both = with_hi | lo
        with_lo = t | lo
        ok = lambda cand: _count(b >= cand) >= capf
        return jnp.where(ok(with_hi), jnp.where(ok(both), both, with_hi),
                         jnp.where(ok(with_lo), with_lo, t))

    def val_body(i, thr):
        hi = jnp.int32(1) << (29 - 2 * i)
        lo = jnp.int32(1) << (28 - 2 * i)
        return tuple(settle(b, t, capf, hi, lo) for b, t, (_, capf) in zip(bits, thr, groups))

    thr = lax.fori_loop(0, 15, val_body,
                        tuple(jnp.zeros((b.shape[0], 1), jnp.int32) for b in bits))
    jlast = _break_ties([(bits[0], thr[0], cap_p), (bits[1], thr[1], cap_s)])
    pos, gate = _slots(groups[0][0], bits[0], thr[0], jlast[0])
    pp_ref[...] = pos.reshape(pp_ref.shape)
    gp_ref[...] = gate.reshape(gp_ref.shape)
    pos, gate = _slots(groups[1][0], bits[1], thr[1], jlast[1])
    ps_ref[...] = pos.reshape(ps_ref.shape)
    gs_ref[...] = gate.reshape(gs_ref.shape)


def _route_call(lt_p, lt_s, cap_p, cap_s):
    shapes = [
        jax.ShapeDtypeStruct(lt_p.shape, _F32), jax.ShapeDtypeStruct(lt_p.shape, _F32),
        jax.ShapeDtypeStruct(lt_s.shape, _F32), jax.ShapeDtypeStruct(lt_s.shape, _F32),
    ]
    return pl.pallas_call(
        functools.partial(_route_kernel, cap_p=cap_p, cap_s=cap_s),
        out_shape=shapes,
        compiler_params=pltpu.CompilerParams(vmem_limit_bytes=VMEM_LIMIT),
        name="route_select",
    )(lt_p, lt_s)


def _dispatch_kernel(pos_ref, gate_ref, h_ref, xs_ref, gs_ref, *, n, cap, g):
    slot = lax.broadcasted_iota(jnp.int32, (cap, n), 0).astype(_F32)
    for j in range(g):
        slots = slice(j * cap, (j + 1) * cap)
        parts = []
        for e in range(N_EXP):
            hit = pos_ref[j, e:e + 1, :] == slot
            parts.append(jnp.where(hit, 1.0, 0.0).astype(_BF))
            gs_ref[e, slots, :] = jnp.sum(jnp.where(hit, gate_ref[j, e:e + 1, :], 0.0),
                                          axis=-1, keepdims=True)
        onehot = jnp.concatenate(parts, axis=0)
        xs = _dot(onehot, h_ref[j]).astype(_BF)
        for e in range(N_EXP):
            xs_ref[e, slots, :] = xs[e * cap:(e + 1) * cap, :]


def _dispatch_call(pos, gate, h2, cap):
    b, n, _ = h2.shape
    g = max(1, SMALL_STEP_TOKENS // n)
    return pl.pallas_call(
        functools.partial(_dispatch_kernel, n=n, cap=cap, g=g),
        out_shape=[
            jax.ShapeDtypeStruct((N_EXP, b * cap, D), _BF),
            jax.ShapeDtypeStruct((N_EXP, b * cap, 1), _F32),
        ],
        grid=(b // g,),
        in_specs=[
            pl.BlockSpec((g, N_EXP, n), lambda i: (i, 0, 0)),
            pl.BlockSpec((g, N_EXP, n), lambda i: (i, 0, 0)),
            pl.BlockSpec((g, n, D), lambda i: (i, 0, 0)),
        ],
        out_specs=[
            pl.BlockSpec((N_EXP, g * cap, D), lambda i: (0, i, 0)),
            pl.BlockSpec((N_EXP, g * cap, 1), lambda i: (0, i, 0)),
        ],
        compiler_params=pltpu.CompilerParams(
            dimension_semantics=("arbitrary",), vmem_limit_bytes=VMEM_LIMIT),
        name="dispatch",
    )(pos, gate, h2)


def _expert_kernel(xp_ref, xs_ref, gp_ref, gs_ref, wg_ref, wu_ref, wd_ref, y_ref, xcat, acc,
                   *, sp, nf, tf):
    f = pl.program_id(1)
    xcat[0:sp, :] = xp_ref[...]
    xcat[sp:, :] = xs_ref[...]
    x = xcat[...]
    total = jnp.where(f == 0, 0.0, acc[...])
    for c in range(tf // FF_CHUNK):
        cs = slice(c * FF_CHUNK, (c + 1) * FF_CHUNK)
        hg = _dot(x, wg_ref[:, cs].astype(_BF))
        hu = _dot(x, wu_ref[:, cs].astype(_BF))
        hid = (_silu(hg) * hu).astype(_BF)
        total = total + _dot(hid, wd_ref[cs, :].astype(_BF))
    acc[...] = total
    y_ref[0:sp, :] = (total[0:sp, :] * gp_ref[...]).astype(_BF)
    y_ref[sp:, :] = (total[sp:, :] * gs_ref[...]).astype(_BF)


def _expert_call(xs_p, xs_s, g_p, g_s, w_gate, w_up, w_down):
    tf = 1024
    sp = xs_p.shape[1]
    ss = xs_s.shape[1]
    nf = FF // tf
    return pl.pallas_call(
        functools.partial(_expert_kernel, sp=sp, nf=nf, tf=tf),
        out_shape=jax.ShapeDtypeStruct((N_EXP, sp + ss, D), _BF),
        grid=(N_EXP, nf),
        in_specs=[
            pl.BlockSpec((None, sp, D), lambda e, f: (e, 0, 0)),
            pl.BlockSpec((None, ss, D), lambda e, f: (e, 0, 0)),
            pl.BlockSpec((None, sp, 1), lambda e, f: (e, 0, 0)),
            pl.BlockSpec((None, ss, 1), lambda e, f: (e, 0, 0)),
            pl.BlockSpec((None, D, tf), lambda e, f: (e, 0, f)),
            pl.BlockSpec((None, D, tf), lambda e, f: (e, 0, f)),
            pl.BlockSpec((None, tf, D), lambda e, f: (e, f, 0)),
        ],
        out_specs=pl.BlockSpec((None, sp + ss, D), lambda e, f: (e, 0, 0)),
        scratch_shapes=[pltpu.VMEM((sp + ss, D), _BF), pltpu.VMEM((sp + ss, D), _F32)],
        compiler_params=pltpu.CompilerParams(
            dimension_semantics=("arbitrary", "arbitrary"), vmem_limit_bytes=VMEM_LIMIT),
        name="expert_ffn",
    )(xs_p, xs_s, g_p, g_s, w_gate, w_up, w_down)


def _combine_kernel(pos_ref, ye_ref, x1_ref, mod_ref, n4_ref, y_ref, *, n, cap, g, latent):
    width = N_EXP * cap
    lane = lax.broadcasted_iota(jnp.int32, (N_EXP, width), 1)
    expand = jnp.where(lane // cap == lax.broadcasted_iota(jnp.int32, (N_EXP, width), 0),
                       1.0, 0.0).astype(_BF)
    tile = min(n, COMBINE_TILE)
    slot = (lax.broadcasted_iota(jnp.int32, (tile, width), 1) % cap).astype(_F32)
    mod_row = (pl.program_id(0) + 1) if latent else 0
    gain = mod_ref[N_MOD - 1, pl.ds(mod_row, 1), :] * n4_ref[...]

    for j in range(g):
        ye = ye_ref[:, j * cap:(j + 1) * cap, :].reshape(width, D)

        def body(r, carry, j=j, ye=ye):
            rows = pl.ds(pl.multiple_of(r * tile, tile), tile)
            pos_e = _dot_tn(pos_ref[j, :, rows].astype(_BF), expand)
            onehot = jnp.where(pos_e == slot, 1.0, 0.0).astype(_BF)
            f = _dot(onehot, ye)
            y_ref[j, rows, :] = x1_ref[j, rows, :] + _rms(f, gain)
            return carry

        _tile_loop(n // tile, body)


def _combine_call(pos, ye, x1, mod, norm_post, cap, slot_off, latent):
    b, n, _ = x1.shape
    g = max(1, SMALL_STEP_TOKENS // n)
    assert g == 1 or not latent
    blk_off = slot_off // (g * cap)
    return pl.pallas_call(
        functools.partial(_combine_kernel, n=n, cap=cap, g=g, latent=latent),
        out_shape=jax.ShapeDtypeStruct((b, n, D), _F32),
        grid=(b // g,),
        in_specs=[
            pl.BlockSpec((g, N_EXP, n), lambda i: (i, 0, 0)),
            pl.BlockSpec((N_EXP, g * cap, D), lambda i: (0, i + blk_off, 0)),
            pl.BlockSpec((g, n, D), lambda i: (i, 0, 0)),
            pl.BlockSpec((N_MOD, SUBLANES, D), lambda i: (0, 0, 0)),
            pl.BlockSpec((1, D), lambda i: (0, 0)),
        ],
        out_specs=pl.BlockSpec((g, n, D), lambda i: (i, 0, 0)),
        compiler_params=pltpu.CompilerParams(
            dimension_semantics=("arbitrary",), vmem_limit_bytes=VMEM_LIMIT),
        name="combine_latent" if latent else "combine_context",
    )(pos, ye, x1, mod, norm_post)


def _block_diag_gates(wa, wi):
    per_half = LRU_HEADS // 2
    side = per_half * LRU_HD
    on_diag = (np.arange(side)[:, None] // LRU_HD) == (np.arange(side)[None, :] // LRU_HD)

    def bd(w):
        rows = w.reshape(2, 2, side, LRU_HD)
        return jnp.where(on_diag, jnp.tile(rows, (1, 1, 1, per_half)), 0.0)

    return 0.5 * jnp.concatenate([bd(wa), bd(wi)], axis=-1)


def _rope_tables(n):
    rows = n // GRID_W
    row = np.repeat(np.arange(rows, dtype=np.float32), GRID_W)
    col = np.tile(np.arange(GRID_W, dtype=np.float32), rows)
    nf = DK // 4
    freqs = np.float32(ROPE_BASE) ** (-np.arange(nf, dtype=np.float32) / np.float32(nf))
    ang = np.concatenate([row[:, None] * freqs, col[:, None] * freqs], axis=-1).astype(np.float32)
    cos = np.cos(ang)
    sin = np.sin(ang)
    return (jnp.asarray(np.concatenate([cos, cos], axis=-1), _F32),
            jnp.asarray(np.concatenate([-sin, sin], axis=-1), _F32))


def _decay_consts(n):
    heads = np.arange(RET_H, dtype=np.float32)
    f32 = np.float32
    lgf = np.log1p(-np.exp2(-(f32(RET_DECAY_OFFSET_FWD) + heads))).astype(f32)
    lgb = np.log1p(-np.exp2(-(f32(RET_DECAY_OFFSET_BWD) + heads))).astype(f32)
    tab = np.stack([lgf, lgb, np.exp(lgf), np.exp(f32(n) * lgb), np.exp(f32(n - 1) * lgf)], axis=1)
    return [[float(v) for v in row] for row in tab.astype(f32)]


def kernel(x_prompt, x_sample, c, state_lru, state_ret, c_ctx, ada_w, ada_b, norm_mix_pre, norm_mix_post, norm_ffn_pre, norm_ffn_post, w_in, conv_w, conv_b, lru_wa, lru_ba, lru_wi, lru_bi, lru_lambda, w_out, router_w, exp_w_gate, exp_w_up, exp_w_down):
    bp, n_p, _ = x_prompt.shape
    bs, n_s, _ = x_sample.shape
    cap_p = 2 * n_p // N_EXP
    cap_s = 2 * n_s // N_EXP
    l = 0

    mod = _ada_call(c_ctx[None, :], c, ada_w[l], ada_b[l][None, :])

    norms = (norm_mix_pre[l][None], norm_mix_post[l][None], norm_ffn_pre[l][None])
    lru = (conv_w[l], conv_b[l][None], lru_ba[l], lru_bi[l], lru_lambda[l])
    wg = _block_diag_gates(lru_wa[l], lru_wi[l])
    rw = jnp.pad(router_w[l], ((0, 0), (0, LANES - N_EXP)))
    cos2, sin2 = _rope_tables(n_s)

    x1_p, h2_p, lt_p, st_lru, st_ret = _mixer_call(
        x_prompt, mod, norms, w_in[l], lru, wg, w_out[l], rw, latent=False)
    x1_s, h2_s, lt_s = _mixer_call(
        x_sample, mod, norms, w_in[l], lru, wg, w_out[l], rw, latent=True,
        extra=(state_lru, state_ret, cos2, sin2), layer=l)

    pos_p, gate_p, pos_s, gate_s = _route_call(lt_p, lt_s, cap_p, cap_s)
    xs_p, gsl_p = _dispatch_call(pos_p, gate_p, h2_p, cap_p)
    xs_s, gsl_s = _dispatch_call(pos_s, gate_s, h2_s, cap_s)
    ye = _expert_call(xs_p, xs_s, gsl_p, gsl_s, exp_w_gate[l], exp_w_up[l], exp_w_down[l])

    norm_post = norm_ffn_post[l][None]
    y_p = _combine_call(pos_p, ye, x1_p, mod, norm_post, cap_p, 0, latent=False)
    y_s = _combine_call(pos_s, ye, x1_s, mod, norm_post, cap_s, bp * cap_p, latent=True)
    return (y_p, y_s, st_lru, st_ret)
```

```python
import functools
import math
import types

import jax
import jax.numpy as jnp
import numpy as np
from jax import lax
from jax.experimental import pallas as pl
from jax.experimental.pallas import tpu as pltpu

D = 1024
LRU_W = 512
LRU_HEADS = 8
LRU_HD = 64
LRU_C = 8.0
RET_W = 512
RET_H = 4
DK = 128
N_EXP = 16
FF = 2048
N_MOD = 6
EPS = 1e-6
GRID_W = 64
ROPE_BASE = 10000.0
RET_DECAY_OFFSET_FWD = 5.0
RET_DECAY_OFFSET_BWD = 5.5

ROW_TILE = 256
WIDE_TILE = 512
FF_CHUNK = 512
COMBINE_TILE = 1024
CONTEXT_GROUP = 2
SMALL_STEP_TOKENS = 1024
SUBLANES = 8
LANES = 128
VMEM_LIMIT = 60 * 1024 * 1024

DEC_LOG_F, DEC_LOG_B, DEC_G_F, DEC_GN_B, DEC_GN1_F = 0, 1, 2, 3, 4

_BF = jnp.bfloat16
_F32 = jnp.float32


def _sigmoid(x):
    return 0.5 * jnp.tanh(0.5 * x) + 0.5


def _silu(x):
    return x * _sigmoid(x)


def _gelu_tanh(x):
    c = math.sqrt(2.0 / math.pi)
    return 0.5 * x * (1.0 + jnp.tanh(c * (x + 0.044715 * (x * x * x))))


def _rms(x, gain):
    return x * lax.rsqrt(jnp.mean(x * x, axis=-1, keepdims=True) + EPS) * gain


def _dot(a, b):
    return jnp.dot(a, b, preferred_element_type=_F32)


def _dot_nt(a, b):
    return lax.dot_general(a, b, (((1,), (1,)), ((), ())), preferred_element_type=_F32)


def _dot_tn(a, b):
    return lax.dot_general(a, b, (((0,), (0,)), ((), ())), preferred_element_type=_F32)


def _ada_kernel(cc_ref, c_ref, w_ref, b_ref, o_ref, s_ref):
    nb = c_ref.shape[0]
    s_ref[...] = jnp.zeros_like(s_ref)
    s_ref[0:1, :] = _silu(cc_ref[...])
    s_ref[1:1 + nb, :] = _silu(c_ref[...])
    o_ref[...] = _dot(s_ref[...], w_ref[...]) + b_ref[...]


def _ada_call(c_ctx, c, ada_w, ada_b):
    nb = c.shape[0]
    assert nb + 1 <= SUBLANES
    return pl.pallas_call(
        _ada_kernel,
        out_shape=jax.ShapeDtypeStruct((N_MOD, SUBLANES, D), _F32),
        grid=(N_MOD,),
        in_specs=[
            pl.BlockSpec((1, D), lambda j: (0, 0)),
            pl.BlockSpec((nb, D), lambda j: (0, 0)),
            pl.BlockSpec((D, D), lambda j: (0, j)),
            pl.BlockSpec((1, D), lambda j: (0, j)),
        ],
        out_specs=pl.BlockSpec((None, SUBLANES, D), lambda j: (j, 0, 0)),
        scratch_shapes=[pltpu.VMEM((SUBLANES, D), _F32)],
        compiler_params=pltpu.CompilerParams(
            dimension_semantics=("arbitrary",), vmem_limit_bytes=VMEM_LIMIT),
        name="ada_mod",
    )(c_ctx, c, ada_w, ada_b)


def _tile_loop(nt, body, unroll=1):
    if nt == 1:
        body(0, 0)
    else:
        lax.fori_loop(0, nt, body, 0, unroll=unroll)


def _mixer_kernel(*refs, n, g, latent):
    if latent:
        (x_ref, mod_ref, n1_ref, n2_ref, n3_ref, wl_ref, win_hbm,
         cw_ref, cb_ref, ba_ref, bi_ref, lam_ref, wg_hbm, wout_hbm, rw_ref,
         h0_ref, s0_ref, cos_hbm, sin_hbm,
         x1_ref, h2_ref, lt_ref,
         xlp_g, gy_g, af_g, ab_g, hf_g, hb_g, wret_v, wg_ref, cos_ref, sin_ref,
         wout_ref, late_sem) = refs
        sg_g, qf_g, qb_g, kf_g, kb_g, v_g = xlp_g, af_g, ab_g, hf_g, hb_g, gy_g
        wqk_ref, wvg_ref = wret_v.at[:, 0:D], wret_v.at[:, D:2 * D]
    else:
        (x_ref, mod_ref, n1_ref, n2_ref, n3_ref, wl_ref, wqk_ref, wvg_ref,
         cw_ref, cb_ref, ba_ref, bi_ref, lam_ref, wg_ref, wout_hbm, rw_ref,
         x1_ref, h2_ref, lt_ref, stl_ref, str_ref,
         xlp_g, gy_g, af_g, ab_g, hf_g, hb_g, sg_g, qf_g, qb_g, kf_g, kb_g, v_g,
         wout_ref, late_sem) = refs
    early_ret = not latent

    first_step = pl.program_id(0) == 0
    late_out = pltpu.make_async_copy(wout_hbm, wout_ref, late_sem.at[0])
    late_gate, late_ret = [], []
    if latent:
        late_gate = [pltpu.make_async_copy(wg_hbm, wg_ref, late_sem.at[1])]
        late_ret = [
            pltpu.make_async_copy(win_hbm.at[:, pl.ds(D, 2 * D)], wret_v, late_sem.at[2]),
            pltpu.make_async_copy(cos_hbm, cos_ref, late_sem.at[3]),
            pltpu.make_async_copy(sin_hbm, sin_ref, late_sem.at[4]),
        ]

    @pl.when(first_step)
    def _():
        for copy in late_gate + late_ret + [late_out]:
            copy.start()

    seqs = []
    for s in range(g):
        q = types.SimpleNamespace(
            x=x_ref.at[s], x1=x1_ref.at[s], h2=h2_ref.at[s], lt=lt_ref.at[s], mix=x1_ref.at[s],
            xlp=xlp_g.at[s], sg=sg_g.at[s], gy=gy_g.at[s], v=v_g.at[s],
            a_f=af_g.at[s], a_b=ab_g.at[s], h_f=hf_g.at[s], h_b=hb_g.at[s],
            q_f=qf_g.at[s], q_b=qb_g.at[s], k_f=kf_g.at[s], k_b=kb_g.at[s])
        if latent:
            q.h0, q.s0 = h0_ref.at[s], s0_ref.at[s]
        else:
            q.stl, q.str = stl_ref.at[s], str_ref.at[s]
        seqs.append(q)

    nt = n // ROW_TILE
    decay = _decay_consts(n)
    mod_row = (pl.program_id(0) + 1) if latent else 0
    mod = lambda k: mod_ref[k, pl.ds(mod_row, 1), :]
    shift = mod(0)
    scale = n1_ref[...] * (1.0 + mod(1))

    def normed(q, rows):
        return _rms(q.x[rows, :], scale) + shift

    wide = min(n, WIDE_TILE)
    a_tile = wide if latent else ROW_TILE

    def stacked(parts):
        return parts[0] if len(parts) == 1 else jnp.concatenate(parts, axis=0)

    def store_retention(q, r0, rows, pqk, pvg):
        q.sg[rows, :] = _silu(pvg[:, RET_W:])
        tpos = (r0 + lax.broadcasted_iota(jnp.int32, (pqk.shape[0], DK), 0)).astype(_F32)
        if latent:
            cos2 = cos_ref[rows, :]
            sin2 = sin_ref[rows, :]
        for hd in range(RET_H):
            cols = slice(hd * DK, (hd + 1) * DK)
            qh = pqk[:, cols] * (DK ** -0.5)
            kh = pqk[:, RET_W + hd * DK:RET_W + (hd + 1) * DK]
            if latent:
                qh = qh * cos2 + pltpu.roll(qh, DK // 2, axis=1) * sin2
                kh = kh * cos2 + pltpu.roll(kh, DK // 2, axis=1) * sin2
            lgf = decay[hd][DEC_LOG_F]
            lgb = decay[hd][DEC_LOG_B]
            q.q_f[hd, rows, :] = qh * jnp.exp(tpos * lgf)
            q.k_f[hd, rows, :] = kh * jnp.exp(tpos * (-lgf))
            q.q_b[hd, rows, :] = qh * jnp.exp(tpos * (-lgb))
            q.k_b[hd, rows, :] = kh * jnp.exp(tpos * lgb)
        q.v[rows, :] = pvg[:, 0:RET_W]

    def phase_a1(r, carry):
        r0 = pl.multiple_of(r * a_tile, a_tile)
        rows = pl.ds(r0, a_tile)
        h = stacked([normed(q, rows) for q in seqs])
        p_all = _dot(h, wl_ref[...])
        if early_ret:
            pqk_all = _dot(h, wqk_ref[...])
            pvg_all = _dot(h, wvg_ref[...])
        for s, q in enumerate(seqs):
            part = slice(s * a_tile, (s + 1) * a_tile)
            q.xlp[pl.ds(r0 + SUBLANES, a_tile), :] = p_all[part, 0:LRU_W]
            q.gy[rows, :] = _gelu_tanh(p_all[part, LRU_W:])
            if early_ret:
                store_retention(q, r0, rows, pqk_all[part], pvg_all[part])
        return carry

    for q in seqs:
        q.xlp[0:SUBLANES, :] = jnp.zeros((SUBLANES, LRU_W), _F32)
        q.xlp[n + SUBLANES:n + 2 * SUBLANES, :] = jnp.zeros((SUBLANES, LRU_W), _F32)

    half = LRU_W // 2

    def softplus_neg(lam):
        z = -lam
        return jnp.maximum(z, 0.0) + jnp.log1p(jnp.exp(-jnp.abs(z)))

    sp = (softplus_neg(lam_ref[0:1, :]), softplus_neg(lam_ref[1:2, :]))

    def phase_b(q):
        def body(r, carry):
            r0 = pl.multiple_of(r * a_tile, a_tile)
            rows = pl.ds(r0, a_tile)
            ext = q.xlp[pl.ds(r0, a_tile + 2 * SUBLANES), :]
            xc = cb_ref[...]
            for tap in range(4):
                back = (2 - tap) % (a_tile + 2 * SUBLANES)
                win = ext if back == 0 else pltpu.roll(ext, back, axis=0)
                xc = xc + win[SUBLANES:SUBLANES + a_tile, :] * cw_ref[tap:tap + 1, :]
            xh = 0.5 * xc
            for d, (a_ref, u_ref) in enumerate(((q.a_f, q.h_f), (q.a_b, q.h_b))):
                bah = 0.5 * ba_ref[d:d + 1, :]
                bih = 0.5 * bi_ref[d:d + 1, :]
                ch = (-0.5 * LRU_C) * sp[d]
                for hh in range(2):
                    cs = slice(hh * half, (hh + 1) * half)
                    pre = _dot(xc[:, cs], wg_ref[d, hh])
                    t_r = jnp.tanh(pre[:, 0:half] + bah[:, cs])
                    t_i = jnp.tanh(pre[:, half:] + bih[:, cs])
                    log_a = t_r * ch[:, cs] + ch[:, cs]
                    a = jnp.exp(log_a)
                    om = -jnp.tanh(log_a) * (a * a + 1.0)
                    root = jnp.where(om > 0.0, om * lax.rsqrt(om), 0.0)
                    u = root * (t_i * xh[:, cs] + xh[:, cs])
                    for j in range(half // LANES):
                        lanes = slice(j * LANES, (j + 1) * LANES)
                        a_ref[hh * (half // LANES) + j, rows, :] = a[:, lanes]
                        u_ref[hh * (half // LANES) + j, rows, :] = u[:, lanes]
            return carry
        return body

    _tile_loop(n // a_tile, phase_a1, unroll=2)
    if late_gate:
        @pl.when(first_step)
        def _():
            for copy in late_gate:
                copy.wait()

    for q in seqs:
        _tile_loop(n // a_tile, phase_b(q))

    row8 = lax.broadcasted_iota(jnp.int32, (SUBLANES, LANES), 0)
    block = SUBLANES * SUBLANES
    n_blocks = n // block
    n_slabs = LRU_W // LANES

    def across_groups(a, b, reverse):
        for s in (1, 2, 4):
            m = (row8 < SUBLANES - s) if reverse else (row8 >= s)
            shift = SUBLANES - s if reverse else s
            a_s = jnp.where(m, pltpu.roll(a, shift, axis=0), 1.0)
            b_s = jnp.where(m, pltpu.roll(b, shift, axis=0), 0.0)
            b = a * b_s + b
            a = a * a_s
        return a, b

    def scan_block(a_ref, h_ref, base, carry, reverse):
        rows = [pl.ds(base + k, SUBLANES, stride=SUBLANES) for k in range(SUBLANES)]
        order = list(reversed(range(SUBLANES))) if reverse else list(range(SUBLANES))
        prod, local = {}, {}
        prev = None
        for k in order:
            a, u = a_ref[rows[k], :], h_ref[rows[k], :]
            prod[k] = a if prev is None else a * prod[prev]
            local[k] = u if prev is None else a * local[prev] + u
            prev = k
        p_all, h_all = across_groups(prod[prev], local[prev], reverse)
        inner = (row8 < SUBLANES - 1) if reverse else (row8 >= 1)
        shift = SUBLANES - 1 if reverse else 1
        enter = (jnp.where(inner, pltpu.roll(p_all, shift, axis=0), 1.0) * carry
                 + jnp.where(inner, pltpu.roll(h_all, shift, axis=0), 0.0))
        for k in order:
            h_ref[rows[k], :] = prod[k] * enter + local[k]
        leave = p_all * carry + h_all
        return leave[0:1, :] if reverse else leave[SUBLANES - 1:SUBLANES, :]

    def initial(q, d, s):
        if latent:
            return q.h0[d:d + 1, s * LANES:(s + 1) * LANES]
        return jnp.zeros((1, LANES), _F32)

    chains = [(q, s) for q in seqs for s in range(n_slabs)]

    def scan_body(i, carry):
        fwd_base = pl.multiple_of(i * block, block)
        bwd_base = pl.multiple_of((n_blocks - 1 - i) * block, block)
        out = []
        for (q, s), (cf, cb) in zip(chains, carry):
            out.append((scan_block(q.a_f.at[s], q.h_f.at[s], fwd_base, cf, False),
                        scan_block(q.a_b.at[s], q.h_b.at[s], bwd_base, cb, True)))
        return tuple(out)

    ends = lax.fori_loop(0, n_blocks, scan_body,
                         tuple((initial(q, 0, s), initial(q, 1, s)) for q, s in chains), unroll=2)

    if not latent:
        @pl.when(first_step)
        def _():
            late_out.wait()

    def lru_out(q):
        def body(r, carry):
            rows = pl.ds(pl.multiple_of(r * ROW_TILE, ROW_TILE), ROW_TILE)
            for s in range(n_slabs):
                lanes = slice(s * LANES, (s + 1) * LANES)
                q.mix[rows, lanes] = (q.h_f[s, rows, :] + q.h_b[s, rows, :]) * q.gy[rows, lanes]
            return carry
        return body

    for (q, s), (last_f, first_b) in zip(chains, ends):
        if not latent:
            q.stl[0:1, s * LANES:(s + 1) * LANES] = last_f
            q.stl[1:2, s * LANES:(s + 1) * LANES] = first_b
    for q in seqs:
        _tile_loop(nt, lru_out(q))

    def phase_a2(r, carry):
        r0 = pl.multiple_of(r * wide, wide)
        rows = pl.ds(r0, wide)
        h = stacked([normed(q, rows) for q in seqs])
        pqk_all = _dot(h, wqk_ref[...])
        pvg_all = _dot(h, wvg_ref[...])
        for s, q in enumerate(seqs):
            part = slice(s * wide, (s + 1) * wide)
            store_retention(q, r0, rows, pqk_all[part], pvg_all[part])
        return carry

    if not early_ret:
        @pl.when(first_step)
        def _():
            for copy in late_ret:
                copy.wait()

        _tile_loop(n // wide, phase_a2)

    lower = (lax.broadcasted_iota(jnp.int32, (ROW_TILE, ROW_TILE), 0)
             >= lax.broadcasted_iota(jnp.int32, (ROW_TILE, ROW_TILE), 1))
    blocks = [slice(r * ROW_TILE, (r + 1) * ROW_TILE) for r in range(nt)]

    def phase_c(q, hd):
        cols = slice(hd * DK, (hd + 1) * DK)
        dec = lambda row: decay[hd][row]
        if nt > 1:
            kv = [_dot_tn(jnp.concatenate([q.k_f[hd, rows, :], q.k_b[hd, rows, :]], axis=1),
                          q.v[rows, cols]) for rows in blocks]
            kv_f = [m[0:DK] for m in kv]
            kv_b = [m[DK:2 * DK] for m in kv]
        else:
            kv_f = [_dot_tn(q.k_f[hd, rows, :], q.v[rows, cols]) for rows in blocks]
            kv_b = [_dot_tn(q.k_b[hd, rows, :], q.v[rows, cols]) for rows in blocks]
        if latent:
            run_f = q.s0[0, hd] * dec(DEC_G_F)
            run_b = q.s0[1, hd] * dec(DEC_GN_B)
        else:
            run_f = run_b = None
        before = []
        for r in range(nt):
            before.append(run_f)
            run_f = kv_f[r] if run_f is None else run_f + kv_f[r]
        after = [None] * nt
        for r in reversed(range(nt)):
            after[r] = run_b
            run_b = kv_b[r] if run_b is None else run_b + kv_b[r]

        for r, rows in enumerate(blocks):
            qf = q.q_f[hd, rows, :]
            qb = q.q_b[hd, rows, :]
            s = jnp.where(lower, _dot_nt(qf, q.k_f[hd, rows, :]), _dot_nt(qb, q.k_b[hd, rows, :]))
            o = _dot(s, q.v[rows, cols])
            if before[r] is not None and after[r] is not None:
                o = o + _dot(jnp.concatenate([qf, qb], axis=1),
                             jnp.concatenate([before[r], after[r]], axis=0))
            elif before[r] is not None:
                o = o + _dot(qf, before[r])
            elif after[r] is not None:
                o = o + _dot(qb, after[r])
            o = o * lax.rsqrt(jnp.mean(o * o, axis=-1, keepdims=True) + EPS)
            q.mix[rows, LRU_W + cols.start:LRU_W + cols.stop] = o * q.sg[rows, cols]
        if not latent:
            q.str[0, hd] = run_f * dec(DEC_GN1_F)
            q.str[1, hd] = run_b

    for hd in range(RET_H):
        for q in seqs:
            phase_c(q, hd)

    gain1 = mod(2) * n2_ref[...]
    gain2 = n3_ref[...] * (1.0 + mod(4))
    sh2 = mod(3)

    def phase_d(r, carry):
        rows = pl.ds(pl.multiple_of(r * wide, wide), wide)
        mix_all = _dot(stacked([q.mix[rows, :] for q in seqs]), wout_ref[...])
        h2s = []
        for s, q in enumerate(seqs):
            x1 = q.x[rows, :] + _rms(mix_all[s * wide:(s + 1) * wide], gain1)
            q.x1[rows, :] = x1
            h2s.append(_rms(x1, gain2) + sh2)
            q.h2[rows, :] = h2s[-1].astype(_BF)
        logits_t = _dot(stacked(h2s), rw_ref[...]).T
        for s, q in enumerate(seqs):
            q.lt[:, rows] = logits_t[0:N_EXP, s * wide:(s + 1) * wide]
        return carry

    if latent:
        @pl.when(first_step)
        def _():
            late_out.wait()

    _tile_loop(n // wide, phase_d, unroll=2)


def _mixer_call(x, mod, norms, w_in, lru, wg, w_out, rw, latent, extra=(), layer=0):
    b, n, _ = x.shape
    g = 1 if latent else CONTEXT_GROUP
    const2 = lambda i: (0, 0)
    whole = pl.BlockSpec(memory_space=pl.ANY)
    in_specs = [
        pl.BlockSpec((g, n, D), lambda i: (i, 0, 0)),
        pl.BlockSpec((N_MOD, SUBLANES, D), lambda i: (0, 0, 0)),
        pl.BlockSpec((1, D), const2), pl.BlockSpec((1, D), const2), pl.BlockSpec((1, D), const2),
        pl.BlockSpec((D, D), lambda i: (0, 0)),
    ]
    if latent:
        in_specs += [whole]
    else:
        in_specs += [pl.BlockSpec((D, D), lambda i: (0, 1)),
                     pl.BlockSpec((D, D), lambda i: (0, 2))]
    in_specs += [
        pl.BlockSpec((4, LRU_W), const2), pl.BlockSpec((1, LRU_W), const2),
        pl.BlockSpec((2, LRU_W), const2), pl.BlockSpec((2, LRU_W), const2),
        pl.BlockSpec((2, LRU_W), const2),
        whole if latent else pl.BlockSpec((2, 2, LRU_W // 2, LRU_W), lambda i: (0, 0, 0, 0)),
        whole,
        pl.BlockSpec((D, LANES), const2),
    ]
    out_shape = [
        jax.ShapeDtypeStruct((b, n, D), _F32),
        jax.ShapeDtypeStruct((b, n, D), _BF),
        jax.ShapeDtypeStruct((b, N_EXP, n), _F32),
    ]
    out_specs = [
        pl.BlockSpec((g, n, D), lambda i: (i, 0, 0)),
        pl.BlockSpec((g, n, D), lambda i: (i, 0, 0)),
        pl.BlockSpec((g, N_EXP, n), lambda i: (i, 0, 0)),
    ]
    if latent:
        in_specs += [
            pl.BlockSpec((g, None, 2, LRU_W), lambda i: (i, layer, 0, 0)),
            pl.BlockSpec((g, None, 2, RET_H, DK, DK), lambda i: (i, layer, 0, 0, 0, 0)),
            whole, whole,
        ]
    else:
        out_shape += [
            jax.ShapeDtypeStruct((b, 1, 2, LRU_W), _F32),
            jax.ShapeDtypeStruct((b, 1, 2, RET_H, DK, DK), _F32),
        ]
        out_specs += [
            pl.BlockSpec((g, None, 2, LRU_W), lambda i: (i, 0, 0, 0)),
            pl.BlockSpec((g, None, 2, RET_H, DK, DK), lambda i: (i, 0, 0, 0, 0, 0)),
        ]
    f32s = lambda shape: pltpu.VMEM((g,) + shape, _F32)
    slabs = (LRU_W // LANES, n, LANES)
    scratch = [
        f32s((n + 2 * SUBLANES, LRU_W)),
        f32s((n, LRU_W)),
        f32s(slabs), f32s(slabs),
        f32s(slabs), f32s(slabs),
    ]
    if latent:
        scratch += [pltpu.VMEM((D, 2 * D), _F32), pltpu.VMEM(wg.shape, _F32),
                    pltpu.VMEM((n, DK), _F32), pltpu.VMEM((n, DK), _F32)]
        w_in_args = (w_in, w_in)
    else:
        scratch += [f32s((n, RET_W))] + [f32s(slabs) for _ in range(4)] + [f32s((n, RET_W))]
        w_in_args = (w_in, w_in, w_in)
    scratch += [pltpu.VMEM((D, D), _F32), pltpu.SemaphoreType.DMA((5 if latent else 1,))]
    return pl.pallas_call(
        functools.partial(_mixer_kernel, n=n, g=g, latent=latent),
        out_shape=out_shape,
        grid=(b // g,),
        in_specs=in_specs,
        out_specs=out_specs,
        scratch_shapes=scratch,
        compiler_params=pltpu.CompilerParams(
            dimension_semantics=("arbitrary",), vmem_limit_bytes=VMEM_LIMIT),
        name="mixer_latent" if latent else "mixer_context",
    )(x, mod, *norms, *w_in_args, *lru, wg, w_out, rw, *extra)


def _count(mask):
    return jnp.sum(jnp.where(mask, 1.0, 0.0), axis=-1, keepdims=True)


def _probs(l3):
    bsz, _, n = l3.shape
    m = jnp.max(l3, axis=1, keepdims=True)
    e = jnp.exp(l3 - m)
    return (e / jnp.sum(e, axis=1, keepdims=True)).reshape(bsz * N_EXP, n)


def _break_ties(parts):
    prep = []
    for bits, thr, cap in parts:
        rows, n = bits.shape
        eq = bits == thr
        need = float(cap) - _count(bits > thr)
        idx = lax.broadcasted_iota(jnp.int32, (rows, n), 1)
        prep.append((eq, need, idx, int(math.log2(n))))
    most = max(nbits for _, _, _, nbits in prep)

    def idx_body(i, last):
        out = []
        for (eq, need, idx, nbits), j in zip(prep, last):
            shift = nbits - 1 - i
            cand = j | jnp.where(shift >= 0, jnp.int32(1) << jnp.maximum(shift, 0), 0)
            out.append(jnp.where(_count(eq & (idx < cand)) < need, cand, j))
        return tuple(out)

    return lax.fori_loop(0, most, idx_body,
                         tuple(jnp.zeros((eq.shape[0], 1), jnp.int32) for eq, _, _, _ in prep))


def _slots(p, bits, thr, jlast):
    rows, n = p.shape
    idx = lax.broadcasted_iota(jnp.int32, (rows, n), 1)
    sel = (bits > thr) | ((bits == thr) & (idx <= jlast))
    before = (lax.broadcasted_iota(jnp.int32, (n, n), 0)
              < lax.broadcasted_iota(jnp.int32, (n, n), 1))
    pos = _dot(jnp.where(sel, 1.0, 0.0).astype(_BF), jnp.where(before, 1.0, 0.0).astype(_BF))
    return jnp.where(sel, pos, -1.0), jnp.where(sel, p, 0.0)


def _route_kernel(lp_ref, ls_ref, pp_ref, gp_ref, ps_ref, gs_ref, *, cap_p, cap_s):
    groups = ((_probs(lp_ref[...]), float(cap_p)), (_probs(ls_ref[...]), float(cap_s)))
    bits = [pltpu.bitcast(p, jnp.int32) for p, _ in groups]

    def settle(b, t, capf, hi, lo):
        with_hi = t | hi
        both = with_hi | lo
        with_lo = t | lo
        ok = lambda cand: _count(b >= cand) >= capf
        return jnp.where(ok(with_hi), jnp.where(ok(both), both, with_hi),
                         jnp.where(ok(with_lo), with_lo, t))

    def val_body(i, thr):
        hi = jnp.int32(1) << (29 - 2 * i)
        lo = jnp.int32(1) << (28 - 2 * i)
        return tuple(settle(b, t, capf, hi, lo) for b, t, (_, capf) in zip(bits, thr, groups))

    thr = lax.fori_loop(0, 15, val_body,
                        tuple(jnp.zeros((b.shape[0], 1), jnp.int32) for b in bits))
    jlast = _break_ties([(bits[0], thr[0], cap_p), (bits[1], thr[1], cap_s)])
    pos, gate = _slots(groups[0][0], bits[0], thr[0], jlast[0])
    pp_ref[...] = pos.reshape(pp_ref.shape)
    gp_ref[...] = gate.reshape(gp_ref.shape)
    pos, gate = _slots(groups[1][0], bits[1], thr[1], jlast[1])
    ps_ref[...] = pos.reshape(ps_ref.shape)
    gs_ref[...] = gate.reshape(gs_ref.shape)


def _route_call(lt_p, lt_s, cap_p, cap_s):
    shapes = [
        jax.ShapeDtypeStruct(lt_p.shape, _F32), jax.ShapeDtypeStruct(lt_p.shape, _F32),
        jax.ShapeDtypeStruct(lt_s.shape, _F32), jax.ShapeDtypeStruct(lt_s.shape, _F32),
    ]
    return pl.pallas_call(
        functools.partial(_route_kernel, cap_p=cap_p, cap_s=cap_s),
        out_shape=shapes,
        compiler_params=pltpu.CompilerParams(vmem_limit_bytes=VMEM_LIMIT),
        name="route_select",
    )(lt_p, lt_s)


def _dispatch_kernel(pos_ref, gate_ref, h_ref, xs_ref, gs_ref, *, n, cap, g):
    slot = lax.broadcasted_iota(jnp.int32, (cap, n), 0).astype(_F32)
    for j in range(g):
        slots = slice(j * cap, (j + 1) * cap)
        parts = []
        for e in range(N_EXP):
            hit = pos_ref[j, e:e + 1, :] == slot
            parts.append(jnp.where(hit, 1.0, 0.0).astype(_BF))
            gs_ref[e, slots, :] = jnp.sum(jnp.where(hit, gate_ref[j, e:e + 1, :], 0.0),
                                          axis=-1, keepdims=True)
        onehot = jnp.concatenate(parts, axis=0)
        xs = _dot(onehot, h_ref[j]).astype(_BF)
        for e in range(N_EXP):
            xs_ref[e, slots, :] = xs[e * cap:(e + 1) * cap, :]


def _dispatch_call(pos, gate, h2, cap):
    b, n, _ = h2.shape
    g = max(1, SMALL_STEP_TOKENS // n)
    return pl.pallas_call(
        functools.partial(_dispatch_kernel, n=n, cap=cap, g=g),
        out_shape=[
            jax.ShapeDtypeStruct((N_EXP, b * cap, D), _BF),
            jax.ShapeDtypeStruct((N_EXP, b * cap, 1), _F32),
        ],
        grid=(b // g,),
        in_specs=[
            pl.BlockSpec((g, N_EXP, n), lambda i: (i, 0, 0)),
            pl.BlockSpec((g, N_EXP, n), lambda i: (i, 0, 0)),
            pl.BlockSpec((g, n, D), lambda i: (i, 0, 0)),
        ],
        out_specs=[
            pl.BlockSpec((N_EXP, g * cap, D), lambda i: (0, i, 0)),
            pl.BlockSpec((N_EXP, g * cap, 1), lambda i: (0, i, 0)),
        ],
        compiler_params=pltpu.CompilerParams(
            dimension_semantics=("arbitrary",), vmem_limit_bytes=VMEM_LIMIT),
        name="dispatch",
    )(pos, gate, h2)


def _expert_kernel(xp_ref, xs_ref, gp_ref, gs_ref, wg_ref, wu_ref, wd_ref, y_ref, xcat, acc,
                   *, sp, nf, tf):
    f = pl.program_id(1)
    xcat[0:sp, :] = xp_ref[...]
    xcat[sp:, :] = xs_ref[...]
    x = xcat[...]
    total = jnp.where(f == 0, 0.0, acc[...])
    for c in range(tf // FF_CHUNK):
        cs = slice(c * FF_CHUNK, (c + 1) * FF_CHUNK)
        hg = _dot(x, wg_ref[:, cs].astype(_BF))
        hu = _dot(x, wu_ref[:, cs].astype(_BF))
        hid = (_silu(hg) * hu).astype(_BF)
        total = total + _dot(hid, wd_ref[cs, :].astype(_BF))
    acc[...] = total
    y_ref[0:sp, :] = (total[0:sp, :] * gp_ref[...]).astype(_BF)
    y_ref[sp:, :] = (total[sp:, :] * gs_ref[...]).astype(_BF)


def _expert_call(xs_p, xs_s, g_p, g_s, w_gate, w_up, w_down):
    tf = 1024
    sp = xs_p.shape[1]
    ss = xs_s.shape[1]
    nf = FF // tf
    return pl.pallas_call(
        functools.partial(_expert_kernel, sp=sp, nf=nf, tf=tf),
        out_shape=jax.ShapeDtypeStruct((N_EXP, sp + ss, D), _BF),
        grid=(N_EXP, nf),
        in_specs=[
            pl.BlockSpec((None, sp, D), lambda e, f: (e, 0, 0)),
            pl.BlockSpec((None, ss, D), lambda e, f: (e, 0, 0)),
            pl.BlockSpec((None, sp, 1), lambda e, f: (e, 0, 0)),
            pl.BlockSpec((None, ss, 1), lambda e, f: (e, 0, 0)),
            pl.BlockSpec((None, D, tf), lambda e, f: (e, 0, f)),
            pl.BlockSpec((None, D, tf), lambda e, f: (e, 0, f)),
            pl.BlockSpec((None, tf, D), lambda e, f: (e, f, 0)),
        ],
        out_specs=pl.BlockSpec((None, sp + ss, D), lambda e, f: (e, 0, 0)),
        scratch_shapes=[pltpu.VMEM((sp + ss, D), _BF), pltpu.VMEM((sp + ss, D), _F32)],
        compiler_params=pltpu.CompilerParams(
            dimension_semantics=("arbitrary", "arbitrary"), vmem_limit_bytes=VMEM_LIMIT),
        name="expert_ffn",
    )(xs_p, xs_s, g_p, g_s, w_gate, w_up, w_down)


def _combine_kernel(pos_ref, ye_ref, x1_ref, mod_ref, n4_ref, y_ref, *, n, cap, g, latent):
    width = N_EXP * cap
    lane = lax.broadcasted_iota(jnp.int32, (N_EXP, width), 1)
    expand = jnp.where(lane // cap == lax.broadcasted_iota(jnp.int32, (N_EXP, width), 0),
                       1.0, 0.0).astype(_BF)
    tile = min(n, COMBINE_TILE)
    slot = (lax.broadcasted_iota(jnp.int32, (tile, width), 1) % cap).astype(_F32)
    mod_row = (pl.program_id(0) + 1) if latent else 0
    gain = mod_ref[N_MOD - 1, pl.ds(mod_row, 1), :] * n4_ref[...]

    for j in range(g):
        ye = ye_ref[:, j * cap:(j + 1) * cap, :].reshape(width, D)

        def body(r, carry, j=j, ye=ye):
            rows = pl.ds(pl.multiple_of(r * tile, tile), tile)
            pos_e = _dot_tn(pos_ref[j, :, rows].astype(_BF), expand)
            onehot = jnp.where(pos_e == slot, 1.0, 0.0).astype(_BF)
            f = _dot(onehot, ye)
            y_ref[j, rows, :] = x1_ref[j, rows, :] + _rms(f, gain)
            return carry

        _tile_loop(n // tile, body)


def _combine_call(pos, ye, x1, mod, norm_post, cap, slot_off, latent):
    b, n, _ = x1.shape
    g = max(1, SMALL_STEP_TOKENS // n)
    assert g == 1 or not latent
    blk_off = slot_off // (g * cap)
    return pl.pallas_call(
        functools.partial(_combine_kernel, n=n, cap=cap, g=g, latent=latent),
        out_shape=jax.ShapeDtypeStruct((b, n, D), _F32),
        grid=(b // g,),
        in_specs=[
            pl.BlockSpec((g, N_EXP, n), lambda i: (i, 0, 0)),
            pl.BlockSpec((N_EXP, g * cap, D), lambda i: (0, i + blk_off, 0)),
            pl.BlockSpec((g, n, D), lambda i: (i, 0, 0)),
            pl.BlockSpec((N_MOD, SUBLANES, D), lambda i: (0, 0, 0)),
            pl.BlockSpec((1, D), lambda i: (0, 0)),
        ],
        out_specs=pl.BlockSpec((g, n, D), lambda i: (i, 0, 0)),
        compiler_params=pltpu.CompilerParams(
            dimension_semantics=("arbitrary",), vmem_limit_bytes=VMEM_LIMIT),
        name="combine_latent" if latent else "combine_context",
    )(pos, ye, x1, mod, norm_post)


def _block_diag_gates(wa, wi):
    per_half = LRU_HEADS // 2
    side = per_half * LRU_HD
    on_diag = (np.arange(side)[:, None] // LRU_HD) == (np.arange(side)[None, :] // LRU_HD)

    def bd(w):
        rows = w.reshape(2, 2, side, LRU_HD)
        return jnp.where(on_diag, jnp.tile(rows, (1, 1, 1, per_half)), 0.0)

    return 0.5 * jnp.concatenate([bd(wa), bd(wi)], axis=-1)


def _rope_tables(n):
    rows = n // GRID_W
    row = np.repeat(np.arange(rows, dtype=np.float32), GRID_W)
    col = np.tile(np.arange(GRID_W, dtype=np.float32), rows)
    nf = DK // 4
    freqs = np.float32(ROPE_BASE) ** (-np.arange(nf, dtype=np.float32) / np.float32(nf))
    ang = np.concatenate([row[:, None] * freqs, col[:, None] * freqs], axis=-1).astype(np.float32)
    cos = np.cos(ang)
    sin = np.sin(ang)
    return (jnp.asarray(np.concatenate([cos, cos], axis=-1), _F32),
            jnp.asarray(np.concatenate([-sin, sin], axis=-1), _F32))


def _decay_consts(n):
    heads = np.arange(RET_H, dtype=np.float32)
    f32 = np.float32
    lgf = np.log1p(-np.exp2(-(f32(RET_DECAY_OFFSET_FWD) + heads))).astype(f32)
    lgb = np.log1p(-np.exp2(-(f32(RET_DECAY_OFFSET_BWD) + heads))).astype(f32)
    tab = np.stack([lgf, lgb, np.exp(lgf), np.exp(f32(n) * lgb), np.exp(f32(n - 1) * lgf)], axis=1)
    return [[float(v) for v in row] for row in tab.astype(f32)]


def kernel(x_prompt, x_sample, c, state_lru, state_ret, c_ctx, ada_w, ada_b, norm_mix_pre, norm_mix_post, norm_ffn_pre, norm_ffn_post, w_in, conv_w, conv_b, lru_wa, lru_ba, lru_wi, lru_bi, lru_lambda, w_out, router_w, exp_w_gate, exp_w_up, exp_w_down):
    bp, n_p, _ = x_prompt.shape
    bs, n_s, _ = x_sample.shape
    cap_p = 2 * n_p // N_EXP
    cap_s = 2 * n_s // N_EXP
    l = 0

    mod = _ada_call(c_ctx[None, :], c, ada_w[l], ada_b[l][None, :])

    norms = (norm_mix_pre[l][None], norm_mix_post[l][None], norm_ffn_pre[l][None])
    lru = (conv_w[l], conv_b[l][None], lru_ba[l], lru_bi[l], lru_lambda[l])
    wg = _block_diag_gates(lru_wa[l], lru_wi[l])
    rw = jnp.pad(router_w[l], ((0, 0), (0, LANES - N_EXP)))
    cos2, sin2 = _rope_tables(n_s)

    x1_p, h2_p, lt_p, st_lru, st_ret = _mixer_call(
        x_prompt, mod, norms, w_in[l], lru, wg, w_out[l], rw, latent=False)
    x1_s, h2_s, lt_s = _mixer_call(
        x_sample, mod, norms, w_in[l], lru, wg, w_out[l], rw, latent=True,
        extra=(state_lru, state_ret, cos2, sin2), layer=l)

    pos_p, gate_p, pos_s, gate_s = _route_call(lt_p, lt_s, cap_p, cap_s)
    xs_p, gsl_p = _dispatch_call(pos_p, gate_p, h2_p, cap_p)
    xs_s, gsl_s = _dispatch_call(pos_s, gate_s, h2_s, cap_s)
    ye = _expert_call(xs_p, xs_s, gsl_p, gsl_s, exp_w_gate[l], exp_w_up[l], exp_w_down[l])

    norm_post = norm_ffn_post[l][None]
    y_p = _combine_call(pos_p, ye, x1_p, mod, norm_post, cap_p, 0, latent=False)
    y_s = _combine_call(pos_s, ye, x1_s, mod, norm_post, cap_s, bp * cap_p, latent=True)
    return (y_p, y_s, st_lru, st_ret)
```

```python
import functools
import math
import types

import jax
import jax.numpy as jnp
import numpy as np
from jax import lax
from jax.experimental import pallas as pl
from jax.experimental.pallas import tpu as pltpu

D = 1024
LRU_W = 512
LRU_HEADS = 8
LRU_HD = 64
LRU_C = 8.0
RET_W = 512
RET_H = 4
DK = 128
N_EXP = 16
FF = 2048
N_MOD = 6
EPS = 1e-6
GRID_W = 64
ROPE_BASE = 10000.0
RET_DECAY_OFFSET_FWD = 5.0
RET_DECAY_OFFSET_BWD = 5.5

ROW_TILE = 256
WIDE_TILE = 512
FF_CHUNK = 512
COMBINE_TILE = 1024
CONTEXT_GROUP = 2
SMALL_STEP_TOKENS = 1024
SUBLANES = 8
LANES = 128
VMEM_LIMIT = 60 * 1024 * 1024

DEC_LOG_F, DEC_LOG_B, DEC_G_F, DEC_GN_B, DEC_GN1_F = 0, 1, 2, 3, 4

_BF = jnp.bfloat16
_F32 = jnp.float32


def _sigmoid(x):
    return 0.5 * jnp.tanh(0.5 * x) + 0.5


def _silu(x):
    return x * _sigmoid(x)


def _gelu_tanh(x):
    c = math.sqrt(2.0 / math.pi)
    return 0.5 * x * (1.0 + jnp.tanh(c * (x + 0.044715 * (x * x * x))))


def _rms(x, gain):
    return x * lax.rsqrt(jnp.mean(x * x, axis=-1, keepdims=True) + EPS) * gain


def _dot(a, b):
    return jnp.dot(a, b, preferred_element_type=_F32)


def _dot_nt(a, b):
    return lax.dot_general(a, b, (((1,), (1,)), ((), ())), preferred_element_type=_F32)


def _dot_tn(a, b):
    return lax.dot_general(a, b, (((0,), (0,)), ((), ())), preferred_element_type=_F32)


def _ada_kernel(cc_ref, c_ref, w_ref, b_ref, o_ref, s_ref):
    nb = c_ref.shape[0]
    s_ref[...] = jnp.zeros_like(s_ref)
    s_ref[0:1, :] = _silu(cc_ref[...])
    s_ref[1:1 + nb, :] = _silu(c_ref[...])
    o_ref[...] = _dot(s_ref[...], w_ref[...]) + b_ref[...]


def _ada_call(c_ctx, c, ada_w, ada_b):
    nb = c.shape[0]
    assert nb + 1 <= SUBLANES
    return pl.pallas_call(
        _ada_kernel,
        out_shape=jax.ShapeDtypeStruct((N_MOD, SUBLANES, D), _F32),
        grid=(N_MOD,),
        in_specs=[
            pl.BlockSpec((1, D), lambda j: (0, 0)),
            pl.BlockSpec((nb, D), lambda j: (0, 0)),
            pl.BlockSpec((D, D), lambda j: (0, j)),
            pl.BlockSpec((1, D), lambda j: (0, j)),
        ],
        out_specs=pl.BlockSpec((None, SUBLANES, D), lambda j: (j, 0, 0)),
        scratch_shapes=[pltpu.VMEM((SUBLANES, D), _F32)],
        compiler_params=pltpu.CompilerParams(
            dimension_semantics=("arbitrary",), vmem_limit_bytes=VMEM_LIMIT),
        name="ada_mod",
    )(c_ctx, c, ada_w, ada_b)


def _tile_loop(nt, body, unroll=1):
    if nt == 1:
        body(0, 0)
    else:
        lax.fori_loop(0, nt, body, 0, unroll=unroll)


def _mixer_kernel(*refs, n, g, latent):
    if latent:
        (x_ref, mod_ref, n1_ref, n2_ref, n3_ref, wl_ref, win_hbm,
         cw_ref, cb_ref, ba_ref, bi_ref, lam_ref, wg_hbm, wout_hbm, rw_ref,
         h0_ref, s0_ref, cos_hbm, sin_hbm,
         x1_ref, h2_ref, lt_ref,
         xlp_g, gy_g, af_g, ab_g, hf_g, hb_g, wret_v, wg_ref, cos_ref, sin_ref,
         wout_ref, late_sem) = refs
        sg_g, qf_g, qb_g, kf_g, kb_g, v_g = xlp_g, af_g, ab_g, hf_g, hb_g, gy_g
        wqk_ref, wvg_ref = wret_v.at[:, 0:D], wret_v.at[:, D:2 * D]
    else:
        (x_ref, mod_ref, n1_ref, n2_ref, n3_ref, wl_ref, wqk_ref, wvg_ref,
         cw_ref, cb_ref, ba_ref, bi_ref, lam_ref, wg_ref, wout_hbm, rw_ref,
         x1_ref, h2_ref, lt_ref, stl_ref, str_ref,
         xlp_g, gy_g, af_g, ab_g, hf_g, hb_g, sg_g, qf_g, qb_g, kf_g, kb_g, v_g,
         wout_ref, late_sem) = refs
    early_ret = not latent

    first_step = pl.program_id(0) == 0
    late_out = pltpu.make_async_copy(wout_hbm, wout_ref, late_sem.at[0])
    late_gate, late_ret = [], []
    if latent:
        late_gate = [pltpu.make_async_copy(wg_hbm, wg_ref, late_sem.at[1])]
        late_ret = [
            pltpu.make_async_copy(win_hbm.at[:, pl.ds(D, 2 * D)], wret_v, late_sem.at[2]),
            pltpu.make_async_copy(cos_hbm, cos_ref, late_sem.at[3]),
            pltpu.make_async_copy(sin_hbm, sin_ref, late_sem.at[4]),
        ]

    @pl.when(first_step)
    def _():
        for copy in late_gate + late_ret + [late_out]:
            copy.start()

    seqs = []
    for s in range(g):
        q = types.SimpleNamespace(
            x=x_ref.at[s], x1=x1_ref.at[s], h2=h2_ref.at[s], lt=lt_ref.at[s], mix=x1_ref.at[s],
            xlp=xlp_g.at[s], sg=sg_g.at[s], gy=gy_g.at[s], v=v_g.at[s],
            a_f=af_g.at[s], a_b=ab_g.at[s], h_f=hf_g.at[s], h_b=hb_g.at[s],
            q_f=qf_g.at[s], q_b=qb_g.at[s], k_f=kf_g.at[s], k_b=kb_g.at[s])
        if latent:
            q.h0, q.s0 = h0_ref.at[s], s0_ref.at[s]
        else:
            q.stl, q.str = stl_ref.at[s], str_ref.at[s]
        seqs.append(q)

    nt = n // ROW_TILE
    decay = _decay_consts(n)
    mod_row = (pl.program_id(0) + 1) if latent else 0
    mod = lambda k: mod_ref[k, pl.ds(mod_row, 1), :]
    shift = mod(0)
    scale = n1_ref[...] * (1.0 + mod(1))

    def normed(q, rows):
        return _rms(q.x[rows, :], scale) + shift

    wide = min(n, WIDE_TILE)
    a_tile = wide if latent else ROW_TILE

    def stacked(parts):
        return parts[0] if len(parts) == 1 else jnp.concatenate(parts, axis=0)

    def store_retention(q, r0, rows, pqk, pvg):
        q.sg[rows, :] = _silu(pvg[:, RET_W:])
        tpos = (r0 + lax.broadcasted_iota(jnp.int32, (pqk.shape[0], DK), 0)).astype(_F32)
        if latent:
            cos2 = cos_ref[rows, :]
            sin2 = sin_ref[rows, :]
        for hd in range(RET_H):
            cols = slice(hd * DK, (hd + 1) * DK)
            qh = pqk[:, cols] * (DK ** -0.5)
            kh = pqk[:, RET_W + hd * DK:RET_W + (hd + 1) * DK]
            if latent:
                qh = qh * cos2 + pltpu.roll(qh, DK // 2, axis=1) * sin2
                kh = kh * cos2 + pltpu.roll(kh, DK // 2, axis=1) * sin2
            lgf = decay[hd][DEC_LOG_F]
            lgb = decay[hd][DEC_LOG_B]
            q.q_f[hd, rows, :] = qh * jnp.exp(tpos * lgf)
            q.k_f[hd, rows, :] = kh * jnp.exp(tpos * (-lgf))
            q.q_b[hd, rows, :] = qh * jnp.exp(tpos * (-lgb))
            q.k_b[hd, rows, :] = kh * jnp.exp(tpos * lgb)
        q.v[rows, :] = pvg[:, 0:RET_W]

    def phase_a1(r, carry):
        r0 = pl.multiple_of(r * a_tile, a_tile)
        rows = pl.ds(r0, a_tile)
        h = stacked([normed(q, rows) for q in seqs])
        p_all = _dot(h, wl_ref[...])
        if early_ret:
            pqk_all = _dot(h, wqk_ref[...])
            pvg_all = _dot(h, wvg_ref[...])
        for s, q in enumerate(seqs):
            part = slice(s * a_tile, (s + 1) * a_tile)
            q.xlp[pl.ds(r0 + SUBLANES, a_tile), :] = p_all[part, 0:LRU_W]
            q.gy[rows, :] = _gelu_tanh(p_all[part, LRU_W:])
            if early_ret:
                store_retention(q, r0, rows, pqk_all[part], pvg_all[part])
        return carry

    for q in seqs:
        q.xlp[0:SUBLANES, :] = jnp.zeros((SUBLANES, LRU_W), _F32)
        q.xlp[n + SUBLANES:n + 2 * SUBLANES, :] = jnp.zeros((SUBLANES, LRU_W), _F32)

    half = LRU_W // 2

    def softplus_neg(lam):
        z = -lam
        return jnp.maximum(z, 0.0) + jnp.log1p(jnp.exp(-jnp.abs(z)))

    sp = (softplus_neg(lam_ref[0:1, :]), softplus_neg(lam_ref[1:2, :]))

    def phase_b(q):
        def body(r, carry):
            r0 = pl.multiple_of(r * a_tile, a_tile)
            rows = pl.ds(r0, a_tile)
            ext = q.xlp[pl.ds(r0, a_tile + 2 * SUBLANES), :]
            xc = cb_ref[...]
            for tap in range(4):
                back = (2 - tap) % (a_tile + 2 * SUBLANES)
                win = ext if back == 0 else pltpu.roll(ext, back, axis=0)
                xc = xc + win[SUBLANES:SUBLANES + a_tile, :] * cw_ref[tap:tap + 1, :]
            xh = 0.5 * xc
            for d, (a_ref, u_ref) in enumerate(((q.a_f, q.h_f), (q.a_b, q.h_b))):
                bah = 0.5 * ba_ref[d:d + 1, :]
                bih = 0.5 * bi_ref[d:d + 1, :]
                ch = (-0.5 * LRU_C) * sp[d]
                for hh in range(2):
                    cs = slice(hh * half, (hh + 1) * half)
                    pre = _dot(xc[:, cs], wg_ref[d, hh])
                    t_r = jnp.tanh(pre[:, 0:half] + bah[:, cs])
                    t_i = jnp.tanh(pre[:, half:] + bih[:, cs])
                    log_a = t_r * ch[:, cs] + ch[:, cs]
                    a = jnp.exp(log_a)
                    om = -jnp.tanh(log_a) * (a * a + 1.0)
                    root = jnp.where(om > 0.0, om * lax.rsqrt(om), 0.0)
                    u = root * (t_i * xh[:, cs] + xh[:, cs])
                    for j in range(half // LANES):
                        lanes = slice(j * LANES, (j + 1) * LANES)
                        a_ref[hh * (half // LANES) + j, rows, :] = a[:, lanes]
                        u_ref[hh * (half // LANES) + j, rows, :] = u[:, lanes]
            return carry
        return body

    _tile_loop(n // a_tile, phase_a1, unroll=2)
    if late_gate:
        @pl.when(first_step)
        def _():
            for copy in late_gate:
                copy.wait()

    for q in seqs:
        _tile_loop(n // a_tile, phase_b(q))

    row8 = lax.broadcasted_iota(jnp.int32, (SUBLANES, LANES), 0)
    block = SUBLANES * SUBLANES
    n_blocks = n // block
    n_slabs = LRU_W // LANES

    def across_groups(a, b, reverse):
        for s in (1, 2, 4):
            m = (row8 < SUBLANES - s) if reverse else (row8 >= s)
            shift = SUBLANES - s if reverse else s
            a_s = jnp.where(m, pltpu.roll(a, shift, axis=0), 1.0)
            b_s = jnp.where(m, pltpu.roll(b, shift, axis=0), 0.0)
            b = a * b_s + b
            a = a * a_s
        return a, b

    def scan_block(a_ref, h_ref, base, carry, reverse):
        rows = [pl.ds(base + k, SUBLANES, stride=SUBLANES) for k in range(SUBLANES)]
        order = list(reversed(range(SUBLANES))) if reverse else list(range(SUBLANES))
        prod, local = {}, {}
        prev = None
        for k in order:
            a, u = a_ref[rows[k], :], h_ref[rows[k], :]
            prod[k] = a if prev is None else a * prod[prev]
            local[k] = u if prev is None else a * local[prev] + u
            prev = k
        p_all, h_all = across_groups(prod[prev], local[prev], reverse)
        inner = (row8 < SUBLANES - 1) if reverse else (row8 >= 1)
        shift = SUBLANES - 1 if reverse else 1
        enter = (jnp.where(inner, pltpu.roll(p_all, shift, axis=0), 1.0) * carry
                 + jnp.where(inner, pltpu.roll(h_all, shift, axis=0), 0.0))
        for k in order:
            h_ref[rows[k], :] = prod[k] * enter + local[k]
        leave = p_all * carry + h_all
        return leave[0:1, :] if reverse else leave[SUBLANES - 1:SUBLANES, :]

    def initial(q, d, s):
        if latent:
            return q.h0[d:d + 1, s * LANES:(s + 1) * LANES]
        return jnp.zeros((1, LANES), _F32)

    chains = [(q, s) for q in seqs for s in range(n_slabs)]

    def scan_body(i, carry):
        fwd_base = pl.multiple_of(i * block, block)
        bwd_base = pl.multiple_of((n_blocks - 1 - i) * block, block)
        out = []
        for (q, s), (cf, cb) in zip(chains, carry):
            out.append((scan_block(q.a_f.at[s], q.h_f.at[s], fwd_base, cf, False),
                        scan_block(q.a_b.at[s], q.h_b.at[s], bwd_base, cb, True)))
        return tuple(out)

    ends = lax.fori_loop(0, n_blocks, scan_body,
                         tuple((initial(q, 0, s), initial(q, 1, s)) for q, s in chains), unroll=2)

    if not latent:
        @pl.when(first_step)
        def _():
            late_out.wait()

    def lru_out(q):
        def body(r, carry):
            rows = pl.ds(pl.multiple_of(r * ROW_TILE, ROW_TILE), ROW_TILE)
            for s in range(n_slabs):
                lanes = slice(s * LANES, (s + 1) * LANES)
                q.mix[rows, lanes] = (q.h_f[s, rows, :] + q.h_b[s, rows, :]) * q.gy[rows, lanes]
            return carry
        return body

    for (q, s), (last_f, first_b) in zip(chains, ends):
        if not latent:
            q.stl[0:1, s * LANES:(s + 1) * LANES] = last_f
            q.stl[1:2, s * LANES:(s + 1) * LANES] = first_b
    for q in seqs:
        _tile_loop(nt, lru_out(q))

    def phase_a2(r, carry):
        r0 = pl.multiple_of(r * wide, wide)
        rows = pl.ds(r0, wide)
        h = stacked([normed(q, rows) for q in seqs])
        pqk_all = _dot(h, wqk_ref[...])
        pvg_all = _dot(h, wvg_ref[...])
        for s, q in enumerate(seqs):
            part = slice(s * wide, (s + 1) * wide)
            store_retention(q, r0, rows, pqk_all[part], pvg_all[part])
        return carry

    if not early_ret:
        @pl.when(first_step)
        def _():
            for copy in late_ret:
                copy.wait()

        _tile_loop(n // wide, phase_a2)

    lower = (lax.broadcasted_iota(jnp.int32, (ROW_TILE, ROW_TILE), 0)
             >= lax.broadcasted_iota(jnp.int32, (ROW_TILE, ROW_TILE), 1))
    blocks = [slice(r * ROW_TILE, (r + 1) * ROW_TILE) for r in range(nt)]

    def phase_c(q, hd):
        cols = slice(hd * DK, (hd + 1) * DK)
        dec = lambda row: decay[hd][row]
        if nt > 1:
            kv = [_dot_tn(jnp.concatenate([q.k_f[hd, rows, :], q.k_b[hd, rows, :]], axis=1),
                          q.v[rows, cols]) for rows in blocks]
            kv_f = [m[0:DK] for m in kv]
            kv_b = [m[DK:2 * DK] for m in kv]
        else:
            kv_f = [_dot_tn(q.k_f[hd, rows, :], q.v[rows, cols]) for rows in blocks]
            kv_b = [_dot_tn(q.k_b[hd, rows, :], q.v[rows, cols]) for rows in blocks]
        if latent:
            run_f = q.s0[0, hd] * dec(DEC_G_F)
            run_b = q.s0[1, hd] * dec(DEC_GN_B)
        else:
            run_f = run_b = None
        before = []
        for r in range(nt):
            before.append(run_f)
            run_f = kv_f[r] if run_f is None else run_f + kv_f[r]
        after = [None] * nt
        for r in reversed(range(nt)):
            after[r] = run_b
            run_b = kv_b[r] if run_b is None else run_b + kv_b[r]

        for r, rows in enumerate(blocks):
            qf = q.q_f[hd, rows, :]
            qb = q.q_b[hd, rows, :]
            s = jnp.where(lower, _dot_nt(qf, q.k_f[hd, rows, :]), _dot_nt(qb, q.k_b[hd, rows, :]))
            o = _dot(s, q.v[rows, cols])
            if before[r] is not None and after[r] is not None:
                o = o + _dot(jnp.concatenate([qf, qb], axis=1),
                             jnp.concatenate([before[r], after[r]], axis=0))
            elif before[r] is not None:
                o = o + _dot(qf, before[r])
            elif after[r] is not None:
                o = o + _dot(qb, after[r])
            o = o * lax.rsqrt(jnp.mean(o * o, axis=-1, keepdims=True) + EPS)
            q.mix[rows, LRU_W + cols.start:LRU_W + cols.stop] = o * q.sg[rows, cols]
        if not latent:
            q.str[0, hd] = run_f * dec(DEC_GN1_F)
            q.str[1, hd] = run_b

    for hd in range(RET_H):
        for q in seqs:
            phase_c(q, hd)

    gain1 = mod(2) * n2_ref[...]
    gain2 = n3_ref[...] * (1.0 + mod(4))
    sh2 = mod(3)

    def phase_d(r, carry):
        rows = pl.ds(pl.multiple_of(r * wide, wide), wide)
        mix_all = _dot(stacked([q.mix[rows, :] for q in seqs]), wout_ref[...])
        h2s = []
        for s, q in enumerate(seqs):
            x1 = q.x[rows, :] + _rms(mix_all[s * wide:(s + 1) * wide], gain1)
            q.x1[rows, :] = x1
            h2s.append(_rms(x1, gain2) + sh2)
            q.h2[rows, :] = h2s[-1].astype(_BF)
        logits_t = _dot(stacked(h2s), rw_ref[...]).T
        for s, q in enumerate(seqs):
            q.lt[:, rows] = logits_t[0:N_EXP, s * wide:(s + 1) * wide]
        return carry

    if latent:
        @pl.when(first_step)
        def _():
            late_out.wait()

    _tile_loop(n // wide, phase_d, unroll=2)


def _mixer_call(x, mod, norms, w_in, lru, wg, w_out, rw, latent, extra=(), layer=0):
    b, n, _ = x.shape
    g = 1 if latent else CONTEXT_GROUP
    const2 = lambda i: (0, 0)
    whole = pl.BlockSpec(memory_space=pl.ANY)
    in_specs = [
        pl.BlockSpec((g, n, D), lambda i: (i, 0, 0)),
        pl.BlockSpec((N_MOD, SUBLANES, D), lambda i: (0, 0, 0)),
        pl.BlockSpec((1, D), const2), pl.BlockSpec((1, D), const2), pl.BlockSpec((1, D), const2),
        pl.BlockSpec((D, D), lambda i: (0, 0)),
    ]
    if latent:
        in_specs += [whole]
    else:
        in_specs += [pl.BlockSpec((D, D), lambda i: (0, 1)),
                     pl.BlockSpec((D, D), lambda i: (0, 2))]
    in_specs += [
        pl.BlockSpec((4, LRU_W), const2), pl.BlockSpec((1, LRU_W), const2),
        pl.BlockSpec((2, LRU_W), const2), pl.BlockSpec((2, LRU_W), const2),
        pl.BlockSpec((2, LRU_W), const2),
        whole if latent else pl.BlockSpec((2, 2, LRU_W // 2, LRU_W), lambda i: (0, 0, 0, 0)),
        whole,
        pl.BlockSpec((D, LANES), const2),
    ]
    out_shape = [
        jax.ShapeDtypeStruct((b, n, D), _F32),
        jax.ShapeDtypeStruct((b, n, D), _BF),
        jax.ShapeDtypeStruct((b, N_EXP, n), _F32),
    ]
    out_specs = [
        pl.BlockSpec((g, n, D), lambda i: (i, 0, 0)),
        pl.BlockSpec((g, n, D), lambda i: (i, 0, 0)),
        pl.BlockSpec((g, N_EXP, n), lambda i: (i, 0, 0)),
    ]
    if latent:
        in_specs += [
            pl.BlockSpec((g, None, 2, LRU_W), lambda i: (i, layer, 0, 0)),
            pl.BlockSpec((g, None, 2, RET_H, DK, DK), lambda i: (i, layer, 0, 0, 0, 0)),
            whole, whole,
        ]
    else:
        out_shape += [
            jax.ShapeDtypeStruct((b, 1, 2, LRU_W), _F32),
            jax.ShapeDtypeStruct((b, 1, 2, RET_H, DK, DK), _F32),
        ]
        out_specs += [
            pl.BlockSpec((g, None, 2, LRU_W), lambda i: (i, 0, 0, 0)),
            pl.BlockSpec((g, None, 2, RET_H, DK, DK), lambda i: (i, 0, 0, 0, 0, 0)),
        ]
    f32s = lambda shape: pltpu.VMEM((g,) + shape, _F32)
    slabs = (LRU_W // LANES, n, LANES)
    scratch = [
        f32s((n + 2 * SUBLANES, LRU_W)),
        f32s((n, LRU_W)),
        f32s(slabs), f32s(slabs),
        f32s(slabs), f32s(slabs),
    ]
    if latent:
        scratch += [pltpu.VMEM((D, 2 * D), _F32), pltpu.VMEM(wg.shape, _F32),
                    pltpu.VMEM((n, DK), _F32), pltpu.VMEM((n, DK), _F32)]
        w_in_args = (w_in, w_in)
    else:
        scratch += [f32s((n, RET_W))] + [f32s(slabs) for _ in range(4)] + [f32s((n, RET_W))]
        w_in_args = (w_in, w_in, w_in)
    scratch += [pltpu.VMEM((D, D), _F32), pltpu.SemaphoreType.DMA((5 if latent else 1,))]
    return pl.pallas_call(
        functools.partial(_mixer_kernel, n=n, g=g, latent=latent),
        out_shape=out_shape,
        grid=(b // g,),
        in_specs=in_specs,
        out_specs=out_specs,
        scratch_shapes=scratch,
        compiler_params=pltpu.CompilerParams(
            dimension_semantics=("arbitrary",), vmem_limit_bytes=VMEM_LIMIT),
        name="mixer_latent" if latent else "mixer_context",
    )(x, mod, *norms, *w_in_args, *lru, wg, w_out, rw, *extra)


def _count(mask):
    return jnp.sum(jnp.where(mask, 1.0, 0.0), axis=-1, keepdims=True)


def _probs(l3):
    bsz, _, n = l3.shape
    m = jnp.max(l3, axis=1, keepdims=True)
    e = jnp.exp(l3 - m)
    return (e / jnp.sum(e, axis=1, keepdims=True)).reshape(bsz * N_EXP, n)


def _break_ties(parts):
    prep = []
    for bits, thr, cap in parts:
        rows, n = bits.shape
        eq = bits == thr
        need = float(cap) - _count(bits > thr)
        idx = lax.broadcasted_iota(jnp.int32, (rows, n), 1)
        prep.append((eq, need, idx, int(math.log2(n))))
    most = max(nbits for _, _, _, nbits in prep)

    def idx_body(i, last):
        out = []
        for (eq, need, idx, nbits), j in zip(prep, last):
            shift = nbits - 1 - i
            cand = j | jnp.where(shift >= 0, jnp.int32(1) << jnp.maximum(shift, 0), 0)
            out.append(jnp.where(_count(eq & (idx < cand)) < need, cand, j))
        return tuple(out)

    return lax.fori_loop(0, most, idx_body,
                         tuple(jnp.zeros((eq.shape[0], 1), jnp.int32) for eq, _, _, _ in prep))


def _slots(p, bits, thr, jlast):
    rows, n = p.shape
    idx = lax.broadcasted_iota(jnp.int32, (rows, n), 1)
    sel = (bits > thr) | ((bits == thr) & (idx <= jlast))
    before = (lax.broadcasted_iota(jnp.int32, (n, n), 0)
              < lax.broadcasted_iota(jnp.int32, (n, n), 1))
    pos = _dot(jnp.where(sel, 1.0, 0.0).astype(_BF), jnp.where(before, 1.0, 0.0).astype(_BF))
    return jnp.where(sel, pos, -1.0), jnp.where(sel, p, 0.0)


def _route_kernel(lp_ref, ls_ref, pp_ref, gp_ref, ps_ref, gs_ref, *, cap_p, cap_s):
    groups = ((_probs(lp_ref[...]), float(cap_p)), (_probs(ls_ref[...]), float(cap_s)))
    bits = [pltpu.bitcast(p, jnp.int32) for p, _ in groups]

    def settle(b, t, capf, hi, lo):
        with_hi = t | hi
        both = with_hi | lo
        with_lo = t | lo
        ok = lambda cand: _count(b >= cand) >= capf
        return jnp.where(ok(with_hi), jnp.where(ok(both), both, with_hi),
                         jnp.where(ok(with_lo), with_lo, t))

    def val_body(i, thr):
        hi = jnp.int32(1) << (29 - 2 * i)
        lo = jnp.int32(1) << (28 - 2 * i)
        return tuple(settle(b, t, capf, hi, lo) for b, t, (_, capf) in zip(bits, thr, groups))

    thr = lax.fori_loop(0, 15, val_body,
                        tuple(jnp.zeros((b.shape[0], 1), jnp.int32) for b in bits))
    jlast = _break_ties([(bits[0], thr[0], cap_p), (bits[1], thr[1], cap_s)])
    pos, gate = _slots(groups[0][0], bits[0], thr[0], jlast[0])
    pp_ref[...] = pos.reshape(pp_ref.shape)
    gp_ref[...] = gate.reshape(gp_ref.shape)
    pos, gate = _slots(groups[1][0], bits[1], thr[1], jlast[1])
    ps_ref[...] = pos.reshape(ps_ref.shape)
    gs_ref[...] = gate.reshape(gs_ref.shape)


def _route_call(lt_p, lt_s, cap_p, cap_s):
    shapes = [
        jax.ShapeDtypeStruct(lt_p.shape, _F32), jax.ShapeDtypeStruct(lt_p.shape, _F32),
        jax.ShapeDtypeStruct(lt_s.shape, _F32), jax.ShapeDtypeStruct(lt_s.shape, _F32),
    ]
    return pl.pallas_call(
        functools.partial(_route_kernel, cap_p=cap_p, cap_s=cap_s),
        out_shape=shapes,
        compiler_params=pltpu.CompilerParams(vmem_limit_bytes=VMEM_LIMIT),
        name="route_select",
    )(lt_p, lt_s)


def _dispatch_kernel(pos_ref, gate_ref, h_ref, xs_ref, gs_ref, *, n, cap, g):
    slot = lax.broadcasted_iota(jnp.int32, (cap, n), 0).astype(_F32)
    for j in range(g):
        slots = slice(j * cap, (j + 1) * cap)
        parts = []
        for e in range(N_EXP):
            hit = pos_ref[j, e:e + 1, :] == slot
            parts.append(jnp.where(hit, 1.0, 0.0).astype(_BF))
            gs_ref[e, slots, :] = jnp.sum(jnp.where(hit, gate_ref[j, e:e + 1, :], 0.0),
                                          axis=-1, keepdims=True)
        onehot = jnp.concatenate(parts, axis=0)
        xs = _dot(onehot, h_ref[j]).astype(_BF)
        for e in range(N_EXP):
            xs_ref[e, slots, :] = xs[e * cap:(e + 1) * cap, :]


def _dispatch_both_kernel(pp_ref, gp_ref, hp_ref, ps_ref, gq_ref, hs_ref, xs_ref, gs_ref,
                          *, steps_p, n_p, cap_p, g_p, n_s, cap_s, g_s):
    step = pl.program_id(0)

    @pl.when(step < steps_p)
    def _():
        _dispatch_kernel(pp_ref, gp_ref, hp_ref, xs_ref, gs_ref, n=n_p, cap=cap_p, g=g_p)

    @pl.when(step >= steps_p)
    def _():
        _dispatch_kernel(ps_ref, gq_ref, hs_ref, xs_ref, gs_ref, n=n_s, cap=cap_s, g=g_s)


def _dispatch_call(pos_p, gate_p, h2_p, cap_p, pos_s, gate_s, h2_s, cap_s):
    bp, n_p, _ = h2_p.shape
    bs, n_s, _ = h2_s.shape
    g_p = max(1, SMALL_STEP_TOKENS // n_p)
    g_s = max(1, SMALL_STEP_TOKENS // n_s)
    slots = g_p * cap_p
    assert slots == g_s * cap_s
    steps_p, steps_s = bp // g_p, bs // g_s
    first = lambda i: (jnp.minimum(i, steps_p - 1), 0, 0)
    second = lambda i: (jnp.maximum(i - steps_p, 0), 0, 0)
    total = bp * cap_p + bs * cap_s
    return pl.pallas_call(
        functools.partial(_dispatch_both_kernel, steps_p=steps_p, n_p=n_p, cap_p=cap_p, g_p=g_p,
                          n_s=n_s, cap_s=cap_s, g_s=g_s),
        out_shape=[
            jax.ShapeDtypeStruct((N_EXP, total, D), _BF),
            jax.ShapeDtypeStruct((N_EXP, total, 1), _F32),
        ],
        grid=(steps_p + steps_s,),
        in_specs=[
            pl.BlockSpec((g_p, N_EXP, n_p), first),
            pl.BlockSpec((g_p, N_EXP, n_p), first),
            pl.BlockSpec((g_p, n_p, D), first),
            pl.BlockSpec((g_s, N_EXP, n_s), second),
            pl.BlockSpec((g_s, N_EXP, n_s), second),
            pl.BlockSpec((g_s, n_s, D), second),
        ],
        out_specs=[
            pl.BlockSpec((N_EXP, slots, D), lambda i: (0, i, 0)),
            pl.BlockSpec((N_EXP, slots, 1), lambda i: (0, i, 0)),
        ],
        compiler_params=pltpu.CompilerParams(
            dimension_semantics=("arbitrary",), vmem_limit_bytes=VMEM_LIMIT),
        name="dispatch",
    )(pos_p, gate_p, h2_p, pos_s, gate_s, h2_s)


def _expert_kernel(x_ref, g_ref, wg_ref, wu_ref, wd_ref, y_ref, acc, *, nf, tf):
    f = pl.program_id(1)
    x = x_ref[...]
    total = jnp.where(f == 0, 0.0, acc[...])
    for c in range(tf // FF_CHUNK):
        cs = slice(c * FF_CHUNK, (c + 1) * FF_CHUNK)
        hg = _dot(x, wg_ref[:, cs].astype(_BF))
        hu = _dot(x, wu_ref[:, cs].astype(_BF))
        hid = (_silu(hg) * hu).astype(_BF)
        total = total + _dot(hid, wd_ref[cs, :].astype(_BF))
    acc[...] = total
    y_ref[...] = (total * g_ref[...]).astype(_BF)


def _expert_call(xs, gs, w_gate, w_up, w_down):
    tf = 1024
    slots = xs.shape[1]
    nf = FF // tf
    return pl.pallas_call(
        functools.partial(_expert_kernel, nf=nf, tf=tf),
        out_shape=jax.ShapeDtypeStruct((N_EXP, slots, D), _BF),
        grid=(N_EXP, nf),
        in_specs=[
            pl.BlockSpec((None, slots, D), lambda e, f: (e, 0, 0)),
            pl.BlockSpec((None, slots, 1), lambda e, f: (e, 0, 0)),
            pl.BlockSpec((None, D, tf), lambda e, f: (e, 0, f)),
            pl.BlockSpec((None, D, tf), lambda e, f: (e, 0, f)),
            pl.BlockSpec((None, tf, D), lambda e, f: (e, f, 0)),
        ],
        out_specs=pl.BlockSpec((None, slots, D), lambda e, f: (e, 0, 0)),
        scratch_shapes=[pltpu.VMEM((slots, D), _F32)],
        compiler_params=pltpu.CompilerParams(
            dimension_semantics=("arbitrary", "arbitrary"), vmem_limit_bytes=VMEM_LIMIT),
        name="expert_ffn",
    )(xs, gs, w_gate, w_up, w_down)


def _combine_kernel(pos_ref, ye_ref, x1_ref, mod_ref, n4_ref, y_ref, *, n, cap, g, latent):
    width = N_EXP * cap
    lane = lax.broadcasted_iota(jnp.int32, (N_EXP, width), 1)
    expand = jnp.where(lane // cap == lax.broadcasted_iota(jnp.int32, (N_EXP, width), 0),
                       1.0, 0.0).astype(_BF)
    tile = min(n, COMBINE_TILE)
    slot = (lax.broadcasted_iota(jnp.int32, (tile, width), 1) % cap).astype(_F32)
    mod_row = (pl.program_id(0) + 1) if latent else 0
    gain = mod_ref[N_MOD - 1, pl.ds(mod_row, 1), :] * n4_ref[...]

    for j in range(g):
        ye = ye_ref[:, j * cap:(j + 1) * cap, :].reshape(width, D)

        def body(r, carry, j=j, ye=ye):
            rows = pl.ds(pl.multiple_of(r * tile, tile), tile)
            pos_e = _dot_tn(pos_ref[j, :, rows].astype(_BF), expand)
            onehot = jnp.where(pos_e == slot, 1.0, 0.0).astype(_BF)
            f = _dot(onehot, ye)
            y_ref[j, rows, :] = x1_ref[j, rows, :] + _rms(f, gain)
            return carry

        _tile_loop(n // tile, body)


def _combine_call(pos, ye, x1, mod, norm_post, cap, slot_off, latent):
    b, n, _ = x1.shape
    g = max(1, SMALL_STEP_TOKENS // n)
    assert g == 1 or not latent
    blk_off = slot_off // (g * cap)
    return pl.pallas_call(
        functools.partial(_combine_kernel, n=n, cap=cap, g=g, latent=latent),
        out_shape=jax.ShapeDtypeStruct((b, n, D), _F32),
        grid=(b // g,),
        in_specs=[
            pl.BlockSpec((g, N_EXP, n), lambda i: (i, 0, 0)),
            pl.BlockSpec((N_EXP, g * cap, D), lambda i: (0, i + blk_off, 0)),
            pl.BlockSpec((g, n, D), lambda i: (i, 0, 0)),
            pl.BlockSpec((N_MOD, SUBLANES, D), lambda i: (0, 0, 0)),
            pl.BlockSpec((1, D), lambda i: (0, 0)),
        ],
        out_specs=pl.BlockSpec((g, n, D), lambda i: (i, 0, 0)),
        compiler_params=pltpu.CompilerParams(
            dimension_semantics=("arbitrary",), vmem_limit_bytes=VMEM_LIMIT),
        name="combine_latent" if latent else "combine_context",
    )(pos, ye, x1, mod, norm_post)


def _block_diag_gates(wa, wi):
    per_half = LRU_HEADS // 2
    side = per_half * LRU_HD
    on_diag = (np.arange(side)[:, None] // LRU_HD) == (np.arange(side)[None, :] // LRU_HD)

    def bd(w):
        rows = w.reshape(2, 2, side, LRU_HD)
        return jnp.where(on_diag, jnp.tile(rows, (1, 1, 1, per_half)), 0.0)

    return 0.5 * jnp.concatenate([bd(wa), bd(wi)], axis=-1)


def _rope_tables(n):
    rows = n // GRID_W
    row = np.repeat(np.arange(rows, dtype=np.float32), GRID_W)
    col = np.tile(np.arange(GRID_W, dtype=np.float32), rows)
    nf = DK // 4
    freqs = np.float32(ROPE_BASE) ** (-np.arange(nf, dtype=np.float32) / np.float32(nf))
    ang = np.concatenate([row[:, None] * freqs, col[:, None] * freqs], axis=-1).astype(np.float32)
    cos = np.cos(ang)
    sin = np.sin(ang)
    return (jnp.asarray(np.concatenate([cos, cos], axis=-1), _F32),
            jnp.asarray(np.concatenate([-sin, sin], axis=-1), _F32))


def _decay_consts(n):
    heads = np.arange(RET_H, dtype=np.float32)
    f32 = np.float32
    lgf = np.log1p(-np.exp2(-(f32(RET_DECAY_OFFSET_FWD) + heads))).astype(f32)
    lgb = np.log1p(-np.exp2(-(f32(RET_DECAY_OFFSET_BWD) + heads))).astype(f32)
    tab = np.stack([lgf, lgb, np.exp(lgf), np.exp(f32(n) * lgb), np.exp(f32(n - 1) * lgf)], axis=1)
    return [[float(v) for v in row] for row in tab.astype(f32)]


def kernel(x_prompt, x_sample, c, state_lru, state_ret, c_ctx, ada_w, ada_b, norm_mix_pre, norm_mix_post, norm_ffn_pre, norm_ffn_post, w_in, conv_w, conv_b, lru_wa, lru_ba, lru_wi, lru_bi, lru_lambda, w_out, router_w, exp_w_gate, exp_w_up, exp_w_down):
    bp, n_p, _ = x_prompt.shape
    bs, n_s, _ = x_sample.shape
    cap_p = 2 * n_p // N_EXP
    cap_s = 2 * n_s // N_EXP
    l = 0

    mod = _ada_call(c_ctx[None, :], c, ada_w[l], ada_b[l][None, :])

    norms = (norm_mix_pre[l][None], norm_mix_post[l][None], norm_ffn_pre[l][None])
    lru = (conv_w[l], conv_b[l][None], lru_ba[l], lru_bi[l], lru_lambda[l])
    wg = _block_diag_gates(lru_wa[l], lru_wi[l])
    rw = jnp.pad(router_w[l], ((0, 0), (0, LANES - N_EXP)))
    cos2, sin2 = _rope_tables(n_s)

    x1_p, h2_p, lt_p, st_lru, st_ret = _mixer_call(
        x_prompt, mod, norms, w_in[l], lru, wg, w_out[l], rw, latent=False)
    x1_s, h2_s, lt_s = _mixer_call(
        x_sample, mod, norms, w_in[l], lru, wg, w_out[l], rw, latent=True,
        extra=(state_lru, state_ret, cos2, sin2), layer=l)

    pos_p, gate_p, pos_s, gate_s = _route_call(lt_p, lt_s, cap_p, cap_s)
    xs, gsl = _dispatch_call(pos_p, gate_p, h2_p, cap_p, pos_s, gate_s, h2_s, cap_s)
    ye = _expert_call(xs, gsl, exp_w_gate[l], exp_w_up[l], exp_w_down[l])

    norm_post = norm_ffn_post[l][None]
    y_p = _combine_call(pos_p, ye, x1_p, mod, norm_post, cap_p, 0, latent=False)
    y_s = _combine_call(pos_s, ye, x1_s, mod, norm_post, cap_s, bp * cap_p, latent=True)
    return (y_p, y_s, st_lru, st_ret)
```

```python
import functools
import math
import types

import jax
import jax.numpy as jnp
import numpy as np
from jax import lax
from jax.experimental import pallas as pl
from jax.experimental.pallas import tpu as pltpu

D = 1024
LRU_W = 512
LRU_HEADS = 8
LRU_HD = 64
LRU_C = 8.0
RET_W = 512
RET_H = 4
DK = 128
N_EXP = 16
FF = 2048
N_MOD = 6
EPS = 1e-6
GRID_W = 64
ROPE_BASE = 10000.0
RET_DECAY_OFFSET_FWD = 5.0
RET_DECAY_OFFSET_BWD = 5.5

ROW_TILE = 256
WIDE_TILE = 512
FF_CHUNK = 512
COMBINE_TILE = 1024
CONTEXT_GROUP = 2
SMALL_STEP_TOKENS = 1024
SUBLANES = 8
LANES = 128
VMEM_LIMIT = 60 * 1024 * 1024

DEC_LOG_F, DEC_LOG_B, DEC_G_F, DEC_GN_B, DEC_GN1_F = 0, 1, 2, 3, 4

_BF = jnp.bfloat16
_F32 = jnp.float32


def _sigmoid(x):
    return 0.5 * jnp.tanh(0.5 * x) + 0.5


def _silu(x):
    return x * _sigmoid(x)


def _gelu_tanh(x):
    c = math.sqrt(2.0 / math.pi)
    return 0.5 * x * (1.0 + jnp.tanh(c * (x + 0.044715 * (x * x * x))))


def _rms(x, gain):
    return x * lax.rsqrt(jnp.mean(x * x, axis=-1, keepdims=True) + EPS) * gain


def _dot(a, b):
    return jnp.dot(a, b, preferred_element_type=_F32)


def _dot_nt(a, b):
    return lax.dot_general(a, b, (((1,), (1,)), ((), ())), preferred_element_type=_F32)


def _dot_tn(a, b):
    return lax.dot_general(a, b, (((0,), (0,)), ((), ())), preferred_element_type=_F32)


def _ada_kernel(cc_ref, c_ref, w_ref, b_ref, o_ref, s_ref):
    nb = c_ref.shape[0]
    s_ref[...] = jnp.zeros_like(s_ref)
    s_ref[0:1, :] = _silu(cc_ref[...])
    s_ref[1:1 + nb, :] = _silu(c_ref[...])
    o_ref[...] = _dot(s_ref[...], w_ref[...]) + b_ref[...]


def _ada_call(c_ctx, c, ada_w, ada_b):
    nb = c.shape[0]
    assert nb + 1 <= SUBLANES
    return pl.pallas_call(
        _ada_kernel,
        out_shape=jax.ShapeDtypeStruct((N_MOD, SUBLANES, D), _F32),
        grid=(N_MOD,),
        in_specs=[
            pl.BlockSpec((1, D), lambda j: (0, 0)),
            pl.BlockSpec((nb, D), lambda j: (0, 0)),
            pl.BlockSpec((D, D), lambda j: (0, j)),
            pl.BlockSpec((1, D), lambda j: (0, j)),
        ],
        out_specs=pl.BlockSpec((None, SUBLANES, D), lambda j: (j, 0, 0)),
        scratch_shapes=[pltpu.VMEM((SUBLANES, D), _F32)],
        compiler_params=pltpu.CompilerParams(
            dimension_semantics=("arbitrary",), vmem_limit_bytes=VMEM_LIMIT),
        name="ada_mod",
    )(c_ctx, c, ada_w, ada_b)


def _tile_loop(nt, body, unroll=1):
    if nt == 1:
        body(0, 0)
    else:
        lax.fori_loop(0, nt, body, 0, unroll=unroll)


def _mixer_kernel(*refs, n, g, latent):
    if latent:
        (x_ref, mod_ref, n1_ref, n2_ref, n3_ref, wl_ref, win_hbm,
         cw_ref, cb_ref, ba_ref, bi_ref, lam_ref, wg_hbm, wout_hbm, rw_ref,
         h0_ref, s0_ref, cos_hbm, sin_hbm,
         x1_ref, h2_ref, lt_ref,
         xlp_g, gy_g, af_g, ab_g, hf_g, hb_g, wret_v, wg_ref, cos_ref, sin_ref,
         wout_ref, late_sem) = refs
        sg_g, qf_g, qb_g, kf_g, kb_g, v_g = xlp_g, af_g, ab_g, hf_g, hb_g, gy_g
        wqk_ref, wvg_ref = wret_v.at[:, 0:D], wret_v.at[:, D:2 * D]
    else:
        (x_ref, mod_ref, n1_ref, n2_ref, n3_ref, wl_ref, wqk_ref, wvg_ref,
         cw_ref, cb_ref, ba_ref, bi_ref, lam_ref, wg_ref, wout_hbm, rw_ref,
         x1_ref, h2_ref, lt_ref, stl_ref, str_ref,
         xlp_g, gy_g, af_g, ab_g, hf_g, hb_g, sg_g, qf_g, qb_g, kf_g, kb_g, v_g,
         wout_ref, late_sem) = refs
    early_ret = not latent

    first_step = pl.program_id(0) == 0
    late_out = pltpu.make_async_copy(wout_hbm, wout_ref, late_sem.at[0])
    late_gate, late_ret = [], []
    if latent:
        late_gate = [pltpu.make_async_copy(wg_hbm, wg_ref, late_sem.at[1])]
        late_ret = [
            pltpu.make_async_copy(win_hbm.at[:, pl.ds(D, 2 * D)], wret_v, late_sem.at[2]),
            pltpu.make_async_copy(cos_hbm, cos_ref, late_sem.at[3]),
            pltpu.make_async_copy(sin_hbm, sin_ref, late_sem.at[4]),
        ]

    @pl.when(first_step)
    def _():
        for copy in late_gate + late_ret + [late_out]:
            copy.start()

    seqs = []
    for s in range(g):
        q = types.SimpleNamespace(
            x=x_ref.at[s], x1=x1_ref.at[s], h2=h2_ref.at[s], lt=lt_ref.at[s], mix=x1_ref.at[s],
            xlp=xlp_g.at[s], sg=sg_g.at[s], gy=gy_g.at[s], v=v_g.at[s],
            a_f=af_g.at[s], a_b=ab_g.at[s], h_f=hf_g.at[s], h_b=hb_g.at[s],
            q_f=qf_g.at[s], q_b=qb_g.at[s], k_f=kf_g.at[s], k_b=kb_g.at[s])
        if latent:
            q.h0, q.s0 = h0_ref.at[s], s0_ref.at[s]
        else:
            q.stl, q.str = stl_ref.at[s], str_ref.at[s]
        seqs.append(q)

    nt = n // ROW_TILE
    decay = _decay_consts(n)
    mod_row = (pl.program_id(0) + 1) if latent else 0
    mod = lambda k: mod_ref[k, pl.ds(mod_row, 1), :]
    shift = mod(0)
    scale = n1_ref[...] * (1.0 + mod(1))

    def normed(q, rows):
        return _rms(q.x[rows, :], scale) + shift

    wide = min(n, WIDE_TILE)
    a_tile = wide if latent else ROW_TILE

    def stacked(parts):
        return parts[0] if len(parts) == 1 else jnp.concatenate(parts, axis=0)

    def store_retention(q, r0, rows, pqk, pvg):
        q.sg[rows, :] = _silu(pvg[:, RET_W:])
        tpos = (r0 + lax.broadcasted_iota(jnp.int32, (pqk.shape[0], DK), 0)).astype(_F32)
        if latent:
            cos2 = cos_ref[rows, :]
            sin2 = sin_ref[rows, :]
        for hd in range(RET_H):
            cols = slice(hd * DK, (hd + 1) * DK)
            qh = pqk[:, cols] * (DK ** -0.5)
            kh = pqk[:, RET_W + hd * DK:RET_W + (hd + 1) * DK]
            if latent:
                qh = qh * cos2 + pltpu.roll(qh, DK // 2, axis=1) * sin2
                kh = kh * cos2 + pltpu.roll(kh, DK // 2, axis=1) * sin2
            lgf = decay[hd][DEC_LOG_F]
            lgb = decay[hd][DEC_LOG_B]
            q.q_f[hd, rows, :] = qh * jnp.exp(tpos * lgf)
            q.k_f[hd, rows, :] = kh * jnp.exp(tpos * (-lgf))
            q.q_b[hd, rows, :] = qh * jnp.exp(tpos * (-lgb))
            q.k_b[hd, rows, :] = kh * jnp.exp(tpos * lgb)
        q.v[rows, :] = pvg[:, 0:RET_W]

    def phase_a1(r, carry):
        r0 = pl.multiple_of(r * a_tile, a_tile)
        rows = pl.ds(r0, a_tile)
        h = stacked([normed(q, rows) for q in seqs])
        p_all = _dot(h, wl_ref[...])
        if early_ret:
            pqk_all = _dot(h, wqk_ref[...])
            pvg_all = _dot(h, wvg_ref[...])
        for s, q in enumerate(seqs):
            part = slice(s * a_tile, (s + 1) * a_tile)
            q.xlp[pl.ds(r0 + SUBLANES, a_tile), :] = p_all[part, 0:LRU_W]
            q.gy[rows, :] = _gelu_tanh(p_all[part, LRU_W:])
            if early_ret:
                store_retention(q, r0, rows, pqk_all[part], pvg_all[part])
        return carry

    for q in seqs:
        q.xlp[0:SUBLANES, :] = jnp.zeros((SUBLANES, LRU_W), _F32)
        q.xlp[n + SUBLANES:n + 2 * SUBLANES, :] = jnp.zeros((SUBLANES, LRU_W), _F32)

    half = LRU_W // 2

    def softplus_neg(lam):
        z = -lam
        return jnp.maximum(z, 0.0) + jnp.log1p(jnp.exp(-jnp.abs(z)))

    sp = (softplus_neg(lam_ref[0:1, :]), softplus_neg(lam_ref[1:2, :]))

    def phase_b(q):
        def body(r, carry):
            r0 = pl.multiple_of(r * a_tile, a_tile)
            rows = pl.ds(r0, a_tile)
            ext = q.xlp[pl.ds(r0, a_tile + 2 * SUBLANES), :]
            xc = cb_ref[...]
            for tap in range(4):
                back = (2 - tap) % (a_tile + 2 * SUBLANES)
                win = ext if back == 0 else pltpu.roll(ext, back, axis=0)
                xc = xc + win[SUBLANES:SUBLANES + a_tile, :] * cw_ref[tap:tap + 1, :]
            xh = 0.5 * xc
            for d, (a_ref, u_ref) in enumerate(((q.a_f, q.h_f), (q.a_b, q.h_b))):
                bah = 0.5 * ba_ref[d:d + 1, :]
                bih = 0.5 * bi_ref[d:d + 1, :]
                ch = (-0.5 * LRU_C) * sp[d]
                for hh in range(2):
                    cs = slice(hh * half, (hh + 1) * half)
                    pre = _dot(xc[:, cs], wg_ref[d, hh])
                    t_r = jnp.tanh(pre[:, 0:half] + bah[:, cs])
                    t_i = jnp.tanh(pre[:, half:] + bih[:, cs])
                    log_a = t_r * ch[:, cs] + ch[:, cs]
                    a = jnp.exp(log_a)
                    om = -jnp.tanh(log_a) * (a * a + 1.0)
                    root = jnp.where(om > 0.0, om * lax.rsqrt(om), 0.0)
                    u = root * (t_i * xh[:, cs] + xh[:, cs])
                    for j in range(half // LANES):
                        lanes = slice(j * LANES, (j + 1) * LANES)
                        a_ref[hh * (half // LANES) + j, rows, :] = a[:, lanes]
                        u_ref[hh * (half // LANES) + j, rows, :] = u[:, lanes]
            return carry
        return body

    _tile_loop(n // a_tile, phase_a1, unroll=2)
    if late_gate:
        @pl.when(first_step)
        def _():
            for copy in late_gate:
                copy.wait()

    for q in seqs:
        _tile_loop(n // a_tile, phase_b(q))

    row8 = lax.broadcasted_iota(jnp.int32, (SUBLANES, LANES), 0)
    block = SUBLANES * SUBLANES
    n_blocks = n // block
    n_slabs = LRU_W // LANES

    def across_groups(a, b, reverse):
        for s in (1, 2, 4):
            m = (row8 < SUBLANES - s) if reverse else (row8 >= s)
            shift = SUBLANES - s if reverse else s
            a_s = jnp.where(m, pltpu.roll(a, shift, axis=0), 1.0)
            b_s = jnp.where(m, pltpu.roll(b, shift, axis=0), 0.0)
            b = a * b_s + b
            a = a * a_s
        return a, b

    def scan_block(a_ref, h_ref, base, carry, reverse):
        rows = [pl.ds(base + k, SUBLANES, stride=SUBLANES) for k in range(SUBLANES)]
        order = list(reversed(range(SUBLANES))) if reverse else list(range(SUBLANES))
        prod, local = {}, {}
        prev = None
        for k in order:
            a, u = a_ref[rows[k], :], h_ref[rows[k], :]
            prod[k] = a if prev is None else a * prod[prev]
            local[k] = u if prev is None else a * local[prev] + u
            prev = k
        p_all, h_all = across_groups(prod[prev], local[prev], reverse)
        inner = (row8 < SUBLANES - 1) if reverse else (row8 >= 1)
        shift = SUBLANES - 1 if reverse else 1
        enter = (jnp.where(inner, pltpu.roll(p_all, shift, axis=0), 1.0) * carry
                 + jnp.where(inner, pltpu.roll(h_all, shift, axis=0), 0.0))
        for k in order:
            h_ref[rows[k], :] = prod[k] * enter + local[k]
        leave = p_all * carry + h_all
        return leave[0:1, :] if reverse else leave[SUBLANES - 1:SUBLANES, :]

    def initial(q, d, s):
        if latent:
            return q.h0[d:d + 1, s * LANES:(s + 1) * LANES]
        return jnp.zeros((1, LANES), _F32)

    chains = [(q, s) for q in seqs for s in range(n_slabs)]

    def scan_body(i, carry):
        fwd_base = pl.multiple_of(i * block, block)
        bwd_base = pl.multiple_of((n_blocks - 1 - i) * block, block)
        out = []
        for (q, s), (cf, cb) in zip(chains, carry):
            out.append((scan_block(q.a_f.at[s], q.h_f.at[s], fwd_base, cf, False),
                        scan_block(q.a_b.at[s], q.h_b.at[s], bwd_base, cb, True)))
        return tuple(out)

    ends = lax.fori_loop(0, n_blocks, scan_body,
                         tuple((initial(q, 0, s), initial(q, 1, s)) for q, s in chains), unroll=2)

    if not latent:
        @pl.when(first_step)
        def _():
            late_out.wait()

    def lru_out(q):
        def body(r, carry):
            rows = pl.ds(pl.multiple_of(r * ROW_TILE, ROW_TILE), ROW_TILE)
            for s in range(n_slabs):
                lanes = slice(s * LANES, (s + 1) * LANES)
                q.mix[rows, lanes] = (q.h_f[s, rows, :] + q.h_b[s, rows, :]) * q.gy[rows, lanes]
            return carry
        return body

    for (q, s), (last_f, first_b) in zip(chains, ends):
        if not latent:
            q.stl[0:1, s * LANES:(s + 1) * LANES] = last_f
            q.stl[1:2, s * LANES:(s + 1) * LANES] = first_b
    for q in seqs:
        _tile_loop(nt, lru_out(q))

    def phase_a2(r, carry):
        r0 = pl.multiple_of(r * wide, wide)
        rows = pl.ds(r0, wide)
        h = stacked([normed(q, rows) for q in seqs])
        pqk_all = _dot(h, wqk_ref[...])
        pvg_all = _dot(h, wvg_ref[...])
        for s, q in enumerate(seqs):
            part = slice(s * wide, (s + 1) * wide)
            store_retention(q, r0, rows, pqk_all[part], pvg_all[part])
        return carry

    if not early_ret:
        @pl.when(first_step)
        def _():
            for copy in late_ret:
                copy.wait()

        _tile_loop(n // wide, phase_a2)

    lower = (lax.broadcasted_iota(jnp.int32, (ROW_TILE, ROW_TILE), 0)
             >= lax.broadcasted_iota(jnp.int32, (ROW_TILE, ROW_TILE), 1))
    blocks = [slice(r * ROW_TILE, (r + 1) * ROW_TILE) for r in range(nt)]

    def phase_c(q, hd):
        cols = slice(hd * DK, (hd + 1) * DK)
        dec = lambda row: decay[hd][row]
        if nt > 1:
            kv = [_dot_tn(jnp.concatenate([q.k_f[hd, rows, :], q.k_b[hd, rows, :]], axis=1),
                          q.v[rows, cols]) for rows in blocks]
            kv_f = [m[0:DK] for m in kv]
            kv_b = [m[DK:2 * DK] for m in kv]
        else:
            kv_f = [_dot_tn(q.k_f[hd, rows, :], q.v[rows, cols]) for rows in blocks]
            kv_b = [_dot_tn(q.k_b[hd, rows, :], q.v[rows, cols]) for rows in blocks]
        if latent:
            run_f = q.s0[0, hd] * dec(DEC_G_F)
            run_b = q.s0[1, hd] * dec(DEC_GN_B)
        else:
            run_f = run_b = None
        before = []
        for r in range(nt):
            before.append(run_f)
            run_f = kv_f[r] if run_f is None else run_f + kv_f[r]
        after = [None] * nt
        for r in reversed(range(nt)):
            after[r] = run_b
            run_b = kv_b[r] if run_b is None else run_b + kv_b[r]

        for r, rows in enumerate(blocks):
            qf = q.q_f[hd, rows, :]
            qb = q.q_b[hd, rows, :]
            s = jnp.where(lower, _dot_nt(qf, q.k_f[hd, rows, :]), _dot_nt(qb, q.k_b[hd, rows, :]))
            o = _dot(s, q.v[rows, cols])
            if before[r] is not None and after[r] is not None:
                o = o + _dot(jnp.concatenate([qf, qb], axis=1),
                             jnp.concatenate([before[r], after[r]], axis=0))
            elif before[r] is not None:
                o = o + _dot(qf, before[r])
            elif after[r] is not None:
                o = o + _dot(qb, after[r])
            o = o * lax.rsqrt(jnp.mean(o * o, axis=-1, keepdims=True) + EPS)
            q.mix[rows, LRU_W + cols.start:LRU_W + cols.stop] = o * q.sg[rows, cols]
        if not latent:
            q.str[0, hd] = run_f * dec(DEC_GN1_F)
            q.str[1, hd] = run_b

    for hd in range(RET_H):
        for q in seqs:
            phase_c(q, hd)

    gain1 = mod(2) * n2_ref[...]
    gain2 = n3_ref[...] * (1.0 + mod(4))
    sh2 = mod(3)

    def phase_d(r, carry):
        rows = pl.ds(pl.multiple_of(r * wide, wide), wide)
        mix_all = _dot(stacked([q.mix[rows, :] for q in seqs]), wout_ref[...])
        h2s = []
        for s, q in enumerate(seqs):
            x1 = q.x[rows, :] + _rms(mix_all[s * wide:(s + 1) * wide], gain1)
            q.x1[rows, :] = x1
            h2s.append(_rms(x1, gain2) + sh2)
            q.h2[rows, :] = h2s[-1].astype(_BF)
        logits_t = _dot(stacked(h2s), rw_ref[...]).T
        for s, q in enumerate(seqs):
            q.lt[:, rows] = logits_t[0:N_EXP, s * wide:(s + 1) * wide]
        return carry

    if latent:
        @pl.when(first_step)
        def _():
            late_out.wait()

    _tile_loop(n // wide, phase_d, unroll=2)


def _mixer_call(x, mod, norms, w_in, lru, wg, w_out, rw, latent, extra=(), layer=0):
    b, n, _ = x.shape
    g = 1 if latent else CONTEXT_GROUP
    const2 = lambda i: (0, 0)
    whole = pl.BlockSpec(memory_space=pl.ANY)
    in_specs = [
        pl.BlockSpec((g, n, D), lambda i: (i, 0, 0)),
        pl.BlockSpec((N_MOD, SUBLANES, D), lambda i: (0, 0, 0)),
        pl.BlockSpec((1, D), const2), pl.BlockSpec((1, D), const2), pl.BlockSpec((1, D), const2),
        pl.BlockSpec((D, D), lambda i: (0, 0)),
    ]
    if latent:
        in_specs += [whole]
    else:
        in_specs += [pl.BlockSpec((D, D), lambda i: (0, 1)),
                     pl.BlockSpec((D, D), lambda i: (0, 2))]
    in_specs += [
        pl.BlockSpec((4, LRU_W), const2), pl.BlockSpec((1, LRU_W), const2),
        pl.BlockSpec((2, LRU_W), const2), pl.BlockSpec((2, LRU_W), const2),
        pl.BlockSpec((2, LRU_W), const2),
        whole if latent else pl.BlockSpec((2, 2, LRU_W // 2, LRU_W), lambda i: (0, 0, 0, 0)),
        whole,
        pl.BlockSpec((D, LANES), const2),
    ]
    out_shape = [
        jax.ShapeDtypeStruct((b, n, D), _F32),
        jax.ShapeDtypeStruct((b, n, D), _BF),
        jax.ShapeDtypeStruct((b, N_EXP, n), _F32),
    ]
    out_specs = [
        pl.BlockSpec((g, n, D), lambda i: (i, 0, 0)),
        pl.BlockSpec((g, n, D), lambda i: (i, 0, 0)),
        pl.BlockSpec((g, N_EXP, n), lambda i: (i, 0, 0)),
    ]
    if latent:
        in_specs += [
            pl.BlockSpec((g, None, 2, LRU_W), lambda i: (i, layer, 0, 0)),
            pl.BlockSpec((g, None, 2, RET_H, DK, DK), lambda i: (i, layer, 0, 0, 0, 0)),
            whole, whole,
        ]
    else:
        out_shape += [
            jax.ShapeDtypeStruct((b, 1, 2, LRU_W), _F32),
            jax.ShapeDtypeStruct((b, 1, 2, RET_H, DK, DK), _F32),
        ]
        out_specs += [
            pl.BlockSpec((g, None, 2, LRU_W), lambda i: (i, 0, 0, 0)),
            pl.BlockSpec((g, None, 2, RET_H, DK, DK), lambda i: (i, 0, 0, 0, 0, 0)),
        ]
    f32s = lambda shape: pltpu.VMEM((g,) + shape, _F32)
    slabs = (LRU_W // LANES, n, LANES)
    scratch = [
        f32s((n + 2 * SUBLANES, LRU_W)),
        f32s((n, LRU_W)),
        f32s(slabs), f32s(slabs),
        f32s(slabs), f32s(slabs),
    ]
    if latent:
        scratch += [pltpu.VMEM((D, 2 * D), _F32), pltpu.VMEM(wg.shape, _F32),
                    pltpu.VMEM((n, DK), _F32), pltpu.VMEM((n, DK), _F32)]
        w_in_args = (w_in, w_in)
    else:
        scratch += [f32s((n, RET_W))] + [f32s(slabs) for _ in range(4)] + [f32s((n, RET_W))]
        w_in_args = (w_in, w_in, w_in)
    scratch += [pltpu.VMEM((D, D), _F32), pltpu.SemaphoreType.DMA((5 if latent else 1,))]
    return pl.pallas_call(
        functools.partial(_mixer_kernel, n=n, g=g, latent=latent),
        out_shape=out_shape,
        grid=(b // g,),
        in_specs=in_specs,
        out_specs=out_specs,
        scratch_shapes=scratch,
        compiler_params=pltpu.CompilerParams(
            dimension_semantics=("arbitrary",), vmem_limit_bytes=VMEM_LIMIT),
        name="mixer_latent" if latent else "mixer_context",
    )(x, mod, *norms, *w_in_args, *lru, wg, w_out, rw, *extra)


def _count(mask):
    return jnp.sum(jnp.where(mask, 1.0, 0.0), axis=-1, keepdims=True)


def _probs(l3):
    bsz, _, n = l3.shape
    m = jnp.max(l3, axis=1, keepdims=True)
    e = jnp.exp(l3 - m)
    return (e / jnp.sum(e, axis=1, keepdims=True)).reshape(bsz * N_EXP, n)


def _break_ties(parts):
    prep = []
    for bits, thr, cap in parts:
        rows, n = bits.shape
        eq = bits == thr
        need = float(cap) - _count(bits > thr)
        idx = lax.broadcasted_iota(jnp.int32, (rows, n), 1)
        prep.append((eq, need, idx, int(math.log2(n))))
    most = max(nbits for _, _, _, nbits in prep)

    def idx_body(i, last):
        out = []
        for (eq, need, idx, nbits), j in zip(prep, last):
            shift = nbits - 1 - i
            cand = j | jnp.where(shift >= 0, jnp.int32(1) << jnp.maximum(shift, 0), 0)
            out.append(jnp.where(_count(eq & (idx < cand)) < need, cand, j))
        return tuple(out)

    return lax.fori_loop(0, most, idx_body,
                         tuple(jnp.zeros((eq.shape[0], 1), jnp.int32) for eq, _, _, _ in prep))


def _slots(p, bits, thr, jlast):
    rows, n = p.shape
    idx = lax.broadcasted_iota(jnp.int32, (rows, n), 1)
    sel = (bits > thr) | ((bits == thr) & (idx <= jlast))
    before = (lax.broadcasted_iota(jnp.int32, (n, n), 0)
              < lax.broadcasted_iota(jnp.int32, (n, n), 1))
    pos = _dot(jnp.where(sel, 1.0, 0.0).astype(_BF), jnp.where(before, 1.0, 0.0).astype(_BF))
    return jnp.where(sel, pos, -1.0), jnp.where(sel, p, 0.0)


def _route_kernel(lp_ref, ls_ref, pp_ref, gp_ref, ps_ref, gs_ref, *, cap_p, cap_s):
    groups = ((_probs(lp_ref[...]), float(cap_p)), (_probs(ls_ref[...]), float(cap_s)))
    bits = [pltpu.bitcast(p, jnp.int32) for p, _ in groups]

    def settle(b, t, capf, hi, lo):
        with_hi = t | hi
        both = with_hi | lo
        with_lo = t | lo
        ok = lambda cand: _count(b >= cand) >= capf
        return jnp.where(ok(with_hi), jnp.where(ok(both), both, with_hi),
                         jnp.where(ok(with_lo), with_lo, t))

    def val_body(i, thr):
        hi = jnp.int32(1) << (29 - 2 * i)
        lo = jnp.int32(1) << (28 - 2 * i)
        return tuple(settle(b, t, capf, hi, lo) for b, t, (_, capf) in zip(bits, thr, groups))

    thr = lax.fori_loop(0, 15, val_body,
                        tuple(jnp.zeros((b.shape[0], 1), jnp.int32) for b in bits))
    jlast = _break_ties([(bits[0], thr[0], cap_p), (bits[1], thr[1], cap_s)])
    pos, gate = _slots(groups[0][0], bits[0], thr[0], jlast[0])
    pp_ref[...] = pos.reshape(pp_ref.shape)
    gp_ref[...] = gate.reshape(gp_ref.shape)
    pos, gate = _slots(groups[1][0], bits[1], thr[1], jlast[1])
    ps_ref[...] = pos.reshape(ps_ref.shape)
    gs_ref[...] = gate.reshape(gs_ref.shape)


def _route_call(lt_p, lt_s, cap_p, cap_s):
    shapes = [
        jax.ShapeDtypeStruct(lt_p.shape, _F32), jax.ShapeDtypeStruct(lt_p.shape, _F32),
        jax.ShapeDtypeStruct(lt_s.shape, _F32), jax.ShapeDtypeStruct(lt_s.shape, _F32),
    ]
    return pl.pallas_call(
        functools.partial(_route_kernel, cap_p=cap_p, cap_s=cap_s),
        out_shape=shapes,
        compiler_params=pltpu.CompilerParams(vmem_limit_bytes=VMEM_LIMIT),
        name="route_select",
    )(lt_p, lt_s)


def _dispatch_kernel(pos_ref, gate_ref, h_ref, xs_ref, gs_ref, *, n, cap, g):
    slot = lax.broadcasted_iota(jnp.int32, (cap, n), 0).astype(_F32)
    for j in range(g):
        slots = slice(j * cap, (j + 1) * cap)
        parts = []
        for e in range(N_EXP):
            hit = pos_ref[j, e:e + 1, :] == slot
            parts.append(jnp.where(hit, 1.0, 0.0).astype(_BF))
            gs_ref[e, slots, :] = jnp.sum(jnp.where(hit, gate_ref[j, e:e + 1, :], 0.0),
                                          axis=-1, keepdims=True)
        onehot = jnp.concatenate(parts, axis=0)
        xs = _dot(onehot, h_ref[j]).astype(_BF)
        for e in range(N_EXP):
            xs_ref[e, slots, :] = xs[e * cap:(e + 1) * cap, :]


def _dispatch_both_kernel(pp_ref, gp_ref, hp_ref, ps_ref, gq_ref, hs_ref, xs_ref, gs_ref,
                          *, steps_p, n_p, cap_p, g_p, n_s, cap_s, g_s):
    step = pl.program_id(0)

    @pl.when(step < steps_p)
    def _():
        _dispatch_kernel(pp_ref, gp_ref, hp_ref, xs_ref, gs_ref, n=n_p, cap=cap_p, g=g_p)

    @pl.when(step >= steps_p)
    def _():
        _dispatch_kernel(ps_ref, gq_ref, hs_ref, xs_ref, gs_ref, n=n_s, cap=cap_s, g=g_s)


def _dispatch_call(pos_p, gate_p, h2_p, cap_p, pos_s, gate_s, h2_s, cap_s):
    bp, n_p, _ = h2_p.shape
    bs, n_s, _ = h2_s.shape
    g_p = max(1, SMALL_STEP_TOKENS // n_p)
    g_s = max(1, SMALL_STEP_TOKENS // n_s)
    slots = g_p * cap_p
    assert slots == g_s * cap_s
    steps_p, steps_s = bp // g_p, bs // g_s
    first = lambda i: (jnp.minimum(i, steps_p - 1), 0, 0)
    second = lambda i: (jnp.maximum(i - steps_p, 0), 0, 0)
    total = bp * cap_p + bs * cap_s
    return pl.pallas_call(
        functools.partial(_dispatch_both_kernel, steps_p=steps_p, n_p=n_p, cap_p=cap_p, g_p=g_p,
                          n_s=n_s, cap_s=cap_s, g_s=g_s),
        out_shape=[
            jax.ShapeDtypeStruct((N_EXP, total, D), _BF),
            jax.ShapeDtypeStruct((N_EXP, total, 1), _F32),
        ],
        grid=(steps_p + steps_s,),
        in_specs=[
            pl.BlockSpec((g_p, N_EXP, n_p), first),
            pl.BlockSpec((g_p, N_EXP, n_p), first),
            pl.BlockSpec((g_p, n_p, D), first),
            pl.BlockSpec((g_s, N_EXP, n_s), second),
            pl.BlockSpec((g_s, N_EXP, n_s), second),
            pl.BlockSpec((g_s, n_s, D), second),
        ],
        out_specs=[
            pl.BlockSpec((N_EXP, slots, D), lambda i: (0, i, 0)),
            pl.BlockSpec((N_EXP, slots, 1), lambda i: (0, i, 0)),
        ],
        compiler_params=pltpu.CompilerParams(
            dimension_semantics=("arbitrary",), vmem_limit_bytes=VMEM_LIMIT),
        name="dispatch",
    )(pos_p, gate_p, h2_p, pos_s, gate_s, h2_s)


def _expert_kernel(x_ref, g_ref, wg_ref, wu_ref, wd_ref, y_ref, acc, *, nf, tf):
    f = pl.program_id(1)
    x = x_ref[...]
    total = jnp.where(f == 0, 0.0, acc[...])
    for c in range(tf // FF_CHUNK):
        cs = slice(c * FF_CHUNK, (c + 1) * FF_CHUNK)
        hg = _dot(x, wg_ref[:, cs].astype(_BF))
        hu = _dot(x, wu_ref[:, cs].astype(_BF))
        hid = (_silu(hg) * hu).astype(_BF)
        total = total + _dot(hid, wd_ref[cs, :].astype(_BF))
    acc[...] = total
    y_ref[...] = (total * g_ref[...]).astype(_BF)


def _expert_call(xs, gs, w_gate, w_up, w_down):
    tf = 1024
    slots = xs.shape[1]
    nf = FF // tf
    return pl.pallas_call(
        functools.partial(_expert_kernel, nf=nf, tf=tf),
        out_shape=jax.ShapeDtypeStruct((N_EXP, slots, D), _BF),
        grid=(N_EXP, nf),
        in_specs=[
            pl.BlockSpec((None, slots, D), lambda e, f: (e, 0, 0)),
            pl.BlockSpec((None, slots, 1), lambda e, f: (e, 0, 0)),
            pl.BlockSpec((None, D, tf), lambda e, f: (e, 0, f)),
            pl.BlockSpec((None, D, tf), lambda e, f: (e, 0, f)),
            pl.BlockSpec((None, tf, D), lambda e, f: (e, f, 0)),
        ],
        out_specs=pl.BlockSpec((None, slots, D), lambda e, f: (e, 0, 0)),
        scratch_shapes=[pltpu.VMEM((slots, D), _F32)],
        compiler_params=pltpu.CompilerParams(
            dimension_semantics=("arbitrary", "arbitrary"), vmem_limit_bytes=VMEM_LIMIT),
        name="expert_ffn",
    )(xs, gs, w_gate, w_up, w_down)


def _combine_kernel(pos_ref, ye_ref, x1_ref, mod_ref, n4_ref, y_ref, *, n, cap, g, mod_row):
    width = N_EXP * cap
    lane = lax.broadcasted_iota(jnp.int32, (N_EXP, width), 1)
    expand = jnp.where(lane // cap == lax.broadcasted_iota(jnp.int32, (N_EXP, width), 0),
                       1.0, 0.0).astype(_BF)
    tile = min(n, COMBINE_TILE)
    slot = (lax.broadcasted_iota(jnp.int32, (tile, width), 1) % cap).astype(_F32)
    gain = mod_ref[N_MOD - 1, pl.ds(mod_row, 1), :] * n4_ref[...]

    for j in range(g):
        ye = ye_ref[:, j * cap:(j + 1) * cap, :].reshape(width, D)

        def body(r, carry, j=j, ye=ye):
            rows = pl.ds(pl.multiple_of(r * tile, tile), tile)
            pos_e = _dot_tn(pos_ref[j, :, rows].astype(_BF), expand)
            onehot = jnp.where(pos_e == slot, 1.0, 0.0).astype(_BF)
            f = _dot(onehot, ye)
            y_ref[j, rows, :] = x1_ref[j, rows, :] + _rms(f, gain)
            return carry

        _tile_loop(n // tile, body)


def _combine_both_kernel(pp_ref, xp_ref, ps_ref, xq_ref, ye_ref, mod_ref, n4_ref, yp_ref, ys_ref,
                         *, steps_p, n_p, cap_p, g_p, n_s, cap_s, g_s):
    step = pl.program_id(0)

    @pl.when(step < steps_p)
    def _():
        _combine_kernel(pp_ref, ye_ref, xp_ref, mod_ref, n4_ref, yp_ref,
                        n=n_p, cap=cap_p, g=g_p, mod_row=0)

    @pl.when(step >= steps_p)
    def _():
        _combine_kernel(ps_ref, ye_ref, xq_ref, mod_ref, n4_ref, ys_ref,
                        n=n_s, cap=cap_s, g=g_s, mod_row=step - steps_p + 1)


def _combine_call(pos_p, x1_p, cap_p, pos_s, x1_s, cap_s, ye, mod, norm_post):
    bp, n_p, _ = x1_p.shape
    bs, n_s, _ = x1_s.shape
    g_p = max(1, SMALL_STEP_TOKENS // n_p)
    g_s = max(1, SMALL_STEP_TOKENS // n_s)
    assert g_s == 1
    slots = g_p * cap_p
    assert slots == g_s * cap_s
    steps_p, steps_s = bp // g_p, bs // g_s
    first = lambda i: (jnp.minimum(i, steps_p - 1), 0, 0)
    second = lambda i: (jnp.maximum(i - steps_p, 0), 0, 0)
    return pl.pallas_call(
        functools.partial(_combine_both_kernel, steps_p=steps_p, n_p=n_p, cap_p=cap_p, g_p=g_p,
                          n_s=n_s, cap_s=cap_s, g_s=g_s),
        out_shape=[
            jax.ShapeDtypeStruct((bp, n_p, D), _F32),
            jax.ShapeDtypeStruct((bs, n_s, D), _F32),
        ],
        grid=(steps_p + steps_s,),
        in_specs=[
            pl.BlockSpec((g_p, N_EXP, n_p), first),
            pl.BlockSpec((g_p, n_p, D), first),
            pl.BlockSpec((g_s, N_EXP, n_s), second),
            pl.BlockSpec((g_s, n_s, D), second),
            pl.BlockSpec((N_EXP, slots, D), lambda i: (0, i, 0)),
            pl.BlockSpec((N_MOD, SUBLANES, D), lambda i: (0, 0, 0)),
            pl.BlockSpec((1, D), lambda i: (0, 0)),
        ],
        out_specs=[
            pl.BlockSpec((g_p, n_p, D), first),
            pl.BlockSpec((g_s, n_s, D), second),
        ],
        compiler_params=pltpu.CompilerParams(
            dimension_semantics=("arbitrary",), vmem_limit_bytes=VMEM_LIMIT),
        name="combine",
    )(pos_p, x1_p, pos_s, x1_s, ye, mod, norm_post)


def _block_diag_gates(wa, wi):
    per_half = LRU_HEADS // 2
    side = per_half * LRU_HD
    on_diag = (np.arange(side)[:, None] // LRU_HD) == (np.arange(side)[None, :] // LRU_HD)

    def bd(w):
        rows = w.reshape(2, 2, side, LRU_HD)
        return jnp.where(on_diag, jnp.tile(rows, (1, 1, 1, per_half)), 0.0)

    return 0.5 * jnp.concatenate([bd(wa), bd(wi)], axis=-1)


def _rope_tables(n):
    rows = n // GRID_W
    row = np.repeat(np.arange(rows, dtype=np.float32), GRID_W)
    col = np.tile(np.arange(GRID_W, dtype=np.float32), rows)
    nf = DK // 4
    freqs = np.float32(ROPE_BASE) ** (-np.arange(nf, dtype=np.float32) / np.float32(nf))
    ang = np.concatenate([row[:, None] * freqs, col[:, None] * freqs], axis=-1).astype(np.float32)
    cos = np.cos(ang)
    sin = np.sin(ang)
    return (jnp.asarray(np.concatenate([cos, cos], axis=-1), _F32),
            jnp.asarray(np.concatenate([-sin, sin], axis=-1), _F32))


def _decay_consts(n):
    heads = np.arange(RET_H, dtype=np.float32)
    f32 = np.float32
    lgf = np.log1p(-np.exp2(-(f32(RET_DECAY_OFFSET_FWD) + heads))).astype(f32)
    lgb = np.log1p(-np.exp2(-(f32(RET_DECAY_OFFSET_BWD) + heads))).astype(f32)
    tab = np.stack([lgf, lgb, np.exp(lgf), np.exp(f32(n) * lgb), np.exp(f32(n - 1) * lgf)], axis=1)
    return [[float(v) for v in row] for row in tab.astype(f32)]


def kernel(x_prompt, x_sample, c, state_lru, state_ret, c_ctx, ada_w, ada_b, norm_mix_pre, norm_mix_post, norm_ffn_pre, norm_ffn_post, w_in, conv_w, conv_b, lru_wa, lru_ba, lru_wi, lru_bi, lru_lambda, w_out, router_w, exp_w_gate, exp_w_up, exp_w_down):
    bp, n_p, _ = x_prompt.shape
    bs, n_s, _ = x_sample.shape
    cap_p = 2 * n_p // N_EXP
    cap_s = 2 * n_s // N_EXP
    l = 0

    mod = _ada_call(c_ctx[None, :], c, ada_w[l], ada_b[l][None, :])

    norms = (norm_mix_pre[l][None], norm_mix_post[l][None], norm_ffn_pre[l][None])
    lru = (conv_w[l], conv_b[l][None], lru_ba[l], lru_bi[l], lru_lambda[l])
    wg = _block_diag_gates(lru_wa[l], lru_wi[l])
    rw = jnp.pad(router_w[l], ((0, 0), (0, LANES - N_EXP)))
    cos2, sin2 = _rope_tables(n_s)

    x1_p, h2_p, lt_p, st_lru, st_ret = _mixer_call(
        x_prompt, mod, norms, w_in[l], lru, wg, w_out[l], rw, latent=False)
    x1_s, h2_s, lt_s = _mixer_call(
        x_sample, mod, norms, w_in[l], lru, wg, w_out[l], rw, latent=True,
        extra=(state_lru, state_ret, cos2, sin2), layer=l)

    pos_p, gate_p, pos_s, gate_s = _route_call(lt_p, lt_s, cap_p, cap_s)
    xs, gsl = _dispatch_call(pos_p, gate_p, h2_p, cap_p, pos_s, gate_s, h2_s, cap_s)
    ye = _expert_call(xs, gsl, exp_w_gate[l], exp_w_up[l], exp_w_down[l])

    norm_post = norm_ffn_post[l][None]
    y_p, y_s = _combine_call(pos_p, x1_p, cap_p, pos_s, x1_s, cap_s, ye, mod, norm_post)
    return (y_p, y_s, st_lru, st_ret)
```
